```python
import jax, jax.numpy as jnp
from jax import lax
import numpy as np

D_MODEL = 2048
BATCH = 8
SEQ = 2048
DEPTH = 1

CHUNK = 64
Q_BLOCK = 128
N_MEM = 256
EPS = 1e-6

MLA_HEADS = 8
MLA_NOPE = 128
MLA_ROPE = 64
MLA_QK = MLA_NOPE + MLA_ROPE
MLA_V = 128
MLA_Q_RANK = 512
MLA_KV_RANK = 256
ROPE_THETA = 10000.0

GLA_HEADS = 4
GLA_DK = 128
GLA_DV = 256
GLA_GATE_RANK = 16
GLA_TAU = 16.0

MIX_WIDTH = MLA_HEADS * MLA_V + GLA_HEADS * GLA_DV

MEM_HEADS = 4
MEM_HEAD_DIM = 128
MEM_WIDTH = MEM_HEADS * MEM_HEAD_DIM

D_FF = 5632

IN_SIZES = [
    MLA_Q_RANK,
    MLA_KV_RANK,
    MLA_ROPE,
    GLA_HEADS * GLA_DK,
    GLA_HEADS * GLA_DK,
    GLA_HEADS * GLA_DV,
    GLA_GATE_RANK,
    GLA_HEADS * GLA_DV,
]
IN_WIDTH = int(sum(IN_SIZES))
IN_SPLITS = [int(s) for s in np.cumsum(IN_SIZES)[:-1]]

kernel_name = "hybrid_mla_gla_macaron_memory_block"


def rmsnorm(x, g):
    xf = x.astype(jnp.float32)
    y = xf * lax.rsqrt(jnp.mean(xf * xf, axis=-1, keepdims=True) + EPS)
    return (y * g.astype(jnp.float32)).astype(x.dtype)


def swiglu(h, w_gate, w_up, w_down):
    return (jax.nn.silu(h @ w_gate) * (h @ w_up)) @ w_down


def rope(x, positions):
    half = x.shape[-1] // 2
    inv_freq = ROPE_THETA ** (-jnp.arange(half, dtype=jnp.float32) / half)
    ang = positions.astype(jnp.float32)[..., None] * inv_freq
    cos = jnp.cos(ang)[:, :, None, :]
    sin = jnp.sin(ang)[:, :, None, :]
    xf = x.astype(jnp.float32)
    x1, x2 = xf[..., :half], xf[..., half:]
    return jnp.concatenate([x1 * cos - x2 * sin, x2 * cos + x1 * sin], axis=-1).astype(x.dtype)


def chunk_causal_attention(q, k, v):
    B, S, H, Dk = q.shape
    Dv = v.shape[-1]
    n_blk = S // Q_BLOCK
    scale = Dk ** -0.5
    k_chunk = jnp.arange(S) // CHUNK
    q_blocks = q.reshape(B, n_blk, Q_BLOCK, H, Dk).transpose(1, 0, 2, 3, 4)

    def one_block(args):
        q_blk, blk = args
        s = jnp.einsum('bqhd,bkhd->bhqk', q_blk, k).astype(jnp.float32) * scale
        q_chunk = (blk * Q_BLOCK + jnp.arange(Q_BLOCK)) // CHUNK
        mask = k_chunk[None, :] <= q_chunk[:, None]
        s = jnp.where(mask[None, None], s, -jnp.inf)
        p = jax.nn.softmax(s, axis=-1).astype(v.dtype)
        return jnp.einsum('bhqk,bkhd->bqhd', p, v)

    out = lax.map(one_block, (q_blocks, jnp.arange(n_blk)))
    return out.transpose(1, 0, 2, 3, 4).reshape(B, S, H, Dv)


def gla_chunked(q, k, v, log_a):
    B, S, H, K = q.shape
    V = v.shape[-1]
    n_chunk = S // CHUNK
    f32 = jnp.float32
    qc = q.astype(f32).reshape(B, n_chunk, CHUNK, H, K) * (K ** -0.5)
    kc = k.astype(f32).reshape(B, n_chunk, CHUNK, H, K)
    vc = v.astype(f32).reshape(B, n_chunk, CHUNK, H, V)
    g = log_a.astype(f32).reshape(B, n_chunk, CHUNK, H, K)
    b = jnp.cumsum(g, axis=2)
    b_end = b[:, :, -1]
    k_dec = kc * jnp.exp(b_end[:, :, None] - b)
    u = jnp.einsum('bnchk,bnchv->bnhkv', k_dec, vc)
    decay = jnp.exp(b_end)

    def step(state, inp):
        d, uc = inp
        state = d[..., None] * state + uc
        return state, state

    s0 = jnp.zeros((B, H, K, V), f32)
    _, states = lax.scan(step, s0, (decay.transpose(1, 0, 2, 3), u.transpose(1, 0, 2, 3, 4)))
    states = states.transpose(1, 0, 2, 3, 4)
    o = jnp.einsum('bnchk,bnhkv->bnchv', qc, states)
    return o.reshape(B, S, H, V).astype(v.dtype)


def memory_cross_attention(h, m, w_q, w_k, w_v, w_o, g_q, g_k):
    B, S, _ = h.shape
    M = m.shape[1]
    q = rmsnorm((h @ w_q).reshape(B, S, MEM_HEADS, MEM_HEAD_DIM), g_q)
    k = rmsnorm((m @ w_k).reshape(B, M, MEM_HEADS, MEM_HEAD_DIM), g_k)
    v = (m @ w_v).reshape(B, M, MEM_HEADS, MEM_HEAD_DIM)
    s = jnp.einsum('bqhd,bkhd->bhqk', q, k).astype(jnp.float32) * (MEM_HEAD_DIM ** -0.5)
    p = jax.nn.softmax(s, axis=-1).astype(v.dtype)
    o = jnp.einsum('bhqk,bkhd->bqhd', p, v).reshape(B, S, MEM_WIDTH)
    return o @ w_o


def _fwd_setup_inputs(seed: int = 0) -> dict:
    key = jax.random.key(seed)
    keys = iter(jax.random.split(key, 40))
    f32 = jnp.float32

    def w(fan_in, fan_out):
        return jax.random.normal(next(keys), (DEPTH, fan_in, fan_out), f32) * fan_in ** -0.5

    def g(n):
        return 1.0 + 0.02 * jax.random.normal(next(keys), (DEPTH, n), f32)

    x = jax.random.normal(next(keys), (BATCH, SEQ, D_MODEL), f32)
    mem = jax.random.normal(next(keys), (BATCH, N_MEM, D_MODEL), f32)
    offset = jax.random.randint(next(keys), (BATCH, 1), 0, 64, dtype=jnp.int32) * CHUNK
    positions = (offset + jnp.arange(SEQ, dtype=jnp.int32)[None, :]).astype(jnp.int32)

    return {
        "x": x,
        "mem": mem,
        "positions": positions,
        "ffn1_norm": g(D_MODEL),
        "ffn1_w_gate": w(D_MODEL, D_FF),
        "ffn1_w_up": w(D_MODEL, D_FF),
        "ffn1_w_down": w(D_FF, D_MODEL),
        "mix_norm": g(D_MODEL),
        "w_in": w(D_MODEL, IN_WIDTH),
        "q_a_norm": g(MLA_Q_RANK),
        "w_q_up": w(MLA_Q_RANK, MLA_HEADS * MLA_QK),
        "kv_a_norm": g(MLA_KV_RANK),
        "w_kv_up": w(MLA_KV_RANK, MLA_HEADS * (MLA_NOPE + MLA_V)),
        "mla_q_norm": g(MLA_QK),
        "mla_k_norm": g(MLA_QK),
        "gla_w_gate2": w(GLA_GATE_RANK, GLA_HEADS * GLA_DK),
        "gla_b_gate": 0.1 * jax.random.normal(next(keys), (DEPTH, GLA_HEADS * GLA_DK), f32),
        "gla_out_norm": g(GLA_DV),
        "w_out": w(MIX_WIDTH, D_MODEL),
        "mem_attn_norm": g(D_MODEL),
        "mem_norm": g(D_MODEL),
        "mem_w_q": w(D_MODEL, MEM_WIDTH),
        "mem_w_k": w(D_MODEL, MEM_WIDTH),
        "mem_w_v": w(D_MODEL, MEM_WIDTH),
        "mem_w_o": w(MEM_WIDTH, D_MODEL),
        "mem_q_norm": g(MEM_HEAD_DIM),
        "mem_k_norm": g(MEM_HEAD_DIM),
        "ffn2_norm": g(D_MODEL),
        "ffn2_w_gate": w(D_MODEL, D_FF),
        "ffn2_w_up": w(D_MODEL, D_FF),
        "ffn2_w_down": w(D_FF, D_MODEL),
    }


def _fwd_reference(x, mem, positions, ffn1_norm, ffn1_w_gate, ffn1_w_up, ffn1_w_down,
              mix_norm, w_in, q_a_norm, w_q_up, kv_a_norm, w_kv_up, mla_q_norm,
              mla_k_norm, gla_w_gate2, gla_b_gate, gla_out_norm, w_out,
              mem_attn_norm, mem_norm, mem_w_q, mem_w_k, mem_w_v, mem_w_o,
              mem_q_norm, mem_k_norm, ffn2_norm, ffn2_w_gate, ffn2_w_up, ffn2_w_down):
    B, S, _ = x.shape
    for l in range(DEPTH):
        x = x + 0.5 * swiglu(rmsnorm(x, ffn1_norm[l]), ffn1_w_gate[l], ffn1_w_up[l], ffn1_w_down[l])

        h = rmsnorm(x, mix_norm[l])
        z = h @ w_in[l]
        zq, zkv, zkr, gq, gk, gv, zg, zr = jnp.split(z, IN_SPLITS, axis=-1)

        q = (rmsnorm(zq, q_a_norm[l]) @ w_q_up[l]).reshape(B, S, MLA_HEADS, MLA_QK)
        kv = (rmsnorm(zkv, kv_a_norm[l]) @ w_kv_up[l]).reshape(B, S, MLA_HEADS, MLA_NOPE + MLA_V)
        k_nope, v = kv[..., :MLA_NOPE], kv[..., MLA_NOPE:]
        k_rope = jnp.broadcast_to(zkr[:, :, None, :], (B, S, MLA_HEADS, MLA_ROPE))
        k = jnp.concatenate([k_nope, k_rope], axis=-1)
        q = rmsnorm(q, mla_q_norm[l])
        k = rmsnorm(k, mla_k_norm[l])
        q = jnp.concatenate([q[..., :MLA_NOPE], rope(q[..., MLA_NOPE:], positions)], axis=-1)
        k = jnp.concatenate([k[..., :MLA_NOPE], rope(k[..., MLA_NOPE:], positions)], axis=-1)
        o_mla = chunk_causal_attention(q, k, v).reshape(B, S, MLA_HEADS * MLA_V)

        log_a = jax.nn.log_sigmoid((zg @ gla_w_gate2[l] + gla_b_gate[l]).astype(jnp.float32)) / GLA_TAU
        o_gla = gla_chunked(gq.reshape(B, S, GLA_HEADS, GLA_DK),
                            gk.reshape(B, S, GLA_HEADS, GLA_DK),
                            gv.reshape(B, S, GLA_HEADS, GLA_DV),
                            log_a.reshape(B, S, GLA_HEADS, GLA_DK))
        o_gla = rmsnorm(o_gla, gla_out_norm[l]).reshape(B, S, GLA_HEADS * GLA_DV) * jax.nn.silu(zr)

        x = x + jnp.concatenate([o_mla, o_gla], axis=-1) @ w_out[l]

        x = x + memory_cross_attention(rmsnorm(x, mem_attn_norm[l]), rmsnorm(mem, mem_norm[l]),
                                       mem_w_q[l], mem_w_k[l], mem_w_v[l], mem_w_o[l],
                                       mem_q_norm[l], mem_k_norm[l])

        x = x + 0.5 * swiglu(rmsnorm(x, ffn2_norm[l]), ffn2_w_gate[l], ffn2_w_up[l], ffn2_w_down[l])
    return x


import jax as _jax
import jax.numpy as _jnp

TWIN_FORMAT = 'train_step'
FWD_PARAMS = ['x', 'mem', 'positions', 'ffn1_norm', 'ffn1_w_gate', 'ffn1_w_up', 'ffn1_w_down', 'mix_norm', 'w_in', 'q_a_norm', 'w_q_up', 'kv_a_norm', 'w_kv_up', 'mla_q_norm', 'mla_k_norm', 'gla_w_gate2', 'gla_b_gate', 'gla_out_norm', 'w_out', 'mem_attn_norm', 'mem_norm', 'mem_w_q', 'mem_w_k', 'mem_w_v', 'mem_w_o', 'mem_q_norm', 'mem_k_norm', 'ffn2_norm', 'ffn2_w_gate', 'ffn2_w_up', 'ffn2_w_down']
TWIN_WEIGHTS = ['ffn1_norm', 'ffn1_w_gate', 'ffn1_w_up', 'ffn1_w_down', 'mix_norm', 'w_in', 'q_a_norm', 'w_q_up', 'kv_a_norm', 'w_kv_up', 'mla_q_norm', 'mla_k_norm', 'gla_w_gate2', 'gla_b_gate', 'gla_out_norm', 'w_out', 'mem_attn_norm', 'mem_norm', 'mem_w_q', 'mem_w_k', 'mem_w_v', 'mem_w_o', 'mem_q_norm', 'mem_k_norm', 'ffn2_norm', 'ffn2_w_gate', 'ffn2_w_up', 'ffn2_w_down']
TWIN_DIFF_INPUT = 'x'
TWIN_INPUTS = ['x', 'mem', 'positions', 'ffn1_norm', 'ffn1_w_gate', 'ffn1_w_up', 'ffn1_w_down', 'mix_norm', 'w_in', 'q_a_norm', 'w_q_up', 'kv_a_norm', 'w_kv_up', 'mla_q_norm', 'mla_k_norm', 'gla_w_gate2', 'gla_b_gate', 'gla_out_norm', 'w_out', 'mem_attn_norm', 'mem_norm', 'mem_w_q', 'mem_w_k', 'mem_w_v', 'mem_w_o', 'mem_q_norm', 'mem_k_norm', 'ffn2_norm', 'ffn2_w_gate', 'ffn2_w_up', 'ffn2_w_down', 'loss_target', 'm_ffn1_norm', 'm_ffn1_w_gate', 'm_ffn1_w_up', 'm_ffn1_w_down', 'm_mix_norm', 'm_w_in', 'm_q_a_norm', 'm_w_q_up', 'm_kv_a_norm', 'm_w_kv_up', 'm_mla_q_norm', 'm_mla_k_norm', 'm_gla_w_gate2', 'm_gla_b_gate', 'm_gla_out_norm', 'm_w_out', 'm_mem_attn_norm', 'm_mem_norm', 'm_mem_w_q', 'm_mem_w_k', 'm_mem_w_v', 'm_mem_w_o', 'm_mem_q_norm', 'm_mem_k_norm', 'm_ffn2_norm', 'm_ffn2_w_gate', 'm_ffn2_w_up', 'm_ffn2_w_down', 'v_ffn1_norm', 'v_ffn1_w_gate', 'v_ffn1_w_up', 'v_ffn1_w_down', 'v_mix_norm', 'v_w_in', 'v_q_a_norm', 'v_w_q_up', 'v_kv_a_norm', 'v_w_kv_up', 'v_mla_q_norm', 'v_mla_k_norm', 'v_gla_w_gate2', 'v_gla_b_gate', 'v_gla_out_norm', 'v_w_out', 'v_mem_attn_norm', 'v_mem_norm', 'v_mem_w_q', 'v_mem_w_k', 'v_mem_w_v', 'v_mem_w_o', 'v_mem_q_norm', 'v_mem_k_norm', 'v_ffn2_norm', 'v_ffn2_w_gate', 'v_ffn2_w_up', 'v_ffn2_w_down']
TWIN_OUTPUTS = ['loss', 'grad_x', 'grad_ffn1_norm', 'grad_ffn1_w_gate', 'grad_ffn1_w_up', 'grad_ffn1_w_down', 'grad_mix_norm', 'grad_w_in', 'grad_q_a_norm', 'grad_w_q_up', 'grad_kv_a_norm', 'grad_w_kv_up', 'grad_mla_q_norm', 'grad_mla_k_norm', 'grad_gla_w_gate2', 'grad_gla_b_gate', 'grad_gla_out_norm', 'grad_w_out', 'grad_mem_attn_norm', 'grad_mem_norm', 'grad_mem_w_q', 'grad_mem_w_k', 'grad_mem_w_v', 'grad_mem_w_o', 'grad_mem_q_norm', 'grad_mem_k_norm', 'grad_ffn2_norm', 'grad_ffn2_w_gate', 'grad_ffn2_w_up', 'grad_ffn2_w_down', 'delta_ffn1_norm', 'delta_ffn1_w_gate', 'delta_ffn1_w_up', 'delta_ffn1_w_down', 'delta_mix_norm', 'delta_w_in', 'delta_q_a_norm', 'delta_w_q_up', 'delta_kv_a_norm', 'delta_w_kv_up', 'delta_mla_q_norm', 'delta_mla_k_norm', 'delta_gla_w_gate2', 'delta_gla_b_gate', 'delta_gla_out_norm', 'delta_w_out', 'delta_mem_attn_norm', 'delta_mem_norm', 'delta_mem_w_q', 'delta_mem_w_k', 'delta_mem_w_v', 'delta_mem_w_o', 'delta_mem_q_norm', 'delta_mem_k_norm', 'delta_ffn2_norm', 'delta_ffn2_w_gate', 'delta_ffn2_w_up', 'delta_ffn2_w_down', 'new_m_ffn1_norm', 'new_m_ffn1_w_gate', 'new_m_ffn1_w_up', 'new_m_ffn1_w_down', 'new_m_mix_norm', 'new_m_w_in', 'new_m_q_a_norm', 'new_m_w_q_up', 'new_m_kv_a_norm', 'new_m_w_kv_up', 'new_m_mla_q_norm', 'new_m_mla_k_norm', 'new_m_gla_w_gate2', 'new_m_gla_b_gate', 'new_m_gla_out_norm', 'new_m_w_out', 'new_m_mem_attn_norm', 'new_m_mem_norm', 'new_m_mem_w_q', 'new_m_mem_w_k', 'new_m_mem_w_v', 'new_m_mem_w_o', 'new_m_mem_q_norm', 'new_m_mem_k_norm', 'new_m_ffn2_norm', 'new_m_ffn2_w_gate', 'new_m_ffn2_w_up', 'new_m_ffn2_w_down', 'new_v_ffn1_norm', 'new_v_ffn1_w_gate', 'new_v_ffn1_w_up', 'new_v_ffn1_w_down', 'new_v_mix_norm', 'new_v_w_in', 'new_v_q_a_norm', 'new_v_w_q_up', 'new_v_kv_a_norm', 'new_v_w_kv_up', 'new_v_mla_q_norm', 'new_v_mla_k_norm', 'new_v_gla_w_gate2', 'new_v_gla_b_gate', 'new_v_gla_out_norm', 'new_v_w_out', 'new_v_mem_attn_norm', 'new_v_mem_norm', 'new_v_mem_w_q', 'new_v_mem_w_k', 'new_v_mem_w_v', 'new_v_mem_w_o', 'new_v_mem_q_norm', 'new_v_mem_k_norm', 'new_v_ffn2_norm', 'new_v_ffn2_w_gate', 'new_v_ffn2_w_up', 'new_v_ffn2_w_down']
TWIN_LEAF_KINDS = {'loss': 'loss', 'grad_x': 'grad_x', 'grad_ffn1_norm': 'grad_w', 'grad_ffn1_w_gate': 'grad_w', 'grad_ffn1_w_up': 'grad_w', 'grad_ffn1_w_down': 'grad_w', 'grad_mix_norm': 'grad_w', 'grad_w_in': 'grad_w', 'grad_q_a_norm': 'grad_w', 'grad_w_q_up': 'grad_w', 'grad_kv_a_norm': 'grad_w', 'grad_w_kv_up': 'grad_w', 'grad_mla_q_norm': 'grad_w', 'grad_mla_k_norm': 'grad_w', 'grad_gla_w_gate2': 'grad_w', 'grad_gla_b_gate': 'grad_w', 'grad_gla_out_norm': 'grad_w', 'grad_w_out': 'grad_w', 'grad_mem_attn_norm': 'grad_w', 'grad_mem_norm': 'grad_w', 'grad_mem_w_q': 'grad_w', 'grad_mem_w_k': 'grad_w', 'grad_mem_w_v': 'grad_w', 'grad_mem_w_o': 'grad_w', 'grad_mem_q_norm': 'grad_w', 'grad_mem_k_norm': 'grad_w', 'grad_ffn2_norm': 'grad_w', 'grad_ffn2_w_gate': 'grad_w', 'grad_ffn2_w_up': 'grad_w', 'grad_ffn2_w_down': 'grad_w', 'delta_ffn1_norm': 'delta_w', 'delta_ffn1_w_gate': 'delta_w', 'delta_ffn1_w_up': 'delta_w', 'delta_ffn1_w_down': 'delta_w', 'delta_mix_norm': 'delta_w', 'delta_w_in': 'delta_w', 'delta_q_a_norm': 'delta_w', 'delta_w_q_up': 'delta_w', 'delta_kv_a_norm': 'delta_w', 'delta_w_kv_up': 'delta_w', 'delta_mla_q_norm': 'delta_w', 'delta_mla_k_norm': 'delta_w', 'delta_gla_w_gate2': 'delta_w', 'delta_gla_b_gate': 'delta_w', 'delta_gla_out_norm': 'delta_w', 'delta_w_out': 'delta_w', 'delta_mem_attn_norm': 'delta_w', 'delta_mem_norm': 'delta_w', 'delta_mem_w_q': 'delta_w', 'delta_mem_w_k': 'delta_w', 'delta_mem_w_v': 'delta_w', 'delta_mem_w_o': 'delta_w', 'delta_mem_q_norm': 'delta_w', 'delta_mem_k_norm': 'delta_w', 'delta_ffn2_norm': 'delta_w', 'delta_ffn2_w_gate': 'delta_w', 'delta_ffn2_w_up': 'delta_w', 'delta_ffn2_w_down': 'delta_w', 'new_m_ffn1_norm': 'new_m', 'new_m_ffn1_w_gate': 'new_m', 'new_m_ffn1_w_up': 'new_m', 'new_m_ffn1_w_down': 'new_m', 'new_m_mix_norm': 'new_m', 'new_m_w_in': 'new_m', 'new_m_q_a_norm': 'new_m', 'new_m_w_q_up': 'new_m', 'new_m_kv_a_norm': 'new_m', 'new_m_w_kv_up': 'new_m', 'new_m_mla_q_norm': 'new_m', 'new_m_mla_k_norm': 'new_m', 'new_m_gla_w_gate2': 'new_m', 'new_m_gla_b_gate': 'new_m', 'new_m_gla_out_norm': 'new_m', 'new_m_w_out': 'new_m', 'new_m_mem_attn_norm': 'new_m', 'new_m_mem_norm': 'new_m', 'new_m_mem_w_q': 'new_m', 'new_m_mem_w_k': 'new_m', 'new_m_mem_w_v': 'new_m', 'new_m_mem_w_o': 'new_m', 'new_m_mem_q_norm': 'new_m', 'new_m_mem_k_norm': 'new_m', 'new_m_ffn2_norm': 'new_m', 'new_m_ffn2_w_gate': 'new_m', 'new_m_ffn2_w_up': 'new_m', 'new_m_ffn2_w_down': 'new_m', 'new_v_ffn1_norm': 'new_v', 'new_v_ffn1_w_gate': 'new_v', 'new_v_ffn1_w_up': 'new_v', 'new_v_ffn1_w_down': 'new_v', 'new_v_mix_norm': 'new_v', 'new_v_w_in': 'new_v', 'new_v_q_a_norm': 'new_v', 'new_v_w_q_up': 'new_v', 'new_v_kv_a_norm': 'new_v', 'new_v_w_kv_up': 'new_v', 'new_v_mla_q_norm': 'new_v', 'new_v_mla_k_norm': 'new_v', 'new_v_gla_w_gate2': 'new_v', 'new_v_gla_b_gate': 'new_v', 'new_v_gla_out_norm': 'new_v', 'new_v_w_out': 'new_v', 'new_v_mem_attn_norm': 'new_v', 'new_v_mem_norm': 'new_v', 'new_v_mem_w_q': 'new_v', 'new_v_mem_w_k': 'new_v', 'new_v_mem_w_v': 'new_v', 'new_v_mem_w_o': 'new_v', 'new_v_mem_q_norm': 'new_v', 'new_v_mem_k_norm': 'new_v', 'new_v_ffn2_norm': 'new_v', 'new_v_ffn2_w_gate': 'new_v', 'new_v_ffn2_w_up': 'new_v', 'new_v_ffn2_w_down': 'new_v'}


def _forward(args):
    return _fwd_reference(*[args[k] for k in FWD_PARAMS])


def _output_shape():
    out = _jax.eval_shape(lambda: _forward(_fwd_setup_inputs(0)))
    return out.shape, out.dtype

N_MICROBATCH = 1
ADAM_LR = 0.001
ADAM_B1 = 0.9
ADAM_B2 = 0.999
ADAM_EPS = 1e-08
ADAM_WD = 0.01
ADAM_STEP = 10
PER_EXAMPLE_BATCH_AXIS = {'x': 0, 'mem': 0, 'positions': 0, 'loss_target': 0}
SHARED_INPUTS = []
_WEIGHT_DTYPES = {'ffn1_norm': _jnp.float32, 'ffn1_w_gate': _jnp.float32, 'ffn1_w_up': _jnp.float32, 'ffn1_w_down': _jnp.float32, 'mix_norm': _jnp.float32, 'w_in': _jnp.float32, 'q_a_norm': _jnp.float32, 'w_q_up': _jnp.float32, 'kv_a_norm': _jnp.float32, 'w_kv_up': _jnp.float32, 'mla_q_norm': _jnp.float32, 'mla_k_norm': _jnp.float32, 'gla_w_gate2': _jnp.float32, 'gla_b_gate': _jnp.float32, 'gla_out_norm': _jnp.float32, 'w_out': _jnp.float32, 'mem_attn_norm': _jnp.float32, 'mem_norm': _jnp.float32, 'mem_w_q': _jnp.float32, 'mem_w_k': _jnp.float32, 'mem_w_v': _jnp.float32, 'mem_w_o': _jnp.float32, 'mem_q_norm': _jnp.float32, 'mem_k_norm': _jnp.float32, 'ffn2_norm': _jnp.float32, 'ffn2_w_gate': _jnp.float32, 'ffn2_w_up': _jnp.float32, 'ffn2_w_down': _jnp.float32}
MOMENT_SCALE = {'ffn1_norm': 1.529297e+00, 'ffn1_w_gate': 3.023547e-02, 'ffn1_w_up': 3.014547e-02, 'ffn1_w_down': 4.928675e-02, 'mix_norm': 1.639511e+00, 'w_in': 9.264209e-02, 'q_a_norm': 2.356891e-02, 'w_q_up': 1.377817e-02, 'kv_a_norm': 2.028731e-01, 'w_kv_up': 1.721535e-02, 'mla_q_norm': 2.221310e-01, 'mla_k_norm': 2.221733e-01, 'gla_w_gate2': 1.395596e-02, 'gla_b_gate': 5.191998e-02, 'gla_out_norm': 1.135620e+01, 'w_out': 5.406000e-02, 'mem_attn_norm': 1.167439e-02, 'mem_norm': 8.721452e-02, 'mem_w_q': 2.498351e-02, 'mem_w_k': 2.526837e-02, 'mem_w_v': 6.634739e-02, 'mem_w_o': 1.885690e-02, 'mem_q_norm': 1.213755e+00, 'mem_k_norm': 1.217265e+00, 'ffn2_norm': 1.535475e+00, 'ffn2_w_gate': 1.983776e-02, 'ffn2_w_up': 2.057598e-02, 'ffn2_w_down': 3.284114e-02}


def _to_microbatches(a, axis):
    t = _jnp.moveaxis(a, axis, 0)
    t = t.reshape((N_MICROBATCH, t.shape[0] // N_MICROBATCH) + t.shape[1:])
    return _jnp.moveaxis(t, 1, axis + 1)


def setup_inputs(seed: int = 0) -> dict:
    inp = _fwd_setup_inputs(seed)
    key = _jax.random.fold_in(_jax.random.key(seed), 7919)
    shape, _ = _output_shape()
    out = dict(inp)
    out["loss_target"] = _jax.random.normal(_jax.random.fold_in(key, 0), shape, _jnp.float32)
    for i, name in enumerate(TWIN_WEIGHTS):
        w = inp[name].astype(_jnp.float32)
        if MOMENT_SCALE is None:
            s = _jnp.sqrt(_jnp.mean(_jnp.square(w)) + 1e-30)
        else:
            s = MOMENT_SCALE[name]
        km, kv = _jax.random.split(_jax.random.fold_in(key, i + 1))
        out[name] = w
        out["m_" + name] = s * _jax.random.normal(km, w.shape, _jnp.float32)
        out["v_" + name] = (s * s) * _jax.random.uniform(kv, w.shape, _jnp.float32, 0.5, 1.5)
    if N_MICROBATCH > 1:
        for name, axis in PER_EXAMPLE_BATCH_AXIS.items():
            out[name] = _to_microbatches(out[name], axis)
    return {'x': out['x'], 'mem': out['mem'], 'positions': out['positions'], 'ffn1_norm': out['ffn1_norm'], 'ffn1_w_gate': out['ffn1_w_gate'], 'ffn1_w_up': out['ffn1_w_up'], 'ffn1_w_down': out['ffn1_w_down'], 'mix_norm': out['mix_norm'], 'w_in': out['w_in'], 'q_a_norm': out['q_a_norm'], 'w_q_up': out['w_q_up'], 'kv_a_norm': out['kv_a_norm'], 'w_kv_up': out['w_kv_up'], 'mla_q_norm': out['mla_q_norm'], 'mla_k_norm': out['mla_k_norm'], 'gla_w_gate2': out['gla_w_gate2'], 'gla_b_gate': out['gla_b_gate'], 'gla_out_norm': out['gla_out_norm'], 'w_out': out['w_out'], 'mem_attn_norm': out['mem_attn_norm'], 'mem_norm': out['mem_norm'], 'mem_w_q': out['mem_w_q'], 'mem_w_k': out['mem_w_k'], 'mem_w_v': out['mem_w_v'], 'mem_w_o': out['mem_w_o'], 'mem_q_norm': out['mem_q_norm'], 'mem_k_norm': out['mem_k_norm'], 'ffn2_norm': out['ffn2_norm'], 'ffn2_w_gate': out['ffn2_w_gate'], 'ffn2_w_up': out['ffn2_w_up'], 'ffn2_w_down': out['ffn2_w_down'], 'loss_target': out['loss_target'], 'm_ffn1_norm': out['m_ffn1_norm'], 'm_ffn1_w_gate': out['m_ffn1_w_gate'], 'm_ffn1_w_up': out['m_ffn1_w_up'], 'm_ffn1_w_down': out['m_ffn1_w_down'], 'm_mix_norm': out['m_mix_norm'], 'm_w_in': out['m_w_in'], 'm_q_a_norm': out['m_q_a_norm'], 'm_w_q_up': out['m_w_q_up'], 'm_kv_a_norm': out['m_kv_a_norm'], 'm_w_kv_up': out['m_w_kv_up'], 'm_mla_q_norm': out['m_mla_q_norm'], 'm_mla_k_norm': out['m_mla_k_norm'], 'm_gla_w_gate2': out['m_gla_w_gate2'], 'm_gla_b_gate': out['m_gla_b_gate'], 'm_gla_out_norm': out['m_gla_out_norm'], 'm_w_out': out['m_w_out'], 'm_mem_attn_norm': out['m_mem_attn_norm'], 'm_mem_norm': out['m_mem_norm'], 'm_mem_w_q': out['m_mem_w_q'], 'm_mem_w_k': out['m_mem_w_k'], 'm_mem_w_v': out['m_mem_w_v'], 'm_mem_w_o': out['m_mem_w_o'], 'm_mem_q_norm': out['m_mem_q_norm'], 'm_mem_k_norm': out['m_mem_k_norm'], 'm_ffn2_norm': out['m_ffn2_norm'], 'm_ffn2_w_gate': out['m_ffn2_w_gate'], 'm_ffn2_w_up': out['m_ffn2_w_up'], 'm_ffn2_w_down': out['m_ffn2_w_down'], 'v_ffn1_norm': out['v_ffn1_norm'], 'v_ffn1_w_gate': out['v_ffn1_w_gate'], 'v_ffn1_w_up': out['v_ffn1_w_up'], 'v_ffn1_w_down': out['v_ffn1_w_down'], 'v_mix_norm': out['v_mix_norm'], 'v_w_in': out['v_w_in'], 'v_q_a_norm': out['v_q_a_norm'], 'v_w_q_up': out['v_w_q_up'], 'v_kv_a_norm': out['v_kv_a_norm'], 'v_w_kv_up': out['v_w_kv_up'], 'v_mla_q_norm': out['v_mla_q_norm'], 'v_mla_k_norm': out['v_mla_k_norm'], 'v_gla_w_gate2': out['v_gla_w_gate2'], 'v_gla_b_gate': out['v_gla_b_gate'], 'v_gla_out_norm': out['v_gla_out_norm'], 'v_w_out': out['v_w_out'], 'v_mem_attn_norm': out['v_mem_attn_norm'], 'v_mem_norm': out['v_mem_norm'], 'v_mem_w_q': out['v_mem_w_q'], 'v_mem_w_k': out['v_mem_w_k'], 'v_mem_w_v': out['v_mem_w_v'], 'v_mem_w_o': out['v_mem_w_o'], 'v_mem_q_norm': out['v_mem_q_norm'], 'v_mem_k_norm': out['v_mem_k_norm'], 'v_ffn2_norm': out['v_ffn2_norm'], 'v_ffn2_w_gate': out['v_ffn2_w_gate'], 'v_ffn2_w_up': out['v_ffn2_w_up'], 'v_ffn2_w_down': out['v_ffn2_w_down']}


def _loss(weights, diff, rest, loss_target):
    with _jax.named_scope("forward"):
        args = {**rest, TWIN_DIFF_INPUT: diff, **{k: w.astype(_WEIGHT_DTYPES[k]) for k, w in weights.items()}}
        y = _forward(args)
    with _jax.named_scope("loss_head"):
        err = _jnp.square(y.astype(_jnp.float32) - loss_target)
        return 0.5 * _jnp.sum(_jnp.mean(err, axis=-1)) if err.ndim else 0.5 * err


def _adamw(w, g, m, v):
    m = ADAM_B1 * m + (1.0 - ADAM_B1) * g
    v = ADAM_B2 * v + (1.0 - ADAM_B2) * _jnp.square(g)
    m_hat = m / (1.0 - ADAM_B1 ** ADAM_STEP)
    v_hat = v / (1.0 - ADAM_B2 ** ADAM_STEP)
    delta = -ADAM_LR * (m_hat / (_jnp.sqrt(v_hat) + ADAM_EPS) + ADAM_WD * w)
    return delta, m, v


def reference(x, mem, positions, ffn1_norm, ffn1_w_gate, ffn1_w_up, ffn1_w_down, mix_norm, w_in, q_a_norm, w_q_up, kv_a_norm, w_kv_up, mla_q_norm, mla_k_norm, gla_w_gate2, gla_b_gate, gla_out_norm, w_out, mem_attn_norm, mem_norm, mem_w_q, mem_w_k, mem_w_v, mem_w_o, mem_q_norm, mem_k_norm, ffn2_norm, ffn2_w_gate, ffn2_w_up, ffn2_w_down, loss_target, m_ffn1_norm, m_ffn1_w_gate, m_ffn1_w_up, m_ffn1_w_down, m_mix_norm, m_w_in, m_q_a_norm, m_w_q_up, m_kv_a_norm, m_w_kv_up, m_mla_q_norm, m_mla_k_norm, m_gla_w_gate2, m_gla_b_gate, m_gla_out_norm, m_w_out, m_mem_attn_norm, m_mem_norm, m_mem_w_q, m_mem_w_k, m_mem_w_v, m_mem_w_o, m_mem_q_norm, m_mem_k_norm, m_ffn2_norm, m_ffn2_w_gate, m_ffn2_w_up, m_ffn2_w_down, v_ffn1_norm, v_ffn1_w_gate, v_ffn1_w_up, v_ffn1_w_down, v_mix_norm, v_w_in, v_q_a_norm, v_w_q_up, v_kv_a_norm, v_w_kv_up, v_mla_q_norm, v_mla_k_norm, v_gla_w_gate2, v_gla_b_gate, v_gla_out_norm, v_w_out, v_mem_attn_norm, v_mem_norm, v_mem_w_q, v_mem_w_k, v_mem_w_v, v_mem_w_o, v_mem_q_norm, v_mem_k_norm, v_ffn2_norm, v_ffn2_w_gate, v_ffn2_w_up, v_ffn2_w_down):
    given = dict(x=x, mem=mem, positions=positions, ffn1_norm=ffn1_norm, ffn1_w_gate=ffn1_w_gate, ffn1_w_up=ffn1_w_up, ffn1_w_down=ffn1_w_down, mix_norm=mix_norm, w_in=w_in, q_a_norm=q_a_norm, w_q_up=w_q_up, kv_a_norm=kv_a_norm, w_kv_up=w_kv_up, mla_q_norm=mla_q_norm, mla_k_norm=mla_k_norm, gla_w_gate2=gla_w_gate2, gla_b_gate=gla_b_gate, gla_out_norm=gla_out_norm, w_out=w_out, mem_attn_norm=mem_attn_norm, mem_norm=mem_norm, mem_w_q=mem_w_q, mem_w_k=mem_w_k, mem_w_v=mem_w_v, mem_w_o=mem_w_o, mem_q_norm=mem_q_norm, mem_k_norm=mem_k_norm, ffn2_norm=ffn2_norm, ffn2_w_gate=ffn2_w_gate, ffn2_w_up=ffn2_w_up, ffn2_w_down=ffn2_w_down, loss_target=loss_target, m_ffn1_norm=m_ffn1_norm, m_ffn1_w_gate=m_ffn1_w_gate, m_ffn1_w_up=m_ffn1_w_up, m_ffn1_w_down=m_ffn1_w_down, m_mix_norm=m_mix_norm, m_w_in=m_w_in, m_q_a_norm=m_q_a_norm, m_w_q_up=m_w_q_up, m_kv_a_norm=m_kv_a_norm, m_w_kv_up=m_w_kv_up, m_mla_q_norm=m_mla_q_norm, m_mla_k_norm=m_mla_k_norm, m_gla_w_gate2=m_gla_w_gate2, m_gla_b_gate=m_gla_b_gate, m_gla_out_norm=m_gla_out_norm, m_w_out=m_w_out, m_mem_attn_norm=m_mem_attn_norm, m_mem_norm=m_mem_norm, m_mem_w_q=m_mem_w_q, m_mem_w_k=m_mem_w_k, m_mem_w_v=m_mem_w_v, m_mem_w_o=m_mem_w_o, m_mem_q_norm=m_mem_q_norm, m_mem_k_norm=m_mem_k_norm, m_ffn2_norm=m_ffn2_norm, m_ffn2_w_gate=m_ffn2_w_gate, m_ffn2_w_up=m_ffn2_w_up, m_ffn2_w_down=m_ffn2_w_down, v_ffn1_norm=v_ffn1_norm, v_ffn1_w_gate=v_ffn1_w_gate, v_ffn1_w_up=v_ffn1_w_up, v_ffn1_w_down=v_ffn1_w_down, v_mix_norm=v_mix_norm, v_w_in=v_w_in, v_q_a_norm=v_q_a_norm, v_w_q_up=v_w_q_up, v_kv_a_norm=v_kv_a_norm, v_w_kv_up=v_w_kv_up, v_mla_q_norm=v_mla_q_norm, v_mla_k_norm=v_mla_k_norm, v_gla_w_gate2=v_gla_w_gate2, v_gla_b_gate=v_gla_b_gate, v_gla_out_norm=v_gla_out_norm, v_w_out=v_w_out, v_mem_attn_norm=v_mem_attn_norm, v_mem_norm=v_mem_norm, v_mem_w_q=v_mem_w_q, v_mem_w_k=v_mem_w_k, v_mem_w_v=v_mem_w_v, v_mem_w_o=v_mem_w_o, v_mem_q_norm=v_mem_q_norm, v_mem_k_norm=v_mem_k_norm, v_ffn2_norm=v_ffn2_norm, v_ffn2_w_gate=v_ffn2_w_gate, v_ffn2_w_up=v_ffn2_w_up, v_ffn2_w_down=v_ffn2_w_down)
    weights = {n: given[n] for n in TWIN_WEIGHTS}
    shared = {n: given[n] for n in SHARED_INPUTS}
    per_example = {n: given[n] for n in ['x', 'mem', 'positions']}
    grad_fn = _jax.value_and_grad(_loss, argnums=(0, 1))

    def one_microbatch(ex, loss_target):
        ex = dict(ex)
        diff = ex.pop(TWIN_DIFF_INPUT)
        return grad_fn(weights, diff, {**shared, **ex}, loss_target)

    if N_MICROBATCH == 1:
        loss, (grad_w, grad_x) = one_microbatch(per_example, given["loss_target"])
    else:
        def body(carry, xs):
            loss_sum, grad_sum = carry
            l_k, (gw_k, gx_k) = one_microbatch(xs[0], xs[1])
            with _jax.named_scope("update"):
                return (loss_sum + l_k, _jax.tree.map(_jnp.add, grad_sum, gw_k)), gx_k

        init = (_jnp.zeros((), _jnp.float32), _jax.tree.map(_jnp.zeros_like, weights))
        (loss, grad_w), grad_x = _jax.lax.scan(body, init, (per_example, given["loss_target"]))
    with _jax.named_scope("update"):
        delta_w, new_m, new_v = {}, {}, {}
        for n in TWIN_WEIGHTS:
            delta_w[n], new_m[n], new_v[n] = _adamw(weights[n], grad_w[n], given["m_" + n], given["v_" + n])
    return (loss, grad_x, *[grad_w[n] for n in TWIN_WEIGHTS], *[delta_w[n] for n in TWIN_WEIGHTS],
            *[new_m[n] for n in TWIN_WEIGHTS], *[new_v[n] for n in TWIN_WEIGHTS])
```

```python
import functools
import math

import jax
import jax.numpy as jnp
from jax import lax
from jax.experimental import pallas as pl
from jax.experimental.pallas import tpu as pltpu

F32 = jnp.float32
BF16 = jnp.bfloat16

N_DEV = 8
EPS = 1e-6
CHUNK = 64
MLA_HEADS, MLA_NOPE, MLA_ROPE, MLA_V = 8, 128, 64, 128
MLA_QK = MLA_NOPE + MLA_ROPE
MLA_Q_RANK, MLA_KV_RANK = 512, 256
ROPE_THETA = 10000.0
GLA_HEADS, GLA_DK, GLA_DV, GLA_GATE_RANK = 4, 128, 256, 16
GLA_TAU = 16.0
MEM_HEADS, MEM_HEAD_DIM = 4, 128
ADAM_LR, ADAM_B1, ADAM_B2, ADAM_EPS, ADAM_WD, ADAM_STEP = 0.001, 0.9, 0.999, 1e-08, 0.01, 10

V7X_VMEM_BYTES = 64 * 1024 * 1024
LANES = 128

NN = (((1,), (0,)), ((), ()))
NT = (((1,), (1,)), ((), ()))
TN = (((0,), (0,)), ((), ()))

WEIGHTS = ['ffn1_norm', 'ffn1_w_gate', 'ffn1_w_up', 'ffn1_w_down', 'mix_norm', 'w_in', 'q_a_norm', 'w_q_up',
           'kv_a_norm', 'w_kv_up', 'mla_q_norm', 'mla_k_norm', 'gla_w_gate2', 'gla_b_gate', 'gla_out_norm', 'w_out',
           'mem_attn_norm', 'mem_norm', 'mem_w_q', 'mem_w_k', 'mem_w_v', 'mem_w_o', 'mem_q_norm', 'mem_k_norm',
           'ffn2_norm', 'ffn2_w_gate', 'ffn2_w_up', 'ffn2_w_down']
SMALL = ['ffn1_norm', 'mix_norm', 'q_a_norm', 'kv_a_norm', 'mla_q_norm', 'mla_k_norm', 'gla_b_gate', 'gla_out_norm',
         'mem_attn_norm', 'mem_norm', 'mem_q_norm', 'mem_k_norm', 'ffn2_norm']
GROUPS = [['ffn1_w_gate', 'ffn1_w_up', 'ffn1_w_down'],
          ['w_in', 'w_q_up', 'w_kv_up', 'gla_w_gate2', 'w_out'],
          ['mem_w_q', 'mem_w_k', 'mem_w_v', 'mem_w_o'],
          ['ffn2_w_gate', 'ffn2_w_up', 'ffn2_w_down']]
ARGS = (['x', 'mem', 'positions'] + WEIGHTS + ['loss_target'] + ['m_' + n for n in WEIGHTS] + ['v_' + n for n in WEIGHTS])

IN_REF = [('zq', 512), ('zkv', 256), ('zkr', 64), ('gq', 512), ('gk', 512), ('gv', 1024), ('zg', 16), ('zr', 1024)]
IN_PAD = [('zq', 512), ('zkv', 256), ('gq', 512), ('gk', 512), ('gv', 1024), ('zr', 1024), ('zkr', 128), ('zg', 128)]
IN_WIDTH = sum(n for _, n in IN_REF)
IN_PAD_WIDTH = sum(n for _, n in IN_PAD)


def _offsets(layout):
    out, off = {}, 0
    for name, n in layout:
        out[name] = off
        off += n
    return out


REF_OFF, PAD_OFF = _offsets(IN_REF), _offsets(IN_PAD)
REF_SIZE = dict(IN_REF)


def _tile(n, pref):
    return pref if n % pref == 0 else n


def _block_bytes(blk, dtype):
    dims = [d for d in blk if d is not None]
    if len(dims) >= 1:
        dims[-1] = -(-dims[-1] // LANES) * LANES
    return math.prod(dims) * jnp.dtype(dtype).itemsize


def _vmem_limit(pipelined_bytes, resident_bytes=0):
    need = 2 * pipelined_bytes + resident_bytes + (8 << 20)
    return int(min(max(need, 32 << 20), V7X_VMEM_BYTES - (6 << 20)))


def _rowwise_params(*semantics):
    return pltpu.CompilerParams(dimension_semantics=semantics, vmem_limit_bytes=48 << 20)


def _mm(name, a, b, grid, a_spec, b_spec, o_spec, out_shape, out_dtype, dims, scale=None, res=None):
    nk = grid[2]
    (a_blk, a_map), (b_blk, b_map), (o_blk, o_map) = a_spec, b_spec, o_spec
    acc_shape = tuple(d for d in o_blk if d is not None)

    def body(*refs):
        if res is None:
            a_ref, b_ref, o_ref = refs[:3]
            r_ref, rest = None, refs[3:]
        else:
            a_ref, b_ref, r_ref, o_ref = refs[:4]
            rest = refs[4:]

        def product():
            return lax.dot_general(a_ref[...].astype(BF16), b_ref[...].astype(BF16), dims, preferred_element_type=F32)

        def finish(r):
            if scale is not None:
                r = r * scale
            if r_ref is not None:
                r = r + r_ref[...].astype(F32)
            o_ref[...] = r.astype(o_ref.dtype)

        if nk == 1:
            finish(product())
        else:
            acc = rest[0]
            k = pl.program_id(2)

            @pl.when(k == 0)
            def _():
                acc[...] = product()

            @pl.when(k > 0)
            def _():
                acc[...] += product()

            @pl.when(k == nk - 1)
            def _():
                finish(acc[...])

    in_specs = [pl.BlockSpec(a_blk, a_map), pl.BlockSpec(b_blk, b_map)]
    operands = [a, b]
    piped = _block_bytes(a_blk, a.dtype) + _block_bytes(b_blk, b.dtype) + _block_bytes(o_blk, out_dtype)
    if res is not None:
        in_specs.append(pl.BlockSpec(o_blk, o_map))
        operands.append(res)
        piped += _block_bytes(o_blk, res.dtype)
    scratch = [pltpu.VMEM(acc_shape, F32)] if nk > 1 else []
    return pl.pallas_call(
        body, name=name, grid=grid, in_specs=in_specs, out_specs=pl.BlockSpec(o_blk, o_map),
        out_shape=jax.ShapeDtypeStruct(out_shape, out_dtype), scratch_shapes=scratch,
        compiler_params=pltpu.CompilerParams(
            dimension_semantics=("parallel", "parallel", "arbitrary"),
            vmem_limit_bytes=_vmem_limit(piped, 3 * _block_bytes(acc_shape, F32))),
    )(*operands)


def _mm2(name, a, b, dims, out_dtype, tm=512, tn=1024, tk=2048, scale=None, res=None):
    if dims is NN:
        (m, kk), n = a.shape, b.shape[1]
    elif dims is NT:
        (m, kk), n = a.shape, b.shape[0]
    else:
        (kk, m), n = a.shape, b.shape[1]
    tm, tn, tk = _tile(m, tm), _tile(n, tn), _tile(kk, tk)
    a_spec = ((tk, tm), lambda i, j, k: (k, i)) if dims is TN else ((tm, tk), lambda i, j, k: (i, k))
    b_spec = ((tn, tk), lambda i, j, k: (j, k)) if dims is NT else ((tk, tn), lambda i, j, k: (k, j))
    return _mm(name, a, b, (m // tm, n // tn, kk // tk), a_spec, b_spec, ((tm, tn), lambda i, j, k: (i, j)),
               (m, n), out_dtype, dims, scale, res)


def _mm_stack_out(name, a, b, out_dtype, tm=512, tk=2048):
    (m, kk), (nj, _, n) = a.shape, b.shape
    tm, tk = _tile(m, tm), _tile(kk, tk)
    return _mm(name, a, b, (nj, m // tm, kk // tk), ((tm, tk), lambda j, i, k: (i, k)),
               ((None, tk, n), lambda j, i, k: (j, k, 0)), ((None, tm, n), lambda j, i, k: (j, i, 0)),
               (nj, m, n), out_dtype, NN)


def _mm_stack_nt_out(name, a, b, out_dtype, scale=None, tm=512, tk=2048):
    (m, kk), (nj, n, _) = a.shape, b.shape
    tm, tk = _tile(m, tm), _tile(kk, tk)
    return _mm(name, a, b, (nj, m // tm, kk // tk), ((tm, tk), lambda j, i, k: (i, k)),
               ((None, n, tk), lambda j, i, k: (j, 0, k)), ((None, tm, n), lambda j, i, k: (j, i, 0)),
               (nj, m, n), out_dtype, NT, scale)


def _mm_stack_sum(name, a, b, out_dtype, scale=None, res=None, tm=512, tn=1024):
    (nj, m, f), n = a.shape, b.shape[2]
    tm, tn = _tile(m, tm), _tile(n, tn)
    return _mm(name, a, b, (m // tm, n // tn, nj), ((None, tm, f), lambda i, j, k: (k, i, 0)),
               ((None, f, tn), lambda i, j, k: (k, 0, j)), ((tm, tn), lambda i, j, k: (i, j)),
               (m, n), out_dtype, NN, scale, res)


def _mm_stack_nt_sum(name, a, b, out_dtype, res=None, tm=512, tn=1024):
    (nj, m, f), n = a.shape, b.shape[1]
    tm, tn = _tile(m, tm), _tile(n, tn)
    return _mm(name, a, b, (m // tm, n // tn, nj), ((None, tm, f), lambda i, j, k: (k, i, 0)),
               ((None, tn, f), lambda i, j, k: (k, j, 0)), ((tm, tn), lambda i, j, k: (i, j)),
               (m, n), out_dtype, NT, None, res)


def _mm_stack_tn_left(name, a, b, out_dtype, tm=1024, tk=512):
    (m, kp), (nj, _, n) = a.shape, b.shape
    tm, tk = _tile(kp, tm), _tile(m, tk)
    return _mm(name, a, b, (nj, kp // tm, m // tk), ((tk, tm), lambda j, i, k: (k, i)),
               ((None, tk, n), lambda j, i, k: (j, k, 0)), ((None, tm, n), lambda j, i, k: (j, i, 0)),
               (nj, kp, n), out_dtype, TN)


def _mm_stack_tn_right(name, a, b, out_dtype, scale=None, tn=2048, tk=512):
    (nj, m, f), n = a.shape, b.shape[1]
    tn, tk = _tile(n, tn), _tile(m, tk)
    return _mm(name, a, b, (nj, n // tn, m // tk), ((None, tk, f), lambda j, i, k: (j, k, 0)),
               ((tk, tn), lambda j, i, k: (k, i)), ((None, f, tn), lambda j, i, k: (j, 0, i)),
               (nj, f, n), out_dtype, TN, scale)


def _rms_fwd(name, x, g, out_dtype, tm=256):
    rows, cols = x.shape
    d = g.shape[1]
    tm = _tile(rows, tm)

    def body(x_ref, g_ref, o_ref):
        xf = x_ref[...].astype(F32)
        r = lax.rsqrt(jnp.mean(xf * xf, axis=-1, keepdims=True) + EPS)
        o_ref[...] = (xf * r * g_ref[...]).astype(o_ref.dtype)

    return pl.pallas_call(
        body, name=name, grid=(rows // tm, cols // d),
        in_specs=[pl.BlockSpec((tm, d), lambda i, c: (i, c)), pl.BlockSpec((1, d), lambda i, c: (0, 0))],
        out_specs=pl.BlockSpec((tm, d), lambda i, c: (i, c)),
        out_shape=jax.ShapeDtypeStruct((rows, cols), out_dtype),
        compiler_params=_rowwise_params("parallel", "parallel"),
    )(x, g)


def _rms_bwd(name, x, g, dy, res=None, tm=256):
    rows, cols = x.shape
    d = g.shape[1]
    tm = _tile(rows, tm)

    def body(*refs):
        if res is None:
            x_ref, g_ref, dy_ref, dx_ref, dg_ref = refs
            r_ref = None
        else:
            x_ref, g_ref, dy_ref, r_ref, dx_ref, dg_ref = refs
        xf = x_ref[...].astype(F32)
        r = lax.rsqrt(jnp.mean(xf * xf, axis=-1, keepdims=True) + EPS)
        xhat = xf * r
        dyf = dy_ref[...].astype(F32)

        @pl.when((pl.program_id(0) == 0) & (pl.program_id(1) == 0))
        def _():
            dg_ref[...] = jnp.zeros_like(dg_ref)

        dg_ref[...] += jnp.sum(dyf * xhat, axis=0, keepdims=True)
        dxh = dyf * g_ref[...]
        dx = r * (dxh - xhat * jnp.mean(dxh * xhat, axis=-1, keepdims=True))
        if r_ref is not None:
            dx = dx + r_ref[...].astype(F32)
        dx_ref[...] = dx

    blk = pl.BlockSpec((tm, d), lambda i, c: (i, c))
    in_specs = [blk, pl.BlockSpec((1, d), lambda i, c: (0, 0)), blk]
    operands = [x, g, dy]
    if res is not None:
        in_specs.append(blk)
        operands.append(res)
    return pl.pallas_call(
        body, name=name, grid=(rows // tm, cols // d), in_specs=in_specs,
        out_specs=(blk, pl.BlockSpec((1, d), lambda i, c: (0, 0))),
        out_shape=(jax.ShapeDtypeStruct((rows, cols), F32), jax.ShapeDtypeStruct((1, d), F32)),
        compiler_params=_rowwise_params("arbitrary", "arbitrary"),
    )(*operands)


def _swiglu_fwd(name, g, u, out_dtype, tm=256):
    rows, cols = g.shape
    tm = _tile(rows, tm)

    def body(g_ref, u_ref, o_ref):
        gf = g_ref[...].astype(F32)
        o_ref[...] = (gf * jax.nn.sigmoid(gf) * u_ref[...].astype(F32)).astype(o_ref.dtype)

    blk = pl.BlockSpec((tm, cols), lambda i: (i, 0))
    return pl.pallas_call(
        body, name=name, grid=(rows // tm,), in_specs=[blk, blk], out_specs=blk,
        out_shape=jax.ShapeDtypeStruct((rows, cols), out_dtype),
        compiler_params=_rowwise_params("parallel"),
    )(g, u)


def _swiglu_bwd(name, da, g, u, out_dtype, tm=256):
    rows, cols = g.shape
    tm = _tile(rows, tm)

    def body(da_ref, g_ref, u_ref, dg_ref, du_ref):
        gf = g_ref[...].astype(F32)
        daf = da_ref[...].astype(F32)
        sig = jax.nn.sigmoid(gf)
        du_ref[...] = (daf * gf * sig).astype(du_ref.dtype)
        dg_ref[...] = (daf * u_ref[...].astype(F32) * sig * (1.0 + gf * (1.0 - sig))).astype(dg_ref.dtype)

    blk = pl.BlockSpec((tm, cols), lambda i: (i, 0))
    sds = jax.ShapeDtypeStruct((rows, cols), out_dtype)
    return pl.pallas_call(
        body, name=name, grid=(rows // tm,), in_specs=[blk, blk, blk], out_specs=(blk, blk), out_shape=(sds, sds),
        compiler_params=_rowwise_params("parallel"),
    )(da, g, u)


def _split3(x):
    hi = x.astype(BF16)
    r1 = x - hi.astype(F32)
    mid = r1.astype(BF16)
    lo = (r1 - mid.astype(F32)).astype(BF16)
    return hi, mid, lo


def _rope(name, x, cos_t, sin_t, swap, out_dtype, backward, tm=256):
    rows, d = x.shape
    t = cos_t.shape[0]
    tm = _tile(t, tm)
    nt = t // tm

    def body(x_ref, c_ref, s_ref, p_ref, o_ref):
        xf = x_ref[...].astype(F32)
        p = p_ref[...]
        lhs = xf * s_ref[...] if backward else xf
        sw = sum(jnp.dot(part, p, preferred_element_type=F32) for part in _split3(lhs))
        if not backward:
            sw = sw * s_ref[...]
        o_ref[...] = (xf * c_ref[...] + sw).astype(o_ref.dtype)

    blk = pl.BlockSpec((tm, d), lambda i: (i, 0))
    tab = pl.BlockSpec((tm, d), lambda i: (i % nt, 0))
    return pl.pallas_call(
        body, name=name, grid=(rows // tm,), in_specs=[blk, tab, tab, pl.BlockSpec((d, d), lambda i: (0, 0))],
        out_specs=blk, out_shape=jax.ShapeDtypeStruct((rows, d), out_dtype),
        compiler_params=_rowwise_params("parallel"),
    )(x, cos_t, sin_t, swap)


def _gate_fwd(name, pre, bias, tm=256):
    rows, cols = pre.shape
    tm = _tile(rows, tm)

    def body(p_ref, b_ref, o_ref):
        z = p_ref[...] + b_ref[...]
        o_ref[...] = (jnp.minimum(z, 0.0) - jnp.log(1.0 + jnp.exp(-jnp.abs(z)))) * (1.0 / GLA_TAU)

    blk = pl.BlockSpec((tm, cols), lambda i: (i, 0))
    return pl.pallas_call(
        body, name=name, grid=(rows // tm,), in_specs=[blk, pl.BlockSpec((1, cols), lambda i: (0, 0))], out_specs=blk,
        out_shape=jax.ShapeDtypeStruct((rows, cols), F32),
        compiler_params=_rowwise_params("parallel"),
    )(pre, bias)


def _gate_bwd(name, pre, bias, dla, tm=256):
    rows, cols = pre.shape
    tm = _tile(rows, tm)

    def body(p_ref, b_ref, d_ref, dp_ref, db_ref):
        z = p_ref[...] + b_ref[...]
        dp = d_ref[...] * (1.0 / GLA_TAU) / (1.0 + jnp.exp(z))
        dp_ref[...] = dp

        @pl.when(pl.program_id(0) == 0)
        def _():
            db_ref[...] = jnp.zeros_like(db_ref)

        db_ref[...] += jnp.sum(dp, axis=0, keepdims=True)

    blk = pl.BlockSpec((tm, cols), lambda i: (i, 0))
    row = pl.BlockSpec((1, cols), lambda i: (0, 0))
    return pl.pallas_call(
        body, name=name, grid=(rows // tm,), in_specs=[blk, row, blk], out_specs=(blk, row),
        out_shape=(jax.ShapeDtypeStruct((rows, cols), F32), jax.ShapeDtypeStruct((1, cols), F32)),
        compiler_params=_rowwise_params("arbitrary"),
    )(pre, bias, dla)


def _loss(name, y, target, tm=256):
    rows, d = y.shape
    tm = _tile(rows, tm)

    def body(y_ref, t_ref, dy_ref, l_ref):
        err = y_ref[...] - t_ref[...]
        dy_ref[...] = err * (1.0 / d)

        @pl.when(pl.program_id(0) == 0)
        def _():
            l_ref[...] = jnp.zeros_like(l_ref)

        sq = (err * err).reshape(tm // 8, 8, d)
        l_ref[...] += jnp.sum(sq, axis=0) * (0.5 / d)

    blk = pl.BlockSpec((tm, d), lambda i: (i, 0))
    return pl.pallas_call(
        body, name=name, grid=(rows // tm,), in_specs=[blk, blk],
        out_specs=(blk, pl.BlockSpec((8, d), lambda i: (0, 0))),
        out_shape=(jax.ShapeDtypeStruct((rows, d), F32), jax.ShapeDtypeStruct((8, d), F32)),
        compiler_params=_rowwise_params("arbitrary"),
    )(y, target)


def _scores(q, k, scale, causal, q0):
    s = lax.dot_general(q, k, NT, preferred_element_type=F32) * scale
    if causal:
        qc = (q0 + lax.broadcasted_iota(jnp.int32, s.shape, 0)) // CHUNK
        kc = lax.broadcasted_iota(jnp.int32, s.shape, 1) // CHUNK
        s = jnp.where(kc <= qc, s, -1e30)
    e = jnp.exp(s - jnp.max(s, axis=-1, keepdims=True))
    return e, jnp.sum(e, axis=-1, keepdims=True)


def _attn_fwd(name, q, k, v, scale, causal, tq=256):
    nh, t, dk = q.shape
    tk, dv = k.shape[1], v.shape[2]
    tq = _tile(t, tq)

    def body(q_ref, k_ref, v_ref, o_ref):
        e, l = _scores(q_ref[...], k_ref[...], scale, causal, pl.program_id(1) * tq)
        o = jnp.dot(e.astype(BF16), v_ref[...], preferred_element_type=F32)
        o_ref[...] = (o / l).astype(o_ref.dtype)

    return pl.pallas_call(
        body, name=name, grid=(nh, t // tq),
        in_specs=[pl.BlockSpec((None, tq, dk), lambda h, i: (h, i, 0)), pl.BlockSpec((None, tk, dk), lambda h, i: (h, 0, 0)),
                  pl.BlockSpec((None, tk, dv), lambda h, i: (h, 0, 0))],
        out_specs=pl.BlockSpec((tq, dv), lambda h, i: (i, h)),
        out_shape=jax.ShapeDtypeStruct((t, nh * dv), BF16),
        compiler_params=pltpu.CompilerParams(dimension_semantics=("parallel", "parallel"),
                                             vmem_limit_bytes=_vmem_limit(0, 6 * tq * tk * 4)),
    )(q, k, v)


def _attn_bwd(name, q, k, v, do, scale, causal, tq=256):
    nh, t, dk = q.shape
    tk, dv = k.shape[1], v.shape[2]
    tq = _tile(t, tq)

    def body(q_ref, k_ref, v_ref, do_ref, dq_ref, dk_ref, dv_ref):
        qb, kb = q_ref[...], k_ref[...]
        e, l = _scores(qb, kb, scale, causal, pl.program_id(1) * tq)
        p = e / l
        dob = do_ref[...].astype(BF16)
        dp = lax.dot_general(dob, v_ref[...], NT, preferred_element_type=F32)
        ds = (p * (dp - jnp.sum(p * dp, axis=-1, keepdims=True)) * scale).astype(BF16)
        dq_ref[...] = jnp.dot(ds, kb, preferred_element_type=F32)

        @pl.when(pl.program_id(1) == 0)
        def _():
            dk_ref[...] = jnp.zeros_like(dk_ref)
            dv_ref[...] = jnp.zeros_like(dv_ref)

        dk_ref[...] += lax.dot_general(ds, qb, TN, preferred_element_type=F32)
        dv_ref[...] += lax.dot_general(p.astype(BF16), dob, TN, preferred_element_type=F32)

    return pl.pallas_call(
        body, name=name, grid=(nh, t // tq),
        in_specs=[pl.BlockSpec((None, tq, dk), lambda h, i: (h, i, 0)), pl.BlockSpec((None, tk, dk), lambda h, i: (h, 0, 0)),
                  pl.BlockSpec((None, tk, dv), lambda h, i: (h, 0, 0)), pl.BlockSpec((tq, dv), lambda h, i: (i, h))],
        out_specs=(pl.BlockSpec((None, tq, dk), lambda h, i: (h, i, 0)), pl.BlockSpec((None, tk, dk), lambda h, i: (h, 0, 0)),
                   pl.BlockSpec((None, tk, dv), lambda h, i: (h, 0, 0))),
        out_shape=(jax.ShapeDtypeStruct((nh, t, dk), F32), jax.ShapeDtypeStruct((nh, tk, dk), F32),
                   jax.ShapeDtypeStruct((nh, tk, dv), F32)),
        compiler_params=pltpu.CompilerParams(dimension_semantics=("parallel", "arbitrary"),
                                             vmem_limit_bytes=_vmem_limit(0, 10 * tq * tk * 4)),
    )(q, k, v, do)


def _tri(lower):
    r = lax.broadcasted_iota(jnp.int32, (CHUNK, CHUNK), 0)
    c = lax.broadcasted_iota(jnp.int32, (CHUNK, CHUNK), 1)
    return jnp.where((c <= r) if lower else (c >= r), 1.0, 0.0).astype(BF16)


def _tri_dot(tri, x):
    return sum(jnp.dot(tri, part, preferred_element_type=F32) for part in _split3(x))


def _gla_fwd(name, q, k, v, la, nh):
    t = q.shape[0]
    dk, dv = q.shape[1] // nh, v.shape[1] // nh
    nc = t // CHUNK

    def body(q_ref, k_ref, v_ref, g_ref, o_ref, st_ref, state):
        @pl.when(pl.program_id(1) == 0)
        def _():
            state[...] = jnp.zeros_like(state)

        g = g_ref[...]
        b = _tri_dot(_tri(True), g)
        b_end = jnp.sum(g, axis=0, keepdims=True)
        k_dec = (k_ref[...] * jnp.exp(b_end - b)).astype(BF16)
        u_t = lax.dot_general(v_ref[...].astype(BF16), k_dec, TN, preferred_element_type=F32)
        new = state[...] * jnp.exp(b_end) + u_t
        state[...] = new
        st_ref[...] = new
        qc = (q_ref[...] * (dk ** -0.5)).astype(BF16)
        o_ref[...] = lax.dot_general(qc, new.astype(BF16), NT, preferred_element_type=F32)

    kblk = pl.BlockSpec((CHUNK, dk), lambda h, n: (n, h))
    vblk = pl.BlockSpec((CHUNK, dv), lambda h, n: (n, h))
    return pl.pallas_call(
        body, name=name, grid=(nh, nc), in_specs=[kblk, kblk, vblk, kblk],
        out_specs=(vblk, pl.BlockSpec((None, None, dv, dk), lambda h, n: (h, n, 0, 0))),
        out_shape=(jax.ShapeDtypeStruct((t, nh * dv), F32), jax.ShapeDtypeStruct((nh, nc, dv, dk), F32)),
        scratch_shapes=[pltpu.VMEM((dv, dk), F32)],
        compiler_params=_rowwise_params("parallel", "arbitrary"),
    )(q, k, v, la)


def _gla_bwd(name, q, k, v, la, states, do, nh):
    t = q.shape[0]
    dk, dv = q.shape[1] // nh, v.shape[1] // nh
    nc = t // CHUNK
    scale = dk ** -0.5

    def body(q_ref, k_ref, v_ref, g_ref, do_ref, st_ref, sp_ref, dq_ref, dk_ref, dv_ref, dg_ref, carry):
        i = pl.program_id(1)

        @pl.when(i == 0)
        def _():
            carry[...] = jnp.zeros_like(carry)

        g = g_ref[...]
        b = _tri_dot(_tri(True), g)
        b_end = jnp.sum(g, axis=0, keepdims=True)
        w = jnp.exp(b_end - b)
        decay = jnp.exp(b_end)
        k_dec = k_ref[...] * w
        qc = (q_ref[...] * scale).astype(BF16)
        dob = do_ref[...].astype(BF16)
        dq_ref[...] = jnp.dot(dob, st_ref[...].astype(BF16), preferred_element_type=F32) * scale
        g_t = carry[...] + lax.dot_general(dob, qc, TN, preferred_element_type=F32)
        g_tb = g_t.astype(BF16)
        dk_dec = jnp.dot(v_ref[...].astype(BF16), g_tb, preferred_element_type=F32)
        dv_ref[...] = lax.dot_general(k_dec.astype(BF16), g_tb, NT, preferred_element_type=F32)
        prev = jnp.where(i < nc - 1, sp_ref[...], 0.0)
        ddecay = jnp.sum(g_t * prev, axis=0, keepdims=True)
        dk_ref[...] = dk_dec * w
        e = dk_dec * k_dec
        db_end = jnp.sum(e, axis=0, keepdims=True) + ddecay * decay
        dg_ref[...] = _tri_dot(_tri(False), -e) + db_end
        carry[...] = g_t * decay

    kblk = pl.BlockSpec((CHUNK, dk), lambda h, i: (nc - 1 - i, h))
    vblk = pl.BlockSpec((CHUNK, dv), lambda h, i: (nc - 1 - i, h))
    ksds = jax.ShapeDtypeStruct((t, nh * dk), F32)
    return pl.pallas_call(
        body, name=name, grid=(nh, nc),
        in_specs=[kblk, kblk, vblk, kblk, vblk,
                  pl.BlockSpec((None, None, dv, dk), lambda h, i: (h, nc - 1 - i, 0, 0)),
                  pl.BlockSpec((None, None, dv, dk), lambda h, i: (h, jnp.maximum(nc - 2 - i, 0), 0, 0))],
        out_specs=(kblk, kblk, vblk, kblk),
        out_shape=(ksds, ksds, jax.ShapeDtypeStruct((t, nh * dv), F32), ksds),
        scratch_shapes=[pltpu.VMEM((dv, dk), F32)],
        compiler_params=_rowwise_params("parallel", "arbitrary"),
    )(q, k, v, la, do, states, states)


def _adamw(name, w, m, v, parts):
    rows, cols = w.shape
    tm = _tile(rows, 1 << int(math.log2(max(8, (1 << 17) // (-(-cols // LANES) * LANES)))))

    def body(w_ref, m_ref, v_ref, p_ref, g_ref, d_ref, nm_ref, nv_ref):
        g = p_ref[0].astype(F32)
        for s in range(1, N_DEV):
            g = g + p_ref[s].astype(F32)
        m_new = ADAM_B1 * m_ref[...] + (1.0 - ADAM_B1) * g
        v_new = ADAM_B2 * v_ref[...] + (1.0 - ADAM_B2) * jnp.square(g)
        m_hat = m_new / (1.0 - ADAM_B1 ** ADAM_STEP)
        v_hat = v_new / (1.0 - ADAM_B2 ** ADAM_STEP)
        g_ref[...] = g
        d_ref[...] = -ADAM_LR * (m_hat / (jnp.sqrt(v_hat) + ADAM_EPS) + ADAM_WD * w_ref[...])
        nm_ref[...] = m_new
        nv_ref[...] = v_new

    blk = pl.BlockSpec((tm, cols), lambda i: (i, 0))
    sds = jax.ShapeDtypeStruct((rows, cols), F32)
    return pl.pallas_call(
        body, name=name, grid=(rows // tm,),
        in_specs=[blk, blk, blk, pl.BlockSpec((N_DEV, tm, cols), lambda i: (0, i, 0))],
        out_specs=(blk, blk, blk, blk), out_shape=(sds, sds, sds, sds),
        compiler_params=_rowwise_params("parallel"),
    )(w, m, v, parts)


HBM = pl.BlockSpec(memory_space=pl.ANY)
MESH = pl.DeviceIdType.MESH


def _all_gather(name, shards):
    n = len(shards)

    def body(*refs):
        ins, outs = refs[:n], refs[n:2 * n]
        send_sems, recv_sems, local_sems = refs[2 * n:]
        x, y, c = lax.axis_index("x"), lax.axis_index("y"), lax.axis_index("c")
        me, sibling = (x, y, c), (x, y, 1 - c)
        chips = [(1 - x, y), (x, 1 - y), (1 - x, 1 - y)]

        def copy(w, k, block, to, src=None):
            dst = outs[w].at[4 * block[0] + 2 * block[1] + block[2]]
            return pltpu.make_async_remote_copy(
                src_ref=dst if src is None else src, dst_ref=dst, send_sem=send_sems.at[7 * w + k],
                recv_sem=recv_sems.at[7 * w + k], device_id=to, device_id_type=MESH)

        mine, first, passed = [], [], []
        for w in range(n):
            cp = pltpu.make_async_copy(ins[w], outs[w].at[4 * x + 2 * y + c], local_sems.at[w])
            cp.start()
            mine.append(cp)
            first.append(copy(w, 0, me, sibling, src=ins[w]))
            first += [copy(w, 1 + j, me, (*chip, c), src=ins[w]) for j, chip in enumerate(chips)]
        for cp in first:
            cp.start()
        for w in range(n):
            for j, chip in enumerate(chips):
                copy(w, 1 + j, (*chip, c), me).wait_recv()
                cp = copy(w, 4 + j, (*chip, c), sibling)
                cp.start()
                passed.append(cp)
        for w in range(n):
            copy(w, 0, sibling, me).wait_recv()
            for j, chip in enumerate(chips):
                copy(w, 4 + j, (*chip, 1 - c), me).wait_recv()
        for cp in first + passed:
            cp.wait_send()
        for cp in mine:
            cp.wait()

    return pl.pallas_call(
        body, name=name, in_specs=[HBM] * n, out_specs=[HBM] * n,
        out_shape=[jax.ShapeDtypeStruct((N_DEV,) + s.shape, s.dtype) for s in shards],
        scratch_shapes=[pltpu.SemaphoreType.DMA((7 * n,)), pltpu.SemaphoreType.DMA((7 * n,)),
                        pltpu.SemaphoreType.DMA((n,))],
    )(*shards)


def _exchange(name, stacks):
    n = len(stacks)

    def body(*refs):
        ins, outs = refs[:n], refs[n:2 * n]
        send_sems, recv_sems, local_sems = refs[2 * n:]
        x, y, c = lax.axis_index("x"), lax.axis_index("y"), lax.axis_index("c")
        my_block = 4 * x + 2 * y + c
        copies = []
        for w in range(n):
            cp = pltpu.make_async_copy(ins[w].at[my_block], outs[w].at[my_block], local_sems.at[w])
            cp.start()
            copies.append(cp)
            for k in range(1, N_DEV):
                px, py, pc = x ^ (k >> 2), y ^ ((k >> 1) & 1), c ^ (k & 1)
                cp = pltpu.make_async_remote_copy(
                    src_ref=ins[w].at[4 * px + 2 * py + pc], dst_ref=outs[w].at[my_block],
                    send_sem=send_sems.at[7 * w + k - 1], recv_sem=recv_sems.at[7 * w + k - 1],
                    device_id=(px, py, pc), device_id_type=MESH)
                cp.start()
                copies.append(cp)
        for cp in copies:
            cp.wait()

    return pl.pallas_call(
        body, name=name, in_specs=[HBM] * n, out_specs=[HBM] * n,
        out_shape=[jax.ShapeDtypeStruct(s.shape, s.dtype) for s in stacks],
        scratch_shapes=[pltpu.SemaphoreType.DMA((7 * n,)), pltpu.SemaphoreType.DMA((7 * n,)),
                        pltpu.SemaphoreType.DMA((n,))],
    )(*stacks)


def _unstack_cols(w):
    return w.transpose(1, 0, 2).reshape(w.shape[1], N_DEV * w.shape[2])


def _stack_cols(w):
    return w.reshape(w.shape[0], N_DEV, w.shape[1] // N_DEV).transpose(1, 0, 2)


def _rope_tables(positions):
    half = MLA_ROPE // 2
    inv_freq = ROPE_THETA ** (-jnp.arange(half, dtype=F32) / half)
    ang = positions.astype(F32)[:, None] * inv_freq
    cos, sin = jnp.cos(ang), jnp.sin(ang)
    t = positions.shape[0]
    cos_t = jnp.concatenate([jnp.ones((t, MLA_NOPE), F32), cos, cos], axis=1)
    sin_t = jnp.concatenate([jnp.zeros((t, MLA_NOPE), F32), -sin, sin], axis=1)
    idx = jnp.arange(MLA_QK)
    partner = jnp.where(idx < MLA_NOPE, -1, jnp.where(idx < MLA_NOPE + half, idx + half, idx - half))
    swap = (idx[:, None] == partner[None, :]).astype(BF16)
    return cos_t, sin_t, swap


def _ffn_fwd(tag, x, gain, wg, wu, wd):
    t = x.shape[0]
    f = wg.shape[2]
    h = _rms_fwd(tag + "_rms", x, gain, BF16)
    g = _mm_stack_out(tag + "_gate", h, wg, BF16)
    u = _mm_stack_out(tag + "_up", h, wu, BF16)
    a = _swiglu_fwd(tag + "_act", g.reshape(N_DEV * t, f), u.reshape(N_DEV * t, f), BF16).reshape(N_DEV, t, f)
    y = _mm_stack_sum(tag + "_down", a, wd, F32, scale=0.5, res=x)
    return y, (x, h, g, u, a)


def _ffn_bwd(tag, dy, saved, gain, wg, wu, wd):
    x, h, g, u, a = saved
    t = x.shape[0]
    f = wg.shape[2]
    da = _mm_stack_nt_out(tag + "_da", dy, wd, BF16, scale=0.5)
    dwd = _mm_stack_tn_right(tag + "_dwd", a, dy, BF16, scale=0.5)
    dg, du = _swiglu_bwd(tag + "_dact", da.reshape(N_DEV * t, f), g.reshape(N_DEV * t, f), u.reshape(N_DEV * t, f), BF16)
    dg, du = dg.reshape(N_DEV, t, f), du.reshape(N_DEV, t, f)
    dwg = _mm_stack_tn_left(tag + "_dwg", h, dg, BF16)
    dwu = _mm_stack_tn_left(tag + "_dwu", h, du, BF16)
    dh = _mm_stack_nt_sum(tag + "_dh_g", dg, wg, F32)
    dh = _mm_stack_nt_sum(tag + "_dh_u", du, wu, F32, res=dh)
    dx, dgain = _rms_bwd(tag + "_drms", x, gain, dh, res=dy)
    return dx, dgain, dwg, dwu, dwd


def _local_step(x, mem, positions, sm, wt, target):
    t, d = x.shape
    nm = mem.shape[0]
    gs, gw = {}, {}

    x1, ffn1_saved = _ffn_fwd("ffn1", x, sm['ffn1_norm'], wt['ffn1_w_gate'], wt['ffn1_w_up'], wt['ffn1_w_down'])

    w_in_ref = _unstack_cols(wt['w_in'])
    pieces = []
    for name, n in IN_PAD:
        piece = w_in_ref[:, REF_OFF[name]:REF_OFF[name] + REF_SIZE[name]]
        if n != REF_SIZE[name]:
            piece = jnp.pad(piece, ((0, 0), (0, n - REF_SIZE[name])))
        pieces.append(piece)
    w_in = jnp.concatenate(pieces, axis=1)
    h2 = _rms_fwd("mix_rms", x1, sm['mix_norm'], BF16)
    z = _mm2("mix_in", h2, w_in, NN, F32)
    zs = {name: z[:, PAD_OFF[name]:PAD_OFF[name] + n] for name, n in IN_PAD}

    cq = _rms_fwd("mla_q_a_rms", zs['zq'], sm['q_a_norm'], BF16)
    q_raw = _mm_stack_out("mla_q_up", cq, wt['w_q_up'], F32)
    ckv = _rms_fwd("mla_kv_a_rms", zs['zkv'], sm['kv_a_norm'], BF16)
    kv = _mm_stack_out("mla_kv_up", ckv, wt['w_kv_up'], F32)
    zkr = zs['zkr'][:, :MLA_ROPE]
    k_raw = jnp.concatenate([kv[:, :, :MLA_NOPE], jnp.broadcast_to(zkr[None], (MLA_HEADS, t, MLA_ROPE))], axis=2)
    v_mla = kv[:, :, MLA_NOPE:].astype(BF16)
    cos_t, sin_t, swap = _rope_tables(positions)
    q_raw2, k_raw2 = q_raw.reshape(MLA_HEADS * t, MLA_QK), k_raw.reshape(MLA_HEADS * t, MLA_QK)
    qn = _rms_fwd("mla_q_rms", q_raw2, sm['mla_q_norm'], F32)
    kn = _rms_fwd("mla_k_rms", k_raw2, sm['mla_k_norm'], F32)
    qf = _rope("mla_q_rope", qn, cos_t, sin_t, swap, BF16, False).reshape(MLA_HEADS, t, MLA_QK)
    kf = _rope("mla_k_rope", kn, cos_t, sin_t, swap, BF16, False).reshape(MLA_HEADS, t, MLA_QK)
    o_mla = _attn_fwd("mla_attn", qf, kf, v_mla, MLA_QK ** -0.5, True)

    w_g2 = jnp.pad(_unstack_cols(wt['gla_w_gate2']), ((0, LANES - GLA_GATE_RANK), (0, 0)))
    pre = _mm2("gla_gate_pre", zs['zg'], w_g2, NN, F32)
    log_a = _gate_fwd("gla_gate", pre, sm['gla_b_gate'])
    o_gla_raw, states = _gla_fwd("gla_scan", zs['gq'], zs['gk'], zs['gv'], log_a, GLA_HEADS)
    o_gla_n = _rms_fwd("gla_out_rms", o_gla_raw, sm['gla_out_norm'], F32)
    o_gla = _swiglu_fwd("gla_out_gate", zs['zr'], o_gla_n, BF16)

    cat = jnp.concatenate([o_mla, o_gla], axis=1)
    w_out = wt['w_out'].reshape(d, d)
    x2 = _mm2("mix_out", cat, w_out, NN, F32, res=x1)

    w_mq, w_mk, w_mv = (wt[n].reshape(d, MEM_HEADS * MEM_HEAD_DIM) for n in ('mem_w_q', 'mem_w_k', 'mem_w_v'))
    hq = _rms_fwd("mem_attn_rms", x2, sm['mem_attn_norm'], BF16)
    hm = _rms_fwd("mem_rms", mem, sm['mem_norm'], BF16)

    def heads_out(name, a, b, out_dtype):
        m, kk = a.shape
        tm = _tile(m, 512)
        return _mm(name, a, b, (m // tm, MEM_HEADS, 1), ((tm, kk), lambda i, h, k: (i, 0)),
                   ((kk, MEM_HEAD_DIM), lambda i, h, k: (0, h)), ((None, tm, MEM_HEAD_DIM), lambda i, h, k: (h, i, 0)),
                   (MEM_HEADS, m, MEM_HEAD_DIM), out_dtype, NN)

    mq_raw = heads_out("mem_q", hq, w_mq, F32)
    mk_raw = heads_out("mem_k", hm, w_mk, F32)
    mv = heads_out("mem_v", hm, w_mv, BF16)
    mq = _rms_fwd("mem_q_rms", mq_raw.reshape(MEM_HEADS * t, MEM_HEAD_DIM), sm['mem_q_norm'], BF16)
    mk = _rms_fwd("mem_k_rms", mk_raw.reshape(MEM_HEADS * nm, MEM_HEAD_DIM), sm['mem_k_norm'], BF16)
    mq, mk = mq.reshape(MEM_HEADS, t, MEM_HEAD_DIM), mk.reshape(MEM_HEADS, nm, MEM_HEAD_DIM)
    o_mem = _attn_fwd("mem_attn", mq, mk, mv, MEM_HEAD_DIM ** -0.5, False)
    w_mo = wt['mem_w_o']
    mo_cols = w_mo.shape[2]
    tm = _tile(t, 512)
    x3 = _mm("mem_out", o_mem, w_mo, (t // tm, N_DEV, 1), ((tm, o_mem.shape[1]), lambda i, j, k: (i, 0)),
             ((None, o_mem.shape[1], mo_cols), lambda i, j, k: (j, 0, 0)), ((tm, mo_cols), lambda i, j, k: (i, j)),
             (t, d), F32, NN, res=x2)

    y, ffn2_saved = _ffn_fwd("ffn2", x3, sm['ffn2_norm'], wt['ffn2_w_gate'], wt['ffn2_w_up'], wt['ffn2_w_down'])
    dy, loss_lanes = _loss("loss", y, target)

    dx3, gs['ffn2_norm'], gw['ffn2_w_gate'], gw['ffn2_w_up'], gw['ffn2_w_down'] = _ffn_bwd(
        "ffn2", dy, ffn2_saved, sm['ffn2_norm'], wt['ffn2_w_gate'], wt['ffn2_w_up'], wt['ffn2_w_down'])

    do_mem = _mm("mem_do", dx3, w_mo, (t // tm, MEM_HEADS, N_DEV), ((tm, mo_cols), lambda i, h, k: (i, k)),
                 ((None, MEM_HEAD_DIM, mo_cols), lambda i, h, k: (k, h, 0)), ((tm, MEM_HEAD_DIM), lambda i, h, k: (i, h)),
                 (t, MEM_HEADS * MEM_HEAD_DIM), BF16, NT)
    tk = _tile(t, 512)
    gw['mem_w_o'] = _mm("mem_dwo", o_mem, dx3, (N_DEV, 1, t // tk), ((tk, o_mem.shape[1]), lambda j, i, k: (k, 0)),
                        ((tk, mo_cols), lambda j, i, k: (k, j)), ((None, o_mem.shape[1], mo_cols), lambda j, i, k: (j, 0, 0)),
                        w_mo.shape, BF16, TN)
    dmq, dmk, dmv = _attn_bwd("mem_dattn", mq, mk, mv, do_mem, MEM_HEAD_DIM ** -0.5, False)
    dmq_raw, gs['mem_q_norm'] = _rms_bwd("mem_q_drms", mq_raw.reshape(MEM_HEADS * t, MEM_HEAD_DIM), sm['mem_q_norm'],
                                         dmq.reshape(MEM_HEADS * t, MEM_HEAD_DIM))
    dmk_raw, gs['mem_k_norm'] = _rms_bwd("mem_k_drms", mk_raw.reshape(MEM_HEADS * nm, MEM_HEAD_DIM), sm['mem_k_norm'],
                                         dmk.reshape(MEM_HEADS * nm, MEM_HEAD_DIM))
    dmq_raw = dmq_raw.reshape(MEM_HEADS, t, MEM_HEAD_DIM)
    dmk_raw = dmk_raw.reshape(MEM_HEADS, nm, MEM_HEAD_DIM)

    def heads_in_nt(name, a, b, res=None):
        m, n = a.shape[1], b.shape[0]
        tm_, tn_ = _tile(m, 512), _tile(n, 1024)
        return _mm(name, a, b, (m // tm_, n // tn_, MEM_HEADS), ((None, tm_, MEM_HEAD_DIM), lambda i, j, k: (k, i, 0)),
                   ((tn_, MEM_HEAD_DIM), lambda i, j, k: (j, k)), ((tm_, tn_), lambda i, j, k: (i, j)), (m, n), F32, NT,
                   None, res)

    def heads_tn(name, a, b):
        m, kp = a.shape
        tm_, tk_ = _tile(kp, 1024), _tile(m, 512)
        return _mm(name, a, b, (kp // tm_, MEM_HEADS, m // tk_), ((tk_, tm_), lambda i, h, k: (k, i)),
                   ((None, tk_, MEM_HEAD_DIM), lambda i, h, k: (h, k, 0)), ((tm_, MEM_HEAD_DIM), lambda i, h, k: (i, h)),
                   (kp, MEM_HEADS * MEM_HEAD_DIM), BF16, TN)

    dhq = heads_in_nt("mem_dhq", dmq_raw, w_mq)
    gw['mem_w_q'] = heads_tn("mem_dwq", hq, dmq_raw).reshape(wt['mem_w_q'].shape)
    dhm = heads_in_nt("mem_dhm_k", dmk_raw, w_mk)
    dhm = heads_in_nt("mem_dhm_v", dmv, w_mv, res=dhm)
    gw['mem_w_k'] = heads_tn("mem_dwk", hm, dmk_raw).reshape(wt['mem_w_k'].shape)
    gw['mem_w_v'] = heads_tn("mem_dwv", hm, dmv).reshape(wt['mem_w_v'].shape)
    _, gs['mem_norm'] = _rms_bwd("mem_drms", mem, sm['mem_norm'], dhm)
    dx2, gs['mem_attn_norm'] = _rms_bwd("mem_attn_drms", x2, sm['mem_attn_norm'], dhq, res=dx3)

    dcat = _mm2("mix_dcat", dx2, w_out, NT, F32)
    gw['w_out'] = _mm2("mix_dwout", cat, dx2, TN, BF16, tm=1024, tn=2048, tk=512).reshape(wt['w_out'].shape)
    do_mla, do_gla = dcat[:, :MLA_HEADS * MLA_V], dcat[:, MLA_HEADS * MLA_V:]

    dzr, dgn = _swiglu_bwd("gla_out_dgate", do_gla, zs['zr'], o_gla_n, F32)
    do_gla_raw, gs['gla_out_norm'] = _rms_bwd("gla_out_drms", o_gla_raw, sm['gla_out_norm'], dgn)
    dgq, dgk, dgv, dlog_a = _gla_bwd("gla_dscan", zs['gq'], zs['gk'], zs['gv'], log_a, states, do_gla_raw, GLA_HEADS)
    dpre, gs['gla_b_gate'] = _gate_bwd("gla_dgate", pre, sm['gla_b_gate'], dlog_a)
    dw_g2 = _mm2("gla_dwgate", zs['zg'], dpre, TN, BF16, tk=512)
    gw['gla_w_gate2'] = _stack_cols(dw_g2[:GLA_GATE_RANK])
    dzg = _mm2("gla_dzg", dpre, w_g2, NT, F32)

    dqf, dkf, dv_mla = _attn_bwd("mla_dattn", qf, kf, v_mla, do_mla, MLA_QK ** -0.5, True)
    dqn = _rope("mla_q_drope", dqf.reshape(MLA_HEADS * t, MLA_QK), cos_t, sin_t, swap, F32, True)
    dkn = _rope("mla_k_drope", dkf.reshape(MLA_HEADS * t, MLA_QK), cos_t, sin_t, swap, F32, True)
    dq_raw, gs['mla_q_norm'] = _rms_bwd("mla_q_drms", q_raw2, sm['mla_q_norm'], dqn)
    dk_raw, gs['mla_k_norm'] = _rms_bwd("mla_k_drms", k_raw2, sm['mla_k_norm'], dkn)
    dq_raw = dq_raw.reshape(MLA_HEADS, t, MLA_QK)
    dk_raw = dk_raw.reshape(MLA_HEADS, t, MLA_QK)
    dkv = jnp.concatenate([dk_raw[:, :, :MLA_NOPE], dv_mla], axis=2)
    dzkr = jnp.sum(dk_raw[:, :, MLA_NOPE:], axis=0)
    gw['w_q_up'] = _mm_stack_tn_left("mla_dwq", cq, dq_raw, BF16)
    gw['w_kv_up'] = _mm_stack_tn_left("mla_dwkv", ckv, dkv, BF16)
    dcq = _mm_stack_nt_sum("mla_dcq", dq_raw, wt['w_q_up'], F32)
    dckv = _mm_stack_nt_sum("mla_dckv", dkv, wt['w_kv_up'], F32)
    dzq, gs['q_a_norm'] = _rms_bwd("mla_q_a_drms", zs['zq'], sm['q_a_norm'], dcq)
    dzkv, gs['kv_a_norm'] = _rms_bwd("mla_kv_a_drms", zs['zkv'], sm['kv_a_norm'], dckv)

    dzs = {'zq': dzq, 'zkv': dzkv, 'gq': dgq, 'gk': dgk, 'gv': dgv, 'zr': dzr,
           'zkr': jnp.pad(dzkr, ((0, 0), (0, LANES - MLA_ROPE))), 'zg': dzg}
    dz = jnp.concatenate([dzs[name].astype(BF16) for name, _ in IN_PAD], axis=1)
    dw_in = _mm2("mix_dwin", h2, dz, TN, BF16, tm=1024, tn=2048, tk=512)
    dw_in_ref = jnp.concatenate([dw_in[:, PAD_OFF[name]:PAD_OFF[name] + n] for name, n in IN_REF], axis=1)
    gw['w_in'] = _stack_cols(dw_in_ref)
    dh2 = _mm2("mix_dh", dz, w_in, NT, F32)
    dx1, gs['mix_norm'] = _rms_bwd("mix_drms", x1, sm['mix_norm'], dh2, res=dx2)

    grad_x, gs['ffn1_norm'], gw['ffn1_w_gate'], gw['ffn1_w_up'], gw['ffn1_w_down'] = _ffn_bwd(
        "ffn1", dx1, ffn1_saved, sm['ffn1_norm'], wt['ffn1_w_gate'], wt['ffn1_w_up'], wt['ffn1_w_down'])
    return loss_lanes, grad_x, gs, gw


def _pad_lanes(v):
    n = v.shape[1]
    return jnp.pad(v, ((0, 0), (0, -n % LANES)))


def _pack_small(vals):
    return jnp.concatenate([_pad_lanes(vals[n]) for n in SMALL], axis=1)


def _unpack_small(packed, like):
    out, off = {}, 0
    for n in SMALL:
        size = like[n].shape[1]
        out[n] = packed[:, off:off + size]
        off += size + (-size % LANES)
    return out


def kernel(x, mem, positions, ffn1_norm, ffn1_w_gate, ffn1_w_up, ffn1_w_down, mix_norm, w_in, q_a_norm, w_q_up, kv_a_norm, w_kv_up, mla_q_norm, mla_k_norm, gla_w_gate2, gla_b_gate, gla_out_norm, w_out, mem_attn_norm, mem_norm, mem_w_q, mem_w_k, mem_w_v, mem_w_o, mem_q_norm, mem_k_norm, ffn2_norm, ffn2_w_gate, ffn2_w_up, ffn2_w_down, loss_target, m_ffn1_norm, m_ffn1_w_gate, m_ffn1_w_up, m_ffn1_w_down, m_mix_norm, m_w_in, m_q_a_norm, m_w_q_up, m_kv_a_norm, m_w_kv_up, m_mla_q_norm, m_mla_k_norm, m_gla_w_gate2, m_gla_b_gate, m_gla_out_norm, m_w_out, m_mem_attn_norm, m_mem_norm, m_mem_w_q, m_mem_w_k, m_mem_w_v, m_mem_w_o, m_mem_q_norm, m_mem_k_norm, m_ffn2_norm, m_ffn2_w_gate, m_ffn2_w_up, m_ffn2_w_down, v_ffn1_norm, v_ffn1_w_gate, v_ffn1_w_up, v_ffn1_w_down, v_mix_norm, v_w_in, v_q_a_norm, v_w_q_up, v_kv_a_norm, v_w_kv_up, v_mla_q_norm, v_mla_k_norm, v_gla_w_gate2, v_gla_b_gate, v_gla_out_norm, v_w_out, v_mem_attn_norm, v_mem_norm, v_mem_w_q, v_mem_w_k, v_mem_w_v, v_mem_w_o, v_mem_q_norm, v_mem_k_norm, v_ffn2_norm, v_ffn2_w_gate, v_ffn2_w_up, v_ffn2_w_down):
    inp = dict(locals())
    x, mem, positions, target = inp['x'][0], inp['mem'][0], inp['positions'][0], inp['loss_target'][0]
    sm = {n: inp[n] for n in SMALL}
    big = [n for g in GROUPS for n in g]

    wt = {}
    for gi, group in enumerate(GROUPS):
        gathered = _all_gather(f"gather_{gi}", [inp[n][0].astype(BF16) for n in group])
        wt.update(zip(group, gathered))

    loss_lanes, grad_x, gs, gw = _local_step(x, mem, positions, sm, wt, target)
    loss = lax.psum(jnp.sum(loss_lanes), ("x", "y", "c"))

    out = {'loss': loss, 'grad_x': grad_x[None]}
    for gi, group in enumerate(GROUPS):
        parts = _exchange(f"exchange_{gi}", [gw[n] for n in group])
        for n, p in zip(group, parts):
            res = _adamw("adamw_" + n, inp[n][0], inp['m_' + n][0], inp['v_' + n][0], p)
            for kind, r in zip(('grad_', 'delta_', 'new_m_', 'new_v_'), res):
                out[kind + n] = r[None]

    small_parts = _all_gather("gather_small", [_pack_small(gs)])[0]
    res = _adamw("adamw_small", _pack_small(sm), _pack_small({n: inp['m_' + n] for n in SMALL}),
                 _pack_small({n: inp['v_' + n] for n in SMALL}), small_parts)
    for kind, r in zip(('grad_', 'delta_', 'new_m_', 'new_v_'), res):
        for n, val in _unpack_small(r, sm).items():
            out[kind + n] = val

    names = ['loss', 'grad_x'] + [k + n for k in ('grad_', 'delta_', 'new_m_', 'new_v_') for n in WEIGHTS]
    return tuple(out[n] for n in names)
```

```python
import math

import jax
import jax.numpy as jnp
from jax import lax
from jax.experimental import pallas as pl
from jax.experimental.pallas import tpu as pltpu

F32 = jnp.float32
BF16 = jnp.bfloat16

N_DEV = 8
EPS = 1e-6
CHUNK = 64
MLA_HEADS, MLA_NOPE, MLA_ROPE, MLA_V = 8, 128, 64, 128
MLA_QK = MLA_NOPE + MLA_ROPE
MLA_Q_RANK, MLA_KV_RANK = 512, 256
ROPE_THETA = 10000.0
GLA_HEADS, GLA_DK, GLA_DV, GLA_GATE_RANK = 4, 128, 256, 16
GLA_TAU = 16.0
MEM_HEADS, MEM_HEAD_DIM = 4, 128
ADAM_LR, ADAM_B1, ADAM_B2, ADAM_EPS, ADAM_WD, ADAM_STEP = 0.001, 0.9, 0.999, 1e-08, 0.01, 10

V7X_VMEM_BYTES = 64 * 1024 * 1024
LANES = 128

NN = (((1,), (0,)), ((), ()))
NT = (((1,), (1,)), ((), ()))
TN = (((0,), (0,)), ((), ()))

WEIGHTS = ['ffn1_norm', 'ffn1_w_gate', 'ffn1_w_up', 'ffn1_w_down', 'mix_norm', 'w_in', 'q_a_norm', 'w_q_up',
           'kv_a_norm', 'w_kv_up', 'mla_q_norm', 'mla_k_norm', 'gla_w_gate2', 'gla_b_gate', 'gla_out_norm', 'w_out',
           'mem_attn_norm', 'mem_norm', 'mem_w_q', 'mem_w_k', 'mem_w_v', 'mem_w_o', 'mem_q_norm', 'mem_k_norm',
           'ffn2_norm', 'ffn2_w_gate', 'ffn2_w_up', 'ffn2_w_down']
SMALL = ['ffn1_norm', 'mix_norm', 'q_a_norm', 'kv_a_norm', 'mla_q_norm', 'mla_k_norm', 'gla_b_gate', 'gla_out_norm',
         'mem_attn_norm', 'mem_norm', 'mem_q_norm', 'mem_k_norm', 'ffn2_norm']
GROUPS = [['ffn1_w_gate', 'ffn1_w_up', 'ffn1_w_down'],
          ['w_in', 'w_q_up', 'w_kv_up', 'gla_w_gate2', 'w_out'],
          ['mem_w_q', 'mem_w_k', 'mem_w_v', 'mem_w_o'],
          ['ffn2_w_gate', 'ffn2_w_up', 'ffn2_w_down']]

IN_REF = [('zq', 512), ('zkv', 256), ('zkr', 64), ('gq', 512), ('gk', 512), ('gv', 1024), ('zg', 16), ('zr', 1024)]
IN_PAD = [('zq', 512), ('zkv', 256), ('gq', 512), ('gk', 512), ('gv', 1024), ('zr', 1024), ('zkr', 128), ('zg', 128)]
IN_WIDTH = sum(n for _, n in IN_REF)
IN_PAD_WIDTH = sum(n for _, n in IN_PAD)


def _offsets(layout):
    out, off = {}, 0
    for name, n in layout:
        out[name] = off
        off += n
    return out


REF_OFF, PAD_OFF = _offsets(IN_REF), _offsets(IN_PAD)
REF_SIZE = dict(IN_REF)


def _tile(n, pref):
    return pref if n % pref == 0 else n


def _block_bytes(blk, dtype):
    dims = [d for d in blk if d is not None]
    if len(dims) >= 1:
        dims[-1] = -(-dims[-1] // LANES) * LANES
    return math.prod(dims) * jnp.dtype(dtype).itemsize


def _vmem_limit(pipelined_bytes, resident_bytes=0):
    need = 2 * pipelined_bytes + resident_bytes + (8 << 20)
    return int(min(max(need, 32 << 20), V7X_VMEM_BYTES - (6 << 20)))


def _rowwise_params(*semantics):
    return pltpu.CompilerParams(dimension_semantics=semantics, vmem_limit_bytes=48 << 20)


def _mm(name, a, b, grid, a_spec, b_spec, o_spec, out_shape, out_dtype, dims, scale=None, res=None):
    nk = grid[2]
    (a_blk, a_map), (b_blk, b_map), (o_blk, o_map) = a_spec, b_spec, o_spec
    acc_shape = tuple(d for d in o_blk if d is not None)

    def body(*refs):
        if res is None:
            a_ref, b_ref, o_ref = refs[:3]
            r_ref, rest = None, refs[3:]
        else:
            a_ref, b_ref, r_ref, o_ref = refs[:4]
            rest = refs[4:]

        def product():
            return lax.dot_general(a_ref[...].astype(BF16), b_ref[...].astype(BF16), dims, preferred_element_type=F32)

        def finish(r):
            if scale is not None:
                r = r * scale
            if r_ref is not None:
                r = r + r_ref[...].astype(F32)
            o_ref[...] = r.astype(o_ref.dtype)

        if nk == 1:
            finish(product())
        else:
            acc = rest[0]
            k = pl.program_id(2)

            @pl.when(k == 0)
            def _():
                acc[...] = product()

            @pl.when(k > 0)
            def _():
                acc[...] += product()

            @pl.when(k == nk - 1)
            def _():
                finish(acc[...])

    in_specs = [pl.BlockSpec(a_blk, a_map), pl.BlockSpec(b_blk, b_map)]
    operands = [a, b]
    piped = _block_bytes(a_blk, a.dtype) + _block_bytes(b_blk, b.dtype) + _block_bytes(o_blk, out_dtype)
    if res is not None:
        in_specs.append(pl.BlockSpec(o_blk, o_map))
        operands.append(res)
        piped += _block_bytes(o_blk, res.dtype)
    scratch = [pltpu.VMEM(acc_shape, F32)] if nk > 1 else []
    return pl.pallas_call(
        body, name=name, grid=grid, in_specs=in_specs, out_specs=pl.BlockSpec(o_blk, o_map),
        out_shape=jax.ShapeDtypeStruct(out_shape, out_dtype), scratch_shapes=scratch,
        compiler_params=pltpu.CompilerParams(
            dimension_semantics=("parallel", "parallel", "arbitrary"),
            vmem_limit_bytes=_vmem_limit(piped, 3 * _block_bytes(acc_shape, F32))),
    )(*operands)


def _mm2(name, a, b, dims, out_dtype, tm=512, tn=1024, tk=2048, scale=None, res=None):
    if dims is NN:
        (m, kk), n = a.shape, b.shape[1]
    elif dims is NT:
        (m, kk), n = a.shape, b.shape[0]
    else:
        (kk, m), n = a.shape, b.shape[1]
    tm, tn, tk = _tile(m, tm), _tile(n, tn), _tile(kk, tk)
    a_spec = ((tk, tm), lambda i, j, k: (k, i)) if dims is TN else ((tm, tk), lambda i, j, k: (i, k))
    b_spec = ((tn, tk), lambda i, j, k: (j, k)) if dims is NT else ((tk, tn), lambda i, j, k: (k, j))
    return _mm(name, a, b, (m // tm, n // tn, kk // tk), a_spec, b_spec, ((tm, tn), lambda i, j, k: (i, j)),
               (m, n), out_dtype, dims, scale, res)


def _mm_stack_out(name, a, b, out_dtype, tm=512, tk=2048):
    (m, kk), (nj, _, n) = a.shape, b.shape
    tm, tk = _tile(m, tm), _tile(kk, tk)
    return _mm(name, a, b, (nj, m // tm, kk // tk), ((tm, tk), lambda j, i, k: (i, k)),
               ((None, tk, n), lambda j, i, k: (j, k, 0)), ((None, tm, n), lambda j, i, k: (j, i, 0)),
               (nj, m, n), out_dtype, NN)


def _mm_stack_nt_out(name, a, b, out_dtype, scale=None, tm=512, tk=2048):
    (m, kk), (nj, n, _) = a.shape, b.shape
    tm, tk = _tile(m, tm), _tile(kk, tk)
    return _mm(name, a, b, (nj, m // tm, kk // tk), ((tm, tk), lambda j, i, k: (i, k)),
               ((None, n, tk), lambda j, i, k: (j, 0, k)), ((None, tm, n), lambda j, i, k: (j, i, 0)),
               (nj, m, n), out_dtype, NT, scale)


def _mm_stack_sum(name, a, b, out_dtype, scale=None, res=None, tm=512, tn=1024):
    (nj, m, f), n = a.shape, b.shape[2]
    tm, tn = _tile(m, tm), _tile(n, tn)
    return _mm(name, a, b, (m // tm, n // tn, nj), ((None, tm, f), lambda i, j, k: (k, i, 0)),
               ((None, f, tn), lambda i, j, k: (k, 0, j)), ((tm, tn), lambda i, j, k: (i, j)),
               (m, n), out_dtype, NN, scale, res)


def _mm_stack_nt_sum(name, a, b, out_dtype, res=None, tm=512, tn=1024):
    (nj, m, f), n = a.shape, b.shape[1]
    tm, tn = _tile(m, tm), _tile(n, tn)
    return _mm(name, a, b, (m // tm, n // tn, nj), ((None, tm, f), lambda i, j, k: (k, i, 0)),
               ((None, tn, f), lambda i, j, k: (k, j, 0)), ((tm, tn), lambda i, j, k: (i, j)),
               (m, n), out_dtype, NT, None, res)


def _mm_stack_tn_left(name, a, b, out_dtype, tm=1024, tk=512):
    (m, kp), (nj, _, n) = a.shape, b.shape
    tm, tk = _tile(kp, tm), _tile(m, tk)
    return _mm(name, a, b, (nj, kp // tm, m // tk), ((tk, tm), lambda j, i, k: (k, i)),
               ((None, tk, n), lambda j, i, k: (j, k, 0)), ((None, tm, n), lambda j, i, k: (j, i, 0)),
               (nj, kp, n), out_dtype, TN)


def _mm_stack_tn_right(name, a, b, out_dtype, scale=None, tn=2048, tk=512):
    (nj, m, f), n = a.shape, b.shape[1]
    tn, tk = _tile(n, tn), _tile(m, tk)
    return _mm(name, a, b, (nj, n // tn, m // tk), ((None, tk, f), lambda j, i, k: (j, k, 0)),
               ((tk, tn), lambda j, i, k: (k, i)), ((None, f, tn), lambda j, i, k: (j, 0, i)),
               (nj, f, n), out_dtype, TN, scale)


def _rms_fwd(name, x, g, out_dtype, tm=256):
    rows, cols = x.shape
    d = g.shape[1]
    tm = _tile(rows, tm)

    def body(x_ref, g_ref, o_ref):
        xf = x_ref[...].astype(F32)
        r = lax.rsqrt(jnp.mean(xf * xf, axis=-1, keepdims=True) + EPS)
        o_ref[...] = (xf * r * g_ref[...]).astype(o_ref.dtype)

    return pl.pallas_call(
        body, name=name, grid=(rows // tm, cols // d),
        in_specs=[pl.BlockSpec((tm, d), lambda i, c: (i, c)), pl.BlockSpec((1, d), lambda i, c: (0, 0))],
        out_specs=pl.BlockSpec((tm, d), lambda i, c: (i, c)),
        out_shape=jax.ShapeDtypeStruct((rows, cols), out_dtype),
        compiler_params=_rowwise_params("parallel", "parallel"),
    )(x, g)


def _rms_bwd(name, x, g, dy, res=None, tm=256):
    rows, cols = x.shape
    d = g.shape[1]
    tm = _tile(rows, tm)

    def body(*refs):
        if res is None:
            x_ref, g_ref, dy_ref, dx_ref, dg_ref = refs
            r_ref = None
        else:
            x_ref, g_ref, dy_ref, r_ref, dx_ref, dg_ref = refs
        xf = x_ref[...].astype(F32)
        r = lax.rsqrt(jnp.mean(xf * xf, axis=-1, keepdims=True) + EPS)
        xhat = xf * r
        dyf = dy_ref[...].astype(F32)

        @pl.when((pl.program_id(0) == 0) & (pl.program_id(1) == 0))
        def _():
            dg_ref[...] = jnp.zeros_like(dg_ref)

        dg_ref[...] += jnp.sum(dyf * xhat, axis=0, keepdims=True)
        dxh = dyf * g_ref[...]
        dx = r * (dxh - xhat * jnp.mean(dxh * xhat, axis=-1, keepdims=True))
        if r_ref is not None:
            dx = dx + r_ref[...].astype(F32)
        dx_ref[...] = dx

    blk = pl.BlockSpec((tm, d), lambda i, c: (i, c))
    in_specs = [blk, pl.BlockSpec((1, d), lambda i, c: (0, 0)), blk]
    operands = [x, g, dy]
    if res is not None:
        in_specs.append(blk)
        operands.append(res)
    return pl.pallas_call(
        body, name=name, grid=(rows // tm, cols // d), in_specs=in_specs,
        out_specs=(blk, pl.BlockSpec((1, d), lambda i, c: (0, 0))),
        out_shape=(jax.ShapeDtypeStruct((rows, cols), F32), jax.ShapeDtypeStruct((1, d), F32)),
        compiler_params=_rowwise_params("arbitrary", "arbitrary"),
    )(*operands)


def _swiglu_fwd(name, g, u, out_dtype, tm=256):
    rows, cols = g.shape
    tm = _tile(rows, tm)

    def body(g_ref, u_ref, o_ref):
        gf = g_ref[...].astype(F32)
        o_ref[...] = (gf * jax.nn.sigmoid(gf) * u_ref[...].astype(F32)).astype(o_ref.dtype)

    blk = pl.BlockSpec((tm, cols), lambda i: (i, 0))
    return pl.pallas_call(
        body, name=name, grid=(rows // tm,), in_specs=[blk, blk], out_specs=blk,
        out_shape=jax.ShapeDtypeStruct((rows, cols), out_dtype),
        compiler_params=_rowwise_params("parallel"),
    )(g, u)


def _swiglu_bwd(name, da, g, u, out_dtype, tm=256):
    rows, cols = g.shape
    tm = _tile(rows, tm)

    def body(da_ref, g_ref, u_ref, dg_ref, du_ref):
        gf = g_ref[...].astype(F32)
        daf = da_ref[...].astype(F32)
        sig = jax.nn.sigmoid(gf)
        du_ref[...] = (daf * gf * sig).astype(du_ref.dtype)
        dg_ref[...] = (daf * u_ref[...].astype(F32) * sig * (1.0 + gf * (1.0 - sig))).astype(dg_ref.dtype)

    blk = pl.BlockSpec((tm, cols), lambda i: (i, 0))
    sds = jax.ShapeDtypeStruct((rows, cols), out_dtype)
    return pl.pallas_call(
        body, name=name, grid=(rows // tm,), in_specs=[blk, blk, blk], out_specs=(blk, blk), out_shape=(sds, sds),
        compiler_params=_rowwise_params("parallel"),
    )(da, g, u)


def _split3(x):
    hi = x.astype(BF16)
    r1 = x - hi.astype(F32)
    mid = r1.astype(BF16)
    lo = (r1 - mid.astype(F32)).astype(BF16)
    return hi, mid, lo


def _rope(name, x, cos_t, sin_t, swap, out_dtype, backward, tm=256):
    rows, d = x.shape
    t = cos_t.shape[0]
    tm = _tile(t, tm)
    nt = t // tm

    def body(x_ref, c_ref, s_ref, p_ref, o_ref):
        xf = x_ref[...].astype(F32)
        p = p_ref[...]
        lhs = xf * s_ref[...] if backward else xf
        sw = sum(jnp.dot(part, p, preferred_element_type=F32) for part in _split3(lhs))
        if not backward:
            sw = sw * s_ref[...]
        o_ref[...] = (xf * c_ref[...] + sw).astype(o_ref.dtype)

    blk = pl.BlockSpec((tm, d), lambda i: (i, 0))
    tab = pl.BlockSpec((tm, d), lambda i: (i % nt, 0))
    return pl.pallas_call(
        body, name=name, grid=(rows // tm,), in_specs=[blk, tab, tab, pl.BlockSpec((d, d), lambda i: (0, 0))],
        out_specs=blk, out_shape=jax.ShapeDtypeStruct((rows, d), out_dtype),
        compiler_params=_rowwise_params("parallel"),
    )(x, cos_t, sin_t, swap)


def _gate_fwd(name, pre, bias, tm=256):
    rows, cols = pre.shape
    tm = _tile(rows, tm)

    def body(p_ref, b_ref, o_ref):
        z = p_ref[...] + b_ref[...]
        o_ref[...] = (jnp.minimum(z, 0.0) - jnp.log(1.0 + jnp.exp(-jnp.abs(z)))) * (1.0 / GLA_TAU)

    blk = pl.BlockSpec((tm, cols), lambda i: (i, 0))
    return pl.pallas_call(
        body, name=name, grid=(rows // tm,), in_specs=[blk, pl.BlockSpec((1, cols), lambda i: (0, 0))], out_specs=blk,
        out_shape=jax.ShapeDtypeStruct((rows, cols), F32),
        compiler_params=_rowwise_params("parallel"),
    )(pre, bias)


def _gate_bwd(name, pre, bias, dla, tm=256):
    rows, cols = pre.shape
    tm = _tile(rows, tm)

    def body(p_ref, b_ref, d_ref, dp_ref, db_ref):
        z = p_ref[...] + b_ref[...]
        dp = d_ref[...] * (1.0 / GLA_TAU) / (1.0 + jnp.exp(z))
        dp_ref[...] = dp

        @pl.when(pl.program_id(0) == 0)
        def _():
            db_ref[...] = jnp.zeros_like(db_ref)

        db_ref[...] += jnp.sum(dp, axis=0, keepdims=True)

    blk = pl.BlockSpec((tm, cols), lambda i: (i, 0))
    row = pl.BlockSpec((1, cols), lambda i: (0, 0))
    return pl.pallas_call(
        body, name=name, grid=(rows // tm,), in_specs=[blk, row, blk], out_specs=(blk, row),
        out_shape=(jax.ShapeDtypeStruct((rows, cols), F32), jax.ShapeDtypeStruct((1, cols), F32)),
        compiler_params=_rowwise_params("arbitrary"),
    )(pre, bias, dla)


def _loss(name, y, target, tm=256):
    rows, d = y.shape
    tm = _tile(rows, tm)

    def body(y_ref, t_ref, dy_ref, l_ref):
        err = y_ref[...] - t_ref[...]
        dy_ref[...] = err * (1.0 / d)

        @pl.when(pl.program_id(0) == 0)
        def _():
            l_ref[...] = jnp.zeros_like(l_ref)

        sq = (err * err).reshape(tm // 8, 8, d)
        l_ref[...] += jnp.sum(sq, axis=0) * (0.5 / d)

    blk = pl.BlockSpec((tm, d), lambda i: (i, 0))
    return pl.pallas_call(
        body, name=name, grid=(rows // tm,), in_specs=[blk, blk],
        out_specs=(blk, pl.BlockSpec((8, d), lambda i: (0, 0))),
        out_shape=(jax.ShapeDtypeStruct((rows, d), F32), jax.ShapeDtypeStruct((8, d), F32)),
        compiler_params=_rowwise_params("arbitrary"),
    )(y, target)


def _scores(q, k, scale, causal, q0):
    s = lax.dot_general(q, k, NT, preferred_element_type=F32) * scale
    if causal:
        qc = (q0 + lax.broadcasted_iota(jnp.int32, s.shape, 0)) // CHUNK
        kc = lax.broadcasted_iota(jnp.int32, s.shape, 1) // CHUNK
        s = jnp.where(kc <= qc, s, -1e30)
    e = jnp.exp(s - jnp.max(s, axis=-1, keepdims=True))
    return e, jnp.sum(e, axis=-1, keepdims=True)


def _attn_fwd(name, q, k, v, scale, causal, tq=256):
    nh, t, dk = q.shape
    tk, dv = k.shape[1], v.shape[2]
    tq = _tile(t, tq)

    def body(q_ref, k_ref, v_ref, o_ref):
        e, l = _scores(q_ref[...], k_ref[...], scale, causal, pl.program_id(1) * tq)
        o = jnp.dot(e.astype(BF16), v_ref[...], preferred_element_type=F32)
        o_ref[...] = (o / l).astype(o_ref.dtype)

    return pl.pallas_call(
        body, name=name, grid=(nh, t // tq),
        in_specs=[pl.BlockSpec((None, tq, dk), lambda h, i: (h, i, 0)), pl.BlockSpec((None, tk, dk), lambda h, i: (h, 0, 0)),
                  pl.BlockSpec((None, tk, dv), lambda h, i: (h, 0, 0))],
        out_specs=pl.BlockSpec((tq, dv), lambda h, i: (i, h)),
        out_shape=jax.ShapeDtypeStruct((t, nh * dv), BF16),
        compiler_params=pltpu.CompilerParams(dimension_semantics=("parallel", "parallel"),
                                             vmem_limit_bytes=_vmem_limit(0, 6 * tq * tk * 4)),
    )(q, k, v)


def _attn_bwd(name, q, k, v, do, scale, causal, tq=256):
    nh, t, dk = q.shape
    tk, dv = k.shape[1], v.shape[2]
    tq = _tile(t, tq)

    def body(q_ref, k_ref, v_ref, do_ref, dq_ref, dk_ref, dv_ref):
        qb, kb = q_ref[...], k_ref[...]
        e, l = _scores(qb, kb, scale, causal, pl.program_id(1) * tq)
        p = e / l
        dob = do_ref[...].astype(BF16)
        dp = lax.dot_general(dob, v_ref[...], NT, preferred_element_type=F32)
        ds = (p * (dp - jnp.sum(p * dp, axis=-1, keepdims=True)) * scale).astype(BF16)
        dq_ref[...] = jnp.dot(ds, kb, preferred_element_type=F32)

        @pl.when(pl.program_id(1) == 0)
        def _():
            dk_ref[...] = jnp.zeros_like(dk_ref)
            dv_ref[...] = jnp.zeros_like(dv_ref)

        dk_ref[...] += lax.dot_general(ds, qb, TN, preferred_element_type=F32)
        dv_ref[...] += lax.dot_general(p.astype(BF16), dob, TN, preferred_element_type=F32)

    return pl.pallas_call(
        body, name=name, grid=(nh, t // tq),
        in_specs=[pl.BlockSpec((None, tq, dk), lambda h, i: (h, i, 0)), pl.BlockSpec((None, tk, dk), lambda h, i: (h, 0, 0)),
                  pl.BlockSpec((None, tk, dv), lambda h, i: (h, 0, 0)), pl.BlockSpec((tq, dv), lambda h, i: (i, h))],
        out_specs=(pl.BlockSpec((None, tq, dk), lambda h, i: (h, i, 0)), pl.BlockSpec((None, tk, dk), lambda h, i: (h, 0, 0)),
                   pl.BlockSpec((None, tk, dv), lambda h, i: (h, 0, 0))),
        out_shape=(jax.ShapeDtypeStruct((nh, t, dk), F32), jax.ShapeDtypeStruct((nh, tk, dk), F32),
                   jax.ShapeDtypeStruct((nh, tk, dv), F32)),
        compiler_params=pltpu.CompilerParams(dimension_semantics=("parallel", "arbitrary"),
                                             vmem_limit_bytes=_vmem_limit(0, 10 * tq * tk * 4)),
    )(q, k, v, do)


def _tri(lower):
    r = lax.broadcasted_iota(jnp.int32, (CHUNK, CHUNK), 0)
    c = lax.broadcasted_iota(jnp.int32, (CHUNK, CHUNK), 1)
    return jnp.where((c <= r) if lower else (c >= r), 1.0, 0.0).astype(BF16)


def _tri_dot(tri, x):
    return sum(jnp.dot(tri, part, preferred_element_type=F32) for part in _split3(x))


def _gla_fwd(name, q, k, v, la, nh):
    t = q.shape[0]
    dk, dv = q.shape[1] // nh, v.shape[1] // nh
    nc = t // CHUNK

    def body(q_ref, k_ref, v_ref, g_ref, o_ref, st_ref, state):
        @pl.when(pl.program_id(1) == 0)
        def _():
            state[...] = jnp.zeros_like(state)

        g = g_ref[...]
        b = _tri_dot(_tri(True), g)
        b_end = jnp.sum(g, axis=0, keepdims=True)
        k_dec = (k_ref[...] * jnp.exp(b_end - b)).astype(BF16)
        u_t = lax.dot_general(v_ref[...].astype(BF16), k_dec, TN, preferred_element_type=F32)
        new = state[...] * jnp.exp(b_end) + u_t
        state[...] = new
        st_ref[...] = new
        qc = (q_ref[...] * (dk ** -0.5)).astype(BF16)
        o_ref[...] = lax.dot_general(qc, new.astype(BF16), NT, preferred_element_type=F32)

    kblk = pl.BlockSpec((CHUNK, dk), lambda h, n: (n, h))
    vblk = pl.BlockSpec((CHUNK, dv), lambda h, n: (n, h))
    return pl.pallas_call(
        body, name=name, grid=(nh, nc), in_specs=[kblk, kblk, vblk, kblk],
        out_specs=(vblk, pl.BlockSpec((None, None, dv, dk), lambda h, n: (h, n, 0, 0))),
        out_shape=(jax.ShapeDtypeStruct((t, nh * dv), F32), jax.ShapeDtypeStruct((nh, nc, dv, dk), F32)),
        scratch_shapes=[pltpu.VMEM((dv, dk), F32)],
        compiler_params=_rowwise_params("parallel", "arbitrary"),
    )(q, k, v, la)


def _gla_bwd(name, q, k, v, la, states, do, nh):
    t = q.shape[0]
    dk, dv = q.shape[1] // nh, v.shape[1] // nh
    nc = t // CHUNK
    scale = dk ** -0.5

    def body(q_ref, k_ref, v_ref, g_ref, do_ref, st_ref, sp_ref, dq_ref, dk_ref, dv_ref, dg_ref, carry):
        i = pl.program_id(1)

        @pl.when(i == 0)
        def _():
            carry[...] = jnp.zeros_like(carry)

        g = g_ref[...]
        b = _tri_dot(_tri(True), g)
        b_end = jnp.sum(g, axis=0, keepdims=True)
        w = jnp.exp(b_end - b)
        decay = jnp.exp(b_end)
        k_dec = k_ref[...] * w
        qc = (q_ref[...] * scale).astype(BF16)
        dob = do_ref[...].astype(BF16)
        dq_ref[...] = jnp.dot(dob, st_ref[...].astype(BF16), preferred_element_type=F32) * scale
        g_t = carry[...] + lax.dot_general(dob, qc, TN, preferred_element_type=F32)
        g_tb = g_t.astype(BF16)
        dk_dec = jnp.dot(v_ref[...].astype(BF16), g_tb, preferred_element_type=F32)
        dv_ref[...] = lax.dot_general(k_dec.astype(BF16), g_tb, NT, preferred_element_type=F32)
        prev = jnp.where(i < nc - 1, sp_ref[...], 0.0)
        ddecay = jnp.sum(g_t * prev, axis=0, keepdims=True)
        dk_ref[...] = dk_dec * w
        e = dk_dec * k_dec
        db_end = jnp.sum(e, axis=0, keepdims=True) + ddecay * decay
        dg_ref[...] = _tri_dot(_tri(False), -e) + db_end
        carry[...] = g_t * decay

    kblk = pl.BlockSpec((CHUNK, dk), lambda h, i: (nc - 1 - i, h))
    vblk = pl.BlockSpec((CHUNK, dv), lambda h, i: (nc - 1 - i, h))
    ksds = jax.ShapeDtypeStruct((t, nh * dk), F32)
    return pl.pallas_call(
        body, name=name, grid=(nh, nc),
        in_specs=[kblk, kblk, vblk, kblk, vblk,
                  pl.BlockSpec((None, None, dv, dk), lambda h, i: (h, nc - 1 - i, 0, 0)),
                  pl.BlockSpec((None, None, dv, dk), lambda h, i: (h, jnp.maximum(nc - 2 - i, 0), 0, 0))],
        out_specs=(kblk, kblk, vblk, kblk),
        out_shape=(ksds, ksds, jax.ShapeDtypeStruct((t, nh * dv), F32), ksds),
        scratch_shapes=[pltpu.VMEM((dv, dk), F32)],
        compiler_params=_rowwise_params("parallel", "arbitrary"),
    )(q, k, v, la, do, states, states)


def _adamw(name, w, m, v, parts):
    rows, cols = w.shape
    tm = _tile(rows, 1 << int(math.log2(max(8, (1 << 17) // (-(-cols // LANES) * LANES)))))

    def body(w_ref, m_ref, v_ref, p_ref, g_ref, d_ref, nm_ref, nv_ref):
        g = p_ref[0].astype(F32)
        for s in range(1, N_DEV):
            g = g + p_ref[s].astype(F32)
        m_new = ADAM_B1 * m_ref[...] + (1.0 - ADAM_B1) * g
        v_new = ADAM_B2 * v_ref[...] + (1.0 - ADAM_B2) * jnp.square(g)
        m_hat = m_new / (1.0 - ADAM_B1 ** ADAM_STEP)
        v_hat = v_new / (1.0 - ADAM_B2 ** ADAM_STEP)
        g_ref[...] = g
        d_ref[...] = -ADAM_LR * (m_hat / (jnp.sqrt(v_hat) + ADAM_EPS) + ADAM_WD * w_ref[...])
        nm_ref[...] = m_new
        nv_ref[...] = v_new

    blk = pl.BlockSpec((tm, cols), lambda i: (i, 0))
    sds = jax.ShapeDtypeStruct((rows, cols), F32)
    return pl.pallas_call(
        body, name=name, grid=(rows // tm,),
        in_specs=[blk, blk, blk, pl.BlockSpec((N_DEV, tm, cols), lambda i: (0, i, 0))],
        out_specs=(blk, blk, blk, blk), out_shape=(sds, sds, sds, sds),
        compiler_params=_rowwise_params("parallel"),
    )(w, m, v, parts)


HBM = pl.BlockSpec(memory_space=pl.ANY)
MESH = pl.DeviceIdType.MESH


def _all_gather(name, shards):
    n = len(shards)

    def body(*refs):
        ins, outs = refs[:n], refs[n:2 * n]
        send_sems, recv_sems, local_sems = refs[2 * n:]
        x, y, c = lax.axis_index("x"), lax.axis_index("y"), lax.axis_index("c")
        me, sibling = (x, y, c), (x, y, 1 - c)
        chips = [(1 - x, y), (x, 1 - y), (1 - x, 1 - y)]

        def copy(w, k, block, to, src=None):
            dst = outs[w].at[4 * block[0] + 2 * block[1] + block[2]]
            return pltpu.make_async_remote_copy(
                src_ref=dst if src is None else src, dst_ref=dst, send_sem=send_sems.at[7 * w + k],
                recv_sem=recv_sems.at[7 * w + k], device_id=to, device_id_type=MESH)

        mine, first, passed = [], [], []
        for w in range(n):
            cp = pltpu.make_async_copy(ins[w], outs[w].at[4 * x + 2 * y + c], local_sems.at[w])
            cp.start()
            mine.append(cp)
            first.append(copy(w, 0, me, sibling, src=ins[w]))
            first += [copy(w, 1 + j, me, (*chip, c), src=ins[w]) for j, chip in enumerate(chips)]
        for cp in first:
            cp.start()
        for w in range(n):
            for j, chip in enumerate(chips):
                copy(w, 1 + j, (*chip, c), me).wait_recv()
                cp = copy(w, 4 + j, (*chip, c), sibling)
                cp.start()
                passed.append(cp)
        for w in range(n):
            copy(w, 0, sibling, me).wait_recv()
            for j, chip in enumerate(chips):
                copy(w, 4 + j, (*chip, 1 - c), me).wait_recv()
        for cp in first + passed:
            cp.wait_send()
        for cp in mine:
            cp.wait()

    return pl.pallas_call(
        body, name=name, in_specs=[HBM] * n, out_specs=[HBM] * n,
        out_shape=[jax.ShapeDtypeStruct((N_DEV,) + s.shape, s.dtype) for s in shards],
        scratch_shapes=[pltpu.SemaphoreType.DMA((7 * n,)), pltpu.SemaphoreType.DMA((7 * n,)),
                        pltpu.SemaphoreType.DMA((n,))],
    )(*shards)


HBM_SPEC = pl.BlockSpec(memory_space=pltpu.HBM)
SEM_SPEC = pl.BlockSpec(memory_space=pltpu.SEMAPHORE)
EFFECT = pltpu.SideEffectType.DATAFLOW_SIDE_EFFECTING
TOKEN = jax.ShapeDtypeStruct((8, LANES), F32)


def _in_hbm(a):
    return pltpu.with_memory_space_constraint(a, pltpu.HBM)


def _place():
    x, y, c = lax.axis_index("x"), lax.axis_index("y"), lax.axis_index("c")
    chips = [(1 - x, y), (x, 1 - y), (1 - x, 1 - y)]
    return x, y, c, chips


def _block_of(px, py, pc):
    return 4 * px + 2 * py + pc


def _gather_copies(ins, outs, sems, w):
    send_sems, recv_sems, local_sems = sems
    x, y, c, chips = _place()
    peers = [(x, y, 1 - c)] + [(*chip, c) for chip in chips]

    def copy(k, block, to, src):
        dst = outs[w].at[_block_of(*block)]
        return pltpu.make_async_remote_copy(src_ref=dst if src is None else src, dst_ref=dst, send_sem=send_sems.at[4 * w + k],
                                            recv_sem=recv_sems.at[4 * w + k], device_id=to, device_id_type=MESH)

    local = pltpu.make_async_copy(ins[w], outs[w].at[_block_of(x, y, c)], local_sems.at[w])
    sends = [copy(k, (x, y, c), peer, ins[w]) for k, peer in enumerate(peers)]
    recvs = [copy(k, peer, (x, y, c), None) for k, peer in enumerate(peers)]
    return local, sends, recvs


def _forward_copies(outs, sems, w):
    send_sems, recv_sems = sems
    x, y, c, chips = _place()

    def copy(j, block, to):
        dst = outs[w].at[_block_of(*block)]
        return pltpu.make_async_remote_copy(src_ref=dst, dst_ref=dst, send_sem=send_sems.at[3 * w + j],
                                            recv_sem=recv_sems.at[3 * w + j], device_id=to, device_id_type=MESH)

    sends = [copy(j, (*chip, c), (x, y, 1 - c)) for j, chip in enumerate(chips)]
    recvs = [copy(j, (*chip, 1 - c), (x, y, c)) for j, chip in enumerate(chips)]
    return sends, recvs


def _gather_start(name, shards):
    n = len(shards)

    def body(*refs):
        ins, outs, sems = refs[:n], refs[n:2 * n], refs[2 * n:2 * n + 3]
        for w in range(n):
            local, sends, _ = _gather_copies(ins, outs, sems, w)
            local.start()
            for cp in sends:
                cp.start()
        refs[-1][...] = jnp.zeros_like(refs[-1])

    lands = [lax.empty((N_DEV,) + s.shape, s.dtype) for s in shards]
    res = pl.pallas_call(
        body, name=name, in_specs=[HBM_SPEC] * (2 * n),
        out_specs=[SEM_SPEC] * 3 + [HBM_SPEC] * (2 * n) + [pl.BlockSpec(memory_space=pltpu.VMEM)],
        out_shape=[pltpu.SemaphoreType.DMA((4 * n,)), pltpu.SemaphoreType.DMA((4 * n,)), pltpu.SemaphoreType.DMA((n,))]
        + [pltpu.HBM(s.shape, s.dtype) for s in shards] + [pltpu.HBM(l.shape, l.dtype) for l in lands] + [TOKEN],
        input_output_aliases={i: 3 + i for i in range(2 * n)},
        compiler_params=pltpu.CompilerParams(has_side_effects=EFFECT),
    )(*[_in_hbm(s) for s in shards], *[_in_hbm(l) for l in lands])
    return (n, res[:3], res[3:3 + n], res[3 + n:3 + 2 * n]), res[-1]


def _gather_mid(name, state, after):
    n, sems, shards, lands = state

    def body(*refs):
        ins, outs, sems_in = refs[:n], refs[n:2 * n], refs[2 * n:2 * n + 3]
        sems_out = refs[2 * n + 4:2 * n + 6]
        for w in range(n):
            local, sends, recvs = _gather_copies(ins, outs, sems_in, w)
            local.wait()
            for cp in sends:
                cp.wait_send()
            for cp in recvs:
                cp.wait_recv()
            for cp in _forward_copies(outs, sems_out, w)[0]:
                cp.start()

    res = pl.pallas_call(
        body, name=name, in_specs=[HBM_SPEC] * (2 * n) + [SEM_SPEC] * 3 + [HBM],
        out_specs=[SEM_SPEC] * 2 + [HBM_SPEC] * n,
        out_shape=[pltpu.SemaphoreType.DMA((3 * n,)), pltpu.SemaphoreType.DMA((3 * n,))]
        + [pltpu.HBM(l.shape, l.dtype) for l in lands],
        input_output_aliases={n + i: 2 + i for i in range(n)},
        compiler_params=pltpu.CompilerParams(has_side_effects=EFFECT),
    )(*shards, *lands, *sems, after)
    return n, res[:2], res[2:]


def _gather_end(name, state, after):
    n, sems, lands = state

    def body(*refs):
        outs, sems_in = refs[:n], refs[n:n + 2]
        for w in range(n):
            sends, recvs = _forward_copies(outs, sems_in, w)
            for cp in sends:
                cp.wait_send()
            for cp in recvs:
                cp.wait_recv()

    return pl.pallas_call(
        body, name=name, in_specs=[HBM_SPEC] * n + [SEM_SPEC] * 2 + [HBM], out_specs=[HBM_SPEC] * n,
        out_shape=[pltpu.HBM(l.shape, l.dtype) for l in lands], input_output_aliases={i: i for i in range(n)},
        compiler_params=pltpu.CompilerParams(has_side_effects=EFFECT),
    )(*lands, *sems, after)


def _exchange_copies(ins, outs, sems, w):
    send_sems, recv_sems, local_sems = sems
    x, y, c, _ = _place()
    mine = _block_of(x, y, c)
    local = pltpu.make_async_copy(ins[w].at[mine], outs[w].at[mine], local_sems.at[w])
    remote = []
    for k in range(1, N_DEV):
        px, py, pc = x ^ (k >> 2), y ^ ((k >> 1) & 1), c ^ (k & 1)
        remote.append(pltpu.make_async_remote_copy(
            src_ref=ins[w].at[_block_of(px, py, pc)], dst_ref=outs[w].at[mine], send_sem=send_sems.at[7 * w + k - 1],
            recv_sem=recv_sems.at[7 * w + k - 1], device_id=(px, py, pc), device_id_type=MESH))
    return local, remote


def _exchange_start(name, stacks):
    n = len(stacks)

    def body(*refs):
        ins, outs, sems = refs[:n], refs[n:2 * n], refs[2 * n:2 * n + 3]
        for w in range(n):
            local, remote = _exchange_copies(ins, outs, sems, w)
            local.start()
            for cp in remote:
                cp.start()
        refs[-1][...] = jnp.zeros_like(refs[-1])

    lands = [lax.empty(s.shape, s.dtype) for s in stacks]
    res = pl.pallas_call(
        body, name=name, in_specs=[HBM_SPEC] * (2 * n),
        out_specs=[SEM_SPEC] * 3 + [HBM_SPEC] * (2 * n) + [pl.BlockSpec(memory_space=pltpu.VMEM)],
        out_shape=[pltpu.SemaphoreType.DMA((7 * n,)), pltpu.SemaphoreType.DMA((7 * n,)), pltpu.SemaphoreType.DMA((n,))]
        + [pltpu.HBM(s.shape, s.dtype) for s in stacks] * 2 + [TOKEN],
        input_output_aliases={i: 3 + i for i in range(2 * n)},
        compiler_params=pltpu.CompilerParams(has_side_effects=EFFECT),
    )(*[_in_hbm(s) for s in stacks], *[_in_hbm(l) for l in lands])
    return (n, res[:3], res[3:3 + n], res[3 + n:3 + 2 * n]), res[-1]


def _exchange_wait(name, state, after):
    n, sems, stacks, lands = state

    def body(*refs):
        ins, outs, sems_in = refs[:n], refs[n:2 * n], refs[2 * n:2 * n + 3]
        for w in range(n):
            local, remote = _exchange_copies(ins, outs, sems_in, w)
            local.wait()
            for cp in remote:
                cp.wait_send()
                cp.wait_recv()

    return pl.pallas_call(
        body, name=name, in_specs=[HBM_SPEC] * (2 * n) + [SEM_SPEC] * 3 + [HBM], out_specs=[HBM_SPEC] * n,
        out_shape=[pltpu.HBM(l.shape, l.dtype) for l in lands], input_output_aliases={n + i: i for i in range(n)},
        compiler_params=pltpu.CompilerParams(has_side_effects=EFFECT),
    )(*stacks, *lands, *sems, after)


def _tie(value, *tokens):
    return lax.optimization_barrier((value, tokens))[0]


def _unstack_cols(w):
    return w.transpose(1, 0, 2).reshape(w.shape[1], N_DEV * w.shape[2])


def _stack_cols(w):
    return w.reshape(w.shape[0], N_DEV, w.shape[1] // N_DEV).transpose(1, 0, 2)


def _rope_tables(positions):
    half = MLA_ROPE // 2
    inv_freq = ROPE_THETA ** (-jnp.arange(half, dtype=F32) / half)
    ang = positions.astype(F32)[:, None] * inv_freq
    cos, sin = jnp.cos(ang), jnp.sin(ang)
    t = positions.shape[0]
    cos_t = jnp.concatenate([jnp.ones((t, MLA_NOPE), F32), cos, cos], axis=1)
    sin_t = jnp.concatenate([jnp.zeros((t, MLA_NOPE), F32), -sin, sin], axis=1)
    idx = jnp.arange(MLA_QK)
    partner = jnp.where(idx < MLA_NOPE, -1, jnp.where(idx < MLA_NOPE + half, idx + half, idx - half))
    swap = (idx[:, None] == partner[None, :]).astype(BF16)
    return cos_t, sin_t, swap


def _ffn_fwd(tag, x, gain, wg, wu, wd):
    t = x.shape[0]
    f = wg.shape[2]
    h = _rms_fwd(tag + "_rms", x, gain, BF16)
    g = _mm_stack_out(tag + "_gate", h, wg, BF16)
    u = _mm_stack_out(tag + "_up", h, wu, BF16)
    a = _swiglu_fwd(tag + "_act", g.reshape(N_DEV * t, f), u.reshape(N_DEV * t, f), BF16).reshape(N_DEV, t, f)
    y = _mm_stack_sum(tag + "_down", a, wd, F32, scale=0.5, res=x)
    return y, (x, h, g, u, a)


def _ffn_bwd_weights(tag, dy, saved, wd):
    x, h, g, u, a = saved
    t = x.shape[0]
    f = wd.shape[1]
    da = _mm_stack_nt_out(tag + "_da", dy, wd, BF16, scale=0.5)
    dwd = _mm_stack_tn_right(tag + "_dwd", a, dy, BF16, scale=0.5)
    dg, du = _swiglu_bwd(tag + "_dact", da.reshape(N_DEV * t, f), g.reshape(N_DEV * t, f), u.reshape(N_DEV * t, f), BF16)
    dg, du = dg.reshape(N_DEV, t, f), du.reshape(N_DEV, t, f)
    dwg = _mm_stack_tn_left(tag + "_dwg", h, dg, BF16)
    dwu = _mm_stack_tn_left(tag + "_dwu", h, du, BF16)
    return (dg, du), dwg, dwu, dwd


def _ffn_bwd_input(tag, dy, saved, dgu, gain, wg, wu):
    dg, du = dgu
    dh = _mm_stack_nt_sum(tag + "_dh_g", dg, wg, F32)
    dh = _mm_stack_nt_sum(tag + "_dh_u", du, wu, F32, res=dh)
    return _rms_bwd(tag + "_drms", saved[0], gain, dh, res=dy)


def _local_step(x, mem, positions, sm, comm, target):
    t, d = x.shape
    nm = mem.shape[0]
    gs, gw, wt = {}, {}, {}

    wt.update(comm.weights(0, x))
    x1, ffn1_saved = _ffn_fwd("ffn1", x, sm['ffn1_norm'], wt['ffn1_w_gate'], wt['ffn1_w_up'], wt['ffn1_w_down'])

    comm.early(1, ffn1_saved[4])
    wt.update(comm.weights(1, x1))
    w_in_ref = _unstack_cols(wt['w_in'])
    pieces = []
    for name, n in IN_PAD:
        piece = w_in_ref[:, REF_OFF[name]:REF_OFF[name] + REF_SIZE[name]]
        if n != REF_SIZE[name]:
            piece = jnp.pad(piece, ((0, 0), (0, n - REF_SIZE[name])))
        pieces.append(piece)
    w_in = jnp.concatenate(pieces, axis=1)
    h2 = _rms_fwd("mix_rms", x1, sm['mix_norm'], BF16)
    z = _mm2("mix_in", h2, w_in, NN, F32)
    zs = {name: z[:, PAD_OFF[name]:PAD_OFF[name] + n] for name, n in IN_PAD}
    comm.early(2, z)

    cq = _rms_fwd("mla_q_a_rms", zs['zq'], sm['q_a_norm'], BF16)
    q_raw = _mm_stack_out("mla_q_up", cq, wt['w_q_up'], F32)
    ckv = _rms_fwd("mla_kv_a_rms", zs['zkv'], sm['kv_a_norm'], BF16)
    kv = _mm_stack_out("mla_kv_up", ckv, wt['w_kv_up'], F32)
    zkr = zs['zkr'][:, :MLA_ROPE]
    k_raw = jnp.concatenate([kv[:, :, :MLA_NOPE], jnp.broadcast_to(zkr[None], (MLA_HEADS, t, MLA_ROPE))], axis=2)
    v_mla = kv[:, :, MLA_NOPE:].astype(BF16)
    cos_t, sin_t, swap = _rope_tables(positions)
    q_raw2, k_raw2 = q_raw.reshape(MLA_HEADS * t, MLA_QK), k_raw.reshape(MLA_HEADS * t, MLA_QK)
    qn = _rms_fwd("mla_q_rms", q_raw2, sm['mla_q_norm'], F32)
    kn = _rms_fwd("mla_k_rms", k_raw2, sm['mla_k_norm'], F32)
    qf = _rope("mla_q_rope", qn, cos_t, sin_t, swap, BF16, False).reshape(MLA_HEADS, t, MLA_QK)
    kf = _rope("mla_k_rope", kn, cos_t, sin_t, swap, BF16, False).reshape(MLA_HEADS, t, MLA_QK)
    o_mla = _attn_fwd("mla_attn", qf, kf, v_mla, MLA_QK ** -0.5, True)
    comm.early(3, o_mla)

    w_g2 = jnp.pad(_unstack_cols(wt['gla_w_gate2']), ((0, LANES - GLA_GATE_RANK), (0, 0)))
    pre = _mm2("gla_gate_pre", zs['zg'], w_g2, NN, F32)
    log_a = _gate_fwd("gla_gate", pre, sm['gla_b_gate'])
    o_gla_raw, states = _gla_fwd("gla_scan", zs['gq'], zs['gk'], zs['gv'], log_a, GLA_HEADS)
    o_gla_n = _rms_fwd("gla_out_rms", o_gla_raw, sm['gla_out_norm'], F32)
    o_gla = _swiglu_fwd("gla_out_gate", zs['zr'], o_gla_n, BF16)

    cat = jnp.concatenate([o_mla, o_gla], axis=1)
    w_out = wt['w_out'].reshape(d, d)
    x2 = _mm2("mix_out", cat, w_out, NN, F32, res=x1)

    wt.update(comm.weights(2, x2))
    w_mq, w_mk, w_mv = (wt[n].reshape(d, MEM_HEADS * MEM_HEAD_DIM) for n in ('mem_w_q', 'mem_w_k', 'mem_w_v'))
    hq = _rms_fwd("mem_attn_rms", x2, sm['mem_attn_norm'], BF16)
    hm = _rms_fwd("mem_rms", mem, sm['mem_norm'], BF16)

    def heads_out(name, a, b, out_dtype):
        m, kk = a.shape
        tm = _tile(m, 512)
        return _mm(name, a, b, (m // tm, MEM_HEADS, 1), ((tm, kk), lambda i, h, k: (i, 0)),
                   ((kk, MEM_HEAD_DIM), lambda i, h, k: (0, h)), ((None, tm, MEM_HEAD_DIM), lambda i, h, k: (h, i, 0)),
                   (MEM_HEADS, m, MEM_HEAD_DIM), out_dtype, NN)

    mq_raw = heads_out("mem_q", hq, w_mq, F32)
    mk_raw = heads_out("mem_k", hm, w_mk, F32)
    mv = heads_out("mem_v", hm, w_mv, BF16)
    mq = _rms_fwd("mem_q_rms", mq_raw.reshape(MEM_HEADS * t, MEM_HEAD_DIM), sm['mem_q_norm'], BF16)
    mk = _rms_fwd("mem_k_rms", mk_raw.reshape(MEM_HEADS * nm, MEM_HEAD_DIM), sm['mem_k_norm'], BF16)
    mq, mk = mq.reshape(MEM_HEADS, t, MEM_HEAD_DIM), mk.reshape(MEM_HEADS, nm, MEM_HEAD_DIM)
    o_mem = _attn_fwd("mem_attn", mq, mk, mv, MEM_HEAD_DIM ** -0.5, False)
    w_mo = wt['mem_w_o']
    mo_cols = w_mo.shape[2]
    tm = _tile(t, 512)
    x3 = _mm("mem_out", o_mem, w_mo, (t // tm, N_DEV, 1), ((tm, o_mem.shape[1]), lambda i, j, k: (i, 0)),
             ((None, o_mem.shape[1], mo_cols), lambda i, j, k: (j, 0, 0)), ((tm, mo_cols), lambda i, j, k: (i, j)),
             (t, d), F32, NN, res=x2)

    wt.update(comm.weights(3, x3))
    y, ffn2_saved = _ffn_fwd("ffn2", x3, sm['ffn2_norm'], wt['ffn2_w_gate'], wt['ffn2_w_up'], wt['ffn2_w_down'])
    dy, loss_lanes = _loss("loss", y, target)

    dgu, gw['ffn2_w_gate'], gw['ffn2_w_up'], gw['ffn2_w_down'] = _ffn_bwd_weights("ffn2", dy, ffn2_saved, wt['ffn2_w_down'])
    token = comm.grads(3, gw)
    dx3, gs['ffn2_norm'] = _ffn_bwd_input("ffn2", dy, ffn2_saved, _tie(dgu, token), sm['ffn2_norm'],
                                          wt['ffn2_w_gate'], wt['ffn2_w_up'])

    do_mem = _mm("mem_do", dx3, w_mo, (t // tm, MEM_HEADS, N_DEV), ((tm, mo_cols), lambda i, h, k: (i, k)),
                 ((None, MEM_HEAD_DIM, mo_cols), lambda i, h, k: (k, h, 0)), ((tm, MEM_HEAD_DIM), lambda i, h, k: (i, h)),
                 (t, MEM_HEADS * MEM_HEAD_DIM), BF16, NT)
    tk = _tile(t, 512)
    gw['mem_w_o'] = _mm("mem_dwo", o_mem, dx3, (N_DEV, 1, t // tk), ((tk, o_mem.shape[1]), lambda j, i, k: (k, 0)),
                        ((tk, mo_cols), lambda j, i, k: (k, j)), ((None, o_mem.shape[1], mo_cols), lambda j, i, k: (j, 0, 0)),
                        w_mo.shape, BF16, TN)
    dmq, dmk, dmv = _attn_bwd("mem_dattn", mq, mk, mv, do_mem, MEM_HEAD_DIM ** -0.5, False)
    dmq_raw, gs['mem_q_norm'] = _rms_bwd("mem_q_drms", mq_raw.reshape(MEM_HEADS * t, MEM_HEAD_DIM), sm['mem_q_norm'],
                                         dmq.reshape(MEM_HEADS * t, MEM_HEAD_DIM))
    dmk_raw, gs['mem_k_norm'] = _rms_bwd("mem_k_drms", mk_raw.reshape(MEM_HEADS * nm, MEM_HEAD_DIM), sm['mem_k_norm'],
                                         dmk.reshape(MEM_HEADS * nm, MEM_HEAD_DIM))
    dmq_raw = dmq_raw.reshape(MEM_HEADS, t, MEM_HEAD_DIM)
    dmk_raw = dmk_raw.reshape(MEM_HEADS, nm, MEM_HEAD_DIM)

    def heads_in_nt(name, a, b, res=None):
        m, n = a.shape[1], b.shape[0]
        tm_, tn_ = _tile(m, 512), _tile(n, 1024)
        return _mm(name, a, b, (m // tm_, n // tn_, MEM_HEADS), ((None, tm_, MEM_HEAD_DIM), lambda i, j, k: (k, i, 0)),
                   ((tn_, MEM_HEAD_DIM), lambda i, j, k: (j, k)), ((tm_, tn_), lambda i, j, k: (i, j)), (m, n), F32, NT,
                   None, res)

    def heads_tn(name, a, b):
        m, kp = a.shape
        tm_, tk_ = _tile(kp, 1024), _tile(m, 512)
        return _mm(name, a, b, (kp // tm_, MEM_HEADS, m // tk_), ((tk_, tm_), lambda i, h, k: (k, i)),
                   ((None, tk_, MEM_HEAD_DIM), lambda i, h, k: (h, k, 0)), ((tm_, MEM_HEAD_DIM), lambda i, h, k: (i, h)),
                   (kp, MEM_HEADS * MEM_HEAD_DIM), BF16, TN)

    dhq = heads_in_nt("mem_dhq", dmq_raw, w_mq)
    gw['mem_w_q'] = heads_tn("mem_dwq", hq, dmq_raw).reshape(wt['mem_w_q'].shape)
    dhm = heads_in_nt("mem_dhm_k", dmk_raw, w_mk)
    dhm = heads_in_nt("mem_dhm_v", dmv, w_mv, res=dhm)
    gw['mem_w_k'] = heads_tn("mem_dwk", hm, dmk_raw).reshape(wt['mem_w_k'].shape)
    gw['mem_w_v'] = heads_tn("mem_dwv", hm, dmv).reshape(wt['mem_w_v'].shape)
    _, gs['mem_norm'] = _rms_bwd("mem_drms", mem, sm['mem_norm'], dhm)
    token = comm.grads(2, gw)
    dx2, gs['mem_attn_norm'] = _rms_bwd("mem_attn_drms", x2, sm['mem_attn_norm'], _tie(dhq, token), res=dx3)

    dcat = _mm2("mix_dcat", dx2, w_out, NT, F32)
    gw['w_out'] = _mm2("mix_dwout", cat, dx2, TN, BF16, tm=1024, tn=2048, tk=512).reshape(wt['w_out'].shape)
    do_mla, do_gla = dcat[:, :MLA_HEADS * MLA_V], dcat[:, MLA_HEADS * MLA_V:]

    dzr, dgn = _swiglu_bwd("gla_out_dgate", do_gla, zs['zr'], o_gla_n, F32)
    do_gla_raw, gs['gla_out_norm'] = _rms_bwd("gla_out_drms", o_gla_raw, sm['gla_out_norm'], dgn)
    dgq, dgk, dgv, dlog_a = _gla_bwd("gla_dscan", zs['gq'], zs['gk'], zs['gv'], log_a, states, do_gla_raw, GLA_HEADS)
    dpre, gs['gla_b_gate'] = _gate_bwd("gla_dgate", pre, sm['gla_b_gate'], dlog_a)
    dw_g2 = _mm2("gla_dwgate", zs['zg'], dpre, TN, BF16, tk=512)
    gw['gla_w_gate2'] = _stack_cols(dw_g2[:GLA_GATE_RANK])
    dzg = _mm2("gla_dzg", dpre, w_g2, NT, F32)

    dqf, dkf, dv_mla = _attn_bwd("mla_dattn", qf, kf, v_mla, do_mla, MLA_QK ** -0.5, True)
    dqn = _rope("mla_q_drope", dqf.reshape(MLA_HEADS * t, MLA_QK), cos_t, sin_t, swap, F32, True)
    dkn = _rope("mla_k_drope", dkf.reshape(MLA_HEADS * t, MLA_QK), cos_t, sin_t, swap, F32, True)
    dq_raw, gs['mla_q_norm'] = _rms_bwd("mla_q_drms", q_raw2, sm['mla_q_norm'], dqn)
    dk_raw, gs['mla_k_norm'] = _rms_bwd("mla_k_drms", k_raw2, sm['mla_k_norm'], dkn)
    dq_raw = dq_raw.reshape(MLA_HEADS, t, MLA_QK)
    dk_raw = dk_raw.reshape(MLA_HEADS, t, MLA_QK)
    dkv = jnp.concatenate([dk_raw[:, :, :MLA_NOPE], dv_mla], axis=2)
    dzkr = jnp.sum(dk_raw[:, :, MLA_NOPE:], axis=0)
    gw['w_q_up'] = _mm_stack_tn_left("mla_dwq", cq, dq_raw, BF16)
    gw['w_kv_up'] = _mm_stack_tn_left("mla_dwkv", ckv, dkv, BF16)
    dcq = _mm_stack_nt_sum("mla_dcq", dq_raw, wt['w_q_up'], F32)
    dckv = _mm_stack_nt_sum("mla_dckv", dkv, wt['w_kv_up'], F32)
    dzq, gs['q_a_norm'] = _rms_bwd("mla_q_a_drms", zs['zq'], sm['q_a_norm'], dcq)
    dzkv, gs['kv_a_norm'] = _rms_bwd("mla_kv_a_drms", zs['zkv'], sm['kv_a_norm'], dckv)

    dzs = {'zq': dzq, 'zkv': dzkv, 'gq': dgq, 'gk': dgk, 'gv': dgv, 'zr': dzr,
           'zkr': jnp.pad(dzkr, ((0, 0), (0, LANES - MLA_ROPE))), 'zg': dzg}
    dz = jnp.concatenate([dzs[name].astype(BF16) for name, _ in IN_PAD], axis=1)
    dw_in = _mm2("mix_dwin", h2, dz, TN, BF16, tm=1024, tn=2048, tk=512)
    dw_in_ref = jnp.concatenate([dw_in[:, PAD_OFF[name]:PAD_OFF[name] + n] for name, n in IN_REF], axis=1)
    gw['w_in'] = _stack_cols(dw_in_ref)
    token = comm.grads(1, gw)
    dh2 = _mm2("mix_dh", _tie(dz, token), w_in, NT, F32)
    dx1, gs['mix_norm'] = _rms_bwd("mix_drms", x1, sm['mix_norm'], dh2, res=dx2)

    dgu, gw['ffn1_w_gate'], gw['ffn1_w_up'], gw['ffn1_w_down'] = _ffn_bwd_weights("ffn1", dx1, ffn1_saved, wt['ffn1_w_down'])
    token = comm.grads(0, gw)
    grad_x, gs['ffn1_norm'] = _ffn_bwd_input("ffn1", dx1, ffn1_saved, _tie(dgu, token), sm['ffn1_norm'],
                                             wt['ffn1_w_gate'], wt['ffn1_w_up'])
    return loss_lanes, grad_x, gs


def _pad_lanes(v):
    n = v.shape[1]
    return jnp.pad(v, ((0, 0), (0, -n % LANES)))


def _pack_small(vals):
    return jnp.concatenate([_pad_lanes(vals[n]) for n in SMALL], axis=1)


def _unpack_small(packed, like):
    out, off = {}, 0
    for n in SMALL:
        size = like[n].shape[1]
        out[n] = packed[:, off:off + size]
        off += size + (-size % LANES)
    return out


def kernel(x, mem, positions, ffn1_norm, ffn1_w_gate, ffn1_w_up, ffn1_w_down, mix_norm, w_in, q_a_norm, w_q_up, kv_a_norm, w_kv_up, mla_q_norm, mla_k_norm, gla_w_gate2, gla_b_gate, gla_out_norm, w_out, mem_attn_norm, mem_norm, mem_w_q, mem_w_k, mem_w_v, mem_w_o, mem_q_norm, mem_k_norm, ffn2_norm, ffn2_w_gate, ffn2_w_up, ffn2_w_down, loss_target, m_ffn1_norm, m_ffn1_w_gate, m_ffn1_w_up, m_ffn1_w_down, m_mix_norm, m_w_in, m_q_a_norm, m_w_q_up, m_kv_a_norm, m_w_kv_up, m_mla_q_norm, m_mla_k_norm, m_gla_w_gate2, m_gla_b_gate, m_gla_out_norm, m_w_out, m_mem_attn_norm, m_mem_norm, m_mem_w_q, m_mem_w_k, m_mem_w_v, m_mem_w_o, m_mem_q_norm, m_mem_k_norm, m_ffn2_norm, m_ffn2_w_gate, m_ffn2_w_up, m_ffn2_w_down, v_ffn1_norm, v_ffn1_w_gate, v_ffn1_w_up, v_ffn1_w_down, v_mix_norm, v_w_in, v_q_a_norm, v_w_q_up, v_kv_a_norm, v_w_kv_up, v_mla_q_norm, v_mla_k_norm, v_gla_w_gate2, v_gla_b_gate, v_gla_out_norm, v_w_out, v_mem_attn_norm, v_mem_norm, v_mem_w_q, v_mem_w_k, v_mem_w_v, v_mem_w_o, v_mem_q_norm, v_mem_k_norm, v_ffn2_norm, v_ffn2_w_gate, v_ffn2_w_up, v_ffn2_w_down):
    inp = dict(locals())
    x, mem, positions, target = inp['x'][0], inp['mem'][0], inp['positions'][0], inp['loss_target'][0]
    sm = {n: inp[n] for n in SMALL}
    out = {}

    class Comm:
        def __init__(self):
            started = [_gather_start(f"gather_start_{gi}", [inp[n][0].astype(BF16) for n in group])
                       for gi, group in enumerate(GROUPS)]
            self.gathers = [state for state, _ in started]
            self.tokens = [token for _, token in started]
            self.forwards, self.exchanges = {}, {}

        def early(self, gi, after):
            if gi not in self.forwards:
                self.forwards[gi] = _gather_mid(f"gather_mid_{gi}", self.gathers[gi], after)

        def weights(self, gi, after):
            self.early(gi, after)
            return dict(zip(GROUPS[gi], _gather_end(f"gather_end_{gi}", self.forwards[gi], after)))

        def grads(self, gi, stacks):
            self.exchanges[gi], token = _exchange_start(f"exchange_start_{gi}", [stacks[n] for n in GROUPS[gi]])
            return token

        def update(self, gi, after):
            parts = _exchange_wait(f"exchange_wait_{gi}", self.exchanges[gi], after)
            for n, p in zip(GROUPS[gi], parts):
                res = _adamw("adamw_" + n, inp[n][0], inp['m_' + n][0], inp['v_' + n][0], p)
                for kind, r in zip(('grad_', 'delta_', 'new_m_', 'new_v_'), res):
                    out[kind + n] = r[None]
            return res[0]

    comm = Comm()
    loss_lanes, grad_x, gs = _local_step(_tie(x, *comm.tokens), mem, positions, sm, comm, target)
    out['loss'] = lax.psum(jnp.sum(loss_lanes), ("x", "y", "c"))
    out['grad_x'] = grad_x[None]

    small_parts = _all_gather("gather_small", [_pack_small(gs)])[0]
    res = _adamw("adamw_small", _pack_small(sm), _pack_small({n: inp['m_' + n] for n in SMALL}),
                 _pack_small({n: inp['v_' + n] for n in SMALL}), small_parts)
    for kind, r in zip(('grad_', 'delta_', 'new_m_', 'new_v_'), res):
        for n, val in _unpack_small(r, sm).items():
            out[kind + n] = val

    after = res[0]
    for gi in (3, 2, 1, 0):
        after = comm.update(gi, after)

    names = ['loss', 'grad_x'] + [k + n for k in ('grad_', 'delta_', 'new_m_', 'new_v_') for n in WEIGHTS]
    return tuple(out[n] for n in names)
```

```python
import math

import jax
import jax.numpy as jnp
from jax import lax
from jax.experimental import pallas as pl
from jax.experimental.pallas import tpu as pltpu

F32 = jnp.float32
BF16 = jnp.bfloat16

N_DEV = 8
EPS = 1e-6
CHUNK = 64
MLA_HEADS, MLA_NOPE, MLA_ROPE, MLA_V = 8, 128, 64, 128
MLA_QK = MLA_NOPE + MLA_ROPE
MLA_Q_RANK, MLA_KV_RANK = 512, 256
ROPE_THETA = 10000.0
GLA_HEADS, GLA_DK, GLA_DV, GLA_GATE_RANK = 4, 128, 256, 16
GLA_TAU = 16.0
MEM_HEADS, MEM_HEAD_DIM = 4, 128
ADAM_LR, ADAM_B1, ADAM_B2, ADAM_EPS, ADAM_WD, ADAM_STEP = 0.001, 0.9, 0.999, 1e-08, 0.01, 10

V7X_VMEM_BYTES = 64 * 1024 * 1024
LANES = 128

NN = (((1,), (0,)), ((), ()))
NT = (((1,), (1,)), ((), ()))
TN = (((0,), (0,)), ((), ()))

WEIGHTS = ['ffn1_norm', 'ffn1_w_gate', 'ffn1_w_up', 'ffn1_w_down', 'mix_norm', 'w_in', 'q_a_norm', 'w_q_up',
           'kv_a_norm', 'w_kv_up', 'mla_q_norm', 'mla_k_norm', 'gla_w_gate2', 'gla_b_gate', 'gla_out_norm', 'w_out',
           'mem_attn_norm', 'mem_norm', 'mem_w_q', 'mem_w_k', 'mem_w_v', 'mem_w_o', 'mem_q_norm', 'mem_k_norm',
           'ffn2_norm', 'ffn2_w_gate', 'ffn2_w_up', 'ffn2_w_down']
SMALL = ['ffn1_norm', 'mix_norm', 'q_a_norm', 'kv_a_norm', 'mla_q_norm', 'mla_k_norm', 'gla_b_gate', 'gla_out_norm',
         'mem_attn_norm', 'mem_norm', 'mem_q_norm', 'mem_k_norm', 'ffn2_norm']
GROUPS = [['ffn1_w_gate', 'ffn1_w_up', 'ffn1_w_down'],
          ['w_in', 'w_q_up', 'w_kv_up', 'gla_w_gate2', 'w_out'],
          ['mem_w_q', 'mem_w_k', 'mem_w_v', 'mem_w_o'],
          ['ffn2_w_gate', 'ffn2_w_up', 'ffn2_w_down']]

IN_REF = [('zq', 512), ('zkv', 256), ('zkr', 64), ('gq', 512), ('gk', 512), ('gv', 1024), ('zg', 16), ('zr', 1024)]
IN_PAD = [('zq', 512), ('zkv', 256), ('gq', 512), ('gk', 512), ('gv', 1024), ('zr', 1024), ('zkr', 128), ('zg', 128)]
IN_WIDTH = sum(n for _, n in IN_REF)
IN_PAD_WIDTH = sum(n for _, n in IN_PAD)


def _offsets(layout):
    out, off = {}, 0
    for name, n in layout:
        out[name] = off
        off += n
    return out


REF_OFF, PAD_OFF = _offsets(IN_REF), _offsets(IN_PAD)
REF_SIZE = dict(IN_REF)


def _tile(n, pref):
    return pref if n % pref == 0 else n


def _block_bytes(blk, dtype):
    dims = [d for d in blk if d is not None]
    if len(dims) >= 1:
        dims[-1] = -(-dims[-1] // LANES) * LANES
    return math.prod(dims) * jnp.dtype(dtype).itemsize


def _vmem_limit(pipelined_bytes, resident_bytes=0):
    need = 2 * pipelined_bytes + resident_bytes + (8 << 20)
    return int(min(max(need, 32 << 20), V7X_VMEM_BYTES - (6 << 20)))


class _Chain:
    last = None


def _chained(body, *, in_specs, link=0, **kwargs):
    def call(*operands):
        dep = _Chain.last
        if dep is None:
            res = pl.pallas_call(body, in_specs=in_specs, **kwargs)(*operands)
        else:
            n = len(operands)

            def chained_body(*refs):
                body(*refs[:n], *refs[n + 1:])

            res = pl.pallas_call(chained_body, in_specs=list(in_specs) + [pl.BlockSpec(memory_space=pl.ANY)],
                                 **kwargs)(*operands, dep)
        _Chain.last = res[link] if isinstance(res, (list, tuple)) else res
        return res

    return call


def _rowwise_params(*semantics):
    return pltpu.CompilerParams(dimension_semantics=semantics, vmem_limit_bytes=48 << 20)


def _mm(name, a, b, grid, a_spec, b_spec, o_spec, out_shape, out_dtype, dims, scale=None, res=None):
    nk = grid[2]
    (a_blk, a_map), (b_blk, b_map), (o_blk, o_map) = a_spec, b_spec, o_spec
    acc_shape = tuple(d for d in o_blk if d is not None)

    def body(*refs):
        if res is None:
            a_ref, b_ref, o_ref = refs[:3]
            r_ref, rest = None, refs[3:]
        else:
            a_ref, b_ref, r_ref, o_ref = refs[:4]
            rest = refs[4:]

        def product():
            return lax.dot_general(a_ref[...].astype(BF16), b_ref[...].astype(BF16), dims, preferred_element_type=F32)

        def finish(r):
            if scale is not None:
                r = r * scale
            if r_ref is not None:
                r = r + r_ref[...].astype(F32)
            o_ref[...] = r.astype(o_ref.dtype)

        if nk == 1:
            finish(product())
        else:
            acc = rest[0]
            k = pl.program_id(2)

            @pl.when(k == 0)
            def _():
                acc[...] = product()

            @pl.when(k > 0)
            def _():
                acc[...] += product()

            @pl.when(k == nk - 1)
            def _():
                finish(acc[...])

    in_specs = [pl.BlockSpec(a_blk, a_map), pl.BlockSpec(b_blk, b_map)]
    operands = [a, b]
    piped = _block_bytes(a_blk, a.dtype) + _block_bytes(b_blk, b.dtype) + _block_bytes(o_blk, out_dtype)
    if res is not None:
        in_specs.append(pl.BlockSpec(o_blk, o_map))
        operands.append(res)
        piped += _block_bytes(o_blk, res.dtype)
    scratch = [pltpu.VMEM(acc_shape, F32)] if nk > 1 else []
    return _chained(
        body, name=name, grid=grid, in_specs=in_specs, out_specs=pl.BlockSpec(o_blk, o_map),
        out_shape=jax.ShapeDtypeStruct(out_shape, out_dtype), scratch_shapes=scratch,
        compiler_params=pltpu.CompilerParams(
            dimension_semantics=("parallel", "parallel", "arbitrary"),
            vmem_limit_bytes=_vmem_limit(piped, 3 * _block_bytes(acc_shape, F32))),
    )(*operands)


def _mm2(name, a, b, dims, out_dtype, tm=512, tn=1024, tk=2048, scale=None, res=None):
    if dims is NN:
        (m, kk), n = a.shape, b.shape[1]
    elif dims is NT:
        (m, kk), n = a.shape, b.shape[0]
    else:
        (kk, m), n = a.shape, b.shape[1]
    tm, tn, tk = _tile(m, tm), _tile(n, tn), _tile(kk, tk)
    a_spec = ((tk, tm), lambda i, j, k: (k, i)) if dims is TN else ((tm, tk), lambda i, j, k: (i, k))
    b_spec = ((tn, tk), lambda i, j, k: (j, k)) if dims is NT else ((tk, tn), lambda i, j, k: (k, j))
    return _mm(name, a, b, (m // tm, n // tn, kk // tk), a_spec, b_spec, ((tm, tn), lambda i, j, k: (i, j)),
               (m, n), out_dtype, dims, scale, res)


def _mm_stack_out(name, a, b, out_dtype, tm=512, tk=2048):
    (m, kk), (nj, _, n) = a.shape, b.shape
    tm, tk = _tile(m, tm), _tile(kk, tk)
    return _mm(name, a, b, (nj, m // tm, kk // tk), ((tm, tk), lambda j, i, k: (i, k)),
               ((None, tk, n), lambda j, i, k: (j, k, 0)), ((None, tm, n), lambda j, i, k: (j, i, 0)),
               (nj, m, n), out_dtype, NN)


def _mm_stack_nt_out(name, a, b, out_dtype, scale=None, tm=512, tk=2048):
    (m, kk), (nj, n, _) = a.shape, b.shape
    tm, tk = _tile(m, tm), _tile(kk, tk)
    return _mm(name, a, b, (nj, m // tm, kk // tk), ((tm, tk), lambda j, i, k: (i, k)),
               ((None, n, tk), lambda j, i, k: (j, 0, k)), ((None, tm, n), lambda j, i, k: (j, i, 0)),
               (nj, m, n), out_dtype, NT, scale)


def _mm_stack_sum(name, a, b, out_dtype, scale=None, res=None, tm=512, tn=1024):
    (nj, m, f), n = a.shape, b.shape[2]
    tm, tn = _tile(m, tm), _tile(n, tn)
    return _mm(name, a, b, (m // tm, n // tn, nj), ((None, tm, f), lambda i, j, k: (k, i, 0)),
               ((None, f, tn), lambda i, j, k: (k, 0, j)), ((tm, tn), lambda i, j, k: (i, j)),
               (m, n), out_dtype, NN, scale, res)


def _mm_stack_nt_sum(name, a, b, out_dtype, res=None, tm=512, tn=1024):
    (nj, m, f), n = a.shape, b.shape[1]
    tm, tn = _tile(m, tm), _tile(n, tn)
    return _mm(name, a, b, (m // tm, n // tn, nj), ((None, tm, f), lambda i, j, k: (k, i, 0)),
               ((None, tn, f), lambda i, j, k: (k, j, 0)), ((tm, tn), lambda i, j, k: (i, j)),
               (m, n), out_dtype, NT, None, res)


def _mm_stack_tn_left(name, a, b, out_dtype, tm=1024, tk=512):
    (m, kp), (nj, _, n) = a.shape, b.shape
    tm, tk = _tile(kp, tm), _tile(m, tk)
    return _mm(name, a, b, (nj, kp // tm, m // tk), ((tk, tm), lambda j, i, k: (k, i)),
               ((None, tk, n), lambda j, i, k: (j, k, 0)), ((None, tm, n), lambda j, i, k: (j, i, 0)),
               (nj, kp, n), out_dtype, TN)


def _mm_stack_tn_right(name, a, b, out_dtype, scale=None, tn=2048, tk=512):
    (nj, m, f), n = a.shape, b.shape[1]
    tn, tk = _tile(n, tn), _tile(m, tk)
    return _mm(name, a, b, (nj, n // tn, m // tk), ((None, tk, f), lambda j, i, k: (j, k, 0)),
               ((tk, tn), lambda j, i, k: (k, i)), ((None, f, tn), lambda j, i, k: (j, 0, i)),
               (nj, f, n), out_dtype, TN, scale)


def _rms_fwd(name, x, g, out_dtype, tm=256):
    rows, cols = x.shape
    d = g.shape[1]
    tm = _tile(rows, tm)

    def body(x_ref, g_ref, o_ref):
        xf = x_ref[...].astype(F32)
        r = lax.rsqrt(jnp.mean(xf * xf, axis=-1, keepdims=True) + EPS)
        o_ref[...] = (xf * r * g_ref[...]).astype(o_ref.dtype)

    return _chained(
        body, name=name, grid=(rows // tm, cols // d),
        in_specs=[pl.BlockSpec((tm, d), lambda i, c: (i, c)), pl.BlockSpec((1, d), lambda i, c: (0, 0))],
        out_specs=pl.BlockSpec((tm, d), lambda i, c: (i, c)),
        out_shape=jax.ShapeDtypeStruct((rows, cols), out_dtype),
        compiler_params=_rowwise_params("parallel", "parallel"),
    )(x, g)


def _rms_bwd(name, x, g, dy, res=None, tm=256):
    rows, cols = x.shape
    d = g.shape[1]
    tm = _tile(rows, tm)

    def body(*refs):
        if res is None:
            x_ref, g_ref, dy_ref, dx_ref, dg_ref = refs
            r_ref = None
        else:
            x_ref, g_ref, dy_ref, r_ref, dx_ref, dg_ref = refs
        xf = x_ref[...].astype(F32)
        r = lax.rsqrt(jnp.mean(xf * xf, axis=-1, keepdims=True) + EPS)
        xhat = xf * r
        dyf = dy_ref[...].astype(F32)

        @pl.when((pl.program_id(0) == 0) & (pl.program_id(1) == 0))
        def _():
            dg_ref[...] = jnp.zeros_like(dg_ref)

        dg_ref[...] += jnp.sum(dyf * xhat, axis=0, keepdims=True)
        dxh = dyf * g_ref[...]
        dx = r * (dxh - xhat * jnp.mean(dxh * xhat, axis=-1, keepdims=True))
        if r_ref is not None:
            dx = dx + r_ref[...].astype(F32)
        dx_ref[...] = dx

    blk = pl.BlockSpec((tm, d), lambda i, c: (i, c))
    in_specs = [blk, pl.BlockSpec((1, d), lambda i, c: (0, 0)), blk]
    operands = [x, g, dy]
    if res is not None:
        in_specs.append(blk)
        operands.append(res)
    return _chained(
        body, name=name, grid=(rows // tm, cols // d), in_specs=in_specs,
        out_specs=(blk, pl.BlockSpec((1, d), lambda i, c: (0, 0))),
        out_shape=(jax.ShapeDtypeStruct((rows, cols), F32), jax.ShapeDtypeStruct((1, d), F32)),
        compiler_params=_rowwise_params("arbitrary", "arbitrary"),
    )(*operands)


def _swiglu_fwd(name, g, u, out_dtype, tm=256):
    rows, cols = g.shape
    tm = _tile(rows, tm)

    def body(g_ref, u_ref, o_ref):
        gf = g_ref[...].astype(F32)
        o_ref[...] = (gf * jax.nn.sigmoid(gf) * u_ref[...].astype(F32)).astype(o_ref.dtype)

    blk = pl.BlockSpec((tm, cols), lambda i: (i, 0))
    return _chained(
        body, name=name, grid=(rows // tm,), in_specs=[blk, blk], out_specs=blk,
        out_shape=jax.ShapeDtypeStruct((rows, cols), out_dtype),
        compiler_params=_rowwise_params("parallel"),
    )(g, u)


def _swiglu_bwd(name, da, g, u, out_dtype, tm=256):
    rows, cols = g.shape
    tm = _tile(rows, tm)

    def body(da_ref, g_ref, u_ref, dg_ref, du_ref):
        gf = g_ref[...].astype(F32)
        daf = da_ref[...].astype(F32)
        sig = jax.nn.sigmoid(gf)
        du_ref[...] = (daf * gf * sig).astype(du_ref.dtype)
        dg_ref[...] = (daf * u_ref[...].astype(F32) * sig * (1.0 + gf * (1.0 - sig))).astype(dg_ref.dtype)

    blk = pl.BlockSpec((tm, cols), lambda i: (i, 0))
    sds = jax.ShapeDtypeStruct((rows, cols), out_dtype)
    return _chained(
        body, name=name, grid=(rows // tm,), in_specs=[blk, blk, blk], out_specs=(blk, blk), out_shape=(sds, sds),
        compiler_params=_rowwise_params("parallel"),
    )(da, g, u)


def _split3(x):
    hi = x.astype(BF16)
    r1 = x - hi.astype(F32)
    mid = r1.astype(BF16)
    lo = (r1 - mid.astype(F32)).astype(BF16)
    return hi, mid, lo


def _rope(name, x, cos_t, sin_t, swap, out_dtype, backward, tm=256):
    rows, d = x.shape
    t = cos_t.shape[0]
    tm = _tile(t, tm)
    nt = t // tm

    def body(x_ref, c_ref, s_ref, p_ref, o_ref):
        xf = x_ref[...].astype(F32)
        p = p_ref[...]
        lhs = xf * s_ref[...] if backward else xf
        sw = sum(jnp.dot(part, p, preferred_element_type=F32) for part in _split3(lhs))
        if not backward:
            sw = sw * s_ref[...]
        o_ref[...] = (xf * c_ref[...] + sw).astype(o_ref.dtype)

    blk = pl.BlockSpec((tm, d), lambda i: (i, 0))
    tab = pl.BlockSpec((tm, d), lambda i: (i % nt, 0))
    return _chained(
        body, name=name, grid=(rows // tm,), in_specs=[blk, tab, tab, pl.BlockSpec((d, d), lambda i: (0, 0))],
        out_specs=blk, out_shape=jax.ShapeDtypeStruct((rows, d), out_dtype),
        compiler_params=_rowwise_params("parallel"),
    )(x, cos_t, sin_t, swap)


def _gate_fwd(name, pre, bias, tm=256):
    rows, cols = pre.shape
    tm = _tile(rows, tm)

    def body(p_ref, b_ref, o_ref):
        z = p_ref[...] + b_ref[...]
        o_ref[...] = (jnp.minimum(z, 0.0) - jnp.log(1.0 + jnp.exp(-jnp.abs(z)))) * (1.0 / GLA_TAU)

    blk = pl.BlockSpec((tm, cols), lambda i: (i, 0))
    return _chained(
        body, name=name, grid=(rows // tm,), in_specs=[blk, pl.BlockSpec((1, cols), lambda i: (0, 0))], out_specs=blk,
        out_shape=jax.ShapeDtypeStruct((rows, cols), F32),
        compiler_params=_rowwise_params("parallel"),
    )(pre, bias)


def _gate_bwd(name, pre, bias, dla, tm=256):
    rows, cols = pre.shape
    tm = _tile(rows, tm)

    def body(p_ref, b_ref, d_ref, dp_ref, db_ref):
        z = p_ref[...] + b_ref[...]
        dp = d_ref[...] * (1.0 / GLA_TAU) / (1.0 + jnp.exp(z))
        dp_ref[...] = dp

        @pl.when(pl.program_id(0) == 0)
        def _():
            db_ref[...] = jnp.zeros_like(db_ref)

        db_ref[...] += jnp.sum(dp, axis=0, keepdims=True)

    blk = pl.BlockSpec((tm, cols), lambda i: (i, 0))
    row = pl.BlockSpec((1, cols), lambda i: (0, 0))
    return _chained(
        body, name=name, grid=(rows // tm,), in_specs=[blk, row, blk], out_specs=(blk, row),
        out_shape=(jax.ShapeDtypeStruct((rows, cols), F32), jax.ShapeDtypeStruct((1, cols), F32)),
        compiler_params=_rowwise_params("arbitrary"),
    )(pre, bias, dla)


def _loss(name, y, target, tm=256):
    rows, d = y.shape
    tm = _tile(rows, tm)

    def body(y_ref, t_ref, dy_ref, l_ref):
        err = y_ref[...] - t_ref[...]
        dy_ref[...] = err * (1.0 / d)

        @pl.when(pl.program_id(0) == 0)
        def _():
            l_ref[...] = jnp.zeros_like(l_ref)

        sq = (err * err).reshape(tm // 8, 8, d)
        l_ref[...] += jnp.sum(sq, axis=0) * (0.5 / d)

    blk = pl.BlockSpec((tm, d), lambda i: (i, 0))
    return _chained(
        body, name=name, grid=(rows // tm,), in_specs=[blk, blk],
        out_specs=(blk, pl.BlockSpec((8, d), lambda i: (0, 0))),
        out_shape=(jax.ShapeDtypeStruct((rows, d), F32), jax.ShapeDtypeStruct((8, d), F32)),
        compiler_params=_rowwise_params("arbitrary"),
    )(y, target)


def _scores(q, k, scale, causal, q0):
    s = lax.dot_general(q, k, NT, preferred_element_type=F32) * scale
    if causal:
        qc = (q0 + lax.broadcasted_iota(jnp.int32, s.shape, 0)) // CHUNK
        kc = lax.broadcasted_iota(jnp.int32, s.shape, 1) // CHUNK
        s = jnp.where(kc <= qc, s, -1e30)
    e = jnp.exp(s - jnp.max(s, axis=-1, keepdims=True))
    return e, jnp.sum(e, axis=-1, keepdims=True)


def _attn_fwd(name, q, k, v, scale, causal, tq=256):
    nh, t, dk = q.shape
    tk, dv = k.shape[1], v.shape[2]
    tq = _tile(t, tq)

    def body(q_ref, k_ref, v_ref, o_ref):
        e, l = _scores(q_ref[...], k_ref[...], scale, causal, pl.program_id(1) * tq)
        o = jnp.dot(e.astype(BF16), v_ref[...], preferred_element_type=F32)
        o_ref[...] = (o / l).astype(o_ref.dtype)

    return _chained(
        body, name=name, grid=(nh, t // tq),
        in_specs=[pl.BlockSpec((None, tq, dk), lambda h, i: (h, i, 0)), pl.BlockSpec((None, tk, dk), lambda h, i: (h, 0, 0)),
                  pl.BlockSpec((None, tk, dv), lambda h, i: (h, 0, 0))],
        out_specs=pl.BlockSpec((tq, dv), lambda h, i: (i, h)),
        out_shape=jax.ShapeDtypeStruct((t, nh * dv), BF16),
        compiler_params=pltpu.CompilerParams(dimension_semantics=("parallel", "parallel"),
                                             vmem_limit_bytes=_vmem_limit(0, 6 * tq * tk * 4)),
    )(q, k, v)


def _attn_bwd(name, q, k, v, do, scale, causal, tq=256):
    nh, t, dk = q.shape
    tk, dv = k.shape[1], v.shape[2]
    tq = _tile(t, tq)

    def body(q_ref, k_ref, v_ref, do_ref, dq_ref, dk_ref, dv_ref):
        qb, kb = q_ref[...], k_ref[...]
        e, l = _scores(qb, kb, scale, causal, pl.program_id(1) * tq)
        p = e / l
        dob = do_ref[...].astype(BF16)
        dp = lax.dot_general(dob, v_ref[...], NT, preferred_element_type=F32)
        ds = (p * (dp - jnp.sum(p * dp, axis=-1, keepdims=True)) * scale).astype(BF16)
        dq_ref[...] = jnp.dot(ds, kb, preferred_element_type=F32)

        @pl.when(pl.program_id(1) == 0)
        def _():
            dk_ref[...] = jnp.zeros_like(dk_ref)
            dv_ref[...] = jnp.zeros_like(dv_ref)

        dk_ref[...] += lax.dot_general(ds, qb, TN, preferred_element_type=F32)
        dv_ref[...] += lax.dot_general(p.astype(BF16), dob, TN, preferred_element_type=F32)

    return _chained(
        body, name=name, grid=(nh, t // tq),
        in_specs=[pl.BlockSpec((None, tq, dk), lambda h, i: (h, i, 0)), pl.BlockSpec((None, tk, dk), lambda h, i: (h, 0, 0)),
                  pl.BlockSpec((None, tk, dv), lambda h, i: (h, 0, 0)), pl.BlockSpec((tq, dv), lambda h, i: (i, h))],
        out_specs=(pl.BlockSpec((None, tq, dk), lambda h, i: (h, i, 0)), pl.BlockSpec((None, tk, dk), lambda h, i: (h, 0, 0)),
                   pl.BlockSpec((None, tk, dv), lambda h, i: (h, 0, 0))),
        out_shape=(jax.ShapeDtypeStruct((nh, t, dk), F32), jax.ShapeDtypeStruct((nh, tk, dk), F32),
                   jax.ShapeDtypeStruct((nh, tk, dv), F32)),
        compiler_params=pltpu.CompilerParams(dimension_semantics=("parallel", "arbitrary"),
                                             vmem_limit_bytes=_vmem_limit(0, 10 * tq * tk * 4)),
    )(q, k, v, do)


def _tri(lower):
    r = lax.broadcasted_iota(jnp.int32, (CHUNK, CHUNK), 0)
    c = lax.broadcasted_iota(jnp.int32, (CHUNK, CHUNK), 1)
    return jnp.where((c <= r) if lower else (c >= r), 1.0, 0.0).astype(BF16)


def _tri_dot(tri, x):
    return sum(jnp.dot(tri, part, preferred_element_type=F32) for part in _split3(x))


def _gla_fwd(name, q, k, v, la, nh):
    t = q.shape[0]
    dk, dv = q.shape[1] // nh, v.shape[1] // nh
    nc = t // CHUNK

    def body(q_ref, k_ref, v_ref, g_ref, o_ref, st_ref, state):
        @pl.when(pl.program_id(1) == 0)
        def _():
            state[...] = jnp.zeros_like(state)

        g = g_ref[...]
        b = _tri_dot(_tri(True), g)
        b_end = jnp.sum(g, axis=0, keepdims=True)
        k_dec = (k_ref[...] * jnp.exp(b_end - b)).astype(BF16)
        u_t = lax.dot_general(v_ref[...].astype(BF16), k_dec, TN, preferred_element_type=F32)
        new = state[...] * jnp.exp(b_end) + u_t
        state[...] = new
        st_ref[...] = new
        qc = (q_ref[...] * (dk ** -0.5)).astype(BF16)
        o_ref[...] = lax.dot_general(qc, new.astype(BF16), NT, preferred_element_type=F32)

    kblk = pl.BlockSpec((CHUNK, dk), lambda h, n: (n, h))
    vblk = pl.BlockSpec((CHUNK, dv), lambda h, n: (n, h))
    return _chained(
        body, name=name, grid=(nh, nc), in_specs=[kblk, kblk, vblk, kblk],
        out_specs=(vblk, pl.BlockSpec((None, None, dv, dk), lambda h, n: (h, n, 0, 0))),
        out_shape=(jax.ShapeDtypeStruct((t, nh * dv), F32), jax.ShapeDtypeStruct((nh, nc, dv, dk), F32)),
        scratch_shapes=[pltpu.VMEM((dv, dk), F32)],
        compiler_params=_rowwise_params("parallel", "arbitrary"),
    )(q, k, v, la)


def _gla_bwd(name, q, k, v, la, states, do, nh):
    t = q.shape[0]
    dk, dv = q.shape[1] // nh, v.shape[1] // nh
    nc = t // CHUNK
    scale = dk ** -0.5

    def body(q_ref, k_ref, v_ref, g_ref, do_ref, st_ref, sp_ref, dq_ref, dk_ref, dv_ref, dg_ref, carry):
        i = pl.program_id(1)

        @pl.when(i == 0)
        def _():
            carry[...] = jnp.zeros_like(carry)

        g = g_ref[...]
        b = _tri_dot(_tri(True), g)
        b_end = jnp.sum(g, axis=0, keepdims=True)
        w = jnp.exp(b_end - b)
        decay = jnp.exp(b_end)
        k_dec = k_ref[...] * w
        qc = (q_ref[...] * scale).astype(BF16)
        dob = do_ref[...].astype(BF16)
        dq_ref[...] = jnp.dot(dob, st_ref[...].astype(BF16), preferred_element_type=F32) * scale
        g_t = carry[...] + lax.dot_general(dob, qc, TN, preferred_element_type=F32)
        g_tb = g_t.astype(BF16)
        dk_dec = jnp.dot(v_ref[...].astype(BF16), g_tb, preferred_element_type=F32)
        dv_ref[...] = lax.dot_general(k_dec.astype(BF16), g_tb, NT, preferred_element_type=F32)
        prev = jnp.where(i < nc - 1, sp_ref[...], 0.0)
        ddecay = jnp.sum(g_t * prev, axis=0, keepdims=True)
        dk_ref[...] = dk_dec * w
        e = dk_dec * k_dec
        db_end = jnp.sum(e, axis=0, keepdims=True) + ddecay * decay
        dg_ref[...] = _tri_dot(_tri(False), -e) + db_end
        carry[...] = g_t * decay

    kblk = pl.BlockSpec((CHUNK, dk), lambda h, i: (nc - 1 - i, h))
    vblk = pl.BlockSpec((CHUNK, dv), lambda h, i: (nc - 1 - i, h))
    ksds = jax.ShapeDtypeStruct((t, nh * dk), F32)
    return _chained(
        body, name=name, grid=(nh, nc),
        in_specs=[kblk, kblk, vblk, kblk, vblk,
                  pl.BlockSpec((None, None, dv, dk), lambda h, i: (h, nc - 1 - i, 0, 0)),
                  pl.BlockSpec((None, None, dv, dk), lambda h, i: (h, jnp.maximum(nc - 2 - i, 0), 0, 0))],
        out_specs=(kblk, kblk, vblk, kblk),
        out_shape=(ksds, ksds, jax.ShapeDtypeStruct((t, nh * dv), F32), ksds),
        scratch_shapes=[pltpu.VMEM((dv, dk), F32)],
        compiler_params=_rowwise_params("parallel", "arbitrary"),
    )(q, k, v, la, do, states, states)


def _adamw(name, w, m, v, parts):
    rows, cols = w.shape
    tm = _tile(rows, 1 << int(math.log2(max(8, (1 << 17) // (-(-cols // LANES) * LANES)))))

    def body(w_ref, m_ref, v_ref, p_ref, g_ref, d_ref, nm_ref, nv_ref):
        g = p_ref[0].astype(F32)
        for s in range(1, N_DEV):
            g = g + p_ref[s].astype(F32)
        m_new = ADAM_B1 * m_ref[...] + (1.0 - ADAM_B1) * g
        v_new = ADAM_B2 * v_ref[...] + (1.0 - ADAM_B2) * jnp.square(g)
        m_hat = m_new / (1.0 - ADAM_B1 ** ADAM_STEP)
        v_hat = v_new / (1.0 - ADAM_B2 ** ADAM_STEP)
        g_ref[...] = g
        d_ref[...] = -ADAM_LR * (m_hat / (jnp.sqrt(v_hat) + ADAM_EPS) + ADAM_WD * w_ref[...])
        nm_ref[...] = m_new
        nv_ref[...] = v_new

    blk = pl.BlockSpec((tm, cols), lambda i: (i, 0))
    sds = jax.ShapeDtypeStruct((rows, cols), F32)
    return _chained(
        body, name=name, grid=(rows // tm,),
        in_specs=[blk, blk, blk, pl.BlockSpec((N_DEV, tm, cols), lambda i: (0, i, 0))],
        out_specs=(blk, blk, blk, blk), out_shape=(sds, sds, sds, sds),
        compiler_params=_rowwise_params("parallel"),
    )(w, m, v, parts)


HBM = pl.BlockSpec(memory_space=pl.ANY)
MESH = pl.DeviceIdType.MESH


def _all_gather(name, shards):
    n = len(shards)

    def body(*refs):
        ins, outs = refs[:n], refs[n:2 * n]
        send_sems, recv_sems, local_sems = refs[2 * n:]
        x, y, c = lax.axis_index("x"), lax.axis_index("y"), lax.axis_index("c")
        me, sibling = (x, y, c), (x, y, 1 - c)
        chips = [(1 - x, y), (x, 1 - y), (1 - x, 1 - y)]

        def copy(w, k, block, to, src=None):
            dst = outs[w].at[4 * block[0] + 2 * block[1] + block[2]]
            return pltpu.make_async_remote_copy(
                src_ref=dst if src is None else src, dst_ref=dst, send_sem=send_sems.at[7 * w + k],
                recv_sem=recv_sems.at[7 * w + k], device_id=to, device_id_type=MESH)

        mine, first, passed = [], [], []
        for w in range(n):
            cp = pltpu.make_async_copy(ins[w], outs[w].at[4 * x + 2 * y + c], local_sems.at[w])
            cp.start()
            mine.append(cp)
            first.append(copy(w, 0, me, sibling, src=ins[w]))
            first += [copy(w, 1 + j, me, (*chip, c), src=ins[w]) for j, chip in enumerate(chips)]
        for cp in first:
            cp.start()
        for w in range(n):
            for j, chip in enumerate(chips):
                copy(w, 1 + j, (*chip, c), me).wait_recv()
                cp = copy(w, 4 + j, (*chip, c), sibling)
                cp.start()
                passed.append(cp)
        for w in range(n):
            copy(w, 0, sibling, me).wait_recv()
            for j, chip in enumerate(chips):
                copy(w, 4 + j, (*chip, 1 - c), me).wait_recv()
        for cp in first + passed:
            cp.wait_send()
        for cp in mine:
            cp.wait()

    return _chained(
        body, name=name, in_specs=[HBM] * n, out_specs=[HBM] * n,
        out_shape=[jax.ShapeDtypeStruct((N_DEV,) + s.shape, s.dtype) for s in shards],
        scratch_shapes=[pltpu.SemaphoreType.DMA((7 * n,)), pltpu.SemaphoreType.DMA((7 * n,)),
                        pltpu.SemaphoreType.DMA((n,))],
    )(*shards)


HBM_SPEC = pl.BlockSpec(memory_space=pltpu.HBM)
SEM_SPEC = pl.BlockSpec(memory_space=pltpu.SEMAPHORE)
EFFECT = pltpu.SideEffectType.DATAFLOW_SIDE_EFFECTING
TOKEN = jax.ShapeDtypeStruct((8, LANES), F32)


def _in_hbm(a):
    return pltpu.with_memory_space_constraint(a, pltpu.HBM)


def _place():
    x, y, c = lax.axis_index("x"), lax.axis_index("y"), lax.axis_index("c")
    chips = [(1 - x, y), (x, 1 - y), (1 - x, 1 - y)]
    return x, y, c, chips


def _block_of(px, py, pc):
    return 4 * px + 2 * py + pc


def _gather_copies(ins, outs, sems, w):
    send_sems, recv_sems, local_sems = sems
    x, y, c, chips = _place()
    peers = [(x, y, 1 - c)] + [(*chip, c) for chip in chips]

    def copy(k, block, to, src):
        dst = outs[w].at[_block_of(*block)]
        return pltpu.make_async_remote_copy(src_ref=dst if src is None else src, dst_ref=dst, send_sem=send_sems.at[4 * w + k],
                                            recv_sem=recv_sems.at[4 * w + k], device_id=to, device_id_type=MESH)

    local = pltpu.make_async_copy(ins[w], outs[w].at[_block_of(x, y, c)], local_sems.at[w])
    sends = [copy(k, (x, y, c), peer, ins[w]) for k, peer in enumerate(peers)]
    recvs = [copy(k, peer, (x, y, c), None) for k, peer in enumerate(peers)]
    return local, sends, recvs


def _forward_copies(outs, sems, w):
    send_sems, recv_sems = sems
    x, y, c, chips = _place()

    def copy(j, block, to):
        dst = outs[w].at[_block_of(*block)]
        return pltpu.make_async_remote_copy(src_ref=dst, dst_ref=dst, send_sem=send_sems.at[3 * w + j],
                                            recv_sem=recv_sems.at[3 * w + j], device_id=to, device_id_type=MESH)

    sends = [copy(j, (*chip, c), (x, y, 1 - c)) for j, chip in enumerate(chips)]
    recvs = [copy(j, (*chip, 1 - c), (x, y, c)) for j, chip in enumerate(chips)]
    return sends, recvs


def _gather_start(name, shards):
    n = len(shards)

    def body(*refs):
        ins, outs, sems = refs[:n], refs[n:2 * n], refs[2 * n:2 * n + 3]
        for w in range(n):
            local, sends, _ = _gather_copies(ins, outs, sems, w)
            local.start()
            for cp in sends:
                cp.start()
        refs[-1][...] = jnp.zeros_like(refs[-1])

    lands = [lax.empty((N_DEV,) + s.shape, s.dtype) for s in shards]
    res = _chained(
        body, name=name, link=-1, in_specs=[HBM_SPEC] * (2 * n),
        out_specs=[SEM_SPEC] * 3 + [HBM_SPEC] * (2 * n) + [pl.BlockSpec(memory_space=pltpu.VMEM)],
        out_shape=[pltpu.SemaphoreType.DMA((4 * n,)), pltpu.SemaphoreType.DMA((4 * n,)), pltpu.SemaphoreType.DMA((n,))]
        + [pltpu.HBM(s.shape, s.dtype) for s in shards] + [pltpu.HBM(l.shape, l.dtype) for l in lands] + [TOKEN],
        input_output_aliases={i: 3 + i for i in range(2 * n)},
        compiler_params=pltpu.CompilerParams(has_side_effects=EFFECT),
    )(*[_in_hbm(s) for s in shards], *[_in_hbm(l) for l in lands])
    return n, res[:3], res[3:3 + n], res[3 + n:3 + 2 * n]


def _gather_mid(name, state):
    n, sems, shards, lands = state

    def body(*refs):
        ins, outs, sems_in = refs[:n], refs[n:2 * n], refs[2 * n:2 * n + 3]
        sems_out = refs[2 * n + 3:2 * n + 5]
        for w in range(n):
            local, sends, recvs = _gather_copies(ins, outs, sems_in, w)
            local.wait()
            for cp in sends:
                cp.wait_send()
            for cp in recvs:
                cp.wait_recv()
            for cp in _forward_copies(outs, sems_out, w)[0]:
                cp.start()
        refs[-1][...] = jnp.zeros_like(refs[-1])

    res = _chained(
        body, name=name, link=-1, in_specs=[HBM_SPEC] * (2 * n) + [SEM_SPEC] * 3,
        out_specs=[SEM_SPEC] * 2 + [HBM_SPEC] * n + [pl.BlockSpec(memory_space=pltpu.VMEM)],
        out_shape=[pltpu.SemaphoreType.DMA((3 * n,)), pltpu.SemaphoreType.DMA((3 * n,))]
        + [pltpu.HBM(l.shape, l.dtype) for l in lands] + [TOKEN],
        input_output_aliases={n + i: 2 + i for i in range(n)},
        compiler_params=pltpu.CompilerParams(has_side_effects=EFFECT),
    )(*shards, *lands, *sems)
    return n, res[:2], res[2:2 + n]


def _gather_end(name, state):
    n, sems, lands = state

    def body(*refs):
        outs, sems_in = refs[:n], refs[n:n + 2]
        for w in range(n):
            sends, recvs = _forward_copies(outs, sems_in, w)
            for cp in sends:
                cp.wait_send()
            for cp in recvs:
                cp.wait_recv()

    return _chained(
        body, name=name, in_specs=[HBM_SPEC] * n + [SEM_SPEC] * 2, out_specs=[HBM_SPEC] * n,
        out_shape=[pltpu.HBM(l.shape, l.dtype) for l in lands], input_output_aliases={i: i for i in range(n)},
        compiler_params=pltpu.CompilerParams(has_side_effects=EFFECT),
    )(*lands, *sems)


def _exchange_copies(ins, outs, sems, w):
    send_sems, recv_sems, local_sems = sems
    x, y, c, _ = _place()
    mine = _block_of(x, y, c)
    local = pltpu.make_async_copy(ins[w].at[mine], outs[w].at[mine], local_sems.at[w])
    remote = []
    for k in range(1, N_DEV):
        px, py, pc = x ^ (k >> 2), y ^ ((k >> 1) & 1), c ^ (k & 1)
        remote.append(pltpu.make_async_remote_copy(
            src_ref=ins[w].at[_block_of(px, py, pc)], dst_ref=outs[w].at[mine], send_sem=send_sems.at[7 * w + k - 1],
            recv_sem=recv_sems.at[7 * w + k - 1], device_id=(px, py, pc), device_id_type=MESH))
    return local, remote


def _exchange_start(name, stacks):
    n = len(stacks)

    def body(*refs):
        ins, outs, sems = refs[:n], refs[n:2 * n], refs[2 * n:2 * n + 3]
        for w in range(n):
            local, remote = _exchange_copies(ins, outs, sems, w)
            local.start()
            for cp in remote:
                cp.start()
        refs[-1][...] = jnp.zeros_like(refs[-1])

    lands = [lax.empty(s.shape, s.dtype) for s in stacks]
    res = _chained(
        body, name=name, link=-1, in_specs=[HBM_SPEC] * (2 * n),
        out_specs=[SEM_SPEC] * 3 + [HBM_SPEC] * (2 * n) + [pl.BlockSpec(memory_space=pltpu.VMEM)],
        out_shape=[pltpu.SemaphoreType.DMA((7 * n,)), pltpu.SemaphoreType.DMA((7 * n,)), pltpu.SemaphoreType.DMA((n,))]
        + [pltpu.HBM(s.shape, s.dtype) for s in stacks] * 2 + [TOKEN],
        input_output_aliases={i: 3 + i for i in range(2 * n)},
        compiler_params=pltpu.CompilerParams(has_side_effects=EFFECT),
    )(*[_in_hbm(s) for s in stacks], *[_in_hbm(l) for l in lands])
    return n, res[:3], res[3:3 + n], res[3 + n:3 + 2 * n]


def _exchange_wait(name, state):
    n, sems, stacks, lands = state

    def body(*refs):
        ins, outs, sems_in = refs[:n], refs[n:2 * n], refs[2 * n:2 * n + 3]
        for w in range(n):
            local, remote = _exchange_copies(ins, outs, sems_in, w)
            local.wait()
            for cp in remote:
                cp.wait_send()
                cp.wait_recv()

    return _chained(
        body, name=name, in_specs=[HBM_SPEC] * (2 * n) + [SEM_SPEC] * 3, out_specs=[HBM_SPEC] * n,
        out_shape=[pltpu.HBM(l.shape, l.dtype) for l in lands], input_output_aliases={n + i: i for i in range(n)},
        compiler_params=pltpu.CompilerParams(has_side_effects=EFFECT),
    )(*stacks, *lands, *sems)


def _unstack_cols(w):
    return w.transpose(1, 0, 2).reshape(w.shape[1], N_DEV * w.shape[2])


def _stack_cols(w):
    return w.reshape(w.shape[0], N_DEV, w.shape[1] // N_DEV).transpose(1, 0, 2)


def _rope_tables(positions):
    half = MLA_ROPE // 2
    inv_freq = ROPE_THETA ** (-jnp.arange(half, dtype=F32) / half)
    ang = positions.astype(F32)[:, None] * inv_freq
    cos, sin = jnp.cos(ang), jnp.sin(ang)
    t = positions.shape[0]
    cos_t = jnp.concatenate([jnp.ones((t, MLA_NOPE), F32), cos, cos], axis=1)
    sin_t = jnp.concatenate([jnp.zeros((t, MLA_NOPE), F32), -sin, sin], axis=1)
    idx = jnp.arange(MLA_QK)
    partner = jnp.where(idx < MLA_NOPE, -1, jnp.where(idx < MLA_NOPE + half, idx + half, idx - half))
    swap = (idx[:, None] == partner[None, :]).astype(BF16)
    return cos_t, sin_t, swap


def _ffn_fwd(tag, x, gain, wg, wu, wd):
    t = x.shape[0]
    f = wg.shape[2]
    h = _rms_fwd(tag + "_rms", x, gain, BF16)
    g = _mm_stack_out(tag + "_gate", h, wg, BF16)
    u = _mm_stack_out(tag + "_up", h, wu, BF16)
    a = _swiglu_fwd(tag + "_act", g.reshape(N_DEV * t, f), u.reshape(N_DEV * t, f), BF16).reshape(N_DEV, t, f)
    y = _mm_stack_sum(tag + "_down", a, wd, F32, scale=0.5, res=x)
    return y, (x, h, g, u, a)


def _ffn_bwd_weights(tag, dy, saved, wd):
    x, h, g, u, a = saved
    t = x.shape[0]
    f = wd.shape[1]
    da = _mm_stack_nt_out(tag + "_da", dy, wd, BF16, scale=0.5)
    dwd = _mm_stack_tn_right(tag + "_dwd", a, dy, BF16, scale=0.5)
    dg, du = _swiglu_bwd(tag + "_dact", da.reshape(N_DEV * t, f), g.reshape(N_DEV * t, f), u.reshape(N_DEV * t, f), BF16)
    dg, du = dg.reshape(N_DEV, t, f), du.reshape(N_DEV, t, f)
    dwg = _mm_stack_tn_left(tag + "_dwg", h, dg, BF16)
    dwu = _mm_stack_tn_left(tag + "_dwu", h, du, BF16)
    return (dg, du), dwg, dwu, dwd


def _ffn_bwd_input(tag, dy, saved, dgu, gain, wg, wu):
    dg, du = dgu
    dh = _mm_stack_nt_sum(tag + "_dh_g", dg, wg, F32)
    dh = _mm_stack_nt_sum(tag + "_dh_u", du, wu, F32, res=dh)
    return _rms_bwd(tag + "_drms", saved[0], gain, dh, res=dy)


def _local_step(x, mem, positions, sm, comm, target):
    t, d = x.shape
    nm = mem.shape[0]
    gs, gw, wt = {}, {}, {}

    wt.update(comm.weights(0))
    x1, ffn1_saved = _ffn_fwd("ffn1", x, sm['ffn1_norm'], wt['ffn1_w_gate'], wt['ffn1_w_up'], wt['ffn1_w_down'])

    wt.update(comm.weights(1))
    w_in_ref = _unstack_cols(wt['w_in'])
    pieces = []
    for name, n in IN_PAD:
        piece = w_in_ref[:, REF_OFF[name]:REF_OFF[name] + REF_SIZE[name]]
        if n != REF_SIZE[name]:
            piece = jnp.pad(piece, ((0, 0), (0, n - REF_SIZE[name])))
        pieces.append(piece)
    w_in = jnp.concatenate(pieces, axis=1)
    h2 = _rms_fwd("mix_rms", x1, sm['mix_norm'], BF16)
    z = _mm2("mix_in", h2, w_in, NN, F32)
    zs = {name: z[:, PAD_OFF[name]:PAD_OFF[name] + n] for name, n in IN_PAD}

    cq = _rms_fwd("mla_q_a_rms", zs['zq'], sm['q_a_norm'], BF16)
    q_raw = _mm_stack_out("mla_q_up", cq, wt['w_q_up'], F32)
    ckv = _rms_fwd("mla_kv_a_rms", zs['zkv'], sm['kv_a_norm'], BF16)
    kv = _mm_stack_out("mla_kv_up", ckv, wt['w_kv_up'], F32)
    zkr = zs['zkr'][:, :MLA_ROPE]
    k_raw = jnp.concatenate([kv[:, :, :MLA_NOPE], jnp.broadcast_to(zkr[None], (MLA_HEADS, t, MLA_ROPE))], axis=2)
    v_mla = kv[:, :, MLA_NOPE:].astype(BF16)
    cos_t, sin_t, swap = _rope_tables(positions)
    q_raw2, k_raw2 = q_raw.reshape(MLA_HEADS * t, MLA_QK), k_raw.reshape(MLA_HEADS * t, MLA_QK)
    qn = _rms_fwd("mla_q_rms", q_raw2, sm['mla_q_norm'], F32)
    kn = _rms_fwd("mla_k_rms", k_raw2, sm['mla_k_norm'], F32)
    qf = _rope("mla_q_rope", qn, cos_t, sin_t, swap, BF16, False).reshape(MLA_HEADS, t, MLA_QK)
    kf = _rope("mla_k_rope", kn, cos_t, sin_t, swap, BF16, False).reshape(MLA_HEADS, t, MLA_QK)
    o_mla = _attn_fwd("mla_attn", qf, kf, v_mla, MLA_QK ** -0.5, True)

    w_g2 = jnp.pad(_unstack_cols(wt['gla_w_gate2']), ((0, LANES - GLA_GATE_RANK), (0, 0)))
    pre = _mm2("gla_gate_pre", zs['zg'], w_g2, NN, F32)
    log_a = _gate_fwd("gla_gate", pre, sm['gla_b_gate'])
    o_gla_raw, states = _gla_fwd("gla_scan", zs['gq'], zs['gk'], zs['gv'], log_a, GLA_HEADS)
    o_gla_n = _rms_fwd("gla_out_rms", o_gla_raw, sm['gla_out_norm'], F32)
    o_gla = _swiglu_fwd("gla_out_gate", zs['zr'], o_gla_n, BF16)

    cat = jnp.concatenate([o_mla, o_gla], axis=1)
    w_out = wt['w_out'].reshape(d, d)
    x2 = _mm2("mix_out", cat, w_out, NN, F32, res=x1)

    wt.update(comm.weights(2))
    w_mq, w_mk, w_mv = (wt[n].reshape(d, MEM_HEADS * MEM_HEAD_DIM) for n in ('mem_w_q', 'mem_w_k', 'mem_w_v'))
    hq = _rms_fwd("mem_attn_rms", x2, sm['mem_attn_norm'], BF16)
    hm = _rms_fwd("mem_rms", mem, sm['mem_norm'], BF16)

    def heads_out(name, a, b, out_dtype):
        m, kk = a.shape
        tm = _tile(m, 512)
        return _mm(name, a, b, (m // tm, MEM_HEADS, 1), ((tm, kk), lambda i, h, k: (i, 0)),
                   ((kk, MEM_HEAD_DIM), lambda i, h, k: (0, h)), ((None, tm, MEM_HEAD_DIM), lambda i, h, k: (h, i, 0)),
                   (MEM_HEADS, m, MEM_HEAD_DIM), out_dtype, NN)

    mq_raw = heads_out("mem_q", hq, w_mq, F32)
    mk_raw = heads_out("mem_k", hm, w_mk, F32)
    mv = heads_out("mem_v", hm, w_mv, BF16)
    mq = _rms_fwd("mem_q_rms", mq_raw.reshape(MEM_HEADS * t, MEM_HEAD_DIM), sm['mem_q_norm'], BF16)
    mk = _rms_fwd("mem_k_rms", mk_raw.reshape(MEM_HEADS * nm, MEM_HEAD_DIM), sm['mem_k_norm'], BF16)
    mq, mk = mq.reshape(MEM_HEADS, t, MEM_HEAD_DIM), mk.reshape(MEM_HEADS, nm, MEM_HEAD_DIM)
    o_mem = _attn_fwd("mem_attn", mq, mk, mv, MEM_HEAD_DIM ** -0.5, False)
    w_mo = wt['mem_w_o']
    mo_cols = w_mo.shape[2]
    tm = _tile(t, 512)
    x3 = _mm("mem_out", o_mem, w_mo, (t // tm, N_DEV, 1), ((tm, o_mem.shape[1]), lambda i, j, k: (i, 0)),
             ((None, o_mem.shape[1], mo_cols), lambda i, j, k: (j, 0, 0)), ((tm, mo_cols), lambda i, j, k: (i, j)),
             (t, d), F32, NN, res=x2)

    wt.update(comm.weights(3))
    y, ffn2_saved = _ffn_fwd("ffn2", x3, sm['ffn2_norm'], wt['ffn2_w_gate'], wt['ffn2_w_up'], wt['ffn2_w_down'])
    dy, loss_lanes = _loss("loss", y, target)

    dgu, gw['ffn2_w_gate'], gw['ffn2_w_up'], gw['ffn2_w_down'] = _ffn_bwd_weights("ffn2", dy, ffn2_saved, wt['ffn2_w_down'])
    comm.grads(3, gw)
    dx3, gs['ffn2_norm'] = _ffn_bwd_input("ffn2", dy, ffn2_saved, dgu, sm['ffn2_norm'],
                                          wt['ffn2_w_gate'], wt['ffn2_w_up'])

    do_mem = _mm("mem_do", dx3, w_mo, (t // tm, MEM_HEADS, N_DEV), ((tm, mo_cols), lambda i, h, k: (i, k)),
                 ((None, MEM_HEAD_DIM, mo_cols), lambda i, h, k: (k, h, 0)), ((tm, MEM_HEAD_DIM), lambda i, h, k: (i, h)),
                 (t, MEM_HEADS * MEM_HEAD_DIM), BF16, NT)
    tk = _tile(t, 512)
    gw['mem_w_o'] = _mm("mem_dwo", o_mem, dx3, (N_DEV, 1, t // tk), ((tk, o_mem.shape[1]), lambda j, i, k: (k, 0)),
                        ((tk, mo_cols), lambda j, i, k: (k, j)), ((None, o_mem.shape[1], mo_cols), lambda j, i, k: (j, 0, 0)),
                        w_mo.shape, BF16, TN)
    dmq, dmk, dmv = _attn_bwd("mem_dattn", mq, mk, mv, do_mem, MEM_HEAD_DIM ** -0.5, False)
    dmq_raw, gs['mem_q_norm'] = _rms_bwd("mem_q_drms", mq_raw.reshape(MEM_HEADS * t, MEM_HEAD_DIM), sm['mem_q_norm'],
                                         dmq.reshape(MEM_HEADS * t, MEM_HEAD_DIM))
    dmk_raw, gs['mem_k_norm'] = _rms_bwd("mem_k_drms", mk_raw.reshape(MEM_HEADS * nm, MEM_HEAD_DIM), sm['mem_k_norm'],
                                         dmk.reshape(MEM_HEADS * nm, MEM_HEAD_DIM))
    dmq_raw = dmq_raw.reshape(MEM_HEADS, t, MEM_HEAD_DIM)
    dmk_raw = dmk_raw.reshape(MEM_HEADS, nm, MEM_HEAD_DIM)

    def heads_in_nt(name, a, b, res=None):
        m, n = a.shape[1], b.shape[0]
        tm_, tn_ = _tile(m, 512), _tile(n, 1024)
        return _mm(name, a, b, (m // tm_, n // tn_, MEM_HEADS), ((None, tm_, MEM_HEAD_DIM), lambda i, j, k: (k, i, 0)),
                   ((tn_, MEM_HEAD_DIM), lambda i, j, k: (j, k)), ((tm_, tn_), lambda i, j, k: (i, j)), (m, n), F32, NT,
                   None, res)

    def heads_tn(name, a, b):
        m, kp = a.shape
        tm_, tk_ = _tile(kp, 1024), _tile(m, 512)
        return _mm(name, a, b, (kp // tm_, MEM_HEADS, m // tk_), ((tk_, tm_), lambda i, h, k: (k, i)),
                   ((None, tk_, MEM_HEAD_DIM), lambda i, h, k: (h, k, 0)), ((tm_, MEM_HEAD_DIM), lambda i, h, k: (i, h)),
                   (kp, MEM_HEADS * MEM_HEAD_DIM), BF16, TN)

    dhq = heads_in_nt("mem_dhq", dmq_raw, w_mq)
    gw['mem_w_q'] = heads_tn("mem_dwq", hq, dmq_raw).reshape(wt['mem_w_q'].shape)
    dhm = heads_in_nt("mem_dhm_k", dmk_raw, w_mk)
    dhm = heads_in_nt("mem_dhm_v", dmv, w_mv, res=dhm)
    gw['mem_w_k'] = heads_tn("mem_dwk", hm, dmk_raw).reshape(wt['mem_w_k'].shape)
    gw['mem_w_v'] = heads_tn("mem_dwv", hm, dmv).reshape(wt['mem_w_v'].shape)
    _, gs['mem_norm'] = _rms_bwd("mem_drms", mem, sm['mem_norm'], dhm)
    comm.grads(2, gw)
    dx2, gs['mem_attn_norm'] = _rms_bwd("mem_attn_drms", x2, sm['mem_attn_norm'], dhq, res=dx3)

    dcat = _mm2("mix_dcat", dx2, w_out, NT, F32)
    gw['w_out'] = _mm2("mix_dwout", cat, dx2, TN, BF16, tm=1024, tn=2048, tk=512).reshape(wt['w_out'].shape)
    do_mla, do_gla = dcat[:, :MLA_HEADS * MLA_V], dcat[:, MLA_HEADS * MLA_V:]

    dzr, dgn = _swiglu_bwd("gla_out_dgate", do_gla, zs['zr'], o_gla_n, F32)
    do_gla_raw, gs['gla_out_norm'] = _rms_bwd("gla_out_drms", o_gla_raw, sm['gla_out_norm'], dgn)
    dgq, dgk, dgv, dlog_a = _gla_bwd("gla_dscan", zs['gq'], zs['gk'], zs['gv'], log_a, states, do_gla_raw, GLA_HEADS)
    dpre, gs['gla_b_gate'] = _gate_bwd("gla_dgate", pre, sm['gla_b_gate'], dlog_a)
    dw_g2 = _mm2("gla_dwgate", zs['zg'], dpre, TN, BF16, tk=512)
    gw['gla_w_gate2'] = _stack_cols(dw_g2[:GLA_GATE_RANK])
    dzg = _mm2("gla_dzg", dpre, w_g2, NT, F32)

    dqf, dkf, dv_mla = _attn_bwd("mla_dattn", qf, kf, v_mla, do_mla, MLA_QK ** -0.5, True)
    dqn = _rope("mla_q_drope", dqf.reshape(MLA_HEADS * t, MLA_QK), cos_t, sin_t, swap, F32, True)
    dkn = _rope("mla_k_drope", dkf.reshape(MLA_HEADS * t, MLA_QK), cos_t, sin_t, swap, F32, True)
    dq_raw, gs['mla_q_norm'] = _rms_bwd("mla_q_drms", q_raw2, sm['mla_q_norm'], dqn)
    dk_raw, gs['mla_k_norm'] = _rms_bwd("mla_k_drms", k_raw2, sm['mla_k_norm'], dkn)
    dq_raw = dq_raw.reshape(MLA_HEADS, t, MLA_QK)
    dk_raw = dk_raw.reshape(MLA_HEADS, t, MLA_QK)
    dkv = jnp.concatenate([dk_raw[:, :, :MLA_NOPE], dv_mla], axis=2)
    dzkr = jnp.sum(dk_raw[:, :, MLA_NOPE:], axis=0)
    gw['w_q_up'] = _mm_stack_tn_left("mla_dwq", cq, dq_raw, BF16)
    gw['w_kv_up'] = _mm_stack_tn_left("mla_dwkv", ckv, dkv, BF16)
    dcq = _mm_stack_nt_sum("mla_dcq", dq_raw, wt['w_q_up'], F32)
    dckv = _mm_stack_nt_sum("mla_dckv", dkv, wt['w_kv_up'], F32)
    dzq, gs['q_a_norm'] = _rms_bwd("mla_q_a_drms", zs['zq'], sm['q_a_norm'], dcq)
    dzkv, gs['kv_a_norm'] = _rms_bwd("mla_kv_a_drms", zs['zkv'], sm['kv_a_norm'], dckv)

    dzs = {'zq': dzq, 'zkv': dzkv, 'gq': dgq, 'gk': dgk, 'gv': dgv, 'zr': dzr,
           'zkr': jnp.pad(dzkr, ((0, 0), (0, LANES - MLA_ROPE))), 'zg': dzg}
    dz = jnp.concatenate([dzs[name].astype(BF16) for name, _ in IN_PAD], axis=1)
    dw_in = _mm2("mix_dwin", h2, dz, TN, BF16, tm=1024, tn=2048, tk=512)
    dw_in_ref = jnp.concatenate([dw_in[:, PAD_OFF[name]:PAD_OFF[name] + n] for name, n in IN_REF], axis=1)
    gw['w_in'] = _stack_cols(dw_in_ref)
    comm.grads(1, gw)
    dh2 = _mm2("mix_dh", dz, w_in, NT, F32)
    dx1, gs['mix_norm'] = _rms_bwd("mix_drms", x1, sm['mix_norm'], dh2, res=dx2)

    dgu, gw['ffn1_w_gate'], gw['ffn1_w_up'], gw['ffn1_w_down'] = _ffn_bwd_weights("ffn1", dx1, ffn1_saved, wt['ffn1_w_down'])
    comm.grads(0, gw)
    grad_x, gs['ffn1_norm'] = _ffn_bwd_input("ffn1", dx1, ffn1_saved, dgu, sm['ffn1_norm'],
                                             wt['ffn1_w_gate'], wt['ffn1_w_up'])
    return loss_lanes, grad_x, gs


def _pad_lanes(v):
    n = v.shape[1]
    return jnp.pad(v, ((0, 0), (0, -n % LANES)))


def _pack_small(vals):
    return jnp.concatenate([_pad_lanes(vals[n]) for n in SMALL], axis=1)


def _unpack_small(packed, like):
    out, off = {}, 0
    for n in SMALL:
        size = like[n].shape[1]
        out[n] = packed[:, off:off + size]
        off += size + (-size % LANES)
    return out


def kernel(x, mem, positions, ffn1_norm, ffn1_w_gate, ffn1_w_up, ffn1_w_down, mix_norm, w_in, q_a_norm, w_q_up, kv_a_norm, w_kv_up, mla_q_norm, mla_k_norm, gla_w_gate2, gla_b_gate, gla_out_norm, w_out, mem_attn_norm, mem_norm, mem_w_q, mem_w_k, mem_w_v, mem_w_o, mem_q_norm, mem_k_norm, ffn2_norm, ffn2_w_gate, ffn2_w_up, ffn2_w_down, loss_target, m_ffn1_norm, m_ffn1_w_gate, m_ffn1_w_up, m_ffn1_w_down, m_mix_norm, m_w_in, m_q_a_norm, m_w_q_up, m_kv_a_norm, m_w_kv_up, m_mla_q_norm, m_mla_k_norm, m_gla_w_gate2, m_gla_b_gate, m_gla_out_norm, m_w_out, m_mem_attn_norm, m_mem_norm, m_mem_w_q, m_mem_w_k, m_mem_w_v, m_mem_w_o, m_mem_q_norm, m_mem_k_norm, m_ffn2_norm, m_ffn2_w_gate, m_ffn2_w_up, m_ffn2_w_down, v_ffn1_norm, v_ffn1_w_gate, v_ffn1_w_up, v_ffn1_w_down, v_mix_norm, v_w_in, v_q_a_norm, v_w_q_up, v_kv_a_norm, v_w_kv_up, v_mla_q_norm, v_mla_k_norm, v_gla_w_gate2, v_gla_b_gate, v_gla_out_norm, v_w_out, v_mem_attn_norm, v_mem_norm, v_mem_w_q, v_mem_w_k, v_mem_w_v, v_mem_w_o, v_mem_q_norm, v_mem_k_norm, v_ffn2_norm, v_ffn2_w_gate, v_ffn2_w_up, v_ffn2_w_down):
    inp = dict(locals())
    x, mem, positions, target = inp['x'][0], inp['mem'][0], inp['positions'][0], inp['loss_target'][0]
    sm = {n: inp[n] for n in SMALL}
    out = {}

    class Comm:
        def __init__(self):
            self.shards = [[inp[n][0].astype(BF16) for n in group] for group in GROUPS]
            self.forwards, self.exchanges = {}, {}
            first = _gather_start("gather_start_0", self.shards[0])
            self.forwards[0] = _gather_mid("gather_mid_0", first)
            self.gathers = {1: _gather_start("gather_start_1", self.shards[1])}

        def weights(self, gi):
            if gi not in self.forwards:
                self.forwards[gi] = _gather_mid(f"gather_mid_{gi}", self.gathers[gi])
            if gi == 1:
                self.gathers.update({g: _gather_start(f"gather_start_{g}", self.shards[g]) for g in (2, 3)})
            if gi == 2:
                self.forwards[3] = _gather_mid("gather_mid_3", self.gathers[3])
            return dict(zip(GROUPS[gi], _gather_end(f"gather_end_{gi}", self.forwards[gi])))

        def grads(self, gi, stacks):
            self.exchanges[gi] = _exchange_start(f"exchange_start_{gi}", [stacks[n] for n in GROUPS[gi]])

        def update(self, gi):
            parts = _exchange_wait(f"exchange_wait_{gi}", self.exchanges[gi])
            for n, p in zip(GROUPS[gi], parts):
                res = _adamw("adamw_" + n, inp[n][0], inp['m_' + n][0], inp['v_' + n][0], p)
                for kind, r in zip(('grad_', 'delta_', 'new_m_', 'new_v_'), res):
                    out[kind + n] = r[None]

    _Chain.last = None
    comm = Comm()
    loss_lanes, grad_x, gs = _local_step(x, mem, positions, sm, comm, target)
    out['loss'] = lax.psum(jnp.sum(loss_lanes), ("x", "y", "c"))
    out['grad_x'] = grad_x[None]

    small_parts = _all_gather("gather_small", [_pack_small(gs)])[0]
    res = _adamw("adamw_small", _pack_small(sm), _pack_small({n: inp['m_' + n] for n in SMALL}),
                 _pack_small({n: inp['v_' + n] for n in SMALL}), small_parts)
    for kind, r in zip(('grad_', 'delta_', 'new_m_', 'new_v_'), res):
        for n, val in _unpack_small(r, sm).items():
            out[kind + n] = val

    for gi in (3, 2, 1, 0):
        comm.update(gi)

    names = ['loss', 'grad_x'] + [k + n for k in ('grad_', 'delta_', 'new_m_', 'new_v_') for n in WEIGHTS]
    return tuple(out[n] for n in names)
```

```python
import math

import jax
import jax.numpy as jnp
from jax import lax
from jax.experimental import pallas as pl
from jax.experimental.pallas import tpu as pltpu

F32 = jnp.float32
BF16 = jnp.bfloat16

N_DEV = 8
EPS = 1e-6
CHUNK = 64
MLA_HEADS, MLA_NOPE, MLA_ROPE, MLA_V = 8, 128, 64, 128
MLA_QK = MLA_NOPE + MLA_ROPE
MLA_Q_RANK, MLA_KV_RANK = 512, 256
ROPE_THETA = 10000.0
GLA_HEADS, GLA_DK, GLA_DV, GLA_GATE_RANK = 4, 128, 256, 16
GLA_TAU = 16.0
MEM_HEADS, MEM_HEAD_DIM = 4, 128
ADAM_LR, ADAM_B1, ADAM_B2, ADAM_EPS, ADAM_WD, ADAM_STEP = 0.001, 0.9, 0.999, 1e-08, 0.01, 10

V7X_VMEM_BYTES = 64 * 1024 * 1024
LANES = 128

NN = (((1,), (0,)), ((), ()))
NT = (((1,), (1,)), ((), ()))
TN = (((0,), (0,)), ((), ()))

WEIGHTS = ['ffn1_norm', 'ffn1_w_gate', 'ffn1_w_up', 'ffn1_w_down', 'mix_norm', 'w_in', 'q_a_norm', 'w_q_up',
           'kv_a_norm', 'w_kv_up', 'mla_q_norm', 'mla_k_norm', 'gla_w_gate2', 'gla_b_gate', 'gla_out_norm', 'w_out',
           'mem_attn_norm', 'mem_norm', 'mem_w_q', 'mem_w_k', 'mem_w_v', 'mem_w_o', 'mem_q_norm', 'mem_k_norm',
           'ffn2_norm', 'ffn2_w_gate', 'ffn2_w_up', 'ffn2_w_down']
SMALL = ['ffn1_norm', 'mix_norm', 'q_a_norm', 'kv_a_norm', 'mla_q_norm', 'mla_k_norm', 'gla_b_gate', 'gla_out_norm',
         'mem_attn_norm', 'mem_norm', 'mem_q_norm', 'mem_k_norm', 'ffn2_norm']
MIXER = ['w_in', 'w_q_up', 'w_kv_up', 'gla_w_gate2', 'w_out']
MEMORY = ['mem_w_q', 'mem_w_k', 'mem_w_v', 'mem_w_o']
GATHERS = [['ffn1_w_gate'], ['ffn1_w_up'], ['ffn1_w_down'], MIXER, MEMORY, ['ffn2_w_gate', 'ffn2_w_up', 'ffn2_w_down']]
NEXT_GATHERS = {0: [1], 1: [2], 2: [3], 3: [4, 5]}
EARLY_FORWARD = {4: 5}

IN_REF = [('zq', 512), ('zkv', 256), ('zkr', 64), ('gq', 512), ('gk', 512), ('gv', 1024), ('zg', 16), ('zr', 1024)]
IN_PAD = [('zq', 512), ('zkv', 256), ('gq', 512), ('gk', 512), ('gv', 1024), ('zr', 1024), ('zkr', 128), ('zg', 128)]
IN_WIDTH = sum(n for _, n in IN_REF)
IN_PAD_WIDTH = sum(n for _, n in IN_PAD)


def _offsets(layout):
    out, off = {}, 0
    for name, n in layout:
        out[name] = off
        off += n
    return out


REF_OFF, PAD_OFF = _offsets(IN_REF), _offsets(IN_PAD)
REF_SIZE = dict(IN_REF)


def _tile(n, pref):
    return pref if n % pref == 0 else n


def _block_bytes(blk, dtype):
    dims = [d for d in blk if d is not None]
    if len(dims) >= 1:
        dims[-1] = -(-dims[-1] // LANES) * LANES
    return math.prod(dims) * jnp.dtype(dtype).itemsize


def _vmem_limit(pipelined_bytes, resident_bytes=0):
    need = 2 * pipelined_bytes + resident_bytes + (8 << 20)
    return int(min(max(need, 32 << 20), V7X_VMEM_BYTES - (6 << 20)))


class _Chain:
    last = None


def _chained(body, *, in_specs, link=0, **kwargs):
    def call(*operands):
        dep = _Chain.last
        if dep is None:
            res = pl.pallas_call(body, in_specs=in_specs, **kwargs)(*operands)
        else:
            n = len(operands)

            def chained_body(*refs):
                body(*refs[:n], *refs[n + 1:])

            res = pl.pallas_call(chained_body, in_specs=list(in_specs) + [pl.BlockSpec(memory_space=pl.ANY)],
                                 **kwargs)(*operands, dep)
        _Chain.last = res[link] if isinstance(res, (list, tuple)) else res
        return res

    return call


def _rowwise_params(*semantics):
    return pltpu.CompilerParams(dimension_semantics=semantics, vmem_limit_bytes=48 << 20)


def _mm(name, a, b, grid, a_spec, b_spec, o_spec, out_shape, out_dtype, dims, scale=None, res=None):
    nk = grid[2]
    (a_blk, a_map), (b_blk, b_map), (o_blk, o_map) = a_spec, b_spec, o_spec
    acc_shape = tuple(d for d in o_blk if d is not None)

    def body(*refs):
        if res is None:
            a_ref, b_ref, o_ref = refs[:3]
            r_ref, rest = None, refs[3:]
        else:
            a_ref, b_ref, r_ref, o_ref = refs[:4]
            rest = refs[4:]

        def product():
            return lax.dot_general(a_ref[...].astype(BF16), b_ref[...].astype(BF16), dims, preferred_element_type=F32)

        def finish(r):
            if scale is not None:
                r = r * scale
            if r_ref is not None:
                r = r + r_ref[...].astype(F32)
            o_ref[...] = r.astype(o_ref.dtype)

        if nk == 1:
            finish(product())
        else:
            acc = rest[0]
            k = pl.program_id(2)

            @pl.when(k == 0)
            def _():
                acc[...] = product()

            @pl.when(k > 0)
            def _():
                acc[...] += product()

            @pl.when(k == nk - 1)
            def _():
                finish(acc[...])

    in_specs = [pl.BlockSpec(a_blk, a_map), pl.BlockSpec(b_blk, b_map)]
    operands = [a, b]
    piped = _block_bytes(a_blk, a.dtype) + _block_bytes(b_blk, b.dtype) + _block_bytes(o_blk, out_dtype)
    if res is not None:
        in_specs.append(pl.BlockSpec(o_blk, o_map))
        operands.append(res)
        piped += _block_bytes(o_blk, res.dtype)
    scratch = [pltpu.VMEM(acc_shape, F32)] if nk > 1 else []
    return _chained(
        body, name=name, grid=grid, in_specs=in_specs, out_specs=pl.BlockSpec(o_blk, o_map),
        out_shape=jax.ShapeDtypeStruct(out_shape, out_dtype), scratch_shapes=scratch,
        compiler_params=pltpu.CompilerParams(
            dimension_semantics=("parallel", "parallel", "arbitrary"),
            vmem_limit_bytes=_vmem_limit(piped, 3 * _block_bytes(acc_shape, F32))),
    )(*operands)


def _mm2(name, a, b, dims, out_dtype, tm=512, tn=1024, tk=2048, scale=None, res=None):
    if dims is NN:
        (m, kk), n = a.shape, b.shape[1]
    elif dims is NT:
        (m, kk), n = a.shape, b.shape[0]
    else:
        (kk, m), n = a.shape, b.shape[1]
    tm, tn, tk = _tile(m, tm), _tile(n, tn), _tile(kk, tk)
    a_spec = ((tk, tm), lambda i, j, k: (k, i)) if dims is TN else ((tm, tk), lambda i, j, k: (i, k))
    b_spec = ((tn, tk), lambda i, j, k: (j, k)) if dims is NT else ((tk, tn), lambda i, j, k: (k, j))
    return _mm(name, a, b, (m // tm, n // tn, kk // tk), a_spec, b_spec, ((tm, tn), lambda i, j, k: (i, j)),
               (m, n), out_dtype, dims, scale, res)


def _mm_stack_out(name, a, b, out_dtype, tm=512, tk=2048):
    (m, kk), (nj, _, n) = a.shape, b.shape
    tm, tk = _tile(m, tm), _tile(kk, tk)
    return _mm(name, a, b, (nj, m // tm, kk // tk), ((tm, tk), lambda j, i, k: (i, k)),
               ((None, tk, n), lambda j, i, k: (j, k, 0)), ((None, tm, n), lambda j, i, k: (j, i, 0)),
               (nj, m, n), out_dtype, NN)


def _mm_stack_nt_out(name, a, b, out_dtype, scale=None, tm=512, tk=2048):
    (m, kk), (nj, n, _) = a.shape, b.shape
    tm, tk = _tile(m, tm), _tile(kk, tk)
    return _mm(name, a, b, (nj, m // tm, kk // tk), ((tm, tk), lambda j, i, k: (i, k)),
               ((None, n, tk), lambda j, i, k: (j, 0, k)), ((None, tm, n), lambda j, i, k: (j, i, 0)),
               (nj, m, n), out_dtype, NT, scale)


def _mm_stack_sum(name, a, b, out_dtype, scale=None, res=None, tm=512, tn=1024):
    (nj, m, f), n = a.shape, b.shape[2]
    tm, tn = _tile(m, tm), _tile(n, tn)
    return _mm(name, a, b, (m // tm, n // tn, nj), ((None, tm, f), lambda i, j, k: (k, i, 0)),
               ((None, f, tn), lambda i, j, k: (k, 0, j)), ((tm, tn), lambda i, j, k: (i, j)),
               (m, n), out_dtype, NN, scale, res)


def _mm_stack_nt_sum(name, a, b, out_dtype, res=None, tm=512, tn=1024):
    (nj, m, f), n = a.shape, b.shape[1]
    tm, tn = _tile(m, tm), _tile(n, tn)
    return _mm(name, a, b, (m // tm, n // tn, nj), ((None, tm, f), lambda i, j, k: (k, i, 0)),
               ((None, tn, f), lambda i, j, k: (k, j, 0)), ((tm, tn), lambda i, j, k: (i, j)),
               (m, n), out_dtype, NT, None, res)


def _mm_stack_tn_left(name, a, b, out_dtype, tm=1024, tk=512):
    (m, kp), (nj, _, n) = a.shape, b.shape
    tm, tk = _tile(kp, tm), _tile(m, tk)
    return _mm(name, a, b, (nj, kp // tm, m // tk), ((tk, tm), lambda j, i, k: (k, i)),
               ((None, tk, n), lambda j, i, k: (j, k, 0)), ((None, tm, n), lambda j, i, k: (j, i, 0)),
               (nj, kp, n), out_dtype, TN)


def _mm_stack_tn_right(name, a, b, out_dtype, scale=None, tn=2048, tk=512):
    (nj, m, f), n = a.shape, b.shape[1]
    tn, tk = _tile(n, tn), _tile(m, tk)
    return _mm(name, a, b, (nj, n // tn, m // tk), ((None, tk, f), lambda j, i, k: (j, k, 0)),
               ((tk, tn), lambda j, i, k: (k, i)), ((None, f, tn), lambda j, i, k: (j, 0, i)),
               (nj, f, n), out_dtype, TN, scale)


def _rms_fwd(name, x, g, out_dtype, tm=256):
    rows, cols = x.shape
    d = g.shape[1]
    tm = _tile(rows, tm)

    def body(x_ref, g_ref, o_ref):
        xf = x_ref[...].astype(F32)
        r = lax.rsqrt(jnp.mean(xf * xf, axis=-1, keepdims=True) + EPS)
        o_ref[...] = (xf * r * g_ref[...]).astype(o_ref.dtype)

    return _chained(
        body, name=name, grid=(rows // tm, cols // d),
        in_specs=[pl.BlockSpec((tm, d), lambda i, c: (i, c)), pl.BlockSpec((1, d), lambda i, c: (0, 0))],
        out_specs=pl.BlockSpec((tm, d), lambda i, c: (i, c)),
        out_shape=jax.ShapeDtypeStruct((rows, cols), out_dtype),
        compiler_params=_rowwise_params("parallel", "parallel"),
    )(x, g)


def _rms_bwd(name, x, g, dy, res=None, tm=256):
    rows, cols = x.shape
    d = g.shape[1]
    tm = _tile(rows, tm)

    def body(*refs):
        if res is None:
            x_ref, g_ref, dy_ref, dx_ref, dg_ref = refs
            r_ref = None
        else:
            x_ref, g_ref, dy_ref, r_ref, dx_ref, dg_ref = refs
        xf = x_ref[...].astype(F32)
        r = lax.rsqrt(jnp.mean(xf * xf, axis=-1, keepdims=True) + EPS)
        xhat = xf * r
        dyf = dy_ref[...].astype(F32)

        @pl.when((pl.program_id(0) == 0) & (pl.program_id(1) == 0))
        def _():
            dg_ref[...] = jnp.zeros_like(dg_ref)

        dg_ref[...] += jnp.sum(dyf * xhat, axis=0, keepdims=True)
        dxh = dyf * g_ref[...]
        dx = r * (dxh - xhat * jnp.mean(dxh * xhat, axis=-1, keepdims=True))
        if r_ref is not None:
            dx = dx + r_ref[...].astype(F32)
        dx_ref[...] = dx

    blk = pl.BlockSpec((tm, d), lambda i, c: (i, c))
    in_specs = [blk, pl.BlockSpec((1, d), lambda i, c: (0, 0)), blk]
    operands = [x, g, dy]
    if res is not None:
        in_specs.append(blk)
        operands.append(res)
    return _chained(
        body, name=name, grid=(rows // tm, cols // d), in_specs=in_specs,
        out_specs=(blk, pl.BlockSpec((1, d), lambda i, c: (0, 0))),
        out_shape=(jax.ShapeDtypeStruct((rows, cols), F32), jax.ShapeDtypeStruct((1, d), F32)),
        compiler_params=_rowwise_params("arbitrary", "arbitrary"),
    )(*operands)


def _swiglu_fwd(name, g, u, out_dtype, tm=256):
    rows, cols = g.shape
    tm = _tile(rows, tm)

    def body(g_ref, u_ref, o_ref):
        gf = g_ref[...].astype(F32)
        o_ref[...] = (gf * jax.nn.sigmoid(gf) * u_ref[...].astype(F32)).astype(o_ref.dtype)

    blk = pl.BlockSpec((tm, cols), lambda i: (i, 0))
    return _chained(
        body, name=name, grid=(rows // tm,), in_specs=[blk, blk], out_specs=blk,
        out_shape=jax.ShapeDtypeStruct((rows, cols), out_dtype),
        compiler_params=_rowwise_params("parallel"),
    )(g, u)


def _swiglu_bwd(name, da, g, u, out_dtype, tm=256):
    rows, cols = g.shape
    tm = _tile(rows, tm)

    def body(da_ref, g_ref, u_ref, dg_ref, du_ref):
        gf = g_ref[...].astype(F32)
        daf = da_ref[...].astype(F32)
        sig = jax.nn.sigmoid(gf)
        du_ref[...] = (daf * gf * sig).astype(du_ref.dtype)
        dg_ref[...] = (daf * u_ref[...].astype(F32) * sig * (1.0 + gf * (1.0 - sig))).astype(dg_ref.dtype)

    blk = pl.BlockSpec((tm, cols), lambda i: (i, 0))
    sds = jax.ShapeDtypeStruct((rows, cols), out_dtype)
    return _chained(
        body, name=name, grid=(rows // tm,), in_specs=[blk, blk, blk], out_specs=(blk, blk), out_shape=(sds, sds),
        compiler_params=_rowwise_params("parallel"),
    )(da, g, u)


def _split3(x):
    hi = x.astype(BF16)
    r1 = x - hi.astype(F32)
    mid = r1.astype(BF16)
    lo = (r1 - mid.astype(F32)).astype(BF16)
    return hi, mid, lo


def _rope(name, x, cos_t, sin_t, swap, out_dtype, backward, tm=256):
    rows, d = x.shape
    t = cos_t.shape[0]
    tm = _tile(t, tm)
    nt = t // tm

    def body(x_ref, c_ref, s_ref, p_ref, o_ref):
        xf = x_ref[...].astype(F32)
        p = p_ref[...]
        lhs = xf * s_ref[...] if backward else xf
        sw = sum(jnp.dot(part, p, preferred_element_type=F32) for part in _split3(lhs))
        if not backward:
            sw = sw * s_ref[...]
        o_ref[...] = (xf * c_ref[...] + sw).astype(o_ref.dtype)

    blk = pl.BlockSpec((tm, d), lambda i: (i, 0))
    tab = pl.BlockSpec((tm, d), lambda i: (i % nt, 0))
    return _chained(
        body, name=name, grid=(rows // tm,), in_specs=[blk, tab, tab, pl.BlockSpec((d, d), lambda i: (0, 0))],
        out_specs=blk, out_shape=jax.ShapeDtypeStruct((rows, d), out_dtype),
        compiler_params=_rowwise_params("parallel"),
    )(x, cos_t, sin_t, swap)


def _gate_fwd(name, pre, bias, tm=256):
    rows, cols = pre.shape
    tm = _tile(rows, tm)

    def body(p_ref, b_ref, o_ref):
        z = p_ref[...] + b_ref[...]
        o_ref[...] = (jnp.minimum(z, 0.0) - jnp.log(1.0 + jnp.exp(-jnp.abs(z)))) * (1.0 / GLA_TAU)

    blk = pl.BlockSpec((tm, cols), lambda i: (i, 0))
    return _chained(
        body, name=name, grid=(rows // tm,), in_specs=[blk, pl.BlockSpec((1, cols), lambda i: (0, 0))], out_specs=blk,
        out_shape=jax.ShapeDtypeStruct((rows, cols), F32),
        compiler_params=_rowwise_params("parallel"),
    )(pre, bias)


def _gate_bwd(name, pre, bias, dla, tm=256):
    rows, cols = pre.shape
    tm = _tile(rows, tm)

    def body(p_ref, b_ref, d_ref, dp_ref, db_ref):
        z = p_ref[...] + b_ref[...]
        dp = d_ref[...] * (1.0 / GLA_TAU) / (1.0 + jnp.exp(z))
        dp_ref[...] = dp

        @pl.when(pl.program_id(0) == 0)
        def _():
            db_ref[...] = jnp.zeros_like(db_ref)

        db_ref[...] += jnp.sum(dp, axis=0, keepdims=True)

    blk = pl.BlockSpec((tm, cols), lambda i: (i, 0))
    row = pl.BlockSpec((1, cols), lambda i: (0, 0))
    return _chained(
        body, name=name, grid=(rows // tm,), in_specs=[blk, row, blk], out_specs=(blk, row),
        out_shape=(jax.ShapeDtypeStruct((rows, cols), F32), jax.ShapeDtypeStruct((1, cols), F32)),
        compiler_params=_rowwise_params("arbitrary"),
    )(pre, bias, dla)


def _loss(name, y, target, tm=256):
    rows, d = y.shape
    tm = _tile(rows, tm)

    def body(y_ref, t_ref, dy_ref, l_ref):
        err = y_ref[...] - t_ref[...]
        dy_ref[...] = err * (1.0 / d)

        @pl.when(pl.program_id(0) == 0)
        def _():
            l_ref[...] = jnp.zeros_like(l_ref)

        sq = (err * err).reshape(tm // 8, 8, d)
        l_ref[...] += jnp.sum(sq, axis=0) * (0.5 / d)

    blk = pl.BlockSpec((tm, d), lambda i: (i, 0))
    return _chained(
        body, name=name, grid=(rows // tm,), in_specs=[blk, blk],
        out_specs=(blk, pl.BlockSpec((8, d), lambda i: (0, 0))),
        out_shape=(jax.ShapeDtypeStruct((rows, d), F32), jax.ShapeDtypeStruct((8, d), F32)),
        compiler_params=_rowwise_params("arbitrary"),
    )(y, target)


def _scores(q, k, scale, causal, q0):
    s = lax.dot_general(q, k, NT, preferred_element_type=F32) * scale
    if causal:
        qc = (q0 + lax.broadcasted_iota(jnp.int32, s.shape, 0)) // CHUNK
        kc = lax.broadcasted_iota(jnp.int32, s.shape, 1) // CHUNK
        s = jnp.where(kc <= qc, s, -1e30)
    e = jnp.exp(s - jnp.max(s, axis=-1, keepdims=True))
    return e, jnp.sum(e, axis=-1, keepdims=True)


def _attn_fwd(name, q, k, v, scale, causal, tq=256):
    nh, t, dk = q.shape
    tk, dv = k.shape[1], v.shape[2]
    tq = _tile(t, tq)

    def body(q_ref, k_ref, v_ref, o_ref):
        e, l = _scores(q_ref[...], k_ref[...], scale, causal, pl.program_id(1) * tq)
        o = jnp.dot(e.astype(BF16), v_ref[...], preferred_element_type=F32)
        o_ref[...] = (o / l).astype(o_ref.dtype)

    return _chained(
        body, name=name, grid=(nh, t // tq),
        in_specs=[pl.BlockSpec((None, tq, dk), lambda h, i: (h, i, 0)), pl.BlockSpec((None, tk, dk), lambda h, i: (h, 0, 0)),
                  pl.BlockSpec((None, tk, dv), lambda h, i: (h, 0, 0))],
        out_specs=pl.BlockSpec((tq, dv), lambda h, i: (i, h)),
        out_shape=jax.ShapeDtypeStruct((t, nh * dv), BF16),
        compiler_params=pltpu.CompilerParams(dimension_semantics=("parallel", "parallel"),
                                             vmem_limit_bytes=_vmem_limit(0, 6 * tq * tk * 4)),
    )(q, k, v)


def _attn_bwd(name, q, k, v, do, scale, causal, tq=256):
    nh, t, dk = q.shape
    tk, dv = k.shape[1], v.shape[2]
    tq = _tile(t, tq)

    def body(q_ref, k_ref, v_ref, do_ref, dq_ref, dk_ref, dv_ref):
        qb, kb = q_ref[...], k_ref[...]
        e, l = _scores(qb, kb, scale, causal, pl.program_id(1) * tq)
        p = e / l
        dob = do_ref[...].astype(BF16)
        dp = lax.dot_general(dob, v_ref[...], NT, preferred_element_type=F32)
        ds = (p * (dp - jnp.sum(p * dp, axis=-1, keepdims=True)) * scale).astype(BF16)
        dq_ref[...] = jnp.dot(ds, kb, preferred_element_type=F32)

        @pl.when(pl.program_id(1) == 0)
        def _():
            dk_ref[...] = jnp.zeros_like(dk_ref)
            dv_ref[...] = jnp.zeros_like(dv_ref)

        dk_ref[...] += lax.dot_general(ds, qb, TN, preferred_element_type=F32)
        dv_ref[...] += lax.dot_general(p.astype(BF16), dob, TN, preferred_element_type=F32)

    return _chained(
        body, name=name, grid=(nh, t // tq),
        in_specs=[pl.BlockSpec((None, tq, dk), lambda h, i: (h, i, 0)), pl.BlockSpec((None, tk, dk), lambda h, i: (h, 0, 0)),
                  pl.BlockSpec((None, tk, dv), lambda h, i: (h, 0, 0)), pl.BlockSpec((tq, dv), lambda h, i: (i, h))],
        out_specs=(pl.BlockSpec((None, tq, dk), lambda h, i: (h, i, 0)), pl.BlockSpec((None, tk, dk), lambda h, i: (h, 0, 0)),
                   pl.BlockSpec((None, tk, dv), lambda h, i: (h, 0, 0))),
        out_shape=(jax.ShapeDtypeStruct((nh, t, dk), F32), jax.ShapeDtypeStruct((nh, tk, dk), F32),
                   jax.ShapeDtypeStruct((nh, tk, dv), F32)),
        compiler_params=pltpu.CompilerParams(dimension_semantics=("parallel", "arbitrary"),
                                             vmem_limit_bytes=_vmem_limit(0, 10 * tq * tk * 4)),
    )(q, k, v, do)


def _tri(lower):
    r = lax.broadcasted_iota(jnp.int32, (CHUNK, CHUNK), 0)
    c = lax.broadcasted_iota(jnp.int32, (CHUNK, CHUNK), 1)
    return jnp.where((c <= r) if lower else (c >= r), 1.0, 0.0).astype(BF16)


def _tri_dot(tri, x):
    return sum(jnp.dot(tri, part, preferred_element_type=F32) for part in _split3(x))


def _gla_fwd(name, q, k, v, la, nh):
    t = q.shape[0]
    dk, dv = q.shape[1] // nh, v.shape[1] // nh
    nc = t // CHUNK

    def body(q_ref, k_ref, v_ref, g_ref, o_ref, st_ref, state):
        @pl.when(pl.program_id(1) == 0)
        def _():
            state[...] = jnp.zeros_like(state)

        g = g_ref[...]
        b = _tri_dot(_tri(True), g)
        b_end = jnp.sum(g, axis=0, keepdims=True)
        k_dec = (k_ref[...] * jnp.exp(b_end - b)).astype(BF16)
        u_t = lax.dot_general(v_ref[...].astype(BF16), k_dec, TN, preferred_element_type=F32)
        new = state[...] * jnp.exp(b_end) + u_t
        state[...] = new
        st_ref[...] = new
        qc = (q_ref[...] * (dk ** -0.5)).astype(BF16)
        o_ref[...] = lax.dot_general(qc, new.astype(BF16), NT, preferred_element_type=F32)

    kblk = pl.BlockSpec((CHUNK, dk), lambda h, n: (n, h))
    vblk = pl.BlockSpec((CHUNK, dv), lambda h, n: (n, h))
    return _chained(
        body, name=name, grid=(nh, nc), in_specs=[kblk, kblk, vblk, kblk],
        out_specs=(vblk, pl.BlockSpec((None, None, dv, dk), lambda h, n: (h, n, 0, 0))),
        out_shape=(jax.ShapeDtypeStruct((t, nh * dv), F32), jax.ShapeDtypeStruct((nh, nc, dv, dk), F32)),
        scratch_shapes=[pltpu.VMEM((dv, dk), F32)],
        compiler_params=_rowwise_params("parallel", "arbitrary"),
    )(q, k, v, la)


def _gla_bwd(name, q, k, v, la, states, do, nh):
    t = q.shape[0]
    dk, dv = q.shape[1] // nh, v.shape[1] // nh
    nc = t // CHUNK
    scale = dk ** -0.5

    def body(q_ref, k_ref, v_ref, g_ref, do_ref, st_ref, sp_ref, dq_ref, dk_ref, dv_ref, dg_ref, carry):
        i = pl.program_id(1)

        @pl.when(i == 0)
        def _():
            carry[...] = jnp.zeros_like(carry)

        g = g_ref[...]
        b = _tri_dot(_tri(True), g)
        b_end = jnp.sum(g, axis=0, keepdims=True)
        w = jnp.exp(b_end - b)
        decay = jnp.exp(b_end)
        k_dec = k_ref[...] * w
        qc = (q_ref[...] * scale).astype(BF16)
        dob = do_ref[...].astype(BF16)
        dq_ref[...] = jnp.dot(dob, st_ref[...].astype(BF16), preferred_element_type=F32) * scale
        g_t = carry[...] + lax.dot_general(dob, qc, TN, preferred_element_type=F32)
        g_tb = g_t.astype(BF16)
        dk_dec = jnp.dot(v_ref[...].astype(BF16), g_tb, preferred_element_type=F32)
        dv_ref[...] = lax.dot_general(k_dec.astype(BF16), g_tb, NT, preferred_element_type=F32)
        prev = jnp.where(i < nc - 1, sp_ref[...], 0.0)
        ddecay = jnp.sum(g_t * prev, axis=0, keepdims=True)
        dk_ref[...] = dk_dec * w
        e = dk_dec * k_dec
        db_end = jnp.sum(e, axis=0, keepdims=True) + ddecay * decay
        dg_ref[...] = _tri_dot(_tri(False), -e) + db_end
        carry[...] = g_t * decay

    kblk = pl.BlockSpec((CHUNK, dk), lambda h, i: (nc - 1 - i, h))
    vblk = pl.BlockSpec((CHUNK, dv), lambda h, i: (nc - 1 - i, h))
    ksds = jax.ShapeDtypeStruct((t, nh * dk), F32)
    return _chained(
        body, name=name, grid=(nh, nc),
        in_specs=[kblk, kblk, vblk, kblk, vblk,
                  pl.BlockSpec((None, None, dv, dk), lambda h, i: (h, nc - 1 - i, 0, 0)),
                  pl.BlockSpec((None, None, dv, dk), lambda h, i: (h, jnp.maximum(nc - 2 - i, 0), 0, 0))],
        out_specs=(kblk, kblk, vblk, kblk),
        out_shape=(ksds, ksds, jax.ShapeDtypeStruct((t, nh * dv), F32), ksds),
        scratch_shapes=[pltpu.VMEM((dv, dk), F32)],
        compiler_params=_rowwise_params("parallel", "arbitrary"),
    )(q, k, v, la, do, states, states)


def _adamw(name, w, m, v, parts):
    rows, cols = w.shape
    tm = _tile(rows, 1 << int(math.log2(max(8, (1 << 17) // (-(-cols // LANES) * LANES)))))

    def body(w_ref, m_ref, v_ref, p_ref, g_ref, d_ref, nm_ref, nv_ref):
        g = p_ref[0].astype(F32)
        for s in range(1, N_DEV):
            g = g + p_ref[s].astype(F32)
        m_new = ADAM_B1 * m_ref[...] + (1.0 - ADAM_B1) * g
        v_new = ADAM_B2 * v_ref[...] + (1.0 - ADAM_B2) * jnp.square(g)
        m_hat = m_new / (1.0 - ADAM_B1 ** ADAM_STEP)
        v_hat = v_new / (1.0 - ADAM_B2 ** ADAM_STEP)
        g_ref[...] = g
        d_ref[...] = -ADAM_LR * (m_hat / (jnp.sqrt(v_hat) + ADAM_EPS) + ADAM_WD * w_ref[...])
        nm_ref[...] = m_new
        nv_ref[...] = v_new

    blk = pl.BlockSpec((tm, cols), lambda i: (i, 0))
    sds = jax.ShapeDtypeStruct((rows, cols), F32)
    return _chained(
        body, name=name, grid=(rows // tm,),
        in_specs=[blk, blk, blk, pl.BlockSpec((N_DEV, tm, cols), lambda i: (0, i, 0))],
        out_specs=(blk, blk, blk, blk), out_shape=(sds, sds, sds, sds),
        compiler_params=_rowwise_params("parallel"),
    )(w, m, v, parts)


HBM = pl.BlockSpec(memory_space=pl.ANY)
MESH = pl.DeviceIdType.MESH


def _all_gather(name, shards):
    n = len(shards)

    def body(*refs):
        ins, outs = refs[:n], refs[n:2 * n]
        send_sems, recv_sems, local_sems = refs[2 * n:]
        x, y, c = lax.axis_index("x"), lax.axis_index("y"), lax.axis_index("c")
        me, sibling = (x, y, c), (x, y, 1 - c)
        chips = [(1 - x, y), (x, 1 - y), (1 - x, 1 - y)]

        def copy(w, k, block, to, src=None):
            dst = outs[w].at[4 * block[0] + 2 * block[1] + block[2]]
            return pltpu.make_async_remote_copy(
                src_ref=dst if src is None else src, dst_ref=dst, send_sem=send_sems.at[7 * w + k],
                recv_sem=recv_sems.at[7 * w + k], device_id=to, device_id_type=MESH)

        mine, first, passed = [], [], []
        for w in range(n):
            cp = pltpu.make_async_copy(ins[w], outs[w].at[4 * x + 2 * y + c], local_sems.at[w])
            cp.start()
            mine.append(cp)
            first.append(copy(w, 0, me, sibling, src=ins[w]))
            first += [copy(w, 1 + j, me, (*chip, c), src=ins[w]) for j, chip in enumerate(chips)]
        for cp in first:
            cp.start()
        for w in range(n):
            for j, chip in enumerate(chips):
                copy(w, 1 + j, (*chip, c), me).wait_recv()
                cp = copy(w, 4 + j, (*chip, c), sibling)
                cp.start()
                passed.append(cp)
        for w in range(n):
            copy(w, 0, sibling, me).wait_recv()
            for j, chip in enumerate(chips):
                copy(w, 4 + j, (*chip, 1 - c), me).wait_recv()
        for cp in first + passed:
            cp.wait_send()
        for cp in mine:
            cp.wait()

    return _chained(
        body, name=name, in_specs=[HBM] * n, out_specs=[HBM] * n,
        out_shape=[jax.ShapeDtypeStruct((N_DEV,) + s.shape, s.dtype) for s in shards],
        scratch_shapes=[pltpu.SemaphoreType.DMA((7 * n,)), pltpu.SemaphoreType.DMA((7 * n,)),
                        pltpu.SemaphoreType.DMA((n,))],
    )(*shards)


HBM_SPEC = pl.BlockSpec(memory_space=pltpu.HBM)
SEM_SPEC = pl.BlockSpec(memory_space=pltpu.SEMAPHORE)
EFFECT = pltpu.SideEffectType.DATAFLOW_SIDE_EFFECTING
TOKEN = jax.ShapeDtypeStruct((8, LANES), F32)


def _in_hbm(a):
    return pltpu.with_memory_space_constraint(a, pltpu.HBM)


def _place():
    x, y, c = lax.axis_index("x"), lax.axis_index("y"), lax.axis_index("c")
    chips = [(1 - x, y), (x, 1 - y), (1 - x, 1 - y)]
    return x, y, c, chips


def _block_of(px, py, pc):
    return 4 * px + 2 * py + pc


def _gather_copies(ins, outs, sems, w):
    send_sems, recv_sems, local_sems = sems
    x, y, c, chips = _place()
    peers = [(x, y, 1 - c)] + [(*chip, c) for chip in chips]

    def copy(k, block, to, src):
        dst = outs[w].at[_block_of(*block)]
        return pltpu.make_async_remote_copy(src_ref=dst if src is None else src, dst_ref=dst, send_sem=send_sems.at[4 * w + k],
                                            recv_sem=recv_sems.at[4 * w + k], device_id=to, device_id_type=MESH)

    local = pltpu.make_async_copy(ins[w], outs[w].at[_block_of(x, y, c)], local_sems.at[w])
    sends = [copy(k, (x, y, c), peer, ins[w]) for k, peer in enumerate(peers)]
    recvs = [copy(k, peer, (x, y, c), None) for k, peer in enumerate(peers)]
    return local, sends, recvs


def _forward_copies(outs, sems, w):
    send_sems, recv_sems = sems
    x, y, c, chips = _place()

    def copy(j, block, to):
        dst = outs[w].at[_block_of(*block)]
        return pltpu.make_async_remote_copy(src_ref=dst, dst_ref=dst, send_sem=send_sems.at[3 * w + j],
                                            recv_sem=recv_sems.at[3 * w + j], device_id=to, device_id_type=MESH)

    sends = [copy(j, (*chip, c), (x, y, 1 - c)) for j, chip in enumerate(chips)]
    recvs = [copy(j, (*chip, 1 - c), (x, y, c)) for j, chip in enumerate(chips)]
    return sends, recvs


def _gather_start(name, shards):
    n = len(shards)

    def body(*refs):
        ins, outs, sems = refs[:n], refs[n:2 * n], refs[2 * n:2 * n + 3]
        for w in range(n):
            local, sends, _ = _gather_copies(ins, outs, sems, w)
            local.start()
            for cp in sends:
                cp.start()
        refs[-1][...] = jnp.zeros_like(refs[-1])

    lands = [lax.empty((N_DEV,) + s.shape, s.dtype) for s in shards]
    res = _chained(
        body, name=name, link=-1, in_specs=[HBM_SPEC] * (2 * n),
        out_specs=[SEM_SPEC] * 3 + [HBM_SPEC] * (2 * n) + [pl.BlockSpec(memory_space=pltpu.VMEM)],
        out_shape=[pltpu.SemaphoreType.DMA((4 * n,)), pltpu.SemaphoreType.DMA((4 * n,)), pltpu.SemaphoreType.DMA((n,))]
        + [pltpu.HBM(s.shape, s.dtype) for s in shards] + [pltpu.HBM(l.shape, l.dtype) for l in lands] + [TOKEN],
        input_output_aliases={i: 3 + i for i in range(2 * n)},
        compiler_params=pltpu.CompilerParams(has_side_effects=EFFECT),
    )(*[_in_hbm(s) for s in shards], *[_in_hbm(l) for l in lands])
    return n, res[:3], res[3:3 + n], res[3 + n:3 + 2 * n]


def _gather_mid(name, state):
    n, sems, shards, lands = state

    def body(*refs):
        ins, outs, sems_in = refs[:n], refs[n:2 * n], refs[2 * n:2 * n + 3]
        sems_out = refs[2 * n + 3:2 * n + 5]
        for w in range(n):
            local, sends, recvs = _gather_copies(ins, outs, sems_in, w)
            local.wait()
            for cp in sends:
                cp.wait_send()
            for cp in recvs:
                cp.wait_recv()
            for cp in _forward_copies(outs, sems_out, w)[0]:
                cp.start()
        refs[-1][...] = jnp.zeros_like(refs[-1])

    res = _chained(
        body, name=name, link=-1, in_specs=[HBM_SPEC] * (2 * n) + [SEM_SPEC] * 3,
        out_specs=[SEM_SPEC] * 2 + [HBM_SPEC] * n + [pl.BlockSpec(memory_space=pltpu.VMEM)],
        out_shape=[pltpu.SemaphoreType.DMA((3 * n,)), pltpu.SemaphoreType.DMA((3 * n,))]
        + [pltpu.HBM(l.shape, l.dtype) for l in lands] + [TOKEN],
        input_output_aliases={n + i: 2 + i for i in range(n)},
        compiler_params=pltpu.CompilerParams(has_side_effects=EFFECT),
    )(*shards, *lands, *sems)
    return n, res[:2], res[2:2 + n]


def _gather_end(name, state):
    n, sems, lands = state

    def body(*refs):
        outs, sems_in = refs[:n], refs[n:n + 2]
        for w in range(n):
            sends, recvs = _forward_copies(outs, sems_in, w)
            for cp in sends:
                cp.wait_send()
            for cp in recvs:
                cp.wait_recv()

    return _chained(
        body, name=name, in_specs=[HBM_SPEC] * n + [SEM_SPEC] * 2, out_specs=[HBM_SPEC] * n,
        out_shape=[pltpu.HBM(l.shape, l.dtype) for l in lands], input_output_aliases={i: i for i in range(n)},
        compiler_params=pltpu.CompilerParams(has_side_effects=EFFECT),
    )(*lands, *sems)


def _exchange_copies(ins, outs, sems, w):
    send_sems, recv_sems, local_sems = sems
    x, y, c, _ = _place()
    mine = _block_of(x, y, c)
    local = pltpu.make_async_copy(ins[w].at[mine], outs[w].at[mine], local_sems.at[w])
    remote = []
    for k in range(1, N_DEV):
        px, py, pc = x ^ (k >> 2), y ^ ((k >> 1) & 1), c ^ (k & 1)
        remote.append(pltpu.make_async_remote_copy(
            src_ref=ins[w].at[_block_of(px, py, pc)], dst_ref=outs[w].at[mine], send_sem=send_sems.at[7 * w + k - 1],
            recv_sem=recv_sems.at[7 * w + k - 1], device_id=(px, py, pc), device_id_type=MESH))
    return local, remote


def _exchange_start(name, stacks):
    n = len(stacks)

    def body(*refs):
        ins, outs, sems = refs[:n], refs[n:2 * n], refs[2 * n:2 * n + 3]
        for w in range(n):
            local, remote = _exchange_copies(ins, outs, sems, w)
            local.start()
            for cp in remote:
                cp.start()
        refs[-1][...] = jnp.zeros_like(refs[-1])

    lands = [lax.empty(s.shape, s.dtype) for s in stacks]
    res = _chained(
        body, name=name, link=-1, in_specs=[HBM_SPEC] * (2 * n),
        out_specs=[SEM_SPEC] * 3 + [HBM_SPEC] * (2 * n) + [pl.BlockSpec(memory_space=pltpu.VMEM)],
        out_shape=[pltpu.SemaphoreType.DMA((7 * n,)), pltpu.SemaphoreType.DMA((7 * n,)), pltpu.SemaphoreType.DMA((n,))]
        + [pltpu.HBM(s.shape, s.dtype) for s in stacks] * 2 + [TOKEN],
        input_output_aliases={i: 3 + i for i in range(2 * n)},
        compiler_params=pltpu.CompilerParams(has_side_effects=EFFECT),
    )(*[_in_hbm(s) for s in stacks], *[_in_hbm(l) for l in lands])
    return n, res[:3], res[3:3 + n], res[3 + n:3 + 2 * n]


def _exchange_wait(name, state):
    n, sems, stacks, lands = state

    def body(*refs):
        ins, outs, sems_in = refs[:n], refs[n:2 * n], refs[2 * n:2 * n + 3]
        for w in range(n):
            local, remote = _exchange_copies(ins, outs, sems_in, w)
            local.wait()
            for cp in remote:
                cp.wait_send()
                cp.wait_recv()

    return _chained(
        body, name=name, in_specs=[HBM_SPEC] * (2 * n) + [SEM_SPEC] * 3, out_specs=[HBM_SPEC] * n,
        out_shape=[pltpu.HBM(l.shape, l.dtype) for l in lands], input_output_aliases={n + i: i for i in range(n)},
        compiler_params=pltpu.CompilerParams(has_side_effects=EFFECT),
    )(*stacks, *lands, *sems)


def _unstack_cols(w):
    return w.transpose(1, 0, 2).reshape(w.shape[1], N_DEV * w.shape[2])


def _stack_cols(w):
    return w.reshape(w.shape[0], N_DEV, w.shape[1] // N_DEV).transpose(1, 0, 2)


def _rope_tables(positions):
    half = MLA_ROPE // 2
    inv_freq = ROPE_THETA ** (-jnp.arange(half, dtype=F32) / half)
    ang = positions.astype(F32)[:, None] * inv_freq
    cos, sin = jnp.cos(ang), jnp.sin(ang)
    t = positions.shape[0]
    cos_t = jnp.concatenate([jnp.ones((t, MLA_NOPE), F32), cos, cos], axis=1)
    sin_t = jnp.concatenate([jnp.zeros((t, MLA_NOPE), F32), -sin, sin], axis=1)
    idx = jnp.arange(MLA_QK)
    partner = jnp.where(idx < MLA_NOPE, -1, jnp.where(idx < MLA_NOPE + half, idx + half, idx - half))
    swap = (idx[:, None] == partner[None, :]).astype(BF16)
    return cos_t, sin_t, swap


def _ffn_fwd(tag, x, gain, wt):
    t = x.shape[0]
    h = _rms_fwd(tag + "_rms", x, gain, BF16)
    g = _mm_stack_out(tag + "_gate", h, wt[tag + '_w_gate'], BF16)
    u = _mm_stack_out(tag + "_up", h, wt[tag + '_w_up'], BF16)
    f = g.shape[2]
    a = _swiglu_fwd(tag + "_act", g.reshape(N_DEV * t, f), u.reshape(N_DEV * t, f), BF16).reshape(N_DEV, t, f)
    y = _mm_stack_sum(tag + "_down", a, wt[tag + '_w_down'], F32, scale=0.5, res=x)
    return y, (x, h, g, u, a)


def _ffn_bwd_weights(tag, dy, saved, wd, comm):
    x, h, g, u, a = saved
    t = x.shape[0]
    f = wd.shape[1]
    da = _mm_stack_nt_out(tag + "_da", dy, wd, BF16, scale=0.5)
    comm.grads({tag + '_w_down': _mm_stack_tn_right(tag + "_dwd", a, dy, BF16, scale=0.5)})
    dg, du = _swiglu_bwd(tag + "_dact", da.reshape(N_DEV * t, f), g.reshape(N_DEV * t, f), u.reshape(N_DEV * t, f), BF16)
    dg, du = dg.reshape(N_DEV, t, f), du.reshape(N_DEV, t, f)
    comm.grads({tag + '_w_gate': _mm_stack_tn_left(tag + "_dwg", h, dg, BF16)})
    comm.grads({tag + '_w_up': _mm_stack_tn_left(tag + "_dwu", h, du, BF16)})
    return dg, du


def _ffn_bwd_input(tag, dy, saved, dgu, gain, wg, wu):
    dg, du = dgu
    dh = _mm_stack_nt_sum(tag + "_dh_g", dg, wg, F32)
    dh = _mm_stack_nt_sum(tag + "_dh_u", du, wu, F32, res=dh)
    return _rms_bwd(tag + "_drms", saved[0], gain, dh, res=dy)


def _local_step(x, mem, positions, sm, comm, target):
    t, d = x.shape
    nm = mem.shape[0]
    gs, gw = {}, {}

    class Weights(dict):
        def __missing__(self, name):
            self.update(comm.weights(next(k for k, names in enumerate(GATHERS) if name in names)))
            return self[name]

    wt = Weights()

    x1, ffn1_saved = _ffn_fwd("ffn1", x, sm['ffn1_norm'], wt)

    w_in_ref = _unstack_cols(wt['w_in'])
    pieces = []
    for name, n in IN_PAD:
        piece = w_in_ref[:, REF_OFF[name]:REF_OFF[name] + REF_SIZE[name]]
        if n != REF_SIZE[name]:
            piece = jnp.pad(piece, ((0, 0), (0, n - REF_SIZE[name])))
        pieces.append(piece)
    w_in = jnp.concatenate(pieces, axis=1)
    h2 = _rms_fwd("mix_rms", x1, sm['mix_norm'], BF16)
    z = _mm2("mix_in", h2, w_in, NN, F32)
    zs = {name: z[:, PAD_OFF[name]:PAD_OFF[name] + n] for name, n in IN_PAD}

    cq = _rms_fwd("mla_q_a_rms", zs['zq'], sm['q_a_norm'], BF16)
    q_raw = _mm_stack_out("mla_q_up", cq, wt['w_q_up'], F32)
    ckv = _rms_fwd("mla_kv_a_rms", zs['zkv'], sm['kv_a_norm'], BF16)
    kv = _mm_stack_out("mla_kv_up", ckv, wt['w_kv_up'], F32)
    zkr = zs['zkr'][:, :MLA_ROPE]
    k_raw = jnp.concatenate([kv[:, :, :MLA_NOPE], jnp.broadcast_to(zkr[None], (MLA_HEADS, t, MLA_ROPE))], axis=2)
    v_mla = kv[:, :, MLA_NOPE:].astype(BF16)
    cos_t, sin_t, swap = _rope_tables(positions)
    q_raw2, k_raw2 = q_raw.reshape(MLA_HEADS * t, MLA_QK), k_raw.reshape(MLA_HEADS * t, MLA_QK)
    qn = _rms_fwd("mla_q_rms", q_raw2, sm['mla_q_norm'], F32)
    kn = _rms_fwd("mla_k_rms", k_raw2, sm['mla_k_norm'], F32)
    qf = _rope("mla_q_rope", qn, cos_t, sin_t, swap, BF16, False).reshape(MLA_HEADS, t, MLA_QK)
    kf = _rope("mla_k_rope", kn, cos_t, sin_t, swap, BF16, False).reshape(MLA_HEADS, t, MLA_QK)
    o_mla = _attn_fwd("mla_attn", qf, kf, v_mla, MLA_QK ** -0.5, True)

    w_g2 = jnp.pad(_unstack_cols(wt['gla_w_gate2']), ((0, LANES - GLA_GATE_RANK), (0, 0)))
    pre = _mm2("gla_gate_pre", zs['zg'], w_g2, NN, F32)
    log_a = _gate_fwd("gla_gate", pre, sm['gla_b_gate'])
    o_gla_raw, states = _gla_fwd("gla_scan", zs['gq'], zs['gk'], zs['gv'], log_a, GLA_HEADS)
    o_gla_n = _rms_fwd("gla_out_rms", o_gla_raw, sm['gla_out_norm'], F32)
    o_gla = _swiglu_fwd("gla_out_gate", zs['zr'], o_gla_n, BF16)

    cat = jnp.concatenate([o_mla, o_gla], axis=1)
    w_out = wt['w_out'].reshape(d, d)
    x2 = _mm2("mix_out", cat, w_out, NN, F32, res=x1)

    w_mq, w_mk, w_mv = (wt[n].reshape(d, MEM_HEADS * MEM_HEAD_DIM) for n in ('mem_w_q', 'mem_w_k', 'mem_w_v'))
    hq = _rms_fwd("mem_attn_rms", x2, sm['mem_attn_norm'], BF16)
    hm = _rms_fwd("mem_rms", mem, sm['mem_norm'], BF16)

    def heads_out(name, a, b, out_dtype):
        m, kk = a.shape
        tm = _tile(m, 512)
        return _mm(name, a, b, (m // tm, MEM_HEADS, 1), ((tm, kk), lambda i, h, k: (i, 0)),
                   ((kk, MEM_HEAD_DIM), lambda i, h, k: (0, h)), ((None, tm, MEM_HEAD_DIM), lambda i, h, k: (h, i, 0)),
                   (MEM_HEADS, m, MEM_HEAD_DIM), out_dtype, NN)

    mq_raw = heads_out("mem_q", hq, w_mq, F32)
    mk_raw = heads_out("mem_k", hm, w_mk, F32)
    mv = heads_out("mem_v", hm, w_mv, BF16)
    mq = _rms_fwd("mem_q_rms", mq_raw.reshape(MEM_HEADS * t, MEM_HEAD_DIM), sm['mem_q_norm'], BF16)
    mk = _rms_fwd("mem_k_rms", mk_raw.reshape(MEM_HEADS * nm, MEM_HEAD_DIM), sm['mem_k_norm'], BF16)
    mq, mk = mq.reshape(MEM_HEADS, t, MEM_HEAD_DIM), mk.reshape(MEM_HEADS, nm, MEM_HEAD_DIM)
    o_mem = _attn_fwd("mem_attn", mq, mk, mv, MEM_HEAD_DIM ** -0.5, False)
    w_mo = wt['mem_w_o']
    mo_cols = w_mo.shape[2]
    tm = _tile(t, 512)
    x3 = _mm("mem_out", o_mem, w_mo, (t // tm, N_DEV, 1), ((tm, o_mem.shape[1]), lambda i, j, k: (i, 0)),
             ((None, o_mem.shape[1], mo_cols), lambda i, j, k: (j, 0, 0)), ((tm, mo_cols), lambda i, j, k: (i, j)),
             (t, d), F32, NN, res=x2)

    y, ffn2_saved = _ffn_fwd("ffn2", x3, sm['ffn2_norm'], wt)
    dy, loss_lanes = _loss("loss", y, target)

    dgu = _ffn_bwd_weights("ffn2", dy, ffn2_saved, wt['ffn2_w_down'], comm)
    dx3, gs['ffn2_norm'] = _ffn_bwd_input("ffn2", dy, ffn2_saved, dgu, sm['ffn2_norm'],
                                          wt['ffn2_w_gate'], wt['ffn2_w_up'])

    do_mem = _mm("mem_do", dx3, w_mo, (t // tm, MEM_HEADS, N_DEV), ((tm, mo_cols), lambda i, h, k: (i, k)),
                 ((None, MEM_HEAD_DIM, mo_cols), lambda i, h, k: (k, h, 0)), ((tm, MEM_HEAD_DIM), lambda i, h, k: (i, h)),
                 (t, MEM_HEADS * MEM_HEAD_DIM), BF16, NT)
    tk = _tile(t, 512)
    gw['mem_w_o'] = _mm("mem_dwo", o_mem, dx3, (N_DEV, 1, t // tk), ((tk, o_mem.shape[1]), lambda j, i, k: (k, 0)),
                        ((tk, mo_cols), lambda j, i, k: (k, j)), ((None, o_mem.shape[1], mo_cols), lambda j, i, k: (j, 0, 0)),
                        w_mo.shape, BF16, TN)
    dmq, dmk, dmv = _attn_bwd("mem_dattn", mq, mk, mv, do_mem, MEM_HEAD_DIM ** -0.5, False)
    dmq_raw, gs['mem_q_norm'] = _rms_bwd("mem_q_drms", mq_raw.reshape(MEM_HEADS * t, MEM_HEAD_DIM), sm['mem_q_norm'],
                                         dmq.reshape(MEM_HEADS * t, MEM_HEAD_DIM))
    dmk_raw, gs['mem_k_norm'] = _rms_bwd("mem_k_drms", mk_raw.reshape(MEM_HEADS * nm, MEM_HEAD_DIM), sm['mem_k_norm'],
                                         dmk.reshape(MEM_HEADS * nm, MEM_HEAD_DIM))
    dmq_raw = dmq_raw.reshape(MEM_HEADS, t, MEM_HEAD_DIM)
    dmk_raw = dmk_raw.reshape(MEM_HEADS, nm, MEM_HEAD_DIM)

    def heads_in_nt(name, a, b, res=None):
        m, n = a.shape[1], b.shape[0]
        tm_, tn_ = _tile(m, 512), _tile(n, 1024)
        return _mm(name, a, b, (m // tm_, n // tn_, MEM_HEADS), ((None, tm_, MEM_HEAD_DIM), lambda i, j, k: (k, i, 0)),
                   ((tn_, MEM_HEAD_DIM), lambda i, j, k: (j, k)), ((tm_, tn_), lambda i, j, k: (i, j)), (m, n), F32, NT,
                   None, res)

    def heads_tn(name, a, b):
        m, kp = a.shape
        tm_, tk_ = _tile(kp, 1024), _tile(m, 512)
        return _mm(name, a, b, (kp // tm_, MEM_HEADS, m // tk_), ((tk_, tm_), lambda i, h, k: (k, i)),
                   ((None, tk_, MEM_HEAD_DIM), lambda i, h, k: (h, k, 0)), ((tm_, MEM_HEAD_DIM), lambda i, h, k: (i, h)),
                   (kp, MEM_HEADS * MEM_HEAD_DIM), BF16, TN)

    dhq = heads_in_nt("mem_dhq", dmq_raw, w_mq)
    gw['mem_w_q'] = heads_tn("mem_dwq", hq, dmq_raw).reshape(wt['mem_w_q'].shape)
    dhm = heads_in_nt("mem_dhm_k", dmk_raw, w_mk)
    dhm = heads_in_nt("mem_dhm_v", dmv, w_mv, res=dhm)
    gw['mem_w_k'] = heads_tn("mem_dwk", hm, dmk_raw).reshape(wt['mem_w_k'].shape)
    gw['mem_w_v'] = heads_tn("mem_dwv", hm, dmv).reshape(wt['mem_w_v'].shape)
    _, gs['mem_norm'] = _rms_bwd("mem_drms", mem, sm['mem_norm'], dhm)
    comm.grads({n: gw[n] for n in MEMORY})
    dx2, gs['mem_attn_norm'] = _rms_bwd("mem_attn_drms", x2, sm['mem_attn_norm'], dhq, res=dx3)

    dcat = _mm2("mix_dcat", dx2, w_out, NT, F32)
    gw['w_out'] = _mm2("mix_dwout", cat, dx2, TN, BF16, tm=1024, tn=2048, tk=512).reshape(wt['w_out'].shape)
    do_mla, do_gla = dcat[:, :MLA_HEADS * MLA_V], dcat[:, MLA_HEADS * MLA_V:]

    dzr, dgn = _swiglu_bwd("gla_out_dgate", do_gla, zs['zr'], o_gla_n, F32)
    do_gla_raw, gs['gla_out_norm'] = _rms_bwd("gla_out_drms", o_gla_raw, sm['gla_out_norm'], dgn)
    dgq, dgk, dgv, dlog_a = _gla_bwd("gla_dscan", zs['gq'], zs['gk'], zs['gv'], log_a, states, do_gla_raw, GLA_HEADS)
    dpre, gs['gla_b_gate'] = _gate_bwd("gla_dgate", pre, sm['gla_b_gate'], dlog_a)
    dw_g2 = _mm2("gla_dwgate", zs['zg'], dpre, TN, BF16, tk=512)
    gw['gla_w_gate2'] = _stack_cols(dw_g2[:GLA_GATE_RANK])
    dzg = _mm2("gla_dzg", dpre, w_g2, NT, F32)

    dqf, dkf, dv_mla = _attn_bwd("mla_dattn", qf, kf, v_mla, do_mla, MLA_QK ** -0.5, True)
    dqn = _rope("mla_q_drope", dqf.reshape(MLA_HEADS * t, MLA_QK), cos_t, sin_t, swap, F32, True)
    dkn = _rope("mla_k_drope", dkf.reshape(MLA_HEADS * t, MLA_QK), cos_t, sin_t, swap, F32, True)
    dq_raw, gs['mla_q_norm'] = _rms_bwd("mla_q_drms", q_raw2, sm['mla_q_norm'], dqn)
    dk_raw, gs['mla_k_norm'] = _rms_bwd("mla_k_drms", k_raw2, sm['mla_k_norm'], dkn)
    dq_raw = dq_raw.reshape(MLA_HEADS, t, MLA_QK)
    dk_raw = dk_raw.reshape(MLA_HEADS, t, MLA_QK)
    dkv = jnp.concatenate([dk_raw[:, :, :MLA_NOPE], dv_mla], axis=2)
    dzkr = jnp.sum(dk_raw[:, :, MLA_NOPE:], axis=0)
    gw['w_q_up'] = _mm_stack_tn_left("mla_dwq", cq, dq_raw, BF16)
    gw['w_kv_up'] = _mm_stack_tn_left("mla_dwkv", ckv, dkv, BF16)
    dcq = _mm_stack_nt_sum("mla_dcq", dq_raw, wt['w_q_up'], F32)
    dckv = _mm_stack_nt_sum("mla_dckv", dkv, wt['w_kv_up'], F32)
    dzq, gs['q_a_norm'] = _rms_bwd("mla_q_a_drms", zs['zq'], sm['q_a_norm'], dcq)
    dzkv, gs['kv_a_norm'] = _rms_bwd("mla_kv_a_drms", zs['zkv'], sm['kv_a_norm'], dckv)

    dzs = {'zq': dzq, 'zkv': dzkv, 'gq': dgq, 'gk': dgk, 'gv': dgv, 'zr': dzr,
           'zkr': jnp.pad(dzkr, ((0, 0), (0, LANES - MLA_ROPE))), 'zg': dzg}
    dz = jnp.concatenate([dzs[name].astype(BF16) for name, _ in IN_PAD], axis=1)
    dw_in = _mm2("mix_dwin", h2, dz, TN, BF16, tm=1024, tn=2048, tk=512)
    dw_in_ref = jnp.concatenate([dw_in[:, PAD_OFF[name]:PAD_OFF[name] + n] for name, n in IN_REF], axis=1)
    gw['w_in'] = _stack_cols(dw_in_ref)
    comm.grads({n: gw[n] for n in MIXER})
    dh2 = _mm2("mix_dh", dz, w_in, NT, F32)
    dx1, gs['mix_norm'] = _rms_bwd("mix_drms", x1, sm['mix_norm'], dh2, res=dx2)

    dgu = _ffn_bwd_weights("ffn1", dx1, ffn1_saved, wt['ffn1_w_down'], comm)
    grad_x, gs['ffn1_norm'] = _ffn_bwd_input("ffn1", dx1, ffn1_saved, dgu, sm['ffn1_norm'],
                                             wt['ffn1_w_gate'], wt['ffn1_w_up'])
    return loss_lanes, grad_x, gs


def _pad_lanes(v):
    n = v.shape[1]
    return jnp.pad(v, ((0, 0), (0, -n % LANES)))


def _pack_small(vals):
    return jnp.concatenate([_pad_lanes(vals[n]) for n in SMALL], axis=1)


def _unpack_small(packed, like):
    out, off = {}, 0
    for n in SMALL:
        size = like[n].shape[1]
        out[n] = packed[:, off:off + size]
        off += size + (-size % LANES)
    return out


def kernel(x, mem, positions, ffn1_norm, ffn1_w_gate, ffn1_w_up, ffn1_w_down, mix_norm, w_in, q_a_norm, w_q_up, kv_a_norm, w_kv_up, mla_q_norm, mla_k_norm, gla_w_gate2, gla_b_gate, gla_out_norm, w_out, mem_attn_norm, mem_norm, mem_w_q, mem_w_k, mem_w_v, mem_w_o, mem_q_norm, mem_k_norm, ffn2_norm, ffn2_w_gate, ffn2_w_up, ffn2_w_down, loss_target, m_ffn1_norm, m_ffn1_w_gate, m_ffn1_w_up, m_ffn1_w_down, m_mix_norm, m_w_in, m_q_a_norm, m_w_q_up, m_kv_a_norm, m_w_kv_up, m_mla_q_norm, m_mla_k_norm, m_gla_w_gate2, m_gla_b_gate, m_gla_out_norm, m_w_out, m_mem_attn_norm, m_mem_norm, m_mem_w_q, m_mem_w_k, m_mem_w_v, m_mem_w_o, m_mem_q_norm, m_mem_k_norm, m_ffn2_norm, m_ffn2_w_gate, m_ffn2_w_up, m_ffn2_w_down, v_ffn1_norm, v_ffn1_w_gate, v_ffn1_w_up, v_ffn1_w_down, v_mix_norm, v_w_in, v_q_a_norm, v_w_q_up, v_kv_a_norm, v_w_kv_up, v_mla_q_norm, v_mla_k_norm, v_gla_w_gate2, v_gla_b_gate, v_gla_out_norm, v_w_out, v_mem_attn_norm, v_mem_norm, v_mem_w_q, v_mem_w_k, v_mem_w_v, v_mem_w_o, v_mem_q_norm, v_mem_k_norm, v_ffn2_norm, v_ffn2_w_gate, v_ffn2_w_up, v_ffn2_w_down):
    inp = dict(locals())
    x, mem, positions, target = inp['x'][0], inp['mem'][0], inp['positions'][0], inp['loss_target'][0]
    sm = {n: inp[n] for n in SMALL}
    out = {}

    class Comm:
        def __init__(self):
            self.gathers = {0: self.start(0)}
            self.forwards, self.exchanges = {}, []

        def start(self, k):
            return _gather_start(f"gather_start_{k}", [inp[n][0].astype(BF16) for n in GATHERS[k]])

        def forward(self, k):
            if k not in self.forwards:
                self.forwards[k] = _gather_mid(f"gather_mid_{k}", self.gathers[k])
                self.gathers.update({nxt: self.start(nxt) for nxt in NEXT_GATHERS.get(k, [])})

        def weights(self, k):
            self.forward(k)
            if k in EARLY_FORWARD:
                self.forward(EARLY_FORWARD[k])
            return dict(zip(GATHERS[k], _gather_end(f"gather_end_{k}", self.forwards[k])))

        def grads(self, stacks):
            names = list(stacks)
            self.exchanges.append((names, _exchange_start("exchange_start_" + names[0], [stacks[n] for n in names])))

        def update(self, count):
            todo, self.exchanges = self.exchanges[:count], self.exchanges[count:]
            for names, state in todo:
                for n, p in zip(names, _exchange_wait("exchange_wait_" + names[0], state)):
                    res = _adamw("adamw_" + n, inp[n][0], inp['m_' + n][0], inp['v_' + n][0], p)
                    for kind, r in zip(('grad_', 'delta_', 'new_m_', 'new_v_'), res):
                        out[kind + n] = r[None]

    _Chain.last = None
    comm = Comm()
    loss_lanes, grad_x, gs = _local_step(x, mem, positions, sm, comm, target)
    out['loss'] = lax.psum(jnp.sum(loss_lanes), ("x", "y", "c"))
    out['grad_x'] = grad_x[None]

    comm.update(5)
    small_parts = _all_gather("gather_small", [_pack_small(gs)])[0]
    res = _adamw("adamw_small", _pack_small(sm), _pack_small({n: inp['m_' + n] for n in SMALL}),
                 _pack_small({n: inp['v_' + n] for n in SMALL}), small_parts)
    for kind, r in zip(('grad_', 'delta_', 'new_m_', 'new_v_'), res):
        for n, val in _unpack_small(r, sm).items():
            out[kind + n] = val

    comm.update(3)

    names = ['loss', 'grad_x'] + [k + n for k in ('grad_', 'delta_', 'new_m_', 'new_v_') for n in WEIGHTS]
    return tuple(out[n] for n in names)
```

```python
import math

import jax
import jax.numpy as jnp
from jax import lax
from jax.experimental import pallas as pl
from jax.experimental.pallas import tpu as pltpu

F32 = jnp.float32
BF16 = jnp.bfloat16

N_DEV = 8
EPS = 1e-6
CHUNK = 64
MLA_HEADS, MLA_NOPE, MLA_ROPE, MLA_V = 8, 128, 64, 128
MLA_QK = MLA_NOPE + MLA_ROPE
MLA_Q_RANK, MLA_KV_RANK = 512, 256
ROPE_THETA = 10000.0
GLA_HEADS, GLA_DK, GLA_DV, GLA_GATE_RANK = 4, 128, 256, 16
GLA_TAU = 16.0
MEM_HEADS, MEM_HEAD_DIM = 4, 128
ADAM_LR, ADAM_B1, ADAM_B2, ADAM_EPS, ADAM_WD, ADAM_STEP = 0.001, 0.9, 0.999, 1e-08, 0.01, 10

V7X_VMEM_BYTES = 64 * 1024 * 1024
LANES = 128

NN = (((1,), (0,)), ((), ()))
NT = (((1,), (1,)), ((), ()))
TN = (((0,), (0,)), ((), ()))

WEIGHTS = ['ffn1_norm', 'ffn1_w_gate', 'ffn1_w_up', 'ffn1_w_down', 'mix_norm', 'w_in', 'q_a_norm', 'w_q_up',
           'kv_a_norm', 'w_kv_up', 'mla_q_norm', 'mla_k_norm', 'gla_w_gate2', 'gla_b_gate', 'gla_out_norm', 'w_out',
           'mem_attn_norm', 'mem_norm', 'mem_w_q', 'mem_w_k', 'mem_w_v', 'mem_w_o', 'mem_q_norm', 'mem_k_norm',
           'ffn2_norm', 'ffn2_w_gate', 'ffn2_w_up', 'ffn2_w_down']
SMALL = ['ffn1_norm', 'mix_norm', 'q_a_norm', 'kv_a_norm', 'mla_q_norm', 'mla_k_norm', 'gla_b_gate', 'gla_out_norm',
         'mem_attn_norm', 'mem_norm', 'mem_q_norm', 'mem_k_norm', 'ffn2_norm']
MIXER = ['w_in', 'w_q_up', 'w_kv_up', 'gla_w_gate2', 'w_out']
MEMORY = ['mem_w_q', 'mem_w_k', 'mem_w_v', 'mem_w_o']
TRANSPOSED = ['ffn1_w_gate', 'ffn1_w_up', 'ffn2_w_gate', 'ffn2_w_up', 'w_q_up']
GATHERS = [['ffn1_w_gate'], ['ffn1_w_up'], ['ffn1_w_down'], MIXER, MEMORY, ['ffn2_w_gate', 'ffn2_w_up', 'ffn2_w_down']]
NEXT_GATHERS = {0: [1], 1: [2], 2: [3], 3: [4, 5]}
EARLY_FORWARD = {4: 5}

IN_REF = [('zq', 512), ('zkv', 256), ('zkr', 64), ('gq', 512), ('gk', 512), ('gv', 1024), ('zg', 16), ('zr', 1024)]
IN_PAD = [('zq', 512), ('zkv', 256), ('gq', 512), ('gk', 512), ('gv', 1024), ('zr', 1024), ('zkr', 128), ('zg', 128)]
IN_WIDTH = sum(n for _, n in IN_REF)
IN_PAD_WIDTH = sum(n for _, n in IN_PAD)


def _offsets(layout):
    out, off = {}, 0
    for name, n in layout:
        out[name] = off
        off += n
    return out


REF_OFF, PAD_OFF = _offsets(IN_REF), _offsets(IN_PAD)
REF_SIZE = dict(IN_REF)


def _tile(n, pref):
    return pref if n % pref == 0 else n


def _block_bytes(blk, dtype):
    dims = [d for d in blk if d is not None]
    if len(dims) >= 1:
        dims[-1] = -(-dims[-1] // LANES) * LANES
    return math.prod(dims) * jnp.dtype(dtype).itemsize


def _vmem_limit(pipelined_bytes, resident_bytes=0):
    need = 2 * pipelined_bytes + resident_bytes + (8 << 20)
    return int(min(max(need, 32 << 20), V7X_VMEM_BYTES - (6 << 20)))


class _Chain:
    last = None


def _chained(body, *, in_specs, link=0, pin=True, **kwargs):
    def call(*operands):
        dep = _Chain.last
        if dep is not None and any(o is dep for o in operands):
            dep = None
        if pin:
            operands = [pltpu.with_memory_space_constraint(o, pltpu.HBM) for o in operands]
        if dep is None:
            res = pl.pallas_call(body, in_specs=in_specs, **kwargs)(*operands)
        else:
            n = len(operands)

            def chained_body(*refs):
                body(*refs[:n], *refs[n + 1:])

            res = pl.pallas_call(chained_body, in_specs=list(in_specs) + [pl.BlockSpec(memory_space=pl.ANY)],
                                 **kwargs)(*operands, dep)
        _Chain.last = res[link] if isinstance(res, (list, tuple)) else res
        return res

    return call


def _rowwise_params(*semantics):
    return pltpu.CompilerParams(dimension_semantics=semantics, vmem_limit_bytes=48 << 20)


def _mm(name, a, b, grid, a_spec, b_spec, o_spec, out_shape, out_dtype, dims, scale=None, res=None):
    nk = grid[2]
    (a_blk, a_map), (b_blk, b_map), (o_blk, o_map) = a_spec, b_spec, o_spec
    acc_shape = tuple(d for d in o_blk if d is not None)

    def body(*refs):
        if res is None:
            a_ref, b_ref, o_ref = refs[:3]
            r_ref, rest = None, refs[3:]
        else:
            a_ref, b_ref, r_ref, o_ref = refs[:4]
            rest = refs[4:]

        def product():
            return lax.dot_general(a_ref[...].astype(BF16), b_ref[...].astype(BF16), dims, preferred_element_type=F32)

        def finish(r):
            if scale is not None:
                r = r * scale
            if r_ref is not None:
                r = r + r_ref[...].astype(F32)
            o_ref[...] = r.astype(o_ref.dtype)

        if nk == 1:
            finish(product())
        else:
            acc = rest[0]
            k = pl.program_id(2)

            @pl.when(k == 0)
            def _():
                acc[...] = product()

            @pl.when(k > 0)
            def _():
                acc[...] += product()

            @pl.when(k == nk - 1)
            def _():
                finish(acc[...])

    in_specs = [pl.BlockSpec(a_blk, a_map), pl.BlockSpec(b_blk, b_map)]
    operands = [a, b]
    piped = _block_bytes(a_blk, a.dtype) + _block_bytes(b_blk, b.dtype) + _block_bytes(o_blk, out_dtype)
    if res is not None:
        in_specs.append(pl.BlockSpec(o_blk, o_map))
        operands.append(res)
        piped += _block_bytes(o_blk, res.dtype)
    scratch = [pltpu.VMEM(acc_shape, F32)] if nk > 1 else []
    return _chained(
        body, name=name, grid=grid, in_specs=in_specs, out_specs=pl.BlockSpec(o_blk, o_map),
        out_shape=jax.ShapeDtypeStruct(out_shape, out_dtype), scratch_shapes=scratch,
        compiler_params=pltpu.CompilerParams(
            dimension_semantics=("parallel", "parallel", "arbitrary"),
            vmem_limit_bytes=_vmem_limit(piped, 3 * _block_bytes(acc_shape, F32))),
    )(*operands)


def _mm2(name, a, b, dims, out_dtype, tm=512, tn=1024, tk=2048, scale=None, res=None):
    if dims is NN:
        (m, kk), n = a.shape, b.shape[1]
    elif dims is NT:
        (m, kk), n = a.shape, b.shape[0]
    else:
        (kk, m), n = a.shape, b.shape[1]
    tm, tn, tk = _tile(m, tm), _tile(n, tn), _tile(kk, tk)
    a_spec = ((tk, tm), lambda i, j, k: (k, i)) if dims is TN else ((tm, tk), lambda i, j, k: (i, k))
    b_spec = ((tn, tk), lambda i, j, k: (j, k)) if dims is NT else ((tk, tn), lambda i, j, k: (k, j))
    return _mm(name, a, b, (m // tm, n // tn, kk // tk), a_spec, b_spec, ((tm, tn), lambda i, j, k: (i, j)),
               (m, n), out_dtype, dims, scale, res)


def _mm_stack_out(name, a, b, out_dtype, tm=512, tk=2048):
    (m, kk), (nj, _, n) = a.shape, b.shape
    tm, tk = _tile(m, tm), _tile(kk, tk)
    return _mm(name, a, b, (nj, m // tm, kk // tk), ((tm, tk), lambda j, i, k: (i, k)),
               ((None, tk, n), lambda j, i, k: (j, k, 0)), ((None, tm, n), lambda j, i, k: (j, i, 0)),
               (nj, m, n), out_dtype, NN)


def _mm_stack_nt_out(name, a, b, out_dtype, scale=None, tm=512, tk=2048):
    (m, kk), (nj, n, _) = a.shape, b.shape
    tm, tk = _tile(m, tm), _tile(kk, tk)
    return _mm(name, a, b, (nj, m // tm, kk // tk), ((tm, tk), lambda j, i, k: (i, k)),
               ((None, n, tk), lambda j, i, k: (j, 0, k)), ((None, tm, n), lambda j, i, k: (j, i, 0)),
               (nj, m, n), out_dtype, NT, scale)


def _mm_stack_sum(name, a, b, out_dtype, scale=None, res=None, tm=512, tn=1024):
    (nj, m, f), n = a.shape, b.shape[2]
    tm, tn = _tile(m, tm), _tile(n, tn)
    return _mm(name, a, b, (m // tm, n // tn, nj), ((None, tm, f), lambda i, j, k: (k, i, 0)),
               ((None, f, tn), lambda i, j, k: (k, 0, j)), ((tm, tn), lambda i, j, k: (i, j)),
               (m, n), out_dtype, NN, scale, res)


def _mm_stack_nt_sum(name, a, b, out_dtype, res=None, tm=512, tn=1024):
    (nj, m, f), n = a.shape, b.shape[1]
    tm, tn = _tile(m, tm), _tile(n, tn)
    return _mm(name, a, b, (m // tm, n // tn, nj), ((None, tm, f), lambda i, j, k: (k, i, 0)),
               ((None, tn, f), lambda i, j, k: (k, j, 0)), ((tm, tn), lambda i, j, k: (i, j)),
               (m, n), out_dtype, NT, None, res)


def _mm_stack_tn_left(name, a, b, out_dtype, tm=1024, tk=512):
    (m, kp), (nj, _, n) = a.shape, b.shape
    tm, tk = _tile(kp, tm), _tile(m, tk)
    return _mm(name, a, b, (nj, kp // tm, m // tk), ((tk, tm), lambda j, i, k: (k, i)),
               ((None, tk, n), lambda j, i, k: (j, k, 0)), ((None, tm, n), lambda j, i, k: (j, i, 0)),
               (nj, kp, n), out_dtype, TN)


def _mm_stack_tn_right(name, a, b, out_dtype, scale=None, tn=2048, tk=512):
    (nj, m, f), n = a.shape, b.shape[1]
    tn, tk = _tile(n, tn), _tile(m, tk)
    return _mm(name, a, b, (nj, n // tn, m // tk), ((None, tk, f), lambda j, i, k: (j, k, 0)),
               ((tk, tn), lambda j, i, k: (k, i)), ((None, f, tn), lambda j, i, k: (j, 0, i)),
               (nj, f, n), out_dtype, TN, scale)


def _rms_fwd(name, x, g, out_dtype, tm=256):
    rows, cols = x.shape
    d = g.shape[1]
    tm = _tile(rows, tm)

    def body(x_ref, g_ref, o_ref):
        xf = x_ref[...].astype(F32)
        r = lax.rsqrt(jnp.mean(xf * xf, axis=-1, keepdims=True) + EPS)
        o_ref[...] = (xf * r * g_ref[...]).astype(o_ref.dtype)

    return _chained(
        body, name=name, grid=(rows // tm, cols // d),
        in_specs=[pl.BlockSpec((tm, d), lambda i, c: (i, c)), pl.BlockSpec((1, d), lambda i, c: (0, 0))],
        out_specs=pl.BlockSpec((tm, d), lambda i, c: (i, c)),
        out_shape=jax.ShapeDtypeStruct((rows, cols), out_dtype),
        compiler_params=_rowwise_params("parallel", "parallel"),
    )(x, g)


def _rms_bwd(name, x, g, dy, res=None, tm=256):
    rows, cols = x.shape
    d = g.shape[1]
    tm = _tile(rows, tm)

    def body(*refs):
        if res is None:
            x_ref, g_ref, dy_ref, dx_ref, dg_ref = refs
            r_ref = None
        else:
            x_ref, g_ref, dy_ref, r_ref, dx_ref, dg_ref = refs
        xf = x_ref[...].astype(F32)
        r = lax.rsqrt(jnp.mean(xf * xf, axis=-1, keepdims=True) + EPS)
        xhat = xf * r
        dyf = dy_ref[...].astype(F32)

        @pl.when((pl.program_id(0) == 0) & (pl.program_id(1) == 0))
        def _():
            dg_ref[...] = jnp.zeros_like(dg_ref)

        dg_ref[...] += jnp.sum(dyf * xhat, axis=0, keepdims=True)
        dxh = dyf * g_ref[...]
        dx = r * (dxh - xhat * jnp.mean(dxh * xhat, axis=-1, keepdims=True))
        if r_ref is not None:
            dx = dx + r_ref[...].astype(F32)
        dx_ref[...] = dx

    blk = pl.BlockSpec((tm, d), lambda i, c: (i, c))
    in_specs = [blk, pl.BlockSpec((1, d), lambda i, c: (0, 0)), blk]
    operands = [x, g, dy]
    if res is not None:
        in_specs.append(blk)
        operands.append(res)
    return _chained(
        body, name=name, grid=(rows // tm, cols // d), in_specs=in_specs,
        out_specs=(blk, pl.BlockSpec((1, d), lambda i, c: (0, 0))),
        out_shape=(jax.ShapeDtypeStruct((rows, cols), F32), jax.ShapeDtypeStruct((1, d), F32)),
        compiler_params=_rowwise_params("arbitrary", "arbitrary"),
    )(*operands)


def _swiglu_fwd(name, g, u, out_dtype, tm=256):
    rows, cols = g.shape
    tm = _tile(rows, tm)

    def body(g_ref, u_ref, o_ref):
        gf = g_ref[...].astype(F32)
        o_ref[...] = (gf * jax.nn.sigmoid(gf) * u_ref[...].astype(F32)).astype(o_ref.dtype)

    blk = pl.BlockSpec((tm, cols), lambda i: (i, 0))
    return _chained(
        body, name=name, grid=(rows // tm,), in_specs=[blk, blk], out_specs=blk,
        out_shape=jax.ShapeDtypeStruct((rows, cols), out_dtype),
        compiler_params=_rowwise_params("parallel"),
    )(g, u)


def _swiglu_bwd(name, da, g, u, out_dtype, tm=256):
    rows, cols = g.shape
    tm = _tile(rows, tm)

    def body(da_ref, g_ref, u_ref, dg_ref, du_ref):
        gf = g_ref[...].astype(F32)
        daf = da_ref[...].astype(F32)
        sig = jax.nn.sigmoid(gf)
        du_ref[...] = (daf * gf * sig).astype(du_ref.dtype)
        dg_ref[...] = (daf * u_ref[...].astype(F32) * sig * (1.0 + gf * (1.0 - sig))).astype(dg_ref.dtype)

    blk = pl.BlockSpec((tm, cols), lambda i: (i, 0))
    sds = jax.ShapeDtypeStruct((rows, cols), out_dtype)
    return _chained(
        body, name=name, grid=(rows // tm,), in_specs=[blk, blk, blk], out_specs=(blk, blk), out_shape=(sds, sds),
        compiler_params=_rowwise_params("parallel"),
    )(da, g, u)


def _split3(x):
    hi = x.astype(BF16)
    r1 = x - hi.astype(F32)
    mid = r1.astype(BF16)
    lo = (r1 - mid.astype(F32)).astype(BF16)
    return hi, mid, lo


def _rope(name, x, cos_t, sin_t, swap, out_dtype, backward, tm=256):
    rows, d = x.shape
    t = cos_t.shape[0]
    tm = _tile(t, tm)
    nt = t // tm

    def body(x_ref, c_ref, s_ref, p_ref, o_ref):
        xf = x_ref[...].astype(F32)
        p = p_ref[...]
        lhs = xf * s_ref[...] if backward else xf
        sw = sum(jnp.dot(part, p, preferred_element_type=F32) for part in _split3(lhs))
        if not backward:
            sw = sw * s_ref[...]
        o_ref[...] = (xf * c_ref[...] + sw).astype(o_ref.dtype)

    blk = pl.BlockSpec((tm, d), lambda i: (i, 0))
    tab = pl.BlockSpec((tm, d), lambda i: (i % nt, 0))
    return _chained(
        body, name=name, grid=(rows // tm,), in_specs=[blk, tab, tab, pl.BlockSpec((d, d), lambda i: (0, 0))],
        out_specs=blk, out_shape=jax.ShapeDtypeStruct((rows, d), out_dtype),
        compiler_params=_rowwise_params("parallel"),
    )(x, cos_t, sin_t, swap)


def _gate_fwd(name, pre, bias, tm=256):
    rows, cols = pre.shape
    tm = _tile(rows, tm)

    def body(p_ref, b_ref, o_ref):
        z = p_ref[...] + b_ref[...]
        o_ref[...] = (jnp.minimum(z, 0.0) - jnp.log(1.0 + jnp.exp(-jnp.abs(z)))) * (1.0 / GLA_TAU)

    blk = pl.BlockSpec((tm, cols), lambda i: (i, 0))
    return _chained(
        body, name=name, grid=(rows // tm,), in_specs=[blk, pl.BlockSpec((1, cols), lambda i: (0, 0))], out_specs=blk,
        out_shape=jax.ShapeDtypeStruct((rows, cols), F32),
        compiler_params=_rowwise_params("parallel"),
    )(pre, bias)


def _gate_bwd(name, pre, bias, dla, tm=256):
    rows, cols = pre.shape
    tm = _tile(rows, tm)

    def body(p_ref, b_ref, d_ref, dp_ref, db_ref):
        z = p_ref[...] + b_ref[...]
        dp = d_ref[...] * (1.0 / GLA_TAU) / (1.0 + jnp.exp(z))
        dp_ref[...] = dp

        @pl.when(pl.program_id(0) == 0)
        def _():
            db_ref[...] = jnp.zeros_like(db_ref)

        db_ref[...] += jnp.sum(dp, axis=0, keepdims=True)

    blk = pl.BlockSpec((tm, cols), lambda i: (i, 0))
    row = pl.BlockSpec((1, cols), lambda i: (0, 0))
    return _chained(
        body, name=name, grid=(rows // tm,), in_specs=[blk, row, blk], out_specs=(blk, row),
        out_shape=(jax.ShapeDtypeStruct((rows, cols), F32), jax.ShapeDtypeStruct((1, cols), F32)),
        compiler_params=_rowwise_params("arbitrary"),
    )(pre, bias, dla)


def _loss(name, y, target, tm=256):
    rows, d = y.shape
    tm = _tile(rows, tm)

    def body(y_ref, t_ref, dy_ref, l_ref):
        err = y_ref[...] - t_ref[...]
        dy_ref[...] = err * (1.0 / d)

        @pl.when(pl.program_id(0) == 0)
        def _():
            l_ref[...] = jnp.zeros_like(l_ref)

        sq = (err * err).reshape(tm // 8, 8, d)
        l_ref[...] += jnp.sum(sq, axis=0) * (0.5 / d)

    blk = pl.BlockSpec((tm, d), lambda i: (i, 0))
    return _chained(
        body, name=name, grid=(rows // tm,), in_specs=[blk, blk],
        out_specs=(blk, pl.BlockSpec((8, d), lambda i: (0, 0))),
        out_shape=(jax.ShapeDtypeStruct((rows, d), F32), jax.ShapeDtypeStruct((8, d), F32)),
        compiler_params=_rowwise_params("arbitrary"),
    )(y, target)


def _scores(q, k, scale, causal, q0):
    s = lax.dot_general(q, k, NT, preferred_element_type=F32) * scale
    if causal:
        qc = (q0 + lax.broadcasted_iota(jnp.int32, s.shape, 0)) // CHUNK
        kc = lax.broadcasted_iota(jnp.int32, s.shape, 1) // CHUNK
        s = jnp.where(kc <= qc, s, -1e30)
    e = jnp.exp(s - jnp.max(s, axis=-1, keepdims=True))
    return e, jnp.sum(e, axis=-1, keepdims=True)


def _attn_fwd(name, q, k, v, scale, causal, tq=256):
    nh, t, dk = q.shape
    tk, dv = k.shape[1], v.shape[2]
    tq = _tile(t, tq)

    def body(q_ref, k_ref, v_ref, o_ref):
        e, l = _scores(q_ref[...], k_ref[...], scale, causal, pl.program_id(1) * tq)
        o = jnp.dot(e.astype(BF16), v_ref[...], preferred_element_type=F32)
        o_ref[...] = (o / l).astype(o_ref.dtype)

    return _chained(
        body, name=name, grid=(nh, t // tq),
        in_specs=[pl.BlockSpec((None, tq, dk), lambda h, i: (h, i, 0)), pl.BlockSpec((None, tk, dk), lambda h, i: (h, 0, 0)),
                  pl.BlockSpec((None, tk, dv), lambda h, i: (h, 0, 0))],
        out_specs=pl.BlockSpec((tq, dv), lambda h, i: (i, h)),
        out_shape=jax.ShapeDtypeStruct((t, nh * dv), BF16),
        compiler_params=pltpu.CompilerParams(dimension_semantics=("parallel", "parallel"),
                                             vmem_limit_bytes=_vmem_limit(0, 6 * tq * tk * 4)),
    )(q, k, v)


def _attn_bwd(name, q, k, v, do, scale, causal, tq=256):
    nh, t, dk = q.shape
    tk, dv = k.shape[1], v.shape[2]
    tq = _tile(t, tq)

    def body(q_ref, k_ref, v_ref, do_ref, dq_ref, dk_ref, dv_ref):
        qb, kb = q_ref[...], k_ref[...]
        e, l = _scores(qb, kb, scale, causal, pl.program_id(1) * tq)
        p = e / l
        dob = do_ref[...].astype(BF16)
        dp = lax.dot_general(dob, v_ref[...], NT, preferred_element_type=F32)
        ds = (p * (dp - jnp.sum(p * dp, axis=-1, keepdims=True)) * scale).astype(BF16)
        dq_ref[...] = jnp.dot(ds, kb, preferred_element_type=F32)

        @pl.when(pl.program_id(1) == 0)
        def _():
            dk_ref[...] = jnp.zeros_like(dk_ref)
            dv_ref[...] = jnp.zeros_like(dv_ref)

        dk_ref[...] += lax.dot_general(ds, qb, TN, preferred_element_type=F32)
        dv_ref[...] += lax.dot_general(p.astype(BF16), dob, TN, preferred_element_type=F32)

    return _chained(
        body, name=name, grid=(nh, t // tq),
        in_specs=[pl.BlockSpec((None, tq, dk), lambda h, i: (h, i, 0)), pl.BlockSpec((None, tk, dk), lambda h, i: (h, 0, 0)),
                  pl.BlockSpec((None, tk, dv), lambda h, i: (h, 0, 0)), pl.BlockSpec((tq, dv), lambda h, i: (i, h))],
        out_specs=(pl.BlockSpec((None, tq, dk), lambda h, i: (h, i, 0)), pl.BlockSpec((None, tk, dk), lambda h, i: (h, 0, 0)),
                   pl.BlockSpec((None, tk, dv), lambda h, i: (h, 0, 0))),
        out_shape=(jax.ShapeDtypeStruct((nh, t, dk), F32), jax.ShapeDtypeStruct((nh, tk, dk), F32),
                   jax.ShapeDtypeStruct((nh, tk, dv), F32)),
        compiler_params=pltpu.CompilerParams(dimension_semantics=("parallel", "arbitrary"),
                                             vmem_limit_bytes=_vmem_limit(0, 10 * tq * tk * 4)),
    )(q, k, v, do)


def _tri(lower):
    r = lax.broadcasted_iota(jnp.int32, (CHUNK, CHUNK), 0)
    c = lax.broadcasted_iota(jnp.int32, (CHUNK, CHUNK), 1)
    return jnp.where((c <= r) if lower else (c >= r), 1.0, 0.0).astype(BF16)


def _tri_dot(tri, x):
    return sum(jnp.dot(tri, part, preferred_element_type=F32) for part in _split3(x))


def _gla_fwd(name, q, k, v, la, nh):
    t = q.shape[0]
    dk, dv = q.shape[1] // nh, v.shape[1] // nh
    nc = t // CHUNK

    def body(q_ref, k_ref, v_ref, g_ref, o_ref, st_ref, state):
        @pl.when(pl.program_id(1) == 0)
        def _():
            state[...] = jnp.zeros_like(state)

        g = g_ref[...]
        b = _tri_dot(_tri(True), g)
        b_end = jnp.sum(g, axis=0, keepdims=True)
        k_dec = (k_ref[...] * jnp.exp(b_end - b)).astype(BF16)
        u_t = lax.dot_general(v_ref[...].astype(BF16), k_dec, TN, preferred_element_type=F32)
        new = state[...] * jnp.exp(b_end) + u_t
        state[...] = new
        st_ref[...] = new
        qc = (q_ref[...] * (dk ** -0.5)).astype(BF16)
        o_ref[...] = lax.dot_general(qc, new.astype(BF16), NT, preferred_element_type=F32)

    kblk = pl.BlockSpec((CHUNK, dk), lambda h, n: (n, h))
    vblk = pl.BlockSpec((CHUNK, dv), lambda h, n: (n, h))
    return _chained(
        body, name=name, grid=(nh, nc), in_specs=[kblk, kblk, vblk, kblk],
        out_specs=(vblk, pl.BlockSpec((None, None, dv, dk), lambda h, n: (h, n, 0, 0))),
        out_shape=(jax.ShapeDtypeStruct((t, nh * dv), F32), jax.ShapeDtypeStruct((nh, nc, dv, dk), F32)),
        scratch_shapes=[pltpu.VMEM((dv, dk), F32)],
        compiler_params=_rowwise_params("parallel", "arbitrary"),
    )(q, k, v, la)


def _gla_bwd(name, q, k, v, la, states, do, nh):
    t = q.shape[0]
    dk, dv = q.shape[1] // nh, v.shape[1] // nh
    nc = t // CHUNK
    scale = dk ** -0.5

    def body(q_ref, k_ref, v_ref, g_ref, do_ref, st_ref, sp_ref, dq_ref, dk_ref, dv_ref, dg_ref, carry):
        i = pl.program_id(1)

        @pl.when(i == 0)
        def _():
            carry[...] = jnp.zeros_like(carry)

        g = g_ref[...]
        b = _tri_dot(_tri(True), g)
        b_end = jnp.sum(g, axis=0, keepdims=True)
        w = jnp.exp(b_end - b)
        decay = jnp.exp(b_end)
        k_dec = k_ref[...] * w
        qc = (q_ref[...] * scale).astype(BF16)
        dob = do_ref[...].astype(BF16)
        dq_ref[...] = jnp.dot(dob, st_ref[...].astype(BF16), preferred_element_type=F32) * scale
        g_t = carry[...] + lax.dot_general(dob, qc, TN, preferred_element_type=F32)
        g_tb = g_t.astype(BF16)
        dk_dec = jnp.dot(v_ref[...].astype(BF16), g_tb, preferred_element_type=F32)
        dv_ref[...] = lax.dot_general(k_dec.astype(BF16), g_tb, NT, preferred_element_type=F32)
        prev = jnp.where(i < nc - 1, sp_ref[...], 0.0)
        ddecay = jnp.sum(g_t * prev, axis=0, keepdims=True)
        dk_ref[...] = dk_dec * w
        e = dk_dec * k_dec
        db_end = jnp.sum(e, axis=0, keepdims=True) + ddecay * decay
        dg_ref[...] = _tri_dot(_tri(False), -e) + db_end
        carry[...] = g_t * decay

    kblk = pl.BlockSpec((CHUNK, dk), lambda h, i: (nc - 1 - i, h))
    vblk = pl.BlockSpec((CHUNK, dv), lambda h, i: (nc - 1 - i, h))
    ksds = jax.ShapeDtypeStruct((t, nh * dk), F32)
    return _chained(
        body, name=name, grid=(nh, nc),
        in_specs=[kblk, kblk, vblk, kblk, vblk,
                  pl.BlockSpec((None, None, dv, dk), lambda h, i: (h, nc - 1 - i, 0, 0)),
                  pl.BlockSpec((None, None, dv, dk), lambda h, i: (h, jnp.maximum(nc - 2 - i, 0), 0, 0))],
        out_specs=(kblk, kblk, vblk, kblk),
        out_shape=(ksds, ksds, jax.ShapeDtypeStruct((t, nh * dv), F32), ksds),
        scratch_shapes=[pltpu.VMEM((dv, dk), F32)],
        compiler_params=_rowwise_params("parallel", "arbitrary"),
    )(q, k, v, la, do, states, states)


def _adamw(name, w, m, v, parts):
    rows, rest = w.shape[0], w.shape[1:]
    cols = rest[-1]
    if len(rest) == 1:
        tm = _tile(rows, 1 << int(math.log2(max(8, (1 << 17) // (-(-cols // LANES) * LANES)))))
    else:
        tm = next(t for t in range(16, 0, -1) if rows % t == 0)
    zeros = (0,) * len(rest)

    def body(w_ref, m_ref, v_ref, p_ref, g_ref, d_ref, nm_ref, nv_ref):
        g = p_ref[0].astype(F32)
        for s in range(1, N_DEV):
            g = g + p_ref[s].astype(F32)
        m_new = ADAM_B1 * m_ref[...] + (1.0 - ADAM_B1) * g
        v_new = ADAM_B2 * v_ref[...] + (1.0 - ADAM_B2) * jnp.square(g)
        m_hat = m_new / (1.0 - ADAM_B1 ** ADAM_STEP)
        v_hat = v_new / (1.0 - ADAM_B2 ** ADAM_STEP)
        g_ref[...] = g
        d_ref[...] = -ADAM_LR * (m_hat / (jnp.sqrt(v_hat) + ADAM_EPS) + ADAM_WD * w_ref[...])
        nm_ref[...] = m_new
        nv_ref[...] = v_new

    blk = pl.BlockSpec((tm,) + rest, lambda i: (i,) + zeros)
    sds = jax.ShapeDtypeStruct(w.shape, F32)
    return _chained(
        body, name=name, grid=(rows // tm,),
        in_specs=[blk, blk, blk, pl.BlockSpec((N_DEV, tm) + rest, lambda i: (0, i) + zeros)],
        out_specs=(blk, blk, blk, blk), out_shape=(sds, sds, sds, sds),
        compiler_params=_rowwise_params("parallel"),
    )(w, m, v, parts)


HBM = pl.BlockSpec(memory_space=pl.ANY)
MESH = pl.DeviceIdType.MESH


def _all_gather(name, shards):
    n = len(shards)

    def body(*refs):
        ins, outs = refs[:n], refs[n:2 * n]
        send_sems, recv_sems, local_sems = refs[2 * n:]
        x, y, c = lax.axis_index("x"), lax.axis_index("y"), lax.axis_index("c")
        me, sibling = (x, y, c), (x, y, 1 - c)
        chips = [(1 - x, y), (x, 1 - y), (1 - x, 1 - y)]

        def copy(w, k, block, to, src=None):
            dst = outs[w].at[4 * block[0] + 2 * block[1] + block[2]]
            return pltpu.make_async_remote_copy(
                src_ref=dst if src is None else src, dst_ref=dst, send_sem=send_sems.at[7 * w + k],
                recv_sem=recv_sems.at[7 * w + k], device_id=to, device_id_type=MESH)

        mine, first, passed = [], [], []
        for w in range(n):
            cp = pltpu.make_async_copy(ins[w], outs[w].at[4 * x + 2 * y + c], local_sems.at[w])
            cp.start()
            mine.append(cp)
            first.append(copy(w, 0, me, sibling, src=ins[w]))
            first += [copy(w, 1 + j, me, (*chip, c), src=ins[w]) for j, chip in enumerate(chips)]
        for cp in first:
            cp.start()
        for w in range(n):
            for j, chip in enumerate(chips):
                copy(w, 1 + j, (*chip, c), me).wait_recv()
                cp = copy(w, 4 + j, (*chip, c), sibling)
                cp.start()
                passed.append(cp)
        for w in range(n):
            copy(w, 0, sibling, me).wait_recv()
            for j, chip in enumerate(chips):
                copy(w, 4 + j, (*chip, 1 - c), me).wait_recv()
        for cp in first + passed:
            cp.wait_send()
        for cp in mine:
            cp.wait()

    return _chained(
        body, name=name, in_specs=[HBM] * n, out_specs=[HBM] * n,
        out_shape=[jax.ShapeDtypeStruct((N_DEV,) + s.shape, s.dtype) for s in shards],
        scratch_shapes=[pltpu.SemaphoreType.DMA((7 * n,)), pltpu.SemaphoreType.DMA((7 * n,)),
                        pltpu.SemaphoreType.DMA((n,))],
    )(*shards)


HBM_SPEC = pl.BlockSpec(memory_space=pltpu.HBM)
SEM_SPEC = pl.BlockSpec(memory_space=pltpu.SEMAPHORE)
EFFECT = pltpu.SideEffectType.DATAFLOW_SIDE_EFFECTING
TOKEN = jax.ShapeDtypeStruct((8, LANES), F32)


def _in_hbm(a):
    return pltpu.with_memory_space_constraint(a, pltpu.HBM)


def _place():
    x, y, c = lax.axis_index("x"), lax.axis_index("y"), lax.axis_index("c")
    chips = [(1 - x, y), (x, 1 - y), (1 - x, 1 - y)]
    return x, y, c, chips


def _block_of(px, py, pc):
    return 4 * px + 2 * py + pc


def _gather_copies(ins, outs, sems, w):
    send_sems, recv_sems, local_sems = sems
    x, y, c, chips = _place()
    peers = [(x, y, 1 - c)] + [(*chip, c) for chip in chips]

    def copy(k, block, to, src):
        dst = outs[w].at[_block_of(*block)]
        return pltpu.make_async_remote_copy(src_ref=dst if src is None else src, dst_ref=dst, send_sem=send_sems.at[4 * w + k],
                                            recv_sem=recv_sems.at[4 * w + k], device_id=to, device_id_type=MESH)

    local = pltpu.make_async_copy(ins[w], outs[w].at[_block_of(x, y, c)], local_sems.at[w])
    sends = [copy(k, (x, y, c), peer, ins[w]) for k, peer in enumerate(peers)]
    recvs = [copy(k, peer, (x, y, c), None) for k, peer in enumerate(peers)]
    return local, sends, recvs


def _forward_copies(outs, sems, w):
    send_sems, recv_sems = sems
    x, y, c, chips = _place()

    def copy(j, block, to):
        dst = outs[w].at[_block_of(*block)]
        return pltpu.make_async_remote_copy(src_ref=dst, dst_ref=dst, send_sem=send_sems.at[3 * w + j],
                                            recv_sem=recv_sems.at[3 * w + j], device_id=to, device_id_type=MESH)

    sends = [copy(j, (*chip, c), (x, y, 1 - c)) for j, chip in enumerate(chips)]
    recvs = [copy(j, (*chip, 1 - c), (x, y, c)) for j, chip in enumerate(chips)]
    return sends, recvs


def _gather_start(name, shards):
    n = len(shards)

    def body(*refs):
        ins, outs, sems = refs[:n], refs[n:2 * n], refs[2 * n:2 * n + 3]
        for w in range(n):
            local, sends, _ = _gather_copies(ins, outs, sems, w)
            local.start()
            for cp in sends:
                cp.start()
        refs[-1][...] = jnp.zeros_like(refs[-1])

    lands = [lax.empty((N_DEV,) + s.shape, s.dtype) for s in shards]
    res = _chained(
        body, name=name, link=-1, pin=False, in_specs=[HBM_SPEC] * (2 * n),
        out_specs=[SEM_SPEC] * 3 + [HBM_SPEC] * (2 * n) + [pl.BlockSpec(memory_space=pltpu.VMEM)],
        out_shape=[pltpu.SemaphoreType.DMA((4 * n,)), pltpu.SemaphoreType.DMA((4 * n,)), pltpu.SemaphoreType.DMA((n,))]
        + [pltpu.HBM(s.shape, s.dtype) for s in shards] + [pltpu.HBM(l.shape, l.dtype) for l in lands] + [TOKEN],
        input_output_aliases={i: 3 + i for i in range(2 * n)},
        compiler_params=pltpu.CompilerParams(has_side_effects=EFFECT),
    )(*[_in_hbm(s) for s in shards], *[_in_hbm(l) for l in lands])
    return n, res[:3], res[3:3 + n], res[3 + n:3 + 2 * n]


def _gather_mid(name, state):
    n, sems, shards, lands = state

    def body(*refs):
        ins, outs, sems_in = refs[:n], refs[n:2 * n], refs[2 * n:2 * n + 3]
        sems_out = refs[2 * n + 3:2 * n + 5]
        for w in range(n):
            local, sends, recvs = _gather_copies(ins, outs, sems_in, w)
            local.wait()
            for cp in sends:
                cp.wait_send()
            for cp in recvs:
                cp.wait_recv()
            for cp in _forward_copies(outs, sems_out, w)[0]:
                cp.start()
        refs[-1][...] = jnp.zeros_like(refs[-1])

    res = _chained(
        body, name=name, link=-1, pin=False, in_specs=[HBM_SPEC] * (2 * n) + [SEM_SPEC] * 3,
        out_specs=[SEM_SPEC] * 2 + [HBM_SPEC] * n + [pl.BlockSpec(memory_space=pltpu.VMEM)],
        out_shape=[pltpu.SemaphoreType.DMA((3 * n,)), pltpu.SemaphoreType.DMA((3 * n,))]
        + [pltpu.HBM(l.shape, l.dtype) for l in lands] + [TOKEN],
        input_output_aliases={n + i: 2 + i for i in range(n)},
        compiler_params=pltpu.CompilerParams(has_side_effects=EFFECT),
    )(*shards, *lands, *sems)
    return n, res[:2], res[2:2 + n]


def _gather_end(name, state):
    n, sems, lands = state

    def body(*refs):
        outs, sems_in = refs[:n], refs[n:n + 2]
        for w in range(n):
            sends, recvs = _forward_copies(outs, sems_in, w)
            for cp in sends:
                cp.wait_send()
            for cp in recvs:
                cp.wait_recv()

    return _chained(
        body, name=name, pin=False, in_specs=[HBM_SPEC] * n + [SEM_SPEC] * 2, out_specs=[HBM_SPEC] * n,
        out_shape=[pltpu.HBM(l.shape, l.dtype) for l in lands], input_output_aliases={i: i for i in range(n)},
        compiler_params=pltpu.CompilerParams(has_side_effects=EFFECT),
    )(*lands, *sems)


def _exchange_copies(ins, outs, sems, w):
    send_sems, recv_sems, local_sems = sems
    x, y, c, _ = _place()
    mine = _block_of(x, y, c)
    local = pltpu.make_async_copy(ins[w].at[mine], outs[w].at[mine], local_sems.at[w])
    remote = []
    for k in range(1, N_DEV):
        px, py, pc = x ^ (k >> 2), y ^ ((k >> 1) & 1), c ^ (k & 1)
        remote.append(pltpu.make_async_remote_copy(
            src_ref=ins[w].at[_block_of(px, py, pc)], dst_ref=outs[w].at[mine], send_sem=send_sems.at[7 * w + k - 1],
            recv_sem=recv_sems.at[7 * w + k - 1], device_id=(px, py, pc), device_id_type=MESH))
    return local, remote


def _exchange_start(name, stacks):
    n = len(stacks)

    def body(*refs):
        ins, outs, sems = refs[:n], refs[n:2 * n], refs[2 * n:2 * n + 3]
        for w in range(n):
            local, remote = _exchange_copies(ins, outs, sems, w)
            local.start()
            for cp in remote:
                cp.start()
        refs[-1][...] = jnp.zeros_like(refs[-1])

    lands = [lax.empty(s.shape, s.dtype) for s in stacks]
    res = _chained(
        body, name=name, link=-1, pin=False, in_specs=[HBM_SPEC] * (2 * n),
        out_specs=[SEM_SPEC] * 3 + [HBM_SPEC] * (2 * n) + [pl.BlockSpec(memory_space=pltpu.VMEM)],
        out_shape=[pltpu.SemaphoreType.DMA((7 * n,)), pltpu.SemaphoreType.DMA((7 * n,)), pltpu.SemaphoreType.DMA((n,))]
        + [pltpu.HBM(s.shape, s.dtype) for s in stacks] * 2 + [TOKEN],
        input_output_aliases={i: 3 + i for i in range(2 * n)},
        compiler_params=pltpu.CompilerParams(has_side_effects=EFFECT),
    )(*[_in_hbm(s) for s in stacks], *[_in_hbm(l) for l in lands])
    return n, res[:3], res[3:3 + n], res[3 + n:3 + 2 * n]


def _exchange_wait(name, state):
    n, sems, stacks, lands = state

    def body(*refs):
        ins, outs, sems_in = refs[:n], refs[n:2 * n], refs[2 * n:2 * n + 3]
        for w in range(n):
            local, remote = _exchange_copies(ins, outs, sems_in, w)
            local.wait()
            for cp in remote:
                cp.wait_send()
                cp.wait_recv()

    return _chained(
        body, name=name, pin=False, in_specs=[HBM_SPEC] * (2 * n) + [SEM_SPEC] * 3, out_specs=[HBM_SPEC] * n,
        out_shape=[pltpu.HBM(l.shape, l.dtype) for l in lands], input_output_aliases={n + i: i for i in range(n)},
        compiler_params=pltpu.CompilerParams(has_side_effects=EFFECT),
    )(*stacks, *lands, *sems)


def _unstack_cols(w):
    return w.transpose(1, 0, 2).reshape(w.shape[1], N_DEV * w.shape[2])


def _stack_cols(w):
    return w.reshape(w.shape[0], N_DEV, w.shape[1] // N_DEV).transpose(1, 0, 2)


def _rope_tables(positions):
    half = MLA_ROPE // 2
    inv_freq = ROPE_THETA ** (-jnp.arange(half, dtype=F32) / half)
    ang = positions.astype(F32)[:, None] * inv_freq
    cos, sin = jnp.cos(ang), jnp.sin(ang)
    t = positions.shape[0]
    cos_t = jnp.concatenate([jnp.ones((t, MLA_NOPE), F32), cos, cos], axis=1)
    sin_t = jnp.concatenate([jnp.zeros((t, MLA_NOPE), F32), -sin, sin], axis=1)
    idx = jnp.arange(MLA_QK)
    partner = jnp.where(idx < MLA_NOPE, -1, jnp.where(idx < MLA_NOPE + half, idx + half, idx - half))
    swap = (idx[:, None] == partner[None, :]).astype(BF16)
    return cos_t, sin_t, swap


def _ffn_fwd(tag, x, gain, wt):
    t = x.shape[0]
    h = _rms_fwd(tag + "_rms", x, gain, BF16)
    g = _mm_stack_nt_out(tag + "_gate", h, wt[tag + '_w_gate'], BF16)
    u = _mm_stack_nt_out(tag + "_up", h, wt[tag + '_w_up'], BF16)
    f = g.shape[2]
    a = _swiglu_fwd(tag + "_act", g.reshape(N_DEV * t, f), u.reshape(N_DEV * t, f), BF16).reshape(N_DEV, t, f)
    y = _mm_stack_sum(tag + "_down", a, wt[tag + '_w_down'], F32, scale=0.5, res=x)
    return y, (x, h, g, u, a)


def _ffn_bwd_weights(tag, dy, saved, wd, comm):
    x, h, g, u, a = saved
    t = x.shape[0]
    f = wd.shape[1]
    da = _mm_stack_nt_out(tag + "_da", dy, wd, BF16, scale=0.5)
    comm.grads({tag + '_w_down': _mm_stack_tn_right(tag + "_dwd", a, dy, BF16, scale=0.5)})
    dg, du = _swiglu_bwd(tag + "_dact", da.reshape(N_DEV * t, f), g.reshape(N_DEV * t, f), u.reshape(N_DEV * t, f), BF16)
    dg, du = dg.reshape(N_DEV, t, f), du.reshape(N_DEV, t, f)
    comm.grads({tag + '_w_gate': _mm_stack_tn_right(tag + "_dwg", dg, h, BF16)})
    comm.grads({tag + '_w_up': _mm_stack_tn_right(tag + "_dwu", du, h, BF16)})
    return dg, du


def _ffn_bwd_input(tag, dy, saved, dgu, gain, wg, wu):
    dg, du = dgu
    dh = _mm_stack_sum(tag + "_dh_g", dg, wg, F32)
    dh = _mm_stack_sum(tag + "_dh_u", du, wu, F32, res=dh)
    return _rms_bwd(tag + "_drms", saved[0], gain, dh, res=dy)


def _local_step(x, mem, positions, sm, comm, target):
    t, d = x.shape
    nm = mem.shape[0]
    gs, gw = {}, {}

    class Weights(dict):
        def __missing__(self, name):
            self.update(comm.weights(next(k for k, names in enumerate(GATHERS) if name in names)))
            return self[name]

    wt = Weights()

    x1, ffn1_saved = _ffn_fwd("ffn1", x, sm['ffn1_norm'], wt)

    h2 = _rms_fwd("mix_rms", x1, sm['mix_norm'], BF16)
    w_in_ref = wt['w_in'].reshape(IN_WIDTH, d)
    pieces = []
    for name, n in IN_PAD:
        piece = w_in_ref[REF_OFF[name]:REF_OFF[name] + REF_SIZE[name]]
        if n != REF_SIZE[name]:
            piece = jnp.pad(piece, ((0, n - REF_SIZE[name]), (0, 0)))
        pieces.append(piece)
    w_in = jnp.concatenate(pieces, axis=0)
    z = _mm2("mix_in", h2, w_in, NT, F32)
    zs = {name: z[:, PAD_OFF[name]:PAD_OFF[name] + n] for name, n in IN_PAD}

    cq = _rms_fwd("mla_q_a_rms", zs['zq'], sm['q_a_norm'], BF16)
    q_raw = _mm_stack_nt_out("mla_q_up", cq, wt['w_q_up'], F32)
    ckv = _rms_fwd("mla_kv_a_rms", zs['zkv'], sm['kv_a_norm'], BF16)
    kv = _mm_stack_out("mla_kv_up", ckv, wt['w_kv_up'], F32)
    zkr = zs['zkr'][:, :MLA_ROPE]
    k_raw = jnp.concatenate([kv[:, :, :MLA_NOPE], jnp.broadcast_to(zkr[None], (MLA_HEADS, t, MLA_ROPE))], axis=2)
    v_mla = kv[:, :, MLA_NOPE:].astype(BF16)
    cos_t, sin_t, swap = _rope_tables(positions)
    q_raw2, k_raw2 = q_raw.reshape(MLA_HEADS * t, MLA_QK), k_raw.reshape(MLA_HEADS * t, MLA_QK)
    qn = _rms_fwd("mla_q_rms", q_raw2, sm['mla_q_norm'], F32)
    kn = _rms_fwd("mla_k_rms", k_raw2, sm['mla_k_norm'], F32)
    qf = _rope("mla_q_rope", qn, cos_t, sin_t, swap, BF16, False).reshape(MLA_HEADS, t, MLA_QK)
    kf = _rope("mla_k_rope", kn, cos_t, sin_t, swap, BF16, False).reshape(MLA_HEADS, t, MLA_QK)
    o_mla = _attn_fwd("mla_attn", qf, kf, v_mla, MLA_QK ** -0.5, True)

    w_g2 = jnp.pad(_unstack_cols(wt['gla_w_gate2']), ((0, LANES - GLA_GATE_RANK), (0, 0)))
    pre = _mm2("gla_gate_pre", zs['zg'], w_g2, NN, F32)
    log_a = _gate_fwd("gla_gate", pre, sm['gla_b_gate'])
    o_gla_raw, states = _gla_fwd("gla_scan", zs['gq'], zs['gk'], zs['gv'], log_a, GLA_HEADS)
    o_gla_n = _rms_fwd("gla_out_rms", o_gla_raw, sm['gla_out_norm'], F32)
    o_gla = _swiglu_fwd("gla_out_gate", zs['zr'], o_gla_n, BF16)

    cat = jnp.concatenate([o_mla, o_gla], axis=1)
    w_out = wt['w_out'].reshape(d, d)
    x2 = _mm2("mix_out", cat, w_out, NN, F32, res=x1)

    w_mq, w_mk, w_mv = (wt[n].reshape(d, MEM_HEADS * MEM_HEAD_DIM) for n in ('mem_w_q', 'mem_w_k', 'mem_w_v'))
    hq = _rms_fwd("mem_attn_rms", x2, sm['mem_attn_norm'], BF16)
    hm = _rms_fwd("mem_rms", mem, sm['mem_norm'], BF16)

    def heads_out(name, a, b, out_dtype):
        m, kk = a.shape
        tm = _tile(m, 512)
        return _mm(name, a, b, (m // tm, MEM_HEADS, 1), ((tm, kk), lambda i, h, k: (i, 0)),
                   ((kk, MEM_HEAD_DIM), lambda i, h, k: (0, h)), ((None, tm, MEM_HEAD_DIM), lambda i, h, k: (h, i, 0)),
                   (MEM_HEADS, m, MEM_HEAD_DIM), out_dtype, NN)

    mq_raw = heads_out("mem_q", hq, w_mq, F32)
    mk_raw = heads_out("mem_k", hm, w_mk, F32)
    mv = heads_out("mem_v", hm, w_mv, BF16)
    mq = _rms_fwd("mem_q_rms", mq_raw.reshape(MEM_HEADS * t, MEM_HEAD_DIM), sm['mem_q_norm'], BF16)
    mk = _rms_fwd("mem_k_rms", mk_raw.reshape(MEM_HEADS * nm, MEM_HEAD_DIM), sm['mem_k_norm'], BF16)
    mq, mk = mq.reshape(MEM_HEADS, t, MEM_HEAD_DIM), mk.reshape(MEM_HEADS, nm, MEM_HEAD_DIM)
    o_mem = _attn_fwd("mem_attn", mq, mk, mv, MEM_HEAD_DIM ** -0.5, False)
    w_mo = wt['mem_w_o']
    mo_cols = w_mo.shape[2]
    tm = _tile(t, 512)
    x3 = _mm("mem_out", o_mem, w_mo, (t // tm, N_DEV, 1), ((tm, o_mem.shape[1]), lambda i, j, k: (i, 0)),
             ((None, o_mem.shape[1], mo_cols), lambda i, j, k: (j, 0, 0)), ((tm, mo_cols), lambda i, j, k: (i, j)),
             (t, d), F32, NN, res=x2)

    y, ffn2_saved = _ffn_fwd("ffn2", x3, sm['ffn2_norm'], wt)
    dy, loss_lanes = _loss("loss", y, target)

    dgu = _ffn_bwd_weights("ffn2", dy, ffn2_saved, wt['ffn2_w_down'], comm)
    dx3, gs['ffn2_norm'] = _ffn_bwd_input("ffn2", dy, ffn2_saved, dgu, sm['ffn2_norm'],
                                          wt['ffn2_w_gate'], wt['ffn2_w_up'])

    do_mem = _mm("mem_do", dx3, w_mo, (t // tm, MEM_HEADS, N_DEV), ((tm, mo_cols), lambda i, h, k: (i, k)),
                 ((None, MEM_HEAD_DIM, mo_cols), lambda i, h, k: (k, h, 0)), ((tm, MEM_HEAD_DIM), lambda i, h, k: (i, h)),
                 (t, MEM_HEADS * MEM_HEAD_DIM), BF16, NT)
    tk = _tile(t, 512)
    gw['mem_w_o'] = _mm("mem_dwo", o_mem, dx3, (N_DEV, 1, t // tk), ((tk, o_mem.shape[1]), lambda j, i, k: (k, 0)),
                        ((tk, mo_cols), lambda j, i, k: (k, j)), ((None, o_mem.shape[1], mo_cols), lambda j, i, k: (j, 0, 0)),
                        w_mo.shape, BF16, TN)
    dmq, dmk, dmv = _attn_bwd("mem_dattn", mq, mk, mv, do_mem, MEM_HEAD_DIM ** -0.5, False)
    dmq_raw, gs['mem_q_norm'] = _rms_bwd("mem_q_drms", mq_raw.reshape(MEM_HEADS * t, MEM_HEAD_DIM), sm['mem_q_norm'],
                                         dmq.reshape(MEM_HEADS * t, MEM_HEAD_DIM))
    dmk_raw, gs['mem_k_norm'] = _rms_bwd("mem_k_drms", mk_raw.reshape(MEM_HEADS * nm, MEM_HEAD_DIM), sm['mem_k_norm'],
                                         dmk.reshape(MEM_HEADS * nm, MEM_HEAD_DIM))
    dmq_raw = dmq_raw.reshape(MEM_HEADS, t, MEM_HEAD_DIM)
    dmk_raw = dmk_raw.reshape(MEM_HEADS, nm, MEM_HEAD_DIM)

    def heads_in_nt(name, a, b, res=None):
        m, n = a.shape[1], b.shape[0]
        tm_, tn_ = _tile(m, 512), _tile(n, 1024)
        return _mm(name, a, b, (m // tm_, n // tn_, MEM_HEADS), ((None, tm_, MEM_HEAD_DIM), lambda i, j, k: (k, i, 0)),
                   ((tn_, MEM_HEAD_DIM), lambda i, j, k: (j, k)), ((tm_, tn_), lambda i, j, k: (i, j)), (m, n), F32, NT,
                   None, res)

    def heads_tn(name, a, b):
        m, kp = a.shape
        tm_, tk_ = _tile(kp, 1024), _tile(m, 512)
        return _mm(name, a, b, (kp // tm_, MEM_HEADS, m // tk_), ((tk_, tm_), lambda i, h, k: (k, i)),
                   ((None, tk_, MEM_HEAD_DIM), lambda i, h, k: (h, k, 0)), ((tm_, MEM_HEAD_DIM), lambda i, h, k: (i, h)),
                   (kp, MEM_HEADS * MEM_HEAD_DIM), BF16, TN)

    dhq = heads_in_nt("mem_dhq", dmq_raw, w_mq)
    gw['mem_w_q'] = heads_tn("mem_dwq", hq, dmq_raw).reshape(wt['mem_w_q'].shape)
    dhm = heads_in_nt("mem_dhm_k", dmk_raw, w_mk)
    dhm = heads_in_nt("mem_dhm_v", dmv, w_mv, res=dhm)
    gw['mem_w_k'] = heads_tn("mem_dwk", hm, dmk_raw).reshape(wt['mem_w_k'].shape)
    gw['mem_w_v'] = heads_tn("mem_dwv", hm, dmv).reshape(wt['mem_w_v'].shape)
    _, gs['mem_norm'] = _rms_bwd("mem_drms", mem, sm['mem_norm'], dhm)
    comm.grads({n: gw[n] for n in MEMORY})
    dx2, gs['mem_attn_norm'] = _rms_bwd("mem_attn_drms", x2, sm['mem_attn_norm'], dhq, res=dx3)

    dcat = _mm2("mix_dcat", dx2, w_out, NT, F32)
    gw['w_out'] = _mm2("mix_dwout", cat, dx2, TN, BF16, tm=1024, tn=2048, tk=512).reshape(wt['w_out'].shape)
    do_mla, do_gla = dcat[:, :MLA_HEADS * MLA_V], dcat[:, MLA_HEADS * MLA_V:]

    dzr, dgn = _swiglu_bwd("gla_out_dgate", do_gla, zs['zr'], o_gla_n, F32)
    do_gla_raw, gs['gla_out_norm'] = _rms_bwd("gla_out_drms", o_gla_raw, sm['gla_out_norm'], dgn)
    dgq, dgk, dgv, dlog_a = _gla_bwd("gla_dscan", zs['gq'], zs['gk'], zs['gv'], log_a, states, do_gla_raw, GLA_HEADS)
    dpre, gs['gla_b_gate'] = _gate_bwd("gla_dgate", pre, sm['gla_b_gate'], dlog_a)
    dw_g2 = _mm2("gla_dwgate", zs['zg'], dpre, TN, BF16, tk=512)
    gw['gla_w_gate2'] = _stack_cols(dw_g2[:GLA_GATE_RANK])
    dzg = _mm2("gla_dzg", dpre, w_g2, NT, F32)

    dqf, dkf, dv_mla = _attn_bwd("mla_dattn", qf, kf, v_mla, do_mla, MLA_QK ** -0.5, True)
    dqn = _rope("mla_q_drope", dqf.reshape(MLA_HEADS * t, MLA_QK), cos_t, sin_t, swap, F32, True)
    dkn = _rope("mla_k_drope", dkf.reshape(MLA_HEADS * t, MLA_QK), cos_t, sin_t, swap, F32, True)
    dq_raw, gs['mla_q_norm'] = _rms_bwd("mla_q_drms", q_raw2, sm['mla_q_norm'], dqn)
    dk_raw, gs['mla_k_norm'] = _rms_bwd("mla_k_drms", k_raw2, sm['mla_k_norm'], dkn)
    dq_raw = dq_raw.reshape(MLA_HEADS, t, MLA_QK)
    dk_raw = dk_raw.reshape(MLA_HEADS, t, MLA_QK)
    dkv = jnp.concatenate([dk_raw[:, :, :MLA_NOPE], dv_mla], axis=2)
    dzkr = jnp.sum(dk_raw[:, :, MLA_NOPE:], axis=0)
    gw['w_q_up'] = _mm_stack_tn_right("mla_dwq", dq_raw, cq, BF16)
    gw['w_kv_up'] = _mm_stack_tn_left("mla_dwkv", ckv, dkv, BF16)
    dcq = _mm_stack_sum("mla_dcq", dq_raw, wt['w_q_up'], F32)
    dckv = _mm_stack_nt_sum("mla_dckv", dkv, wt['w_kv_up'], F32)
    dzq, gs['q_a_norm'] = _rms_bwd("mla_q_a_drms", zs['zq'], sm['q_a_norm'], dcq)
    dzkv, gs['kv_a_norm'] = _rms_bwd("mla_kv_a_drms", zs['zkv'], sm['kv_a_norm'], dckv)

    dzs = {'zq': dzq, 'zkv': dzkv, 'gq': dgq, 'gk': dgk, 'gv': dgv, 'zr': dzr,
           'zkr': jnp.pad(dzkr, ((0, 0), (0, LANES - MLA_ROPE))), 'zg': dzg}
    dz = jnp.concatenate([dzs[name].astype(BF16) for name, _ in IN_PAD], axis=1)
    dw_in = _mm2("mix_dwin", dz, h2, TN, BF16, tm=1024, tn=2048, tk=512)
    dw_in_ref = jnp.concatenate([dw_in[PAD_OFF[name]:PAD_OFF[name] + n] for name, n in IN_REF], axis=0)
    gw['w_in'] = dw_in_ref.reshape(wt['w_in'].shape)
    comm.grads({n: gw[n] for n in MIXER})
    dh2 = _mm2("mix_dh", dz, w_in, NN, F32)
    dx1, gs['mix_norm'] = _rms_bwd("mix_drms", x1, sm['mix_norm'], dh2, res=dx2)

    dgu = _ffn_bwd_weights("ffn1", dx1, ffn1_saved, wt['ffn1_w_down'], comm)
    grad_x, gs['ffn1_norm'] = _ffn_bwd_input("ffn1", dx1, ffn1_saved, dgu, sm['ffn1_norm'],
                                             wt['ffn1_w_gate'], wt['ffn1_w_up'])
    return loss_lanes, grad_x, gs


def _pad_lanes(v):
    n = v.shape[1]
    return jnp.pad(v, ((0, 0), (0, -n % LANES)))


def _pack_small(vals):
    return jnp.concatenate([_pad_lanes(vals[n]) for n in SMALL], axis=1)


def _unpack_small(packed, like):
    out, off = {}, 0
    for n in SMALL:
        size = like[n].shape[1]
        out[n] = packed[:, off:off + size]
        off += size + (-size % LANES)
    return out


def kernel(x, mem, positions, ffn1_norm, ffn1_w_gate, ffn1_w_up, ffn1_w_down, mix_norm, w_in, q_a_norm, w_q_up, kv_a_norm, w_kv_up, mla_q_norm, mla_k_norm, gla_w_gate2, gla_b_gate, gla_out_norm, w_out, mem_attn_norm, mem_norm, mem_w_q, mem_w_k, mem_w_v, mem_w_o, mem_q_norm, mem_k_norm, ffn2_norm, ffn2_w_gate, ffn2_w_up, ffn2_w_down, loss_target, m_ffn1_norm, m_ffn1_w_gate, m_ffn1_w_up, m_ffn1_w_down, m_mix_norm, m_w_in, m_q_a_norm, m_w_q_up, m_kv_a_norm, m_w_kv_up, m_mla_q_norm, m_mla_k_norm, m_gla_w_gate2, m_gla_b_gate, m_gla_out_norm, m_w_out, m_mem_attn_norm, m_mem_norm, m_mem_w_q, m_mem_w_k, m_mem_w_v, m_mem_w_o, m_mem_q_norm, m_mem_k_norm, m_ffn2_norm, m_ffn2_w_gate, m_ffn2_w_up, m_ffn2_w_down, v_ffn1_norm, v_ffn1_w_gate, v_ffn1_w_up, v_ffn1_w_down, v_mix_norm, v_w_in, v_q_a_norm, v_w_q_up, v_kv_a_norm, v_w_kv_up, v_mla_q_norm, v_mla_k_norm, v_gla_w_gate2, v_gla_b_gate, v_gla_out_norm, v_w_out, v_mem_attn_norm, v_mem_norm, v_mem_w_q, v_mem_w_k, v_mem_w_v, v_mem_w_o, v_mem_q_norm, v_mem_k_norm, v_ffn2_norm, v_ffn2_w_gate, v_ffn2_w_up, v_ffn2_w_down):
    inp = dict(locals())
    x, mem, positions, target = inp['x'][0], inp['mem'][0], inp['positions'][0], inp['loss_target'][0]
    sm = {n: inp[n] for n in SMALL}
    out = {}

    def stored(key):
        name = key[2:] if key[:2] in ('m_', 'v_') else key
        if name == 'w_in':
            return jnp.transpose(inp[key], (2, 0, 1))
        return inp[key][0].T if name in TRANSPOSED else inp[key][0]

    def as_given(name, r):
        if name == 'w_in':
            return jnp.transpose(r, (1, 2, 0))
        return r.T[None] if name in TRANSPOSED else r[None]

    class Comm:
        def __init__(self):
            self.gathers = {0: self.start(0)}
            self.forwards, self.exchanges = {}, []

        def start(self, k):
            return _gather_start(f"gather_start_{k}", [stored(n).astype(BF16) for n in GATHERS[k]])

        def forward(self, k):
            if k not in self.forwards:
                self.forwards[k] = _gather_mid(f"gather_mid_{k}", self.gathers[k])
                self.gathers.update({nxt: self.start(nxt) for nxt in NEXT_GATHERS.get(k, [])})

        def weights(self, k):
            self.forward(k)
            if k in EARLY_FORWARD:
                self.forward(EARLY_FORWARD[k])
            return dict(zip(GATHERS[k], _gather_end(f"gather_end_{k}", self.forwards[k])))

        def grads(self, stacks):
            names = list(stacks)
            self.exchanges.append((names, _exchange_start("exchange_start_" + names[0], [stacks[n] for n in names])))

        def update(self, count):
            todo, self.exchanges = self.exchanges[:count], self.exchanges[count:]
            for names, state in todo:
                for n, p in zip(names, _exchange_wait("exchange_wait_" + names[0], state)):
                    res = _adamw("adamw_" + n, stored(n), stored('m_' + n), stored('v_' + n), p)
                    for kind, r in zip(('grad_', 'delta_', 'new_m_', 'new_v_'), res):
                        out[kind + n] = as_given(n, r)

    _Chain.last = None
    comm = Comm()
    loss_lanes, grad_x, gs = _local_step(x, mem, positions, sm, comm, target)
    out['loss'] = lax.psum(jnp.sum(loss_lanes), ("x", "y", "c"))
    out['grad_x'] = grad_x[None]

    comm.update(5)
    small_parts = _all_gather("gather_small", [_pack_small(gs)])[0]
    res = _adamw("adamw_small", _pack_small(sm), _pack_small({n: inp['m_' + n] for n in SMALL}),
                 _pack_small({n: inp['v_' + n] for n in SMALL}), small_parts)
    for kind, r in zip(('grad_', 'delta_', 'new_m_', 'new_v_'), res):
        for n, val in _unpack_small(r, sm).items():
            out[kind + n] = val

    comm.update(3)

    names = ['loss', 'grad_x'] + [k + n for k in ('grad_', 'delta_', 'new_m_', 'new_v_') for n in WEIGHTS]
    return tuple(out[n] for n in names)
```

```python
import math

import jax
import jax.numpy as jnp
from jax import lax
from jax.experimental import pallas as pl
from jax.experimental.pallas import tpu as pltpu

F32 = jnp.float32
BF16 = jnp.bfloat16

N_DEV = 8
EPS = 1e-6
CHUNK = 64
MLA_HEADS, MLA_NOPE, MLA_ROPE, MLA_V = 8, 128, 64, 128
MLA_QK = MLA_NOPE + MLA_ROPE
MLA_Q_RANK, MLA_KV_RANK = 512, 256
ROPE_THETA = 10000.0
GLA_HEADS, GLA_DK, GLA_DV, GLA_GATE_RANK = 4, 128, 256, 16
GLA_TAU = 16.0
MEM_HEADS, MEM_HEAD_DIM = 4, 128
ADAM_LR, ADAM_B1, ADAM_B2, ADAM_EPS, ADAM_WD, ADAM_STEP = 0.001, 0.9, 0.999, 1e-08, 0.01, 10

V7X_VMEM_BYTES = 64 * 1024 * 1024
LANES = 128

NN = (((1,), (0,)), ((), ()))
NT = (((1,), (1,)), ((), ()))
TN = (((0,), (0,)), ((), ()))

WEIGHTS = ['ffn1_norm', 'ffn1_w_gate', 'ffn1_w_up', 'ffn1_w_down', 'mix_norm', 'w_in', 'q_a_norm', 'w_q_up',
           'kv_a_norm', 'w_kv_up', 'mla_q_norm', 'mla_k_norm', 'gla_w_gate2', 'gla_b_gate', 'gla_out_norm', 'w_out',
           'mem_attn_norm', 'mem_norm', 'mem_w_q', 'mem_w_k', 'mem_w_v', 'mem_w_o', 'mem_q_norm', 'mem_k_norm',
           'ffn2_norm', 'ffn2_w_gate', 'ffn2_w_up', 'ffn2_w_down']
SMALL = ['ffn1_norm', 'mix_norm', 'q_a_norm', 'kv_a_norm', 'mla_q_norm', 'mla_k_norm', 'gla_b_gate', 'gla_out_norm',
         'mem_attn_norm', 'mem_norm', 'mem_q_norm', 'mem_k_norm', 'ffn2_norm']
MIXER = ['w_in', 'w_q_up', 'w_kv_up', 'gla_w_gate2', 'w_out']
MEMORY = ['mem_w_q', 'mem_w_k', 'mem_w_v', 'mem_w_o']
TRANSPOSED = ['ffn1_w_gate', 'ffn1_w_up', 'ffn2_w_gate', 'ffn2_w_up', 'w_q_up']
GATHERS = [['ffn1_w_gate'], ['ffn1_w_up'], ['ffn1_w_down'], MIXER, MEMORY, ['ffn2_w_gate', 'ffn2_w_up', 'ffn2_w_down']]
NEXT_GATHERS = {0: [1], 1: [2], 2: [3], 3: [4, 5]}
EARLY_FORWARD = {4: 5}

IN_REF = [('zq', 512), ('zkv', 256), ('zkr', 64), ('gq', 512), ('gk', 512), ('gv', 1024), ('zg', 16), ('zr', 1024)]
IN_PAD = [('zq', 512), ('zkv', 256), ('gq', 512), ('gk', 512), ('gv', 1024), ('zr', 1024), ('zkr', 128), ('zg', 128)]
IN_WIDTH = sum(n for _, n in IN_REF)
IN_PAD_WIDTH = sum(n for _, n in IN_PAD)


def _offsets(layout):
    out, off = {}, 0
    for name, n in layout:
        out[name] = off
        off += n
    return out


REF_OFF, PAD_OFF = _offsets(IN_REF), _offsets(IN_PAD)
REF_SIZE = dict(IN_REF)


def _tile(n, pref):
    return pref if n % pref == 0 else n


def _block_bytes(blk, dtype):
    dims = [d for d in blk if d is not None]
    if len(dims) >= 1:
        dims[-1] = -(-dims[-1] // LANES) * LANES
    return math.prod(dims) * jnp.dtype(dtype).itemsize


def _vmem_limit(pipelined_bytes, resident_bytes=0):
    need = 2 * pipelined_bytes + resident_bytes + (8 << 20)
    return int(min(max(need, 32 << 20), V7X_VMEM_BYTES - (6 << 20)))


class _Chain:
    last = None


def _chained(body, *, in_specs, link=0, **kwargs):
    def call(*operands):
        dep = _Chain.last
        if dep is not None and any(o is dep for o in operands):
            dep = None
        if dep is None:
            res = pl.pallas_call(body, in_specs=in_specs, **kwargs)(*operands)
        else:
            n = len(operands)

            def chained_body(*refs):
                body(*refs[:n], *refs[n + 1:])

            res = pl.pallas_call(chained_body, in_specs=list(in_specs) + [pl.BlockSpec(memory_space=pl.ANY)],
                                 **kwargs)(*operands, dep)
        _Chain.last = res[link] if isinstance(res, (list, tuple)) else res
        return res

    return call


def _rowwise_params(*semantics):
    return pltpu.CompilerParams(dimension_semantics=semantics, vmem_limit_bytes=48 << 20)


def _mm(name, a, b, grid, a_spec, b_spec, o_spec, out_shape, out_dtype, dims, scale=None, res=None):
    nk = grid[2]
    (a_blk, a_map), (b_blk, b_map), (o_blk, o_map) = a_spec, b_spec, o_spec
    acc_shape = tuple(d for d in o_blk if d is not None)

    def body(*refs):
        if res is None:
            a_ref, b_ref, o_ref = refs[:3]
            r_ref, rest = None, refs[3:]
        else:
            a_ref, b_ref, r_ref, o_ref = refs[:4]
            rest = refs[4:]

        def product():
            return lax.dot_general(a_ref[...].astype(BF16), b_ref[...].astype(BF16), dims, preferred_element_type=F32)

        def finish(r):
            if scale is not None:
                r = r * scale
            if r_ref is not None:
                r = r + r_ref[...].astype(F32)
            o_ref[...] = r.astype(o_ref.dtype)

        if nk == 1:
            finish(product())
        else:
            acc = rest[0]
            k = pl.program_id(2)

            @pl.when(k == 0)
            def _():
                acc[...] = product()

            @pl.when(k > 0)
            def _():
                acc[...] += product()

            @pl.when(k == nk - 1)
            def _():
                finish(acc[...])

    in_specs = [pl.BlockSpec(a_blk, a_map), pl.BlockSpec(b_blk, b_map)]
    operands = [a, b]
    piped = _block_bytes(a_blk, a.dtype) + _block_bytes(b_blk, b.dtype) + _block_bytes(o_blk, out_dtype)
    if res is not None:
        in_specs.append(pl.BlockSpec(o_blk, o_map))
        operands.append(res)
        piped += _block_bytes(o_blk, res.dtype)
    scratch = [pltpu.VMEM(acc_shape, F32)] if nk > 1 else []
    return _chained(
        body, name=name, grid=grid, in_specs=in_specs, out_specs=pl.BlockSpec(o_blk, o_map),
        out_shape=jax.ShapeDtypeStruct(out_shape, out_dtype), scratch_shapes=scratch,
        compiler_params=pltpu.CompilerParams(
            dimension_semantics=("parallel", "parallel", "arbitrary"),
            vmem_limit_bytes=_vmem_limit(piped, 3 * _block_bytes(acc_shape, F32))),
    )(*operands)


def _mm2(name, a, b, dims, out_dtype, tm=512, tn=1024, tk=2048, scale=None, res=None):
    if dims is NN:
        (m, kk), n = a.shape, b.shape[1]
    elif dims is NT:
        (m, kk), n = a.shape, b.shape[0]
    else:
        (kk, m), n = a.shape, b.shape[1]
    tm, tn, tk = _tile(m, tm), _tile(n, tn), _tile(kk, tk)
    a_spec = ((tk, tm), lambda i, j, k: (k, i)) if dims is TN else ((tm, tk), lambda i, j, k: (i, k))
    b_spec = ((tn, tk), lambda i, j, k: (j, k)) if dims is NT else ((tk, tn), lambda i, j, k: (k, j))
    return _mm(name, a, b, (m // tm, n // tn, kk // tk), a_spec, b_spec, ((tm, tn), lambda i, j, k: (i, j)),
               (m, n), out_dtype, dims, scale, res)


def _mm_stack_out(name, a, b, out_dtype, tm=512, tk=2048):
    (m, kk), (nj, _, n) = a.shape, b.shape
    tm, tk = _tile(m, tm), _tile(kk, tk)
    return _mm(name, a, b, (nj, m // tm, kk // tk), ((tm, tk), lambda j, i, k: (i, k)),
               ((None, tk, n), lambda j, i, k: (j, k, 0)), ((None, tm, n), lambda j, i, k: (j, i, 0)),
               (nj, m, n), out_dtype, NN)


def _mm_stack_nt_out(name, a, b, out_dtype, scale=None, tm=512, tk=2048):
    (m, kk), (nj, n, _) = a.shape, b.shape
    tm, tk = _tile(m, tm), _tile(kk, tk)
    return _mm(name, a, b, (nj, m // tm, kk // tk), ((tm, tk), lambda j, i, k: (i, k)),
               ((None, n, tk), lambda j, i, k: (j, 0, k)), ((None, tm, n), lambda j, i, k: (j, i, 0)),
               (nj, m, n), out_dtype, NT, scale)


def _mm_stack_sum(name, a, b, out_dtype, scale=None, res=None, tm=512, tn=1024):
    (nj, m, f), n = a.shape, b.shape[2]
    tm, tn = _tile(m, tm), _tile(n, tn)
    return _mm(name, a, b, (m // tm, n // tn, nj), ((None, tm, f), lambda i, j, k: (k, i, 0)),
               ((None, f, tn), lambda i, j, k: (k, 0, j)), ((tm, tn), lambda i, j, k: (i, j)),
               (m, n), out_dtype, NN, scale, res)


def _mm_stack_nt_sum(name, a, b, out_dtype, res=None, tm=512, tn=1024):
    (nj, m, f), n = a.shape, b.shape[1]
    tm, tn = _tile(m, tm), _tile(n, tn)
    return _mm(name, a, b, (m // tm, n // tn, nj), ((None, tm, f), lambda i, j, k: (k, i, 0)),
               ((None, tn, f), lambda i, j, k: (k, j, 0)), ((tm, tn), lambda i, j, k: (i, j)),
               (m, n), out_dtype, NT, None, res)


def _mm_stack_tn_left(name, a, b, out_dtype, tm=1024, tk=512):
    (m, kp), (nj, _, n) = a.shape, b.shape
    tm, tk = _tile(kp, tm), _tile(m, tk)
    return _mm(name, a, b, (nj, kp // tm, m // tk), ((tk, tm), lambda j, i, k: (k, i)),
               ((None, tk, n), lambda j, i, k: (j, k, 0)), ((None, tm, n), lambda j, i, k: (j, i, 0)),
               (nj, kp, n), out_dtype, TN)


def _mm_stack_tn_right(name, a, b, out_dtype, scale=None, tn=2048, tk=512):
    (nj, m, f), n = a.shape, b.shape[1]
    tn, tk = _tile(n, tn), _tile(m, tk)
    return _mm(name, a, b, (nj, n // tn, m // tk), ((None, tk, f), lambda j, i, k: (j, k, 0)),
               ((tk, tn), lambda j, i, k: (k, i)), ((None, f, tn), lambda j, i, k: (j, 0, i)),
               (nj, f, n), out_dtype, TN, scale)


def _rms_fwd(name, x, g, out_dtype, tm=256):
    rows, cols = x.shape
    d = g.shape[1]
    tm = _tile(rows, tm)

    def body(x_ref, g_ref, o_ref):
        xf = x_ref[...].astype(F32)
        r = lax.rsqrt(jnp.mean(xf * xf, axis=-1, keepdims=True) + EPS)
        o_ref[...] = (xf * r * g_ref[...]).astype(o_ref.dtype)

    return _chained(
        body, name=name, grid=(rows // tm, cols // d),
        in_specs=[pl.BlockSpec((tm, d), lambda i, c: (i, c)), pl.BlockSpec((1, d), lambda i, c: (0, 0))],
        out_specs=pl.BlockSpec((tm, d), lambda i, c: (i, c)),
        out_shape=jax.ShapeDtypeStruct((rows, cols), out_dtype),
        compiler_params=_rowwise_params("parallel", "parallel"),
    )(x, g)


def _rms_bwd(name, x, g, dy, res=None, tm=256):
    rows, cols = x.shape
    d = g.shape[1]
    tm = _tile(rows, tm)

    def body(*refs):
        if res is None:
            x_ref, g_ref, dy_ref, dx_ref, dg_ref = refs
            r_ref = None
        else:
            x_ref, g_ref, dy_ref, r_ref, dx_ref, dg_ref = refs
        xf = x_ref[...].astype(F32)
        r = lax.rsqrt(jnp.mean(xf * xf, axis=-1, keepdims=True) + EPS)
        xhat = xf * r
        dyf = dy_ref[...].astype(F32)

        @pl.when((pl.program_id(0) == 0) & (pl.program_id(1) == 0))
        def _():
            dg_ref[...] = jnp.zeros_like(dg_ref)

        dg_ref[...] += jnp.sum(dyf * xhat, axis=0, keepdims=True)
        dxh = dyf * g_ref[...]
        dx = r * (dxh - xhat * jnp.mean(dxh * xhat, axis=-1, keepdims=True))
        if r_ref is not None:
            dx = dx + r_ref[...].astype(F32)
        dx_ref[...] = dx

    blk = pl.BlockSpec((tm, d), lambda i, c: (i, c))
    in_specs = [blk, pl.BlockSpec((1, d), lambda i, c: (0, 0)), blk]
    operands = [x, g, dy]
    if res is not None:
        in_specs.append(blk)
        operands.append(res)
    return _chained(
        body, name=name, grid=(rows // tm, cols // d), in_specs=in_specs,
        out_specs=(blk, pl.BlockSpec((1, d), lambda i, c: (0, 0))),
        out_shape=(jax.ShapeDtypeStruct((rows, cols), F32), jax.ShapeDtypeStruct((1, d), F32)),
        compiler_params=_rowwise_params("arbitrary", "arbitrary"),
    )(*operands)


def _swiglu_fwd(name, g, u, out_dtype, tm=256):
    rows, cols = g.shape
    tm = _tile(rows, tm)

    def body(g_ref, u_ref, o_ref):
        gf = g_ref[...].astype(F32)
        o_ref[...] = (gf * jax.nn.sigmoid(gf) * u_ref[...].astype(F32)).astype(o_ref.dtype)

    blk = pl.BlockSpec((tm, cols), lambda i: (i, 0))
    return _chained(
        body, name=name, grid=(rows // tm,), in_specs=[blk, blk], out_specs=blk,
        out_shape=jax.ShapeDtypeStruct((rows, cols), out_dtype),
        compiler_params=_rowwise_params("parallel"),
    )(g, u)


def _swiglu_bwd(name, da, g, u, out_dtype, tm=256):
    rows, cols = g.shape
    tm = _tile(rows, tm)

    def body(da_ref, g_ref, u_ref, dg_ref, du_ref):
        gf = g_ref[...].astype(F32)
        daf = da_ref[...].astype(F32)
        sig = jax.nn.sigmoid(gf)
        du_ref[...] = (daf * gf * sig).astype(du_ref.dtype)
        dg_ref[...] = (daf * u_ref[...].astype(F32) * sig * (1.0 + gf * (1.0 - sig))).astype(dg_ref.dtype)

    blk = pl.BlockSpec((tm, cols), lambda i: (i, 0))
    sds = jax.ShapeDtypeStruct((rows, cols), out_dtype)
    return _chained(
        body, name=name, grid=(rows // tm,), in_specs=[blk, blk, blk], out_specs=(blk, blk), out_shape=(sds, sds),
        compiler_params=_rowwise_params("parallel"),
    )(da, g, u)


def _split3(x):
    hi = x.astype(BF16)
    r1 = x - hi.astype(F32)
    mid = r1.astype(BF16)
    lo = (r1 - mid.astype(F32)).astype(BF16)
    return hi, mid, lo


def _rope(name, x, cos_t, sin_t, swap, out_dtype, backward, tm=256):
    rows, d = x.shape
    t = cos_t.shape[0]
    tm = _tile(t, tm)
    nt = t // tm

    def body(x_ref, c_ref, s_ref, p_ref, o_ref):
        xf = x_ref[...].astype(F32)
        p = p_ref[...]
        lhs = xf * s_ref[...] if backward else xf
        sw = sum(jnp.dot(part, p, preferred_element_type=F32) for part in _split3(lhs))
        if not backward:
            sw = sw * s_ref[...]
        o_ref[...] = (xf * c_ref[...] + sw).astype(o_ref.dtype)

    blk = pl.BlockSpec((tm, d), lambda i: (i, 0))
    tab = pl.BlockSpec((tm, d), lambda i: (i % nt, 0))
    return _chained(
        body, name=name, grid=(rows // tm,), in_specs=[blk, tab, tab, pl.BlockSpec((d, d), lambda i: (0, 0))],
        out_specs=blk, out_shape=jax.ShapeDtypeStruct((rows, d), out_dtype),
        compiler_params=_rowwise_params("parallel"),
    )(x, cos_t, sin_t, swap)


def _gate_fwd(name, pre, bias, tm=256):
    rows, cols = pre.shape
    tm = _tile(rows, tm)

    def body(p_ref, b_ref, o_ref):
        z = p_ref[...] + b_ref[...]
        o_ref[...] = (jnp.minimum(z, 0.0) - jnp.log(1.0 + jnp.exp(-jnp.abs(z)))) * (1.0 / GLA_TAU)

    blk = pl.BlockSpec((tm, cols), lambda i: (i, 0))
    return _chained(
        body, name=name, grid=(rows // tm,), in_specs=[blk, pl.BlockSpec((1, cols), lambda i: (0, 0))], out_specs=blk,
        out_shape=jax.ShapeDtypeStruct((rows, cols), F32),
        compiler_params=_rowwise_params("parallel"),
    )(pre, bias)


def _gate_bwd(name, pre, bias, dla, tm=256):
    rows, cols = pre.shape
    tm = _tile(rows, tm)

    def body(p_ref, b_ref, d_ref, dp_ref, db_ref):
        z = p_ref[...] + b_ref[...]
        dp = d_ref[...] * (1.0 / GLA_TAU) / (1.0 + jnp.exp(z))
        dp_ref[...] = dp

        @pl.when(pl.program_id(0) == 0)
        def _():
            db_ref[...] = jnp.zeros_like(db_ref)

        db_ref[...] += jnp.sum(dp, axis=0, keepdims=True)

    blk = pl.BlockSpec((tm, cols), lambda i: (i, 0))
    row = pl.BlockSpec((1, cols), lambda i: (0, 0))
    return _chained(
        body, name=name, grid=(rows // tm,), in_specs=[blk, row, blk], out_specs=(blk, row),
        out_shape=(jax.ShapeDtypeStruct((rows, cols), F32), jax.ShapeDtypeStruct((1, cols), F32)),
        compiler_params=_rowwise_params("arbitrary"),
    )(pre, bias, dla)


def _loss(name, y, target, tm=256):
    rows, d = y.shape
    tm = _tile(rows, tm)

    def body(y_ref, t_ref, dy_ref, l_ref):
        err = y_ref[...] - t_ref[...]
        dy_ref[...] = err * (1.0 / d)

        @pl.when(pl.program_id(0) == 0)
        def _():
            l_ref[...] = jnp.zeros_like(l_ref)

        sq = (err * err).reshape(tm // 8, 8, d)
        l_ref[...] += jnp.sum(sq, axis=0) * (0.5 / d)

    blk = pl.BlockSpec((tm, d), lambda i: (i, 0))
    return _chained(
        body, name=name, grid=(rows // tm,), in_specs=[blk, blk],
        out_specs=(blk, pl.BlockSpec((8, d), lambda i: (0, 0))),
        out_shape=(jax.ShapeDtypeStruct((rows, d), F32), jax.ShapeDtypeStruct((8, d), F32)),
        compiler_params=_rowwise_params("arbitrary"),
    )(y, target)


def _scores(q, k, scale, causal, q0):
    s = lax.dot_general(q, k, NT, preferred_element_type=F32) * scale
    if causal:
        qc = (q0 + lax.broadcasted_iota(jnp.int32, s.shape, 0)) // CHUNK
        kc = lax.broadcasted_iota(jnp.int32, s.shape, 1) // CHUNK
        s = jnp.where(kc <= qc, s, -1e30)
    e = jnp.exp(s - jnp.max(s, axis=-1, keepdims=True))
    return e, jnp.sum(e, axis=-1, keepdims=True)


def _attn_fwd(name, q, k, v, scale, causal, tq=256):
    nh, t, dk = q.shape
    tk, dv = k.shape[1], v.shape[2]
    tq = _tile(t, tq)

    def body(q_ref, k_ref, v_ref, o_ref):
        e, l = _scores(q_ref[...], k_ref[...], scale, causal, pl.program_id(1) * tq)
        o = jnp.dot(e.astype(BF16), v_ref[...], preferred_element_type=F32)
        o_ref[...] = (o / l).astype(o_ref.dtype)

    return _chained(
        body, name=name, grid=(nh, t // tq),
        in_specs=[pl.BlockSpec((None, tq, dk), lambda h, i: (h, i, 0)), pl.BlockSpec((None, tk, dk), lambda h, i: (h, 0, 0)),
                  pl.BlockSpec((None, tk, dv), lambda h, i: (h, 0, 0))],
        out_specs=pl.BlockSpec((tq, dv), lambda h, i: (i, h)),
        out_shape=jax.ShapeDtypeStruct((t, nh * dv), BF16),
        compiler_params=pltpu.CompilerParams(dimension_semantics=("parallel", "parallel"),
                                             vmem_limit_bytes=_vmem_limit(0, 6 * tq * tk * 4)),
    )(q, k, v)


def _attn_bwd(name, q, k, v, do, scale, causal, tq=256):
    nh, t, dk = q.shape
    tk, dv = k.shape[1], v.shape[2]
    tq = _tile(t, tq)

    def body(q_ref, k_ref, v_ref, do_ref, dq_ref, dk_ref, dv_ref):
        qb, kb = q_ref[...], k_ref[...]
        e, l = _scores(qb, kb, scale, causal, pl.program_id(1) * tq)
        p = e / l
        dob = do_ref[...].astype(BF16)
        dp = lax.dot_general(dob, v_ref[...], NT, preferred_element_type=F32)
        ds = (p * (dp - jnp.sum(p * dp, axis=-1, keepdims=True)) * scale).astype(BF16)
        dq_ref[...] = jnp.dot(ds, kb, preferred_element_type=F32)

        @pl.when(pl.program_id(1) == 0)
        def _():
            dk_ref[...] = jnp.zeros_like(dk_ref)
            dv_ref[...] = jnp.zeros_like(dv_ref)

        dk_ref[...] += lax.dot_general(ds, qb, TN, preferred_element_type=F32)
        dv_ref[...] += lax.dot_general(p.astype(BF16), dob, TN, preferred_element_type=F32)

    return _chained(
        body, name=name, grid=(nh, t // tq),
        in_specs=[pl.BlockSpec((None, tq, dk), lambda h, i: (h, i, 0)), pl.BlockSpec((None, tk, dk), lambda h, i: (h, 0, 0)),
                  pl.BlockSpec((None, tk, dv), lambda h, i: (h, 0, 0)), pl.BlockSpec((tq, dv), lambda h, i: (i, h))],
        out_specs=(pl.BlockSpec((None, tq, dk), lambda h, i: (h, i, 0)), pl.BlockSpec((None, tk, dk), lambda h, i: (h, 0, 0)),
                   pl.BlockSpec((None, tk, dv), lambda h, i: (h, 0, 0))),
        out_shape=(jax.ShapeDtypeStruct((nh, t, dk), F32), jax.ShapeDtypeStruct((nh, tk, dk), F32),
                   jax.ShapeDtypeStruct((nh, tk, dv), F32)),
        compiler_params=pltpu.CompilerParams(dimension_semantics=("parallel", "arbitrary"),
                                             vmem_limit_bytes=_vmem_limit(0, 10 * tq * tk * 4)),
    )(q, k, v, do)


def _tri(lower):
    r = lax.broadcasted_iota(jnp.int32, (CHUNK, CHUNK), 0)
    c = lax.broadcasted_iota(jnp.int32, (CHUNK, CHUNK), 1)
    return jnp.where((c <= r) if lower else (c >= r), 1.0, 0.0).astype(BF16)


def _tri_dot(tri, x):
    return sum(jnp.dot(tri, part, preferred_element_type=F32) for part in _split3(x))


def _gla_fwd(name, q, k, v, la, nh):
    t = q.shape[0]
    dk, dv = q.shape[1] // nh, v.shape[1] // nh
    nc = t // CHUNK

    def body(q_ref, k_ref, v_ref, g_ref, o_ref, st_ref, state):
        @pl.when(pl.program_id(1) == 0)
        def _():
            state[...] = jnp.zeros_like(state)

        g = g_ref[...]
        b = _tri_dot(_tri(True), g)
        b_end = jnp.sum(g, axis=0, keepdims=True)
        k_dec = (k_ref[...] * jnp.exp(b_end - b)).astype(BF16)
        u_t = lax.dot_general(v_ref[...].astype(BF16), k_dec, TN, preferred_element_type=F32)
        new = state[...] * jnp.exp(b_end) + u_t
        state[...] = new
        st_ref[...] = new
        qc = (q_ref[...] * (dk ** -0.5)).astype(BF16)
        o_ref[...] = lax.dot_general(qc, new.astype(BF16), NT, preferred_element_type=F32)

    kblk = pl.BlockSpec((CHUNK, dk), lambda h, n: (n, h))
    vblk = pl.BlockSpec((CHUNK, dv), lambda h, n: (n, h))
    return _chained(
        body, name=name, grid=(nh, nc), in_specs=[kblk, kblk, vblk, kblk],
        out_specs=(vblk, pl.BlockSpec((None, None, dv, dk), lambda h, n: (h, n, 0, 0))),
        out_shape=(jax.ShapeDtypeStruct((t, nh * dv), F32), jax.ShapeDtypeStruct((nh, nc, dv, dk), F32)),
        scratch_shapes=[pltpu.VMEM((dv, dk), F32)],
        compiler_params=_rowwise_params("parallel", "arbitrary"),
    )(q, k, v, la)


def _gla_bwd(name, q, k, v, la, states, do, nh):
    t = q.shape[0]
    dk, dv = q.shape[1] // nh, v.shape[1] // nh
    nc = t // CHUNK
    scale = dk ** -0.5

    def body(q_ref, k_ref, v_ref, g_ref, do_ref, st_ref, sp_ref, dq_ref, dk_ref, dv_ref, dg_ref, carry):
        i = pl.program_id(1)

        @pl.when(i == 0)
        def _():
            carry[...] = jnp.zeros_like(carry)

        g = g_ref[...]
        b = _tri_dot(_tri(True), g)
        b_end = jnp.sum(g, axis=0, keepdims=True)
        w = jnp.exp(b_end - b)
        decay = jnp.exp(b_end)
        k_dec = k_ref[...] * w
        qc = (q_ref[...] * scale).astype(BF16)
        dob = do_ref[...].astype(BF16)
        dq_ref[...] = jnp.dot(dob, st_ref[...].astype(BF16), preferred_element_type=F32) * scale
        g_t = carry[...] + lax.dot_general(dob, qc, TN, preferred_element_type=F32)
        g_tb = g_t.astype(BF16)
        dk_dec = jnp.dot(v_ref[...].astype(BF16), g_tb, preferred_element_type=F32)
        dv_ref[...] = lax.dot_general(k_dec.astype(BF16), g_tb, NT, preferred_element_type=F32)
        prev = jnp.where(i < nc - 1, sp_ref[...], 0.0)
        ddecay = jnp.sum(g_t * prev, axis=0, keepdims=True)
        dk_ref[...] = dk_dec * w
        e = dk_dec * k_dec
        db_end = jnp.sum(e, axis=0, keepdims=True) + ddecay * decay
        dg_ref[...] = _tri_dot(_tri(False), -e) + db_end
        carry[...] = g_t * decay

    kblk = pl.BlockSpec((CHUNK, dk), lambda h, i: (nc - 1 - i, h))
    vblk = pl.BlockSpec((CHUNK, dv), lambda h, i: (nc - 1 - i, h))
    ksds = jax.ShapeDtypeStruct((t, nh * dk), F32)
    return _chained(
        body, name=name, grid=(nh, nc),
        in_specs=[kblk, kblk, vblk, kblk, vblk,
                  pl.BlockSpec((None, None, dv, dk), lambda h, i: (h, nc - 1 - i, 0, 0)),
                  pl.BlockSpec((None, None, dv, dk), lambda h, i: (h, jnp.maximum(nc - 2 - i, 0), 0, 0))],
        out_specs=(kblk, kblk, vblk, kblk),
        out_shape=(ksds, ksds, jax.ShapeDtypeStruct((t, nh * dv), F32), ksds),
        scratch_shapes=[pltpu.VMEM((dv, dk), F32)],
        compiler_params=_rowwise_params("parallel", "arbitrary"),
    )(q, k, v, la, do, states, states)


def _adamw(name, w, m, v, parts):
    rows, cols = w.shape
    tm = _tile(rows, 1 << int(math.log2(max(8, (1 << 17) // (-(-cols // LANES) * LANES)))))

    def body(w_ref, m_ref, v_ref, p_ref, g_ref, d_ref, nm_ref, nv_ref):
        g = p_ref[0].astype(F32)
        for s in range(1, N_DEV):
            g = g + p_ref[s].astype(F32)
        m_new = ADAM_B1 * m_ref[...] + (1.0 - ADAM_B1) * g
        v_new = ADAM_B2 * v_ref[...] + (1.0 - ADAM_B2) * jnp.square(g)
        m_hat = m_new / (1.0 - ADAM_B1 ** ADAM_STEP)
        v_hat = v_new / (1.0 - ADAM_B2 ** ADAM_STEP)
        g_ref[...] = g
        d_ref[...] = -ADAM_LR * (m_hat / (jnp.sqrt(v_hat) + ADAM_EPS) + ADAM_WD * w_ref[...])
        nm_ref[...] = m_new
        nv_ref[...] = v_new

    blk = pl.BlockSpec((tm, cols), lambda i: (i, 0))
    sds = jax.ShapeDtypeStruct((rows, cols), F32)
    return _chained(
        body, name=name, grid=(rows // tm,),
        in_specs=[blk, blk, blk, pl.BlockSpec((N_DEV, tm, cols), lambda i: (0, i, 0))],
        out_specs=(blk, blk, blk, blk), out_shape=(sds, sds, sds, sds),
        compiler_params=_rowwise_params("parallel"),
    )(w, m, v, parts)


HBM = pl.BlockSpec(memory_space=pl.ANY)
MESH = pl.DeviceIdType.MESH


def _all_gather(name, shards):
    n = len(shards)

    def body(*refs):
        ins, outs = refs[:n], refs[n:2 * n]
        send_sems, recv_sems, local_sems = refs[2 * n:]
        x, y, c = lax.axis_index("x"), lax.axis_index("y"), lax.axis_index("c")
        me, sibling = (x, y, c), (x, y, 1 - c)
        chips = [(1 - x, y), (x, 1 - y), (1 - x, 1 - y)]

        def copy(w, k, block, to, src=None):
            dst = outs[w].at[4 * block[0] + 2 * block[1] + block[2]]
            return pltpu.make_async_remote_copy(
                src_ref=dst if src is None else src, dst_ref=dst, send_sem=send_sems.at[7 * w + k],
                recv_sem=recv_sems.at[7 * w + k], device_id=to, device_id_type=MESH)

        mine, first, passed = [], [], []
        for w in range(n):
            cp = pltpu.make_async_copy(ins[w], outs[w].at[4 * x + 2 * y + c], local_sems.at[w])
            cp.start()
            mine.append(cp)
            first.append(copy(w, 0, me, sibling, src=ins[w]))
            first += [copy(w, 1 + j, me, (*chip, c), src=ins[w]) for j, chip in enumerate(chips)]
        for cp in first:
            cp.start()
        for w in range(n):
            for j, chip in enumerate(chips):
                copy(w, 1 + j, (*chip, c), me).wait_recv()
                cp = copy(w, 4 + j, (*chip, c), sibling)
                cp.start()
                passed.append(cp)
        for w in range(n):
            copy(w, 0, sibling, me).wait_recv()
            for j, chip in enumerate(chips):
                copy(w, 4 + j, (*chip, 1 - c), me).wait_recv()
        for cp in first + passed:
            cp.wait_send()
        for cp in mine:
            cp.wait()

    return _chained(
        body, name=name, in_specs=[HBM] * n, out_specs=[HBM] * n,
        out_shape=[jax.ShapeDtypeStruct((N_DEV,) + s.shape, s.dtype) for s in shards],
        scratch_shapes=[pltpu.SemaphoreType.DMA((7 * n,)), pltpu.SemaphoreType.DMA((7 * n,)),
                        pltpu.SemaphoreType.DMA((n,))],
    )(*shards)


HBM_SPEC = pl.BlockSpec(memory_space=pltpu.HBM)
SEM_SPEC = pl.BlockSpec(memory_space=pltpu.SEMAPHORE)
EFFECT = pltpu.SideEffectType.DATAFLOW_SIDE_EFFECTING
TOKEN = jax.ShapeDtypeStruct((8, LANES), F32)


def _in_hbm(a):
    return pltpu.with_memory_space_constraint(a, pltpu.HBM)


def _place():
    x, y, c = lax.axis_index("x"), lax.axis_index("y"), lax.axis_index("c")
    chips = [(1 - x, y), (x, 1 - y), (1 - x, 1 - y)]
    return x, y, c, chips


def _block_of(px, py, pc):
    return 4 * px + 2 * py + pc


def _gather_copies(ins, outs, sems, w):
    send_sems, recv_sems, local_sems = sems
    x, y, c, chips = _place()
    peers = [(x, y, 1 - c)] + [(*chip, c) for chip in chips]

    def copy(k, block, to, src):
        dst = outs[w].at[_block_of(*block)]
        return pltpu.make_async_remote_copy(src_ref=dst if src is None else src, dst_ref=dst, send_sem=send_sems.at[4 * w + k],
                                            recv_sem=recv_sems.at[4 * w + k], device_id=to, device_id_type=MESH)

    local = pltpu.make_async_copy(ins[w], outs[w].at[_block_of(x, y, c)], local_sems.at[w])
    sends = [copy(k, (x, y, c), peer, ins[w]) for k, peer in enumerate(peers)]
    recvs = [copy(k, peer, (x, y, c), None) for k, peer in enumerate(peers)]
    return local, sends, recvs


def _forward_copies(outs, sems, w):
    send_sems, recv_sems = sems
    x, y, c, chips = _place()

    def copy(j, block, to):
        dst = outs[w].at[_block_of(*block)]
        return pltpu.make_async_remote_copy(src_ref=dst, dst_ref=dst, send_sem=send_sems.at[3 * w + j],
                                            recv_sem=recv_sems.at[3 * w + j], device_id=to, device_id_type=MESH)

    sends = [copy(j, (*chip, c), (x, y, 1 - c)) for j, chip in enumerate(chips)]
    recvs = [copy(j, (*chip, 1 - c), (x, y, c)) for j, chip in enumerate(chips)]
    return sends, recvs


def _gather_start(name, shards):
    n = len(shards)

    def body(*refs):
        ins, outs, sems = refs[:n], refs[n:2 * n], refs[2 * n:2 * n + 3]
        for w in range(n):
            local, sends, _ = _gather_copies(ins, outs, sems, w)
            local.start()
            for cp in sends:
                cp.start()
        refs[-1][...] = jnp.zeros_like(refs[-1])

    lands = [lax.empty((N_DEV,) + s.shape, s.dtype) for s in shards]
    res = _chained(
        body, name=name, link=-1, in_specs=[HBM_SPEC] * (2 * n),
        out_specs=[SEM_SPEC] * 3 + [HBM_SPEC] * (2 * n) + [pl.BlockSpec(memory_space=pltpu.VMEM)],
        out_shape=[pltpu.SemaphoreType.DMA((4 * n,)), pltpu.SemaphoreType.DMA((4 * n,)), pltpu.SemaphoreType.DMA((n,))]
        + [pltpu.HBM(s.shape, s.dtype) for s in shards] + [pltpu.HBM(l.shape, l.dtype) for l in lands] + [TOKEN],
        input_output_aliases={i: 3 + i for i in range(2 * n)},
        compiler_params=pltpu.CompilerParams(has_side_effects=EFFECT),
    )(*[_in_hbm(s) for s in shards], *[_in_hbm(l) for l in lands])
    return n, res[:3], res[3:3 + n], res[3 + n:3 + 2 * n]


def _gather_mid(name, state):
    n, sems, shards, lands = state

    def body(*refs):
        ins, outs, sems_in = refs[:n], refs[n:2 * n], refs[2 * n:2 * n + 3]
        sems_out = refs[2 * n + 3:2 * n + 5]
        for w in range(n):
            local, sends, recvs = _gather_copies(ins, outs, sems_in, w)
            local.wait()
            for cp in sends:
                cp.wait_send()
            for cp in recvs:
                cp.wait_recv()
            for cp in _forward_copies(outs, sems_out, w)[0]:
                cp.start()
        refs[-1][...] = jnp.zeros_like(refs[-1])

    res = _chained(
        body, name=name, link=-1, in_specs=[HBM_SPEC] * (2 * n) + [SEM_SPEC] * 3,
        out_specs=[SEM_SPEC] * 2 + [HBM_SPEC] * n + [pl.BlockSpec(memory_space=pltpu.VMEM)],
        out_shape=[pltpu.SemaphoreType.DMA((3 * n,)), pltpu.SemaphoreType.DMA((3 * n,))]
        + [pltpu.HBM(l.shape, l.dtype) for l in lands] + [TOKEN],
        input_output_aliases={n + i: 2 + i for i in range(n)},
        compiler_params=pltpu.CompilerParams(has_side_effects=EFFECT),
    )(*shards, *lands, *sems)
    return n, res[:2], res[2:2 + n]


def _gather_end(name, state):
    n, sems, lands = state

    def body(*refs):
        outs, sems_in = refs[:n], refs[n:n + 2]
        for w in range(n):
            sends, recvs = _forward_copies(outs, sems_in, w)
            for cp in sends:
                cp.wait_send()
            for cp in recvs:
                cp.wait_recv()

    return _chained(
        body, name=name, in_specs=[HBM_SPEC] * n + [SEM_SPEC] * 2, out_specs=[HBM_SPEC] * n,
        out_shape=[pltpu.HBM(l.shape, l.dtype) for l in lands], input_output_aliases={i: i for i in range(n)},
        compiler_params=pltpu.CompilerParams(has_side_effects=EFFECT),
    )(*lands, *sems)


def _exchange_copies(ins, outs, sems, w):
    send_sems, recv_sems, local_sems = sems
    x, y, c, _ = _place()
    mine = _block_of(x, y, c)
    local = pltpu.make_async_copy(ins[w].at[mine], outs[w].at[mine], local_sems.at[w])
    remote = []
    for k in range(1, N_DEV):
        px, py, pc = x ^ (k >> 2), y ^ ((k >> 1) & 1), c ^ (k & 1)
        remote.append(pltpu.make_async_remote_copy(
            src_ref=ins[w].at[_block_of(px, py, pc)], dst_ref=outs[w].at[mine], send_sem=send_sems.at[7 * w + k - 1],
            recv_sem=recv_sems.at[7 * w + k - 1], device_id=(px, py, pc), device_id_type=MESH))
    return local, remote


def _exchange_start(name, stacks):
    n = len(stacks)

    def body(*refs):
        ins, outs, sems = refs[:n], refs[n:2 * n], refs[2 * n:2 * n + 3]
        for w in range(n):
            local, remote = _exchange_copies(ins, outs, sems, w)
            local.start()
            for cp in remote:
                cp.start()
        refs[-1][...] = jnp.zeros_like(refs[-1])

    lands = [lax.empty(s.shape, s.dtype) for s in stacks]
    res = _chained(
        body, name=name, link=-1, in_specs=[HBM_SPEC] * (2 * n),
        out_specs=[SEM_SPEC] * 3 + [HBM_SPEC] * (2 * n) + [pl.BlockSpec(memory_space=pltpu.VMEM)],
        out_shape=[pltpu.SemaphoreType.DMA((7 * n,)), pltpu.SemaphoreType.DMA((7 * n,)), pltpu.SemaphoreType.DMA((n,))]
        + [pltpu.HBM(s.shape, s.dtype) for s in stacks] * 2 + [TOKEN],
        input_output_aliases={i: 3 + i for i in range(2 * n)},
        compiler_params=pltpu.CompilerParams(has_side_effects=EFFECT),
    )(*[_in_hbm(s) for s in stacks], *[_in_hbm(l) for l in lands])
    return n, res[:3], res[3:3 + n], res[3 + n:3 + 2 * n]


def _exchange_wait(name, state):
    n, sems, stacks, lands = state

    def body(*refs):
        ins, outs, sems_in = refs[:n], refs[n:2 * n], refs[2 * n:2 * n + 3]
        for w in range(n):
            local, remote = _exchange_copies(ins, outs, sems_in, w)
            local.wait()
            for cp in remote:
                cp.wait_send()
                cp.wait_recv()

    return _chained(
        body, name=name, in_specs=[HBM_SPEC] * (2 * n) + [SEM_SPEC] * 3, out_specs=[HBM_SPEC] * n,
        out_shape=[pltpu.HBM(l.shape, l.dtype) for l in lands], input_output_aliases={n + i: i for i in range(n)},
        compiler_params=pltpu.CompilerParams(has_side_effects=EFFECT),
    )(*stacks, *lands, *sems)


def _unstack_cols(w):
    return w.transpose(1, 0, 2).reshape(w.shape[1], N_DEV * w.shape[2])


def _stack_cols(w):
    return w.reshape(w.shape[0], N_DEV, w.shape[1] // N_DEV).transpose(1, 0, 2)


def _rope_tables(positions):
    half = MLA_ROPE // 2
    inv_freq = ROPE_THETA ** (-jnp.arange(half, dtype=F32) / half)
    ang = positions.astype(F32)[:, None] * inv_freq
    cos, sin = jnp.cos(ang), jnp.sin(ang)
    t = positions.shape[0]
    cos_t = jnp.concatenate([jnp.ones((t, MLA_NOPE), F32), cos, cos], axis=1)
    sin_t = jnp.concatenate([jnp.zeros((t, MLA_NOPE), F32), -sin, sin], axis=1)
    idx = jnp.arange(MLA_QK)
    partner = jnp.where(idx < MLA_NOPE, -1, jnp.where(idx < MLA_NOPE + half, idx + half, idx - half))
    swap = (idx[:, None] == partner[None, :]).astype(BF16)
    return cos_t, sin_t, swap


def _ffn_fwd(tag, x, gain, wt):
    t = x.shape[0]
    h = _rms_fwd(tag + "_rms", x, gain, BF16)
    g = _mm_stack_nt_out(tag + "_gate", h, wt[tag + '_w_gate'], BF16)
    u = _mm_stack_nt_out(tag + "_up", h, wt[tag + '_w_up'], BF16)
    f = g.shape[2]
    a = _swiglu_fwd(tag + "_act", g.reshape(N_DEV * t, f), u.reshape(N_DEV * t, f), BF16).reshape(N_DEV, t, f)
    y = _mm_stack_sum(tag + "_down", a, wt[tag + '_w_down'], F32, scale=0.5, res=x)
    return y, (x, h, g, u, a)


def _ffn_bwd_weights(tag, dy, saved, wd, comm):
    x, h, g, u, a = saved
    t = x.shape[0]
    f = wd.shape[1]
    da = _mm_stack_nt_out(tag + "_da", dy, wd, BF16, scale=0.5)
    comm.grads({tag + '_w_down': _mm_stack_tn_right(tag + "_dwd", a, dy, BF16, scale=0.5)})
    dg, du = _swiglu_bwd(tag + "_dact", da.reshape(N_DEV * t, f), g.reshape(N_DEV * t, f), u.reshape(N_DEV * t, f), BF16)
    dg, du = dg.reshape(N_DEV, t, f), du.reshape(N_DEV, t, f)
    comm.grads({tag + '_w_gate': _mm_stack_tn_right(tag + "_dwg", dg, h, BF16)})
    comm.grads({tag + '_w_up': _mm_stack_tn_right(tag + "_dwu", du, h, BF16)})
    return dg, du


def _ffn_bwd_input(tag, dy, saved, dgu, gain, wg, wu):
    dg, du = dgu
    dh = _mm_stack_sum(tag + "_dh_g", dg, wg, F32)
    dh = _mm_stack_sum(tag + "_dh_u", du, wu, F32, res=dh)
    return _rms_bwd(tag + "_drms", saved[0], gain, dh, res=dy)


def _local_step(x, mem, positions, sm, comm, target):
    t, d = x.shape
    nm = mem.shape[0]
    gs, gw = {}, {}

    class Weights(dict):
        def __missing__(self, name):
            self.update(comm.weights(next(k for k, names in enumerate(GATHERS) if name in names)))
            return self[name]

    wt = Weights()

    x1, ffn1_saved = _ffn_fwd("ffn1", x, sm['ffn1_norm'], wt)

    h2 = _rms_fwd("mix_rms", x1, sm['mix_norm'], BF16)
    w_in_ref = _unstack_cols(wt['w_in'])
    pieces = []
    for name, n in IN_PAD:
        piece = w_in_ref[:, REF_OFF[name]:REF_OFF[name] + REF_SIZE[name]]
        if n != REF_SIZE[name]:
            piece = jnp.pad(piece, ((0, 0), (0, n - REF_SIZE[name])))
        pieces.append(piece)
    w_in = jnp.concatenate(pieces, axis=1)
    z = _mm2("mix_in", h2, w_in, NN, F32)
    zs = {name: z[:, PAD_OFF[name]:PAD_OFF[name] + n] for name, n in IN_PAD}

    cq = _rms_fwd("mla_q_a_rms", zs['zq'], sm['q_a_norm'], BF16)
    q_raw = _mm_stack_nt_out("mla_q_up", cq, wt['w_q_up'], F32)
    ckv = _rms_fwd("mla_kv_a_rms", zs['zkv'], sm['kv_a_norm'], BF16)
    kv = _mm_stack_out("mla_kv_up", ckv, wt['w_kv_up'], F32)
    zkr = zs['zkr'][:, :MLA_ROPE]
    k_raw = jnp.concatenate([kv[:, :, :MLA_NOPE], jnp.broadcast_to(zkr[None], (MLA_HEADS, t, MLA_ROPE))], axis=2)
    v_mla = kv[:, :, MLA_NOPE:].astype(BF16)
    cos_t, sin_t, swap = _rope_tables(positions)
    q_raw2, k_raw2 = q_raw.reshape(MLA_HEADS * t, MLA_QK), k_raw.reshape(MLA_HEADS * t, MLA_QK)
    qn = _rms_fwd("mla_q_rms", q_raw2, sm['mla_q_norm'], F32)
    kn = _rms_fwd("mla_k_rms", k_raw2, sm['mla_k_norm'], F32)
    qf = _rope("mla_q_rope", qn, cos_t, sin_t, swap, BF16, False).reshape(MLA_HEADS, t, MLA_QK)
    kf = _rope("mla_k_rope", kn, cos_t, sin_t, swap, BF16, False).reshape(MLA_HEADS, t, MLA_QK)
    o_mla = _attn_fwd("mla_attn", qf, kf, v_mla, MLA_QK ** -0.5, True)

    w_g2 = jnp.pad(_unstack_cols(wt['gla_w_gate2']), ((0, LANES - GLA_GATE_RANK), (0, 0)))
    pre = _mm2("gla_gate_pre", zs['zg'], w_g2, NN, F32)
    log_a = _gate_fwd("gla_gate", pre, sm['gla_b_gate'])
    o_gla_raw, states = _gla_fwd("gla_scan", zs['gq'], zs['gk'], zs['gv'], log_a, GLA_HEADS)
    o_gla_n = _rms_fwd("gla_out_rms", o_gla_raw, sm['gla_out_norm'], F32)
    o_gla = _swiglu_fwd("gla_out_gate", zs['zr'], o_gla_n, BF16)

    cat = jnp.concatenate([o_mla, o_gla], axis=1)
    w_out = wt['w_out'].reshape(d, d)
    x2 = _mm2("mix_out", cat, w_out, NN, F32, res=x1)

    w_mq, w_mk, w_mv = (wt[n].reshape(d, MEM_HEADS * MEM_HEAD_DIM) for n in ('mem_w_q', 'mem_w_k', 'mem_w_v'))
    hq = _rms_fwd("mem_attn_rms", x2, sm['mem_attn_norm'], BF16)
    hm = _rms_fwd("mem_rms", mem, sm['mem_norm'], BF16)

    def heads_out(name, a, b, out_dtype):
        m, kk = a.shape
        tm = _tile(m, 512)
        return _mm(name, a, b, (m // tm, MEM_HEADS, 1), ((tm, kk), lambda i, h, k: (i, 0)),
                   ((kk, MEM_HEAD_DIM), lambda i, h, k: (0, h)), ((None, tm, MEM_HEAD_DIM), lambda i, h, k: (h, i, 0)),
                   (MEM_HEADS, m, MEM_HEAD_DIM), out_dtype, NN)

    mq_raw = heads_out("mem_q", hq, w_mq, F32)
    mk_raw = heads_out("mem_k", hm, w_mk, F32)
    mv = heads_out("mem_v", hm, w_mv, BF16)
    mq = _rms_fwd("mem_q_rms", mq_raw.reshape(MEM_HEADS * t, MEM_HEAD_DIM), sm['mem_q_norm'], BF16)
    mk = _rms_fwd("mem_k_rms", mk_raw.reshape(MEM_HEADS * nm, MEM_HEAD_DIM), sm['mem_k_norm'], BF16)
    mq, mk = mq.reshape(MEM_HEADS, t, MEM_HEAD_DIM), mk.reshape(MEM_HEADS, nm, MEM_HEAD_DIM)
    o_mem = _attn_fwd("mem_attn", mq, mk, mv, MEM_HEAD_DIM ** -0.5, False)
    w_mo = wt['mem_w_o']
    mo_cols = w_mo.shape[2]
    tm = _tile(t, 512)
    x3 = _mm("mem_out", o_mem, w_mo, (t // tm, N_DEV, 1), ((tm, o_mem.shape[1]), lambda i, j, k: (i, 0)),
             ((None, o_mem.shape[1], mo_cols), lambda i, j, k: (j, 0, 0)), ((tm, mo_cols), lambda i, j, k: (i, j)),
             (t, d), F32, NN, res=x2)

    y, ffn2_saved = _ffn_fwd("ffn2", x3, sm['ffn2_norm'], wt)
    dy, loss_lanes = _loss("loss", y, target)

    dgu = _ffn_bwd_weights("ffn2", dy, ffn2_saved, wt['ffn2_w_down'], comm)
    dx3, gs['ffn2_norm'] = _ffn_bwd_input("ffn2", dy, ffn2_saved, dgu, sm['ffn2_norm'],
                                          wt['ffn2_w_gate'], wt['ffn2_w_up'])

    do_mem = _mm("mem_do", dx3, w_mo, (t // tm, MEM_HEADS, N_DEV), ((tm, mo_cols), lambda i, h, k: (i, k)),
                 ((None, MEM_HEAD_DIM, mo_cols), lambda i, h, k: (k, h, 0)), ((tm, MEM_HEAD_DIM), lambda i, h, k: (i, h)),
                 (t, MEM_HEADS * MEM_HEAD_DIM), BF16, NT)
    tk = _tile(t, 512)
    gw['mem_w_o'] = _mm("mem_dwo", o_mem, dx3, (N_DEV, 1, t // tk), ((tk, o_mem.shape[1]), lambda j, i, k: (k, 0)),
                        ((tk, mo_cols), lambda j, i, k: (k, j)), ((None, o_mem.shape[1], mo_cols), lambda j, i, k: (j, 0, 0)),
                        w_mo.shape, BF16, TN)
    dmq, dmk, dmv = _attn_bwd("mem_dattn", mq, mk, mv, do_mem, MEM_HEAD_DIM ** -0.5, False)
    dmq_raw, gs['mem_q_norm'] = _rms_bwd("mem_q_drms", mq_raw.reshape(MEM_HEADS * t, MEM_HEAD_DIM), sm['mem_q_norm'],
                                         dmq.reshape(MEM_HEADS * t, MEM_HEAD_DIM))
    dmk_raw, gs['mem_k_norm'] = _rms_bwd("mem_k_drms", mk_raw.reshape(MEM_HEADS * nm, MEM_HEAD_DIM), sm['mem_k_norm'],
                                         dmk.reshape(MEM_HEADS * nm, MEM_HEAD_DIM))
    dmq_raw = dmq_raw.reshape(MEM_HEADS, t, MEM_HEAD_DIM)
    dmk_raw = dmk_raw.reshape(MEM_HEADS, nm, MEM_HEAD_DIM)

    def heads_in_nt(name, a, b, res=None):
        m, n = a.shape[1], b.shape[0]
        tm_, tn_ = _tile(m, 512), _tile(n, 1024)
        return _mm(name, a, b, (m // tm_, n // tn_, MEM_HEADS), ((None, tm_, MEM_HEAD_DIM), lambda i, j, k: (k, i, 0)),
                   ((tn_, MEM_HEAD_DIM), lambda i, j, k: (j, k)), ((tm_, tn_), lambda i, j, k: (i, j)), (m, n), F32, NT,
                   None, res)

    def heads_tn(name, a, b):
        m, kp = a.shape
        tm_, tk_ = _tile(kp, 1024), _tile(m, 512)
        return _mm(name, a, b, (kp // tm_, MEM_HEADS, m // tk_), ((tk_, tm_), lambda i, h, k: (k, i)),
                   ((None, tk_, MEM_HEAD_DIM), lambda i, h, k: (h, k, 0)), ((tm_, MEM_HEAD_DIM), lambda i, h, k: (i, h)),
                   (kp, MEM_HEADS * MEM_HEAD_DIM), BF16, TN)

    dhq = heads_in_nt("mem_dhq", dmq_raw, w_mq)
    gw['mem_w_q'] = heads_tn("mem_dwq", hq, dmq_raw).reshape(wt['mem_w_q'].shape)
    dhm = heads_in_nt("mem_dhm_k", dmk_raw, w_mk)
    dhm = heads_in_nt("mem_dhm_v", dmv, w_mv, res=dhm)
    gw['mem_w_k'] = heads_tn("mem_dwk", hm, dmk_raw).reshape(wt['mem_w_k'].shape)
    gw['mem_w_v'] = heads_tn("mem_dwv", hm, dmv).reshape(wt['mem_w_v'].shape)
    _, gs['mem_norm'] = _rms_bwd("mem_drms", mem, sm['mem_norm'], dhm)
    comm.grads({n: gw[n] for n in MEMORY})
    dx2, gs['mem_attn_norm'] = _rms_bwd("mem_attn_drms", x2, sm['mem_attn_norm'], dhq, res=dx3)

    dcat = _mm2("mix_dcat", dx2, w_out, NT, F32)
    gw['w_out'] = _mm2("mix_dwout", cat, dx2, TN, BF16, tm=1024, tn=2048, tk=512).reshape(wt['w_out'].shape)
    do_mla, do_gla = dcat[:, :MLA_HEADS * MLA_V], dcat[:, MLA_HEADS * MLA_V:]

    dzr, dgn = _swiglu_bwd("gla_out_dgate", do_gla, zs['zr'], o_gla_n, F32)
    do_gla_raw, gs['gla_out_norm'] = _rms_bwd("gla_out_drms", o_gla_raw, sm['gla_out_norm'], dgn)
    dgq, dgk, dgv, dlog_a = _gla_bwd("gla_dscan", zs['gq'], zs['gk'], zs['gv'], log_a, states, do_gla_raw, GLA_HEADS)
    dpre, gs['gla_b_gate'] = _gate_bwd("gla_dgate", pre, sm['gla_b_gate'], dlog_a)
    dw_g2 = _mm2("gla_dwgate", zs['zg'], dpre, TN, BF16, tk=512)
    gw['gla_w_gate2'] = _stack_cols(dw_g2[:GLA_GATE_RANK])
    dzg = _mm2("gla_dzg", dpre, w_g2, NT, F32)

    dqf, dkf, dv_mla = _attn_bwd("mla_dattn", qf, kf, v_mla, do_mla, MLA_QK ** -0.5, True)
    dqn = _rope("mla_q_drope", dqf.reshape(MLA_HEADS * t, MLA_QK), cos_t, sin_t, swap, F32, True)
    dkn = _rope("mla_k_drope", dkf.reshape(MLA_HEADS * t, MLA_QK), cos_t, sin_t, swap, F32, True)
    dq_raw, gs['mla_q_norm'] = _rms_bwd("mla_q_drms", q_raw2, sm['mla_q_norm'], dqn)
    dk_raw, gs['mla_k_norm'] = _rms_bwd("mla_k_drms", k_raw2, sm['mla_k_norm'], dkn)
    dq_raw = dq_raw.reshape(MLA_HEADS, t, MLA_QK)
    dk_raw = dk_raw.reshape(MLA_HEADS, t, MLA_QK)
    dkv = jnp.concatenate([dk_raw[:, :, :MLA_NOPE], dv_mla], axis=2)
    dzkr = jnp.sum(dk_raw[:, :, MLA_NOPE:], axis=0)
    gw['w_q_up'] = _mm_stack_tn_right("mla_dwq", dq_raw, cq, BF16)
    gw['w_kv_up'] = _mm_stack_tn_left("mla_dwkv", ckv, dkv, BF16)
    dcq = _mm_stack_sum("mla_dcq", dq_raw, wt['w_q_up'], F32)
    dckv = _mm_stack_nt_sum("mla_dckv", dkv, wt['w_kv_up'], F32)
    dzq, gs['q_a_norm'] = _rms_bwd("mla_q_a_drms", zs['zq'], sm['q_a_norm'], dcq)
    dzkv, gs['kv_a_norm'] = _rms_bwd("mla_kv_a_drms", zs['zkv'], sm['kv_a_norm'], dckv)

    dzs = {'zq': dzq, 'zkv': dzkv, 'gq': dgq, 'gk': dgk, 'gv': dgv, 'zr': dzr,
           'zkr': jnp.pad(dzkr, ((0, 0), (0, LANES - MLA_ROPE))), 'zg': dzg}
    dz = jnp.concatenate([dzs[name].astype(BF16) for name, _ in IN_PAD], axis=1)
    dw_in = _mm2("mix_dwin", h2, dz, TN, BF16, tm=1024, tn=2048, tk=512)
    dw_in_ref = jnp.concatenate([dw_in[:, PAD_OFF[name]:PAD_OFF[name] + n] for name, n in IN_REF], axis=1)
    gw['w_in'] = _stack_cols(dw_in_ref)
    comm.grads({n: gw[n] for n in MIXER})
    dh2 = _mm2("mix_dh", dz, w_in, NT, F32)
    dx1, gs['mix_norm'] = _rms_bwd("mix_drms", x1, sm['mix_norm'], dh2, res=dx2)

    dgu = _ffn_bwd_weights("ffn1", dx1, ffn1_saved, wt['ffn1_w_down'], comm)
    grad_x, gs['ffn1_norm'] = _ffn_bwd_input("ffn1", dx1, ffn1_saved, dgu, sm['ffn1_norm'],
                                             wt['ffn1_w_gate'], wt['ffn1_w_up'])
    return loss_lanes, grad_x, gs


def _pad_lanes(v):
    n = v.shape[1]
    return jnp.pad(v, ((0, 0), (0, -n % LANES)))


def _pack_small(vals):
    return jnp.concatenate([_pad_lanes(vals[n]) for n in SMALL], axis=1)


def _unpack_small(packed, like):
    out, off = {}, 0
    for n in SMALL:
        size = like[n].shape[1]
        out[n] = packed[:, off:off + size]
        off += size + (-size % LANES)
    return out


def kernel(x, mem, positions, ffn1_norm, ffn1_w_gate, ffn1_w_up, ffn1_w_down, mix_norm, w_in, q_a_norm, w_q_up, kv_a_norm, w_kv_up, mla_q_norm, mla_k_norm, gla_w_gate2, gla_b_gate, gla_out_norm, w_out, mem_attn_norm, mem_norm, mem_w_q, mem_w_k, mem_w_v, mem_w_o, mem_q_norm, mem_k_norm, ffn2_norm, ffn2_w_gate, ffn2_w_up, ffn2_w_down, loss_target, m_ffn1_norm, m_ffn1_w_gate, m_ffn1_w_up, m_ffn1_w_down, m_mix_norm, m_w_in, m_q_a_norm, m_w_q_up, m_kv_a_norm, m_w_kv_up, m_mla_q_norm, m_mla_k_norm, m_gla_w_gate2, m_gla_b_gate, m_gla_out_norm, m_w_out, m_mem_attn_norm, m_mem_norm, m_mem_w_q, m_mem_w_k, m_mem_w_v, m_mem_w_o, m_mem_q_norm, m_mem_k_norm, m_ffn2_norm, m_ffn2_w_gate, m_ffn2_w_up, m_ffn2_w_down, v_ffn1_norm, v_ffn1_w_gate, v_ffn1_w_up, v_ffn1_w_down, v_mix_norm, v_w_in, v_q_a_norm, v_w_q_up, v_kv_a_norm, v_w_kv_up, v_mla_q_norm, v_mla_k_norm, v_gla_w_gate2, v_gla_b_gate, v_gla_out_norm, v_w_out, v_mem_attn_norm, v_mem_norm, v_mem_w_q, v_mem_w_k, v_mem_w_v, v_mem_w_o, v_mem_q_norm, v_mem_k_norm, v_ffn2_norm, v_ffn2_w_gate, v_ffn2_w_up, v_ffn2_w_down):
    inp = dict(locals())
    x, mem, positions, target = inp['x'][0], inp['mem'][0], inp['positions'][0], inp['loss_target'][0]
    sm = {n: inp[n] for n in SMALL}
    out = {}

    def stored(key):
        name = key[2:] if key[:2] in ('m_', 'v_') else key
        return inp[key][0].T if name in TRANSPOSED else inp[key][0]

    def as_given(name, r):
        return r.T[None] if name in TRANSPOSED else r[None]

    class Comm:
        def __init__(self):
            self.gathers = {0: self.start(0)}
            self.forwards, self.exchanges = {}, []

        def start(self, k):
            return _gather_start(f"gather_start_{k}", [stored(n).astype(BF16) for n in GATHERS[k]])

        def forward(self, k):
            if k not in self.forwards:
                self.forwards[k] = _gather_mid(f"gather_mid_{k}", self.gathers[k])
                self.gathers.update({nxt: self.start(nxt) for nxt in NEXT_GATHERS.get(k, [])})

        def weights(self, k):
            self.forward(k)
            if k in EARLY_FORWARD:
                self.forward(EARLY_FORWARD[k])
            return dict(zip(GATHERS[k], _gather_end(f"gather_end_{k}", self.forwards[k])))

        def grads(self, stacks):
            names = list(stacks)
            self.exchanges.append((names, _exchange_start("exchange_start_" + names[0], [stacks[n] for n in names])))

        def update(self, count):
            todo, self.exchanges = self.exchanges[:count], self.exchanges[count:]
            for names, state in todo:
                for n, p in zip(names, _exchange_wait("exchange_wait_" + names[0], state)):
                    res = _adamw("adamw_" + n, stored(n), stored('m_' + n), stored('v_' + n), p)
                    for kind, r in zip(('grad_', 'delta_', 'new_m_', 'new_v_'), res):
                        out[kind + n] = as_given(n, r)

    _Chain.last = None
    comm = Comm()
    loss_lanes, grad_x, gs = _local_step(x, mem, positions, sm, comm, target)
    out['loss'] = lax.psum(jnp.sum(loss_lanes), ("x", "y", "c"))
    out['grad_x'] = grad_x[None]

    comm.update(5)
    small_parts = _all_gather("gather_small", [_pack_small(gs)])[0]
    res = _adamw("adamw_small", _pack_small(sm), _pack_small({n: inp['m_' + n] for n in SMALL}),
                 _pack_small({n: inp['v_' + n] for n in SMALL}), small_parts)
    for kind, r in zip(('grad_', 'delta_', 'new_m_', 'new_v_'), res):
        for n, val in _unpack_small(r, sm).items():
            out[kind + n] = val

    comm.update(3)

    names = ['loss', 'grad_x'] + [k + n for k in ('grad_', 'delta_', 'new_m_', 'new_v_') for n in WEIGHTS]
    return tuple(out[n] for n in names)
```

```python
import math

import jax
import jax.numpy as jnp
from jax import lax
from jax.experimental import pallas as pl
from jax.experimental.pallas import tpu as pltpu

F32 = jnp.float32
BF16 = jnp.bfloat16

N_DEV = 8
EPS = 1e-6
CHUNK = 64
MLA_HEADS, MLA_NOPE, MLA_ROPE, MLA_V = 8, 128, 64, 128
MLA_QK = MLA_NOPE + MLA_ROPE
MLA_Q_RANK, MLA_KV_RANK = 512, 256
ROPE_THETA = 10000.0
GLA_HEADS, GLA_DK, GLA_DV, GLA_GATE_RANK = 4, 128, 256, 16
GLA_TAU = 16.0
MEM_HEADS, MEM_HEAD_DIM = 4, 128
ADAM_LR, ADAM_B1, ADAM_B2, ADAM_EPS, ADAM_WD, ADAM_STEP = 0.001, 0.9, 0.999, 1e-08, 0.01, 10

V7X_VMEM_BYTES = 64 * 1024 * 1024
LANES = 128

NN = (((1,), (0,)), ((), ()))
NT = (((1,), (1,)), ((), ()))
TN = (((0,), (0,)), ((), ()))

WEIGHTS = ['ffn1_norm', 'ffn1_w_gate', 'ffn1_w_up', 'ffn1_w_down', 'mix_norm', 'w_in', 'q_a_norm', 'w_q_up',
           'kv_a_norm', 'w_kv_up', 'mla_q_norm', 'mla_k_norm', 'gla_w_gate2', 'gla_b_gate', 'gla_out_norm', 'w_out',
           'mem_attn_norm', 'mem_norm', 'mem_w_q', 'mem_w_k', 'mem_w_v', 'mem_w_o', 'mem_q_norm', 'mem_k_norm',
           'ffn2_norm', 'ffn2_w_gate', 'ffn2_w_up', 'ffn2_w_down']
SMALL = ['ffn1_norm', 'mix_norm', 'q_a_norm', 'kv_a_norm', 'mla_q_norm', 'mla_k_norm', 'gla_b_gate', 'gla_out_norm',
         'mem_attn_norm', 'mem_norm', 'mem_q_norm', 'mem_k_norm', 'ffn2_norm']
MIXER = ['w_in', 'w_q_up', 'w_kv_up', 'gla_w_gate2', 'w_out']
MEMORY = ['mem_w_q', 'mem_w_k', 'mem_w_v', 'mem_w_o']
TRANSPOSED = ['ffn1_w_gate', 'ffn1_w_up', 'ffn2_w_gate', 'ffn2_w_up', 'w_q_up']
GATHERS = [['ffn1_w_gate'], ['ffn1_w_up'], ['ffn1_w_down'], MIXER, MEMORY, ['ffn2_w_gate', 'ffn2_w_up', 'ffn2_w_down']]
NEXT_GATHERS = {0: [1], 1: [2], 2: [3], 3: [4, 5]}
EARLY_FORWARD = {4: 5}

IN_REF = [('zq', 512), ('zkv', 256), ('zkr', 64), ('gq', 512), ('gk', 512), ('gv', 1024), ('zg', 16), ('zr', 1024)]
IN_PAD = [('zq', 512), ('zkv', 256), ('gq', 512), ('gk', 512), ('gv', 1024), ('zr', 1024), ('zkr', 128), ('zg', 128)]
IN_WIDTH = sum(n for _, n in IN_REF)
IN_PAD_WIDTH = sum(n for _, n in IN_PAD)


def _offsets(layout):
    out, off = {}, 0
    for name, n in layout:
        out[name] = off
        off += n
    return out


REF_OFF, PAD_OFF = _offsets(IN_REF), _offsets(IN_PAD)
REF_SIZE = dict(IN_REF)


def _tile(n, pref):
    return pref if n % pref == 0 else n


def _block_bytes(blk, dtype):
    dims = [d for d in blk if d is not None]
    if len(dims) >= 1:
        dims[-1] = -(-dims[-1] // LANES) * LANES
    return math.prod(dims) * jnp.dtype(dtype).itemsize


def _vmem_limit(pipelined_bytes, resident_bytes=0):
    need = 2 * pipelined_bytes + resident_bytes + (8 << 20)
    return int(min(max(need, 32 << 20), V7X_VMEM_BYTES - (6 << 20)))


class _Chain:
    last = None


def _chained(body, *, in_specs, link=0, **kwargs):
    def call(*operands):
        dep = _Chain.last
        if dep is not None and any(o is dep for o in operands):
            dep = None
        if dep is None:
            res = pl.pallas_call(body, in_specs=in_specs, **kwargs)(*operands)
        else:
            n = len(operands)

            def chained_body(*refs):
                body(*refs[:n], *refs[n + 1:])

            res = pl.pallas_call(chained_body, in_specs=list(in_specs) + [pl.BlockSpec(memory_space=pl.ANY)],
                                 **kwargs)(*operands, dep)
        _Chain.last = res[link] if isinstance(res, (list, tuple)) else res
        return res

    return call


def _rowwise_params(*semantics):
    return pltpu.CompilerParams(dimension_semantics=semantics, vmem_limit_bytes=48 << 20)


def _mm(name, a, b, grid, a_spec, b_spec, o_spec, out_shape, out_dtype, dims, scale=None, res=None, fuse=None):
    nk = grid[2]
    (a_blk, a_map), (b_blk, b_map), (o_blk, o_map) = a_spec, b_spec, o_spec
    acc_shape = tuple(d for d in o_blk if d is not None)
    extras = [res] if res is not None else (list(fuse[1]) if fuse else [])
    n_out = 2 if fuse else 1

    def body(*refs):
        a_ref, b_ref = refs[:2]
        extra_refs = refs[2:2 + len(extras)]
        out_refs = refs[2 + len(extras):2 + len(extras) + n_out]
        rest = refs[2 + len(extras) + n_out:]
        r_ref = extra_refs[0] if res is not None else None
        o_ref = out_refs[0]

        def product():
            return lax.dot_general(a_ref[...].astype(BF16), b_ref[...].astype(BF16), dims, preferred_element_type=F32)

        def finish(r):
            if scale is not None:
                r = r * scale
            if r_ref is not None:
                r = r + r_ref[...].astype(F32)
            if fuse:
                for ref, val in zip(out_refs, fuse[0](r, *[e[...].astype(F32) for e in extra_refs])):
                    ref[...] = val.astype(ref.dtype)
            else:
                o_ref[...] = r.astype(o_ref.dtype)

        if nk == 1:
            finish(product())
        else:
            acc = rest[0]
            k = pl.program_id(2)

            @pl.when(k == 0)
            def _():
                acc[...] = product()

            @pl.when(k > 0)
            def _():
                acc[...] += product()

            @pl.when(k == nk - 1)
            def _():
                finish(acc[...])

    in_specs = [pl.BlockSpec(a_blk, a_map), pl.BlockSpec(b_blk, b_map)]
    operands = [a, b]
    piped = _block_bytes(a_blk, a.dtype) + _block_bytes(b_blk, b.dtype) + _block_bytes(o_blk, out_dtype)
    for extra in extras:
        in_specs.append(pl.BlockSpec(o_blk, o_map))
        operands.append(extra)
        piped += _block_bytes(o_blk, extra.dtype)
    piped += (n_out - 1) * _block_bytes(o_blk, out_dtype)
    scratch = [pltpu.VMEM(acc_shape, F32)] if nk > 1 else []
    out_spec, out_sds = pl.BlockSpec(o_blk, o_map), jax.ShapeDtypeStruct(out_shape, out_dtype)
    return _chained(
        body, name=name, grid=grid, in_specs=in_specs, out_specs=(out_spec,) * n_out if fuse else out_spec,
        out_shape=(out_sds,) * n_out if fuse else out_sds, scratch_shapes=scratch,
        compiler_params=pltpu.CompilerParams(
            dimension_semantics=("parallel", "parallel", "arbitrary"),
            vmem_limit_bytes=_vmem_limit(piped, 3 * _block_bytes(acc_shape, F32))),
    )(*operands)


def _mm2(name, a, b, dims, out_dtype, tm=512, tn=1024, tk=2048, scale=None, res=None):
    if dims is NN:
        (m, kk), n = a.shape, b.shape[1]
    elif dims is NT:
        (m, kk), n = a.shape, b.shape[0]
    else:
        (kk, m), n = a.shape, b.shape[1]
    tm, tn, tk = _tile(m, tm), _tile(n, tn), _tile(kk, tk)
    a_spec = ((tk, tm), lambda i, j, k: (k, i)) if dims is TN else ((tm, tk), lambda i, j, k: (i, k))
    b_spec = ((tn, tk), lambda i, j, k: (j, k)) if dims is NT else ((tk, tn), lambda i, j, k: (k, j))
    return _mm(name, a, b, (m // tm, n // tn, kk // tk), a_spec, b_spec, ((tm, tn), lambda i, j, k: (i, j)),
               (m, n), out_dtype, dims, scale, res)


def _mm_stack_out(name, a, b, out_dtype, tm=512, tk=2048):
    (m, kk), (nj, _, n) = a.shape, b.shape
    tm, tk = _tile(m, tm), _tile(kk, tk)
    return _mm(name, a, b, (nj, m // tm, kk // tk), ((tm, tk), lambda j, i, k: (i, k)),
               ((None, tk, n), lambda j, i, k: (j, k, 0)), ((None, tm, n), lambda j, i, k: (j, i, 0)),
               (nj, m, n), out_dtype, NN)


def _mm_stack_nt_out(name, a, b, out_dtype, scale=None, fuse=None, tm=512, tk=2048):
    (m, kk), (nj, n, _) = a.shape, b.shape
    tm, tk = _tile(m, tm), _tile(kk, tk)
    return _mm(name, a, b, (nj, m // tm, kk // tk), ((tm, tk), lambda j, i, k: (i, k)),
               ((None, n, tk), lambda j, i, k: (j, 0, k)), ((None, tm, n), lambda j, i, k: (j, i, 0)),
               (nj, m, n), out_dtype, NT, scale, fuse=fuse)


def _mm_stack_sum(name, a, b, out_dtype, scale=None, res=None, tm=512, tn=1024):
    (nj, m, f), n = a.shape, b.shape[2]
    tm, tn = _tile(m, tm), _tile(n, tn)
    return _mm(name, a, b, (m // tm, n // tn, nj), ((None, tm, f), lambda i, j, k: (k, i, 0)),
               ((None, f, tn), lambda i, j, k: (k, 0, j)), ((tm, tn), lambda i, j, k: (i, j)),
               (m, n), out_dtype, NN, scale, res)


def _mm_stack_nt_sum(name, a, b, out_dtype, res=None, tm=512, tn=1024):
    (nj, m, f), n = a.shape, b.shape[1]
    tm, tn = _tile(m, tm), _tile(n, tn)
    return _mm(name, a, b, (m // tm, n // tn, nj), ((None, tm, f), lambda i, j, k: (k, i, 0)),
               ((None, tn, f), lambda i, j, k: (k, j, 0)), ((tm, tn), lambda i, j, k: (i, j)),
               (m, n), out_dtype, NT, None, res)


def _mm_stack_tn_left(name, a, b, out_dtype, tm=1024, tk=512):
    (m, kp), (nj, _, n) = a.shape, b.shape
    tm, tk = _tile(kp, tm), _tile(m, tk)
    return _mm(name, a, b, (nj, kp // tm, m // tk), ((tk, tm), lambda j, i, k: (k, i)),
               ((None, tk, n), lambda j, i, k: (j, k, 0)), ((None, tm, n), lambda j, i, k: (j, i, 0)),
               (nj, kp, n), out_dtype, TN)


def _mm_stack_tn_right(name, a, b, out_dtype, scale=None, tn=2048, tk=512):
    (nj, m, f), n = a.shape, b.shape[1]
    tn, tk = _tile(n, tn), _tile(m, tk)
    return _mm(name, a, b, (nj, n // tn, m // tk), ((None, tk, f), lambda j, i, k: (j, k, 0)),
               ((tk, tn), lambda j, i, k: (k, i)), ((None, f, tn), lambda j, i, k: (j, 0, i)),
               (nj, f, n), out_dtype, TN, scale)


def _split3(x):
    hi = x.astype(BF16)
    r1 = x - hi.astype(F32)
    mid = r1.astype(BF16)
    lo = (r1 - mid.astype(F32)).astype(BF16)
    return hi, mid, lo


def _swap_halves(x, swap):
    return sum(jnp.dot(part, swap, preferred_element_type=F32) for part in _split3(x))


def _rope_specs(rope, tm):
    cos_t, _, swap = rope
    nt = cos_t.shape[0] // tm
    tab = pl.BlockSpec((tm, cos_t.shape[1]), lambda i, c: (i % nt, 0))
    return [tab, tab, pl.BlockSpec(swap.shape, lambda i, c: (0, 0))]


def _rms_fwd(name, x, g, out_dtype, rope=None, tm=256):
    rows, cols = x.shape
    d = g.shape[1]
    tm = _tile(rows if rope is None else rope[0].shape[0], tm)

    def body(x_ref, g_ref, *refs):
        xf = x_ref[...].astype(F32)
        r = lax.rsqrt(jnp.mean(xf * xf, axis=-1, keepdims=True) + EPS)
        y = xf * r * g_ref[...]
        if rope is not None:
            c_ref, s_ref, p_ref = refs[:3]
            y = y * c_ref[...] + _swap_halves(y, p_ref[...]) * s_ref[...]
        refs[-1][...] = y.astype(refs[-1].dtype)

    return _chained(
        body, name=name, grid=(rows // tm, cols // d),
        in_specs=[pl.BlockSpec((tm, d), lambda i, c: (i, c)), pl.BlockSpec((1, d), lambda i, c: (0, 0))]
        + (_rope_specs(rope, tm) if rope is not None else []),
        out_specs=pl.BlockSpec((tm, d), lambda i, c: (i, c)),
        out_shape=jax.ShapeDtypeStruct((rows, cols), out_dtype),
        compiler_params=_rowwise_params("parallel", "parallel"),
    )(x, g, *(rope or ()))


def _rms_bwd(name, x, g, dy, res=None, rope=None, tm=256):
    rows, cols = x.shape
    d = g.shape[1]
    tm = _tile(rows if rope is None else rope[0].shape[0], tm)

    def body(*refs):
        x_ref, g_ref, dy_ref = refs[:3]
        dx_ref, dg_ref = refs[-2:]
        r_ref = refs[3] if res is not None else None
        xf = x_ref[...].astype(F32)
        r = lax.rsqrt(jnp.mean(xf * xf, axis=-1, keepdims=True) + EPS)
        xhat = xf * r
        dyf = dy_ref[...].astype(F32)
        if rope is not None:
            c_ref, s_ref, p_ref = refs[-5:-2]
            dyf = dyf * c_ref[...] + _swap_halves(dyf * s_ref[...], p_ref[...])

        @pl.when((pl.program_id(0) == 0) & (pl.program_id(1) == 0))
        def _():
            dg_ref[...] = jnp.zeros_like(dg_ref)

        dg_ref[...] += jnp.sum(dyf * xhat, axis=0, keepdims=True)
        dxh = dyf * g_ref[...]
        dx = r * (dxh - xhat * jnp.mean(dxh * xhat, axis=-1, keepdims=True))
        if r_ref is not None:
            dx = dx + r_ref[...].astype(F32)
        dx_ref[...] = dx

    blk = pl.BlockSpec((tm, d), lambda i, c: (i, c))
    in_specs = [blk, pl.BlockSpec((1, d), lambda i, c: (0, 0)), blk]
    operands = [x, g, dy]
    if res is not None:
        in_specs.append(blk)
        operands.append(res)
    if rope is not None:
        in_specs += _rope_specs(rope, tm)
        operands += list(rope)
    return _chained(
        body, name=name, grid=(rows // tm, cols // d), in_specs=in_specs,
        out_specs=(blk, pl.BlockSpec((1, d), lambda i, c: (0, 0))),
        out_shape=(jax.ShapeDtypeStruct((rows, cols), F32), jax.ShapeDtypeStruct((1, d), F32)),
        compiler_params=_rowwise_params("arbitrary", "arbitrary"),
    )(*operands)


def _swiglu_fwd(name, g, u, out_dtype, tm=256):
    rows, cols = g.shape
    tm = _tile(rows, tm)

    def body(g_ref, u_ref, o_ref):
        gf = g_ref[...].astype(F32)
        o_ref[...] = (gf * jax.nn.sigmoid(gf) * u_ref[...].astype(F32)).astype(o_ref.dtype)

    blk = pl.BlockSpec((tm, cols), lambda i: (i, 0))
    return _chained(
        body, name=name, grid=(rows // tm,), in_specs=[blk, blk], out_specs=blk,
        out_shape=jax.ShapeDtypeStruct((rows, cols), out_dtype),
        compiler_params=_rowwise_params("parallel"),
    )(g, u)


def _swiglu_bwd(name, da, g, u, out_dtype, tm=256):
    rows, cols = g.shape
    tm = _tile(rows, tm)

    def body(da_ref, g_ref, u_ref, dg_ref, du_ref):
        gf = g_ref[...].astype(F32)
        daf = da_ref[...].astype(F32)
        sig = jax.nn.sigmoid(gf)
        du_ref[...] = (daf * gf * sig).astype(du_ref.dtype)
        dg_ref[...] = (daf * u_ref[...].astype(F32) * sig * (1.0 + gf * (1.0 - sig))).astype(dg_ref.dtype)

    blk = pl.BlockSpec((tm, cols), lambda i: (i, 0))
    sds = jax.ShapeDtypeStruct((rows, cols), out_dtype)
    return _chained(
        body, name=name, grid=(rows // tm,), in_specs=[blk, blk, blk], out_specs=(blk, blk), out_shape=(sds, sds),
        compiler_params=_rowwise_params("parallel"),
    )(da, g, u)


def _gate_fwd(name, pre, bias, tm=256):
    rows, cols = pre.shape
    tm = _tile(rows, tm)

    def body(p_ref, b_ref, o_ref):
        z = p_ref[...] + b_ref[...]
        o_ref[...] = (jnp.minimum(z, 0.0) - jnp.log(1.0 + jnp.exp(-jnp.abs(z)))) * (1.0 / GLA_TAU)

    blk = pl.BlockSpec((tm, cols), lambda i: (i, 0))
    return _chained(
        body, name=name, grid=(rows // tm,), in_specs=[blk, pl.BlockSpec((1, cols), lambda i: (0, 0))], out_specs=blk,
        out_shape=jax.ShapeDtypeStruct((rows, cols), F32),
        compiler_params=_rowwise_params("parallel"),
    )(pre, bias)


def _gate_bwd(name, pre, bias, dla, tm=256):
    rows, cols = pre.shape
    tm = _tile(rows, tm)

    def body(p_ref, b_ref, d_ref, dp_ref, db_ref):
        z = p_ref[...] + b_ref[...]
        dp = d_ref[...] * (1.0 / GLA_TAU) / (1.0 + jnp.exp(z))
        dp_ref[...] = dp

        @pl.when(pl.program_id(0) == 0)
        def _():
            db_ref[...] = jnp.zeros_like(db_ref)

        db_ref[...] += jnp.sum(dp, axis=0, keepdims=True)

    blk = pl.BlockSpec((tm, cols), lambda i: (i, 0))
    row = pl.BlockSpec((1, cols), lambda i: (0, 0))
    return _chained(
        body, name=name, grid=(rows // tm,), in_specs=[blk, row, blk], out_specs=(blk, row),
        out_shape=(jax.ShapeDtypeStruct((rows, cols), F32), jax.ShapeDtypeStruct((1, cols), F32)),
        compiler_params=_rowwise_params("arbitrary"),
    )(pre, bias, dla)


def _loss(name, y, target, tm=256):
    rows, d = y.shape
    tm = _tile(rows, tm)

    def body(y_ref, t_ref, dy_ref, l_ref):
        err = y_ref[...] - t_ref[...]
        dy_ref[...] = err * (1.0 / d)

        @pl.when(pl.program_id(0) == 0)
        def _():
            l_ref[...] = jnp.zeros_like(l_ref)

        sq = (err * err).reshape(tm // 8, 8, d)
        l_ref[...] += jnp.sum(sq, axis=0) * (0.5 / d)

    blk = pl.BlockSpec((tm, d), lambda i: (i, 0))
    return _chained(
        body, name=name, grid=(rows // tm,), in_specs=[blk, blk],
        out_specs=(blk, pl.BlockSpec((8, d), lambda i: (0, 0))),
        out_shape=(jax.ShapeDtypeStruct((rows, d), F32), jax.ShapeDtypeStruct((8, d), F32)),
        compiler_params=_rowwise_params("arbitrary"),
    )(y, target)


def _scores(q, k, scale, causal, q0):
    s = lax.dot_general(q, k, NT, preferred_element_type=F32) * scale
    if causal:
        qc = (q0 + lax.broadcasted_iota(jnp.int32, s.shape, 0)) // CHUNK
        kc = lax.broadcasted_iota(jnp.int32, s.shape, 1) // CHUNK
        s = jnp.where(kc <= qc, s, -1e30)
    e = jnp.exp(s - jnp.max(s, axis=-1, keepdims=True))
    return e, jnp.sum(e, axis=-1, keepdims=True)


def _attn_fwd(name, q, k, v, scale, causal, tq=256):
    nh, t, dk = q.shape
    tk, dv = k.shape[1], v.shape[2]
    tq = _tile(t, tq)

    def body(q_ref, k_ref, v_ref, o_ref):
        e, l = _scores(q_ref[...], k_ref[...], scale, causal, pl.program_id(1) * tq)
        o = jnp.dot(e.astype(BF16), v_ref[...], preferred_element_type=F32)
        o_ref[...] = (o / l).astype(o_ref.dtype)

    return _chained(
        body, name=name, grid=(nh, t // tq),
        in_specs=[pl.BlockSpec((None, tq, dk), lambda h, i: (h, i, 0)), pl.BlockSpec((None, tk, dk), lambda h, i: (h, 0, 0)),
                  pl.BlockSpec((None, tk, dv), lambda h, i: (h, 0, 0))],
        out_specs=pl.BlockSpec((tq, dv), lambda h, i: (i, h)),
        out_shape=jax.ShapeDtypeStruct((t, nh * dv), BF16),
        compiler_params=pltpu.CompilerParams(dimension_semantics=("parallel", "parallel"),
                                             vmem_limit_bytes=_vmem_limit(0, 6 * tq * tk * 4)),
    )(q, k, v)


def _attn_bwd(name, q, k, v, do, scale, causal, tq=256):
    nh, t, dk = q.shape
    tk, dv = k.shape[1], v.shape[2]
    tq = _tile(t, tq)

    def body(q_ref, k_ref, v_ref, do_ref, dq_ref, dk_ref, dv_ref):
        qb, kb = q_ref[...], k_ref[...]
        e, l = _scores(qb, kb, scale, causal, pl.program_id(1) * tq)
        p = e / l
        dob = do_ref[...].astype(BF16)
        dp = lax.dot_general(dob, v_ref[...], NT, preferred_element_type=F32)
        ds = (p * (dp - jnp.sum(p * dp, axis=-1, keepdims=True)) * scale).astype(BF16)
        dq_ref[...] = jnp.dot(ds, kb, preferred_element_type=F32)

        @pl.when(pl.program_id(1) == 0)
        def _():
            dk_ref[...] = jnp.zeros_like(dk_ref)
            dv_ref[...] = jnp.zeros_like(dv_ref)

        dk_ref[...] += lax.dot_general(ds, qb, TN, preferred_element_type=F32)
        dv_ref[...] += lax.dot_general(p.astype(BF16), dob, TN, preferred_element_type=F32)

    return _chained(
        body, name=name, grid=(nh, t // tq),
        in_specs=[pl.BlockSpec((None, tq, dk), lambda h, i: (h, i, 0)), pl.BlockSpec((None, tk, dk), lambda h, i: (h, 0, 0)),
                  pl.BlockSpec((None, tk, dv), lambda h, i: (h, 0, 0)), pl.BlockSpec((tq, dv), lambda h, i: (i, h))],
        out_specs=(pl.BlockSpec((None, tq, dk), lambda h, i: (h, i, 0)), pl.BlockSpec((None, tk, dk), lambda h, i: (h, 0, 0)),
                   pl.BlockSpec((None, tk, dv), lambda h, i: (h, 0, 0))),
        out_shape=(jax.ShapeDtypeStruct((nh, t, dk), F32), jax.ShapeDtypeStruct((nh, tk, dk), F32),
                   jax.ShapeDtypeStruct((nh, tk, dv), F32)),
        compiler_params=pltpu.CompilerParams(dimension_semantics=("parallel", "arbitrary"),
                                             vmem_limit_bytes=_vmem_limit(0, 10 * tq * tk * 4)),
    )(q, k, v, do)


def _tri(lower):
    r = lax.broadcasted_iota(jnp.int32, (CHUNK, CHUNK), 0)
    c = lax.broadcasted_iota(jnp.int32, (CHUNK, CHUNK), 1)
    return jnp.where((c <= r) if lower else (c >= r), 1.0, 0.0).astype(BF16)


def _tri_dot(tri, x):
    return sum(jnp.dot(tri, part, preferred_element_type=F32) for part in _split3(x))


def _gla_fwd(name, q, k, v, la, nh):
    t = q.shape[0]
    dk, dv = q.shape[1] // nh, v.shape[1] // nh
    nc = t // CHUNK

    def body(q_ref, k_ref, v_ref, g_ref, o_ref, st_ref, state):
        @pl.when(pl.program_id(1) == 0)
        def _():
            state[...] = jnp.zeros_like(state)

        g = g_ref[...]
        b = _tri_dot(_tri(True), g)
        b_end = jnp.sum(g, axis=0, keepdims=True)
        k_dec = (k_ref[...] * jnp.exp(b_end - b)).astype(BF16)
        u_t = lax.dot_general(v_ref[...].astype(BF16), k_dec, TN, preferred_element_type=F32)
        new = state[...] * jnp.exp(b_end) + u_t
        state[...] = new
        st_ref[...] = new
        qc = (q_ref[...] * (dk ** -0.5)).astype(BF16)
        o_ref[...] = lax.dot_general(qc, new.astype(BF16), NT, preferred_element_type=F32)

    kblk = pl.BlockSpec((CHUNK, dk), lambda h, n: (n, h))
    vblk = pl.BlockSpec((CHUNK, dv), lambda h, n: (n, h))
    return _chained(
        body, name=name, grid=(nh, nc), in_specs=[kblk, kblk, vblk, kblk],
        out_specs=(vblk, pl.BlockSpec((None, None, dv, dk), lambda h, n: (h, n, 0, 0))),
        out_shape=(jax.ShapeDtypeStruct((t, nh * dv), F32), jax.ShapeDtypeStruct((nh, nc, dv, dk), F32)),
        scratch_shapes=[pltpu.VMEM((dv, dk), F32)],
        compiler_params=_rowwise_params("parallel", "arbitrary"),
    )(q, k, v, la)


def _gla_bwd(name, q, k, v, la, states, do, nh):
    t = q.shape[0]
    dk, dv = q.shape[1] // nh, v.shape[1] // nh
    nc = t // CHUNK
    scale = dk ** -0.5

    def body(q_ref, k_ref, v_ref, g_ref, do_ref, st_ref, sp_ref, dq_ref, dk_ref, dv_ref, dg_ref, carry):
        i = pl.program_id(1)

        @pl.when(i == 0)
        def _():
            carry[...] = jnp.zeros_like(carry)

        g = g_ref[...]
        b = _tri_dot(_tri(True), g)
        b_end = jnp.sum(g, axis=0, keepdims=True)
        w = jnp.exp(b_end - b)
        decay = jnp.exp(b_end)
        k_dec = k_ref[...] * w
        qc = (q_ref[...] * scale).astype(BF16)
        dob = do_ref[...].astype(BF16)
        dq_ref[...] = jnp.dot(dob, st_ref[...].astype(BF16), preferred_element_type=F32) * scale
        g_t = carry[...] + lax.dot_general(dob, qc, TN, preferred_element_type=F32)
        g_tb = g_t.astype(BF16)
        dk_dec = jnp.dot(v_ref[...].astype(BF16), g_tb, preferred_element_type=F32)
        dv_ref[...] = lax.dot_general(k_dec.astype(BF16), g_tb, NT, preferred_element_type=F32)
        prev = jnp.where(i < nc - 1, sp_ref[...], 0.0)
        ddecay = jnp.sum(g_t * prev, axis=0, keepdims=True)
        dk_ref[...] = dk_dec * w
        e = dk_dec * k_dec
        db_end = jnp.sum(e, axis=0, keepdims=True) + ddecay * decay
        dg_ref[...] = _tri_dot(_tri(False), -e) + db_end
        carry[...] = g_t * decay

    kblk = pl.BlockSpec((CHUNK, dk), lambda h, i: (nc - 1 - i, h))
    vblk = pl.BlockSpec((CHUNK, dv), lambda h, i: (nc - 1 - i, h))
    ksds = jax.ShapeDtypeStruct((t, nh * dk), F32)
    return _chained(
        body, name=name, grid=(nh, nc),
        in_specs=[kblk, kblk, vblk, kblk, vblk,
                  pl.BlockSpec((None, None, dv, dk), lambda h, i: (h, nc - 1 - i, 0, 0)),
                  pl.BlockSpec((None, None, dv, dk), lambda h, i: (h, jnp.maximum(nc - 2 - i, 0), 0, 0))],
        out_specs=(kblk, kblk, vblk, kblk),
        out_shape=(ksds, ksds, jax.ShapeDtypeStruct((t, nh * dv), F32), ksds),
        scratch_shapes=[pltpu.VMEM((dv, dk), F32)],
        compiler_params=_rowwise_params("parallel", "arbitrary"),
    )(q, k, v, la, do, states, states)


def _adamw(name, w, m, v, parts):
    rows, cols = w.shape
    tm = _tile(rows, 1 << int(math.log2(max(8, (1 << 17) // (-(-cols // LANES) * LANES)))))

    def body(w_ref, m_ref, v_ref, p_ref, g_ref, d_ref, nm_ref, nv_ref):
        g = p_ref[0].astype(F32)
        for s in range(1, N_DEV):
            g = g + p_ref[s].astype(F32)
        m_new = ADAM_B1 * m_ref[...] + (1.0 - ADAM_B1) * g
        v_new = ADAM_B2 * v_ref[...] + (1.0 - ADAM_B2) * jnp.square(g)
        m_hat = m_new / (1.0 - ADAM_B1 ** ADAM_STEP)
        v_hat = v_new / (1.0 - ADAM_B2 ** ADAM_STEP)
        g_ref[...] = g
        d_ref[...] = -ADAM_LR * (m_hat / (jnp.sqrt(v_hat) + ADAM_EPS) + ADAM_WD * w_ref[...])
        nm_ref[...] = m_new
        nv_ref[...] = v_new

    blk = pl.BlockSpec((tm, cols), lambda i: (i, 0))
    sds = jax.ShapeDtypeStruct((rows, cols), F32)
    return _chained(
        body, name=name, grid=(rows // tm,),
        in_specs=[blk, blk, blk, pl.BlockSpec((N_DEV, tm, cols), lambda i: (0, i, 0))],
        out_specs=(blk, blk, blk, blk), out_shape=(sds, sds, sds, sds),
        compiler_params=_rowwise_params("parallel"),
    )(w, m, v, parts)


HBM = pl.BlockSpec(memory_space=pl.ANY)
MESH = pl.DeviceIdType.MESH


def _all_gather(name, shards):
    n = len(shards)

    def body(*refs):
        ins, outs = refs[:n], refs[n:2 * n]
        send_sems, recv_sems, local_sems = refs[2 * n:]
        x, y, c = lax.axis_index("x"), lax.axis_index("y"), lax.axis_index("c")
        me, sibling = (x, y, c), (x, y, 1 - c)
        chips = [(1 - x, y), (x, 1 - y), (1 - x, 1 - y)]

        def copy(w, k, block, to, src=None):
            dst = outs[w].at[4 * block[0] + 2 * block[1] + block[2]]
            return pltpu.make_async_remote_copy(
                src_ref=dst if src is None else src, dst_ref=dst, send_sem=send_sems.at[7 * w + k],
                recv_sem=recv_sems.at[7 * w + k], device_id=to, device_id_type=MESH)

        mine, first, passed = [], [], []
        for w in range(n):
            cp = pltpu.make_async_copy(ins[w], outs[w].at[4 * x + 2 * y + c], local_sems.at[w])
            cp.start()
            mine.append(cp)
            first.append(copy(w, 0, me, sibling, src=ins[w]))
            first += [copy(w, 1 + j, me, (*chip, c), src=ins[w]) for j, chip in enumerate(chips)]
        for cp in first:
            cp.start()
        for w in range(n):
            for j, chip in enumerate(chips):
                copy(w, 1 + j, (*chip, c), me).wait_recv()
                cp = copy(w, 4 + j, (*chip, c), sibling)
                cp.start()
                passed.append(cp)
        for w in range(n):
            copy(w, 0, sibling, me).wait_recv()
            for j, chip in enumerate(chips):
                copy(w, 4 + j, (*chip, 1 - c), me).wait_recv()
        for cp in first + passed:
            cp.wait_send()
        for cp in mine:
            cp.wait()

    return _chained(
        body, name=name, in_specs=[HBM] * n, out_specs=[HBM] * n,
        out_shape=[jax.ShapeDtypeStruct((N_DEV,) + s.shape, s.dtype) for s in shards],
        scratch_shapes=[pltpu.SemaphoreType.DMA((7 * n,)), pltpu.SemaphoreType.DMA((7 * n,)),
                        pltpu.SemaphoreType.DMA((n,))],
    )(*shards)


HBM_SPEC = pl.BlockSpec(memory_space=pltpu.HBM)
SEM_SPEC = pl.BlockSpec(memory_space=pltpu.SEMAPHORE)
EFFECT = pltpu.SideEffectType.DATAFLOW_SIDE_EFFECTING
TOKEN = jax.ShapeDtypeStruct((8, LANES), F32)


def _in_hbm(a):
    return pltpu.with_memory_space_constraint(a, pltpu.HBM)


def _place():
    x, y, c = lax.axis_index("x"), lax.axis_index("y"), lax.axis_index("c")
    chips = [(1 - x, y), (x, 1 - y), (1 - x, 1 - y)]
    return x, y, c, chips


def _block_of(px, py, pc):
    return 4 * px + 2 * py + pc


def _gather_copies(ins, outs, sems, w):
    send_sems, recv_sems, local_sems = sems
    x, y, c, chips = _place()
    peers = [(x, y, 1 - c)] + [(*chip, c) for chip in chips]

    def copy(k, block, to, src):
        dst = outs[w].at[_block_of(*block)]
        return pltpu.make_async_remote_copy(src_ref=dst if src is None else src, dst_ref=dst, send_sem=send_sems.at[4 * w + k],
                                            recv_sem=recv_sems.at[4 * w + k], device_id=to, device_id_type=MESH)

    local = pltpu.make_async_copy(ins[w], outs[w].at[_block_of(x, y, c)], local_sems.at[w])
    sends = [copy(k, (x, y, c), peer, ins[w]) for k, peer in enumerate(peers)]
    recvs = [copy(k, peer, (x, y, c), None) for k, peer in enumerate(peers)]
    return local, sends, recvs


def _forward_copies(outs, sems, w):
    send_sems, recv_sems = sems
    x, y, c, chips = _place()

    def copy(j, block, to):
        dst = outs[w].at[_block_of(*block)]
        return pltpu.make_async_remote_copy(src_ref=dst, dst_ref=dst, send_sem=send_sems.at[3 * w + j],
                                            recv_sem=recv_sems.at[3 * w + j], device_id=to, device_id_type=MESH)

    sends = [copy(j, (*chip, c), (x, y, 1 - c)) for j, chip in enumerate(chips)]
    recvs = [copy(j, (*chip, 1 - c), (x, y, c)) for j, chip in enumerate(chips)]
    return sends, recvs


def _gather_start(name, shards):
    n = len(shards)

    def body(*refs):
        ins, outs, sems = refs[:n], refs[n:2 * n], refs[2 * n:2 * n + 3]
        for w in range(n):
            local, sends, _ = _gather_copies(ins, outs, sems, w)
            local.start()
            for cp in sends:
                cp.start()
        refs[-1][...] = jnp.zeros_like(refs[-1])

    lands = [lax.empty((N_DEV,) + s.shape, s.dtype) for s in shards]
    res = _chained(
        body, name=name, link=-1, in_specs=[HBM_SPEC] * (2 * n),
        out_specs=[SEM_SPEC] * 3 + [HBM_SPEC] * (2 * n) + [pl.BlockSpec(memory_space=pltpu.VMEM)],
        out_shape=[pltpu.SemaphoreType.DMA((4 * n,)), pltpu.SemaphoreType.DMA((4 * n,)), pltpu.SemaphoreType.DMA((n,))]
        + [pltpu.HBM(s.shape, s.dtype) for s in shards] + [pltpu.HBM(l.shape, l.dtype) for l in lands] + [TOKEN],
        input_output_aliases={i: 3 + i for i in range(2 * n)},
        compiler_params=pltpu.CompilerParams(has_side_effects=EFFECT),
    )(*[_in_hbm(s) for s in shards], *[_in_hbm(l) for l in lands])
    return n, res[:3], res[3:3 + n], res[3 + n:3 + 2 * n]


def _gather_mid(name, state):
    n, sems, shards, lands = state

    def body(*refs):
        ins, outs, sems_in = refs[:n], refs[n:2 * n], refs[2 * n:2 * n + 3]
        sems_out = refs[2 * n + 3:2 * n + 5]
        for w in range(n):
            local, sends, recvs = _gather_copies(ins, outs, sems_in, w)
            local.wait()
            for cp in sends:
                cp.wait_send()
            for cp in recvs:
                cp.wait_recv()
            for cp in _forward_copies(outs, sems_out, w)[0]:
                cp.start()
        refs[-1][...] = jnp.zeros_like(refs[-1])

    res = _chained(
        body, name=name, link=-1, in_specs=[HBM_SPEC] * (2 * n) + [SEM_SPEC] * 3,
        out_specs=[SEM_SPEC] * 2 + [HBM_SPEC] * n + [pl.BlockSpec(memory_space=pltpu.VMEM)],
        out_shape=[pltpu.SemaphoreType.DMA((3 * n,)), pltpu.SemaphoreType.DMA((3 * n,))]
        + [pltpu.HBM(l.shape, l.dtype) for l in lands] + [TOKEN],
        input_output_aliases={n + i: 2 + i for i in range(n)},
        compiler_params=pltpu.CompilerParams(has_side_effects=EFFECT),
    )(*shards, *lands, *sems)
    return n, res[:2], res[2:2 + n]


def _gather_end(name, state):
    n, sems, lands = state

    def body(*refs):
        outs, sems_in = refs[:n], refs[n:n + 2]
        for w in range(n):
            sends, recvs = _forward_copies(outs, sems_in, w)
            for cp in sends:
                cp.wait_send()
            for cp in recvs:
                cp.wait_recv()

    return _chained(
        body, name=name, in_specs=[HBM_SPEC] * n + [SEM_SPEC] * 2, out_specs=[HBM_SPEC] * n,
        out_shape=[pltpu.HBM(l.shape, l.dtype) for l in lands], input_output_aliases={i: i for i in range(n)},
        compiler_params=pltpu.CompilerParams(has_side_effects=EFFECT),
    )(*lands, *sems)


def _exchange_copies(ins, outs, sems, w):
    send_sems, recv_sems, local_sems = sems
    x, y, c, _ = _place()
    mine = _block_of(x, y, c)
    local = pltpu.make_async_copy(ins[w].at[mine], outs[w].at[mine], local_sems.at[w])
    remote = []
    for k in range(1, N_DEV):
        px, py, pc = x ^ (k >> 2), y ^ ((k >> 1) & 1), c ^ (k & 1)
        remote.append(pltpu.make_async_remote_copy(
            src_ref=ins[w].at[_block_of(px, py, pc)], dst_ref=outs[w].at[mine], send_sem=send_sems.at[7 * w + k - 1],
            recv_sem=recv_sems.at[7 * w + k - 1], device_id=(px, py, pc), device_id_type=MESH))
    return local, remote


def _exchange_start(name, stacks):
    n = len(stacks)

    def body(*refs):
        ins, outs, sems = refs[:n], refs[n:2 * n], refs[2 * n:2 * n + 3]
        for w in range(n):
            local, remote = _exchange_copies(ins, outs, sems, w)
            local.start()
            for cp in remote:
                cp.start()
        refs[-1][...] = jnp.zeros_like(refs[-1])

    lands = [lax.empty(s.shape, s.dtype) for s in stacks]
    if any(s is _Chain.last for s in stacks):
        _Chain.last = None
    res = _chained(
        body, name=name, link=-1, in_specs=[HBM_SPEC] * (2 * n),
        out_specs=[SEM_SPEC] * 3 + [HBM_SPEC] * (2 * n) + [pl.BlockSpec(memory_space=pltpu.VMEM)],
        out_shape=[pltpu.SemaphoreType.DMA((7 * n,)), pltpu.SemaphoreType.DMA((7 * n,)), pltpu.SemaphoreType.DMA((n,))]
        + [pltpu.HBM(s.shape, s.dtype) for s in stacks] * 2 + [TOKEN],
        input_output_aliases={i: 3 + i for i in range(2 * n)},
        compiler_params=pltpu.CompilerParams(has_side_effects=EFFECT),
    )(*[_in_hbm(s) for s in stacks], *[_in_hbm(l) for l in lands])
    return n, res[:3], res[3:3 + n], res[3 + n:3 + 2 * n]


def _exchange_wait(name, state):
    n, sems, stacks, lands = state

    def body(*refs):
        ins, outs, sems_in = refs[:n], refs[n:2 * n], refs[2 * n:2 * n + 3]
        for w in range(n):
            local, remote = _exchange_copies(ins, outs, sems_in, w)
            local.wait()
            for cp in remote:
                cp.wait_send()
                cp.wait_recv()

    return _chained(
        body, name=name, in_specs=[HBM_SPEC] * (2 * n) + [SEM_SPEC] * 3, out_specs=[HBM_SPEC] * n,
        out_shape=[pltpu.HBM(l.shape, l.dtype) for l in lands], input_output_aliases={n + i: i for i in range(n)},
        compiler_params=pltpu.CompilerParams(has_side_effects=EFFECT),
    )(*stacks, *lands, *sems)


def _unstack_cols(w):
    return w.transpose(1, 0, 2).reshape(w.shape[1], N_DEV * w.shape[2])


def _stack_cols(w):
    return w.reshape(w.shape[0], N_DEV, w.shape[1] // N_DEV).transpose(1, 0, 2)


def _rope_tables(positions):
    half = MLA_ROPE // 2
    inv_freq = ROPE_THETA ** (-jnp.arange(half, dtype=F32) / half)
    ang = positions.astype(F32)[:, None] * inv_freq
    cos, sin = jnp.cos(ang), jnp.sin(ang)
    t = positions.shape[0]
    cos_t = jnp.concatenate([jnp.ones((t, MLA_NOPE), F32), cos, cos], axis=1)
    sin_t = jnp.concatenate([jnp.zeros((t, MLA_NOPE), F32), -sin, sin], axis=1)
    idx = jnp.arange(MLA_QK)
    partner = jnp.where(idx < MLA_NOPE, -1, jnp.where(idx < MLA_NOPE + half, idx + half, idx - half))
    swap = (idx[:, None] == partner[None, :]).astype(BF16)
    return cos_t, sin_t, swap


def _ffn_fwd(tag, x, gain, wt):
    h = _rms_fwd(tag + "_rms", x, gain, BF16)
    g = _mm_stack_nt_out(tag + "_gate", h, wt[tag + '_w_gate'], BF16)
    u, a = _mm_stack_nt_out(tag + "_up", h, wt[tag + '_w_up'], BF16,
                            fuse=(lambda u_blk, g_blk: (u_blk, g_blk * jax.nn.sigmoid(g_blk) * u_blk), [g]))
    y = _mm_stack_sum(tag + "_down", a, wt[tag + '_w_down'], F32, scale=0.5, res=x)
    return y, (x, h, g, u, a)


def _ffn_bwd_weights(tag, dy, saved, wd, comm):
    x, h, g, u, a = saved
    comm.grads({tag + '_w_down': _mm_stack_tn_right(tag + "_dwd", a, dy, BF16, scale=0.5)})

    def dact(da, g_blk, u_blk):
        sig = jax.nn.sigmoid(g_blk)
        return da * u_blk * sig * (1.0 + g_blk * (1.0 - sig)), da * g_blk * sig

    dg, du = _mm_stack_nt_out(tag + "_da", dy, wd, BF16, scale=0.5, fuse=(dact, [g, u]))
    comm.grads({tag + '_w_gate': _mm_stack_tn_right(tag + "_dwg", dg, h, BF16)})
    comm.grads({tag + '_w_up': _mm_stack_tn_right(tag + "_dwu", du, h, BF16)})
    return dg, du


def _ffn_bwd_input(tag, dy, saved, dgu, gain, wg, wu):
    dg, du = dgu
    dh = _mm_stack_sum(tag + "_dh_g", dg, wg, F32)
    dh = _mm_stack_sum(tag + "_dh_u", du, wu, F32, res=dh)
    return _rms_bwd(tag + "_drms", saved[0], gain, dh, res=dy)


def _local_step(x, mem, positions, sm, comm, target):
    t, d = x.shape
    nm = mem.shape[0]
    gs, gw = {}, {}

    class Weights(dict):
        def __missing__(self, name):
            self.update(comm.weights(next(k for k, names in enumerate(GATHERS) if name in names)))
            return self[name]

    wt = Weights()

    x1, ffn1_saved = _ffn_fwd("ffn1", x, sm['ffn1_norm'], wt)

    h2 = _rms_fwd("mix_rms", x1, sm['mix_norm'], BF16)
    w_in_ref = _unstack_cols(wt['w_in'])
    pieces = []
    for name, n in IN_PAD:
        piece = w_in_ref[:, REF_OFF[name]:REF_OFF[name] + REF_SIZE[name]]
        if n != REF_SIZE[name]:
            piece = jnp.pad(piece, ((0, 0), (0, n - REF_SIZE[name])))
        pieces.append(piece)
    w_in = jnp.concatenate(pieces, axis=1)
    z = _mm2("mix_in", h2, w_in, NN, F32)
    zs = {name: z[:, PAD_OFF[name]:PAD_OFF[name] + n] for name, n in IN_PAD}

    cq = _rms_fwd("mla_q_a_rms", zs['zq'], sm['q_a_norm'], BF16)
    q_raw = _mm_stack_nt_out("mla_q_up", cq, wt['w_q_up'], F32)
    ckv = _rms_fwd("mla_kv_a_rms", zs['zkv'], sm['kv_a_norm'], BF16)
    kv = _mm_stack_out("mla_kv_up", ckv, wt['w_kv_up'], F32)
    zkr = zs['zkr'][:, :MLA_ROPE]
    k_raw = jnp.concatenate([kv[:, :, :MLA_NOPE], jnp.broadcast_to(zkr[None], (MLA_HEADS, t, MLA_ROPE))], axis=2)
    v_mla = kv[:, :, MLA_NOPE:].astype(BF16)
    cos_t, sin_t, swap = _rope_tables(positions)
    q_raw2, k_raw2 = q_raw.reshape(MLA_HEADS * t, MLA_QK), k_raw.reshape(MLA_HEADS * t, MLA_QK)
    rope = (cos_t, sin_t, swap)
    qf = _rms_fwd("mla_q_rms", q_raw2, sm['mla_q_norm'], BF16, rope=rope).reshape(MLA_HEADS, t, MLA_QK)
    kf = _rms_fwd("mla_k_rms", k_raw2, sm['mla_k_norm'], BF16, rope=rope).reshape(MLA_HEADS, t, MLA_QK)
    o_mla = _attn_fwd("mla_attn", qf, kf, v_mla, MLA_QK ** -0.5, True)

    w_g2 = jnp.pad(_unstack_cols(wt['gla_w_gate2']), ((0, LANES - GLA_GATE_RANK), (0, 0)))
    pre = _mm2("gla_gate_pre", zs['zg'], w_g2, NN, F32)
    log_a = _gate_fwd("gla_gate", pre, sm['gla_b_gate'])
    o_gla_raw, states = _gla_fwd("gla_scan", zs['gq'], zs['gk'], zs['gv'], log_a, GLA_HEADS)
    o_gla_n = _rms_fwd("gla_out_rms", o_gla_raw, sm['gla_out_norm'], F32)
    o_gla = _swiglu_fwd("gla_out_gate", zs['zr'], o_gla_n, BF16)

    cat = jnp.concatenate([o_mla, o_gla], axis=1)
    w_out = wt['w_out'].reshape(d, d)
    x2 = _mm2("mix_out", cat, w_out, NN, F32, res=x1)

    w_mq, w_mk, w_mv = (wt[n].reshape(d, MEM_HEADS * MEM_HEAD_DIM) for n in ('mem_w_q', 'mem_w_k', 'mem_w_v'))
    hq = _rms_fwd("mem_attn_rms", x2, sm['mem_attn_norm'], BF16)
    hm = _rms_fwd("mem_rms", mem, sm['mem_norm'], BF16)

    def heads_out(name, a, b, out_dtype):
        m, kk = a.shape
        tm = _tile(m, 512)
        return _mm(name, a, b, (m // tm, MEM_HEADS, 1), ((tm, kk), lambda i, h, k: (i, 0)),
                   ((kk, MEM_HEAD_DIM), lambda i, h, k: (0, h)), ((None, tm, MEM_HEAD_DIM), lambda i, h, k: (h, i, 0)),
                   (MEM_HEADS, m, MEM_HEAD_DIM), out_dtype, NN)

    mq_raw = heads_out("mem_q", hq, w_mq, F32)
    mk_raw = heads_out("mem_k", hm, w_mk, F32)
    mv = heads_out("mem_v", hm, w_mv, BF16)
    mq = _rms_fwd("mem_q_rms", mq_raw.reshape(MEM_HEADS * t, MEM_HEAD_DIM), sm['mem_q_norm'], BF16)
    mk = _rms_fwd("mem_k_rms", mk_raw.reshape(MEM_HEADS * nm, MEM_HEAD_DIM), sm['mem_k_norm'], BF16)
    mq, mk = mq.reshape(MEM_HEADS, t, MEM_HEAD_DIM), mk.reshape(MEM_HEADS, nm, MEM_HEAD_DIM)
    o_mem = _attn_fwd("mem_attn", mq, mk, mv, MEM_HEAD_DIM ** -0.5, False)
    w_mo = wt['mem_w_o']
    mo_cols = w_mo.shape[2]
    tm = _tile(t, 512)
    x3 = _mm("mem_out", o_mem, w_mo, (t // tm, N_DEV, 1), ((tm, o_mem.shape[1]), lambda i, j, k: (i, 0)),
             ((None, o_mem.shape[1], mo_cols), lambda i, j, k: (j, 0, 0)), ((tm, mo_cols), lambda i, j, k: (i, j)),
             (t, d), F32, NN, res=x2)

    y, ffn2_saved = _ffn_fwd("ffn2", x3, sm['ffn2_norm'], wt)
    dy, loss_lanes = _loss("loss", y, target)

    dgu = _ffn_bwd_weights("ffn2", dy, ffn2_saved, wt['ffn2_w_down'], comm)
    dx3, gs['ffn2_norm'] = _ffn_bwd_input("ffn2", dy, ffn2_saved, dgu, sm['ffn2_norm'],
                                          wt['ffn2_w_gate'], wt['ffn2_w_up'])

    do_mem = _mm("mem_do", dx3, w_mo, (t // tm, MEM_HEADS, N_DEV), ((tm, mo_cols), lambda i, h, k: (i, k)),
                 ((None, MEM_HEAD_DIM, mo_cols), lambda i, h, k: (k, h, 0)), ((tm, MEM_HEAD_DIM), lambda i, h, k: (i, h)),
                 (t, MEM_HEADS * MEM_HEAD_DIM), BF16, NT)
    tk = _tile(t, 512)
    gw['mem_w_o'] = _mm("mem_dwo", o_mem, dx3, (N_DEV, 1, t // tk), ((tk, o_mem.shape[1]), lambda j, i, k: (k, 0)),
                        ((tk, mo_cols), lambda j, i, k: (k, j)), ((None, o_mem.shape[1], mo_cols), lambda j, i, k: (j, 0, 0)),
                        w_mo.shape, BF16, TN)
    dmq, dmk, dmv = _attn_bwd("mem_dattn", mq, mk, mv, do_mem, MEM_HEAD_DIM ** -0.5, False)
    dmq_raw, gs['mem_q_norm'] = _rms_bwd("mem_q_drms", mq_raw.reshape(MEM_HEADS * t, MEM_HEAD_DIM), sm['mem_q_norm'],
                                         dmq.reshape(MEM_HEADS * t, MEM_HEAD_DIM))
    dmk_raw, gs['mem_k_norm'] = _rms_bwd("mem_k_drms", mk_raw.reshape(MEM_HEADS * nm, MEM_HEAD_DIM), sm['mem_k_norm'],
                                         dmk.reshape(MEM_HEADS * nm, MEM_HEAD_DIM))
    dmq_raw = dmq_raw.reshape(MEM_HEADS, t, MEM_HEAD_DIM)
    dmk_raw = dmk_raw.reshape(MEM_HEADS, nm, MEM_HEAD_DIM)

    def heads_in_nt(name, a, b, res=None):
        m, n = a.shape[1], b.shape[0]
        tm_, tn_ = _tile(m, 512), _tile(n, 1024)
        return _mm(name, a, b, (m // tm_, n // tn_, MEM_HEADS), ((None, tm_, MEM_HEAD_DIM), lambda i, j, k: (k, i, 0)),
                   ((tn_, MEM_HEAD_DIM), lambda i, j, k: (j, k)), ((tm_, tn_), lambda i, j, k: (i, j)), (m, n), F32, NT,
                   None, res)

    def heads_tn(name, a, b):
        m, kp = a.shape
        tm_, tk_ = _tile(kp, 1024), _tile(m, 512)
        return _mm(name, a, b, (kp // tm_, MEM_HEADS, m // tk_), ((tk_, tm_), lambda i, h, k: (k, i)),
                   ((None, tk_, MEM_HEAD_DIM), lambda i, h, k: (h, k, 0)), ((tm_, MEM_HEAD_DIM), lambda i, h, k: (i, h)),
                   (kp, MEM_HEADS * MEM_HEAD_DIM), BF16, TN)

    dhq = heads_in_nt("mem_dhq", dmq_raw, w_mq)
    gw['mem_w_q'] = heads_tn("mem_dwq", hq, dmq_raw).reshape(wt['mem_w_q'].shape)
    dhm = heads_in_nt("mem_dhm_k", dmk_raw, w_mk)
    dhm = heads_in_nt("mem_dhm_v", dmv, w_mv, res=dhm)
    gw['mem_w_k'] = heads_tn("mem_dwk", hm, dmk_raw).reshape(wt['mem_w_k'].shape)
    gw['mem_w_v'] = heads_tn("mem_dwv", hm, dmv).reshape(wt['mem_w_v'].shape)
    _, gs['mem_norm'] = _rms_bwd("mem_drms", mem, sm['mem_norm'], dhm)
    comm.grads({n: gw[n] for n in MEMORY})
    dx2, gs['mem_attn_norm'] = _rms_bwd("mem_attn_drms", x2, sm['mem_attn_norm'], dhq, res=dx3)

    dcat = _mm2("mix_dcat", dx2, w_out, NT, F32)
    gw['w_out'] = _mm2("mix_dwout", cat, dx2, TN, BF16, tm=1024, tn=2048, tk=512).reshape(wt['w_out'].shape)
    do_mla, do_gla = dcat[:, :MLA_HEADS * MLA_V], dcat[:, MLA_HEADS * MLA_V:]

    dzr, dgn = _swiglu_bwd("gla_out_dgate", do_gla, zs['zr'], o_gla_n, F32)
    do_gla_raw, gs['gla_out_norm'] = _rms_bwd("gla_out_drms", o_gla_raw, sm['gla_out_norm'], dgn)
    dgq, dgk, dgv, dlog_a = _gla_bwd("gla_dscan", zs['gq'], zs['gk'], zs['gv'], log_a, states, do_gla_raw, GLA_HEADS)
    dpre, gs['gla_b_gate'] = _gate_bwd("gla_dgate", pre, sm['gla_b_gate'], dlog_a)
    dw_g2 = _mm2("gla_dwgate", zs['zg'], dpre, TN, BF16, tk=512)
    gw['gla_w_gate2'] = _stack_cols(dw_g2[:GLA_GATE_RANK])
    dzg = _mm2("gla_dzg", dpre, w_g2, NT, F32)

    dqf, dkf, dv_mla = _attn_bwd("mla_dattn", qf, kf, v_mla, do_mla, MLA_QK ** -0.5, True)
    dq_raw, gs['mla_q_norm'] = _rms_bwd("mla_q_drms", q_raw2, sm['mla_q_norm'], dqf.reshape(MLA_HEADS * t, MLA_QK), rope=rope)
    dk_raw, gs['mla_k_norm'] = _rms_bwd("mla_k_drms", k_raw2, sm['mla_k_norm'], dkf.reshape(MLA_HEADS * t, MLA_QK), rope=rope)
    dq_raw = dq_raw.reshape(MLA_HEADS, t, MLA_QK)
    dk_raw = dk_raw.reshape(MLA_HEADS, t, MLA_QK)
    dkv = jnp.concatenate([dk_raw[:, :, :MLA_NOPE], dv_mla], axis=2)
    dzkr = jnp.sum(dk_raw[:, :, MLA_NOPE:], axis=0)
    gw['w_q_up'] = _mm_stack_tn_right("mla_dwq", dq_raw, cq, BF16)
    gw['w_kv_up'] = _mm_stack_tn_left("mla_dwkv", ckv, dkv, BF16)
    dcq = _mm_stack_sum("mla_dcq", dq_raw, wt['w_q_up'], F32)
    dckv = _mm_stack_nt_sum("mla_dckv", dkv, wt['w_kv_up'], F32)
    dzq, gs['q_a_norm'] = _rms_bwd("mla_q_a_drms", zs['zq'], sm['q_a_norm'], dcq)
    dzkv, gs['kv_a_norm'] = _rms_bwd("mla_kv_a_drms", zs['zkv'], sm['kv_a_norm'], dckv)

    dzs = {'zq': dzq, 'zkv': dzkv, 'gq': dgq, 'gk': dgk, 'gv': dgv, 'zr': dzr,
           'zkr': jnp.pad(dzkr, ((0, 0), (0, LANES - MLA_ROPE))), 'zg': dzg}
    dz = jnp.concatenate([dzs[name].astype(BF16) for name, _ in IN_PAD], axis=1)
    dw_in = _mm2("mix_dwin", h2, dz, TN, BF16, tm=1024, tn=2048, tk=512)
    dw_in_ref = jnp.concatenate([dw_in[:, PAD_OFF[name]:PAD_OFF[name] + n] for name, n in IN_REF], axis=1)
    gw['w_in'] = _stack_cols(dw_in_ref)
    comm.grads({n: gw[n] for n in MIXER})
    dh2 = _mm2("mix_dh", dz, w_in, NT, F32)
    dx1, gs['mix_norm'] = _rms_bwd("mix_drms", x1, sm['mix_norm'], dh2, res=dx2)

    dgu = _ffn_bwd_weights("ffn1", dx1, ffn1_saved, wt['ffn1_w_down'], comm)
    grad_x, gs['ffn1_norm'] = _ffn_bwd_input("ffn1", dx1, ffn1_saved, dgu, sm['ffn1_norm'],
                                             wt['ffn1_w_gate'], wt['ffn1_w_up'])
    return loss_lanes, grad_x, gs


def _pad_lanes(v):
    n = v.shape[1]
    return jnp.pad(v, ((0, 0), (0, -n % LANES)))


def _pack_small(vals):
    return jnp.concatenate([_pad_lanes(vals[n]) for n in SMALL], axis=1)


def _unpack_small(packed, like):
    out, off = {}, 0
    for n in SMALL:
        size = like[n].shape[1]
        out[n] = packed[:, off:off + size]
        off += size + (-size % LANES)
    return out


def kernel(x, mem, positions, ffn1_norm, ffn1_w_gate, ffn1_w_up, ffn1_w_down, mix_norm, w_in, q_a_norm, w_q_up, kv_a_norm, w_kv_up, mla_q_norm, mla_k_norm, gla_w_gate2, gla_b_gate, gla_out_norm, w_out, mem_attn_norm, mem_norm, mem_w_q, mem_w_k, mem_w_v, mem_w_o, mem_q_norm, mem_k_norm, ffn2_norm, ffn2_w_gate, ffn2_w_up, ffn2_w_down, loss_target, m_ffn1_norm, m_ffn1_w_gate, m_ffn1_w_up, m_ffn1_w_down, m_mix_norm, m_w_in, m_q_a_norm, m_w_q_up, m_kv_a_norm, m_w_kv_up, m_mla_q_norm, m_mla_k_norm, m_gla_w_gate2, m_gla_b_gate, m_gla_out_norm, m_w_out, m_mem_attn_norm, m_mem_norm, m_mem_w_q, m_mem_w_k, m_mem_w_v, m_mem_w_o, m_mem_q_norm, m_mem_k_norm, m_ffn2_norm, m_ffn2_w_gate, m_ffn2_w_up, m_ffn2_w_down, v_ffn1_norm, v_ffn1_w_gate, v_ffn1_w_up, v_ffn1_w_down, v_mix_norm, v_w_in, v_q_a_norm, v_w_q_up, v_kv_a_norm, v_w_kv_up, v_mla_q_norm, v_mla_k_norm, v_gla_w_gate2, v_gla_b_gate, v_gla_out_norm, v_w_out, v_mem_attn_norm, v_mem_norm, v_mem_w_q, v_mem_w_k, v_mem_w_v, v_mem_w_o, v_mem_q_norm, v_mem_k_norm, v_ffn2_norm, v_ffn2_w_gate, v_ffn2_w_up, v_ffn2_w_down):
    inp = dict(locals())
    x, mem, positions, target = inp['x'][0], inp['mem'][0], inp['positions'][0], inp['loss_target'][0]
    sm = {n: inp[n] for n in SMALL}
    out = {}

    def stored(key):
        name = key[2:] if key[:2] in ('m_', 'v_') else key
        return inp[key][0].T if name in TRANSPOSED else inp[key][0]

    def as_given(name, r):
        return r.T[None] if name in TRANSPOSED else r[None]

    class Comm:
        def __init__(self):
            self.gathers = {0: self.start(0)}
            self.forwards, self.exchanges = {}, []

        def start(self, k):
            return _gather_start(f"gather_start_{k}", [stored(n).astype(BF16) for n in GATHERS[k]])

        def forward(self, k):
            if k not in self.forwards:
                self.forwards[k] = _gather_mid(f"gather_mid_{k}", self.gathers[k])
                self.gathers.update({nxt: self.start(nxt) for nxt in NEXT_GATHERS.get(k, [])})

        def weights(self, k):
            self.forward(k)
            if k in EARLY_FORWARD:
                self.forward(EARLY_FORWARD[k])
            return dict(zip(GATHERS[k], _gather_end(f"gather_end_{k}", self.forwards[k])))

        def grads(self, stacks):
            names = list(stacks)
            self.exchanges.append((names, _exchange_start("exchange_start_" + names[0], [stacks[n] for n in names])))

        def update(self, count):
            todo, self.exchanges = self.exchanges[:count], self.exchanges[count:]
            for names, state in todo:
                for n, p in zip(names, _exchange_wait("exchange_wait_" + names[0], state)):
                    res = _adamw("adamw_" + n, stored(n), stored('m_' + n), stored('v_' + n), p)
                    for kind, r in zip(('grad_', 'delta_', 'new_m_', 'new_v_'), res):
                        out[kind + n] = as_given(n, r)

    _Chain.last = None
    comm = Comm()
    loss_lanes, grad_x, gs = _local_step(x, mem, positions, sm, comm, target)
    out['loss'] = lax.psum(jnp.sum(loss_lanes), ("x", "y", "c"))
    out['grad_x'] = grad_x[None]

    comm.update(5)
    small_parts = _all_gather("gather_small", [_pack_small(gs)])[0]
    res = _adamw("adamw_small", _pack_small(sm), _pack_small({n: inp['m_' + n] for n in SMALL}),
                 _pack_small({n: inp['v_' + n] for n in SMALL}), small_parts)
    for kind, r in zip(('grad_', 'delta_', 'new_m_', 'new_v_'), res):
        for n, val in _unpack_small(r, sm).items():
            out[kind + n] = val

    comm.update(3)

    names = ['loss', 'grad_x'] + [k + n for k in ('grad_', 'delta_', 'new_m_', 'new_v_') for n in WEIGHTS]
    return tuple(out[n] for n in names)
```

```python
import math

import jax
import jax.numpy as jnp
from jax import lax
from jax.experimental import pallas as pl
from jax.experimental.pallas import tpu as pltpu

F32 = jnp.float32
BF16 = jnp.bfloat16

N_DEV = 8
EPS = 1e-6
CHUNK = 64
MLA_HEADS, MLA_NOPE, MLA_ROPE, MLA_V = 8, 128, 64, 128
MLA_QK = MLA_NOPE + MLA_ROPE
MLA_Q_RANK, MLA_KV_RANK = 512, 256
ROPE_THETA = 10000.0
GLA_HEADS, GLA_DK, GLA_DV, GLA_GATE_RANK = 4, 128, 256, 16
GLA_TAU = 16.0
MEM_HEADS, MEM_HEAD_DIM = 4, 128
ADAM_LR, ADAM_B1, ADAM_B2, ADAM_EPS, ADAM_WD, ADAM_STEP = 0.001, 0.9, 0.999, 1e-08, 0.01, 10

V7X_VMEM_BYTES = 64 * 1024 * 1024
LANES = 128

NN = (((1,), (0,)), ((), ()))
NT = (((1,), (1,)), ((), ()))
TN = (((0,), (0,)), ((), ()))

WEIGHTS = ['ffn1_norm', 'ffn1_w_gate', 'ffn1_w_up', 'ffn1_w_down', 'mix_norm', 'w_in', 'q_a_norm', 'w_q_up',
           'kv_a_norm', 'w_kv_up', 'mla_q_norm', 'mla_k_norm', 'gla_w_gate2', 'gla_b_gate', 'gla_out_norm', 'w_out',
           'mem_attn_norm', 'mem_norm', 'mem_w_q', 'mem_w_k', 'mem_w_v', 'mem_w_o', 'mem_q_norm', 'mem_k_norm',
           'ffn2_norm', 'ffn2_w_gate', 'ffn2_w_up', 'ffn2_w_down']
SMALL = ['ffn1_norm', 'mix_norm', 'q_a_norm', 'kv_a_norm', 'mla_q_norm', 'mla_k_norm', 'gla_b_gate', 'gla_out_norm',
         'mem_attn_norm', 'mem_norm', 'mem_q_norm', 'mem_k_norm', 'ffn2_norm']
MIXER = ['w_in', 'w_q_up', 'w_kv_up', 'gla_w_gate2', 'w_out']
MEMORY = ['mem_w_q', 'mem_w_k', 'mem_w_v', 'mem_w_o']
TRANSPOSED = ['ffn1_w_gate', 'ffn1_w_up', 'ffn2_w_gate', 'ffn2_w_up', 'w_q_up']
TWO_STAGE = ['ffn1_w_down', 'ffn1_w_gate', 'ffn1_w_up']
GATHERS = [['ffn1_w_gate'], ['ffn1_w_up'], ['ffn1_w_down'], MIXER, MEMORY, ['ffn2_w_gate', 'ffn2_w_up', 'ffn2_w_down']]
NEXT_GATHERS = {0: [1], 1: [2], 2: [3], 3: [4, 5]}
EARLY_FORWARD = {4: 5}

IN_REF = [('zq', 512), ('zkv', 256), ('zkr', 64), ('gq', 512), ('gk', 512), ('gv', 1024), ('zg', 16), ('zr', 1024)]
IN_PAD = [('zq', 512), ('zkv', 256), ('gq', 512), ('gk', 512), ('gv', 1024), ('zr', 1024), ('zkr', 128), ('zg', 128)]
IN_WIDTH = sum(n for _, n in IN_REF)
IN_PAD_WIDTH = sum(n for _, n in IN_PAD)


def _offsets(layout):
    out, off = {}, 0
    for name, n in layout:
        out[name] = off
        off += n
    return out


REF_OFF, PAD_OFF = _offsets(IN_REF), _offsets(IN_PAD)
REF_SIZE = dict(IN_REF)


def _tile(n, pref):
    return pref if n % pref == 0 else n


def _block_bytes(blk, dtype):
    dims = [d for d in blk if d is not None]
    if len(dims) >= 1:
        dims[-1] = -(-dims[-1] // LANES) * LANES
    return math.prod(dims) * jnp.dtype(dtype).itemsize


def _vmem_limit(pipelined_bytes, resident_bytes=0):
    need = 2 * pipelined_bytes + resident_bytes + (8 << 20)
    return int(min(max(need, 32 << 20), V7X_VMEM_BYTES - (6 << 20)))


class _Chain:
    last = None


def _chained(body, *, in_specs, link=0, **kwargs):
    def call(*operands):
        dep = _Chain.last
        if dep is not None and any(o is dep for o in operands):
            dep = None
        if dep is None:
            res = pl.pallas_call(body, in_specs=in_specs, **kwargs)(*operands)
        else:
            n = len(operands)

            def chained_body(*refs):
                body(*refs[:n], *refs[n + 1:])

            res = pl.pallas_call(chained_body, in_specs=list(in_specs) + [pl.BlockSpec(memory_space=pl.ANY)],
                                 **kwargs)(*operands, dep)
        _Chain.last = res[link] if isinstance(res, (list, tuple)) else res
        return res

    return call


def _rowwise_params(*semantics):
    return pltpu.CompilerParams(dimension_semantics=semantics, vmem_limit_bytes=48 << 20)


def _mm(name, a, b, grid, a_spec, b_spec, o_spec, out_shape, out_dtype, dims, scale=None, res=None, fuse=None):
    nk = grid[2]
    (a_blk, a_map), (b_blk, b_map), (o_blk, o_map) = a_spec, b_spec, o_spec
    acc_shape = tuple(d for d in o_blk if d is not None)
    extras = [res] if res is not None else (list(fuse[1]) if fuse else [])
    n_out = 2 if fuse else 1

    def body(*refs):
        a_ref, b_ref = refs[:2]
        extra_refs = refs[2:2 + len(extras)]
        out_refs = refs[2 + len(extras):2 + len(extras) + n_out]
        rest = refs[2 + len(extras) + n_out:]
        r_ref = extra_refs[0] if res is not None else None
        o_ref = out_refs[0]

        def product():
            return lax.dot_general(a_ref[...].astype(BF16), b_ref[...].astype(BF16), dims, preferred_element_type=F32)

        def finish(r):
            if scale is not None:
                r = r * scale
            if r_ref is not None:
                r = r + r_ref[...].astype(F32)
            if fuse:
                for ref, val in zip(out_refs, fuse[0](r, *[e[...].astype(F32) for e in extra_refs])):
                    ref[...] = val.astype(ref.dtype)
            else:
                o_ref[...] = r.astype(o_ref.dtype)

        if nk == 1:
            finish(product())
        else:
            acc = rest[0]
            k = pl.program_id(2)

            @pl.when(k == 0)
            def _():
                acc[...] = product()

            @pl.when(k > 0)
            def _():
                acc[...] += product()

            @pl.when(k == nk - 1)
            def _():
                finish(acc[...])

    in_specs = [pl.BlockSpec(a_blk, a_map), pl.BlockSpec(b_blk, b_map)]
    operands = [a, b]
    piped = _block_bytes(a_blk, a.dtype) + _block_bytes(b_blk, b.dtype) + _block_bytes(o_blk, out_dtype)
    for extra in extras:
        in_specs.append(pl.BlockSpec(o_blk, o_map))
        operands.append(extra)
        piped += _block_bytes(o_blk, extra.dtype)
    piped += (n_out - 1) * _block_bytes(o_blk, out_dtype)
    scratch = [pltpu.VMEM(acc_shape, F32)] if nk > 1 else []
    out_spec, out_sds = pl.BlockSpec(o_blk, o_map), jax.ShapeDtypeStruct(out_shape, out_dtype)
    return _chained(
        body, name=name, grid=grid, in_specs=in_specs, out_specs=(out_spec,) * n_out if fuse else out_spec,
        out_shape=(out_sds,) * n_out if fuse else out_sds, scratch_shapes=scratch,
        compiler_params=pltpu.CompilerParams(
            dimension_semantics=("parallel", "parallel", "arbitrary"),
            vmem_limit_bytes=_vmem_limit(piped, 3 * _block_bytes(acc_shape, F32))),
    )(*operands)


def _mm2(name, a, b, dims, out_dtype, tm=512, tn=1024, tk=2048, scale=None, res=None):
    if dims is NN:
        (m, kk), n = a.shape, b.shape[1]
    elif dims is NT:
        (m, kk), n = a.shape, b.shape[0]
    else:
        (kk, m), n = a.shape, b.shape[1]
    tm, tn, tk = _tile(m, tm), _tile(n, tn), _tile(kk, tk)
    a_spec = ((tk, tm), lambda i, j, k: (k, i)) if dims is TN else ((tm, tk), lambda i, j, k: (i, k))
    b_spec = ((tn, tk), lambda i, j, k: (j, k)) if dims is NT else ((tk, tn), lambda i, j, k: (k, j))
    return _mm(name, a, b, (m // tm, n // tn, kk // tk), a_spec, b_spec, ((tm, tn), lambda i, j, k: (i, j)),
               (m, n), out_dtype, dims, scale, res)


def _mm_stack_out(name, a, b, out_dtype, tm=512, tk=2048):
    (m, kk), (nj, _, n) = a.shape, b.shape
    tm, tk = _tile(m, tm), _tile(kk, tk)
    return _mm(name, a, b, (nj, m // tm, kk // tk), ((tm, tk), lambda j, i, k: (i, k)),
               ((None, tk, n), lambda j, i, k: (j, k, 0)), ((None, tm, n), lambda j, i, k: (j, i, 0)),
               (nj, m, n), out_dtype, NN)


def _mm_stack_nt_out(name, a, b, out_dtype, scale=None, fuse=None, tm=512, tk=2048):
    (m, kk), (nj, n, _) = a.shape, b.shape
    tm, tk = _tile(m, tm), _tile(kk, tk)
    return _mm(name, a, b, (nj, m // tm, kk // tk), ((tm, tk), lambda j, i, k: (i, k)),
               ((None, n, tk), lambda j, i, k: (j, 0, k)), ((None, tm, n), lambda j, i, k: (j, i, 0)),
               (nj, m, n), out_dtype, NT, scale, fuse=fuse)


def _mm_stack_sum(name, a, b, out_dtype, scale=None, res=None, tm=512, tn=1024):
    (nj, m, f), n = a.shape, b.shape[2]
    tm, tn = _tile(m, tm), _tile(n, tn)
    return _mm(name, a, b, (m // tm, n // tn, nj), ((None, tm, f), lambda i, j, k: (k, i, 0)),
               ((None, f, tn), lambda i, j, k: (k, 0, j)), ((tm, tn), lambda i, j, k: (i, j)),
               (m, n), out_dtype, NN, scale, res)


def _mm_stack_nt_sum(name, a, b, out_dtype, res=None, tm=512, tn=1024):
    (nj, m, f), n = a.shape, b.shape[1]
    tm, tn = _tile(m, tm), _tile(n, tn)
    return _mm(name, a, b, (m // tm, n // tn, nj), ((None, tm, f), lambda i, j, k: (k, i, 0)),
               ((None, tn, f), lambda i, j, k: (k, j, 0)), ((tm, tn), lambda i, j, k: (i, j)),
               (m, n), out_dtype, NT, None, res)


def _mm_stack_tn_left(name, a, b, out_dtype, tm=1024, tk=512):
    (m, kp), (nj, _, n) = a.shape, b.shape
    tm, tk = _tile(kp, tm), _tile(m, tk)
    return _mm(name, a, b, (nj, kp // tm, m // tk), ((tk, tm), lambda j, i, k: (k, i)),
               ((None, tk, n), lambda j, i, k: (j, k, 0)), ((None, tm, n), lambda j, i, k: (j, i, 0)),
               (nj, kp, n), out_dtype, TN)


def _mm_stack_tn_right(name, a, b, out_dtype, scale=None, tn=2048, tk=512):
    (nj, m, f), n = a.shape, b.shape[1]
    tn, tk = _tile(n, tn), _tile(m, tk)
    return _mm(name, a, b, (nj, n // tn, m // tk), ((None, tk, f), lambda j, i, k: (j, k, 0)),
               ((tk, tn), lambda j, i, k: (k, i)), ((None, f, tn), lambda j, i, k: (j, 0, i)),
               (nj, f, n), out_dtype, TN, scale)


def _split3(x):
    hi = x.astype(BF16)
    r1 = x - hi.astype(F32)
    mid = r1.astype(BF16)
    lo = (r1 - mid.astype(F32)).astype(BF16)
    return hi, mid, lo


def _swap_halves(x, swap):
    return sum(jnp.dot(part, swap, preferred_element_type=F32) for part in _split3(x))


def _rope_specs(rope, tm):
    cos_t, _, swap = rope
    nt = cos_t.shape[0] // tm
    tab = pl.BlockSpec((tm, cos_t.shape[1]), lambda i, c: (i % nt, 0))
    return [tab, tab, pl.BlockSpec(swap.shape, lambda i, c: (0, 0))]


def _rms_fwd(name, x, g, out_dtype, rope=None, tm=256):
    rows, cols = x.shape
    d = g.shape[1]
    tm = _tile(rows if rope is None else rope[0].shape[0], tm)

    def body(x_ref, g_ref, *refs):
        xf = x_ref[...].astype(F32)
        r = lax.rsqrt(jnp.mean(xf * xf, axis=-1, keepdims=True) + EPS)
        y = xf * r * g_ref[...]
        if rope is not None:
            c_ref, s_ref, p_ref = refs[:3]
            y = y * c_ref[...] + _swap_halves(y, p_ref[...]) * s_ref[...]
        refs[-1][...] = y.astype(refs[-1].dtype)

    return _chained(
        body, name=name, grid=(rows // tm, cols // d),
        in_specs=[pl.BlockSpec((tm, d), lambda i, c: (i, c)), pl.BlockSpec((1, d), lambda i, c: (0, 0))]
        + (_rope_specs(rope, tm) if rope is not None else []),
        out_specs=pl.BlockSpec((tm, d), lambda i, c: (i, c)),
        out_shape=jax.ShapeDtypeStruct((rows, cols), out_dtype),
        compiler_params=_rowwise_params("parallel", "parallel"),
    )(x, g, *(rope or ()))


def _rms_bwd(name, x, g, dy, res=None, rope=None, tm=256):
    rows, cols = x.shape
    d = g.shape[1]
    tm = _tile(rows if rope is None else rope[0].shape[0], tm)

    def body(*refs):
        x_ref, g_ref, dy_ref = refs[:3]
        dx_ref, dg_ref = refs[-2:]
        r_ref = refs[3] if res is not None else None
        xf = x_ref[...].astype(F32)
        r = lax.rsqrt(jnp.mean(xf * xf, axis=-1, keepdims=True) + EPS)
        xhat = xf * r
        dyf = dy_ref[...].astype(F32)
        if rope is not None:
            c_ref, s_ref, p_ref = refs[-5:-2]
            dyf = dyf * c_ref[...] + _swap_halves(dyf * s_ref[...], p_ref[...])

        @pl.when((pl.program_id(0) == 0) & (pl.program_id(1) == 0))
        def _():
            dg_ref[...] = jnp.zeros_like(dg_ref)

        dg_ref[...] += jnp.sum(dyf * xhat, axis=0, keepdims=True)
        dxh = dyf * g_ref[...]
        dx = r * (dxh - xhat * jnp.mean(dxh * xhat, axis=-1, keepdims=True))
        if r_ref is not None:
            dx = dx + r_ref[...].astype(F32)
        dx_ref[...] = dx

    blk = pl.BlockSpec((tm, d), lambda i, c: (i, c))
    in_specs = [blk, pl.BlockSpec((1, d), lambda i, c: (0, 0)), blk]
    operands = [x, g, dy]
    if res is not None:
        in_specs.append(blk)
        operands.append(res)
    if rope is not None:
        in_specs += _rope_specs(rope, tm)
        operands += list(rope)
    return _chained(
        body, name=name, grid=(rows // tm, cols // d), in_specs=in_specs,
        out_specs=(blk, pl.BlockSpec((1, d), lambda i, c: (0, 0))),
        out_shape=(jax.ShapeDtypeStruct((rows, cols), F32), jax.ShapeDtypeStruct((1, d), F32)),
        compiler_params=_rowwise_params("arbitrary", "arbitrary"),
    )(*operands)


def _swiglu_fwd(name, g, u, out_dtype, tm=256):
    rows, cols = g.shape
    tm = _tile(rows, tm)

    def body(g_ref, u_ref, o_ref):
        gf = g_ref[...].astype(F32)
        o_ref[...] = (gf * jax.nn.sigmoid(gf) * u_ref[...].astype(F32)).astype(o_ref.dtype)

    blk = pl.BlockSpec((tm, cols), lambda i: (i, 0))
    return _chained(
        body, name=name, grid=(rows // tm,), in_specs=[blk, blk], out_specs=blk,
        out_shape=jax.ShapeDtypeStruct((rows, cols), out_dtype),
        compiler_params=_rowwise_params("parallel"),
    )(g, u)


def _swiglu_bwd(name, da, g, u, out_dtype, tm=256):
    rows, cols = g.shape
    tm = _tile(rows, tm)

    def body(da_ref, g_ref, u_ref, dg_ref, du_ref):
        gf = g_ref[...].astype(F32)
        daf = da_ref[...].astype(F32)
        sig = jax.nn.sigmoid(gf)
        du_ref[...] = (daf * gf * sig).astype(du_ref.dtype)
        dg_ref[...] = (daf * u_ref[...].astype(F32) * sig * (1.0 + gf * (1.0 - sig))).astype(dg_ref.dtype)

    blk = pl.BlockSpec((tm, cols), lambda i: (i, 0))
    sds = jax.ShapeDtypeStruct((rows, cols), out_dtype)
    return _chained(
        body, name=name, grid=(rows // tm,), in_specs=[blk, blk, blk], out_specs=(blk, blk), out_shape=(sds, sds),
        compiler_params=_rowwise_params("parallel"),
    )(da, g, u)


def _gate_fwd(name, pre, bias, tm=256):
    rows, cols = pre.shape
    tm = _tile(rows, tm)

    def body(p_ref, b_ref, o_ref):
        z = p_ref[...] + b_ref[...]
        o_ref[...] = (jnp.minimum(z, 0.0) - jnp.log(1.0 + jnp.exp(-jnp.abs(z)))) * (1.0 / GLA_TAU)

    blk = pl.BlockSpec((tm, cols), lambda i: (i, 0))
    return _chained(
        body, name=name, grid=(rows // tm,), in_specs=[blk, pl.BlockSpec((1, cols), lambda i: (0, 0))], out_specs=blk,
        out_shape=jax.ShapeDtypeStruct((rows, cols), F32),
        compiler_params=_rowwise_params("parallel"),
    )(pre, bias)


def _gate_bwd(name, pre, bias, dla, tm=256):
    rows, cols = pre.shape
    tm = _tile(rows, tm)

    def body(p_ref, b_ref, d_ref, dp_ref, db_ref):
        z = p_ref[...] + b_ref[...]
        dp = d_ref[...] * (1.0 / GLA_TAU) / (1.0 + jnp.exp(z))
        dp_ref[...] = dp

        @pl.when(pl.program_id(0) == 0)
        def _():
            db_ref[...] = jnp.zeros_like(db_ref)

        db_ref[...] += jnp.sum(dp, axis=0, keepdims=True)

    blk = pl.BlockSpec((tm, cols), lambda i: (i, 0))
    row = pl.BlockSpec((1, cols), lambda i: (0, 0))
    return _chained(
        body, name=name, grid=(rows // tm,), in_specs=[blk, row, blk], out_specs=(blk, row),
        out_shape=(jax.ShapeDtypeStruct((rows, cols), F32), jax.ShapeDtypeStruct((1, cols), F32)),
        compiler_params=_rowwise_params("arbitrary"),
    )(pre, bias, dla)


def _loss(name, y, target, tm=256):
    rows, d = y.shape
    tm = _tile(rows, tm)

    def body(y_ref, t_ref, dy_ref, l_ref):
        err = y_ref[...] - t_ref[...]
        dy_ref[...] = err * (1.0 / d)

        @pl.when(pl.program_id(0) == 0)
        def _():
            l_ref[...] = jnp.zeros_like(l_ref)

        sq = (err * err).reshape(tm // 8, 8, d)
        l_ref[...] += jnp.sum(sq, axis=0) * (0.5 / d)

    blk = pl.BlockSpec((tm, d), lambda i: (i, 0))
    return _chained(
        body, name=name, grid=(rows // tm,), in_specs=[blk, blk],
        out_specs=(blk, pl.BlockSpec((8, d), lambda i: (0, 0))),
        out_shape=(jax.ShapeDtypeStruct((rows, d), F32), jax.ShapeDtypeStruct((8, d), F32)),
        compiler_params=_rowwise_params("arbitrary"),
    )(y, target)


def _scores(q, k, scale, causal, q0):
    s = lax.dot_general(q, k, NT, preferred_element_type=F32) * scale
    if causal:
        qc = (q0 + lax.broadcasted_iota(jnp.int32, s.shape, 0)) // CHUNK
        kc = lax.broadcasted_iota(jnp.int32, s.shape, 1) // CHUNK
        s = jnp.where(kc <= qc, s, -1e30)
    e = jnp.exp(s - jnp.max(s, axis=-1, keepdims=True))
    return e, jnp.sum(e, axis=-1, keepdims=True)


def _attn_fwd(name, q, k, v, scale, causal, tq=256):
    nh, t, dk = q.shape
    tk, dv = k.shape[1], v.shape[2]
    tq = _tile(t, tq)

    def body(q_ref, k_ref, v_ref, o_ref):
        e, l = _scores(q_ref[...], k_ref[...], scale, causal, pl.program_id(1) * tq)
        o = jnp.dot(e.astype(BF16), v_ref[...], preferred_element_type=F32)
        o_ref[...] = (o / l).astype(o_ref.dtype)

    return _chained(
        body, name=name, grid=(nh, t // tq),
        in_specs=[pl.BlockSpec((None, tq, dk), lambda h, i: (h, i, 0)), pl.BlockSpec((None, tk, dk), lambda h, i: (h, 0, 0)),
                  pl.BlockSpec((None, tk, dv), lambda h, i: (h, 0, 0))],
        out_specs=pl.BlockSpec((tq, dv), lambda h, i: (i, h)),
        out_shape=jax.ShapeDtypeStruct((t, nh * dv), BF16),
        compiler_params=pltpu.CompilerParams(dimension_semantics=("parallel", "parallel"),
                                             vmem_limit_bytes=_vmem_limit(0, 6 * tq * tk * 4)),
    )(q, k, v)


def _attn_bwd(name, q, k, v, do, scale, causal, tq=256):
    nh, t, dk = q.shape
    tk, dv = k.shape[1], v.shape[2]
    tq = _tile(t, tq)

    def body(q_ref, k_ref, v_ref, do_ref, dq_ref, dk_ref, dv_ref):
        qb, kb = q_ref[...], k_ref[...]
        e, l = _scores(qb, kb, scale, causal, pl.program_id(1) * tq)
        p = e / l
        dob = do_ref[...].astype(BF16)
        dp = lax.dot_general(dob, v_ref[...], NT, preferred_element_type=F32)
        ds = (p * (dp - jnp.sum(p * dp, axis=-1, keepdims=True)) * scale).astype(BF16)
        dq_ref[...] = jnp.dot(ds, kb, preferred_element_type=F32)

        @pl.when(pl.program_id(1) == 0)
        def _():
            dk_ref[...] = jnp.zeros_like(dk_ref)
            dv_ref[...] = jnp.zeros_like(dv_ref)

        dk_ref[...] += lax.dot_general(ds, qb, TN, preferred_element_type=F32)
        dv_ref[...] += lax.dot_general(p.astype(BF16), dob, TN, preferred_element_type=F32)

    return _chained(
        body, name=name, grid=(nh, t // tq),
        in_specs=[pl.BlockSpec((None, tq, dk), lambda h, i: (h, i, 0)), pl.BlockSpec((None, tk, dk), lambda h, i: (h, 0, 0)),
                  pl.BlockSpec((None, tk, dv), lambda h, i: (h, 0, 0)), pl.BlockSpec((tq, dv), lambda h, i: (i, h))],
        out_specs=(pl.BlockSpec((None, tq, dk), lambda h, i: (h, i, 0)), pl.BlockSpec((None, tk, dk), lambda h, i: (h, 0, 0)),
                   pl.BlockSpec((None, tk, dv), lambda h, i: (h, 0, 0))),
        out_shape=(jax.ShapeDtypeStruct((nh, t, dk), F32), jax.ShapeDtypeStruct((nh, tk, dk), F32),
                   jax.ShapeDtypeStruct((nh, tk, dv), F32)),
        compiler_params=pltpu.CompilerParams(dimension_semantics=("parallel", "arbitrary"),
                                             vmem_limit_bytes=_vmem_limit(0, 10 * tq * tk * 4)),
    )(q, k, v, do)


def _tri(lower):
    r = lax.broadcasted_iota(jnp.int32, (CHUNK, CHUNK), 0)
    c = lax.broadcasted_iota(jnp.int32, (CHUNK, CHUNK), 1)
    return jnp.where((c <= r) if lower else (c >= r), 1.0, 0.0).astype(BF16)


def _tri_dot(tri, x):
    return sum(jnp.dot(tri, part, preferred_element_type=F32) for part in _split3(x))


def _gla_fwd(name, q, k, v, la, nh):
    t = q.shape[0]
    dk, dv = q.shape[1] // nh, v.shape[1] // nh
    nc = t // CHUNK

    def body(q_ref, k_ref, v_ref, g_ref, o_ref, st_ref, state):
        @pl.when(pl.program_id(1) == 0)
        def _():
            state[...] = jnp.zeros_like(state)

        g = g_ref[...]
        b = _tri_dot(_tri(True), g)
        b_end = jnp.sum(g, axis=0, keepdims=True)
        k_dec = (k_ref[...] * jnp.exp(b_end - b)).astype(BF16)
        u_t = lax.dot_general(v_ref[...].astype(BF16), k_dec, TN, preferred_element_type=F32)
        new = state[...] * jnp.exp(b_end) + u_t
        state[...] = new
        st_ref[...] = new
        qc = (q_ref[...] * (dk ** -0.5)).astype(BF16)
        o_ref[...] = lax.dot_general(qc, new.astype(BF16), NT, preferred_element_type=F32)

    kblk = pl.BlockSpec((CHUNK, dk), lambda h, n: (n, h))
    vblk = pl.BlockSpec((CHUNK, dv), lambda h, n: (n, h))
    return _chained(
        body, name=name, grid=(nh, nc), in_specs=[kblk, kblk, vblk, kblk],
        out_specs=(vblk, pl.BlockSpec((None, None, dv, dk), lambda h, n: (h, n, 0, 0))),
        out_shape=(jax.ShapeDtypeStruct((t, nh * dv), F32), jax.ShapeDtypeStruct((nh, nc, dv, dk), F32)),
        scratch_shapes=[pltpu.VMEM((dv, dk), F32)],
        compiler_params=_rowwise_params("parallel", "arbitrary"),
    )(q, k, v, la)


def _gla_bwd(name, q, k, v, la, states, do, nh):
    t = q.shape[0]
    dk, dv = q.shape[1] // nh, v.shape[1] // nh
    nc = t // CHUNK
    scale = dk ** -0.5

    def body(q_ref, k_ref, v_ref, g_ref, do_ref, st_ref, sp_ref, dq_ref, dk_ref, dv_ref, dg_ref, carry):
        i = pl.program_id(1)

        @pl.when(i == 0)
        def _():
            carry[...] = jnp.zeros_like(carry)

        g = g_ref[...]
        b = _tri_dot(_tri(True), g)
        b_end = jnp.sum(g, axis=0, keepdims=True)
        w = jnp.exp(b_end - b)
        decay = jnp.exp(b_end)
        k_dec = k_ref[...] * w
        qc = (q_ref[...] * scale).astype(BF16)
        dob = do_ref[...].astype(BF16)
        dq_ref[...] = jnp.dot(dob, st_ref[...].astype(BF16), preferred_element_type=F32) * scale
        g_t = carry[...] + lax.dot_general(dob, qc, TN, preferred_element_type=F32)
        g_tb = g_t.astype(BF16)
        dk_dec = jnp.dot(v_ref[...].astype(BF16), g_tb, preferred_element_type=F32)
        dv_ref[...] = lax.dot_general(k_dec.astype(BF16), g_tb, NT, preferred_element_type=F32)
        prev = jnp.where(i < nc - 1, sp_ref[...], 0.0)
        ddecay = jnp.sum(g_t * prev, axis=0, keepdims=True)
        dk_ref[...] = dk_dec * w
        e = dk_dec * k_dec
        db_end = jnp.sum(e, axis=0, keepdims=True) + ddecay * decay
        dg_ref[...] = _tri_dot(_tri(False), -e) + db_end
        carry[...] = g_t * decay

    kblk = pl.BlockSpec((CHUNK, dk), lambda h, i: (nc - 1 - i, h))
    vblk = pl.BlockSpec((CHUNK, dv), lambda h, i: (nc - 1 - i, h))
    ksds = jax.ShapeDtypeStruct((t, nh * dk), F32)
    return _chained(
        body, name=name, grid=(nh, nc),
        in_specs=[kblk, kblk, vblk, kblk, vblk,
                  pl.BlockSpec((None, None, dv, dk), lambda h, i: (h, nc - 1 - i, 0, 0)),
                  pl.BlockSpec((None, None, dv, dk), lambda h, i: (h, jnp.maximum(nc - 2 - i, 0), 0, 0))],
        out_specs=(kblk, kblk, vblk, kblk),
        out_shape=(ksds, ksds, jax.ShapeDtypeStruct((t, nh * dv), F32), ksds),
        scratch_shapes=[pltpu.VMEM((dv, dk), F32)],
        compiler_params=_rowwise_params("parallel", "arbitrary"),
    )(q, k, v, la, do, states, states)


def _adamw(name, w, m, v, parts):
    rows, cols = w.shape
    tm = _tile(rows, 1 << int(math.log2(max(8, (1 << 17) // (-(-cols // LANES) * LANES)))))

    def body(w_ref, m_ref, v_ref, p_ref, g_ref, d_ref, nm_ref, nv_ref):
        g = p_ref[0].astype(F32)
        for s in range(1, parts.shape[0]):
            g = g + p_ref[s].astype(F32)
        m_new = ADAM_B1 * m_ref[...] + (1.0 - ADAM_B1) * g
        v_new = ADAM_B2 * v_ref[...] + (1.0 - ADAM_B2) * jnp.square(g)
        m_hat = m_new / (1.0 - ADAM_B1 ** ADAM_STEP)
        v_hat = v_new / (1.0 - ADAM_B2 ** ADAM_STEP)
        g_ref[...] = g
        d_ref[...] = -ADAM_LR * (m_hat / (jnp.sqrt(v_hat) + ADAM_EPS) + ADAM_WD * w_ref[...])
        nm_ref[...] = m_new
        nv_ref[...] = v_new

    blk = pl.BlockSpec((tm, cols), lambda i: (i, 0))
    sds = jax.ShapeDtypeStruct((rows, cols), F32)
    return _chained(
        body, name=name, grid=(rows // tm,),
        in_specs=[blk, blk, blk, pl.BlockSpec((parts.shape[0], tm, cols), lambda i: (0, i, 0))],
        out_specs=(blk, blk, blk, blk), out_shape=(sds, sds, sds, sds),
        compiler_params=_rowwise_params("parallel"),
    )(w, m, v, parts)


HBM = pl.BlockSpec(memory_space=pl.ANY)
MESH = pl.DeviceIdType.MESH


def _all_gather(name, shards):
    n = len(shards)

    def body(*refs):
        ins, outs = refs[:n], refs[n:2 * n]
        send_sems, recv_sems, local_sems = refs[2 * n:]
        x, y, c = lax.axis_index("x"), lax.axis_index("y"), lax.axis_index("c")
        me, sibling = (x, y, c), (x, y, 1 - c)
        chips = [(1 - x, y), (x, 1 - y), (1 - x, 1 - y)]

        def copy(w, k, block, to, src=None):
            dst = outs[w].at[4 * block[0] + 2 * block[1] + block[2]]
            return pltpu.make_async_remote_copy(
                src_ref=dst if src is None else src, dst_ref=dst, send_sem=send_sems.at[7 * w + k],
                recv_sem=recv_sems.at[7 * w + k], device_id=to, device_id_type=MESH)

        mine, first, passed = [], [], []
        for w in range(n):
            cp = pltpu.make_async_copy(ins[w], outs[w].at[4 * x + 2 * y + c], local_sems.at[w])
            cp.start()
            mine.append(cp)
            first.append(copy(w, 0, me, sibling, src=ins[w]))
            first += [copy(w, 1 + j, me, (*chip, c), src=ins[w]) for j, chip in enumerate(chips)]
        for cp in first:
            cp.start()
        for w in range(n):
            for j, chip in enumerate(chips):
                copy(w, 1 + j, (*chip, c), me).wait_recv()
                cp = copy(w, 4 + j, (*chip, c), sibling)
                cp.start()
                passed.append(cp)
        for w in range(n):
            copy(w, 0, sibling, me).wait_recv()
            for j, chip in enumerate(chips):
                copy(w, 4 + j, (*chip, 1 - c), me).wait_recv()
        for cp in first + passed:
            cp.wait_send()
        for cp in mine:
            cp.wait()

    return _chained(
        body, name=name, in_specs=[HBM] * n, out_specs=[HBM] * n,
        out_shape=[jax.ShapeDtypeStruct((N_DEV,) + s.shape, s.dtype) for s in shards],
        scratch_shapes=[pltpu.SemaphoreType.DMA((7 * n,)), pltpu.SemaphoreType.DMA((7 * n,)),
                        pltpu.SemaphoreType.DMA((n,))],
    )(*shards)


HBM_SPEC = pl.BlockSpec(memory_space=pltpu.HBM)
SEM_SPEC = pl.BlockSpec(memory_space=pltpu.SEMAPHORE)
EFFECT = pltpu.SideEffectType.DATAFLOW_SIDE_EFFECTING
TOKEN = jax.ShapeDtypeStruct((8, LANES), F32)


def _in_hbm(a):
    return pltpu.with_memory_space_constraint(a, pltpu.HBM)


def _place():
    x, y, c = lax.axis_index("x"), lax.axis_index("y"), lax.axis_index("c")
    chips = [(1 - x, y), (x, 1 - y), (1 - x, 1 - y)]
    return x, y, c, chips


def _block_of(px, py, pc):
    return 4 * px + 2 * py + pc


def _gather_copies(ins, outs, sems, w):
    send_sems, recv_sems, local_sems = sems
    x, y, c, chips = _place()
    peers = [(x, y, 1 - c)] + [(*chip, c) for chip in chips]

    def copy(k, block, to, src):
        dst = outs[w].at[_block_of(*block)]
        return pltpu.make_async_remote_copy(src_ref=dst if src is None else src, dst_ref=dst, send_sem=send_sems.at[4 * w + k],
                                            recv_sem=recv_sems.at[4 * w + k], device_id=to, device_id_type=MESH)

    local = pltpu.make_async_copy(ins[w], outs[w].at[_block_of(x, y, c)], local_sems.at[w])
    sends = [copy(k, (x, y, c), peer, ins[w]) for k, peer in enumerate(peers)]
    recvs = [copy(k, peer, (x, y, c), None) for k, peer in enumerate(peers)]
    return local, sends, recvs


def _forward_copies(outs, sems, w):
    send_sems, recv_sems = sems
    x, y, c, chips = _place()

    def copy(j, block, to):
        dst = outs[w].at[_block_of(*block)]
        return pltpu.make_async_remote_copy(src_ref=dst, dst_ref=dst, send_sem=send_sems.at[3 * w + j],
                                            recv_sem=recv_sems.at[3 * w + j], device_id=to, device_id_type=MESH)

    sends = [copy(j, (*chip, c), (x, y, 1 - c)) for j, chip in enumerate(chips)]
    recvs = [copy(j, (*chip, 1 - c), (x, y, c)) for j, chip in enumerate(chips)]
    return sends, recvs


def _gather_start(name, shards):
    n = len(shards)

    def body(*refs):
        ins, outs, sems = refs[:n], refs[n:2 * n], refs[2 * n:2 * n + 3]
        for w in range(n):
            local, sends, _ = _gather_copies(ins, outs, sems, w)
            local.start()
            for cp in sends:
                cp.start()
        refs[-1][...] = jnp.zeros_like(refs[-1])

    lands = [lax.empty((N_DEV,) + s.shape, s.dtype) for s in shards]
    res = _chained(
        body, name=name, link=-1, in_specs=[HBM_SPEC] * (2 * n),
        out_specs=[SEM_SPEC] * 3 + [HBM_SPEC] * (2 * n) + [pl.BlockSpec(memory_space=pltpu.VMEM)],
        out_shape=[pltpu.SemaphoreType.DMA((4 * n,)), pltpu.SemaphoreType.DMA((4 * n,)), pltpu.SemaphoreType.DMA((n,))]
        + [pltpu.HBM(s.shape, s.dtype) for s in shards] + [pltpu.HBM(l.shape, l.dtype) for l in lands] + [TOKEN],
        input_output_aliases={i: 3 + i for i in range(2 * n)},
        compiler_params=pltpu.CompilerParams(has_side_effects=EFFECT),
    )(*[_in_hbm(s) for s in shards], *[_in_hbm(l) for l in lands])
    return n, res[:3], res[3:3 + n], res[3 + n:3 + 2 * n]


def _gather_mid(name, state):
    n, sems, shards, lands = state

    def body(*refs):
        ins, outs, sems_in = refs[:n], refs[n:2 * n], refs[2 * n:2 * n + 3]
        sems_out = refs[2 * n + 3:2 * n + 5]
        for w in range(n):
            local, sends, recvs = _gather_copies(ins, outs, sems_in, w)
            local.wait()
            for cp in sends:
                cp.wait_send()
            for cp in recvs:
                cp.wait_recv()
            for cp in _forward_copies(outs, sems_out, w)[0]:
                cp.start()
        refs[-1][...] = jnp.zeros_like(refs[-1])

    res = _chained(
        body, name=name, link=-1, in_specs=[HBM_SPEC] * (2 * n) + [SEM_SPEC] * 3,
        out_specs=[SEM_SPEC] * 2 + [HBM_SPEC] * n + [pl.BlockSpec(memory_space=pltpu.VMEM)],
        out_shape=[pltpu.SemaphoreType.DMA((3 * n,)), pltpu.SemaphoreType.DMA((3 * n,))]
        + [pltpu.HBM(l.shape, l.dtype) for l in lands] + [TOKEN],
        input_output_aliases={n + i: 2 + i for i in range(n)},
        compiler_params=pltpu.CompilerParams(has_side_effects=EFFECT),
    )(*shards, *lands, *sems)
    return n, res[:2], res[2:2 + n]


def _gather_end(name, state):
    n, sems, lands = state

    def body(*refs):
        outs, sems_in = refs[:n], refs[n:n + 2]
        for w in range(n):
            sends, recvs = _forward_copies(outs, sems_in, w)
            for cp in sends:
                cp.wait_send()
            for cp in recvs:
                cp.wait_recv()

    return _chained(
        body, name=name, in_specs=[HBM_SPEC] * n + [SEM_SPEC] * 2, out_specs=[HBM_SPEC] * n,
        out_shape=[pltpu.HBM(l.shape, l.dtype) for l in lands], input_output_aliases={i: i for i in range(n)},
        compiler_params=pltpu.CompilerParams(has_side_effects=EFFECT),
    )(*lands, *sems)


def _exchange_copies(ins, outs, sems, w):
    send_sems, recv_sems, local_sems = sems
    x, y, c, _ = _place()
    mine = _block_of(x, y, c)
    local = pltpu.make_async_copy(ins[w].at[mine], outs[w].at[mine], local_sems.at[w])
    remote = []
    for k in range(1, N_DEV):
        px, py, pc = x ^ (k >> 2), y ^ ((k >> 1) & 1), c ^ (k & 1)
        remote.append(pltpu.make_async_remote_copy(
            src_ref=ins[w].at[_block_of(px, py, pc)], dst_ref=outs[w].at[mine], send_sem=send_sems.at[7 * w + k - 1],
            recv_sem=recv_sems.at[7 * w + k - 1], device_id=(px, py, pc), device_id_type=MESH))
    return local, remote


def _exchange_start(name, stacks):
    n = len(stacks)

    def body(*refs):
        ins, outs, sems = refs[:n], refs[n:2 * n], refs[2 * n:2 * n + 3]
        for w in range(n):
            local, remote = _exchange_copies(ins, outs, sems, w)
            local.start()
            for cp in remote:
                cp.start()
        refs[-1][...] = jnp.zeros_like(refs[-1])

    lands = [lax.empty(s.shape, s.dtype) for s in stacks]
    if any(s is _Chain.last for s in stacks):
        _Chain.last = None
    res = _chained(
        body, name=name, link=-1, in_specs=[HBM_SPEC] * (2 * n),
        out_specs=[SEM_SPEC] * 3 + [HBM_SPEC] * (2 * n) + [pl.BlockSpec(memory_space=pltpu.VMEM)],
        out_shape=[pltpu.SemaphoreType.DMA((7 * n,)), pltpu.SemaphoreType.DMA((7 * n,)), pltpu.SemaphoreType.DMA((n,))]
        + [pltpu.HBM(s.shape, s.dtype) for s in stacks] * 2 + [TOKEN],
        input_output_aliases={i: 3 + i for i in range(2 * n)},
        compiler_params=pltpu.CompilerParams(has_side_effects=EFFECT),
    )(*[_in_hbm(s) for s in stacks], *[_in_hbm(l) for l in lands])
    return n, res[:3], res[3:3 + n], res[3 + n:3 + 2 * n]


def _exchange_wait(name, state):
    n, sems, stacks, lands = state

    def body(*refs):
        ins, outs, sems_in = refs[:n], refs[n:2 * n], refs[2 * n:2 * n + 3]
        for w in range(n):
            local, remote = _exchange_copies(ins, outs, sems_in, w)
            local.wait()
            for cp in remote:
                cp.wait_send()
                cp.wait_recv()

    return _chained(
        body, name=name, in_specs=[HBM_SPEC] * (2 * n) + [SEM_SPEC] * 3, out_specs=[HBM_SPEC] * n,
        out_shape=[pltpu.HBM(l.shape, l.dtype) for l in lands], input_output_aliases={n + i: i for i in range(n)},
        compiler_params=pltpu.CompilerParams(has_side_effects=EFFECT),
    )(*stacks, *lands, *sems)


N_CHIP = N_DEV // 2


def _pair_copies(stack, own, land, sems):
    send_sems, recv_sems, local_sems = sems
    x, y, c, _ = _place()
    local = [pltpu.make_async_copy(stack.at[2 * k + c], own.at[k], local_sems.at[k]) for k in range(N_CHIP)]
    remote = [pltpu.make_async_remote_copy(src_ref=stack.at[2 * k + 1 - c], dst_ref=land.at[k], send_sem=send_sems.at[k],
                                           recv_sem=recv_sems.at[k], device_id=(x, y, 1 - c), device_id_type=MESH)
              for k in range(N_CHIP)]
    return local, remote


def _chip_copies(pairs, land, sems):
    send_sems, recv_sems, local_sems = sems
    x, y, c, _ = _place()
    mine = 2 * x + y
    local = pltpu.make_async_copy(pairs.at[mine], land.at[mine], local_sems.at[0])
    remote = []
    for m in range(1, N_CHIP):
        px, py = x ^ (m >> 1), y ^ (m & 1)
        remote.append(pltpu.make_async_remote_copy(
            src_ref=pairs.at[2 * px + py], dst_ref=land.at[mine], send_sem=send_sems.at[m - 1],
            recv_sem=recv_sems.at[m - 1], device_id=(px, py, c), device_id_type=MESH))
    return local, remote


def _stage_start(name, copies_of, src_arr, n_land, n_sems):
    half = (N_CHIP,) + src_arr.shape[1:]

    def body(*refs):
        local, remote = copies_of(refs[0], *refs[1:1 + n_land], refs[1 + n_land:4 + n_land])
        for cp in (local if isinstance(local, list) else [local]) + remote:
            cp.start()
        refs[-1][...] = jnp.zeros_like(refs[-1])

    if src_arr is _Chain.last:
        _Chain.last = None
    lands = [lax.empty(half, src_arr.dtype) for _ in range(n_land)]
    res = _chained(
        body, name=name, link=-1, in_specs=[HBM_SPEC] * (1 + n_land),
        out_specs=[SEM_SPEC] * 3 + [HBM_SPEC] * (1 + n_land) + [pl.BlockSpec(memory_space=pltpu.VMEM)],
        out_shape=[pltpu.SemaphoreType.DMA((n,)) for n in n_sems] + [pltpu.HBM(src_arr.shape, src_arr.dtype)]
        + [pltpu.HBM(half, src_arr.dtype)] * n_land + [TOKEN],
        input_output_aliases={i: 3 + i for i in range(1 + n_land)},
        compiler_params=pltpu.CompilerParams(has_side_effects=EFFECT),
    )(_in_hbm(src_arr), *[_in_hbm(l) for l in lands])
    return res[:3], res[3], res[4:4 + n_land]


def _stage_wait(name, copies_of, state):
    sems, src_arr, lands = state
    n_land = len(lands)

    def body(*refs):
        local, remote = copies_of(refs[0], *refs[1:1 + n_land], refs[1 + n_land:4 + n_land])
        for cp in (local if isinstance(local, list) else [local]):
            cp.wait()
        for cp in remote:
            cp.wait_send()
            cp.wait_recv()

    return _chained(
        body, name=name, in_specs=[HBM_SPEC] * (1 + n_land) + [SEM_SPEC] * 3, out_specs=[HBM_SPEC] * n_land,
        out_shape=[pltpu.HBM(l.shape, l.dtype) for l in lands],
        input_output_aliases={1 + i: i for i in range(n_land)},
        compiler_params=pltpu.CompilerParams(has_side_effects=EFFECT),
    )(src_arr, *lands, *sems)


def _pair_sum(name, own, land, tm=64):
    n, rows, cols = own.shape
    tm = _tile(rows, tm)

    def body(a_ref, b_ref, o_ref):
        o_ref[...] = (a_ref[...].astype(F32) + b_ref[...].astype(F32)).astype(o_ref.dtype)

    blk = pl.BlockSpec((n, tm, cols), lambda i: (0, i, 0))
    return _chained(body, name=name, grid=(rows // tm,), in_specs=[blk, blk], out_specs=blk,
                    out_shape=jax.ShapeDtypeStruct(own.shape, own.dtype), compiler_params=_rowwise_params("parallel"))(own, land)


def _unstack_cols(w):
    return w.transpose(1, 0, 2).reshape(w.shape[1], N_DEV * w.shape[2])


def _stack_cols(w):
    return w.reshape(w.shape[0], N_DEV, w.shape[1] // N_DEV).transpose(1, 0, 2)


def _rope_tables(positions):
    half = MLA_ROPE // 2
    inv_freq = ROPE_THETA ** (-jnp.arange(half, dtype=F32) / half)
    ang = positions.astype(F32)[:, None] * inv_freq
    cos, sin = jnp.cos(ang), jnp.sin(ang)
    t = positions.shape[0]
    cos_t = jnp.concatenate([jnp.ones((t, MLA_NOPE), F32), cos, cos], axis=1)
    sin_t = jnp.concatenate([jnp.zeros((t, MLA_NOPE), F32), -sin, sin], axis=1)
    idx = jnp.arange(MLA_QK)
    partner = jnp.where(idx < MLA_NOPE, -1, jnp.where(idx < MLA_NOPE + half, idx + half, idx - half))
    swap = (idx[:, None] == partner[None, :]).astype(BF16)
    return cos_t, sin_t, swap


def _ffn_fwd(tag, x, gain, wt):
    h = _rms_fwd(tag + "_rms", x, gain, BF16)
    g = _mm_stack_nt_out(tag + "_gate", h, wt[tag + '_w_gate'], BF16)
    u, a = _mm_stack_nt_out(tag + "_up", h, wt[tag + '_w_up'], BF16,
                            fuse=(lambda u_blk, g_blk: (u_blk, g_blk * jax.nn.sigmoid(g_blk) * u_blk), [g]))
    y = _mm_stack_sum(tag + "_down", a, wt[tag + '_w_down'], F32, scale=0.5, res=x)
    return y, (x, h, g, u, a)


def _ffn_bwd_weights(tag, dy, saved, wd, comm):
    x, h, g, u, a = saved
    comm.grads({tag + '_w_down': _mm_stack_tn_right(tag + "_dwd", a, dy, BF16, scale=0.5)})

    def dact(da, g_blk, u_blk):
        sig = jax.nn.sigmoid(g_blk)
        return da * u_blk * sig * (1.0 + g_blk * (1.0 - sig)), da * g_blk * sig

    dg, du = _mm_stack_nt_out(tag + "_da", dy, wd, BF16, scale=0.5, fuse=(dact, [g, u]))
    comm.advance()
    comm.grads({tag + '_w_gate': _mm_stack_tn_right(tag + "_dwg", dg, h, BF16)})
    dwu = _mm_stack_tn_right(tag + "_dwu", du, h, BF16)
    comm.advance()
    comm.grads({tag + '_w_up': dwu})
    return dg, du


def _ffn_bwd_input(tag, dy, saved, dgu, gain, wg, wu, comm):
    dg, du = dgu
    dh = _mm_stack_sum(tag + "_dh_g", dg, wg, F32)
    comm.advance()
    dh = _mm_stack_sum(tag + "_dh_u", du, wu, F32, res=dh)
    return _rms_bwd(tag + "_drms", saved[0], gain, dh, res=dy)


def _local_step(x, mem, positions, sm, comm, target):
    t, d = x.shape
    nm = mem.shape[0]
    gs, gw = {}, {}

    class Weights(dict):
        def __missing__(self, name):
            self.update(comm.weights(next(k for k, names in enumerate(GATHERS) if name in names)))
            return self[name]

    wt = Weights()

    x1, ffn1_saved = _ffn_fwd("ffn1", x, sm['ffn1_norm'], wt)

    h2 = _rms_fwd("mix_rms", x1, sm['mix_norm'], BF16)
    w_in_ref = _unstack_cols(wt['w_in'])
    pieces = []
    for name, n in IN_PAD:
        piece = w_in_ref[:, REF_OFF[name]:REF_OFF[name] + REF_SIZE[name]]
        if n != REF_SIZE[name]:
            piece = jnp.pad(piece, ((0, 0), (0, n - REF_SIZE[name])))
        pieces.append(piece)
    w_in = jnp.concatenate(pieces, axis=1)
    z = _mm2("mix_in", h2, w_in, NN, F32)
    zs = {name: z[:, PAD_OFF[name]:PAD_OFF[name] + n] for name, n in IN_PAD}

    cq = _rms_fwd("mla_q_a_rms", zs['zq'], sm['q_a_norm'], BF16)
    q_raw = _mm_stack_nt_out("mla_q_up", cq, wt['w_q_up'], F32)
    ckv = _rms_fwd("mla_kv_a_rms", zs['zkv'], sm['kv_a_norm'], BF16)
    kv = _mm_stack_out("mla_kv_up", ckv, wt['w_kv_up'], F32)
    zkr = zs['zkr'][:, :MLA_ROPE]
    k_raw = jnp.concatenate([kv[:, :, :MLA_NOPE], jnp.broadcast_to(zkr[None], (MLA_HEADS, t, MLA_ROPE))], axis=2)
    v_mla = kv[:, :, MLA_NOPE:].astype(BF16)
    cos_t, sin_t, swap = _rope_tables(positions)
    q_raw2, k_raw2 = q_raw.reshape(MLA_HEADS * t, MLA_QK), k_raw.reshape(MLA_HEADS * t, MLA_QK)
    rope = (cos_t, sin_t, swap)
    qf = _rms_fwd("mla_q_rms", q_raw2, sm['mla_q_norm'], BF16, rope=rope).reshape(MLA_HEADS, t, MLA_QK)
    kf = _rms_fwd("mla_k_rms", k_raw2, sm['mla_k_norm'], BF16, rope=rope).reshape(MLA_HEADS, t, MLA_QK)
    o_mla = _attn_fwd("mla_attn", qf, kf, v_mla, MLA_QK ** -0.5, True)

    w_g2 = jnp.pad(_unstack_cols(wt['gla_w_gate2']), ((0, LANES - GLA_GATE_RANK), (0, 0)))
    pre = _mm2("gla_gate_pre", zs['zg'], w_g2, NN, F32)
    log_a = _gate_fwd("gla_gate", pre, sm['gla_b_gate'])
    o_gla_raw, states = _gla_fwd("gla_scan", zs['gq'], zs['gk'], zs['gv'], log_a, GLA_HEADS)
    o_gla_n = _rms_fwd("gla_out_rms", o_gla_raw, sm['gla_out_norm'], F32)
    o_gla = _swiglu_fwd("gla_out_gate", zs['zr'], o_gla_n, BF16)

    cat = jnp.concatenate([o_mla, o_gla], axis=1)
    w_out = wt['w_out'].reshape(d, d)
    x2 = _mm2("mix_out", cat, w_out, NN, F32, res=x1)

    w_mq, w_mk, w_mv = (wt[n].reshape(d, MEM_HEADS * MEM_HEAD_DIM) for n in ('mem_w_q', 'mem_w_k', 'mem_w_v'))
    hq = _rms_fwd("mem_attn_rms", x2, sm['mem_attn_norm'], BF16)
    hm = _rms_fwd("mem_rms", mem, sm['mem_norm'], BF16)

    def heads_out(name, a, b, out_dtype):
        m, kk = a.shape
        tm = _tile(m, 512)
        return _mm(name, a, b, (m // tm, MEM_HEADS, 1), ((tm, kk), lambda i, h, k: (i, 0)),
                   ((kk, MEM_HEAD_DIM), lambda i, h, k: (0, h)), ((None, tm, MEM_HEAD_DIM), lambda i, h, k: (h, i, 0)),
                   (MEM_HEADS, m, MEM_HEAD_DIM), out_dtype, NN)

    mq_raw = heads_out("mem_q", hq, w_mq, F32)
    mk_raw = heads_out("mem_k", hm, w_mk, F32)
    mv = heads_out("mem_v", hm, w_mv, BF16)
    mq = _rms_fwd("mem_q_rms", mq_raw.reshape(MEM_HEADS * t, MEM_HEAD_DIM), sm['mem_q_norm'], BF16)
    mk = _rms_fwd("mem_k_rms", mk_raw.reshape(MEM_HEADS * nm, MEM_HEAD_DIM), sm['mem_k_norm'], BF16)
    mq, mk = mq.reshape(MEM_HEADS, t, MEM_HEAD_DIM), mk.reshape(MEM_HEADS, nm, MEM_HEAD_DIM)
    o_mem = _attn_fwd("mem_attn", mq, mk, mv, MEM_HEAD_DIM ** -0.5, False)
    w_mo = wt['mem_w_o']
    mo_cols = w_mo.shape[2]
    tm = _tile(t, 512)
    x3 = _mm("mem_out", o_mem, w_mo, (t // tm, N_DEV, 1), ((tm, o_mem.shape[1]), lambda i, j, k: (i, 0)),
             ((None, o_mem.shape[1], mo_cols), lambda i, j, k: (j, 0, 0)), ((tm, mo_cols), lambda i, j, k: (i, j)),
             (t, d), F32, NN, res=x2)

    y, ffn2_saved = _ffn_fwd("ffn2", x3, sm['ffn2_norm'], wt)
    dy, loss_lanes = _loss("loss", y, target)

    dgu = _ffn_bwd_weights("ffn2", dy, ffn2_saved, wt['ffn2_w_down'], comm)
    dx3, gs['ffn2_norm'] = _ffn_bwd_input("ffn2", dy, ffn2_saved, dgu, sm['ffn2_norm'],
                                          wt['ffn2_w_gate'], wt['ffn2_w_up'], comm)

    do_mem = _mm("mem_do", dx3, w_mo, (t // tm, MEM_HEADS, N_DEV), ((tm, mo_cols), lambda i, h, k: (i, k)),
                 ((None, MEM_HEAD_DIM, mo_cols), lambda i, h, k: (k, h, 0)), ((tm, MEM_HEAD_DIM), lambda i, h, k: (i, h)),
                 (t, MEM_HEADS * MEM_HEAD_DIM), BF16, NT)
    tk = _tile(t, 512)
    gw['mem_w_o'] = _mm("mem_dwo", o_mem, dx3, (N_DEV, 1, t // tk), ((tk, o_mem.shape[1]), lambda j, i, k: (k, 0)),
                        ((tk, mo_cols), lambda j, i, k: (k, j)), ((None, o_mem.shape[1], mo_cols), lambda j, i, k: (j, 0, 0)),
                        w_mo.shape, BF16, TN)
    dmq, dmk, dmv = _attn_bwd("mem_dattn", mq, mk, mv, do_mem, MEM_HEAD_DIM ** -0.5, False)
    dmq_raw, gs['mem_q_norm'] = _rms_bwd("mem_q_drms", mq_raw.reshape(MEM_HEADS * t, MEM_HEAD_DIM), sm['mem_q_norm'],
                                         dmq.reshape(MEM_HEADS * t, MEM_HEAD_DIM))
    dmk_raw, gs['mem_k_norm'] = _rms_bwd("mem_k_drms", mk_raw.reshape(MEM_HEADS * nm, MEM_HEAD_DIM), sm['mem_k_norm'],
                                         dmk.reshape(MEM_HEADS * nm, MEM_HEAD_DIM))
    dmq_raw = dmq_raw.reshape(MEM_HEADS, t, MEM_HEAD_DIM)
    dmk_raw = dmk_raw.reshape(MEM_HEADS, nm, MEM_HEAD_DIM)

    def heads_in_nt(name, a, b, res=None):
        m, n = a.shape[1], b.shape[0]
        tm_, tn_ = _tile(m, 512), _tile(n, 1024)
        return _mm(name, a, b, (m // tm_, n // tn_, MEM_HEADS), ((None, tm_, MEM_HEAD_DIM), lambda i, j, k: (k, i, 0)),
                   ((tn_, MEM_HEAD_DIM), lambda i, j, k: (j, k)), ((tm_, tn_), lambda i, j, k: (i, j)), (m, n), F32, NT,
                   None, res)

    def heads_tn(name, a, b):
        m, kp = a.shape
        tm_, tk_ = _tile(kp, 1024), _tile(m, 512)
        return _mm(name, a, b, (kp // tm_, MEM_HEADS, m // tk_), ((tk_, tm_), lambda i, h, k: (k, i)),
                   ((None, tk_, MEM_HEAD_DIM), lambda i, h, k: (h, k, 0)), ((tm_, MEM_HEAD_DIM), lambda i, h, k: (i, h)),
                   (kp, MEM_HEADS * MEM_HEAD_DIM), BF16, TN)

    dhq = heads_in_nt("mem_dhq", dmq_raw, w_mq)
    gw['mem_w_q'] = heads_tn("mem_dwq", hq, dmq_raw).reshape(wt['mem_w_q'].shape)
    dhm = heads_in_nt("mem_dhm_k", dmk_raw, w_mk)
    dhm = heads_in_nt("mem_dhm_v", dmv, w_mv, res=dhm)
    gw['mem_w_k'] = heads_tn("mem_dwk", hm, dmk_raw).reshape(wt['mem_w_k'].shape)
    gw['mem_w_v'] = heads_tn("mem_dwv", hm, dmv).reshape(wt['mem_w_v'].shape)
    _, gs['mem_norm'] = _rms_bwd("mem_drms", mem, sm['mem_norm'], dhm)
    comm.grads({n: gw[n] for n in MEMORY})
    dx2, gs['mem_attn_norm'] = _rms_bwd("mem_attn_drms", x2, sm['mem_attn_norm'], dhq, res=dx3)

    dcat = _mm2("mix_dcat", dx2, w_out, NT, F32)
    gw['w_out'] = _mm2("mix_dwout", cat, dx2, TN, BF16, tm=1024, tn=2048, tk=512).reshape(wt['w_out'].shape)
    do_mla, do_gla = dcat[:, :MLA_HEADS * MLA_V], dcat[:, MLA_HEADS * MLA_V:]

    dzr, dgn = _swiglu_bwd("gla_out_dgate", do_gla, zs['zr'], o_gla_n, F32)
    do_gla_raw, gs['gla_out_norm'] = _rms_bwd("gla_out_drms", o_gla_raw, sm['gla_out_norm'], dgn)
    dgq, dgk, dgv, dlog_a = _gla_bwd("gla_dscan", zs['gq'], zs['gk'], zs['gv'], log_a, states, do_gla_raw, GLA_HEADS)
    dpre, gs['gla_b_gate'] = _gate_bwd("gla_dgate", pre, sm['gla_b_gate'], dlog_a)
    dw_g2 = _mm2("gla_dwgate", zs['zg'], dpre, TN, BF16, tk=512)
    gw['gla_w_gate2'] = _stack_cols(dw_g2[:GLA_GATE_RANK])
    dzg = _mm2("gla_dzg", dpre, w_g2, NT, F32)

    dqf, dkf, dv_mla = _attn_bwd("mla_dattn", qf, kf, v_mla, do_mla, MLA_QK ** -0.5, True)
    dq_raw, gs['mla_q_norm'] = _rms_bwd("mla_q_drms", q_raw2, sm['mla_q_norm'], dqf.reshape(MLA_HEADS * t, MLA_QK), rope=rope)
    dk_raw, gs['mla_k_norm'] = _rms_bwd("mla_k_drms", k_raw2, sm['mla_k_norm'], dkf.reshape(MLA_HEADS * t, MLA_QK), rope=rope)
    dq_raw = dq_raw.reshape(MLA_HEADS, t, MLA_QK)
    dk_raw = dk_raw.reshape(MLA_HEADS, t, MLA_QK)
    dkv = jnp.concatenate([dk_raw[:, :, :MLA_NOPE], dv_mla], axis=2)
    dzkr = jnp.sum(dk_raw[:, :, MLA_NOPE:], axis=0)
    gw['w_q_up'] = _mm_stack_tn_right("mla_dwq", dq_raw, cq, BF16)
    gw['w_kv_up'] = _mm_stack_tn_left("mla_dwkv", ckv, dkv, BF16)
    dcq = _mm_stack_sum("mla_dcq", dq_raw, wt['w_q_up'], F32)
    dckv = _mm_stack_nt_sum("mla_dckv", dkv, wt['w_kv_up'], F32)
    dzq, gs['q_a_norm'] = _rms_bwd("mla_q_a_drms", zs['zq'], sm['q_a_norm'], dcq)
    dzkv, gs['kv_a_norm'] = _rms_bwd("mla_kv_a_drms", zs['zkv'], sm['kv_a_norm'], dckv)

    dzs = {'zq': dzq, 'zkv': dzkv, 'gq': dgq, 'gk': dgk, 'gv': dgv, 'zr': dzr,
           'zkr': jnp.pad(dzkr, ((0, 0), (0, LANES - MLA_ROPE))), 'zg': dzg}
    dz = jnp.concatenate([dzs[name].astype(BF16) for name, _ in IN_PAD], axis=1)
    dw_in = _mm2("mix_dwin", h2, dz, TN, BF16, tm=1024, tn=2048, tk=512)
    dw_in_ref = jnp.concatenate([dw_in[:, PAD_OFF[name]:PAD_OFF[name] + n] for name, n in IN_REF], axis=1)
    gw['w_in'] = _stack_cols(dw_in_ref)
    comm.grads({n: gw[n] for n in MIXER})
    dh2 = _mm2("mix_dh", dz, w_in, NT, F32)
    dx1, gs['mix_norm'] = _rms_bwd("mix_drms", x1, sm['mix_norm'], dh2, res=dx2)

    dgu = _ffn_bwd_weights("ffn1", dx1, ffn1_saved, wt['ffn1_w_down'], comm)
    grad_x, gs['ffn1_norm'] = _ffn_bwd_input("ffn1", dx1, ffn1_saved, dgu, sm['ffn1_norm'],
                                             wt['ffn1_w_gate'], wt['ffn1_w_up'], comm)
    return loss_lanes, grad_x, gs


def _pad_lanes(v):
    n = v.shape[1]
    return jnp.pad(v, ((0, 0), (0, -n % LANES)))


def _pack_small(vals):
    return jnp.concatenate([_pad_lanes(vals[n]) for n in SMALL], axis=1)


def _unpack_small(packed, like):
    out, off = {}, 0
    for n in SMALL:
        size = like[n].shape[1]
        out[n] = packed[:, off:off + size]
        off += size + (-size % LANES)
    return out


def kernel(x, mem, positions, ffn1_norm, ffn1_w_gate, ffn1_w_up, ffn1_w_down, mix_norm, w_in, q_a_norm, w_q_up, kv_a_norm, w_kv_up, mla_q_norm, mla_k_norm, gla_w_gate2, gla_b_gate, gla_out_norm, w_out, mem_attn_norm, mem_norm, mem_w_q, mem_w_k, mem_w_v, mem_w_o, mem_q_norm, mem_k_norm, ffn2_norm, ffn2_w_gate, ffn2_w_up, ffn2_w_down, loss_target, m_ffn1_norm, m_ffn1_w_gate, m_ffn1_w_up, m_ffn1_w_down, m_mix_norm, m_w_in, m_q_a_norm, m_w_q_up, m_kv_a_norm, m_w_kv_up, m_mla_q_norm, m_mla_k_norm, m_gla_w_gate2, m_gla_b_gate, m_gla_out_norm, m_w_out, m_mem_attn_norm, m_mem_norm, m_mem_w_q, m_mem_w_k, m_mem_w_v, m_mem_w_o, m_mem_q_norm, m_mem_k_norm, m_ffn2_norm, m_ffn2_w_gate, m_ffn2_w_up, m_ffn2_w_down, v_ffn1_norm, v_ffn1_w_gate, v_ffn1_w_up, v_ffn1_w_down, v_mix_norm, v_w_in, v_q_a_norm, v_w_q_up, v_kv_a_norm, v_w_kv_up, v_mla_q_norm, v_mla_k_norm, v_gla_w_gate2, v_gla_b_gate, v_gla_out_norm, v_w_out, v_mem_attn_norm, v_mem_norm, v_mem_w_q, v_mem_w_k, v_mem_w_v, v_mem_w_o, v_mem_q_norm, v_mem_k_norm, v_ffn2_norm, v_ffn2_w_gate, v_ffn2_w_up, v_ffn2_w_down):
    inp = dict(locals())
    x, mem, positions, target = inp['x'][0], inp['mem'][0], inp['positions'][0], inp['loss_target'][0]
    sm = {n: inp[n] for n in SMALL}
    out = {}

    def stored(key):
        name = key[2:] if key[:2] in ('m_', 'v_') else key
        return inp[key][0].T if name in TRANSPOSED else inp[key][0]

    def as_given(name, r):
        return r.T[None] if name in TRANSPOSED else r[None]

    class Comm:
        def __init__(self):
            self.gathers = {0: self.start(0)}
            self.forwards, self.exchanges, self.pairs = {}, [], []

        def start(self, k):
            return _gather_start(f"gather_start_{k}", [stored(n).astype(BF16) for n in GATHERS[k]])

        def forward(self, k):
            if k not in self.forwards:
                self.forwards[k] = _gather_mid(f"gather_mid_{k}", self.gathers[k])
                self.gathers.update({nxt: self.start(nxt) for nxt in NEXT_GATHERS.get(k, [])})

        def weights(self, k):
            self.forward(k)
            if k in EARLY_FORWARD:
                self.forward(EARLY_FORWARD[k])
            return dict(zip(GATHERS[k], _gather_end(f"gather_end_{k}", self.forwards[k])))

        def grads(self, stacks):
            names = list(stacks)
            if names[0] in TWO_STAGE:
                (n,) = names
                self.pairs.append((n, _stage_start("pair_start_" + n, _pair_copies, stacks[n], 2, (N_CHIP,) * 3)))
            else:
                self.exchanges.append((names, _exchange_start("exchange_start_" + names[0], [stacks[n] for n in names])))

        def advance(self):
            for n, state in self.pairs:
                own, land = _stage_wait("pair_wait_" + n, _pair_copies, state)
                pairs = _pair_sum("pair_sum_" + n, own, land)
                self.exchanges.append(([n], _stage_start("chip_start_" + n, _chip_copies, pairs, 1, (N_CHIP - 1, N_CHIP - 1, 1))))
            self.pairs = []

        def update(self, count):
            todo, self.exchanges = self.exchanges[:count], self.exchanges[count:]
            for names, state in todo:
                if names[0] in TWO_STAGE:
                    parts = _stage_wait("chip_wait_" + names[0], _chip_copies, state)
                else:
                    parts = _exchange_wait("exchange_wait_" + names[0], state)
                for n, p in zip(names, parts):
                    res = _adamw("adamw_" + n, stored(n), stored('m_' + n), stored('v_' + n), p)
                    for kind, r in zip(('grad_', 'delta_', 'new_m_', 'new_v_'), res):
                        out[kind + n] = as_given(n, r)

    _Chain.last = None
    comm = Comm()
    loss_lanes, grad_x, gs = _local_step(x, mem, positions, sm, comm, target)
    out['loss'] = lax.psum(jnp.sum(loss_lanes), ("x", "y", "c"))
    out['grad_x'] = grad_x[None]

    comm.update(5)
    small_parts = _all_gather("gather_small", [_pack_small(gs)])[0]
    res = _adamw("adamw_small", _pack_small(sm), _pack_small({n: inp['m_' + n] for n in SMALL}),
                 _pack_small({n: inp['v_' + n] for n in SMALL}), small_parts)
    for kind, r in zip(('grad_', 'delta_', 'new_m_', 'new_v_'), res):
        for n, val in _unpack_small(r, sm).items():
            out[kind + n] = val

    comm.update(3)

    names = ['loss', 'grad_x'] + [k + n for k in ('grad_', 'delta_', 'new_m_', 'new_v_') for n in WEIGHTS]
    return tuple(out[n] for n in names)
```

```python
import functools
import math

import jax
import jax.numpy as jnp
from jax import lax
from jax.experimental import pallas as pl
from jax.experimental.pallas import tpu as pltpu

F32 = jnp.float32
BF16 = jnp.bfloat16

N_DEV = 8
EPS = 1e-6
CHUNK = 64
MLA_HEADS, MLA_NOPE, MLA_ROPE, MLA_V = 8, 128, 64, 128
MLA_QK = MLA_NOPE + MLA_ROPE
MLA_Q_RANK, MLA_KV_RANK = 512, 256
ROPE_THETA = 10000.0
GLA_HEADS, GLA_DK, GLA_DV, GLA_GATE_RANK = 4, 128, 256, 16
GLA_TAU = 16.0
MEM_HEADS, MEM_HEAD_DIM = 4, 128
ADAM_LR, ADAM_B1, ADAM_B2, ADAM_EPS, ADAM_WD, ADAM_STEP = 0.001, 0.9, 0.999, 1e-08, 0.01, 10

V7X_VMEM_BYTES = 64 * 1024 * 1024
LANES = 128

NN = (((1,), (0,)), ((), ()))
NT = (((1,), (1,)), ((), ()))
TN = (((0,), (0,)), ((), ()))

WEIGHTS = ['ffn1_norm', 'ffn1_w_gate', 'ffn1_w_up', 'ffn1_w_down', 'mix_norm', 'w_in', 'q_a_norm', 'w_q_up',
           'kv_a_norm', 'w_kv_up', 'mla_q_norm', 'mla_k_norm', 'gla_w_gate2', 'gla_b_gate', 'gla_out_norm', 'w_out',
           'mem_attn_norm', 'mem_norm', 'mem_w_q', 'mem_w_k', 'mem_w_v', 'mem_w_o', 'mem_q_norm', 'mem_k_norm',
           'ffn2_norm', 'ffn2_w_gate', 'ffn2_w_up', 'ffn2_w_down']
SMALL = ['ffn1_norm', 'mix_norm', 'q_a_norm', 'kv_a_norm', 'mla_q_norm', 'mla_k_norm', 'gla_b_gate', 'gla_out_norm',
         'mem_attn_norm', 'mem_norm', 'mem_q_norm', 'mem_k_norm', 'ffn2_norm']
MIXER = ['w_in', 'w_q_up', 'w_kv_up', 'gla_w_gate2', 'w_out']
MEMORY = ['mem_w_q', 'mem_w_k', 'mem_w_v', 'mem_w_o']
TRANSPOSED = ['ffn1_w_gate', 'ffn1_w_up', 'ffn2_w_gate', 'ffn2_w_up', 'w_q_up']
GATHERS = [['ffn1_w_gate'], ['ffn1_w_up'], ['ffn1_w_down'], MIXER, MEMORY, ['ffn2_w_gate', 'ffn2_w_up', 'ffn2_w_down']]
NEXT_GATHERS = {0: [1], 1: [2], 2: [3], 3: [4, 5]}
EARLY_FORWARD = {4: 5}

IN_REF = [('zq', 512), ('zkv', 256), ('zkr', 64), ('gq', 512), ('gk', 512), ('gv', 1024), ('zg', 16), ('zr', 1024)]
IN_PAD = [('zq', 512), ('zkv', 256), ('gq', 512), ('gk', 512), ('gv', 1024), ('zr', 1024), ('zkr', 128), ('zg', 128)]
IN_WIDTH = sum(n for _, n in IN_REF)
IN_PAD_WIDTH = sum(n for _, n in IN_PAD)


def _offsets(layout):
    out, off = {}, 0
    for name, n in layout:
        out[name] = off
        off += n
    return out


REF_OFF, PAD_OFF = _offsets(IN_REF), _offsets(IN_PAD)
REF_SIZE = dict(IN_REF)


def _tile(n, pref):
    return pref if n % pref == 0 else n


def _block_bytes(blk, dtype):
    dims = [d for d in blk if d is not None]
    if len(dims) >= 1:
        dims[-1] = -(-dims[-1] // LANES) * LANES
    return math.prod(dims) * jnp.dtype(dtype).itemsize


def _vmem_limit(pipelined_bytes, resident_bytes=0):
    need = 2 * pipelined_bytes + resident_bytes + (8 << 20)
    return int(min(max(need, 32 << 20), V7X_VMEM_BYTES - (6 << 20)))


class _Chain:
    last = None


def _chained(body, *, in_specs, link=0, **kwargs):
    def call(*operands):
        dep = _Chain.last
        if dep is not None and any(o is dep for o in operands):
            dep = None
        if dep is None:
            res = pl.pallas_call(body, in_specs=in_specs, **kwargs)(*operands)
        else:
            n = len(operands)

            def chained_body(*refs):
                body(*refs[:n], *refs[n + 1:])

            res = pl.pallas_call(chained_body, in_specs=list(in_specs) + [pl.BlockSpec(memory_space=pl.ANY)],
                                 **kwargs)(*operands, dep)
        _Chain.last = res[link] if isinstance(res, (list, tuple)) else res
        return res

    return call


def _rowwise_params(*semantics):
    return pltpu.CompilerParams(dimension_semantics=semantics, vmem_limit_bytes=48 << 20)


def _mm(name, a, b, grid, a_spec, b_spec, o_spec, out_shape, out_dtype, dims, scale=None, res=None, fuse=None):
    nk = grid[2]
    (a_blk, a_map), (b_blk, b_map), (o_blk, o_map) = a_spec, b_spec, o_spec
    acc_shape = tuple(d for d in o_blk if d is not None)
    extras = [res] if res is not None else (list(fuse[1]) if fuse else [])
    n_out = 2 if fuse else 1
    a_list, b_list = (list(a), list(b)) if isinstance(a, (tuple, list)) else ([a], [b])
    n_in = 2 * len(a_list)

    def body(*refs):
        a_refs, b_refs = refs[:n_in // 2], refs[n_in // 2:n_in]
        extra_refs = refs[n_in:n_in + len(extras)]
        out_refs = refs[n_in + len(extras):n_in + len(extras) + n_out]
        rest = refs[n_in + len(extras) + n_out:]
        r_ref = extra_refs[0] if res is not None else None
        o_ref = out_refs[0]

        def product():
            return sum(lax.dot_general(a_ref[...].astype(BF16), b_ref[...].astype(BF16), dims, preferred_element_type=F32)
                       for a_ref, b_ref in zip(a_refs, b_refs))

        def finish(r):
            if scale is not None:
                r = r * scale
            if r_ref is not None:
                r = r + r_ref[...].astype(F32)
            if fuse:
                for ref, val in zip(out_refs, fuse[0](r, *[e[...].astype(F32) for e in extra_refs])):
                    ref[...] = val.astype(ref.dtype)
            else:
                o_ref[...] = r.astype(o_ref.dtype)

        if nk == 1:
            finish(product())
        else:
            acc = rest[0]
            k = pl.program_id(2)

            @pl.when(k == 0)
            def _():
                acc[...] = product()

            @pl.when(k > 0)
            def _():
                acc[...] += product()

            @pl.when(k == nk - 1)
            def _():
                finish(acc[...])

    in_specs = [pl.BlockSpec(a_blk, a_map)] * len(a_list) + [pl.BlockSpec(b_blk, b_map)] * len(b_list)
    operands = a_list + b_list
    piped = sum(_block_bytes(a_blk, v.dtype) for v in a_list) + sum(_block_bytes(b_blk, v.dtype) for v in b_list)
    piped += _block_bytes(o_blk, out_dtype)
    for extra in extras:
        in_specs.append(pl.BlockSpec(o_blk, o_map))
        operands.append(extra)
        piped += _block_bytes(o_blk, extra.dtype)
    piped += (n_out - 1) * _block_bytes(o_blk, out_dtype)
    scratch = [pltpu.VMEM(acc_shape, F32)] if nk > 1 else []
    out_spec, out_sds = pl.BlockSpec(o_blk, o_map), jax.ShapeDtypeStruct(out_shape, out_dtype)
    return _chained(
        body, name=name, grid=grid, in_specs=in_specs, out_specs=(out_spec,) * n_out if fuse else out_spec,
        out_shape=(out_sds,) * n_out if fuse else out_sds, scratch_shapes=scratch,
        compiler_params=pltpu.CompilerParams(
            dimension_semantics=("parallel", "parallel", "arbitrary"),
            vmem_limit_bytes=_vmem_limit(piped, 3 * _block_bytes(acc_shape, F32))),
    )(*operands)


def _mm2(name, a, b, dims, out_dtype, tm=512, tn=1024, tk=2048, scale=None, res=None):
    if dims is NN:
        (m, kk), n = a.shape, b.shape[1]
    elif dims is NT:
        (m, kk), n = a.shape, b.shape[0]
    else:
        (kk, m), n = a.shape, b.shape[1]
    tm, tn, tk = _tile(m, tm), _tile(n, tn), _tile(kk, tk)
    a_spec = ((tk, tm), lambda i, j, k: (k, i)) if dims is TN else ((tm, tk), lambda i, j, k: (i, k))
    b_spec = ((tn, tk), lambda i, j, k: (j, k)) if dims is NT else ((tk, tn), lambda i, j, k: (k, j))
    return _mm(name, a, b, (m // tm, n // tn, kk // tk), a_spec, b_spec, ((tm, tn), lambda i, j, k: (i, j)),
               (m, n), out_dtype, dims, scale, res)


def _mm_stack_out(name, a, b, out_dtype, tm=512, tk=2048):
    (m, kk), (nj, _, n) = a.shape, b.shape
    tm, tk = _tile(m, tm), _tile(kk, tk)
    return _mm(name, a, b, (nj, m // tm, kk // tk), ((tm, tk), lambda j, i, k: (i, k)),
               ((None, tk, n), lambda j, i, k: (j, k, 0)), ((None, tm, n), lambda j, i, k: (j, i, 0)),
               (nj, m, n), out_dtype, NN)


def _mm_stack_nt_out(name, a, b, out_dtype, scale=None, fuse=None, tm=512, tk=2048):
    (m, kk), (nj, n, _) = a.shape, b.shape
    tm, tk = _tile(m, tm), _tile(kk, tk)
    return _mm(name, a, b, (nj, m // tm, kk // tk), ((tm, tk), lambda j, i, k: (i, k)),
               ((None, n, tk), lambda j, i, k: (j, 0, k)), ((None, tm, n), lambda j, i, k: (j, i, 0)),
               (nj, m, n), out_dtype, NT, scale, fuse=fuse)


def _mm_stack_sum(name, a, b, out_dtype, scale=None, res=None, tm=512, tn=1024):
    (nj, m, f), n = (a[0] if isinstance(a, tuple) else a).shape, (b[0] if isinstance(b, tuple) else b).shape[2]
    tm, tn = _tile(m, tm), _tile(n, tn)
    return _mm(name, a, b, (m // tm, n // tn, nj), ((None, tm, f), lambda i, j, k: (k, i, 0)),
               ((None, f, tn), lambda i, j, k: (k, 0, j)), ((tm, tn), lambda i, j, k: (i, j)),
               (m, n), out_dtype, NN, scale, res)


def _mm_stack_nt_sum(name, a, b, out_dtype, res=None, tm=512, tn=1024):
    (nj, m, f), n = a.shape, b.shape[1]
    tm, tn = _tile(m, tm), _tile(n, tn)
    return _mm(name, a, b, (m // tm, n // tn, nj), ((None, tm, f), lambda i, j, k: (k, i, 0)),
               ((None, tn, f), lambda i, j, k: (k, j, 0)), ((tm, tn), lambda i, j, k: (i, j)),
               (m, n), out_dtype, NT, None, res)


def _mm_stack_tn_left(name, a, b, out_dtype, tm=1024, tk=512):
    (m, kp), (nj, _, n) = a.shape, b.shape
    tm, tk = _tile(kp, tm), _tile(m, tk)
    return _mm(name, a, b, (nj, kp // tm, m // tk), ((tk, tm), lambda j, i, k: (k, i)),
               ((None, tk, n), lambda j, i, k: (j, k, 0)), ((None, tm, n), lambda j, i, k: (j, i, 0)),
               (nj, kp, n), out_dtype, TN)


def _mm_stack_tn_right(name, a, b, out_dtype, scale=None, tn=2048, tk=512):
    (nj, m, f), n = a.shape, b.shape[1]
    tn, tk = _tile(n, tn), _tile(m, tk)
    return _mm(name, a, b, (nj, n // tn, m // tk), ((None, tk, f), lambda j, i, k: (j, k, 0)),
               ((tk, tn), lambda j, i, k: (k, i)), ((None, f, tn), lambda j, i, k: (j, 0, i)),
               (nj, f, n), out_dtype, TN, scale)


def _split3(x):
    hi = x.astype(BF16)
    r1 = x - hi.astype(F32)
    mid = r1.astype(BF16)
    lo = (r1 - mid.astype(F32)).astype(BF16)
    return hi, mid, lo


def _swap_halves(x, swap):
    return sum(jnp.dot(part, swap, preferred_element_type=F32) for part in _split3(x))


def _rope_specs(rope, tm):
    cos_t, _, swap = rope
    nt = cos_t.shape[0] // tm
    tab = pl.BlockSpec((tm, cos_t.shape[1]), lambda i, c: (i % nt, 0))
    return [tab, tab, pl.BlockSpec(swap.shape, lambda i, c: (0, 0))]


def _rms_fwd(name, x, g, out_dtype, rope=None, tm=256):
    rows, cols = x.shape
    d = g.shape[1]
    tm = _tile(rows if rope is None else rope[0].shape[0], tm)

    def body(x_ref, g_ref, *refs):
        xf = x_ref[...].astype(F32)
        r = lax.rsqrt(jnp.mean(xf * xf, axis=-1, keepdims=True) + EPS)
        y = xf * r * g_ref[...]
        if rope is not None:
            c_ref, s_ref, p_ref = refs[:3]
            y = y * c_ref[...] + _swap_halves(y, p_ref[...]) * s_ref[...]
        refs[-1][...] = y.astype(refs[-1].dtype)

    return _chained(
        body, name=name, grid=(rows // tm, cols // d),
        in_specs=[pl.BlockSpec((tm, d), lambda i, c: (i, c)), pl.BlockSpec((1, d), lambda i, c: (0, 0))]
        + (_rope_specs(rope, tm) if rope is not None else []),
        out_specs=pl.BlockSpec((tm, d), lambda i, c: (i, c)),
        out_shape=jax.ShapeDtypeStruct((rows, cols), out_dtype),
        compiler_params=_rowwise_params("parallel", "parallel"),
    )(x, g, *(rope or ()))


def _rms_bwd(name, x, g, dy, res=None, rope=None, tm=256):
    rows, cols = x.shape
    d = g.shape[1]
    tm = _tile(rows if rope is None else rope[0].shape[0], tm)

    def body(*refs):
        x_ref, g_ref, dy_ref = refs[:3]
        dx_ref, dg_ref = refs[-2:]
        r_ref = refs[3] if res is not None else None
        xf = x_ref[...].astype(F32)
        r = lax.rsqrt(jnp.mean(xf * xf, axis=-1, keepdims=True) + EPS)
        xhat = xf * r
        dyf = dy_ref[...].astype(F32)
        if rope is not None:
            c_ref, s_ref, p_ref = refs[-5:-2]
            dyf = dyf * c_ref[...] + _swap_halves(dyf * s_ref[...], p_ref[...])

        @pl.when((pl.program_id(0) == 0) & (pl.program_id(1) == 0))
        def _():
            dg_ref[...] = jnp.zeros_like(dg_ref)

        dg_ref[...] += jnp.sum(dyf * xhat, axis=0, keepdims=True)
        dxh = dyf * g_ref[...]
        dx = r * (dxh - xhat * jnp.mean(dxh * xhat, axis=-1, keepdims=True))
        if r_ref is not None:
            dx = dx + r_ref[...].astype(F32)
        dx_ref[...] = dx

    blk = pl.BlockSpec((tm, d), lambda i, c: (i, c))
    in_specs = [blk, pl.BlockSpec((1, d), lambda i, c: (0, 0)), blk]
    operands = [x, g, dy]
    if res is not None:
        in_specs.append(blk)
        operands.append(res)
    if rope is not None:
        in_specs += _rope_specs(rope, tm)
        operands += list(rope)
    return _chained(
        body, name=name, grid=(rows // tm, cols // d), in_specs=in_specs,
        out_specs=(blk, pl.BlockSpec((1, d), lambda i, c: (0, 0))),
        out_shape=(jax.ShapeDtypeStruct((rows, cols), F32), jax.ShapeDtypeStruct((1, d), F32)),
        compiler_params=_rowwise_params("arbitrary", "arbitrary"),
    )(*operands)


def _swiglu_fwd(name, g, u, out_dtype, tm=256):
    rows, cols = g.shape
    tm = _tile(rows, tm)

    def body(g_ref, u_ref, o_ref):
        gf = g_ref[...].astype(F32)
        o_ref[...] = (gf * jax.nn.sigmoid(gf) * u_ref[...].astype(F32)).astype(o_ref.dtype)

    blk = pl.BlockSpec((tm, cols), lambda i: (i, 0))
    return _chained(
        body, name=name, grid=(rows // tm,), in_specs=[blk, blk], out_specs=blk,
        out_shape=jax.ShapeDtypeStruct((rows, cols), out_dtype),
        compiler_params=_rowwise_params("parallel"),
    )(g, u)


def _swiglu_bwd(name, da, g, u, out_dtype, tm=256):
    rows, cols = g.shape
    tm = _tile(rows, tm)

    def body(da_ref, g_ref, u_ref, dg_ref, du_ref):
        gf = g_ref[...].astype(F32)
        daf = da_ref[...].astype(F32)
        sig = jax.nn.sigmoid(gf)
        du_ref[...] = (daf * gf * sig).astype(du_ref.dtype)
        dg_ref[...] = (daf * u_ref[...].astype(F32) * sig * (1.0 + gf * (1.0 - sig))).astype(dg_ref.dtype)

    blk = pl.BlockSpec((tm, cols), lambda i: (i, 0))
    sds = jax.ShapeDtypeStruct((rows, cols), out_dtype)
    return _chained(
        body, name=name, grid=(rows // tm,), in_specs=[blk, blk, blk], out_specs=(blk, blk), out_shape=(sds, sds),
        compiler_params=_rowwise_params("parallel"),
    )(da, g, u)


def _gate_fwd(name, pre, bias, tm=256):
    rows, cols = pre.shape
    tm = _tile(rows, tm)

    def body(p_ref, b_ref, o_ref):
        z = p_ref[...] + b_ref[...]
        o_ref[...] = (jnp.minimum(z, 0.0) - jnp.log(1.0 + jnp.exp(-jnp.abs(z)))) * (1.0 / GLA_TAU)

    blk = pl.BlockSpec((tm, cols), lambda i: (i, 0))
    return _chained(
        body, name=name, grid=(rows // tm,), in_specs=[blk, pl.BlockSpec((1, cols), lambda i: (0, 0))], out_specs=blk,
        out_shape=jax.ShapeDtypeStruct((rows, cols), F32),
        compiler_params=_rowwise_params("parallel"),
    )(pre, bias)


def _gate_bwd(name, pre, bias, dla, tm=256):
    rows, cols = pre.shape
    tm = _tile(rows, tm)

    def body(p_ref, b_ref, d_ref, dp_ref, db_ref):
        z = p_ref[...] + b_ref[...]
        dp = d_ref[...] * (1.0 / GLA_TAU) / (1.0 + jnp.exp(z))
        dp_ref[...] = dp

        @pl.when(pl.program_id(0) == 0)
        def _():
            db_ref[...] = jnp.zeros_like(db_ref)

        db_ref[...] += jnp.sum(dp, axis=0, keepdims=True)

    blk = pl.BlockSpec((tm, cols), lambda i: (i, 0))
    row = pl.BlockSpec((1, cols), lambda i: (0, 0))
    return _chained(
        body, name=name, grid=(rows // tm,), in_specs=[blk, row, blk], out_specs=(blk, row),
        out_shape=(jax.ShapeDtypeStruct((rows, cols), F32), jax.ShapeDtypeStruct((1, cols), F32)),
        compiler_params=_rowwise_params("arbitrary"),
    )(pre, bias, dla)


def _loss(name, y, target, tm=256):
    rows, d = y.shape
    tm = _tile(rows, tm)

    def body(y_ref, t_ref, dy_ref, l_ref):
        err = y_ref[...] - t_ref[...]
        dy_ref[...] = err * (1.0 / d)

        @pl.when(pl.program_id(0) == 0)
        def _():
            l_ref[...] = jnp.zeros_like(l_ref)

        sq = (err * err).reshape(tm // 8, 8, d)
        l_ref[...] += jnp.sum(sq, axis=0) * (0.5 / d)

    blk = pl.BlockSpec((tm, d), lambda i: (i, 0))
    return _chained(
        body, name=name, grid=(rows // tm,), in_specs=[blk, blk],
        out_specs=(blk, pl.BlockSpec((8, d), lambda i: (0, 0))),
        out_shape=(jax.ShapeDtypeStruct((rows, d), F32), jax.ShapeDtypeStruct((8, d), F32)),
        compiler_params=_rowwise_params("arbitrary"),
    )(y, target)


def _per_query_block(causal, nq, tq, tk, inner):
    if not causal:
        inner(0, tk)
        return
    for block in range(nq):
        pl.when(pl.program_id(1) == block)(functools.partial(inner, block * tq, (block + 1) * tq))


def _scores(q, k, scale, causal, q0):
    s = lax.dot_general(q, k, NT, preferred_element_type=F32) * scale
    if causal:
        qc = (q0 + lax.broadcasted_iota(jnp.int32, s.shape, 0)) // CHUNK
        kc = lax.broadcasted_iota(jnp.int32, s.shape, 1) // CHUNK
        s = jnp.where(kc <= qc, s, -1e30)
    e = jnp.exp(s - jnp.max(s, axis=-1, keepdims=True))
    return e, jnp.sum(e, axis=-1, keepdims=True)


def _attn_fwd(name, q, k, v, scale, causal, tq=256):
    nh, t, dk = q.shape
    tk, dv = k.shape[1], v.shape[2]
    tq = _tile(t, tq)

    def body(q_ref, k_ref, v_ref, o_ref):
        def inner(q0, kv):
            e, l = _scores(q_ref[...], k_ref[0:kv, :], scale, causal, q0)
            o = jnp.dot(e.astype(BF16), v_ref[0:kv, :], preferred_element_type=F32)
            o_ref[...] = (o / l).astype(o_ref.dtype)

        _per_query_block(causal, t // tq, tq, tk, inner)

    return _chained(
        body, name=name, grid=(nh, t // tq),
        in_specs=[pl.BlockSpec((None, tq, dk), lambda h, i: (h, i, 0)), pl.BlockSpec((None, tk, dk), lambda h, i: (h, 0, 0)),
                  pl.BlockSpec((None, tk, dv), lambda h, i: (h, 0, 0))],
        out_specs=pl.BlockSpec((tq, dv), lambda h, i: (i, h)),
        out_shape=jax.ShapeDtypeStruct((t, nh * dv), BF16),
        compiler_params=pltpu.CompilerParams(dimension_semantics=("parallel", "parallel"),
                                             vmem_limit_bytes=_vmem_limit(0, 6 * tq * tk * 4)),
    )(q, k, v)


def _attn_bwd(name, q, k, v, do, scale, causal, tq=256):
    nh, t, dk = q.shape
    tk, dv = k.shape[1], v.shape[2]
    tq = _tile(t, tq)

    def body(q_ref, k_ref, v_ref, do_ref, dq_ref, dk_ref, dv_ref):
        @pl.when(pl.program_id(1) == 0)
        def _():
            dk_ref[...] = jnp.zeros_like(dk_ref)
            dv_ref[...] = jnp.zeros_like(dv_ref)

        def inner(q0, kv):
            qb, kb = q_ref[...], k_ref[0:kv, :]
            e, l = _scores(qb, kb, scale, causal, q0)
            p = e / l
            dob = do_ref[...].astype(BF16)
            dp = lax.dot_general(dob, v_ref[0:kv, :], NT, preferred_element_type=F32)
            ds = (p * (dp - jnp.sum(p * dp, axis=-1, keepdims=True)) * scale).astype(BF16)
            dq_ref[...] = jnp.dot(ds, kb, preferred_element_type=F32)
            dk_ref[0:kv, :] += lax.dot_general(ds, qb, TN, preferred_element_type=F32)
            dv_ref[0:kv, :] += lax.dot_general(p.astype(BF16), dob, TN, preferred_element_type=F32)

        _per_query_block(causal, t // tq, tq, tk, inner)

    return _chained(
        body, name=name, grid=(nh, t // tq),
        in_specs=[pl.BlockSpec((None, tq, dk), lambda h, i: (h, i, 0)), pl.BlockSpec((None, tk, dk), lambda h, i: (h, 0, 0)),
                  pl.BlockSpec((None, tk, dv), lambda h, i: (h, 0, 0)), pl.BlockSpec((tq, dv), lambda h, i: (i, h))],
        out_specs=(pl.BlockSpec((None, tq, dk), lambda h, i: (h, i, 0)), pl.BlockSpec((None, tk, dk), lambda h, i: (h, 0, 0)),
                   pl.BlockSpec((None, tk, dv), lambda h, i: (h, 0, 0))),
        out_shape=(jax.ShapeDtypeStruct((nh, t, dk), F32), jax.ShapeDtypeStruct((nh, tk, dk), F32),
                   jax.ShapeDtypeStruct((nh, tk, dv), F32)),
        compiler_params=pltpu.CompilerParams(dimension_semantics=("parallel", "arbitrary"),
                                             vmem_limit_bytes=_vmem_limit(0, 10 * tq * tk * 4)),
    )(q, k, v, do)


def _tri(lower):
    r = lax.broadcasted_iota(jnp.int32, (CHUNK, CHUNK), 0)
    c = lax.broadcasted_iota(jnp.int32, (CHUNK, CHUNK), 1)
    return jnp.where((c <= r) if lower else (c >= r), 1.0, 0.0).astype(BF16)


def _tri_dot(tri, x):
    return sum(jnp.dot(tri, part, preferred_element_type=F32) for part in _split3(x))


def _gla_fwd(name, q, k, v, la, nh):
    t = q.shape[0]
    dk, dv = q.shape[1] // nh, v.shape[1] // nh
    nc = t // CHUNK

    def body(q_ref, k_ref, v_ref, g_ref, o_ref, st_ref, state):
        @pl.when(pl.program_id(1) == 0)
        def _():
            state[...] = jnp.zeros_like(state)

        g = g_ref[...]
        b = _tri_dot(_tri(True), g)
        b_end = jnp.sum(g, axis=0, keepdims=True)
        k_dec = (k_ref[...] * jnp.exp(b_end - b)).astype(BF16)
        u_t = lax.dot_general(v_ref[...].astype(BF16), k_dec, TN, preferred_element_type=F32)
        new = state[...] * jnp.exp(b_end) + u_t
        state[...] = new
        st_ref[...] = new
        qc = (q_ref[...] * (dk ** -0.5)).astype(BF16)
        o_ref[...] = lax.dot_general(qc, new.astype(BF16), NT, preferred_element_type=F32)

    kblk = pl.BlockSpec((CHUNK, dk), lambda h, n: (n, h))
    vblk = pl.BlockSpec((CHUNK, dv), lambda h, n: (n, h))
    return _chained(
        body, name=name, grid=(nh, nc), in_specs=[kblk, kblk, vblk, kblk],
        out_specs=(vblk, pl.BlockSpec((None, None, dv, dk), lambda h, n: (h, n, 0, 0))),
        out_shape=(jax.ShapeDtypeStruct((t, nh * dv), F32), jax.ShapeDtypeStruct((nh, nc, dv, dk), F32)),
        scratch_shapes=[pltpu.VMEM((dv, dk), F32)],
        compiler_params=_rowwise_params("parallel", "arbitrary"),
    )(q, k, v, la)


def _gla_bwd(name, q, k, v, la, states, do, nh):
    t = q.shape[0]
    dk, dv = q.shape[1] // nh, v.shape[1] // nh
    nc = t // CHUNK
    scale = dk ** -0.5

    def body(q_ref, k_ref, v_ref, g_ref, do_ref, st_ref, sp_ref, dq_ref, dk_ref, dv_ref, dg_ref, carry):
        i = pl.program_id(1)

        @pl.when(i == 0)
        def _():
            carry[...] = jnp.zeros_like(carry)

        g = g_ref[...]
        b = _tri_dot(_tri(True), g)
        b_end = jnp.sum(g, axis=0, keepdims=True)
        w = jnp.exp(b_end - b)
        decay = jnp.exp(b_end)
        k_dec = k_ref[...] * w
        qc = (q_ref[...] * scale).astype(BF16)
        dob = do_ref[...].astype(BF16)
        dq_ref[...] = jnp.dot(dob, st_ref[...].astype(BF16), preferred_element_type=F32) * scale
        g_t = carry[...] + lax.dot_general(dob, qc, TN, preferred_element_type=F32)
        g_tb = g_t.astype(BF16)
        dk_dec = jnp.dot(v_ref[...].astype(BF16), g_tb, preferred_element_type=F32)
        dv_ref[...] = lax.dot_general(k_dec.astype(BF16), g_tb, NT, preferred_element_type=F32)
        prev = jnp.where(i < nc - 1, sp_ref[...], 0.0)
        ddecay = jnp.sum(g_t * prev, axis=0, keepdims=True)
        dk_ref[...] = dk_dec * w
        e = dk_dec * k_dec
        db_end = jnp.sum(e, axis=0, keepdims=True) + ddecay * decay
        dg_ref[...] = _tri_dot(_tri(False), -e) + db_end
        carry[...] = g_t * decay

    kblk = pl.BlockSpec((CHUNK, dk), lambda h, i: (nc - 1 - i, h))
    vblk = pl.BlockSpec((CHUNK, dv), lambda h, i: (nc - 1 - i, h))
    ksds = jax.ShapeDtypeStruct((t, nh * dk), F32)
    return _chained(
        body, name=name, grid=(nh, nc),
        in_specs=[kblk, kblk, vblk, kblk, vblk,
                  pl.BlockSpec((None, None, dv, dk), lambda h, i: (h, nc - 1 - i, 0, 0)),
                  pl.BlockSpec((None, None, dv, dk), lambda h, i: (h, jnp.maximum(nc - 2 - i, 0), 0, 0))],
        out_specs=(kblk, kblk, vblk, kblk),
        out_shape=(ksds, ksds, jax.ShapeDtypeStruct((t, nh * dv), F32), ksds),
        scratch_shapes=[pltpu.VMEM((dv, dk), F32)],
        compiler_params=_rowwise_params("parallel", "arbitrary"),
    )(q, k, v, la, do, states, states)


def _adamw(name, w, m, v, parts):
    rows, cols = w.shape
    tm = _tile(rows, 1 << int(math.log2(max(8, (1 << 17) // (-(-cols // LANES) * LANES)))))

    def body(w_ref, m_ref, v_ref, p_ref, g_ref, d_ref, nm_ref, nv_ref):
        g = p_ref[0].astype(F32)
        for s in range(1, N_DEV):
            g = g + p_ref[s].astype(F32)
        m_new = ADAM_B1 * m_ref[...] + (1.0 - ADAM_B1) * g
        v_new = ADAM_B2 * v_ref[...] + (1.0 - ADAM_B2) * jnp.square(g)
        m_hat = m_new / (1.0 - ADAM_B1 ** ADAM_STEP)
        v_hat = v_new / (1.0 - ADAM_B2 ** ADAM_STEP)
        g_ref[...] = g
        d_ref[...] = -ADAM_LR * (m_hat / (jnp.sqrt(v_hat) + ADAM_EPS) + ADAM_WD * w_ref[...])
        nm_ref[...] = m_new
        nv_ref[...] = v_new

    blk = pl.BlockSpec((tm, cols), lambda i: (i, 0))
    sds = jax.ShapeDtypeStruct((rows, cols), F32)
    return _chained(
        body, name=name, grid=(rows // tm,),
        in_specs=[blk, blk, blk, pl.BlockSpec((N_DEV, tm, cols), lambda i: (0, i, 0))],
        out_specs=(blk, blk, blk, blk), out_shape=(sds, sds, sds, sds),
        compiler_params=_rowwise_params("parallel"),
    )(w, m, v, parts)


HBM = pl.BlockSpec(memory_space=pl.ANY)
MESH = pl.DeviceIdType.MESH


def _all_gather(name, shards):
    n = len(shards)

    def body(*refs):
        ins, outs = refs[:n], refs[n:2 * n]
        send_sems, recv_sems, local_sems = refs[2 * n:]
        x, y, c = lax.axis_index("x"), lax.axis_index("y"), lax.axis_index("c")
        me, sibling = (x, y, c), (x, y, 1 - c)
        chips = [(1 - x, y), (x, 1 - y), (1 - x, 1 - y)]

        def copy(w, k, block, to, src=None):
            dst = outs[w].at[4 * block[0] + 2 * block[1] + block[2]]
            return pltpu.make_async_remote_copy(
                src_ref=dst if src is None else src, dst_ref=dst, send_sem=send_sems.at[7 * w + k],
                recv_sem=recv_sems.at[7 * w + k], device_id=to, device_id_type=MESH)

        mine, first, passed = [], [], []
        for w in range(n):
            cp = pltpu.make_async_copy(ins[w], outs[w].at[4 * x + 2 * y + c], local_sems.at[w])
            cp.start()
            mine.append(cp)
            first.append(copy(w, 0, me, sibling, src=ins[w]))
            first += [copy(w, 1 + j, me, (*chip, c), src=ins[w]) for j, chip in enumerate(chips)]
        for cp in first:
            cp.start()
        for w in range(n):
            for j, chip in enumerate(chips):
                copy(w, 1 + j, (*chip, c), me).wait_recv()
                cp = copy(w, 4 + j, (*chip, c), sibling)
                cp.start()
                passed.append(cp)
        for w in range(n):
            copy(w, 0, sibling, me).wait_recv()
            for j, chip in enumerate(chips):
                copy(w, 4 + j, (*chip, 1 - c), me).wait_recv()
        for cp in first + passed:
            cp.wait_send()
        for cp in mine:
            cp.wait()

    return _chained(
        body, name=name, in_specs=[HBM] * n, out_specs=[HBM] * n,
        out_shape=[jax.ShapeDtypeStruct((N_DEV,) + s.shape, s.dtype) for s in shards],
        scratch_shapes=[pltpu.SemaphoreType.DMA((7 * n,)), pltpu.SemaphoreType.DMA((7 * n,)),
                        pltpu.SemaphoreType.DMA((n,))],
    )(*shards)


HBM_SPEC = pl.BlockSpec(memory_space=pltpu.HBM)
SEM_SPEC = pl.BlockSpec(memory_space=pltpu.SEMAPHORE)
EFFECT = pltpu.SideEffectType.DATAFLOW_SIDE_EFFECTING
TOKEN = jax.ShapeDtypeStruct((8, LANES), F32)


def _in_hbm(a):
    return pltpu.with_memory_space_constraint(a, pltpu.HBM)


def _place():
    x, y, c = lax.axis_index("x"), lax.axis_index("y"), lax.axis_index("c")
    chips = [(1 - x, y), (x, 1 - y), (1 - x, 1 - y)]
    return x, y, c, chips


def _block_of(px, py, pc):
    return 4 * px + 2 * py + pc


def _gather_copies(ins, outs, sems, w):
    send_sems, recv_sems, local_sems = sems
    x, y, c, chips = _place()
    peers = [(x, y, 1 - c)] + [(*chip, c) for chip in chips]

    def copy(k, block, to, src):
        dst = outs[w].at[_block_of(*block)]
        return pltpu.make_async_remote_copy(src_ref=dst if src is None else src, dst_ref=dst, send_sem=send_sems.at[4 * w + k],
                                            recv_sem=recv_sems.at[4 * w + k], device_id=to, device_id_type=MESH)

    local = pltpu.make_async_copy(ins[w], outs[w].at[_block_of(x, y, c)], local_sems.at[w])
    sends = [copy(k, (x, y, c), peer, ins[w]) for k, peer in enumerate(peers)]
    recvs = [copy(k, peer, (x, y, c), None) for k, peer in enumerate(peers)]
    return local, sends, recvs


def _forward_copies(outs, sems, w):
    send_sems, recv_sems = sems
    x, y, c, chips = _place()

    def copy(j, block, to):
        dst = outs[w].at[_block_of(*block)]
        return pltpu.make_async_remote_copy(src_ref=dst, dst_ref=dst, send_sem=send_sems.at[3 * w + j],
                                            recv_sem=recv_sems.at[3 * w + j], device_id=to, device_id_type=MESH)

    sends = [copy(j, (*chip, c), (x, y, 1 - c)) for j, chip in enumerate(chips)]
    recvs = [copy(j, (*chip, 1 - c), (x, y, c)) for j, chip in enumerate(chips)]
    return sends, recvs


def _gather_start(name, shards):
    n = len(shards)

    def body(*refs):
        ins, outs, sems = refs[:n], refs[n:2 * n], refs[2 * n:2 * n + 3]
        for w in range(n):
            local, sends, _ = _gather_copies(ins, outs, sems, w)
            local.start()
            for cp in sends:
                cp.start()
        refs[-1][...] = jnp.zeros_like(refs[-1])

    lands = [lax.empty((N_DEV,) + s.shape, s.dtype) for s in shards]
    res = _chained(
        body, name=name, link=-1, in_specs=[HBM_SPEC] * (2 * n),
        out_specs=[SEM_SPEC] * 3 + [HBM_SPEC] * (2 * n) + [pl.BlockSpec(memory_space=pltpu.VMEM)],
        out_shape=[pltpu.SemaphoreType.DMA((4 * n,)), pltpu.SemaphoreType.DMA((4 * n,)), pltpu.SemaphoreType.DMA((n,))]
        + [pltpu.HBM(s.shape, s.dtype) for s in shards] + [pltpu.HBM(l.shape, l.dtype) for l in lands] + [TOKEN],
        input_output_aliases={i: 3 + i for i in range(2 * n)},
        compiler_params=pltpu.CompilerParams(has_side_effects=EFFECT),
    )(*[_in_hbm(s) for s in shards], *[_in_hbm(l) for l in lands])
    return n, res[:3], res[3:3 + n], res[3 + n:3 + 2 * n]


def _gather_mid(name, state):
    n, sems, shards, lands = state

    def body(*refs):
        ins, outs, sems_in = refs[:n], refs[n:2 * n], refs[2 * n:2 * n + 3]
        sems_out = refs[2 * n + 3:2 * n + 5]
        for w in range(n):
            local, sends, recvs = _gather_copies(ins, outs, sems_in, w)
            local.wait()
            for cp in sends:
                cp.wait_send()
            for cp in recvs:
                cp.wait_recv()
            for cp in _forward_copies(outs, sems_out, w)[0]:
                cp.start()
        refs[-1][...] = jnp.zeros_like(refs[-1])

    res = _chained(
        body, name=name, link=-1, in_specs=[HBM_SPEC] * (2 * n) + [SEM_SPEC] * 3,
        out_specs=[SEM_SPEC] * 2 + [HBM_SPEC] * n + [pl.BlockSpec(memory_space=pltpu.VMEM)],
        out_shape=[pltpu.SemaphoreType.DMA((3 * n,)), pltpu.SemaphoreType.DMA((3 * n,))]
        + [pltpu.HBM(l.shape, l.dtype) for l in lands] + [TOKEN],
        input_output_aliases={n + i: 2 + i for i in range(n)},
        compiler_params=pltpu.CompilerParams(has_side_effects=EFFECT),
    )(*shards, *lands, *sems)
    return n, res[:2], res[2:2 + n]


def _gather_end(name, state):
    n, sems, lands = state

    def body(*refs):
        outs, sems_in = refs[:n], refs[n:n + 2]
        for w in range(n):
            sends, recvs = _forward_copies(outs, sems_in, w)
            for cp in sends:
                cp.wait_send()
            for cp in recvs:
                cp.wait_recv()

    return _chained(
        body, name=name, in_specs=[HBM_SPEC] * n + [SEM_SPEC] * 2, out_specs=[HBM_SPEC] * n,
        out_shape=[pltpu.HBM(l.shape, l.dtype) for l in lands], input_output_aliases={i: i for i in range(n)},
        compiler_params=pltpu.CompilerParams(has_side_effects=EFFECT),
    )(*lands, *sems)


def _exchange_copies(ins, outs, sems, w):
    send_sems, recv_sems, local_sems = sems
    x, y, c, _ = _place()
    mine = _block_of(x, y, c)
    local = pltpu.make_async_copy(ins[w].at[mine], outs[w].at[mine], local_sems.at[w])
    remote = []
    for k in range(1, N_DEV):
        px, py, pc = x ^ (k >> 2), y ^ ((k >> 1) & 1), c ^ (k & 1)
        remote.append(pltpu.make_async_remote_copy(
            src_ref=ins[w].at[_block_of(px, py, pc)], dst_ref=outs[w].at[mine], send_sem=send_sems.at[7 * w + k - 1],
            recv_sem=recv_sems.at[7 * w + k - 1], device_id=(px, py, pc), device_id_type=MESH))
    return local, remote


def _exchange_start(name, stacks):
    n = len(stacks)

    def body(*refs):
        ins, outs, sems = refs[:n], refs[n:2 * n], refs[2 * n:2 * n + 3]
        for w in range(n):
            local, remote = _exchange_copies(ins, outs, sems, w)
            local.start()
            for cp in remote:
                cp.start()
        refs[-1][...] = jnp.zeros_like(refs[-1])

    lands = [lax.empty(s.shape, s.dtype) for s in stacks]
    if any(s is _Chain.last for s in stacks):
        _Chain.last = None
    res = _chained(
        body, name=name, link=-1, in_specs=[HBM_SPEC] * (2 * n),
        out_specs=[SEM_SPEC] * 3 + [HBM_SPEC] * (2 * n) + [pl.BlockSpec(memory_space=pltpu.VMEM)],
        out_shape=[pltpu.SemaphoreType.DMA((7 * n,)), pltpu.SemaphoreType.DMA((7 * n,)), pltpu.SemaphoreType.DMA((n,))]
        + [pltpu.HBM(s.shape, s.dtype) for s in stacks] * 2 + [TOKEN],
        input_output_aliases={i: 3 + i for i in range(2 * n)},
        compiler_params=pltpu.CompilerParams(has_side_effects=EFFECT),
    )(*[_in_hbm(s) for s in stacks], *[_in_hbm(l) for l in lands])
    return n, res[:3], res[3:3 + n], res[3 + n:3 + 2 * n]


def _exchange_wait(name, state):
    n, sems, stacks, lands = state

    def body(*refs):
        ins, outs, sems_in = refs[:n], refs[n:2 * n], refs[2 * n:2 * n + 3]
        for w in range(n):
            local, remote = _exchange_copies(ins, outs, sems_in, w)
            local.wait()
            for cp in remote:
                cp.wait_send()
                cp.wait_recv()

    return _chained(
        body, name=name, in_specs=[HBM_SPEC] * (2 * n) + [SEM_SPEC] * 3, out_specs=[HBM_SPEC] * n,
        out_shape=[pltpu.HBM(l.shape, l.dtype) for l in lands], input_output_aliases={n + i: i for i in range(n)},
        compiler_params=pltpu.CompilerParams(has_side_effects=EFFECT),
    )(*stacks, *lands, *sems)


def _unstack_cols(w):
    return w.transpose(1, 0, 2).reshape(w.shape[1], N_DEV * w.shape[2])


def _stack_cols(w):
    return w.reshape(w.shape[0], N_DEV, w.shape[1] // N_DEV).transpose(1, 0, 2)


def _rope_tables(positions):
    half = MLA_ROPE // 2
    inv_freq = ROPE_THETA ** (-jnp.arange(half, dtype=F32) / half)
    ang = positions.astype(F32)[:, None] * inv_freq
    cos, sin = jnp.cos(ang), jnp.sin(ang)
    t = positions.shape[0]
    cos_t = jnp.concatenate([jnp.ones((t, MLA_NOPE), F32), cos, cos], axis=1)
    sin_t = jnp.concatenate([jnp.zeros((t, MLA_NOPE), F32), -sin, sin], axis=1)
    idx = jnp.arange(MLA_QK)
    partner = jnp.where(idx < MLA_NOPE, -1, jnp.where(idx < MLA_NOPE + half, idx + half, idx - half))
    swap = (idx[:, None] == partner[None, :]).astype(BF16)
    return cos_t, sin_t, swap


def _ffn_fwd(tag, x, gain, wt):
    h = _rms_fwd(tag + "_rms", x, gain, BF16)
    g = _mm_stack_nt_out(tag + "_gate", h, wt[tag + '_w_gate'], BF16)
    u, a = _mm_stack_nt_out(tag + "_up", h, wt[tag + '_w_up'], BF16,
                            fuse=(lambda u_blk, g_blk: (u_blk, g_blk * jax.nn.sigmoid(g_blk) * u_blk), [g]))
    y = _mm_stack_sum(tag + "_down", a, wt[tag + '_w_down'], F32, scale=0.5, res=x)
    return y, (x, h, g, u, a)


def _ffn_bwd_weights(tag, dy, saved, wd, comm):
    x, h, g, u, a = saved
    comm.grads({tag + '_w_down': _mm_stack_tn_right(tag + "_dwd", a, dy, BF16, scale=0.5)})

    def dact(da, g_blk, u_blk):
        sig = jax.nn.sigmoid(g_blk)
        return da * u_blk * sig * (1.0 + g_blk * (1.0 - sig)), da * g_blk * sig

    dg, du = _mm_stack_nt_out(tag + "_da", dy, wd, BF16, scale=0.5, fuse=(dact, [g, u]))
    comm.grads({tag + '_w_gate': _mm_stack_tn_right(tag + "_dwg", dg, h, BF16)})
    comm.grads({tag + '_w_up': _mm_stack_tn_right(tag + "_dwu", du, h, BF16)})
    return dg, du


def _ffn_bwd_input(tag, dy, saved, dgu, gain, wg, wu):
    dg, du = dgu
    dh = _mm_stack_sum(tag + "_dh", (dg, du), (wg, wu), F32)
    return _rms_bwd(tag + "_drms", saved[0], gain, dh, res=dy)


def _local_step(x, mem, positions, sm, comm, target):
    t, d = x.shape
    nm = mem.shape[0]
    gs, gw = {}, {}

    class Weights(dict):
        def __missing__(self, name):
            self.update(comm.weights(next(k for k, names in enumerate(GATHERS) if name in names)))
            return self[name]

    wt = Weights()

    x1, ffn1_saved = _ffn_fwd("ffn1", x, sm['ffn1_norm'], wt)

    h2 = _rms_fwd("mix_rms", x1, sm['mix_norm'], BF16)
    w_in_ref = _unstack_cols(wt['w_in'])
    pieces = []
    for name, n in IN_PAD:
        piece = w_in_ref[:, REF_OFF[name]:REF_OFF[name] + REF_SIZE[name]]
        if n != REF_SIZE[name]:
            piece = jnp.pad(piece, ((0, 0), (0, n - REF_SIZE[name])))
        pieces.append(piece)
    w_in = jnp.concatenate(pieces, axis=1)
    z = _mm2("mix_in", h2, w_in, NN, F32)
    zs = {name: z[:, PAD_OFF[name]:PAD_OFF[name] + n] for name, n in IN_PAD}

    cq = _rms_fwd("mla_q_a_rms", zs['zq'], sm['q_a_norm'], BF16)
    q_raw = _mm_stack_nt_out("mla_q_up", cq, wt['w_q_up'], F32)
    ckv = _rms_fwd("mla_kv_a_rms", zs['zkv'], sm['kv_a_norm'], BF16)
    kv = _mm_stack_out("mla_kv_up", ckv, wt['w_kv_up'], F32)
    zkr = zs['zkr'][:, :MLA_ROPE]
    k_raw = jnp.concatenate([kv[:, :, :MLA_NOPE], jnp.broadcast_to(zkr[None], (MLA_HEADS, t, MLA_ROPE))], axis=2)
    v_mla = kv[:, :, MLA_NOPE:].astype(BF16)
    cos_t, sin_t, swap = _rope_tables(positions)
    q_raw2, k_raw2 = q_raw.reshape(MLA_HEADS * t, MLA_QK), k_raw.reshape(MLA_HEADS * t, MLA_QK)
    rope = (cos_t, sin_t, swap)
    qf = _rms_fwd("mla_q_rms", q_raw2, sm['mla_q_norm'], BF16, rope=rope).reshape(MLA_HEADS, t, MLA_QK)
    kf = _rms_fwd("mla_k_rms", k_raw2, sm['mla_k_norm'], BF16, rope=rope).reshape(MLA_HEADS, t, MLA_QK)
    o_mla = _attn_fwd("mla_attn", qf, kf, v_mla, MLA_QK ** -0.5, True)

    w_g2 = jnp.pad(_unstack_cols(wt['gla_w_gate2']), ((0, LANES - GLA_GATE_RANK), (0, 0)))
    pre = _mm2("gla_gate_pre", zs['zg'], w_g2, NN, F32)
    log_a = _gate_fwd("gla_gate", pre, sm['gla_b_gate'])
    o_gla_raw, states = _gla_fwd("gla_scan", zs['gq'], zs['gk'], zs['gv'], log_a, GLA_HEADS)
    o_gla_n = _rms_fwd("gla_out_rms", o_gla_raw, sm['gla_out_norm'], F32)
    o_gla = _swiglu_fwd("gla_out_gate", zs['zr'], o_gla_n, BF16)

    cat = jnp.concatenate([o_mla, o_gla], axis=1)
    w_out = wt['w_out'].reshape(d, d)
    x2 = _mm2("mix_out", cat, w_out, NN, F32, res=x1)

    w_mq, w_mk, w_mv = (wt[n].reshape(d, MEM_HEADS * MEM_HEAD_DIM) for n in ('mem_w_q', 'mem_w_k', 'mem_w_v'))
    hq = _rms_fwd("mem_attn_rms", x2, sm['mem_attn_norm'], BF16)
    hm = _rms_fwd("mem_rms", mem, sm['mem_norm'], BF16)

    def heads_out(name, a, b, out_dtype):
        m, kk = a.shape
        tm = _tile(m, 512)
        return _mm(name, a, b, (m // tm, MEM_HEADS, 1), ((tm, kk), lambda i, h, k: (i, 0)),
                   ((kk, MEM_HEAD_DIM), lambda i, h, k: (0, h)), ((None, tm, MEM_HEAD_DIM), lambda i, h, k: (h, i, 0)),
                   (MEM_HEADS, m, MEM_HEAD_DIM), out_dtype, NN)

    mq_raw = heads_out("mem_q", hq, w_mq, F32)
    mk_raw = heads_out("mem_k", hm, w_mk, F32)
    mv = heads_out("mem_v", hm, w_mv, BF16)
    mq = _rms_fwd("mem_q_rms", mq_raw.reshape(MEM_HEADS * t, MEM_HEAD_DIM), sm['mem_q_norm'], BF16)
    mk = _rms_fwd("mem_k_rms", mk_raw.reshape(MEM_HEADS * nm, MEM_HEAD_DIM), sm['mem_k_norm'], BF16)
    mq, mk = mq.reshape(MEM_HEADS, t, MEM_HEAD_DIM), mk.reshape(MEM_HEADS, nm, MEM_HEAD_DIM)
    o_mem = _attn_fwd("mem_attn", mq, mk, mv, MEM_HEAD_DIM ** -0.5, False)
    w_mo = wt['mem_w_o']
    mo_cols = w_mo.shape[2]
    tm = _tile(t, 512)
    x3 = _mm("mem_out", o_mem, w_mo, (t // tm, N_DEV, 1), ((tm, o_mem.shape[1]), lambda i, j, k: (i, 0)),
             ((None, o_mem.shape[1], mo_cols), lambda i, j, k: (j, 0, 0)), ((tm, mo_cols), lambda i, j, k: (i, j)),
             (t, d), F32, NN, res=x2)

    y, ffn2_saved = _ffn_fwd("ffn2", x3, sm['ffn2_norm'], wt)
    dy, loss_lanes = _loss("loss", y, target)

    dgu = _ffn_bwd_weights("ffn2", dy, ffn2_saved, wt['ffn2_w_down'], comm)
    dx3, gs['ffn2_norm'] = _ffn_bwd_input("ffn2", dy, ffn2_saved, dgu, sm['ffn2_norm'],
                                          wt['ffn2_w_gate'], wt['ffn2_w_up'])

    do_mem = _mm("mem_do", dx3, w_mo, (t // tm, MEM_HEADS, N_DEV), ((tm, mo_cols), lambda i, h, k: (i, k)),
                 ((None, MEM_HEAD_DIM, mo_cols), lambda i, h, k: (k, h, 0)), ((tm, MEM_HEAD_DIM), lambda i, h, k: (i, h)),
                 (t, MEM_HEADS * MEM_HEAD_DIM), BF16, NT)
    tk = _tile(t, 512)
    gw['mem_w_o'] = _mm("mem_dwo", o_mem, dx3, (N_DEV, 1, t // tk), ((tk, o_mem.shape[1]), lambda j, i, k: (k, 0)),
                        ((tk, mo_cols), lambda j, i, k: (k, j)), ((None, o_mem.shape[1], mo_cols), lambda j, i, k: (j, 0, 0)),
                        w_mo.shape, BF16, TN)
    dmq, dmk, dmv = _attn_bwd("mem_dattn", mq, mk, mv, do_mem, MEM_HEAD_DIM ** -0.5, False)
    dmq_raw, gs['mem_q_norm'] = _rms_bwd("mem_q_drms", mq_raw.reshape(MEM_HEADS * t, MEM_HEAD_DIM), sm['mem_q_norm'],
                                         dmq.reshape(MEM_HEADS * t, MEM_HEAD_DIM))
    dmk_raw, gs['mem_k_norm'] = _rms_bwd("mem_k_drms", mk_raw.reshape(MEM_HEADS * nm, MEM_HEAD_DIM), sm['mem_k_norm'],
                                         dmk.reshape(MEM_HEADS * nm, MEM_HEAD_DIM))
    dmq_raw = dmq_raw.reshape(MEM_HEADS, t, MEM_HEAD_DIM)
    dmk_raw = dmk_raw.reshape(MEM_HEADS, nm, MEM_HEAD_DIM)

    def heads_in_nt(name, a, b, res=None):
        m, n = a.shape[1], b.shape[0]
        tm_, tn_ = _tile(m, 512), _tile(n, 1024)
        return _mm(name, a, b, (m // tm_, n // tn_, MEM_HEADS), ((None, tm_, MEM_HEAD_DIM), lambda i, j, k: (k, i, 0)),
                   ((tn_, MEM_HEAD_DIM), lambda i, j, k: (j, k)), ((tm_, tn_), lambda i, j, k: (i, j)), (m, n), F32, NT,
                   None, res)

    def heads_tn(name, a, b):
        m, kp = a.shape
        tm_, tk_ = _tile(kp, 1024), _tile(m, 512)
        return _mm(name, a, b, (kp // tm_, MEM_HEADS, m // tk_), ((tk_, tm_), lambda i, h, k: (k, i)),
                   ((None, tk_, MEM_HEAD_DIM), lambda i, h, k: (h, k, 0)), ((tm_, MEM_HEAD_DIM), lambda i, h, k: (i, h)),
                   (kp, MEM_HEADS * MEM_HEAD_DIM), BF16, TN)

    dhq = heads_in_nt("mem_dhq", dmq_raw, w_mq)
    gw['mem_w_q'] = heads_tn("mem_dwq", hq, dmq_raw).reshape(wt['mem_w_q'].shape)
    dhm = heads_in_nt("mem_dhm_k", dmk_raw, w_mk)
    dhm = heads_in_nt("mem_dhm_v", dmv, w_mv, res=dhm)
    gw['mem_w_k'] = heads_tn("mem_dwk", hm, dmk_raw).reshape(wt['mem_w_k'].shape)
    gw['mem_w_v'] = heads_tn("mem_dwv", hm, dmv).reshape(wt['mem_w_v'].shape)
    _, gs['mem_norm'] = _rms_bwd("mem_drms", mem, sm['mem_norm'], dhm)
    comm.grads({n: gw[n] for n in MEMORY})
    dx2, gs['mem_attn_norm'] = _rms_bwd("mem_attn_drms", x2, sm['mem_attn_norm'], dhq, res=dx3)

    dcat = _mm2("mix_dcat", dx2, w_out, NT, F32)
    comm.grads({'w_out': _mm2("mix_dwout", cat, dx2, TN, BF16, tm=1024, tn=2048, tk=512).reshape(wt['w_out'].shape)})
    do_mla, do_gla = dcat[:, :MLA_HEADS * MLA_V], dcat[:, MLA_HEADS * MLA_V:]

    dzr, dgn = _swiglu_bwd("gla_out_dgate", do_gla, zs['zr'], o_gla_n, F32)
    do_gla_raw, gs['gla_out_norm'] = _rms_bwd("gla_out_drms", o_gla_raw, sm['gla_out_norm'], dgn)
    dgq, dgk, dgv, dlog_a = _gla_bwd("gla_dscan", zs['gq'], zs['gk'], zs['gv'], log_a, states, do_gla_raw, GLA_HEADS)
    dpre, gs['gla_b_gate'] = _gate_bwd("gla_dgate", pre, sm['gla_b_gate'], dlog_a)
    dw_g2 = _mm2("gla_dwgate", zs['zg'], dpre, TN, BF16, tk=512)
    comm.grads({'gla_w_gate2': _stack_cols(dw_g2[:GLA_GATE_RANK])})
    dzg = _mm2("gla_dzg", dpre, w_g2, NT, F32)

    dqf, dkf, dv_mla = _attn_bwd("mla_dattn", qf, kf, v_mla, do_mla, MLA_QK ** -0.5, True)
    dq_raw, gs['mla_q_norm'] = _rms_bwd("mla_q_drms", q_raw2, sm['mla_q_norm'], dqf.reshape(MLA_HEADS * t, MLA_QK), rope=rope)
    dk_raw, gs['mla_k_norm'] = _rms_bwd("mla_k_drms", k_raw2, sm['mla_k_norm'], dkf.reshape(MLA_HEADS * t, MLA_QK), rope=rope)
    dq_raw = dq_raw.reshape(MLA_HEADS, t, MLA_QK)
    dk_raw = dk_raw.reshape(MLA_HEADS, t, MLA_QK)
    dkv = jnp.concatenate([dk_raw[:, :, :MLA_NOPE], dv_mla], axis=2)
    dzkr = jnp.sum(dk_raw[:, :, MLA_NOPE:], axis=0)
    comm.grads({'w_q_up': _mm_stack_tn_right("mla_dwq", dq_raw, cq, BF16),
                'w_kv_up': _mm_stack_tn_left("mla_dwkv", ckv, dkv, BF16)})
    dcq = _mm_stack_sum("mla_dcq", dq_raw, wt['w_q_up'], F32)
    dckv = _mm_stack_nt_sum("mla_dckv", dkv, wt['w_kv_up'], F32)
    dzq, gs['q_a_norm'] = _rms_bwd("mla_q_a_drms", zs['zq'], sm['q_a_norm'], dcq)
    dzkv, gs['kv_a_norm'] = _rms_bwd("mla_kv_a_drms", zs['zkv'], sm['kv_a_norm'], dckv)

    dzs = {'zq': dzq, 'zkv': dzkv, 'gq': dgq, 'gk': dgk, 'gv': dgv, 'zr': dzr,
           'zkr': jnp.pad(dzkr, ((0, 0), (0, LANES - MLA_ROPE))), 'zg': dzg}
    dz = jnp.concatenate([dzs[name].astype(BF16) for name, _ in IN_PAD], axis=1)
    dw_in = _mm2("mix_dwin", h2, dz, TN, BF16, tm=1024, tn=2048, tk=512)
    dw_in_ref = jnp.concatenate([dw_in[:, PAD_OFF[name]:PAD_OFF[name] + n] for name, n in IN_REF], axis=1)
    comm.grads({'w_in': _stack_cols(dw_in_ref)})
    dh2 = _mm2("mix_dh", dz, w_in, NT, F32)
    dx1, gs['mix_norm'] = _rms_bwd("mix_drms", x1, sm['mix_norm'], dh2, res=dx2)

    dgu = _ffn_bwd_weights("ffn1", dx1, ffn1_saved, wt['ffn1_w_down'], comm)
    grad_x, gs['ffn1_norm'] = _ffn_bwd_input("ffn1", dx1, ffn1_saved, dgu, sm['ffn1_norm'],
                                             wt['ffn1_w_gate'], wt['ffn1_w_up'])
    return loss_lanes, grad_x, gs


def _pad_lanes(v):
    n = v.shape[1]
    return jnp.pad(v, ((0, 0), (0, -n % LANES)))


def _pack_small(vals):
    return jnp.concatenate([_pad_lanes(vals[n]) for n in SMALL], axis=1)


def _unpack_small(packed, like):
    out, off = {}, 0
    for n in SMALL:
        size = like[n].shape[1]
        out[n] = packed[:, off:off + size]
        off += size + (-size % LANES)
    return out


def kernel(x, mem, positions, ffn1_norm, ffn1_w_gate, ffn1_w_up, ffn1_w_down, mix_norm, w_in, q_a_norm, w_q_up, kv_a_norm, w_kv_up, mla_q_norm, mla_k_norm, gla_w_gate2, gla_b_gate, gla_out_norm, w_out, mem_attn_norm, mem_norm, mem_w_q, mem_w_k, mem_w_v, mem_w_o, mem_q_norm, mem_k_norm, ffn2_norm, ffn2_w_gate, ffn2_w_up, ffn2_w_down, loss_target, m_ffn1_norm, m_ffn1_w_gate, m_ffn1_w_up, m_ffn1_w_down, m_mix_norm, m_w_in, m_q_a_norm, m_w_q_up, m_kv_a_norm, m_w_kv_up, m_mla_q_norm, m_mla_k_norm, m_gla_w_gate2, m_gla_b_gate, m_gla_out_norm, m_w_out, m_mem_attn_norm, m_mem_norm, m_mem_w_q, m_mem_w_k, m_mem_w_v, m_mem_w_o, m_mem_q_norm, m_mem_k_norm, m_ffn2_norm, m_ffn2_w_gate, m_ffn2_w_up, m_ffn2_w_down, v_ffn1_norm, v_ffn1_w_gate, v_ffn1_w_up, v_ffn1_w_down, v_mix_norm, v_w_in, v_q_a_norm, v_w_q_up, v_kv_a_norm, v_w_kv_up, v_mla_q_norm, v_mla_k_norm, v_gla_w_gate2, v_gla_b_gate, v_gla_out_norm, v_w_out, v_mem_attn_norm, v_mem_norm, v_mem_w_q, v_mem_w_k, v_mem_w_v, v_mem_w_o, v_mem_q_norm, v_mem_k_norm, v_ffn2_norm, v_ffn2_w_gate, v_ffn2_w_up, v_ffn2_w_down):
    inp = dict(locals())
    x, mem, positions, target = inp['x'][0], inp['mem'][0], inp['positions'][0], inp['loss_target'][0]
    sm = {n: inp[n] for n in SMALL}
    out = {}

    def stored(key):
        name = key[2:] if key[:2] in ('m_', 'v_') else key
        return inp[key][0].T if name in TRANSPOSED else inp[key][0]

    def as_given(name, r):
        return r.T[None] if name in TRANSPOSED else r[None]

    class Comm:
        def __init__(self):
            self.gathers = {0: self.start(0)}
            self.forwards, self.exchanges = {}, []

        def start(self, k):
            return _gather_start(f"gather_start_{k}", [stored(n).astype(BF16) for n in GATHERS[k]])

        def forward(self, k):
            if k not in self.forwards:
                self.forwards[k] = _gather_mid(f"gather_mid_{k}", self.gathers[k])
                self.gathers.update({nxt: self.start(nxt) for nxt in NEXT_GATHERS.get(k, [])})

        def weights(self, k):
            self.forward(k)
            if k in EARLY_FORWARD:
                self.forward(EARLY_FORWARD[k])
            return dict(zip(GATHERS[k], _gather_end(f"gather_end_{k}", self.forwards[k])))

        def grads(self, stacks):
            names = list(stacks)
            self.exchanges.append((names, _exchange_start("exchange_start_" + names[0], [stacks[n] for n in names])))

        def update(self, count):
            todo, self.exchanges = self.exchanges[:count], self.exchanges[count:]
            for names, state in todo:
                for n, p in zip(names, _exchange_wait("exchange_wait_" + names[0], state)):
                    res = _adamw("adamw_" + n, stored(n), stored('m_' + n), stored('v_' + n), p)
                    for kind, r in zip(('grad_', 'delta_', 'new_m_', 'new_v_'), res):
                        out[kind + n] = as_given(n, r)

    _Chain.last = None
    comm = Comm()
    loss_lanes, grad_x, gs = _local_step(x, mem, positions, sm, comm, target)
    out['loss'] = lax.psum(jnp.sum(loss_lanes), ("x", "y", "c"))
    out['grad_x'] = grad_x[None]

    comm.update(len(comm.exchanges) - 3)
    small_parts = _all_gather("gather_small", [_pack_small(gs)])[0]
    res = _adamw("adamw_small", _pack_small(sm), _pack_small({n: inp['m_' + n] for n in SMALL}),
                 _pack_small({n: inp['v_' + n] for n in SMALL}), small_parts)
    for kind, r in zip(('grad_', 'delta_', 'new_m_', 'new_v_'), res):
        for n, val in _unpack_small(r, sm).items():
            out[kind + n] = val

    comm.update(3)

    names = ['loss', 'grad_x'] + [k + n for k in ('grad_', 'delta_', 'new_m_', 'new_v_') for n in WEIGHTS]
    return tuple(out[n] for n in names)
```

```python
import functools
import math

import jax
import jax.numpy as jnp
from jax import lax
from jax.experimental import pallas as pl
from jax.experimental.pallas import tpu as pltpu

F32 = jnp.float32
BF16 = jnp.bfloat16

N_DEV = 8
EPS = 1e-6
CHUNK = 64
MLA_HEADS, MLA_NOPE, MLA_ROPE, MLA_V = 8, 128, 64, 128
MLA_QK = MLA_NOPE + MLA_ROPE
MLA_Q_RANK, MLA_KV_RANK = 512, 256
ROPE_THETA = 10000.0
GLA_HEADS, GLA_DK, GLA_DV, GLA_GATE_RANK = 4, 128, 256, 16
GLA_TAU = 16.0
MEM_HEADS, MEM_HEAD_DIM = 4, 128
ADAM_LR, ADAM_B1, ADAM_B2, ADAM_EPS, ADAM_WD, ADAM_STEP = 0.001, 0.9, 0.999, 1e-08, 0.01, 10

V7X_VMEM_BYTES = 64 * 1024 * 1024
LANES = 128

NN = (((1,), (0,)), ((), ()))
NT = (((1,), (1,)), ((), ()))
TN = (((0,), (0,)), ((), ()))

WEIGHTS = ['ffn1_norm', 'ffn1_w_gate', 'ffn1_w_up', 'ffn1_w_down', 'mix_norm', 'w_in', 'q_a_norm', 'w_q_up',
           'kv_a_norm', 'w_kv_up', 'mla_q_norm', 'mla_k_norm', 'gla_w_gate2', 'gla_b_gate', 'gla_out_norm', 'w_out',
           'mem_attn_norm', 'mem_norm', 'mem_w_q', 'mem_w_k', 'mem_w_v', 'mem_w_o', 'mem_q_norm', 'mem_k_norm',
           'ffn2_norm', 'ffn2_w_gate', 'ffn2_w_up', 'ffn2_w_down']
SMALL = ['ffn1_norm', 'mix_norm', 'q_a_norm', 'kv_a_norm', 'mla_q_norm', 'mla_k_norm', 'gla_b_gate', 'gla_out_norm',
         'mem_attn_norm', 'mem_norm', 'mem_q_norm', 'mem_k_norm', 'ffn2_norm']
MIXER = ['w_in', 'w_q_up', 'w_kv_up', 'gla_w_gate2', 'w_out']
MEMORY = ['mem_w_q', 'mem_w_k', 'mem_w_v', 'mem_w_o']
TRANSPOSED = ['ffn1_w_gate', 'ffn1_w_up', 'ffn2_w_gate', 'ffn2_w_up', 'w_q_up']
GATHERS = [['ffn1_w_gate'], ['ffn1_w_up'], ['ffn1_w_down'], MIXER, MEMORY, ['ffn2_w_gate', 'ffn2_w_up', 'ffn2_w_down']]
NEXT_GATHERS = {0: [1], 1: [2], 2: [3], 3: [4, 5]}
EARLY_FORWARD = {4: 5}

IN_REF = [('zq', 512), ('zkv', 256), ('zkr', 64), ('gq', 512), ('gk', 512), ('gv', 1024), ('zg', 16), ('zr', 1024)]
IN_PAD = [('zq', 512), ('zkv', 256), ('gq', 512), ('gk', 512), ('gv', 1024), ('zr', 1024), ('zkr', 128), ('zg', 128)]
IN_WIDTH = sum(n for _, n in IN_REF)
IN_PAD_WIDTH = sum(n for _, n in IN_PAD)


def _offsets(layout):
    out, off = {}, 0
    for name, n in layout:
        out[name] = off
        off += n
    return out


REF_OFF, PAD_OFF = _offsets(IN_REF), _offsets(IN_PAD)
REF_SIZE = dict(IN_REF)


def _tile(n, pref):
    return pref if n % pref == 0 else n


def _block_bytes(blk, dtype):
    dims = [d for d in blk if d is not None]
    if len(dims) >= 1:
        dims[-1] = -(-dims[-1] // LANES) * LANES
    return math.prod(dims) * jnp.dtype(dtype).itemsize


def _vmem_limit(pipelined_bytes, resident_bytes=0):
    need = 2 * pipelined_bytes + resident_bytes + (8 << 20)
    return int(min(max(need, 32 << 20), V7X_VMEM_BYTES - (6 << 20)))


class _Chain:
    last = None


def _chained(body, *, in_specs, link=0, **kwargs):
    def call(*operands):
        dep = _Chain.last
        if dep is not None and any(o is dep for o in operands):
            dep = None
        if dep is None:
            res = pl.pallas_call(body, in_specs=in_specs, **kwargs)(*operands)
        else:
            n = len(operands)

            def chained_body(*refs):
                body(*refs[:n], *refs[n + 1:])

            res = pl.pallas_call(chained_body, in_specs=list(in_specs) + [pl.BlockSpec(memory_space=pl.ANY)],
                                 **kwargs)(*operands, dep)
        _Chain.last = res[link] if isinstance(res, (list, tuple)) else res
        return res

    return call


def _rowwise_params(*semantics):
    return pltpu.CompilerParams(dimension_semantics=semantics, vmem_limit_bytes=48 << 20)


def _mm(name, a, b, grid, a_spec, b_spec, o_spec, out_shape, out_dtype, dims, scale=None, res=None, fuse=None):
    nk = grid[2]
    o_blk, o_map = o_spec
    acc_shape = tuple(d for d in o_blk if d is not None)
    extras = [res] if res is not None else (list(fuse[1]) if fuse else [])
    n_out = 2 if fuse else 1
    a_list, b_list = (list(a), list(b)) if isinstance(a, (tuple, list)) else ([a], [b])
    a_specs = a_spec if isinstance(a_spec, list) else [a_spec] * len(a_list)
    b_specs = b_spec if isinstance(b_spec, list) else [b_spec] * len(b_list)
    n_in = 2 * len(a_list)

    def body(*refs):
        a_refs, b_refs = refs[:n_in // 2], refs[n_in // 2:n_in]
        extra_refs = refs[n_in:n_in + len(extras)]
        out_refs = refs[n_in + len(extras):n_in + len(extras) + n_out]
        rest = refs[n_in + len(extras) + n_out:]
        r_ref = extra_refs[0] if res is not None else None
        o_ref = out_refs[0]

        def product():
            return sum(lax.dot_general(a_ref[...].astype(BF16), b_ref[...].astype(BF16), dims, preferred_element_type=F32)
                       for a_ref, b_ref in zip(a_refs, b_refs))

        def finish(r):
            if scale is not None:
                r = r * scale
            if r_ref is not None:
                r = r + r_ref[...].astype(F32)
            if fuse:
                for ref, val in zip(out_refs, fuse[0](r, *[e[...].astype(F32) for e in extra_refs])):
                    ref[...] = val.astype(ref.dtype)
            else:
                o_ref[...] = r.astype(o_ref.dtype)

        if nk == 1:
            finish(product())
        else:
            acc = rest[0]
            k = pl.program_id(2)

            @pl.when(k == 0)
            def _():
                acc[...] = product()

            @pl.when(k > 0)
            def _():
                acc[...] += product()

            @pl.when(k == nk - 1)
            def _():
                finish(acc[...])

    in_specs = [pl.BlockSpec(*spec) for spec in a_specs + b_specs]
    operands = a_list + b_list
    piped = sum(_block_bytes(spec[0], v.dtype) for spec, v in zip(a_specs + b_specs, operands))
    piped += _block_bytes(o_blk, out_dtype)
    for extra in extras:
        in_specs.append(pl.BlockSpec(o_blk, o_map))
        operands.append(extra)
        piped += _block_bytes(o_blk, extra.dtype)
    piped += (n_out - 1) * _block_bytes(o_blk, out_dtype)
    scratch = [pltpu.VMEM(acc_shape, F32)] if nk > 1 else []
    out_spec, out_sds = pl.BlockSpec(o_blk, o_map), jax.ShapeDtypeStruct(out_shape, out_dtype)
    return _chained(
        body, name=name, grid=grid, in_specs=in_specs, out_specs=(out_spec,) * n_out if fuse else out_spec,
        out_shape=(out_sds,) * n_out if fuse else out_sds, scratch_shapes=scratch,
        compiler_params=pltpu.CompilerParams(
            dimension_semantics=("parallel", "parallel", "arbitrary"),
            vmem_limit_bytes=_vmem_limit(piped, 3 * _block_bytes(acc_shape, F32))),
    )(*operands)


def _mm2(name, a, b, dims, out_dtype, tm=512, tn=1024, tk=2048, scale=None, res=None):
    if dims is NN:
        (m, kk), n = a.shape, b.shape[1]
    elif dims is NT:
        (m, kk), n = a.shape, b.shape[0]
    else:
        (kk, m), n = a.shape, b.shape[1]
    tm, tn, tk = _tile(m, tm), _tile(n, tn), _tile(kk, tk)
    a_spec = ((tk, tm), lambda i, j, k: (k, i)) if dims is TN else ((tm, tk), lambda i, j, k: (i, k))
    b_spec = ((tn, tk), lambda i, j, k: (j, k)) if dims is NT else ((tk, tn), lambda i, j, k: (k, j))
    return _mm(name, a, b, (m // tm, n // tn, kk // tk), a_spec, b_spec, ((tm, tn), lambda i, j, k: (i, j)),
               (m, n), out_dtype, dims, scale, res)


def _mm_stack_out(name, a, b, out_dtype, tm=512, tk=2048):
    (m, kk), (nj, _, n) = a.shape, b.shape
    tm, tk = _tile(m, tm), _tile(kk, tk)
    return _mm(name, a, b, (nj, m // tm, kk // tk), ((tm, tk), lambda j, i, k: (i, k)),
               ((None, tk, n), lambda j, i, k: (j, k, 0)), ((None, tm, n), lambda j, i, k: (j, i, 0)),
               (nj, m, n), out_dtype, NN)


def _mm_stack_nt_out(name, a, b, out_dtype, scale=None, fuse=None, tm=512, tk=2048):
    (m, kk), (nj, n, _) = a.shape, b.shape
    tm, tk = _tile(m, tm), _tile(kk, tk)
    return _mm(name, a, b, (nj, m // tm, kk // tk), ((tm, tk), lambda j, i, k: (i, k)),
               ((None, n, tk), lambda j, i, k: (j, 0, k)), ((None, tm, n), lambda j, i, k: (j, i, 0)),
               (nj, m, n), out_dtype, NT, scale, fuse=fuse)


def _mm_stack_sum(name, a, b, out_dtype, scale=None, res=None, tm=512, tn=1024):
    (nj, m, f), n = (a[0] if isinstance(a, tuple) else a).shape, (b[0] if isinstance(b, tuple) else b).shape[2]
    tm, tn = _tile(m, tm), _tile(n, tn)
    a_spec = ((None, tm, f), lambda i, j, k: (k, i, 0))
    b_spec = ((None, f, tn), lambda i, j, k: (k, 0, j))
    if not isinstance(a, tuple) and nj % 2 == 0:
        nj, a, b = nj // 2, (a, a), (b, b)
        a_spec = [((None, tm, f), lambda i, j, k: (2 * k, i, 0)), ((None, tm, f), lambda i, j, k: (2 * k + 1, i, 0))]
        b_spec = [((None, f, tn), lambda i, j, k: (2 * k, 0, j)), ((None, f, tn), lambda i, j, k: (2 * k + 1, 0, j))]
    return _mm(name, a, b, (m // tm, n // tn, nj), a_spec, b_spec, ((tm, tn), lambda i, j, k: (i, j)),
               (m, n), out_dtype, NN, scale, res)


def _mm_stack_nt_sum(name, a, b, out_dtype, res=None, tm=512, tn=1024):
    (nj, m, f), n = a.shape, b.shape[1]
    tm, tn = _tile(m, tm), _tile(n, tn)
    return _mm(name, a, b, (m // tm, n // tn, nj), ((None, tm, f), lambda i, j, k: (k, i, 0)),
               ((None, tn, f), lambda i, j, k: (k, j, 0)), ((tm, tn), lambda i, j, k: (i, j)),
               (m, n), out_dtype, NT, None, res)


def _mm_stack_tn_left(name, a, b, out_dtype, tm=1024, tk=512):
    (m, kp), (nj, _, n) = a.shape, b.shape
    tm, tk = _tile(kp, tm), _tile(m, tk)
    return _mm(name, a, b, (nj, kp // tm, m // tk), ((tk, tm), lambda j, i, k: (k, i)),
               ((None, tk, n), lambda j, i, k: (j, k, 0)), ((None, tm, n), lambda j, i, k: (j, i, 0)),
               (nj, kp, n), out_dtype, TN)


def _mm_stack_tn_right(name, a, b, out_dtype, scale=None, tn=2048, tk=512):
    (nj, m, f), n = a.shape, b.shape[1]
    tn, tk = _tile(n, tn), _tile(m, tk)
    return _mm(name, a, b, (nj, n // tn, m // tk), ((None, tk, f), lambda j, i, k: (j, k, 0)),
               ((tk, tn), lambda j, i, k: (k, i)), ((None, f, tn), lambda j, i, k: (j, 0, i)),
               (nj, f, n), out_dtype, TN, scale)


def _split3(x):
    hi = x.astype(BF16)
    r1 = x - hi.astype(F32)
    mid = r1.astype(BF16)
    lo = (r1 - mid.astype(F32)).astype(BF16)
    return hi, mid, lo


def _swap_halves(x, swap):
    return sum(jnp.dot(part, swap, preferred_element_type=F32) for part in _split3(x))


def _rope_specs(rope, tm):
    cos_t, _, swap = rope
    nt = cos_t.shape[0] // tm
    tab = pl.BlockSpec((tm, cos_t.shape[1]), lambda i, c: (i % nt, 0))
    return [tab, tab, pl.BlockSpec(swap.shape, lambda i, c: (0, 0))]


def _rms_fwd(name, x, g, out_dtype, rope=None, tm=256):
    rows, cols = x.shape
    d = g.shape[1]
    tm = _tile(rows if rope is None else rope[0].shape[0], tm)

    def body(x_ref, g_ref, *refs):
        xf = x_ref[...].astype(F32)
        r = lax.rsqrt(jnp.mean(xf * xf, axis=-1, keepdims=True) + EPS)
        y = xf * r * g_ref[...]
        if rope is not None:
            c_ref, s_ref, p_ref = refs[:3]
            y = y * c_ref[...] + _swap_halves(y, p_ref[...]) * s_ref[...]
        refs[-1][...] = y.astype(refs[-1].dtype)

    return _chained(
        body, name=name, grid=(rows // tm, cols // d),
        in_specs=[pl.BlockSpec((tm, d), lambda i, c: (i, c)), pl.BlockSpec((1, d), lambda i, c: (0, 0))]
        + (_rope_specs(rope, tm) if rope is not None else []),
        out_specs=pl.BlockSpec((tm, d), lambda i, c: (i, c)),
        out_shape=jax.ShapeDtypeStruct((rows, cols), out_dtype),
        compiler_params=_rowwise_params("parallel", "parallel"),
    )(x, g, *(rope or ()))


def _rms_bwd(name, x, g, dy, res=None, rope=None, narrow=False, tm=256):
    rows, cols = x.shape
    d = g.shape[1]
    tm = _tile(rows if rope is None else rope[0].shape[0], tm)

    def body(*refs):
        x_ref, g_ref, dy_ref = refs[:3]
        n_out = 3 if narrow else 2
        dx_ref, dg_ref = refs[-n_out], refs[-1]
        r_ref = refs[3] if res is not None else None
        xf = x_ref[...].astype(F32)
        r = lax.rsqrt(jnp.mean(xf * xf, axis=-1, keepdims=True) + EPS)
        xhat = xf * r
        dyf = dy_ref[...].astype(F32)
        if rope is not None:
            c_ref, s_ref, p_ref = refs[-n_out - 3:-n_out]
            dyf = dyf * c_ref[...] + _swap_halves(dyf * s_ref[...], p_ref[...])

        @pl.when((pl.program_id(0) == 0) & (pl.program_id(1) == 0))
        def _():
            dg_ref[...] = jnp.zeros_like(dg_ref)

        dg_ref[...] += jnp.sum(dyf * xhat, axis=0, keepdims=True)
        dxh = dyf * g_ref[...]
        dx = r * (dxh - xhat * jnp.mean(dxh * xhat, axis=-1, keepdims=True))
        if r_ref is not None:
            dx = dx + r_ref[...].astype(F32)
        dx_ref[...] = dx
        if narrow:
            refs[-2][...] = dx.astype(BF16)

    blk = pl.BlockSpec((tm, d), lambda i, c: (i, c))
    in_specs = [blk, pl.BlockSpec((1, d), lambda i, c: (0, 0)), blk]
    operands = [x, g, dy]
    if res is not None:
        in_specs.append(blk)
        operands.append(res)
    if rope is not None:
        in_specs += _rope_specs(rope, tm)
        operands += list(rope)
    wide = [(blk, jax.ShapeDtypeStruct((rows, cols), F32))] + ([(blk, jax.ShapeDtypeStruct((rows, cols), BF16))] if narrow else [])
    outs = wide + [(pl.BlockSpec((1, d), lambda i, c: (0, 0)), jax.ShapeDtypeStruct((1, d), F32))]
    return _chained(
        body, name=name, grid=(rows // tm, cols // d), in_specs=in_specs,
        out_specs=tuple(spec for spec, _ in outs), out_shape=tuple(sds for _, sds in outs),
        compiler_params=_rowwise_params("arbitrary", "arbitrary"),
    )(*operands)


def _swiglu_fwd(name, g, u, out_dtype, tm=256):
    rows, cols = g.shape
    tm = _tile(rows, tm)

    def body(g_ref, u_ref, o_ref):
        gf = g_ref[...].astype(F32)
        o_ref[...] = (gf * jax.nn.sigmoid(gf) * u_ref[...].astype(F32)).astype(o_ref.dtype)

    blk = pl.BlockSpec((tm, cols), lambda i: (i, 0))
    return _chained(
        body, name=name, grid=(rows // tm,), in_specs=[blk, blk], out_specs=blk,
        out_shape=jax.ShapeDtypeStruct((rows, cols), out_dtype),
        compiler_params=_rowwise_params("parallel"),
    )(g, u)


def _swiglu_bwd(name, da, g, u, out_dtype, tm=256):
    rows, cols = g.shape
    tm = _tile(rows, tm)

    def body(da_ref, g_ref, u_ref, dg_ref, du_ref):
        gf = g_ref[...].astype(F32)
        daf = da_ref[...].astype(F32)
        sig = jax.nn.sigmoid(gf)
        du_ref[...] = (daf * gf * sig).astype(du_ref.dtype)
        dg_ref[...] = (daf * u_ref[...].astype(F32) * sig * (1.0 + gf * (1.0 - sig))).astype(dg_ref.dtype)

    blk = pl.BlockSpec((tm, cols), lambda i: (i, 0))
    sds = jax.ShapeDtypeStruct((rows, cols), out_dtype)
    return _chained(
        body, name=name, grid=(rows // tm,), in_specs=[blk, blk, blk], out_specs=(blk, blk), out_shape=(sds, sds),
        compiler_params=_rowwise_params("parallel"),
    )(da, g, u)


def _gate_fwd(name, pre, bias, tm=256):
    rows, cols = pre.shape
    tm = _tile(rows, tm)

    def body(p_ref, b_ref, o_ref):
        z = p_ref[...] + b_ref[...]
        o_ref[...] = (jnp.minimum(z, 0.0) - jnp.log(1.0 + jnp.exp(-jnp.abs(z)))) * (1.0 / GLA_TAU)

    blk = pl.BlockSpec((tm, cols), lambda i: (i, 0))
    return _chained(
        body, name=name, grid=(rows // tm,), in_specs=[blk, pl.BlockSpec((1, cols), lambda i: (0, 0))], out_specs=blk,
        out_shape=jax.ShapeDtypeStruct((rows, cols), F32),
        compiler_params=_rowwise_params("parallel"),
    )(pre, bias)


def _gate_bwd(name, pre, bias, dla, tm=256):
    rows, cols = pre.shape
    tm = _tile(rows, tm)

    def body(p_ref, b_ref, d_ref, dp_ref, db_ref):
        z = p_ref[...] + b_ref[...]
        dp = d_ref[...] * (1.0 / GLA_TAU) / (1.0 + jnp.exp(z))
        dp_ref[...] = dp

        @pl.when(pl.program_id(0) == 0)
        def _():
            db_ref[...] = jnp.zeros_like(db_ref)

        db_ref[...] += jnp.sum(dp, axis=0, keepdims=True)

    blk = pl.BlockSpec((tm, cols), lambda i: (i, 0))
    row = pl.BlockSpec((1, cols), lambda i: (0, 0))
    return _chained(
        body, name=name, grid=(rows // tm,), in_specs=[blk, row, blk], out_specs=(blk, row),
        out_shape=(jax.ShapeDtypeStruct((rows, cols), F32), jax.ShapeDtypeStruct((1, cols), F32)),
        compiler_params=_rowwise_params("arbitrary"),
    )(pre, bias, dla)


def _loss(name, y, target, tm=256):
    rows, d = y.shape
    tm = _tile(rows, tm)

    def body(y_ref, t_ref, dy_ref, dyb_ref, l_ref):
        err = y_ref[...] - t_ref[...]
        dy_ref[...] = err * (1.0 / d)
        dyb_ref[...] = (err * (1.0 / d)).astype(BF16)

        @pl.when(pl.program_id(0) == 0)
        def _():
            l_ref[...] = jnp.zeros_like(l_ref)

        sq = (err * err).reshape(tm // 8, 8, d)
        l_ref[...] += jnp.sum(sq, axis=0) * (0.5 / d)

    blk = pl.BlockSpec((tm, d), lambda i: (i, 0))
    return _chained(
        body, name=name, grid=(rows // tm,), in_specs=[blk, blk],
        out_specs=(blk, blk, pl.BlockSpec((8, d), lambda i: (0, 0))),
        out_shape=(jax.ShapeDtypeStruct((rows, d), F32), jax.ShapeDtypeStruct((rows, d), BF16),
                   jax.ShapeDtypeStruct((8, d), F32)),
        compiler_params=_rowwise_params("arbitrary"),
    )(y, target)


def _per_query_block(causal, nq, tq, tk, inner):
    if not causal:
        inner(0, tk)
        return
    for block in range(nq):
        pl.when(pl.program_id(1) == block)(functools.partial(inner, block * tq, (block + 1) * tq))


def _scores(q, k, scale, causal, q0):
    s = lax.dot_general(q, k, NT, preferred_element_type=F32) * scale
    if causal:
        qc = (q0 + lax.broadcasted_iota(jnp.int32, s.shape, 0)) // CHUNK
        kc = lax.broadcasted_iota(jnp.int32, s.shape, 1) // CHUNK
        s = jnp.where(kc <= qc, s, -1e30)
    e = jnp.exp(s - jnp.max(s, axis=-1, keepdims=True))
    return e, jnp.sum(e, axis=-1, keepdims=True)


def _attn_fwd(name, q, k, v, scale, causal, tq=256):
    nh, t, dk = q.shape
    tk, dv = k.shape[1], v.shape[2]
    tq = _tile(t, tq)

    def body(q_ref, k_ref, v_ref, o_ref):
        def inner(q0, kv):
            e, l = _scores(q_ref[...], k_ref[0:kv, :], scale, causal, q0)
            o = jnp.dot(e.astype(BF16), v_ref[0:kv, :], preferred_element_type=F32)
            o_ref[...] = (o / l).astype(o_ref.dtype)

        _per_query_block(causal, t // tq, tq, tk, inner)

    return _chained(
        body, name=name, grid=(nh, t // tq),
        in_specs=[pl.BlockSpec((None, tq, dk), lambda h, i: (h, i, 0)), pl.BlockSpec((None, tk, dk), lambda h, i: (h, 0, 0)),
                  pl.BlockSpec((None, tk, dv), lambda h, i: (h, 0, 0))],
        out_specs=pl.BlockSpec((tq, dv), lambda h, i: (i, h)),
        out_shape=jax.ShapeDtypeStruct((t, nh * dv), BF16),
        compiler_params=pltpu.CompilerParams(dimension_semantics=("parallel", "parallel"),
                                             vmem_limit_bytes=_vmem_limit(0, 6 * tq * tk * 4)),
    )(q, k, v)


def _attn_bwd(name, q, k, v, do, scale, causal, tq=256):
    nh, t, dk = q.shape
    tk, dv = k.shape[1], v.shape[2]
    tq = _tile(t, tq)

    def body(q_ref, k_ref, v_ref, do_ref, dq_ref, dk_ref, dv_ref):
        @pl.when(pl.program_id(1) == 0)
        def _():
            dk_ref[...] = jnp.zeros_like(dk_ref)
            dv_ref[...] = jnp.zeros_like(dv_ref)

        def inner(q0, kv):
            qb, kb = q_ref[...], k_ref[0:kv, :]
            e, l = _scores(qb, kb, scale, causal, q0)
            p = e / l
            dob = do_ref[...].astype(BF16)
            dp = lax.dot_general(dob, v_ref[0:kv, :], NT, preferred_element_type=F32)
            ds = (p * (dp - jnp.sum(p * dp, axis=-1, keepdims=True)) * scale).astype(BF16)
            dq_ref[...] = jnp.dot(ds, kb, preferred_element_type=F32)
            dk_ref[0:kv, :] += lax.dot_general(ds, qb, TN, preferred_element_type=F32)
            dv_ref[0:kv, :] += lax.dot_general(p.astype(BF16), dob, TN, preferred_element_type=F32)

        _per_query_block(causal, t // tq, tq, tk, inner)

    return _chained(
        body, name=name, grid=(nh, t // tq),
        in_specs=[pl.BlockSpec((None, tq, dk), lambda h, i: (h, i, 0)), pl.BlockSpec((None, tk, dk), lambda h, i: (h, 0, 0)),
                  pl.BlockSpec((None, tk, dv), lambda h, i: (h, 0, 0)), pl.BlockSpec((tq, dv), lambda h, i: (i, h))],
        out_specs=(pl.BlockSpec((None, tq, dk), lambda h, i: (h, i, 0)), pl.BlockSpec((None, tk, dk), lambda h, i: (h, 0, 0)),
                   pl.BlockSpec((None, tk, dv), lambda h, i: (h, 0, 0))),
        out_shape=(jax.ShapeDtypeStruct((nh, t, dk), F32), jax.ShapeDtypeStruct((nh, tk, dk), F32),
                   jax.ShapeDtypeStruct((nh, tk, dv), F32)),
        compiler_params=pltpu.CompilerParams(dimension_semantics=("parallel", "arbitrary"),
                                             vmem_limit_bytes=_vmem_limit(0, 10 * tq * tk * 4)),
    )(q, k, v, do)


def _tri(lower):
    r = lax.broadcasted_iota(jnp.int32, (CHUNK, CHUNK), 0)
    c = lax.broadcasted_iota(jnp.int32, (CHUNK, CHUNK), 1)
    return jnp.where((c <= r) if lower else (c >= r), 1.0, 0.0).astype(BF16)


def _tri_dot(tri, x):
    return sum(jnp.dot(tri, part, preferred_element_type=F32) for part in _split3(x))


def _gla_fwd(name, q, k, v, la, nh):
    t = q.shape[0]
    dk, dv = q.shape[1] // nh, v.shape[1] // nh
    nc = t // CHUNK

    def body(q_ref, k_ref, v_ref, g_ref, o_ref, st_ref, state):
        @pl.when(pl.program_id(0) == 0)
        def _():
            state[...] = jnp.zeros_like(state)

        g = g_ref[...]
        b = _tri_dot(_tri(True), g)
        b_end = jnp.sum(g, axis=0, keepdims=True)
        k_dec = (k_ref[...] * jnp.exp(b_end - b)).astype(BF16)
        decay = jnp.exp(b_end)
        qc = (q_ref[...] * (dk ** -0.5)).astype(BF16)
        vb = v_ref[...].astype(BF16)
        for h in range(nh):
            ks, vs = slice(h * dk, (h + 1) * dk), slice(h * dv, (h + 1) * dv)
            u_t = lax.dot_general(vb[:, vs], k_dec[:, ks], TN, preferred_element_type=F32)
            new = state[h] * decay[:, ks] + u_t
            state[h] = new
            st_ref[h] = new
            o_ref[:, vs] = lax.dot_general(qc[:, ks], new.astype(BF16), NT, preferred_element_type=F32)

    kblk = pl.BlockSpec((CHUNK, nh * dk), lambda n: (n, 0))
    vblk = pl.BlockSpec((CHUNK, nh * dv), lambda n: (n, 0))
    return _chained(
        body, name=name, grid=(nc,), in_specs=[kblk, kblk, vblk, kblk],
        out_specs=(vblk, pl.BlockSpec((nh, None, dv, dk), lambda n: (0, n, 0, 0))),
        out_shape=(jax.ShapeDtypeStruct((t, nh * dv), F32), jax.ShapeDtypeStruct((nh, nc, dv, dk), F32)),
        scratch_shapes=[pltpu.VMEM((nh, dv, dk), F32)],
        compiler_params=_rowwise_params("arbitrary"),
    )(q, k, v, la)


def _gla_bwd(name, q, k, v, la, states, do, nh):
    t = q.shape[0]
    dk, dv = q.shape[1] // nh, v.shape[1] // nh
    nc = t // CHUNK
    scale = dk ** -0.5

    def body(q_ref, k_ref, v_ref, g_ref, do_ref, st_ref, sp_ref, dq_ref, dk_ref, dv_ref, dg_ref, carry):
        i = pl.program_id(0)

        @pl.when(i == 0)
        def _():
            carry[...] = jnp.zeros_like(carry)

        g = g_ref[...]
        b = _tri_dot(_tri(True), g)
        b_end = jnp.sum(g, axis=0, keepdims=True)
        w = jnp.exp(b_end - b)
        decay = jnp.exp(b_end)
        k_dec = k_ref[...] * w
        k_decb = k_dec.astype(BF16)
        qc = (q_ref[...] * scale).astype(BF16)
        dob = do_ref[...].astype(BF16)
        vb = v_ref[...].astype(BF16)
        dk_dec, ddecay = [], []
        for h in range(nh):
            ks, vs = slice(h * dk, (h + 1) * dk), slice(h * dv, (h + 1) * dv)
            dq_ref[:, ks] = jnp.dot(dob[:, vs], st_ref[h].astype(BF16), preferred_element_type=F32) * scale
            g_t = carry[h] + lax.dot_general(dob[:, vs], qc[:, ks], TN, preferred_element_type=F32)
            g_tb = g_t.astype(BF16)
            dk_dec.append(jnp.dot(vb[:, vs], g_tb, preferred_element_type=F32))
            dv_ref[:, vs] = lax.dot_general(k_decb[:, ks], g_tb, NT, preferred_element_type=F32)
            prev = jnp.where(i < nc - 1, sp_ref[h], 0.0)
            ddecay.append(jnp.sum(g_t * prev, axis=0, keepdims=True))
            carry[h] = g_t * decay[:, ks]
        dk_dec = jnp.concatenate(dk_dec, axis=1)
        dk_ref[...] = dk_dec * w
        e = dk_dec * k_dec
        db_end = jnp.sum(e, axis=0, keepdims=True) + jnp.concatenate(ddecay, axis=1) * decay
        dg_ref[...] = _tri_dot(_tri(False), -e) + db_end

    kblk = pl.BlockSpec((CHUNK, nh * dk), lambda i: (nc - 1 - i, 0))
    vblk = pl.BlockSpec((CHUNK, nh * dv), lambda i: (nc - 1 - i, 0))
    ksds = jax.ShapeDtypeStruct((t, nh * dk), F32)
    return _chained(
        body, name=name, grid=(nc,),
        in_specs=[kblk, kblk, vblk, kblk, vblk,
                  pl.BlockSpec((nh, None, dv, dk), lambda i: (0, nc - 1 - i, 0, 0)),
                  pl.BlockSpec((nh, None, dv, dk), lambda i: (0, jnp.maximum(nc - 2 - i, 0), 0, 0))],
        out_specs=(kblk, kblk, vblk, kblk),
        out_shape=(ksds, ksds, jax.ShapeDtypeStruct((t, nh * dv), F32), ksds),
        scratch_shapes=[pltpu.VMEM((nh, dv, dk), F32)],
        compiler_params=_rowwise_params("arbitrary"),
    )(q, k, v, la, do, states, states)


def _adamw(name, w, m, v, parts):
    rows, cols = w.shape
    tm = _tile(rows, 1 << int(math.log2(max(8, (1 << 17) // (-(-cols // LANES) * LANES)))))

    def body(w_ref, m_ref, v_ref, p_ref, g_ref, d_ref, nm_ref, nv_ref):
        g = p_ref[0].astype(F32)
        for s in range(1, N_DEV):
            g = g + p_ref[s].astype(F32)
        m_new = ADAM_B1 * m_ref[...] + (1.0 - ADAM_B1) * g
        v_new = ADAM_B2 * v_ref[...] + (1.0 - ADAM_B2) * jnp.square(g)
        m_hat = m_new / (1.0 - ADAM_B1 ** ADAM_STEP)
        v_hat = v_new / (1.0 - ADAM_B2 ** ADAM_STEP)
        g_ref[...] = g
        d_ref[...] = -ADAM_LR * (m_hat / (jnp.sqrt(v_hat) + ADAM_EPS) + ADAM_WD * w_ref[...])
        nm_ref[...] = m_new
        nv_ref[...] = v_new

    blk = pl.BlockSpec((tm, cols), lambda i: (i, 0))
    sds = jax.ShapeDtypeStruct((rows, cols), F32)
    return _chained(
        body, name=name, grid=(rows // tm,),
        in_specs=[blk, blk, blk, pl.BlockSpec((N_DEV, tm, cols), lambda i: (0, i, 0))],
        out_specs=(blk, blk, blk, blk), out_shape=(sds, sds, sds, sds),
        compiler_params=_rowwise_params("parallel"),
    )(w, m, v, parts)


HBM = pl.BlockSpec(memory_space=pl.ANY)
MESH = pl.DeviceIdType.MESH


def _all_gather(name, shards):
    n = len(shards)

    def body(*refs):
        ins, outs = refs[:n], refs[n:2 * n]
        send_sems, recv_sems, local_sems = refs[2 * n:]
        x, y, c = lax.axis_index("x"), lax.axis_index("y"), lax.axis_index("c")
        me, sibling = (x, y, c), (x, y, 1 - c)
        chips = [(1 - x, y), (x, 1 - y), (1 - x, 1 - y)]

        def copy(w, k, block, to, src=None):
            dst = outs[w].at[4 * block[0] + 2 * block[1] + block[2]]
            return pltpu.make_async_remote_copy(
                src_ref=dst if src is None else src, dst_ref=dst, send_sem=send_sems.at[7 * w + k],
                recv_sem=recv_sems.at[7 * w + k], device_id=to, device_id_type=MESH)

        mine, first, passed = [], [], []
        for w in range(n):
            cp = pltpu.make_async_copy(ins[w], outs[w].at[4 * x + 2 * y + c], local_sems.at[w])
            cp.start()
            mine.append(cp)
            first.append(copy(w, 0, me, sibling, src=ins[w]))
            first += [copy(w, 1 + j, me, (*chip, c), src=ins[w]) for j, chip in enumerate(chips)]
        for cp in first:
            cp.start()
        for w in range(n):
            for j, chip in enumerate(chips):
                copy(w, 1 + j, (*chip, c), me).wait_recv()
                cp = copy(w, 4 + j, (*chip, c), sibling)
                cp.start()
                passed.append(cp)
        for w in range(n):
            copy(w, 0, sibling, me).wait_recv()
            for j, chip in enumerate(chips):
                copy(w, 4 + j, (*chip, 1 - c), me).wait_recv()
        for cp in first + passed:
            cp.wait_send()
        for cp in mine:
            cp.wait()

    return _chained(
        body, name=name, in_specs=[HBM] * n, out_specs=[HBM] * n,
        out_shape=[jax.ShapeDtypeStruct((N_DEV,) + s.shape, s.dtype) for s in shards],
        scratch_shapes=[pltpu.SemaphoreType.DMA((7 * n,)), pltpu.SemaphoreType.DMA((7 * n,)),
                        pltpu.SemaphoreType.DMA((n,))],
    )(*shards)


HBM_SPEC = pl.BlockSpec(memory_space=pltpu.HBM)
SEM_SPEC = pl.BlockSpec(memory_space=pltpu.SEMAPHORE)
EFFECT = pltpu.SideEffectType.DATAFLOW_SIDE_EFFECTING
TOKEN = jax.ShapeDtypeStruct((8, LANES), F32)


def _in_hbm(a):
    return pltpu.with_memory_space_constraint(a, pltpu.HBM)


def _place():
    x, y, c = lax.axis_index("x"), lax.axis_index("y"), lax.axis_index("c")
    chips = [(1 - x, y), (x, 1 - y), (1 - x, 1 - y)]
    return x, y, c, chips


def _block_of(px, py, pc):
    return 4 * px + 2 * py + pc


def _gather_copies(ins, outs, sems, w):
    send_sems, recv_sems, local_sems = sems
    x, y, c, chips = _place()
    peers = [(x, y, 1 - c)] + [(*chip, c) for chip in chips]

    def copy(k, block, to, src):
        dst = outs[w].at[_block_of(*block)]
        return pltpu.make_async_remote_copy(src_ref=dst if src is None else src, dst_ref=dst, send_sem=send_sems.at[4 * w + k],
                                            recv_sem=recv_sems.at[4 * w + k], device_id=to, device_id_type=MESH)

    local = pltpu.make_async_copy(ins[w], outs[w].at[_block_of(x, y, c)], local_sems.at[w])
    sends = [copy(k, (x, y, c), peer, ins[w]) for k, peer in enumerate(peers)]
    recvs = [copy(k, peer, (x, y, c), None) for k, peer in enumerate(peers)]
    return local, sends, recvs


def _forward_copies(outs, sems, w):
    send_sems, recv_sems = sems
    x, y, c, chips = _place()

    def copy(j, block, to):
        dst = outs[w].at[_block_of(*block)]
        return pltpu.make_async_remote_copy(src_ref=dst, dst_ref=dst, send_sem=send_sems.at[3 * w + j],
                                            recv_sem=recv_sems.at[3 * w + j], device_id=to, device_id_type=MESH)

    sends = [copy(j, (*chip, c), (x, y, 1 - c)) for j, chip in enumerate(chips)]
    recvs = [copy(j, (*chip, 1 - c), (x, y, c)) for j, chip in enumerate(chips)]
    return sends, recvs


def _gather_start(name, shards):
    n = len(shards)

    def body(*refs):
        ins, outs, sems = refs[:n], refs[n:2 * n], refs[2 * n:2 * n + 3]
        for w in range(n):
            local, sends, _ = _gather_copies(ins, outs, sems, w)
            local.start()
            for cp in sends:
                cp.start()
        refs[-1][...] = jnp.zeros_like(refs[-1])

    lands = [lax.empty((N_DEV,) + s.shape, s.dtype) for s in shards]
    res = _chained(
        body, name=name, link=-1, in_specs=[HBM_SPEC] * (2 * n),
        out_specs=[SEM_SPEC] * 3 + [HBM_SPEC] * (2 * n) + [pl.BlockSpec(memory_space=pltpu.VMEM)],
        out_shape=[pltpu.SemaphoreType.DMA((4 * n,)), pltpu.SemaphoreType.DMA((4 * n,)), pltpu.SemaphoreType.DMA((n,))]
        + [pltpu.HBM(s.shape, s.dtype) for s in shards] + [pltpu.HBM(l.shape, l.dtype) for l in lands] + [TOKEN],
        input_output_aliases={i: 3 + i for i in range(2 * n)},
        compiler_params=pltpu.CompilerParams(has_side_effects=EFFECT),
    )(*[_in_hbm(s) for s in shards], *[_in_hbm(l) for l in lands])
    return n, res[:3], res[3:3 + n], res[3 + n:3 + 2 * n]


def _gather_mid(name, state):
    n, sems, shards, lands = state

    def body(*refs):
        ins, outs, sems_in = refs[:n], refs[n:2 * n], refs[2 * n:2 * n + 3]
        sems_out = refs[2 * n + 3:2 * n + 5]
        for w in range(n):
            local, sends, recvs = _gather_copies(ins, outs, sems_in, w)
            local.wait()
            for cp in sends:
                cp.wait_send()
            for cp in recvs:
                cp.wait_recv()
            for cp in _forward_copies(outs, sems_out, w)[0]:
                cp.start()
        refs[-1][...] = jnp.zeros_like(refs[-1])

    res = _chained(
        body, name=name, link=-1, in_specs=[HBM_SPEC] * (2 * n) + [SEM_SPEC] * 3,
        out_specs=[SEM_SPEC] * 2 + [HBM_SPEC] * n + [pl.BlockSpec(memory_space=pltpu.VMEM)],
        out_shape=[pltpu.SemaphoreType.DMA((3 * n,)), pltpu.SemaphoreType.DMA((3 * n,))]
        + [pltpu.HBM(l.shape, l.dtype) for l in lands] + [TOKEN],
        input_output_aliases={n + i: 2 + i for i in range(n)},
        compiler_params=pltpu.CompilerParams(has_side_effects=EFFECT),
    )(*shards, *lands, *sems)
    return n, res[:2], res[2:2 + n]


def _gather_end(name, state):
    n, sems, lands = state

    def body(*refs):
        outs, sems_in = refs[:n], refs[n:n + 2]
        for w in range(n):
            sends, recvs = _forward_copies(outs, sems_in, w)
            for cp in sends:
                cp.wait_send()
            for cp in recvs:
                cp.wait_recv()

    return _chained(
        body, name=name, in_specs=[HBM_SPEC] * n + [SEM_SPEC] * 2, out_specs=[HBM_SPEC] * n,
        out_shape=[pltpu.HBM(l.shape, l.dtype) for l in lands], input_output_aliases={i: i for i in range(n)},
        compiler_params=pltpu.CompilerParams(has_side_effects=EFFECT),
    )(*lands, *sems)


def _exchange_copies(ins, outs, sems, w):
    send_sems, recv_sems, local_sems = sems
    x, y, c, _ = _place()
    mine = _block_of(x, y, c)
    local = pltpu.make_async_copy(ins[w].at[mine], outs[w].at[mine], local_sems.at[w])
    remote = []
    for k in range(1, N_DEV):
        px, py, pc = x ^ (k >> 2), y ^ ((k >> 1) & 1), c ^ (k & 1)
        remote.append(pltpu.make_async_remote_copy(
            src_ref=ins[w].at[_block_of(px, py, pc)], dst_ref=outs[w].at[mine], send_sem=send_sems.at[7 * w + k - 1],
            recv_sem=recv_sems.at[7 * w + k - 1], device_id=(px, py, pc), device_id_type=MESH))
    return local, remote


def _exchange_start(name, stacks):
    n = len(stacks)

    def body(*refs):
        ins, outs, sems = refs[:n], refs[n:2 * n], refs[2 * n:2 * n + 3]
        for w in range(n):
            local, remote = _exchange_copies(ins, outs, sems, w)
            local.start()
            for cp in remote:
                cp.start()
        refs[-1][...] = jnp.zeros_like(refs[-1])

    lands = [lax.empty(s.shape, s.dtype) for s in stacks]
    if any(s is _Chain.last for s in stacks):
        _Chain.last = None
    res = _chained(
        body, name=name, link=-1, in_specs=[HBM_SPEC] * (2 * n),
        out_specs=[SEM_SPEC] * 3 + [HBM_SPEC] * (2 * n) + [pl.BlockSpec(memory_space=pltpu.VMEM)],
        out_shape=[pltpu.SemaphoreType.DMA((7 * n,)), pltpu.SemaphoreType.DMA((7 * n,)), pltpu.SemaphoreType.DMA((n,))]
        + [pltpu.HBM(s.shape, s.dtype) for s in stacks] * 2 + [TOKEN],
        input_output_aliases={i: 3 + i for i in range(2 * n)},
        compiler_params=pltpu.CompilerParams(has_side_effects=EFFECT),
    )(*[_in_hbm(s) for s in stacks], *[_in_hbm(l) for l in lands])
    return n, res[:3], res[3:3 + n], res[3 + n:3 + 2 * n]


def _exchange_wait(name, state):
    n, sems, stacks, lands = state

    def body(*refs):
        ins, outs, sems_in = refs[:n], refs[n:2 * n], refs[2 * n:2 * n + 3]
        for w in range(n):
            local, remote = _exchange_copies(ins, outs, sems_in, w)
            local.wait()
            for cp in remote:
                cp.wait_send()
                cp.wait_recv()

    return _chained(
        body, name=name, in_specs=[HBM_SPEC] * (2 * n) + [SEM_SPEC] * 3, out_specs=[HBM_SPEC] * n,
        out_shape=[pltpu.HBM(l.shape, l.dtype) for l in lands], input_output_aliases={n + i: i for i in range(n)},
        compiler_params=pltpu.CompilerParams(has_side_effects=EFFECT),
    )(*stacks, *lands, *sems)


def _unstack_cols(w):
    return w.transpose(1, 0, 2).reshape(w.shape[1], N_DEV * w.shape[2])


def _stack_cols(w):
    return w.reshape(w.shape[0], N_DEV, w.shape[1] // N_DEV).transpose(1, 0, 2)


def _rope_tables(positions):
    half = MLA_ROPE // 2
    inv_freq = ROPE_THETA ** (-jnp.arange(half, dtype=F32) / half)
    ang = positions.astype(F32)[:, None] * inv_freq
    cos, sin = jnp.cos(ang), jnp.sin(ang)
    t = positions.shape[0]
    cos_t = jnp.concatenate([jnp.ones((t, MLA_NOPE), F32), cos, cos], axis=1)
    sin_t = jnp.concatenate([jnp.zeros((t, MLA_NOPE), F32), -sin, sin], axis=1)
    idx = jnp.arange(MLA_QK)
    partner = jnp.where(idx < MLA_NOPE, -1, jnp.where(idx < MLA_NOPE + half, idx + half, idx - half))
    swap = (idx[:, None] == partner[None, :]).astype(BF16)
    return cos_t, sin_t, swap


def _ffn_fwd(tag, x, gain, wt):
    h = _rms_fwd(tag + "_rms", x, gain, BF16)
    g = _mm_stack_nt_out(tag + "_gate", h, wt[tag + '_w_gate'], BF16)
    u, a = _mm_stack_nt_out(tag + "_up", h, wt[tag + '_w_up'], BF16,
                            fuse=(lambda u_blk, g_blk: (u_blk, g_blk * jax.nn.sigmoid(g_blk) * u_blk), [g]))
    y = _mm_stack_sum(tag + "_down", a, wt[tag + '_w_down'], F32, scale=0.5, res=x)
    return y, (x, h, g, u, a)


def _ffn_bwd_weights(tag, dy, saved, wd, comm):
    x, h, g, u, a = saved
    comm.grads({tag + '_w_down': _mm_stack_tn_right(tag + "_dwd", a, dy, BF16, scale=0.5)})

    def dact(da, g_blk, u_blk):
        sig = jax.nn.sigmoid(g_blk)
        return da * u_blk * sig * (1.0 + g_blk * (1.0 - sig)), da * g_blk * sig

    dg, du = _mm_stack_nt_out(tag + "_da", dy, wd, BF16, scale=0.5, fuse=(dact, [g, u]))
    comm.grads({tag + '_w_gate': _mm_stack_tn_right(tag + "_dwg", dg, h, BF16)})
    comm.grads({tag + '_w_up': _mm_stack_tn_right(tag + "_dwu", du, h, BF16)})
    return dg, du


def _ffn_bwd_input(tag, dy, saved, dgu, gain, wg, wu):
    dg, du = dgu
    dh = _mm_stack_sum(tag + "_dh", (dg, du), (wg, wu), F32)
    return _rms_bwd(tag + "_drms", saved[0], gain, dh, res=dy)


def _local_step(x, mem, positions, sm, comm, target):
    t, d = x.shape
    nm = mem.shape[0]
    gs, gw = {}, {}

    class Weights(dict):
        def __missing__(self, name):
            self.update(comm.weights(next(k for k, names in enumerate(GATHERS) if name in names)))
            return self[name]

    wt = Weights()

    x1, ffn1_saved = _ffn_fwd("ffn1", x, sm['ffn1_norm'], wt)

    h2 = _rms_fwd("mix_rms", x1, sm['mix_norm'], BF16)
    w_in_ref = _unstack_cols(wt['w_in'])
    pieces = []
    for name, n in IN_PAD:
        piece = w_in_ref[:, REF_OFF[name]:REF_OFF[name] + REF_SIZE[name]]
        if n != REF_SIZE[name]:
            piece = jnp.pad(piece, ((0, 0), (0, n - REF_SIZE[name])))
        pieces.append(piece)
    w_in = jnp.concatenate(pieces, axis=1)
    z = _mm2("mix_in", h2, w_in, NN, F32)
    zs = {name: z[:, PAD_OFF[name]:PAD_OFF[name] + n] for name, n in IN_PAD}

    cq = _rms_fwd("mla_q_a_rms", zs['zq'], sm['q_a_norm'], BF16)
    q_raw = _mm_stack_nt_out("mla_q_up", cq, wt['w_q_up'], F32)
    ckv = _rms_fwd("mla_kv_a_rms", zs['zkv'], sm['kv_a_norm'], BF16)
    kv = _mm_stack_out("mla_kv_up", ckv, wt['w_kv_up'], F32)
    zkr = zs['zkr'][:, :MLA_ROPE]
    k_raw = jnp.concatenate([kv[:, :, :MLA_NOPE], jnp.broadcast_to(zkr[None], (MLA_HEADS, t, MLA_ROPE))], axis=2)
    v_mla = kv[:, :, MLA_NOPE:].astype(BF16)
    cos_t, sin_t, swap = _rope_tables(positions)
    q_raw2, k_raw2 = q_raw.reshape(MLA_HEADS * t, MLA_QK), k_raw.reshape(MLA_HEADS * t, MLA_QK)
    rope = (cos_t, sin_t, swap)
    qf = _rms_fwd("mla_q_rms", q_raw2, sm['mla_q_norm'], BF16, rope=rope).reshape(MLA_HEADS, t, MLA_QK)
    kf = _rms_fwd("mla_k_rms", k_raw2, sm['mla_k_norm'], BF16, rope=rope).reshape(MLA_HEADS, t, MLA_QK)
    o_mla = _attn_fwd("mla_attn", qf, kf, v_mla, MLA_QK ** -0.5, True)

    w_g2 = jnp.pad(_unstack_cols(wt['gla_w_gate2']), ((0, LANES - GLA_GATE_RANK), (0, 0)))
    pre = _mm2("gla_gate_pre", zs['zg'], w_g2, NN, F32)
    log_a = _gate_fwd("gla_gate", pre, sm['gla_b_gate'])
    o_gla_raw, states = _gla_fwd("gla_scan", zs['gq'], zs['gk'], zs['gv'], log_a, GLA_HEADS)
    o_gla_n = _rms_fwd("gla_out_rms", o_gla_raw, sm['gla_out_norm'], F32)
    o_gla = _swiglu_fwd("gla_out_gate", zs['zr'], o_gla_n, BF16)

    cat = jnp.concatenate([o_mla, o_gla], axis=1)
    w_out = wt['w_out'].reshape(d, d)
    x2 = _mm2("mix_out", cat, w_out, NN, F32, res=x1)

    w_mq, w_mk, w_mv = (wt[n].reshape(d, MEM_HEADS * MEM_HEAD_DIM) for n in ('mem_w_q', 'mem_w_k', 'mem_w_v'))
    hq = _rms_fwd("mem_attn_rms", x2, sm['mem_attn_norm'], BF16)
    hm = _rms_fwd("mem_rms", mem, sm['mem_norm'], BF16)

    def heads_out(name, a, b, out_dtype):
        m, kk = a.shape
        tm = _tile(m, 512)
        return _mm(name, a, b, (m // tm, MEM_HEADS, 1), ((tm, kk), lambda i, h, k: (i, 0)),
                   ((kk, MEM_HEAD_DIM), lambda i, h, k: (0, h)), ((None, tm, MEM_HEAD_DIM), lambda i, h, k: (h, i, 0)),
                   (MEM_HEADS, m, MEM_HEAD_DIM), out_dtype, NN)

    mq_raw = heads_out("mem_q", hq, w_mq, F32)
    mk_raw = heads_out("mem_k", hm, w_mk, F32)
    mv = heads_out("mem_v", hm, w_mv, BF16)
    mq = _rms_fwd("mem_q_rms", mq_raw.reshape(MEM_HEADS * t, MEM_HEAD_DIM), sm['mem_q_norm'], BF16)
    mk = _rms_fwd("mem_k_rms", mk_raw.reshape(MEM_HEADS * nm, MEM_HEAD_DIM), sm['mem_k_norm'], BF16)
    mq, mk = mq.reshape(MEM_HEADS, t, MEM_HEAD_DIM), mk.reshape(MEM_HEADS, nm, MEM_HEAD_DIM)
    o_mem = _attn_fwd("mem_attn", mq, mk, mv, MEM_HEAD_DIM ** -0.5, False)
    w_mo = wt['mem_w_o']
    mo_cols = w_mo.shape[2]
    tm = _tile(t, 512)
    x3 = _mm("mem_out", o_mem, w_mo, (t // tm, N_DEV, 1), ((tm, o_mem.shape[1]), lambda i, j, k: (i, 0)),
             ((None, o_mem.shape[1], mo_cols), lambda i, j, k: (j, 0, 0)), ((tm, mo_cols), lambda i, j, k: (i, j)),
             (t, d), F32, NN, res=x2)

    y, ffn2_saved = _ffn_fwd("ffn2", x3, sm['ffn2_norm'], wt)
    dy, dy_narrow, loss_lanes = _loss("loss", y, target)

    dgu = _ffn_bwd_weights("ffn2", dy_narrow, ffn2_saved, wt['ffn2_w_down'], comm)
    dx3, gs['ffn2_norm'] = _ffn_bwd_input("ffn2", dy, ffn2_saved, dgu, sm['ffn2_norm'],
                                          wt['ffn2_w_gate'], wt['ffn2_w_up'])

    do_mem = _mm("mem_do", dx3, w_mo, (t // tm, MEM_HEADS, N_DEV), ((tm, mo_cols), lambda i, h, k: (i, k)),
                 ((None, MEM_HEAD_DIM, mo_cols), lambda i, h, k: (k, h, 0)), ((tm, MEM_HEAD_DIM), lambda i, h, k: (i, h)),
                 (t, MEM_HEADS * MEM_HEAD_DIM), BF16, NT)
    tk = _tile(t, 512)
    gw['mem_w_o'] = _mm("mem_dwo", o_mem, dx3, (N_DEV, 1, t // tk), ((tk, o_mem.shape[1]), lambda j, i, k: (k, 0)),
                        ((tk, mo_cols), lambda j, i, k: (k, j)), ((None, o_mem.shape[1], mo_cols), lambda j, i, k: (j, 0, 0)),
                        w_mo.shape, BF16, TN)
    dmq, dmk, dmv = _attn_bwd("mem_dattn", mq, mk, mv, do_mem, MEM_HEAD_DIM ** -0.5, False)
    dmq_raw, gs['mem_q_norm'] = _rms_bwd("mem_q_drms", mq_raw.reshape(MEM_HEADS * t, MEM_HEAD_DIM), sm['mem_q_norm'],
                                         dmq.reshape(MEM_HEADS * t, MEM_HEAD_DIM))
    dmk_raw, gs['mem_k_norm'] = _rms_bwd("mem_k_drms", mk_raw.reshape(MEM_HEADS * nm, MEM_HEAD_DIM), sm['mem_k_norm'],
                                         dmk.reshape(MEM_HEADS * nm, MEM_HEAD_DIM))
    dmq_raw = dmq_raw.reshape(MEM_HEADS, t, MEM_HEAD_DIM)
    dmk_raw = dmk_raw.reshape(MEM_HEADS, nm, MEM_HEAD_DIM)

    def heads_in_nt(name, a, b, res=None):
        m, n = a.shape[1], b.shape[0]
        tm_, tn_ = _tile(m, 512), _tile(n, 1024)
        return _mm(name, a, b, (m // tm_, n // tn_, MEM_HEADS), ((None, tm_, MEM_HEAD_DIM), lambda i, j, k: (k, i, 0)),
                   ((tn_, MEM_HEAD_DIM), lambda i, j, k: (j, k)), ((tm_, tn_), lambda i, j, k: (i, j)), (m, n), F32, NT,
                   None, res)

    def heads_tn(name, a, b):
        m, kp = a.shape
        tm_, tk_ = _tile(kp, 1024), _tile(m, 512)
        return _mm(name, a, b, (kp // tm_, MEM_HEADS, m // tk_), ((tk_, tm_), lambda i, h, k: (k, i)),
                   ((None, tk_, MEM_HEAD_DIM), lambda i, h, k: (h, k, 0)), ((tm_, MEM_HEAD_DIM), lambda i, h, k: (i, h)),
                   (kp, MEM_HEADS * MEM_HEAD_DIM), BF16, TN)

    dhq = heads_in_nt("mem_dhq", dmq_raw, w_mq)
    gw['mem_w_q'] = heads_tn("mem_dwq", hq, dmq_raw).reshape(wt['mem_w_q'].shape)
    dhm = heads_in_nt("mem_dhm_k", dmk_raw, w_mk)
    dhm = heads_in_nt("mem_dhm_v", dmv, w_mv, res=dhm)
    gw['mem_w_k'] = heads_tn("mem_dwk", hm, dmk_raw).reshape(wt['mem_w_k'].shape)
    gw['mem_w_v'] = heads_tn("mem_dwv", hm, dmv).reshape(wt['mem_w_v'].shape)
    _, gs['mem_norm'] = _rms_bwd("mem_drms", mem, sm['mem_norm'], dhm)
    comm.grads({n: gw[n] for n in MEMORY})
    dx2, gs['mem_attn_norm'] = _rms_bwd("mem_attn_drms", x2, sm['mem_attn_norm'], dhq, res=dx3)

    dcat = _mm2("mix_dcat", dx2, w_out, NT, F32)
    comm.grads({'w_out': _mm2("mix_dwout", cat, dx2, TN, BF16, tm=1024, tn=2048, tk=512).reshape(wt['w_out'].shape)})
    do_mla, do_gla = dcat[:, :MLA_HEADS * MLA_V], dcat[:, MLA_HEADS * MLA_V:]

    dzr, dgn = _swiglu_bwd("gla_out_dgate", do_gla, zs['zr'], o_gla_n, F32)
    do_gla_raw, gs['gla_out_norm'] = _rms_bwd("gla_out_drms", o_gla_raw, sm['gla_out_norm'], dgn)
    dgq, dgk, dgv, dlog_a = _gla_bwd("gla_dscan", zs['gq'], zs['gk'], zs['gv'], log_a, states, do_gla_raw, GLA_HEADS)
    dpre, gs['gla_b_gate'] = _gate_bwd("gla_dgate", pre, sm['gla_b_gate'], dlog_a)
    dw_g2 = _mm2("gla_dwgate", zs['zg'], dpre, TN, BF16, tk=512)
    comm.grads({'gla_w_gate2': _stack_cols(dw_g2[:GLA_GATE_RANK])})
    dzg = _mm2("gla_dzg", dpre, w_g2, NT, F32)

    dqf, dkf, dv_mla = _attn_bwd("mla_dattn", qf, kf, v_mla, do_mla, MLA_QK ** -0.5, True)
    dq_raw, gs['mla_q_norm'] = _rms_bwd("mla_q_drms", q_raw2, sm['mla_q_norm'], dqf.reshape(MLA_HEADS * t, MLA_QK), rope=rope)
    dk_raw, gs['mla_k_norm'] = _rms_bwd("mla_k_drms", k_raw2, sm['mla_k_norm'], dkf.reshape(MLA_HEADS * t, MLA_QK), rope=rope)
    dq_raw = dq_raw.reshape(MLA_HEADS, t, MLA_QK)
    dk_raw = dk_raw.reshape(MLA_HEADS, t, MLA_QK)
    dkv = jnp.concatenate([dk_raw[:, :, :MLA_NOPE], dv_mla], axis=2)
    dzkr = jnp.sum(dk_raw[:, :, MLA_NOPE:], axis=0)
    comm.grads({'w_q_up': _mm_stack_tn_right("mla_dwq", dq_raw, cq, BF16),
                'w_kv_up': _mm_stack_tn_left("mla_dwkv", ckv, dkv, BF16)})
    dcq = _mm_stack_sum("mla_dcq", dq_raw, wt['w_q_up'], F32)
    dckv = _mm_stack_nt_sum("mla_dckv", dkv, wt['w_kv_up'], F32)
    dzq, gs['q_a_norm'] = _rms_bwd("mla_q_a_drms", zs['zq'], sm['q_a_norm'], dcq)
    dzkv, gs['kv_a_norm'] = _rms_bwd("mla_kv_a_drms", zs['zkv'], sm['kv_a_norm'], dckv)

    dzs = {'zq': dzq, 'zkv': dzkv, 'gq': dgq, 'gk': dgk, 'gv': dgv, 'zr': dzr,
           'zkr': jnp.pad(dzkr, ((0, 0), (0, LANES - MLA_ROPE))), 'zg': dzg}
    dz = jnp.concatenate([dzs[name].astype(BF16) for name, _ in IN_PAD], axis=1)
    dw_in = _mm2("mix_dwin", h2, dz, TN, BF16, tm=1024, tn=2048, tk=512)
    dw_in_ref = jnp.concatenate([dw_in[:, PAD_OFF[name]:PAD_OFF[name] + n] for name, n in IN_REF], axis=1)
    comm.grads({'w_in': _stack_cols(dw_in_ref)})
    dh2 = _mm2("mix_dh", dz, w_in, NT, F32)
    dx1, dx1_narrow, gs['mix_norm'] = _rms_bwd("mix_drms", x1, sm['mix_norm'], dh2, res=dx2, narrow=True)

    dgu = _ffn_bwd_weights("ffn1", dx1_narrow, ffn1_saved, wt['ffn1_w_down'], comm)
    grad_x, gs['ffn1_norm'] = _ffn_bwd_input("ffn1", dx1, ffn1_saved, dgu, sm['ffn1_norm'],
                                             wt['ffn1_w_gate'], wt['ffn1_w_up'])
    return loss_lanes, grad_x, gs


def _pad_lanes(v):
    n = v.shape[1]
    return jnp.pad(v, ((0, 0), (0, -n % LANES)))


def _pack_small(vals):
    return jnp.concatenate([_pad_lanes(vals[n]) for n in SMALL], axis=1)


def _unpack_small(packed, like):
    out, off = {}, 0
    for n in SMALL:
        size = like[n].shape[1]
        out[n] = packed[:, off:off + size]
        off += size + (-size % LANES)
    return out


def kernel(x, mem, positions, ffn1_norm, ffn1_w_gate, ffn1_w_up, ffn1_w_down, mix_norm, w_in, q_a_norm, w_q_up, kv_a_norm, w_kv_up, mla_q_norm, mla_k_norm, gla_w_gate2, gla_b_gate, gla_out_norm, w_out, mem_attn_norm, mem_norm, mem_w_q, mem_w_k, mem_w_v, mem_w_o, mem_q_norm, mem_k_norm, ffn2_norm, ffn2_w_gate, ffn2_w_up, ffn2_w_down, loss_target, m_ffn1_norm, m_ffn1_w_gate, m_ffn1_w_up, m_ffn1_w_down, m_mix_norm, m_w_in, m_q_a_norm, m_w_q_up, m_kv_a_norm, m_w_kv_up, m_mla_q_norm, m_mla_k_norm, m_gla_w_gate2, m_gla_b_gate, m_gla_out_norm, m_w_out, m_mem_attn_norm, m_mem_norm, m_mem_w_q, m_mem_w_k, m_mem_w_v, m_mem_w_o, m_mem_q_norm, m_mem_k_norm, m_ffn2_norm, m_ffn2_w_gate, m_ffn2_w_up, m_ffn2_w_down, v_ffn1_norm, v_ffn1_w_gate, v_ffn1_w_up, v_ffn1_w_down, v_mix_norm, v_w_in, v_q_a_norm, v_w_q_up, v_kv_a_norm, v_w_kv_up, v_mla_q_norm, v_mla_k_norm, v_gla_w_gate2, v_gla_b_gate, v_gla_out_norm, v_w_out, v_mem_attn_norm, v_mem_norm, v_mem_w_q, v_mem_w_k, v_mem_w_v, v_mem_w_o, v_mem_q_norm, v_mem_k_norm, v_ffn2_norm, v_ffn2_w_gate, v_ffn2_w_up, v_ffn2_w_down):
    inp = dict(locals())
    x, mem, positions, target = inp['x'][0], inp['mem'][0], inp['positions'][0], inp['loss_target'][0]
    sm = {n: inp[n] for n in SMALL}
    out = {}

    def stored(key):
        name = key[2:] if key[:2] in ('m_', 'v_') else key
        return inp[key][0].T if name in TRANSPOSED else inp[key][0]

    def as_given(name, r):
        return r.T[None] if name in TRANSPOSED else r[None]

    class Comm:
        def __init__(self):
            self.gathers = {0: self.start(0)}
            self.forwards, self.exchanges = {}, []

        def start(self, k):
            return _gather_start(f"gather_start_{k}", [stored(n).astype(BF16) for n in GATHERS[k]])

        def forward(self, k):
            if k not in self.forwards:
                self.forwards[k] = _gather_mid(f"gather_mid_{k}", self.gathers[k])
                self.gathers.update({nxt: self.start(nxt) for nxt in NEXT_GATHERS.get(k, [])})

        def weights(self, k):
            self.forward(k)
            if k in EARLY_FORWARD:
                self.forward(EARLY_FORWARD[k])
            return dict(zip(GATHERS[k], _gather_end(f"gather_end_{k}", self.forwards[k])))

        def grads(self, stacks):
            names = list(stacks)
            self.exchanges.append((names, _exchange_start("exchange_start_" + names[0], [stacks[n] for n in names])))

        def update(self, count):
            todo, self.exchanges = self.exchanges[:count], self.exchanges[count:]
            for names, state in todo:
                for n, p in zip(names, _exchange_wait("exchange_wait_" + names[0], state)):
                    res = _adamw("adamw_" + n, stored(n), stored('m_' + n), stored('v_' + n), p)
                    for kind, r in zip(('grad_', 'delta_', 'new_m_', 'new_v_'), res):
                        out[kind + n] = as_given(n, r)

    _Chain.last = None
    comm = Comm()
    loss_lanes, grad_x, gs = _local_step(x, mem, positions, sm, comm, target)
    out['loss'] = lax.psum(jnp.sum(loss_lanes), ("x", "y", "c"))
    out['grad_x'] = grad_x[None]

    comm.update(len(comm.exchanges) - 3)
    small_parts = _all_gather("gather_small", [_pack_small(gs)])[0]
    res = _adamw("adamw_small", _pack_small(sm), _pack_small({n: inp['m_' + n] for n in SMALL}),
                 _pack_small({n: inp['v_' + n] for n in SMALL}), small_parts)
    for kind, r in zip(('grad_', 'delta_', 'new_m_', 'new_v_'), res):
        for n, val in _unpack_small(r, sm).items():
            out[kind + n] = val

    comm.update(3)

    names = ['loss', 'grad_x'] + [k + n for k in ('grad_', 'delta_', 'new_m_', 'new_v_') for n in WEIGHTS]
    return tuple(out[n] for n in names)
```

```python
import functools
import math

import jax
import jax.numpy as jnp
from jax import lax
from jax.experimental import pallas as pl
from jax.experimental.pallas import tpu as pltpu

F32 = jnp.float32
BF16 = jnp.bfloat16

N_DEV = 8
EPS = 1e-6
CHUNK = 64
MLA_HEADS, MLA_NOPE, MLA_ROPE, MLA_V = 8, 128, 64, 128
MLA_QK = MLA_NOPE + MLA_ROPE
MLA_Q_RANK, MLA_KV_RANK = 512, 256
ROPE_THETA = 10000.0
GLA_HEADS, GLA_DK, GLA_DV, GLA_GATE_RANK = 4, 128, 256, 16
GLA_TAU = 16.0
MEM_HEADS, MEM_HEAD_DIM = 4, 128
ADAM_LR, ADAM_B1, ADAM_B2, ADAM_EPS, ADAM_WD, ADAM_STEP = 0.001, 0.9, 0.999, 1e-08, 0.01, 10

V7X_VMEM_BYTES = 64 * 1024 * 1024
LANES = 128

NN = (((1,), (0,)), ((), ()))
NT = (((1,), (1,)), ((), ()))
TN = (((0,), (0,)), ((), ()))

WEIGHTS = ['ffn1_norm', 'ffn1_w_gate', 'ffn1_w_up', 'ffn1_w_down', 'mix_norm', 'w_in', 'q_a_norm', 'w_q_up',
           'kv_a_norm', 'w_kv_up', 'mla_q_norm', 'mla_k_norm', 'gla_w_gate2', 'gla_b_gate', 'gla_out_norm', 'w_out',
           'mem_attn_norm', 'mem_norm', 'mem_w_q', 'mem_w_k', 'mem_w_v', 'mem_w_o', 'mem_q_norm', 'mem_k_norm',
           'ffn2_norm', 'ffn2_w_gate', 'ffn2_w_up', 'ffn2_w_down']
SMALL = ['ffn1_norm', 'mix_norm', 'q_a_norm', 'kv_a_norm', 'mla_q_norm', 'mla_k_norm', 'gla_b_gate', 'gla_out_norm',
         'mem_attn_norm', 'mem_norm', 'mem_q_norm', 'mem_k_norm', 'ffn2_norm']
MIXER = ['w_in', 'w_q_up', 'w_kv_up', 'gla_w_gate2', 'w_out']
MEMORY = ['mem_w_q', 'mem_w_k', 'mem_w_v', 'mem_w_o']
TRANSPOSED = ['ffn1_w_gate', 'ffn1_w_up', 'ffn2_w_gate', 'ffn2_w_up', 'w_q_up']
GATHERS = [['ffn1_w_gate'], ['ffn1_w_up'], ['ffn1_w_down'], MIXER, MEMORY, ['ffn2_w_gate', 'ffn2_w_up', 'ffn2_w_down']]
NEXT_GATHERS = {0: [1], 1: [2], 2: [3], 3: [4, 5]}
EARLY_FORWARD = {4: 5}

IN_REF = [('zq', 512), ('zkv', 256), ('zkr', 64), ('gq', 512), ('gk', 512), ('gv', 1024), ('zg', 16), ('zr', 1024)]
IN_PAD = [('zq', 512), ('zkv', 256), ('gq', 512), ('gk', 512), ('gv', 1024), ('zr', 1024), ('zkr', 128), ('zg', 128)]
IN_WIDTH = sum(n for _, n in IN_REF)
IN_PAD_WIDTH = sum(n for _, n in IN_PAD)


def _offsets(layout):
    out, off = {}, 0
    for name, n in layout:
        out[name] = off
        off += n
    return out


REF_OFF, PAD_OFF = _offsets(IN_REF), _offsets(IN_PAD)
REF_SIZE = dict(IN_REF)


def _tile(n, pref):
    return pref if n % pref == 0 else n


def _block_bytes(blk, dtype):
    dims = [d for d in blk if d is not None]
    if len(dims) >= 1:
        dims[-1] = -(-dims[-1] // LANES) * LANES
    return math.prod(dims) * jnp.dtype(dtype).itemsize


def _vmem_limit(pipelined_bytes, resident_bytes=0):
    need = 2 * pipelined_bytes + resident_bytes + (8 << 20)
    return int(min(max(need, 32 << 20), V7X_VMEM_BYTES - (6 << 20)))


class _Chain:
    last = None


def _chained(body, *, in_specs, link=0, **kwargs):
    def call(*operands):
        dep = _Chain.last
        if dep is not None and any(o is dep for o in operands):
            dep = None
        if dep is None:
            res = pl.pallas_call(body, in_specs=in_specs, **kwargs)(*operands)
        else:
            n = len(operands)

            def chained_body(*refs):
                body(*refs[:n], *refs[n + 1:])

            res = pl.pallas_call(chained_body, in_specs=list(in_specs) + [pl.BlockSpec(memory_space=pl.ANY)],
                                 **kwargs)(*operands, dep)
        _Chain.last = res[link] if isinstance(res, (list, tuple)) else res
        return res

    return call


def _row_tile(rows, width, elements=1 << 18):
    lanes = -(-width // LANES) * LANES
    return _tile(rows, max(256, 1 << int(math.log2(max(1, elements // lanes)))))


def _rowwise_params(*semantics):
    return pltpu.CompilerParams(dimension_semantics=semantics, vmem_limit_bytes=48 << 20)


def _mm(name, a, b, grid, a_spec, b_spec, o_spec, out_shape, out_dtype, dims, scale=None, res=None, fuse=None):
    nk = grid[2]
    o_blk, o_map = o_spec
    acc_shape = tuple(d for d in o_blk if d is not None)
    extras = [res] if res is not None else (list(fuse[1]) if fuse else [])
    n_out = 2 if fuse else 1
    a_list, b_list = (list(a), list(b)) if isinstance(a, (tuple, list)) else ([a], [b])
    a_specs = a_spec if isinstance(a_spec, list) else [a_spec] * len(a_list)
    b_specs = b_spec if isinstance(b_spec, list) else [b_spec] * len(b_list)
    n_in = 2 * len(a_list)

    def body(*refs):
        a_refs, b_refs = refs[:n_in // 2], refs[n_in // 2:n_in]
        extra_refs = refs[n_in:n_in + len(extras)]
        out_refs = refs[n_in + len(extras):n_in + len(extras) + n_out]
        rest = refs[n_in + len(extras) + n_out:]
        r_ref = extra_refs[0] if res is not None else None
        o_ref = out_refs[0]

        def product():
            return sum(lax.dot_general(a_ref[...].astype(BF16), b_ref[...].astype(BF16), dims, preferred_element_type=F32)
                       for a_ref, b_ref in zip(a_refs, b_refs))

        def finish(r):
            if scale is not None:
                r = r * scale
            if r_ref is not None:
                r = r + r_ref[...].astype(F32)
            if fuse:
                for ref, val in zip(out_refs, fuse[0](r, *[e[...].astype(F32) for e in extra_refs])):
                    ref[...] = val.astype(ref.dtype)
            else:
                o_ref[...] = r.astype(o_ref.dtype)

        if nk == 1:
            finish(product())
        else:
            acc = rest[0]
            k = pl.program_id(2)

            @pl.when(k == 0)
            def _():
                acc[...] = product()

            @pl.when(k > 0)
            def _():
                acc[...] += product()

            @pl.when(k == nk - 1)
            def _():
                finish(acc[...])

    in_specs = [pl.BlockSpec(*spec) for spec in a_specs + b_specs]
    operands = a_list + b_list
    piped = sum(_block_bytes(spec[0], v.dtype) for spec, v in zip(a_specs + b_specs, operands))
    piped += _block_bytes(o_blk, out_dtype)
    for extra in extras:
        in_specs.append(pl.BlockSpec(o_blk, o_map))
        operands.append(extra)
        piped += _block_bytes(o_blk, extra.dtype)
    piped += (n_out - 1) * _block_bytes(o_blk, out_dtype)
    scratch = [pltpu.VMEM(acc_shape, F32)] if nk > 1 else []
    out_spec, out_sds = pl.BlockSpec(o_blk, o_map), jax.ShapeDtypeStruct(out_shape, out_dtype)
    return _chained(
        body, name=name, grid=grid, in_specs=in_specs, out_specs=(out_spec,) * n_out if fuse else out_spec,
        out_shape=(out_sds,) * n_out if fuse else out_sds, scratch_shapes=scratch,
        compiler_params=pltpu.CompilerParams(
            dimension_semantics=("parallel", "parallel", "arbitrary"),
            vmem_limit_bytes=_vmem_limit(piped, 3 * _block_bytes(acc_shape, F32))),
    )(*operands)


def _mm2(name, a, b, dims, out_dtype, tm=512, tn=1024, tk=2048, scale=None, res=None):
    if dims is NN:
        (m, kk), n = a.shape, b.shape[1]
    elif dims is NT:
        (m, kk), n = a.shape, b.shape[0]
    else:
        (kk, m), n = a.shape, b.shape[1]
    tm, tn, tk = _tile(m, tm), _tile(n, tn), _tile(kk, tk)
    a_spec = ((tk, tm), lambda i, j, k: (k, i)) if dims is TN else ((tm, tk), lambda i, j, k: (i, k))
    b_spec = ((tn, tk), lambda i, j, k: (j, k)) if dims is NT else ((tk, tn), lambda i, j, k: (k, j))
    return _mm(name, a, b, (m // tm, n // tn, kk // tk), a_spec, b_spec, ((tm, tn), lambda i, j, k: (i, j)),
               (m, n), out_dtype, dims, scale, res)


def _mm_stack_out(name, a, b, out_dtype, tm=512, tk=2048):
    (m, kk), (nj, _, n) = a.shape, b.shape
    tm, tk = _tile(m, tm), _tile(kk, tk)
    return _mm(name, a, b, (nj, m // tm, kk // tk), ((tm, tk), lambda j, i, k: (i, k)),
               ((None, tk, n), lambda j, i, k: (j, k, 0)), ((None, tm, n), lambda j, i, k: (j, i, 0)),
               (nj, m, n), out_dtype, NN)


def _mm_stack_nt_out(name, a, b, out_dtype, scale=None, fuse=None, tm=512, tk=2048):
    (m, kk), (nj, n, _) = a.shape, b.shape
    tm, tk = _tile(m, tm), _tile(kk, tk)
    return _mm(name, a, b, (nj, m // tm, kk // tk), ((tm, tk), lambda j, i, k: (i, k)),
               ((None, n, tk), lambda j, i, k: (j, 0, k)), ((None, tm, n), lambda j, i, k: (j, i, 0)),
               (nj, m, n), out_dtype, NT, scale, fuse=fuse)


def _mm_stack_sum(name, a, b, out_dtype, scale=None, res=None, tm=512, tn=1024):
    (nj, m, f), n = (a[0] if isinstance(a, tuple) else a).shape, (b[0] if isinstance(b, tuple) else b).shape[2]
    tm, tn = _tile(m, tm), _tile(n, tn)
    a_spec = ((None, tm, f), lambda i, j, k: (k, i, 0))
    b_spec = ((None, f, tn), lambda i, j, k: (k, 0, j))
    if not isinstance(a, tuple) and nj % 2 == 0:
        nj, a, b = nj // 2, (a, a), (b, b)
        a_spec = [((None, tm, f), lambda i, j, k: (2 * k, i, 0)), ((None, tm, f), lambda i, j, k: (2 * k + 1, i, 0))]
        b_spec = [((None, f, tn), lambda i, j, k: (2 * k, 0, j)), ((None, f, tn), lambda i, j, k: (2 * k + 1, 0, j))]
    return _mm(name, a, b, (m // tm, n // tn, nj), a_spec, b_spec, ((tm, tn), lambda i, j, k: (i, j)),
               (m, n), out_dtype, NN, scale, res)


def _mm_stack_nt_sum(name, a, b, out_dtype, res=None, tm=512, tn=1024):
    (nj, m, f), n = a.shape, b.shape[1]
    tm, tn = _tile(m, tm), _tile(n, tn)
    return _mm(name, a, b, (m // tm, n // tn, nj), ((None, tm, f), lambda i, j, k: (k, i, 0)),
               ((None, tn, f), lambda i, j, k: (k, j, 0)), ((tm, tn), lambda i, j, k: (i, j)),
               (m, n), out_dtype, NT, None, res)


def _mm_stack_tn_left(name, a, b, out_dtype, tm=1024, tk=512):
    (m, kp), (nj, _, n) = a.shape, b.shape
    tm, tk = _tile(kp, tm), _tile(m, tk)
    return _mm(name, a, b, (nj, kp // tm, m // tk), ((tk, tm), lambda j, i, k: (k, i)),
               ((None, tk, n), lambda j, i, k: (j, k, 0)), ((None, tm, n), lambda j, i, k: (j, i, 0)),
               (nj, kp, n), out_dtype, TN)


def _mm_stack_tn_right(name, a, b, out_dtype, scale=None, tn=2048, tk=512):
    (nj, m, f), n = a.shape, b.shape[1]
    tn, tk = _tile(n, tn), _tile(m, tk)
    return _mm(name, a, b, (nj, n // tn, m // tk), ((None, tk, f), lambda j, i, k: (j, k, 0)),
               ((tk, tn), lambda j, i, k: (k, i)), ((None, f, tn), lambda j, i, k: (j, 0, i)),
               (nj, f, n), out_dtype, TN, scale)


def _split3(x):
    hi = x.astype(BF16)
    r1 = x - hi.astype(F32)
    mid = r1.astype(BF16)
    lo = (r1 - mid.astype(F32)).astype(BF16)
    return hi, mid, lo


def _swap_halves(x, swap):
    return sum(jnp.dot(part, swap, preferred_element_type=F32) for part in _split3(x))


def _rope_specs(rope, tm):
    cos_t, _, swap = rope
    nt = cos_t.shape[0] // tm
    tab = pl.BlockSpec((tm, cos_t.shape[1]), lambda i, c: (i % nt, 0))
    return [tab, tab, pl.BlockSpec(swap.shape, lambda i, c: (0, 0))]


def _rms_fwd(name, x, g, out_dtype, rope=None):
    rows, cols = x.shape
    d = g.shape[1]
    tm = _row_tile(rows if rope is None else rope[0].shape[0], d)

    def body(x_ref, g_ref, *refs):
        xf = x_ref[...].astype(F32)
        r = lax.rsqrt(jnp.mean(xf * xf, axis=-1, keepdims=True) + EPS)
        y = xf * r * g_ref[...]
        if rope is not None:
            c_ref, s_ref, p_ref = refs[:3]
            y = y * c_ref[...] + _swap_halves(y, p_ref[...]) * s_ref[...]
        refs[-1][...] = y.astype(refs[-1].dtype)

    return _chained(
        body, name=name, grid=(rows // tm, cols // d),
        in_specs=[pl.BlockSpec((tm, d), lambda i, c: (i, c)), pl.BlockSpec((1, d), lambda i, c: (0, 0))]
        + (_rope_specs(rope, tm) if rope is not None else []),
        out_specs=pl.BlockSpec((tm, d), lambda i, c: (i, c)),
        out_shape=jax.ShapeDtypeStruct((rows, cols), out_dtype),
        compiler_params=_rowwise_params("parallel", "parallel"),
    )(x, g, *(rope or ()))


def _rms_bwd(name, x, g, dy, res=None, rope=None, narrow=False):
    rows, cols = x.shape
    d = g.shape[1]
    tm = _row_tile(rows if rope is None else rope[0].shape[0], d)

    def body(*refs):
        x_ref, g_ref, dy_ref = refs[:3]
        n_out = 3 if narrow else 2
        dx_ref, dg_ref = refs[-n_out], refs[-1]
        r_ref = refs[3] if res is not None else None
        xf = x_ref[...].astype(F32)
        r = lax.rsqrt(jnp.mean(xf * xf, axis=-1, keepdims=True) + EPS)
        xhat = xf * r
        dyf = dy_ref[...].astype(F32)
        if rope is not None:
            c_ref, s_ref, p_ref = refs[-n_out - 3:-n_out]
            dyf = dyf * c_ref[...] + _swap_halves(dyf * s_ref[...], p_ref[...])

        @pl.when((pl.program_id(0) == 0) & (pl.program_id(1) == 0))
        def _():
            dg_ref[...] = jnp.zeros_like(dg_ref)

        dg_ref[...] += jnp.sum(dyf * xhat, axis=0, keepdims=True)
        dxh = dyf * g_ref[...]
        dx = r * (dxh - xhat * jnp.mean(dxh * xhat, axis=-1, keepdims=True))
        if r_ref is not None:
            dx = dx + r_ref[...].astype(F32)
        dx_ref[...] = dx
        if narrow:
            refs[-2][...] = dx.astype(BF16)

    blk = pl.BlockSpec((tm, d), lambda i, c: (i, c))
    in_specs = [blk, pl.BlockSpec((1, d), lambda i, c: (0, 0)), blk]
    operands = [x, g, dy]
    if res is not None:
        in_specs.append(blk)
        operands.append(res)
    if rope is not None:
        in_specs += _rope_specs(rope, tm)
        operands += list(rope)
    wide = [(blk, jax.ShapeDtypeStruct((rows, cols), F32))] + ([(blk, jax.ShapeDtypeStruct((rows, cols), BF16))] if narrow else [])
    outs = wide + [(pl.BlockSpec((1, d), lambda i, c: (0, 0)), jax.ShapeDtypeStruct((1, d), F32))]
    return _chained(
        body, name=name, grid=(rows // tm, cols // d), in_specs=in_specs,
        out_specs=tuple(spec for spec, _ in outs), out_shape=tuple(sds for _, sds in outs),
        compiler_params=_rowwise_params("arbitrary", "arbitrary"),
    )(*operands)


def _swiglu_fwd(name, g, u, out_dtype, tm=256):
    rows, cols = g.shape
    tm = _tile(rows, tm)

    def body(g_ref, u_ref, o_ref):
        gf = g_ref[...].astype(F32)
        o_ref[...] = (gf * jax.nn.sigmoid(gf) * u_ref[...].astype(F32)).astype(o_ref.dtype)

    blk = pl.BlockSpec((tm, cols), lambda i: (i, 0))
    return _chained(
        body, name=name, grid=(rows // tm,), in_specs=[blk, blk], out_specs=blk,
        out_shape=jax.ShapeDtypeStruct((rows, cols), out_dtype),
        compiler_params=_rowwise_params("parallel"),
    )(g, u)


def _swiglu_bwd(name, da, g, u, out_dtype, tm=256):
    rows, cols = g.shape
    tm = _tile(rows, tm)

    def body(da_ref, g_ref, u_ref, dg_ref, du_ref):
        gf = g_ref[...].astype(F32)
        daf = da_ref[...].astype(F32)
        sig = jax.nn.sigmoid(gf)
        du_ref[...] = (daf * gf * sig).astype(du_ref.dtype)
        dg_ref[...] = (daf * u_ref[...].astype(F32) * sig * (1.0 + gf * (1.0 - sig))).astype(dg_ref.dtype)

    blk = pl.BlockSpec((tm, cols), lambda i: (i, 0))
    sds = jax.ShapeDtypeStruct((rows, cols), out_dtype)
    return _chained(
        body, name=name, grid=(rows // tm,), in_specs=[blk, blk, blk], out_specs=(blk, blk), out_shape=(sds, sds),
        compiler_params=_rowwise_params("parallel"),
    )(da, g, u)


def _gate_fwd(name, pre, bias, tm=256):
    rows, cols = pre.shape
    tm = _tile(rows, tm)

    def body(p_ref, b_ref, o_ref):
        z = p_ref[...] + b_ref[...]
        o_ref[...] = (jnp.minimum(z, 0.0) - jnp.log(1.0 + jnp.exp(-jnp.abs(z)))) * (1.0 / GLA_TAU)

    blk = pl.BlockSpec((tm, cols), lambda i: (i, 0))
    return _chained(
        body, name=name, grid=(rows // tm,), in_specs=[blk, pl.BlockSpec((1, cols), lambda i: (0, 0))], out_specs=blk,
        out_shape=jax.ShapeDtypeStruct((rows, cols), F32),
        compiler_params=_rowwise_params("parallel"),
    )(pre, bias)


def _gate_bwd(name, pre, bias, dla, tm=256):
    rows, cols = pre.shape
    tm = _tile(rows, tm)

    def body(p_ref, b_ref, d_ref, dp_ref, db_ref):
        z = p_ref[...] + b_ref[...]
        dp = d_ref[...] * (1.0 / GLA_TAU) / (1.0 + jnp.exp(z))
        dp_ref[...] = dp

        @pl.when(pl.program_id(0) == 0)
        def _():
            db_ref[...] = jnp.zeros_like(db_ref)

        db_ref[...] += jnp.sum(dp, axis=0, keepdims=True)

    blk = pl.BlockSpec((tm, cols), lambda i: (i, 0))
    row = pl.BlockSpec((1, cols), lambda i: (0, 0))
    return _chained(
        body, name=name, grid=(rows // tm,), in_specs=[blk, row, blk], out_specs=(blk, row),
        out_shape=(jax.ShapeDtypeStruct((rows, cols), F32), jax.ShapeDtypeStruct((1, cols), F32)),
        compiler_params=_rowwise_params("arbitrary"),
    )(pre, bias, dla)


def _loss(name, y, target, tm=256):
    rows, d = y.shape
    tm = _tile(rows, tm)

    def body(y_ref, t_ref, dy_ref, dyb_ref, l_ref):
        err = y_ref[...] - t_ref[...]
        dy_ref[...] = err * (1.0 / d)
        dyb_ref[...] = (err * (1.0 / d)).astype(BF16)

        @pl.when(pl.program_id(0) == 0)
        def _():
            l_ref[...] = jnp.zeros_like(l_ref)

        sq = (err * err).reshape(tm // 8, 8, d)
        l_ref[...] += jnp.sum(sq, axis=0) * (0.5 / d)

    blk = pl.BlockSpec((tm, d), lambda i: (i, 0))
    return _chained(
        body, name=name, grid=(rows // tm,), in_specs=[blk, blk],
        out_specs=(blk, blk, pl.BlockSpec((8, d), lambda i: (0, 0))),
        out_shape=(jax.ShapeDtypeStruct((rows, d), F32), jax.ShapeDtypeStruct((rows, d), BF16),
                   jax.ShapeDtypeStruct((8, d), F32)),
        compiler_params=_rowwise_params("arbitrary"),
    )(y, target)


def _per_query_block(causal, nq, tq, tk, inner):
    if not causal:
        inner(0, tk)
        return
    for block in range(nq):
        pl.when(pl.program_id(1) == block)(functools.partial(inner, block * tq, (block + 1) * tq))


def _scores(q, k, scale, causal, q0):
    s = lax.dot_general(q, k, NT, preferred_element_type=F32) * scale
    if causal:
        qc = (q0 + lax.broadcasted_iota(jnp.int32, s.shape, 0)) // CHUNK
        kc = lax.broadcasted_iota(jnp.int32, s.shape, 1) // CHUNK
        s = jnp.where(kc <= qc, s, -1e30)
    e = jnp.exp(s - jnp.max(s, axis=-1, keepdims=True))
    return e, jnp.sum(e, axis=-1, keepdims=True)


def _attn_fwd(name, q, k, v, scale, causal, tq=256):
    nh, t, dk = q.shape
    tk, dv = k.shape[1], v.shape[2]
    tq = _tile(t, tq)

    def body(q_ref, k_ref, v_ref, o_ref):
        def inner(q0, kv):
            e, l = _scores(q_ref[...], k_ref[0:kv, :], scale, causal, q0)
            o = jnp.dot(e.astype(BF16), v_ref[0:kv, :], preferred_element_type=F32)
            o_ref[...] = (o / l).astype(o_ref.dtype)

        _per_query_block(causal, t // tq, tq, tk, inner)

    return _chained(
        body, name=name, grid=(nh, t // tq),
        in_specs=[pl.BlockSpec((None, tq, dk), lambda h, i: (h, i, 0)), pl.BlockSpec((None, tk, dk), lambda h, i: (h, 0, 0)),
                  pl.BlockSpec((None, tk, dv), lambda h, i: (h, 0, 0))],
        out_specs=pl.BlockSpec((tq, dv), lambda h, i: (i, h)),
        out_shape=jax.ShapeDtypeStruct((t, nh * dv), BF16),
        compiler_params=pltpu.CompilerParams(dimension_semantics=("parallel", "parallel"),
                                             vmem_limit_bytes=_vmem_limit(0, 6 * tq * tk * 4)),
    )(q, k, v)


def _attn_bwd(name, q, k, v, do, scale, causal, tq=256):
    nh, t, dk = q.shape
    tk, dv = k.shape[1], v.shape[2]
    tq = _tile(t, tq)

    def body(q_ref, k_ref, v_ref, do_ref, dq_ref, dk_ref, dv_ref):
        @pl.when(pl.program_id(1) == 0)
        def _():
            dk_ref[...] = jnp.zeros_like(dk_ref)
            dv_ref[...] = jnp.zeros_like(dv_ref)

        def inner(q0, kv):
            qb, kb = q_ref[...], k_ref[0:kv, :]
            e, l = _scores(qb, kb, scale, causal, q0)
            p = e / l
            dob = do_ref[...].astype(BF16)
            dp = lax.dot_general(dob, v_ref[0:kv, :], NT, preferred_element_type=F32)
            ds = (p * (dp - jnp.sum(p * dp, axis=-1, keepdims=True)) * scale).astype(BF16)
            dq_ref[...] = jnp.dot(ds, kb, preferred_element_type=F32)
            dk_ref[0:kv, :] += lax.dot_general(ds, qb, TN, preferred_element_type=F32)
            dv_ref[0:kv, :] += lax.dot_general(p.astype(BF16), dob, TN, preferred_element_type=F32)

        _per_query_block(causal, t // tq, tq, tk, inner)

    return _chained(
        body, name=name, grid=(nh, t // tq),
        in_specs=[pl.BlockSpec((None, tq, dk), lambda h, i: (h, i, 0)), pl.BlockSpec((None, tk, dk), lambda h, i: (h, 0, 0)),
                  pl.BlockSpec((None, tk, dv), lambda h, i: (h, 0, 0)), pl.BlockSpec((tq, dv), lambda h, i: (i, h))],
        out_specs=(pl.BlockSpec((None, tq, dk), lambda h, i: (h, i, 0)), pl.BlockSpec((None, tk, dk), lambda h, i: (h, 0, 0)),
                   pl.BlockSpec((None, tk, dv), lambda h, i: (h, 0, 0))),
        out_shape=(jax.ShapeDtypeStruct((nh, t, dk), F32), jax.ShapeDtypeStruct((nh, tk, dk), F32),
                   jax.ShapeDtypeStruct((nh, tk, dv), F32)),
        compiler_params=pltpu.CompilerParams(dimension_semantics=("parallel", "arbitrary"),
                                             vmem_limit_bytes=_vmem_limit(0, 10 * tq * tk * 4)),
    )(q, k, v, do)


def _tri(lower):
    r = lax.broadcasted_iota(jnp.int32, (CHUNK, CHUNK), 0)
    c = lax.broadcasted_iota(jnp.int32, (CHUNK, CHUNK), 1)
    return jnp.where((c <= r) if lower else (c >= r), 1.0, 0.0).astype(BF16)


def _tri_dot(tri, x):
    return sum(jnp.dot(tri, part, preferred_element_type=F32) for part in _split3(x))


def _gla_fwd(name, q, k, v, la, nh):
    t = q.shape[0]
    dk, dv = q.shape[1] // nh, v.shape[1] // nh
    nc = t // CHUNK

    def body(q_ref, k_ref, v_ref, g_ref, o_ref, st_ref, state):
        @pl.when(pl.program_id(0) == 0)
        def _():
            state[...] = jnp.zeros_like(state)

        g = g_ref[...]
        b = _tri_dot(_tri(True), g)
        b_end = jnp.sum(g, axis=0, keepdims=True)
        k_dec = (k_ref[...] * jnp.exp(b_end - b)).astype(BF16)
        decay = jnp.exp(b_end)
        qc = (q_ref[...] * (dk ** -0.5)).astype(BF16)
        vb = v_ref[...].astype(BF16)
        for h in range(nh):
            ks, vs = slice(h * dk, (h + 1) * dk), slice(h * dv, (h + 1) * dv)
            u_t = lax.dot_general(vb[:, vs], k_dec[:, ks], TN, preferred_element_type=F32)
            new = state[h] * decay[:, ks] + u_t
            state[h] = new
            st_ref[h] = new
            o_ref[:, vs] = lax.dot_general(qc[:, ks], new.astype(BF16), NT, preferred_element_type=F32)

    kblk = pl.BlockSpec((CHUNK, nh * dk), lambda n: (n, 0))
    vblk = pl.BlockSpec((CHUNK, nh * dv), lambda n: (n, 0))
    return _chained(
        body, name=name, grid=(nc,), in_specs=[kblk, kblk, vblk, kblk],
        out_specs=(vblk, pl.BlockSpec((nh, None, dv, dk), lambda n: (0, n, 0, 0))),
        out_shape=(jax.ShapeDtypeStruct((t, nh * dv), F32), jax.ShapeDtypeStruct((nh, nc, dv, dk), F32)),
        scratch_shapes=[pltpu.VMEM((nh, dv, dk), F32)],
        compiler_params=_rowwise_params("arbitrary"),
    )(q, k, v, la)


def _gla_bwd(name, q, k, v, la, states, do, nh):
    t = q.shape[0]
    dk, dv = q.shape[1] // nh, v.shape[1] // nh
    nc = t // CHUNK
    scale = dk ** -0.5

    def body(q_ref, k_ref, v_ref, g_ref, do_ref, st_ref, sp_ref, dq_ref, dk_ref, dv_ref, dg_ref, carry):
        i = pl.program_id(0)

        @pl.when(i == 0)
        def _():
            carry[...] = jnp.zeros_like(carry)

        g = g_ref[...]
        b = _tri_dot(_tri(True), g)
        b_end = jnp.sum(g, axis=0, keepdims=True)
        w = jnp.exp(b_end - b)
        decay = jnp.exp(b_end)
        k_dec = k_ref[...] * w
        k_decb = k_dec.astype(BF16)
        qc = (q_ref[...] * scale).astype(BF16)
        dob = do_ref[...].astype(BF16)
        vb = v_ref[...].astype(BF16)
        dk_dec, ddecay = [], []
        for h in range(nh):
            ks, vs = slice(h * dk, (h + 1) * dk), slice(h * dv, (h + 1) * dv)
            dq_ref[:, ks] = jnp.dot(dob[:, vs], st_ref[h].astype(BF16), preferred_element_type=F32) * scale
            g_t = carry[h] + lax.dot_general(dob[:, vs], qc[:, ks], TN, preferred_element_type=F32)
            g_tb = g_t.astype(BF16)
            dk_dec.append(jnp.dot(vb[:, vs], g_tb, preferred_element_type=F32))
            dv_ref[:, vs] = lax.dot_general(k_decb[:, ks], g_tb, NT, preferred_element_type=F32)
            prev = jnp.where(i < nc - 1, sp_ref[h], 0.0)
            ddecay.append(jnp.sum(g_t * prev, axis=0, keepdims=True))
            carry[h] = g_t * decay[:, ks]
        dk_dec = jnp.concatenate(dk_dec, axis=1)
        dk_ref[...] = dk_dec * w
        e = dk_dec * k_dec
        db_end = jnp.sum(e, axis=0, keepdims=True) + jnp.concatenate(ddecay, axis=1) * decay
        dg_ref[...] = _tri_dot(_tri(False), -e) + db_end

    kblk = pl.BlockSpec((CHUNK, nh * dk), lambda i: (nc - 1 - i, 0))
    vblk = pl.BlockSpec((CHUNK, nh * dv), lambda i: (nc - 1 - i, 0))
    ksds = jax.ShapeDtypeStruct((t, nh * dk), F32)
    return _chained(
        body, name=name, grid=(nc,),
        in_specs=[kblk, kblk, vblk, kblk, vblk,
                  pl.BlockSpec((nh, None, dv, dk), lambda i: (0, nc - 1 - i, 0, 0)),
                  pl.BlockSpec((nh, None, dv, dk), lambda i: (0, jnp.maximum(nc - 2 - i, 0), 0, 0))],
        out_specs=(kblk, kblk, vblk, kblk),
        out_shape=(ksds, ksds, jax.ShapeDtypeStruct((t, nh * dv), F32), ksds),
        scratch_shapes=[pltpu.VMEM((nh, dv, dk), F32)],
        compiler_params=_rowwise_params("arbitrary"),
    )(q, k, v, la, do, states, states)


def _adamw(name, w, m, v, parts):
    rows, cols = w.shape
    tm = 1 << int(math.log2(max(8, (1 << 18) // (-(-cols // LANES) * LANES))))
    while rows % tm and tm > 8:
        tm //= 2
    tm = _tile(rows, tm)

    def body(w_ref, m_ref, v_ref, p_ref, g_ref, d_ref, nm_ref, nv_ref):
        g = p_ref[0].astype(F32)
        for s in range(1, N_DEV):
            g = g + p_ref[s].astype(F32)
        m_new = ADAM_B1 * m_ref[...] + (1.0 - ADAM_B1) * g
        v_new = ADAM_B2 * v_ref[...] + (1.0 - ADAM_B2) * jnp.square(g)
        m_hat = m_new / (1.0 - ADAM_B1 ** ADAM_STEP)
        v_hat = v_new / (1.0 - ADAM_B2 ** ADAM_STEP)
        g_ref[...] = g
        d_ref[...] = -ADAM_LR * (m_hat / (jnp.sqrt(v_hat) + ADAM_EPS) + ADAM_WD * w_ref[...])
        nm_ref[...] = m_new
        nv_ref[...] = v_new

    blk = pl.BlockSpec((tm, cols), lambda i: (i, 0))
    sds = jax.ShapeDtypeStruct((rows, cols), F32)
    return _chained(
        body, name=name, grid=(rows // tm,),
        in_specs=[blk, blk, blk, pl.BlockSpec((N_DEV, tm, cols), lambda i: (0, i, 0))],
        out_specs=(blk, blk, blk, blk), out_shape=(sds, sds, sds, sds),
        compiler_params=_rowwise_params("parallel"),
    )(w, m, v, parts)


HBM = pl.BlockSpec(memory_space=pl.ANY)
MESH = pl.DeviceIdType.MESH


def _all_gather(name, shards):
    n = len(shards)

    def body(*refs):
        ins, outs = refs[:n], refs[n:2 * n]
        send_sems, recv_sems, local_sems = refs[2 * n:]
        x, y, c = lax.axis_index("x"), lax.axis_index("y"), lax.axis_index("c")
        me, sibling = (x, y, c), (x, y, 1 - c)
        chips = [(1 - x, y), (x, 1 - y), (1 - x, 1 - y)]

        def copy(w, k, block, to, src=None):
            dst = outs[w].at[4 * block[0] + 2 * block[1] + block[2]]
            return pltpu.make_async_remote_copy(
                src_ref=dst if src is None else src, dst_ref=dst, send_sem=send_sems.at[7 * w + k],
                recv_sem=recv_sems.at[7 * w + k], device_id=to, device_id_type=MESH)

        mine, first, passed = [], [], []
        for w in range(n):
            cp = pltpu.make_async_copy(ins[w], outs[w].at[4 * x + 2 * y + c], local_sems.at[w])
            cp.start()
            mine.append(cp)
            first.append(copy(w, 0, me, sibling, src=ins[w]))
            first += [copy(w, 1 + j, me, (*chip, c), src=ins[w]) for j, chip in enumerate(chips)]
        for cp in first:
            cp.start()
        for w in range(n):
            for j, chip in enumerate(chips):
                copy(w, 1 + j, (*chip, c), me).wait_recv()
                cp = copy(w, 4 + j, (*chip, c), sibling)
                cp.start()
                passed.append(cp)
        for w in range(n):
            copy(w, 0, sibling, me).wait_recv()
            for j, chip in enumerate(chips):
                copy(w, 4 + j, (*chip, 1 - c), me).wait_recv()
        for cp in first + passed:
            cp.wait_send()
        for cp in mine:
            cp.wait()

    return _chained(
        body, name=name, in_specs=[HBM] * n, out_specs=[HBM] * n,
        out_shape=[jax.ShapeDtypeStruct((N_DEV,) + s.shape, s.dtype) for s in shards],
        scratch_shapes=[pltpu.SemaphoreType.DMA((7 * n,)), pltpu.SemaphoreType.DMA((7 * n,)),
                        pltpu.SemaphoreType.DMA((n,))],
    )(*shards)


HBM_SPEC = pl.BlockSpec(memory_space=pltpu.HBM)
SEM_SPEC = pl.BlockSpec(memory_space=pltpu.SEMAPHORE)
EFFECT = pltpu.SideEffectType.DATAFLOW_SIDE_EFFECTING
TOKEN = jax.ShapeDtypeStruct((8, LANES), F32)


def _in_hbm(a):
    return pltpu.with_memory_space_constraint(a, pltpu.HBM)


def _place():
    x, y, c = lax.axis_index("x"), lax.axis_index("y"), lax.axis_index("c")
    chips = [(1 - x, y), (x, 1 - y), (1 - x, 1 - y)]
    return x, y, c, chips


def _block_of(px, py, pc):
    return 4 * px + 2 * py + pc


def _gather_copies(ins, outs, sems, w):
    send_sems, recv_sems, local_sems = sems
    x, y, c, chips = _place()
    peers = [(x, y, 1 - c)] + [(*chip, c) for chip in chips]

    def copy(k, block, to, src):
        dst = outs[w].at[_block_of(*block)]
        return pltpu.make_async_remote_copy(src_ref=dst if src is None else src, dst_ref=dst, send_sem=send_sems.at[4 * w + k],
                                            recv_sem=recv_sems.at[4 * w + k], device_id=to, device_id_type=MESH)

    local = pltpu.make_async_copy(ins[w], outs[w].at[_block_of(x, y, c)], local_sems.at[w])
    sends = [copy(k, (x, y, c), peer, ins[w]) for k, peer in enumerate(peers)]
    recvs = [copy(k, peer, (x, y, c), None) for k, peer in enumerate(peers)]
    return local, sends, recvs


def _forward_copies(outs, sems, w):
    send_sems, recv_sems = sems
    x, y, c, chips = _place()

    def copy(j, block, to):
        dst = outs[w].at[_block_of(*block)]
        return pltpu.make_async_remote_copy(src_ref=dst, dst_ref=dst, send_sem=send_sems.at[3 * w + j],
                                            recv_sem=recv_sems.at[3 * w + j], device_id=to, device_id_type=MESH)

    sends = [copy(j, (*chip, c), (x, y, 1 - c)) for j, chip in enumerate(chips)]
    recvs = [copy(j, (*chip, 1 - c), (x, y, c)) for j, chip in enumerate(chips)]
    return sends, recvs


def _gather_start(name, shards):
    n = len(shards)

    def body(*refs):
        ins, outs, sems = refs[:n], refs[n:2 * n], refs[2 * n:2 * n + 3]
        for w in range(n):
            local, sends, _ = _gather_copies(ins, outs, sems, w)
            local.start()
            for cp in sends:
                cp.start()
        refs[-1][...] = jnp.zeros_like(refs[-1])

    lands = [lax.empty((N_DEV,) + s.shape, s.dtype) for s in shards]
    res = _chained(
        body, name=name, link=-1, in_specs=[HBM_SPEC] * (2 * n),
        out_specs=[SEM_SPEC] * 3 + [HBM_SPEC] * (2 * n) + [pl.BlockSpec(memory_space=pltpu.VMEM)],
        out_shape=[pltpu.SemaphoreType.DMA((4 * n,)), pltpu.SemaphoreType.DMA((4 * n,)), pltpu.SemaphoreType.DMA((n,))]
        + [pltpu.HBM(s.shape, s.dtype) for s in shards] + [pltpu.HBM(l.shape, l.dtype) for l in lands] + [TOKEN],
        input_output_aliases={i: 3 + i for i in range(2 * n)},
        compiler_params=pltpu.CompilerParams(has_side_effects=EFFECT),
    )(*[_in_hbm(s) for s in shards], *[_in_hbm(l) for l in lands])
    return n, res[:3], res[3:3 + n], res[3 + n:3 + 2 * n]


def _gather_mid(name, state):
    n, sems, shards, lands = state

    def body(*refs):
        ins, outs, sems_in = refs[:n], refs[n:2 * n], refs[2 * n:2 * n + 3]
        sems_out = refs[2 * n + 3:2 * n + 5]
        for w in range(n):
            local, sends, recvs = _gather_copies(ins, outs, sems_in, w)
            local.wait()
            for cp in sends:
                cp.wait_send()
            for cp in recvs:
                cp.wait_recv()
            for cp in _forward_copies(outs, sems_out, w)[0]:
                cp.start()
        refs[-1][...] = jnp.zeros_like(refs[-1])

    res = _chained(
        body, name=name, link=-1, in_specs=[HBM_SPEC] * (2 * n) + [SEM_SPEC] * 3,
        out_specs=[SEM_SPEC] * 2 + [HBM_SPEC] * n + [pl.BlockSpec(memory_space=pltpu.VMEM)],
        out_shape=[pltpu.SemaphoreType.DMA((3 * n,)), pltpu.SemaphoreType.DMA((3 * n,))]
        + [pltpu.HBM(l.shape, l.dtype) for l in lands] + [TOKEN],
        input_output_aliases={n + i: 2 + i for i in range(n)},
        compiler_params=pltpu.CompilerParams(has_side_effects=EFFECT),
    )(*shards, *lands, *sems)
    return n, res[:2], res[2:2 + n]


def _gather_end(name, state):
    n, sems, lands = state

    def body(*refs):
        outs, sems_in = refs[:n], refs[n:n + 2]
        for w in range(n):
            sends, recvs = _forward_copies(outs, sems_in, w)
            for cp in sends:
                cp.wait_send()
            for cp in recvs:
                cp.wait_recv()

    return _chained(
        body, name=name, in_specs=[HBM_SPEC] * n + [SEM_SPEC] * 2, out_specs=[HBM_SPEC] * n,
        out_shape=[pltpu.HBM(l.shape, l.dtype) for l in lands], input_output_aliases={i: i for i in range(n)},
        compiler_params=pltpu.CompilerParams(has_side_effects=EFFECT),
    )(*lands, *sems)


def _exchange_copies(ins, outs, sems, w):
    send_sems, recv_sems, local_sems = sems
    x, y, c, _ = _place()
    mine = _block_of(x, y, c)
    local = pltpu.make_async_copy(ins[w].at[mine], outs[w].at[mine], local_sems.at[w])
    remote = []
    for k in range(1, N_DEV):
        px, py, pc = x ^ (k >> 2), y ^ ((k >> 1) & 1), c ^ (k & 1)
        remote.append(pltpu.make_async_remote_copy(
            src_ref=ins[w].at[_block_of(px, py, pc)], dst_ref=outs[w].at[mine], send_sem=send_sems.at[7 * w + k - 1],
            recv_sem=recv_sems.at[7 * w + k - 1], device_id=(px, py, pc), device_id_type=MESH))
    return local, remote


def _exchange_start(name, stacks):
    n = len(stacks)

    def body(*refs):
        ins, outs, sems = refs[:n], refs[n:2 * n], refs[2 * n:2 * n + 3]
        for w in range(n):
            local, remote = _exchange_copies(ins, outs, sems, w)
            local.start()
            for cp in remote:
                cp.start()
        refs[-1][...] = jnp.zeros_like(refs[-1])

    lands = [lax.empty(s.shape, s.dtype) for s in stacks]
    if any(s is _Chain.last for s in stacks):
        _Chain.last = None
    res = _chained(
        body, name=name, link=-1, in_specs=[HBM_SPEC] * (2 * n),
        out_specs=[SEM_SPEC] * 3 + [HBM_SPEC] * (2 * n) + [pl.BlockSpec(memory_space=pltpu.VMEM)],
        out_shape=[pltpu.SemaphoreType.DMA((7 * n,)), pltpu.SemaphoreType.DMA((7 * n,)), pltpu.SemaphoreType.DMA((n,))]
        + [pltpu.HBM(s.shape, s.dtype) for s in stacks] * 2 + [TOKEN],
        input_output_aliases={i: 3 + i for i in range(2 * n)},
        compiler_params=pltpu.CompilerParams(has_side_effects=EFFECT),
    )(*[_in_hbm(s) for s in stacks], *[_in_hbm(l) for l in lands])
    return n, res[:3], res[3:3 + n], res[3 + n:3 + 2 * n]


def _exchange_wait(name, state):
    n, sems, stacks, lands = state

    def body(*refs):
        ins, outs, sems_in = refs[:n], refs[n:2 * n], refs[2 * n:2 * n + 3]
        for w in range(n):
            local, remote = _exchange_copies(ins, outs, sems_in, w)
            local.wait()
            for cp in remote:
                cp.wait_send()
                cp.wait_recv()

    return _chained(
        body, name=name, in_specs=[HBM_SPEC] * (2 * n) + [SEM_SPEC] * 3, out_specs=[HBM_SPEC] * n,
        out_shape=[pltpu.HBM(l.shape, l.dtype) for l in lands], input_output_aliases={n + i: i for i in range(n)},
        compiler_params=pltpu.CompilerParams(has_side_effects=EFFECT),
    )(*stacks, *lands, *sems)


def _unstack_cols(w):
    return w.transpose(1, 0, 2).reshape(w.shape[1], N_DEV * w.shape[2])


def _stack_cols(w):
    return w.reshape(w.shape[0], N_DEV, w.shape[1] // N_DEV).transpose(1, 0, 2)


def _rope_tables(positions):
    half = MLA_ROPE // 2
    inv_freq = ROPE_THETA ** (-jnp.arange(half, dtype=F32) / half)
    ang = positions.astype(F32)[:, None] * inv_freq
    cos, sin = jnp.cos(ang), jnp.sin(ang)
    t = positions.shape[0]
    cos_t = jnp.concatenate([jnp.ones((t, MLA_NOPE), F32), cos, cos], axis=1)
    sin_t = jnp.concatenate([jnp.zeros((t, MLA_NOPE), F32), -sin, sin], axis=1)
    idx = jnp.arange(MLA_QK)
    partner = jnp.where(idx < MLA_NOPE, -1, jnp.where(idx < MLA_NOPE + half, idx + half, idx - half))
    swap = (idx[:, None] == partner[None, :]).astype(BF16)
    return cos_t, sin_t, swap


def _ffn_fwd(tag, x, gain, wt):
    h = _rms_fwd(tag + "_rms", x, gain, BF16)
    g = _mm_stack_nt_out(tag + "_gate", h, wt[tag + '_w_gate'], BF16)
    u, a = _mm_stack_nt_out(tag + "_up", h, wt[tag + '_w_up'], BF16,
                            fuse=(lambda u_blk, g_blk: (u_blk, g_blk * jax.nn.sigmoid(g_blk) * u_blk), [g]))
    y = _mm_stack_sum(tag + "_down", a, wt[tag + '_w_down'], F32, scale=0.5, res=x)
    return y, (x, h, g, u, a)


def _ffn_bwd_weights(tag, dy, saved, wd, comm):
    x, h, g, u, a = saved
    comm.grads({tag + '_w_down': _mm_stack_tn_right(tag + "_dwd", a, dy, BF16, scale=0.5)})

    def dact(da, g_blk, u_blk):
        sig = jax.nn.sigmoid(g_blk)
        return da * u_blk * sig * (1.0 + g_blk * (1.0 - sig)), da * g_blk * sig

    dg, du = _mm_stack_nt_out(tag + "_da", dy, wd, BF16, scale=0.5, fuse=(dact, [g, u]))
    comm.grads({tag + '_w_gate': _mm_stack_tn_right(tag + "_dwg", dg, h, BF16)})
    comm.grads({tag + '_w_up': _mm_stack_tn_right(tag + "_dwu", du, h, BF16)})
    return dg, du


def _ffn_bwd_input(tag, dy, saved, dgu, gain, wg, wu):
    dg, du = dgu
    dh = _mm_stack_sum(tag + "_dh", (dg, du), (wg, wu), F32)
    return _rms_bwd(tag + "_drms", saved[0], gain, dh, res=dy)


def _local_step(x, mem, positions, sm, comm, target):
    t, d = x.shape
    nm = mem.shape[0]
    gs, gw = {}, {}

    class Weights(dict):
        def __missing__(self, name):
            self.update(comm.weights(next(k for k, names in enumerate(GATHERS) if name in names)))
            return self[name]

    wt = Weights()

    x1, ffn1_saved = _ffn_fwd("ffn1", x, sm['ffn1_norm'], wt)

    h2 = _rms_fwd("mix_rms", x1, sm['mix_norm'], BF16)
    w_in_ref = _unstack_cols(wt['w_in'])
    pieces = []
    for name, n in IN_PAD:
        piece = w_in_ref[:, REF_OFF[name]:REF_OFF[name] + REF_SIZE[name]]
        if n != REF_SIZE[name]:
            piece = jnp.pad(piece, ((0, 0), (0, n - REF_SIZE[name])))
        pieces.append(piece)
    w_in = jnp.concatenate(pieces, axis=1)
    z = _mm2("mix_in", h2, w_in, NN, F32)
    zs = {name: z[:, PAD_OFF[name]:PAD_OFF[name] + n] for name, n in IN_PAD}

    cq = _rms_fwd("mla_q_a_rms", zs['zq'], sm['q_a_norm'], BF16)
    q_raw = _mm_stack_nt_out("mla_q_up", cq, wt['w_q_up'], F32)
    ckv = _rms_fwd("mla_kv_a_rms", zs['zkv'], sm['kv_a_norm'], BF16)
    kv = _mm_stack_out("mla_kv_up", ckv, wt['w_kv_up'], F32)
    zkr = zs['zkr'][:, :MLA_ROPE]
    k_raw = jnp.concatenate([kv[:, :, :MLA_NOPE], jnp.broadcast_to(zkr[None], (MLA_HEADS, t, MLA_ROPE))], axis=2)
    v_mla = kv[:, :, MLA_NOPE:].astype(BF16)
    cos_t, sin_t, swap = _rope_tables(positions)
    q_raw2, k_raw2 = q_raw.reshape(MLA_HEADS * t, MLA_QK), k_raw.reshape(MLA_HEADS * t, MLA_QK)
    rope = (cos_t, sin_t, swap)
    qf = _rms_fwd("mla_q_rms", q_raw2, sm['mla_q_norm'], BF16, rope=rope).reshape(MLA_HEADS, t, MLA_QK)
    kf = _rms_fwd("mla_k_rms", k_raw2, sm['mla_k_norm'], BF16, rope=rope).reshape(MLA_HEADS, t, MLA_QK)
    o_mla = _attn_fwd("mla_attn", qf, kf, v_mla, MLA_QK ** -0.5, True)

    w_g2 = jnp.pad(_unstack_cols(wt['gla_w_gate2']), ((0, LANES - GLA_GATE_RANK), (0, 0)))
    pre = _mm2("gla_gate_pre", zs['zg'], w_g2, NN, F32)
    log_a = _gate_fwd("gla_gate", pre, sm['gla_b_gate'])
    o_gla_raw, states = _gla_fwd("gla_scan", zs['gq'], zs['gk'], zs['gv'], log_a, GLA_HEADS)
    o_gla_n = _rms_fwd("gla_out_rms", o_gla_raw, sm['gla_out_norm'], F32)
    o_gla = _swiglu_fwd("gla_out_gate", zs['zr'], o_gla_n, BF16)

    cat = jnp.concatenate([o_mla, o_gla], axis=1)
    w_out = wt['w_out'].reshape(d, d)
    x2 = _mm2("mix_out", cat, w_out, NN, F32, res=x1)

    w_mq, w_mk, w_mv = (wt[n].reshape(d, MEM_HEADS * MEM_HEAD_DIM) for n in ('mem_w_q', 'mem_w_k', 'mem_w_v'))
    hq = _rms_fwd("mem_attn_rms", x2, sm['mem_attn_norm'], BF16)
    hm = _rms_fwd("mem_rms", mem, sm['mem_norm'], BF16)

    def heads_out(name, a, b, out_dtype):
        m, kk = a.shape
        tm = _tile(m, 512)
        return _mm(name, a, b, (m // tm, MEM_HEADS, 1), ((tm, kk), lambda i, h, k: (i, 0)),
                   ((kk, MEM_HEAD_DIM), lambda i, h, k: (0, h)), ((None, tm, MEM_HEAD_DIM), lambda i, h, k: (h, i, 0)),
                   (MEM_HEADS, m, MEM_HEAD_DIM), out_dtype, NN)

    mq_raw = heads_out("mem_q", hq, w_mq, F32)
    mk_raw = heads_out("mem_k", hm, w_mk, F32)
    mv = heads_out("mem_v", hm, w_mv, BF16)
    mq = _rms_fwd("mem_q_rms", mq_raw.reshape(MEM_HEADS * t, MEM_HEAD_DIM), sm['mem_q_norm'], BF16)
    mk = _rms_fwd("mem_k_rms", mk_raw.reshape(MEM_HEADS * nm, MEM_HEAD_DIM), sm['mem_k_norm'], BF16)
    mq, mk = mq.reshape(MEM_HEADS, t, MEM_HEAD_DIM), mk.reshape(MEM_HEADS, nm, MEM_HEAD_DIM)
    o_mem = _attn_fwd("mem_attn", mq, mk, mv, MEM_HEAD_DIM ** -0.5, False)
    w_mo = wt['mem_w_o']
    mo_cols = w_mo.shape[2]
    tm = _tile(t, 512)
    x3 = _mm("mem_out", o_mem, w_mo, (t // tm, N_DEV, 1), ((tm, o_mem.shape[1]), lambda i, j, k: (i, 0)),
             ((None, o_mem.shape[1], mo_cols), lambda i, j, k: (j, 0, 0)), ((tm, mo_cols), lambda i, j, k: (i, j)),
             (t, d), F32, NN, res=x2)

    y, ffn2_saved = _ffn_fwd("ffn2", x3, sm['ffn2_norm'], wt)
    dy, dy_narrow, loss_lanes = _loss("loss", y, target)

    dgu = _ffn_bwd_weights("ffn2", dy_narrow, ffn2_saved, wt['ffn2_w_down'], comm)
    dx3, gs['ffn2_norm'] = _ffn_bwd_input("ffn2", dy, ffn2_saved, dgu, sm['ffn2_norm'],
                                          wt['ffn2_w_gate'], wt['ffn2_w_up'])

    do_mem = _mm("mem_do", dx3, w_mo, (t // tm, MEM_HEADS, N_DEV), ((tm, mo_cols), lambda i, h, k: (i, k)),
                 ((None, MEM_HEAD_DIM, mo_cols), lambda i, h, k: (k, h, 0)), ((tm, MEM_HEAD_DIM), lambda i, h, k: (i, h)),
                 (t, MEM_HEADS * MEM_HEAD_DIM), BF16, NT)
    tk = _tile(t, 512)
    gw['mem_w_o'] = _mm("mem_dwo", o_mem, dx3, (N_DEV, 1, t // tk), ((tk, o_mem.shape[1]), lambda j, i, k: (k, 0)),
                        ((tk, mo_cols), lambda j, i, k: (k, j)), ((None, o_mem.shape[1], mo_cols), lambda j, i, k: (j, 0, 0)),
                        w_mo.shape, BF16, TN)
    dmq, dmk, dmv = _attn_bwd("mem_dattn", mq, mk, mv, do_mem, MEM_HEAD_DIM ** -0.5, False)
    dmq_raw, gs['mem_q_norm'] = _rms_bwd("mem_q_drms", mq_raw.reshape(MEM_HEADS * t, MEM_HEAD_DIM), sm['mem_q_norm'],
                                         dmq.reshape(MEM_HEADS * t, MEM_HEAD_DIM))
    dmk_raw, gs['mem_k_norm'] = _rms_bwd("mem_k_drms", mk_raw.reshape(MEM_HEADS * nm, MEM_HEAD_DIM), sm['mem_k_norm'],
                                         dmk.reshape(MEM_HEADS * nm, MEM_HEAD_DIM))
    dmq_raw = dmq_raw.reshape(MEM_HEADS, t, MEM_HEAD_DIM)
    dmk_raw = dmk_raw.reshape(MEM_HEADS, nm, MEM_HEAD_DIM)

    def heads_in_nt(name, a, b, res=None):
        m, n = a.shape[1], b.shape[0]
        tm_, tn_ = _tile(m, 512), _tile(n, 1024)
        return _mm(name, a, b, (m // tm_, n // tn_, MEM_HEADS), ((None, tm_, MEM_HEAD_DIM), lambda i, j, k: (k, i, 0)),
                   ((tn_, MEM_HEAD_DIM), lambda i, j, k: (j, k)), ((tm_, tn_), lambda i, j, k: (i, j)), (m, n), F32, NT,
                   None, res)

    def heads_tn(name, a, b):
        m, kp = a.shape
        tm_, tk_ = _tile(kp, 1024), _tile(m, 512)
        return _mm(name, a, b, (kp // tm_, MEM_HEADS, m // tk_), ((tk_, tm_), lambda i, h, k: (k, i)),
                   ((None, tk_, MEM_HEAD_DIM), lambda i, h, k: (h, k, 0)), ((tm_, MEM_HEAD_DIM), lambda i, h, k: (i, h)),
                   (kp, MEM_HEADS * MEM_HEAD_DIM), BF16, TN)

    dhq = heads_in_nt("mem_dhq", dmq_raw, w_mq)
    gw['mem_w_q'] = heads_tn("mem_dwq", hq, dmq_raw).reshape(wt['mem_w_q'].shape)
    dhm = heads_in_nt("mem_dhm_k", dmk_raw, w_mk)
    dhm = heads_in_nt("mem_dhm_v", dmv, w_mv, res=dhm)
    gw['mem_w_k'] = heads_tn("mem_dwk", hm, dmk_raw).reshape(wt['mem_w_k'].shape)
    gw['mem_w_v'] = heads_tn("mem_dwv", hm, dmv).reshape(wt['mem_w_v'].shape)
    _, gs['mem_norm'] = _rms_bwd("mem_drms", mem, sm['mem_norm'], dhm)
    comm.grads({n: gw[n] for n in MEMORY})
    dx2, gs['mem_attn_norm'] = _rms_bwd("mem_attn_drms", x2, sm['mem_attn_norm'], dhq, res=dx3)

    dcat = _mm2("mix_dcat", dx2, w_out, NT, F32)
    comm.grads({'w_out': _mm2("mix_dwout", cat, dx2, TN, BF16, tm=1024, tn=2048, tk=512).reshape(wt['w_out'].shape)})
    do_mla, do_gla = dcat[:, :MLA_HEADS * MLA_V], dcat[:, MLA_HEADS * MLA_V:]

    dzr, dgn = _swiglu_bwd("gla_out_dgate", do_gla, zs['zr'], o_gla_n, F32)
    do_gla_raw, gs['gla_out_norm'] = _rms_bwd("gla_out_drms", o_gla_raw, sm['gla_out_norm'], dgn)
    dgq, dgk, dgv, dlog_a = _gla_bwd("gla_dscan", zs['gq'], zs['gk'], zs['gv'], log_a, states, do_gla_raw, GLA_HEADS)
    dpre, gs['gla_b_gate'] = _gate_bwd("gla_dgate", pre, sm['gla_b_gate'], dlog_a)
    dw_g2 = _mm2("gla_dwgate", zs['zg'], dpre, TN, BF16, tk=512)
    comm.grads({'gla_w_gate2': _stack_cols(dw_g2[:GLA_GATE_RANK])})
    dzg = _mm2("gla_dzg", dpre, w_g2, NT, F32)

    dqf, dkf, dv_mla = _attn_bwd("mla_dattn", qf, kf, v_mla, do_mla, MLA_QK ** -0.5, True)
    dq_raw, gs['mla_q_norm'] = _rms_bwd("mla_q_drms", q_raw2, sm['mla_q_norm'], dqf.reshape(MLA_HEADS * t, MLA_QK), rope=rope)
    dk_raw, gs['mla_k_norm'] = _rms_bwd("mla_k_drms", k_raw2, sm['mla_k_norm'], dkf.reshape(MLA_HEADS * t, MLA_QK), rope=rope)
    dq_raw = dq_raw.reshape(MLA_HEADS, t, MLA_QK)
    dk_raw = dk_raw.reshape(MLA_HEADS, t, MLA_QK)
    dkv = jnp.concatenate([dk_raw[:, :, :MLA_NOPE], dv_mla], axis=2)
    dzkr = jnp.sum(dk_raw[:, :, MLA_NOPE:], axis=0)
    comm.grads({'w_q_up': _mm_stack_tn_right("mla_dwq", dq_raw, cq, BF16),
                'w_kv_up': _mm_stack_tn_left("mla_dwkv", ckv, dkv, BF16)})
    dcq = _mm_stack_sum("mla_dcq", dq_raw, wt['w_q_up'], F32)
    dckv = _mm_stack_nt_sum("mla_dckv", dkv, wt['w_kv_up'], F32)
    dzq, gs['q_a_norm'] = _rms_bwd("mla_q_a_drms", zs['zq'], sm['q_a_norm'], dcq)
    dzkv, gs['kv_a_norm'] = _rms_bwd("mla_kv_a_drms", zs['zkv'], sm['kv_a_norm'], dckv)

    dzs = {'zq': dzq, 'zkv': dzkv, 'gq': dgq, 'gk': dgk, 'gv': dgv, 'zr': dzr,
           'zkr': jnp.pad(dzkr, ((0, 0), (0, LANES - MLA_ROPE))), 'zg': dzg}
    dz = jnp.concatenate([dzs[name].astype(BF16) for name, _ in IN_PAD], axis=1)
    dw_in = _mm2("mix_dwin", h2, dz, TN, BF16, tm=1024, tn=2048, tk=512)
    dw_in_ref = jnp.concatenate([dw_in[:, PAD_OFF[name]:PAD_OFF[name] + n] for name, n in IN_REF], axis=1)
    comm.grads({'w_in': _stack_cols(dw_in_ref)})
    dh2 = _mm2("mix_dh", dz, w_in, NT, F32)
    dx1, dx1_narrow, gs['mix_norm'] = _rms_bwd("mix_drms", x1, sm['mix_norm'], dh2, res=dx2, narrow=True)

    dgu = _ffn_bwd_weights("ffn1", dx1_narrow, ffn1_saved, wt['ffn1_w_down'], comm)
    grad_x, gs['ffn1_norm'] = _ffn_bwd_input("ffn1", dx1, ffn1_saved, dgu, sm['ffn1_norm'],
                                             wt['ffn1_w_gate'], wt['ffn1_w_up'])
    return loss_lanes, grad_x, gs


def _pad_lanes(v):
    n = v.shape[1]
    return jnp.pad(v, ((0, 0), (0, -n % LANES)))


def _pack_small(vals):
    return jnp.concatenate([_pad_lanes(vals[n]) for n in SMALL], axis=1)


def _unpack_small(packed, like):
    out, off = {}, 0
    for n in SMALL:
        size = like[n].shape[1]
        out[n] = packed[:, off:off + size]
        off += size + (-size % LANES)
    return out


def kernel(x, mem, positions, ffn1_norm, ffn1_w_gate, ffn1_w_up, ffn1_w_down, mix_norm, w_in, q_a_norm, w_q_up, kv_a_norm, w_kv_up, mla_q_norm, mla_k_norm, gla_w_gate2, gla_b_gate, gla_out_norm, w_out, mem_attn_norm, mem_norm, mem_w_q, mem_w_k, mem_w_v, mem_w_o, mem_q_norm, mem_k_norm, ffn2_norm, ffn2_w_gate, ffn2_w_up, ffn2_w_down, loss_target, m_ffn1_norm, m_ffn1_w_gate, m_ffn1_w_up, m_ffn1_w_down, m_mix_norm, m_w_in, m_q_a_norm, m_w_q_up, m_kv_a_norm, m_w_kv_up, m_mla_q_norm, m_mla_k_norm, m_gla_w_gate2, m_gla_b_gate, m_gla_out_norm, m_w_out, m_mem_attn_norm, m_mem_norm, m_mem_w_q, m_mem_w_k, m_mem_w_v, m_mem_w_o, m_mem_q_norm, m_mem_k_norm, m_ffn2_norm, m_ffn2_w_gate, m_ffn2_w_up, m_ffn2_w_down, v_ffn1_norm, v_ffn1_w_gate, v_ffn1_w_up, v_ffn1_w_down, v_mix_norm, v_w_in, v_q_a_norm, v_w_q_up, v_kv_a_norm, v_w_kv_up, v_mla_q_norm, v_mla_k_norm, v_gla_w_gate2, v_gla_b_gate, v_gla_out_norm, v_w_out, v_mem_attn_norm, v_mem_norm, v_mem_w_q, v_mem_w_k, v_mem_w_v, v_mem_w_o, v_mem_q_norm, v_mem_k_norm, v_ffn2_norm, v_ffn2_w_gate, v_ffn2_w_up, v_ffn2_w_down):
    inp = dict(locals())
    x, mem, positions, target = inp['x'][0], inp['mem'][0], inp['positions'][0], inp['loss_target'][0]
    sm = {n: inp[n] for n in SMALL}
    out = {}

    def stored(key):
        name = key[2:] if key[:2] in ('m_', 'v_') else key
        return inp[key][0].T if name in TRANSPOSED else inp[key][0]

    def as_given(name, r):
        return r.T[None] if name in TRANSPOSED else r[None]

    class Comm:
        def __init__(self):
            self.gathers = {0: self.start(0)}
            self.forwards, self.exchanges = {}, []

        def start(self, k):
            return _gather_start(f"gather_start_{k}", [stored(n).astype(BF16) for n in GATHERS[k]])

        def forward(self, k):
            if k not in self.forwards:
                self.forwards[k] = _gather_mid(f"gather_mid_{k}", self.gathers[k])
                self.gathers.update({nxt: self.start(nxt) for nxt in NEXT_GATHERS.get(k, [])})

        def weights(self, k):
            self.forward(k)
            if k in EARLY_FORWARD:
                self.forward(EARLY_FORWARD[k])
            return dict(zip(GATHERS[k], _gather_end(f"gather_end_{k}", self.forwards[k])))

        def grads(self, stacks):
            names = list(stacks)
            self.exchanges.append((names, _exchange_start("exchange_start_" + names[0], [stacks[n] for n in names])))

        def update(self, count):
            todo, self.exchanges = self.exchanges[:count], self.exchanges[count:]
            for names, state in todo:
                for n, p in zip(names, _exchange_wait("exchange_wait_" + names[0], state)):
                    res = _adamw("adamw_" + n, stored(n), stored('m_' + n), stored('v_' + n), p)
                    for kind, r in zip(('grad_', 'delta_', 'new_m_', 'new_v_'), res):
                        out[kind + n] = as_given(n, r)

    _Chain.last = None
    comm = Comm()
    loss_lanes, grad_x, gs = _local_step(x, mem, positions, sm, comm, target)
    out['loss'] = lax.psum(jnp.sum(loss_lanes), ("x", "y", "c"))
    out['grad_x'] = grad_x[None]

    comm.update(len(comm.exchanges) - 3)
    small_parts = _all_gather("gather_small", [_pack_small(gs)])[0]
    res = _adamw("adamw_small", _pack_small(sm), _pack_small({n: inp['m_' + n] for n in SMALL}),
                 _pack_small({n: inp['v_' + n] for n in SMALL}), small_parts)
    for kind, r in zip(('grad_', 'delta_', 'new_m_', 'new_v_'), res):
        for n, val in _unpack_small(r, sm).items():
            out[kind + n] = val

    comm.update(3)

    names = ['loss', 'grad_x'] + [k + n for k in ('grad_', 'delta_', 'new_m_', 'new_v_') for n in WEIGHTS]
    return tuple(out[n] for n in names)
```

```python
import functools
import math

import jax
import jax.numpy as jnp
from jax import lax
from jax.experimental import pallas as pl
from jax.experimental.pallas import tpu as pltpu

F32 = jnp.float32
BF16 = jnp.bfloat16

N_DEV = 8
EPS = 1e-6
CHUNK = 64
MLA_HEADS, MLA_NOPE, MLA_ROPE, MLA_V = 8, 128, 64, 128
MLA_QK = MLA_NOPE + MLA_ROPE
MLA_Q_RANK, MLA_KV_RANK = 512, 256
ROPE_THETA = 10000.0
GLA_HEADS, GLA_DK, GLA_DV, GLA_GATE_RANK = 4, 128, 256, 16
GLA_TAU = 16.0
MEM_HEADS, MEM_HEAD_DIM = 4, 128
ADAM_LR, ADAM_B1, ADAM_B2, ADAM_EPS, ADAM_WD, ADAM_STEP = 0.001, 0.9, 0.999, 1e-08, 0.01, 10

V7X_VMEM_BYTES = 64 * 1024 * 1024
LANES = 128

NN = (((1,), (0,)), ((), ()))
NT = (((1,), (1,)), ((), ()))
TN = (((0,), (0,)), ((), ()))

WEIGHTS = ['ffn1_norm', 'ffn1_w_gate', 'ffn1_w_up', 'ffn1_w_down', 'mix_norm', 'w_in', 'q_a_norm', 'w_q_up',
           'kv_a_norm', 'w_kv_up', 'mla_q_norm', 'mla_k_norm', 'gla_w_gate2', 'gla_b_gate', 'gla_out_norm', 'w_out',
           'mem_attn_norm', 'mem_norm', 'mem_w_q', 'mem_w_k', 'mem_w_v', 'mem_w_o', 'mem_q_norm', 'mem_k_norm',
           'ffn2_norm', 'ffn2_w_gate', 'ffn2_w_up', 'ffn2_w_down']
SMALL = ['ffn1_norm', 'mix_norm', 'q_a_norm', 'kv_a_norm', 'mla_q_norm', 'mla_k_norm', 'gla_b_gate', 'gla_out_norm',
         'mem_attn_norm', 'mem_norm', 'mem_q_norm', 'mem_k_norm', 'ffn2_norm']
MIXER = ['w_in', 'w_q_up', 'w_kv_up', 'gla_w_gate2', 'w_out']
MEMORY = ['mem_w_q', 'mem_w_k', 'mem_w_v', 'mem_w_o']
TRANSPOSED = ['ffn1_w_gate', 'ffn1_w_up', 'ffn2_w_gate', 'ffn2_w_up', 'w_q_up']
TWO_STAGE = ['w_in', 'ffn1_w_down', 'ffn1_w_gate', 'ffn1_w_up']
GATHERS = [['ffn1_w_gate'], ['ffn1_w_up'], ['ffn1_w_down'], MIXER, MEMORY, ['ffn2_w_gate', 'ffn2_w_up', 'ffn2_w_down']]
NEXT_GATHERS = {0: [1], 1: [2], 2: [3], 3: [4, 5]}
EARLY_FORWARD = {4: 5}

IN_REF = [('zq', 512), ('zkv', 256), ('zkr', 64), ('gq', 512), ('gk', 512), ('gv', 1024), ('zg', 16), ('zr', 1024)]
IN_PAD = [('zq', 512), ('zkv', 256), ('gq', 512), ('gk', 512), ('gv', 1024), ('zr', 1024), ('zkr', 128), ('zg', 128)]
IN_WIDTH = sum(n for _, n in IN_REF)
IN_PAD_WIDTH = sum(n for _, n in IN_PAD)


def _offsets(layout):
    out, off = {}, 0
    for name, n in layout:
        out[name] = off
        off += n
    return out


REF_OFF, PAD_OFF = _offsets(IN_REF), _offsets(IN_PAD)
REF_SIZE = dict(IN_REF)


def _tile(n, pref):
    return pref if n % pref == 0 else n


def _block_bytes(blk, dtype):
    dims = [d for d in blk if d is not None]
    if len(dims) >= 1:
        dims[-1] = -(-dims[-1] // LANES) * LANES
    return math.prod(dims) * jnp.dtype(dtype).itemsize


def _vmem_limit(pipelined_bytes, resident_bytes=0):
    need = 2 * pipelined_bytes + resident_bytes + (8 << 20)
    return int(min(max(need, 32 << 20), V7X_VMEM_BYTES - (6 << 20)))


class _Chain:
    last = None


def _chained(body, *, in_specs, link=0, **kwargs):
    def call(*operands):
        dep = _Chain.last
        if dep is not None and any(o is dep for o in operands):
            dep = None
        if dep is None:
            res = pl.pallas_call(body, in_specs=in_specs, **kwargs)(*operands)
        else:
            n = len(operands)

            def chained_body(*refs):
                body(*refs[:n], *refs[n + 1:])

            res = pl.pallas_call(chained_body, in_specs=list(in_specs) + [pl.BlockSpec(memory_space=pl.ANY)],
                                 **kwargs)(*operands, dep)
        _Chain.last = res[link] if isinstance(res, (list, tuple)) else res
        return res

    return call


def _row_tile(rows, width, elements=1 << 18):
    lanes = -(-width // LANES) * LANES
    return _tile(rows, max(256, 1 << int(math.log2(max(1, elements // lanes)))))


def _rowwise_params(*semantics):
    return pltpu.CompilerParams(dimension_semantics=semantics, vmem_limit_bytes=48 << 20)


def _mm(name, a, b, grid, a_spec, b_spec, o_spec, out_shape, out_dtype, dims, scale=None, res=None, fuse=None):
    nk = grid[2]
    o_blk, o_map = o_spec
    acc_shape = tuple(d for d in o_blk if d is not None)
    extras = [res] if res is not None else (list(fuse[1]) if fuse else [])
    n_out = 2 if fuse else 1
    a_list, b_list = (list(a), list(b)) if isinstance(a, (tuple, list)) else ([a], [b])
    a_specs = a_spec if isinstance(a_spec, list) else [a_spec] * len(a_list)
    b_specs = b_spec if isinstance(b_spec, list) else [b_spec] * len(b_list)
    n_in = 2 * len(a_list)

    def body(*refs):
        a_refs, b_refs = refs[:n_in // 2], refs[n_in // 2:n_in]
        extra_refs = refs[n_in:n_in + len(extras)]
        out_refs = refs[n_in + len(extras):n_in + len(extras) + n_out]
        rest = refs[n_in + len(extras) + n_out:]
        r_ref = extra_refs[0] if res is not None else None
        o_ref = out_refs[0]

        def product():
            return sum(lax.dot_general(a_ref[...].astype(BF16), b_ref[...].astype(BF16), dims, preferred_element_type=F32)
                       for a_ref, b_ref in zip(a_refs, b_refs))

        def finish(r):
            if scale is not None:
                r = r * scale
            if r_ref is not None:
                r = r + r_ref[...].astype(F32)
            if fuse:
                for ref, val in zip(out_refs, fuse[0](r, *[e[...].astype(F32) for e in extra_refs])):
                    ref[...] = val.astype(ref.dtype)
            else:
                o_ref[...] = r.astype(o_ref.dtype)

        if nk == 1:
            finish(product())
        else:
            acc = rest[0]
            k = pl.program_id(2)

            @pl.when(k == 0)
            def _():
                acc[...] = product()

            @pl.when(k > 0)
            def _():
                acc[...] += product()

            @pl.when(k == nk - 1)
            def _():
                finish(acc[...])

    in_specs = [pl.BlockSpec(*spec) for spec in a_specs + b_specs]
    operands = a_list + b_list
    piped = sum(_block_bytes(spec[0], v.dtype) for spec, v in zip(a_specs + b_specs, operands))
    piped += _block_bytes(o_blk, out_dtype)
    for extra in extras:
        in_specs.append(pl.BlockSpec(o_blk, o_map))
        operands.append(extra)
        piped += _block_bytes(o_blk, extra.dtype)
    piped += (n_out - 1) * _block_bytes(o_blk, out_dtype)
    scratch = [pltpu.VMEM(acc_shape, F32)] if nk > 1 else []
    out_spec, out_sds = pl.BlockSpec(o_blk, o_map), jax.ShapeDtypeStruct(out_shape, out_dtype)
    return _chained(
        body, name=name, grid=grid, in_specs=in_specs, out_specs=(out_spec,) * n_out if fuse else out_spec,
        out_shape=(out_sds,) * n_out if fuse else out_sds, scratch_shapes=scratch,
        compiler_params=pltpu.CompilerParams(
            dimension_semantics=("parallel", "parallel", "arbitrary"),
            vmem_limit_bytes=_vmem_limit(piped, 3 * _block_bytes(acc_shape, F32))),
    )(*operands)


def _mm2(name, a, b, dims, out_dtype, tm=512, tn=1024, tk=2048, scale=None, res=None):
    if dims is NN:
        (m, kk), n = a.shape, b.shape[1]
    elif dims is NT:
        (m, kk), n = a.shape, b.shape[0]
    else:
        (kk, m), n = a.shape, b.shape[1]
    tm, tn, tk = _tile(m, tm), _tile(n, tn), _tile(kk, tk)
    a_spec = ((tk, tm), lambda i, j, k: (k, i)) if dims is TN else ((tm, tk), lambda i, j, k: (i, k))
    b_spec = ((tn, tk), lambda i, j, k: (j, k)) if dims is NT else ((tk, tn), lambda i, j, k: (k, j))
    return _mm(name, a, b, (m // tm, n // tn, kk // tk), a_spec, b_spec, ((tm, tn), lambda i, j, k: (i, j)),
               (m, n), out_dtype, dims, scale, res)


def _mm_stack_out(name, a, b, out_dtype, tm=512, tk=2048):
    (m, kk), (nj, _, n) = a.shape, b.shape
    tm, tk = _tile(m, tm), _tile(kk, tk)
    return _mm(name, a, b, (nj, m // tm, kk // tk), ((tm, tk), lambda j, i, k: (i, k)),
               ((None, tk, n), lambda j, i, k: (j, k, 0)), ((None, tm, n), lambda j, i, k: (j, i, 0)),
               (nj, m, n), out_dtype, NN)


def _mm_stack_nt_out(name, a, b, out_dtype, scale=None, fuse=None, tm=512, tk=2048):
    (m, kk), (nj, n, _) = a.shape, b.shape
    tm, tk = _tile(m, tm), _tile(kk, tk)
    return _mm(name, a, b, (nj, m // tm, kk // tk), ((tm, tk), lambda j, i, k: (i, k)),
               ((None, n, tk), lambda j, i, k: (j, 0, k)), ((None, tm, n), lambda j, i, k: (j, i, 0)),
               (nj, m, n), out_dtype, NT, scale, fuse=fuse)


def _mm_stack_sum(name, a, b, out_dtype, scale=None, res=None, tm=512, tn=1024):
    (nj, m, f), n = (a[0] if isinstance(a, tuple) else a).shape, (b[0] if isinstance(b, tuple) else b).shape[2]
    tm, tn = _tile(m, tm), _tile(n, tn)
    a_spec = ((None, tm, f), lambda i, j, k: (k, i, 0))
    b_spec = ((None, f, tn), lambda i, j, k: (k, 0, j))
    if not isinstance(a, tuple) and nj % 2 == 0:
        nj, a, b = nj // 2, (a, a), (b, b)
        a_spec = [((None, tm, f), lambda i, j, k: (2 * k, i, 0)), ((None, tm, f), lambda i, j, k: (2 * k + 1, i, 0))]
        b_spec = [((None, f, tn), lambda i, j, k: (2 * k, 0, j)), ((None, f, tn), lambda i, j, k: (2 * k + 1, 0, j))]
    return _mm(name, a, b, (m // tm, n // tn, nj), a_spec, b_spec, ((tm, tn), lambda i, j, k: (i, j)),
               (m, n), out_dtype, NN, scale, res)


def _mm_stack_nt_sum(name, a, b, out_dtype, res=None, tm=512, tn=1024):
    (nj, m, f), n = a.shape, b.shape[1]
    tm, tn = _tile(m, tm), _tile(n, tn)
    return _mm(name, a, b, (m // tm, n // tn, nj), ((None, tm, f), lambda i, j, k: (k, i, 0)),
               ((None, tn, f), lambda i, j, k: (k, j, 0)), ((tm, tn), lambda i, j, k: (i, j)),
               (m, n), out_dtype, NT, None, res)


def _mm_stack_tn_left(name, a, b, out_dtype, tm=1024, tk=512):
    (m, kp), (nj, _, n) = a.shape, b.shape
    tm, tk = _tile(kp, tm), _tile(m, tk)
    return _mm(name, a, b, (nj, kp // tm, m // tk), ((tk, tm), lambda j, i, k: (k, i)),
               ((None, tk, n), lambda j, i, k: (j, k, 0)), ((None, tm, n), lambda j, i, k: (j, i, 0)),
               (nj, kp, n), out_dtype, TN)


def _mm_stack_tn_right(name, a, b, out_dtype, scale=None, tn=2048, tk=512):
    (nj, m, f), n = a.shape, b.shape[1]
    tn, tk = _tile(n, tn), _tile(m, tk)
    return _mm(name, a, b, (nj, n // tn, m // tk), ((None, tk, f), lambda j, i, k: (j, k, 0)),
               ((tk, tn), lambda j, i, k: (k, i)), ((None, f, tn), lambda j, i, k: (j, 0, i)),
               (nj, f, n), out_dtype, TN, scale)


def _split3(x):
    hi = x.astype(BF16)
    r1 = x - hi.astype(F32)
    mid = r1.astype(BF16)
    lo = (r1 - mid.astype(F32)).astype(BF16)
    return hi, mid, lo


def _swap_halves(x, swap):
    return sum(jnp.dot(part, swap, preferred_element_type=F32) for part in _split3(x))


def _rope_specs(rope, tm):
    cos_t, _, swap = rope
    nt = cos_t.shape[0] // tm
    tab = pl.BlockSpec((tm, cos_t.shape[1]), lambda i, c: (i % nt, 0))
    return [tab, tab, pl.BlockSpec(swap.shape, lambda i, c: (0, 0))]


def _rms_fwd(name, x, g, out_dtype, rope=None):
    rows, cols = x.shape
    d = g.shape[1]
    tm = _row_tile(rows if rope is None else rope[0].shape[0], d)

    def body(x_ref, g_ref, *refs):
        xf = x_ref[...].astype(F32)
        r = lax.rsqrt(jnp.mean(xf * xf, axis=-1, keepdims=True) + EPS)
        y = xf * r * g_ref[...]
        if rope is not None:
            c_ref, s_ref, p_ref = refs[:3]
            y = y * c_ref[...] + _swap_halves(y, p_ref[...]) * s_ref[...]
        refs[-1][...] = y.astype(refs[-1].dtype)

    return _chained(
        body, name=name, grid=(rows // tm, cols // d),
        in_specs=[pl.BlockSpec((tm, d), lambda i, c: (i, c)), pl.BlockSpec((1, d), lambda i, c: (0, 0))]
        + (_rope_specs(rope, tm) if rope is not None else []),
        out_specs=pl.BlockSpec((tm, d), lambda i, c: (i, c)),
        out_shape=jax.ShapeDtypeStruct((rows, cols), out_dtype),
        compiler_params=_rowwise_params("parallel", "parallel"),
    )(x, g, *(rope or ()))


def _rms_bwd(name, x, g, dy, res=None, rope=None, narrow=False):
    rows, cols = x.shape
    d = g.shape[1]
    tm = _row_tile(rows if rope is None else rope[0].shape[0], d)

    def body(*refs):
        x_ref, g_ref, dy_ref = refs[:3]
        n_out = 3 if narrow else 2
        dx_ref, dg_ref = refs[-n_out], refs[-1]
        r_ref = refs[3] if res is not None else None
        xf = x_ref[...].astype(F32)
        r = lax.rsqrt(jnp.mean(xf * xf, axis=-1, keepdims=True) + EPS)
        xhat = xf * r
        dyf = dy_ref[...].astype(F32)
        if rope is not None:
            c_ref, s_ref, p_ref = refs[-n_out - 3:-n_out]
            dyf = dyf * c_ref[...] + _swap_halves(dyf * s_ref[...], p_ref[...])

        @pl.when((pl.program_id(0) == 0) & (pl.program_id(1) == 0))
        def _():
            dg_ref[...] = jnp.zeros_like(dg_ref)

        dg_ref[...] += jnp.sum(dyf * xhat, axis=0, keepdims=True)
        dxh = dyf * g_ref[...]
        dx = r * (dxh - xhat * jnp.mean(dxh * xhat, axis=-1, keepdims=True))
        if r_ref is not None:
            dx = dx + r_ref[...].astype(F32)
        dx_ref[...] = dx
        if narrow:
            refs[-2][...] = dx.astype(BF16)

    blk = pl.BlockSpec((tm, d), lambda i, c: (i, c))
    in_specs = [blk, pl.BlockSpec((1, d), lambda i, c: (0, 0)), blk]
    operands = [x, g, dy]
    if res is not None:
        in_specs.append(blk)
        operands.append(res)
    if rope is not None:
        in_specs += _rope_specs(rope, tm)
        operands += list(rope)
    wide = [(blk, jax.ShapeDtypeStruct((rows, cols), F32))] + ([(blk, jax.ShapeDtypeStruct((rows, cols), BF16))] if narrow else [])
    outs = wide + [(pl.BlockSpec((1, d), lambda i, c: (0, 0)), jax.ShapeDtypeStruct((1, d), F32))]
    return _chained(
        body, name=name, grid=(rows // tm, cols // d), in_specs=in_specs,
        out_specs=tuple(spec for spec, _ in outs), out_shape=tuple(sds for _, sds in outs),
        compiler_params=_rowwise_params("arbitrary", "arbitrary"),
    )(*operands)


def _swiglu_fwd(name, g, u, out_dtype, tm=256):
    rows, cols = g.shape
    tm = _tile(rows, tm)

    def body(g_ref, u_ref, o_ref):
        gf = g_ref[...].astype(F32)
        o_ref[...] = (gf * jax.nn.sigmoid(gf) * u_ref[...].astype(F32)).astype(o_ref.dtype)

    blk = pl.BlockSpec((tm, cols), lambda i: (i, 0))
    return _chained(
        body, name=name, grid=(rows // tm,), in_specs=[blk, blk], out_specs=blk,
        out_shape=jax.ShapeDtypeStruct((rows, cols), out_dtype),
        compiler_params=_rowwise_params("parallel"),
    )(g, u)


def _swiglu_bwd(name, da, g, u, out_dtype, tm=256):
    rows, cols = g.shape
    tm = _tile(rows, tm)

    def body(da_ref, g_ref, u_ref, dg_ref, du_ref):
        gf = g_ref[...].astype(F32)
        daf = da_ref[...].astype(F32)
        sig = jax.nn.sigmoid(gf)
        du_ref[...] = (daf * gf * sig).astype(du_ref.dtype)
        dg_ref[...] = (daf * u_ref[...].astype(F32) * sig * (1.0 + gf * (1.0 - sig))).astype(dg_ref.dtype)

    blk = pl.BlockSpec((tm, cols), lambda i: (i, 0))
    sds = jax.ShapeDtypeStruct((rows, cols), out_dtype)
    return _chained(
        body, name=name, grid=(rows // tm,), in_specs=[blk, blk, blk], out_specs=(blk, blk), out_shape=(sds, sds),
        compiler_params=_rowwise_params("parallel"),
    )(da, g, u)


def _gate_fwd(name, pre, bias, tm=256):
    rows, cols = pre.shape
    tm = _tile(rows, tm)

    def body(p_ref, b_ref, o_ref):
        z = p_ref[...] + b_ref[...]
        o_ref[...] = (jnp.minimum(z, 0.0) - jnp.log(1.0 + jnp.exp(-jnp.abs(z)))) * (1.0 / GLA_TAU)

    blk = pl.BlockSpec((tm, cols), lambda i: (i, 0))
    return _chained(
        body, name=name, grid=(rows // tm,), in_specs=[blk, pl.BlockSpec((1, cols), lambda i: (0, 0))], out_specs=blk,
        out_shape=jax.ShapeDtypeStruct((rows, cols), F32),
        compiler_params=_rowwise_params("parallel"),
    )(pre, bias)


def _gate_bwd(name, pre, bias, dla, tm=256):
    rows, cols = pre.shape
    tm = _tile(rows, tm)

    def body(p_ref, b_ref, d_ref, dp_ref, db_ref):
        z = p_ref[...] + b_ref[...]
        dp = d_ref[...] * (1.0 / GLA_TAU) / (1.0 + jnp.exp(z))
        dp_ref[...] = dp

        @pl.when(pl.program_id(0) == 0)
        def _():
            db_ref[...] = jnp.zeros_like(db_ref)

        db_ref[...] += jnp.sum(dp, axis=0, keepdims=True)

    blk = pl.BlockSpec((tm, cols), lambda i: (i, 0))
    row = pl.BlockSpec((1, cols), lambda i: (0, 0))
    return _chained(
        body, name=name, grid=(rows // tm,), in_specs=[blk, row, blk], out_specs=(blk, row),
        out_shape=(jax.ShapeDtypeStruct((rows, cols), F32), jax.ShapeDtypeStruct((1, cols), F32)),
        compiler_params=_rowwise_params("arbitrary"),
    )(pre, bias, dla)


def _loss(name, y, target, tm=256):
    rows, d = y.shape
    tm = _tile(rows, tm)

    def body(y_ref, t_ref, dy_ref, dyb_ref, l_ref):
        err = y_ref[...] - t_ref[...]
        dy_ref[...] = err * (1.0 / d)
        dyb_ref[...] = (err * (1.0 / d)).astype(BF16)

        @pl.when(pl.program_id(0) == 0)
        def _():
            l_ref[...] = jnp.zeros_like(l_ref)

        sq = (err * err).reshape(tm // 8, 8, d)
        l_ref[...] += jnp.sum(sq, axis=0) * (0.5 / d)

    blk = pl.BlockSpec((tm, d), lambda i: (i, 0))
    return _chained(
        body, name=name, grid=(rows // tm,), in_specs=[blk, blk],
        out_specs=(blk, blk, pl.BlockSpec((8, d), lambda i: (0, 0))),
        out_shape=(jax.ShapeDtypeStruct((rows, d), F32), jax.ShapeDtypeStruct((rows, d), BF16),
                   jax.ShapeDtypeStruct((8, d), F32)),
        compiler_params=_rowwise_params("arbitrary"),
    )(y, target)


def _per_query_block(causal, nq, tq, tk, inner):
    if not causal:
        inner(0, tk)
        return
    for block in range(nq):
        pl.when(pl.program_id(1) == block)(functools.partial(inner, block * tq, (block + 1) * tq))


def _scores(q, k, scale, causal, q0):
    s = lax.dot_general(q, k, NT, preferred_element_type=F32) * scale
    if causal:
        qc = (q0 + lax.broadcasted_iota(jnp.int32, s.shape, 0)) // CHUNK
        kc = lax.broadcasted_iota(jnp.int32, s.shape, 1) // CHUNK
        s = jnp.where(kc <= qc, s, -1e30)
    e = jnp.exp(s - jnp.max(s, axis=-1, keepdims=True))
    return e, jnp.sum(e, axis=-1, keepdims=True)


def _attn_fwd(name, q, k, v, scale, causal, tq=256):
    nh, t, dk = q.shape
    tk, dv = k.shape[1], v.shape[2]
    tq = _tile(t, tq)

    def body(q_ref, k_ref, v_ref, o_ref):
        def inner(q0, kv):
            e, l = _scores(q_ref[...], k_ref[0:kv, :], scale, causal, q0)
            o = jnp.dot(e.astype(BF16), v_ref[0:kv, :], preferred_element_type=F32)
            o_ref[...] = (o / l).astype(o_ref.dtype)

        _per_query_block(causal, t // tq, tq, tk, inner)

    return _chained(
        body, name=name, grid=(nh, t // tq),
        in_specs=[pl.BlockSpec((None, tq, dk), lambda h, i: (h, i, 0)), pl.BlockSpec((None, tk, dk), lambda h, i: (h, 0, 0)),
                  pl.BlockSpec((None, tk, dv), lambda h, i: (h, 0, 0))],
        out_specs=pl.BlockSpec((tq, dv), lambda h, i: (i, h)),
        out_shape=jax.ShapeDtypeStruct((t, nh * dv), BF16),
        compiler_params=pltpu.CompilerParams(dimension_semantics=("parallel", "parallel"),
                                             vmem_limit_bytes=_vmem_limit(0, 6 * tq * tk * 4)),
    )(q, k, v)


def _attn_bwd(name, q, k, v, do, scale, causal, tq=256):
    nh, t, dk = q.shape
    tk, dv = k.shape[1], v.shape[2]
    tq = _tile(t, tq)

    def body(q_ref, k_ref, v_ref, do_ref, dq_ref, dk_ref, dv_ref):
        @pl.when(pl.program_id(1) == 0)
        def _():
            dk_ref[...] = jnp.zeros_like(dk_ref)
            dv_ref[...] = jnp.zeros_like(dv_ref)

        def inner(q0, kv):
            qb, kb = q_ref[...], k_ref[0:kv, :]
            e, l = _scores(qb, kb, scale, causal, q0)
            p = e / l
            dob = do_ref[...].astype(BF16)
            dp = lax.dot_general(dob, v_ref[0:kv, :], NT, preferred_element_type=F32)
            ds = (p * (dp - jnp.sum(p * dp, axis=-1, keepdims=True)) * scale).astype(BF16)
            dq_ref[...] = jnp.dot(ds, kb, preferred_element_type=F32)
            dk_ref[0:kv, :] += lax.dot_general(ds, qb, TN, preferred_element_type=F32)
            dv_ref[0:kv, :] += lax.dot_general(p.astype(BF16), dob, TN, preferred_element_type=F32)

        _per_query_block(causal, t // tq, tq, tk, inner)

    return _chained(
        body, name=name, grid=(nh, t // tq),
        in_specs=[pl.BlockSpec((None, tq, dk), lambda h, i: (h, i, 0)), pl.BlockSpec((None, tk, dk), lambda h, i: (h, 0, 0)),
                  pl.BlockSpec((None, tk, dv), lambda h, i: (h, 0, 0)), pl.BlockSpec((tq, dv), lambda h, i: (i, h))],
        out_specs=(pl.BlockSpec((None, tq, dk), lambda h, i: (h, i, 0)), pl.BlockSpec((None, tk, dk), lambda h, i: (h, 0, 0)),
                   pl.BlockSpec((None, tk, dv), lambda h, i: (h, 0, 0))),
        out_shape=(jax.ShapeDtypeStruct((nh, t, dk), F32), jax.ShapeDtypeStruct((nh, tk, dk), F32),
                   jax.ShapeDtypeStruct((nh, tk, dv), F32)),
        compiler_params=pltpu.CompilerParams(dimension_semantics=("parallel", "arbitrary"),
                                             vmem_limit_bytes=_vmem_limit(0, 10 * tq * tk * 4)),
    )(q, k, v, do)


def _tri(lower):
    r = lax.broadcasted_iota(jnp.int32, (CHUNK, CHUNK), 0)
    c = lax.broadcasted_iota(jnp.int32, (CHUNK, CHUNK), 1)
    return jnp.where((c <= r) if lower else (c >= r), 1.0, 0.0).astype(BF16)


def _tri_dot(tri, x):
    return sum(jnp.dot(tri, part, preferred_element_type=F32) for part in _split3(x))


def _gla_fwd(name, q, k, v, la, nh):
    t = q.shape[0]
    dk, dv = q.shape[1] // nh, v.shape[1] // nh
    nc = t // CHUNK

    def body(q_ref, k_ref, v_ref, g_ref, o_ref, st_ref, state):
        @pl.when(pl.program_id(0) == 0)
        def _():
            state[...] = jnp.zeros_like(state)

        g = g_ref[...]
        b = _tri_dot(_tri(True), g)
        b_end = jnp.sum(g, axis=0, keepdims=True)
        k_dec = (k_ref[...] * jnp.exp(b_end - b)).astype(BF16)
        decay = jnp.exp(b_end)
        qc = (q_ref[...] * (dk ** -0.5)).astype(BF16)
        vb = v_ref[...].astype(BF16)
        for h in range(nh):
            ks, vs = slice(h * dk, (h + 1) * dk), slice(h * dv, (h + 1) * dv)
            u_t = lax.dot_general(vb[:, vs], k_dec[:, ks], TN, preferred_element_type=F32)
            new = state[h] * decay[:, ks] + u_t
            state[h] = new
            st_ref[h] = new
            o_ref[:, vs] = lax.dot_general(qc[:, ks], new.astype(BF16), NT, preferred_element_type=F32)

    kblk = pl.BlockSpec((CHUNK, nh * dk), lambda n: (n, 0))
    vblk = pl.BlockSpec((CHUNK, nh * dv), lambda n: (n, 0))
    return _chained(
        body, name=name, grid=(nc,), in_specs=[kblk, kblk, vblk, kblk],
        out_specs=(vblk, pl.BlockSpec((nh, None, dv, dk), lambda n: (0, n, 0, 0))),
        out_shape=(jax.ShapeDtypeStruct((t, nh * dv), F32), jax.ShapeDtypeStruct((nh, nc, dv, dk), F32)),
        scratch_shapes=[pltpu.VMEM((nh, dv, dk), F32)],
        compiler_params=_rowwise_params("arbitrary"),
    )(q, k, v, la)


def _gla_bwd(name, q, k, v, la, states, do, nh):
    t = q.shape[0]
    dk, dv = q.shape[1] // nh, v.shape[1] // nh
    nc = t // CHUNK
    scale = dk ** -0.5

    def body(q_ref, k_ref, v_ref, g_ref, do_ref, st_ref, sp_ref, dq_ref, dk_ref, dv_ref, dg_ref, carry):
        i = pl.program_id(0)

        @pl.when(i == 0)
        def _():
            carry[...] = jnp.zeros_like(carry)

        g = g_ref[...]
        b = _tri_dot(_tri(True), g)
        b_end = jnp.sum(g, axis=0, keepdims=True)
        w = jnp.exp(b_end - b)
        decay = jnp.exp(b_end)
        k_dec = k_ref[...] * w
        k_decb = k_dec.astype(BF16)
        qc = (q_ref[...] * scale).astype(BF16)
        dob = do_ref[...].astype(BF16)
        vb = v_ref[...].astype(BF16)
        dk_dec, ddecay = [], []
        for h in range(nh):
            ks, vs = slice(h * dk, (h + 1) * dk), slice(h * dv, (h + 1) * dv)
            dq_ref[:, ks] = jnp.dot(dob[:, vs], st_ref[h].astype(BF16), preferred_element_type=F32) * scale
            g_t = carry[h] + lax.dot_general(dob[:, vs], qc[:, ks], TN, preferred_element_type=F32)
            g_tb = g_t.astype(BF16)
            dk_dec.append(jnp.dot(vb[:, vs], g_tb, preferred_element_type=F32))
            dv_ref[:, vs] = lax.dot_general(k_decb[:, ks], g_tb, NT, preferred_element_type=F32)
            prev = jnp.where(i < nc - 1, sp_ref[h], 0.0)
            ddecay.append(jnp.sum(g_t * prev, axis=0, keepdims=True))
            carry[h] = g_t * decay[:, ks]
        dk_dec = jnp.concatenate(dk_dec, axis=1)
        dk_ref[...] = dk_dec * w
        e = dk_dec * k_dec
        db_end = jnp.sum(e, axis=0, keepdims=True) + jnp.concatenate(ddecay, axis=1) * decay
        dg_ref[...] = _tri_dot(_tri(False), -e) + db_end

    kblk = pl.BlockSpec((CHUNK, nh * dk), lambda i: (nc - 1 - i, 0))
    vblk = pl.BlockSpec((CHUNK, nh * dv), lambda i: (nc - 1 - i, 0))
    ksds = jax.ShapeDtypeStruct((t, nh * dk), F32)
    return _chained(
        body, name=name, grid=(nc,),
        in_specs=[kblk, kblk, vblk, kblk, vblk,
                  pl.BlockSpec((nh, None, dv, dk), lambda i: (0, nc - 1 - i, 0, 0)),
                  pl.BlockSpec((nh, None, dv, dk), lambda i: (0, jnp.maximum(nc - 2 - i, 0), 0, 0))],
        out_specs=(kblk, kblk, vblk, kblk),
        out_shape=(ksds, ksds, jax.ShapeDtypeStruct((t, nh * dv), F32), ksds),
        scratch_shapes=[pltpu.VMEM((nh, dv, dk), F32)],
        compiler_params=_rowwise_params("arbitrary"),
    )(q, k, v, la, do, states, states)


def _adamw(name, w, m, v, parts):
    rows, cols = w.shape
    tm = 1 << int(math.log2(max(8, (1 << 18) // (-(-cols // LANES) * LANES))))
    while rows % tm and tm > 8:
        tm //= 2
    tm = _tile(rows, tm)

    def body(w_ref, m_ref, v_ref, p_ref, g_ref, d_ref, nm_ref, nv_ref):
        g = p_ref[0].astype(F32)
        for s in range(1, parts.shape[0]):
            g = g + p_ref[s].astype(F32)
        m_new = ADAM_B1 * m_ref[...] + (1.0 - ADAM_B1) * g
        v_new = ADAM_B2 * v_ref[...] + (1.0 - ADAM_B2) * jnp.square(g)
        m_hat = m_new / (1.0 - ADAM_B1 ** ADAM_STEP)
        v_hat = v_new / (1.0 - ADAM_B2 ** ADAM_STEP)
        g_ref[...] = g
        d_ref[...] = -ADAM_LR * (m_hat / (jnp.sqrt(v_hat) + ADAM_EPS) + ADAM_WD * w_ref[...])
        nm_ref[...] = m_new
        nv_ref[...] = v_new

    blk = pl.BlockSpec((tm, cols), lambda i: (i, 0))
    sds = jax.ShapeDtypeStruct((rows, cols), F32)
    return _chained(
        body, name=name, grid=(rows // tm,),
        in_specs=[blk, blk, blk, pl.BlockSpec((parts.shape[0], tm, cols), lambda i: (0, i, 0))],
        out_specs=(blk, blk, blk, blk), out_shape=(sds, sds, sds, sds),
        compiler_params=_rowwise_params("parallel"),
    )(w, m, v, parts)


HBM = pl.BlockSpec(memory_space=pl.ANY)
MESH = pl.DeviceIdType.MESH


def _all_gather(name, shards):
    n = len(shards)

    def body(*refs):
        ins, outs = refs[:n], refs[n:2 * n]
        send_sems, recv_sems, local_sems = refs[2 * n:]
        x, y, c = lax.axis_index("x"), lax.axis_index("y"), lax.axis_index("c")
        me, sibling = (x, y, c), (x, y, 1 - c)
        chips = [(1 - x, y), (x, 1 - y), (1 - x, 1 - y)]

        def copy(w, k, block, to, src=None):
            dst = outs[w].at[4 * block[0] + 2 * block[1] + block[2]]
            return pltpu.make_async_remote_copy(
                src_ref=dst if src is None else src, dst_ref=dst, send_sem=send_sems.at[7 * w + k],
                recv_sem=recv_sems.at[7 * w + k], device_id=to, device_id_type=MESH)

        mine, first, passed = [], [], []
        for w in range(n):
            cp = pltpu.make_async_copy(ins[w], outs[w].at[4 * x + 2 * y + c], local_sems.at[w])
            cp.start()
            mine.append(cp)
            first.append(copy(w, 0, me, sibling, src=ins[w]))
            first += [copy(w, 1 + j, me, (*chip, c), src=ins[w]) for j, chip in enumerate(chips)]
        for cp in first:
            cp.start()
        for w in range(n):
            for j, chip in enumerate(chips):
                copy(w, 1 + j, (*chip, c), me).wait_recv()
                cp = copy(w, 4 + j, (*chip, c), sibling)
                cp.start()
                passed.append(cp)
        for w in range(n):
            copy(w, 0, sibling, me).wait_recv()
            for j, chip in enumerate(chips):
                copy(w, 4 + j, (*chip, 1 - c), me).wait_recv()
        for cp in first + passed:
            cp.wait_send()
        for cp in mine:
            cp.wait()

    return _chained(
        body, name=name, in_specs=[HBM] * n, out_specs=[HBM] * n,
        out_shape=[jax.ShapeDtypeStruct((N_DEV,) + s.shape, s.dtype) for s in shards],
        scratch_shapes=[pltpu.SemaphoreType.DMA((7 * n,)), pltpu.SemaphoreType.DMA((7 * n,)),
                        pltpu.SemaphoreType.DMA((n,))],
    )(*shards)


HBM_SPEC = pl.BlockSpec(memory_space=pltpu.HBM)
SEM_SPEC = pl.BlockSpec(memory_space=pltpu.SEMAPHORE)
EFFECT = pltpu.SideEffectType.DATAFLOW_SIDE_EFFECTING
TOKEN = jax.ShapeDtypeStruct((8, LANES), F32)


def _in_hbm(a):
    return pltpu.with_memory_space_constraint(a, pltpu.HBM)


def _place():
    x, y, c = lax.axis_index("x"), lax.axis_index("y"), lax.axis_index("c")
    chips = [(1 - x, y), (x, 1 - y), (1 - x, 1 - y)]
    return x, y, c, chips


def _block_of(px, py, pc):
    return 4 * px + 2 * py + pc


def _gather_copies(ins, outs, sems, w):
    send_sems, recv_sems, local_sems = sems
    x, y, c, chips = _place()
    peers = [(x, y, 1 - c)] + [(*chip, c) for chip in chips]

    def copy(k, block, to, src):
        dst = outs[w].at[_block_of(*block)]
        return pltpu.make_async_remote_copy(src_ref=dst if src is None else src, dst_ref=dst, send_sem=send_sems.at[4 * w + k],
                                            recv_sem=recv_sems.at[4 * w + k], device_id=to, device_id_type=MESH)

    local = pltpu.make_async_copy(ins[w], outs[w].at[_block_of(x, y, c)], local_sems.at[w])
    sends = [copy(k, (x, y, c), peer, ins[w]) for k, peer in enumerate(peers)]
    recvs = [copy(k, peer, (x, y, c), None) for k, peer in enumerate(peers)]
    return local, sends, recvs


def _forward_copies(outs, sems, w):
    send_sems, recv_sems = sems
    x, y, c, chips = _place()

    def copy(j, block, to):
        dst = outs[w].at[_block_of(*block)]
        return pltpu.make_async_remote_copy(src_ref=dst, dst_ref=dst, send_sem=send_sems.at[3 * w + j],
                                            recv_sem=recv_sems.at[3 * w + j], device_id=to, device_id_type=MESH)

    sends = [copy(j, (*chip, c), (x, y, 1 - c)) for j, chip in enumerate(chips)]
    recvs = [copy(j, (*chip, 1 - c), (x, y, c)) for j, chip in enumerate(chips)]
    return sends, recvs


def _gather_start(name, shards):
    n = len(shards)

    def body(*refs):
        ins, outs, sems = refs[:n], refs[n:2 * n], refs[2 * n:2 * n + 3]
        for w in range(n):
            local, sends, _ = _gather_copies(ins, outs, sems, w)
            local.start()
            for cp in sends:
                cp.start()
        refs[-1][...] = jnp.zeros_like(refs[-1])

    lands = [lax.empty((N_DEV,) + s.shape, s.dtype) for s in shards]
    res = _chained(
        body, name=name, link=-1, in_specs=[HBM_SPEC] * (2 * n),
        out_specs=[SEM_SPEC] * 3 + [HBM_SPEC] * (2 * n) + [pl.BlockSpec(memory_space=pltpu.VMEM)],
        out_shape=[pltpu.SemaphoreType.DMA((4 * n,)), pltpu.SemaphoreType.DMA((4 * n,)), pltpu.SemaphoreType.DMA((n,))]
        + [pltpu.HBM(s.shape, s.dtype) for s in shards] + [pltpu.HBM(l.shape, l.dtype) for l in lands] + [TOKEN],
        input_output_aliases={i: 3 + i for i in range(2 * n)},
        compiler_params=pltpu.CompilerParams(has_side_effects=EFFECT),
    )(*[_in_hbm(s) for s in shards], *[_in_hbm(l) for l in lands])
    return n, res[:3], res[3:3 + n], res[3 + n:3 + 2 * n]


def _gather_mid(name, state):
    n, sems, shards, lands = state

    def body(*refs):
        ins, outs, sems_in = refs[:n], refs[n:2 * n], refs[2 * n:2 * n + 3]
        sems_out = refs[2 * n + 3:2 * n + 5]
        for w in range(n):
            local, sends, recvs = _gather_copies(ins, outs, sems_in, w)
            local.wait()
            for cp in sends:
                cp.wait_send()
            for cp in recvs:
                cp.wait_recv()
            for cp in _forward_copies(outs, sems_out, w)[0]:
                cp.start()
        refs[-1][...] = jnp.zeros_like(refs[-1])

    res = _chained(
        body, name=name, link=-1, in_specs=[HBM_SPEC] * (2 * n) + [SEM_SPEC] * 3,
        out_specs=[SEM_SPEC] * 2 + [HBM_SPEC] * n + [pl.BlockSpec(memory_space=pltpu.VMEM)],
        out_shape=[pltpu.SemaphoreType.DMA((3 * n,)), pltpu.SemaphoreType.DMA((3 * n,))]
        + [pltpu.HBM(l.shape, l.dtype) for l in lands] + [TOKEN],
        input_output_aliases={n + i: 2 + i for i in range(n)},
        compiler_params=pltpu.CompilerParams(has_side_effects=EFFECT),
    )(*shards, *lands, *sems)
    return n, res[:2], res[2:2 + n]


def _gather_end(name, state):
    n, sems, lands = state

    def body(*refs):
        outs, sems_in = refs[:n], refs[n:n + 2]
        for w in range(n):
            sends, recvs = _forward_copies(outs, sems_in, w)
            for cp in sends:
                cp.wait_send()
            for cp in recvs:
                cp.wait_recv()

    return _chained(
        body, name=name, in_specs=[HBM_SPEC] * n + [SEM_SPEC] * 2, out_specs=[HBM_SPEC] * n,
        out_shape=[pltpu.HBM(l.shape, l.dtype) for l in lands], input_output_aliases={i: i for i in range(n)},
        compiler_params=pltpu.CompilerParams(has_side_effects=EFFECT),
    )(*lands, *sems)


def _exchange_copies(ins, outs, sems, w):
    send_sems, recv_sems, local_sems = sems
    x, y, c, _ = _place()
    mine = _block_of(x, y, c)
    local = pltpu.make_async_copy(ins[w].at[mine], outs[w].at[mine], local_sems.at[w])
    remote = []
    for k in range(1, N_DEV):
        px, py, pc = x ^ (k >> 2), y ^ ((k >> 1) & 1), c ^ (k & 1)
        remote.append(pltpu.make_async_remote_copy(
            src_ref=ins[w].at[_block_of(px, py, pc)], dst_ref=outs[w].at[mine], send_sem=send_sems.at[7 * w + k - 1],
            recv_sem=recv_sems.at[7 * w + k - 1], device_id=(px, py, pc), device_id_type=MESH))
    return local, remote


def _exchange_start(name, stacks):
    n = len(stacks)

    def body(*refs):
        ins, outs, sems = refs[:n], refs[n:2 * n], refs[2 * n:2 * n + 3]
        for w in range(n):
            local, remote = _exchange_copies(ins, outs, sems, w)
            local.start()
            for cp in remote:
                cp.start()
        refs[-1][...] = jnp.zeros_like(refs[-1])

    lands = [lax.empty(s.shape, s.dtype) for s in stacks]
    if any(s is _Chain.last for s in stacks):
        _Chain.last = None
    res = _chained(
        body, name=name, link=-1, in_specs=[HBM_SPEC] * (2 * n),
        out_specs=[SEM_SPEC] * 3 + [HBM_SPEC] * (2 * n) + [pl.BlockSpec(memory_space=pltpu.VMEM)],
        out_shape=[pltpu.SemaphoreType.DMA((7 * n,)), pltpu.SemaphoreType.DMA((7 * n,)), pltpu.SemaphoreType.DMA((n,))]
        + [pltpu.HBM(s.shape, s.dtype) for s in stacks] * 2 + [TOKEN],
        input_output_aliases={i: 3 + i for i in range(2 * n)},
        compiler_params=pltpu.CompilerParams(has_side_effects=EFFECT),
    )(*[_in_hbm(s) for s in stacks], *[_in_hbm(l) for l in lands])
    return n, res[:3], res[3:3 + n], res[3 + n:3 + 2 * n]


def _exchange_wait(name, state):
    n, sems, stacks, lands = state

    def body(*refs):
        ins, outs, sems_in = refs[:n], refs[n:2 * n], refs[2 * n:2 * n + 3]
        for w in range(n):
            local, remote = _exchange_copies(ins, outs, sems_in, w)
            local.wait()
            for cp in remote:
                cp.wait_send()
                cp.wait_recv()

    return _chained(
        body, name=name, in_specs=[HBM_SPEC] * (2 * n) + [SEM_SPEC] * 3, out_specs=[HBM_SPEC] * n,
        out_shape=[pltpu.HBM(l.shape, l.dtype) for l in lands], input_output_aliases={n + i: i for i in range(n)},
        compiler_params=pltpu.CompilerParams(has_side_effects=EFFECT),
    )(*stacks, *lands, *sems)


N_CHIP = N_DEV // 2


def _pair_copies(stack, own, land, sems):
    send_sems, recv_sems, local_sems = sems
    x, y, c, _ = _place()
    local = [pltpu.make_async_copy(stack.at[2 * k + c], own.at[k], local_sems.at[k]) for k in range(N_CHIP)]
    remote = [pltpu.make_async_remote_copy(src_ref=stack.at[2 * k + 1 - c], dst_ref=land.at[k], send_sem=send_sems.at[k],
                                           recv_sem=recv_sems.at[k], device_id=(x, y, 1 - c), device_id_type=MESH)
              for k in range(N_CHIP)]
    return local, remote


def _chip_copies(pairs, land, sems):
    send_sems, recv_sems, local_sems = sems
    x, y, c, _ = _place()
    mine = 2 * x + y
    local = pltpu.make_async_copy(pairs.at[mine], land.at[mine], local_sems.at[0])
    remote = []
    for m in range(1, N_CHIP):
        px, py = x ^ (m >> 1), y ^ (m & 1)
        remote.append(pltpu.make_async_remote_copy(
            src_ref=pairs.at[2 * px + py], dst_ref=land.at[mine], send_sem=send_sems.at[m - 1],
            recv_sem=recv_sems.at[m - 1], device_id=(px, py, c), device_id_type=MESH))
    return local, remote


def _stage_start(name, copies_of, src_arr, n_land, n_sems):
    half = (N_CHIP,) + src_arr.shape[1:]

    def body(*refs):
        local, remote = copies_of(refs[0], *refs[1:1 + n_land], refs[1 + n_land:4 + n_land])
        for cp in (local if isinstance(local, list) else [local]) + remote:
            cp.start()
        refs[-1][...] = jnp.zeros_like(refs[-1])

    if src_arr is _Chain.last:
        _Chain.last = None
    lands = [lax.empty(half, src_arr.dtype) for _ in range(n_land)]
    res = _chained(
        body, name=name, link=-1, in_specs=[HBM_SPEC] * (1 + n_land),
        out_specs=[SEM_SPEC] * 3 + [HBM_SPEC] * (1 + n_land) + [pl.BlockSpec(memory_space=pltpu.VMEM)],
        out_shape=[pltpu.SemaphoreType.DMA((n,)) for n in n_sems] + [pltpu.HBM(src_arr.shape, src_arr.dtype)]
        + [pltpu.HBM(half, src_arr.dtype)] * n_land + [TOKEN],
        input_output_aliases={i: 3 + i for i in range(1 + n_land)},
        compiler_params=pltpu.CompilerParams(has_side_effects=EFFECT),
    )(_in_hbm(src_arr), *[_in_hbm(l) for l in lands])
    return res[:3], res[3], res[4:4 + n_land]


def _stage_wait(name, copies_of, state):
    sems, src_arr, lands = state
    n_land = len(lands)

    def body(*refs):
        local, remote = copies_of(refs[0], *refs[1:1 + n_land], refs[1 + n_land:4 + n_land])
        for cp in (local if isinstance(local, list) else [local]):
            cp.wait()
        for cp in remote:
            cp.wait_send()
            cp.wait_recv()

    return _chained(
        body, name=name, in_specs=[HBM_SPEC] * (1 + n_land) + [SEM_SPEC] * 3, out_specs=[HBM_SPEC] * n_land,
        out_shape=[pltpu.HBM(l.shape, l.dtype) for l in lands],
        input_output_aliases={1 + i: i for i in range(n_land)},
        compiler_params=pltpu.CompilerParams(has_side_effects=EFFECT),
    )(src_arr, *lands, *sems)


def _pair_sum(name, own, land, tm=64):
    n, rows, cols = own.shape
    tm = _tile(rows, tm)

    def body(a_ref, b_ref, o_ref):
        o_ref[...] = (a_ref[...].astype(F32) + b_ref[...].astype(F32)).astype(o_ref.dtype)

    blk = pl.BlockSpec((n, tm, cols), lambda i: (0, i, 0))
    return _chained(body, name=name, grid=(rows // tm,), in_specs=[blk, blk], out_specs=blk,
                    out_shape=jax.ShapeDtypeStruct(own.shape, own.dtype), compiler_params=_rowwise_params("parallel"))(own, land)


def _unstack_cols(w):
    return w.transpose(1, 0, 2).reshape(w.shape[1], N_DEV * w.shape[2])


def _stack_cols(w):
    return w.reshape(w.shape[0], N_DEV, w.shape[1] // N_DEV).transpose(1, 0, 2)


def _rope_tables(positions):
    half = MLA_ROPE // 2
    inv_freq = ROPE_THETA ** (-jnp.arange(half, dtype=F32) / half)
    ang = positions.astype(F32)[:, None] * inv_freq
    cos, sin = jnp.cos(ang), jnp.sin(ang)
    t = positions.shape[0]
    cos_t = jnp.concatenate([jnp.ones((t, MLA_NOPE), F32), cos, cos], axis=1)
    sin_t = jnp.concatenate([jnp.zeros((t, MLA_NOPE), F32), -sin, sin], axis=1)
    idx = jnp.arange(MLA_QK)
    partner = jnp.where(idx < MLA_NOPE, -1, jnp.where(idx < MLA_NOPE + half, idx + half, idx - half))
    swap = (idx[:, None] == partner[None, :]).astype(BF16)
    return cos_t, sin_t, swap


def _ffn_fwd(tag, x, gain, wt):
    h = _rms_fwd(tag + "_rms", x, gain, BF16)
    g = _mm_stack_nt_out(tag + "_gate", h, wt[tag + '_w_gate'], BF16)
    u, a = _mm_stack_nt_out(tag + "_up", h, wt[tag + '_w_up'], BF16,
                            fuse=(lambda u_blk, g_blk: (u_blk, g_blk * jax.nn.sigmoid(g_blk) * u_blk), [g]))
    y = _mm_stack_sum(tag + "_down", a, wt[tag + '_w_down'], F32, scale=0.5, res=x)
    return y, (x, h, g, u, a)


def _ffn_bwd_weights(tag, dy, saved, wd, comm):
    x, h, g, u, a = saved
    comm.grads({tag + '_w_down': _mm_stack_tn_right(tag + "_dwd", a, dy, BF16, scale=0.5)})

    def dact(da, g_blk, u_blk):
        sig = jax.nn.sigmoid(g_blk)
        return da * u_blk * sig * (1.0 + g_blk * (1.0 - sig)), da * g_blk * sig

    dg, du = _mm_stack_nt_out(tag + "_da", dy, wd, BF16, scale=0.5, fuse=(dact, [g, u]))
    comm.advance()
    comm.grads({tag + '_w_gate': _mm_stack_tn_right(tag + "_dwg", dg, h, BF16)})
    dwu = _mm_stack_tn_right(tag + "_dwu", du, h, BF16)
    comm.advance()
    comm.grads({tag + '_w_up': dwu})
    return dg, du


def _ffn_bwd_input(tag, dy, saved, dgu, gain, wg, wu, comm):
    dg, du = dgu
    dh = _mm_stack_sum(tag + "_dh", (dg, du), (wg, wu), F32)
    comm.advance()
    return _rms_bwd(tag + "_drms", saved[0], gain, dh, res=dy)


def _local_step(x, mem, positions, sm, comm, target):
    t, d = x.shape
    nm = mem.shape[0]
    gs, gw = {}, {}

    class Weights(dict):
        def __missing__(self, name):
            self.update(comm.weights(next(k for k, names in enumerate(GATHERS) if name in names)))
            return self[name]

    wt = Weights()

    x1, ffn1_saved = _ffn_fwd("ffn1", x, sm['ffn1_norm'], wt)

    h2 = _rms_fwd("mix_rms", x1, sm['mix_norm'], BF16)
    w_in_ref = _unstack_cols(wt['w_in'])
    pieces = []
    for name, n in IN_PAD:
        piece = w_in_ref[:, REF_OFF[name]:REF_OFF[name] + REF_SIZE[name]]
        if n != REF_SIZE[name]:
            piece = jnp.pad(piece, ((0, 0), (0, n - REF_SIZE[name])))
        pieces.append(piece)
    w_in = jnp.concatenate(pieces, axis=1)
    z = _mm2("mix_in", h2, w_in, NN, F32)
    zs = {name: z[:, PAD_OFF[name]:PAD_OFF[name] + n] for name, n in IN_PAD}

    cq = _rms_fwd("mla_q_a_rms", zs['zq'], sm['q_a_norm'], BF16)
    q_raw = _mm_stack_nt_out("mla_q_up", cq, wt['w_q_up'], F32)
    ckv = _rms_fwd("mla_kv_a_rms", zs['zkv'], sm['kv_a_norm'], BF16)
    kv = _mm_stack_out("mla_kv_up", ckv, wt['w_kv_up'], F32)
    zkr = zs['zkr'][:, :MLA_ROPE]
    k_raw = jnp.concatenate([kv[:, :, :MLA_NOPE], jnp.broadcast_to(zkr[None], (MLA_HEADS, t, MLA_ROPE))], axis=2)
    v_mla = kv[:, :, MLA_NOPE:].astype(BF16)
    cos_t, sin_t, swap = _rope_tables(positions)
    q_raw2, k_raw2 = q_raw.reshape(MLA_HEADS * t, MLA_QK), k_raw.reshape(MLA_HEADS * t, MLA_QK)
    rope = (cos_t, sin_t, swap)
    qf = _rms_fwd("mla_q_rms", q_raw2, sm['mla_q_norm'], BF16, rope=rope).reshape(MLA_HEADS, t, MLA_QK)
    kf = _rms_fwd("mla_k_rms", k_raw2, sm['mla_k_norm'], BF16, rope=rope).reshape(MLA_HEADS, t, MLA_QK)
    o_mla = _attn_fwd("mla_attn", qf, kf, v_mla, MLA_QK ** -0.5, True)

    w_g2 = jnp.pad(_unstack_cols(wt['gla_w_gate2']), ((0, LANES - GLA_GATE_RANK), (0, 0)))
    pre = _mm2("gla_gate_pre", zs['zg'], w_g2, NN, F32)
    log_a = _gate_fwd("gla_gate", pre, sm['gla_b_gate'])
    o_gla_raw, states = _gla_fwd("gla_scan", zs['gq'], zs['gk'], zs['gv'], log_a, GLA_HEADS)
    o_gla_n = _rms_fwd("gla_out_rms", o_gla_raw, sm['gla_out_norm'], F32)
    o_gla = _swiglu_fwd("gla_out_gate", zs['zr'], o_gla_n, BF16)

    cat = jnp.concatenate([o_mla, o_gla], axis=1)
    w_out = wt['w_out'].reshape(d, d)
    x2 = _mm2("mix_out", cat, w_out, NN, F32, res=x1)

    w_mq, w_mk, w_mv = (wt[n].reshape(d, MEM_HEADS * MEM_HEAD_DIM) for n in ('mem_w_q', 'mem_w_k', 'mem_w_v'))
    hq = _rms_fwd("mem_attn_rms", x2, sm['mem_attn_norm'], BF16)
    hm = _rms_fwd("mem_rms", mem, sm['mem_norm'], BF16)

    def heads_out(name, a, b, out_dtype):
        m, kk = a.shape
        tm = _tile(m, 512)
        return _mm(name, a, b, (m // tm, MEM_HEADS, 1), ((tm, kk), lambda i, h, k: (i, 0)),
                   ((kk, MEM_HEAD_DIM), lambda i, h, k: (0, h)), ((None, tm, MEM_HEAD_DIM), lambda i, h, k: (h, i, 0)),
                   (MEM_HEADS, m, MEM_HEAD_DIM), out_dtype, NN)

    mq_raw = heads_out("mem_q", hq, w_mq, F32)
    mk_raw = heads_out("mem_k", hm, w_mk, F32)
    mv = heads_out("mem_v", hm, w_mv, BF16)
    mq = _rms_fwd("mem_q_rms", mq_raw.reshape(MEM_HEADS * t, MEM_HEAD_DIM), sm['mem_q_norm'], BF16)
    mk = _rms_fwd("mem_k_rms", mk_raw.reshape(MEM_HEADS * nm, MEM_HEAD_DIM), sm['mem_k_norm'], BF16)
    mq, mk = mq.reshape(MEM_HEADS, t, MEM_HEAD_DIM), mk.reshape(MEM_HEADS, nm, MEM_HEAD_DIM)
    o_mem = _attn_fwd("mem_attn", mq, mk, mv, MEM_HEAD_DIM ** -0.5, False)
    w_mo = wt['mem_w_o']
    mo_cols = w_mo.shape[2]
    tm = _tile(t, 512)
    x3 = _mm("mem_out", o_mem, w_mo, (t // tm, N_DEV, 1), ((tm, o_mem.shape[1]), lambda i, j, k: (i, 0)),
             ((None, o_mem.shape[1], mo_cols), lambda i, j, k: (j, 0, 0)), ((tm, mo_cols), lambda i, j, k: (i, j)),
             (t, d), F32, NN, res=x2)

    y, ffn2_saved = _ffn_fwd("ffn2", x3, sm['ffn2_norm'], wt)
    dy, dy_narrow, loss_lanes = _loss("loss", y, target)

    dgu = _ffn_bwd_weights("ffn2", dy_narrow, ffn2_saved, wt['ffn2_w_down'], comm)
    dx3, gs['ffn2_norm'] = _ffn_bwd_input("ffn2", dy, ffn2_saved, dgu, sm['ffn2_norm'],
                                          wt['ffn2_w_gate'], wt['ffn2_w_up'], comm)

    do_mem = _mm("mem_do", dx3, w_mo, (t // tm, MEM_HEADS, N_DEV), ((tm, mo_cols), lambda i, h, k: (i, k)),
                 ((None, MEM_HEAD_DIM, mo_cols), lambda i, h, k: (k, h, 0)), ((tm, MEM_HEAD_DIM), lambda i, h, k: (i, h)),
                 (t, MEM_HEADS * MEM_HEAD_DIM), BF16, NT)
    tk = _tile(t, 512)
    gw['mem_w_o'] = _mm("mem_dwo", o_mem, dx3, (N_DEV, 1, t // tk), ((tk, o_mem.shape[1]), lambda j, i, k: (k, 0)),
                        ((tk, mo_cols), lambda j, i, k: (k, j)), ((None, o_mem.shape[1], mo_cols), lambda j, i, k: (j, 0, 0)),
                        w_mo.shape, BF16, TN)
    dmq, dmk, dmv = _attn_bwd("mem_dattn", mq, mk, mv, do_mem, MEM_HEAD_DIM ** -0.5, False)
    dmq_raw, gs['mem_q_norm'] = _rms_bwd("mem_q_drms", mq_raw.reshape(MEM_HEADS * t, MEM_HEAD_DIM), sm['mem_q_norm'],
                                         dmq.reshape(MEM_HEADS * t, MEM_HEAD_DIM))
    dmk_raw, gs['mem_k_norm'] = _rms_bwd("mem_k_drms", mk_raw.reshape(MEM_HEADS * nm, MEM_HEAD_DIM), sm['mem_k_norm'],
                                         dmk.reshape(MEM_HEADS * nm, MEM_HEAD_DIM))
    dmq_raw = dmq_raw.reshape(MEM_HEADS, t, MEM_HEAD_DIM)
    dmk_raw = dmk_raw.reshape(MEM_HEADS, nm, MEM_HEAD_DIM)

    def heads_in_nt(name, a, b, res=None):
        m, n = a.shape[1], b.shape[0]
        tm_, tn_ = _tile(m, 512), _tile(n, 1024)
        return _mm(name, a, b, (m // tm_, n // tn_, MEM_HEADS), ((None, tm_, MEM_HEAD_DIM), lambda i, j, k: (k, i, 0)),
                   ((tn_, MEM_HEAD_DIM), lambda i, j, k: (j, k)), ((tm_, tn_), lambda i, j, k: (i, j)), (m, n), F32, NT,
                   None, res)

    def heads_tn(name, a, b):
        m, kp = a.shape
        tm_, tk_ = _tile(kp, 1024), _tile(m, 512)
        return _mm(name, a, b, (kp // tm_, MEM_HEADS, m // tk_), ((tk_, tm_), lambda i, h, k: (k, i)),
                   ((None, tk_, MEM_HEAD_DIM), lambda i, h, k: (h, k, 0)), ((tm_, MEM_HEAD_DIM), lambda i, h, k: (i, h)),
                   (kp, MEM_HEADS * MEM_HEAD_DIM), BF16, TN)

    dhq = heads_in_nt("mem_dhq", dmq_raw, w_mq)
    gw['mem_w_q'] = heads_tn("mem_dwq", hq, dmq_raw).reshape(wt['mem_w_q'].shape)
    dhm = heads_in_nt("mem_dhm_k", dmk_raw, w_mk)
    dhm = heads_in_nt("mem_dhm_v", dmv, w_mv, res=dhm)
    gw['mem_w_k'] = heads_tn("mem_dwk", hm, dmk_raw).reshape(wt['mem_w_k'].shape)
    gw['mem_w_v'] = heads_tn("mem_dwv", hm, dmv).reshape(wt['mem_w_v'].shape)
    _, gs['mem_norm'] = _rms_bwd("mem_drms", mem, sm['mem_norm'], dhm)
    comm.grads({n: gw[n] for n in MEMORY})
    dx2, gs['mem_attn_norm'] = _rms_bwd("mem_attn_drms", x2, sm['mem_attn_norm'], dhq, res=dx3)

    dcat = _mm2("mix_dcat", dx2, w_out, NT, F32)
    comm.grads({'w_out': _mm2("mix_dwout", cat, dx2, TN, BF16, tm=1024, tn=2048, tk=512).reshape(wt['w_out'].shape)})
    do_mla, do_gla = dcat[:, :MLA_HEADS * MLA_V], dcat[:, MLA_HEADS * MLA_V:]

    dzr, dgn = _swiglu_bwd("gla_out_dgate", do_gla, zs['zr'], o_gla_n, F32)
    do_gla_raw, gs['gla_out_norm'] = _rms_bwd("gla_out_drms", o_gla_raw, sm['gla_out_norm'], dgn)
    dgq, dgk, dgv, dlog_a = _gla_bwd("gla_dscan", zs['gq'], zs['gk'], zs['gv'], log_a, states, do_gla_raw, GLA_HEADS)
    dpre, gs['gla_b_gate'] = _gate_bwd("gla_dgate", pre, sm['gla_b_gate'], dlog_a)
    dw_g2 = _mm2("gla_dwgate", zs['zg'], dpre, TN, BF16, tk=512)
    comm.grads({'gla_w_gate2': _stack_cols(dw_g2[:GLA_GATE_RANK])})
    dzg = _mm2("gla_dzg", dpre, w_g2, NT, F32)

    dqf, dkf, dv_mla = _attn_bwd("mla_dattn", qf, kf, v_mla, do_mla, MLA_QK ** -0.5, True)
    dq_raw, gs['mla_q_norm'] = _rms_bwd("mla_q_drms", q_raw2, sm['mla_q_norm'], dqf.reshape(MLA_HEADS * t, MLA_QK), rope=rope)
    dk_raw, gs['mla_k_norm'] = _rms_bwd("mla_k_drms", k_raw2, sm['mla_k_norm'], dkf.reshape(MLA_HEADS * t, MLA_QK), rope=rope)
    dq_raw = dq_raw.reshape(MLA_HEADS, t, MLA_QK)
    dk_raw = dk_raw.reshape(MLA_HEADS, t, MLA_QK)
    dkv = jnp.concatenate([dk_raw[:, :, :MLA_NOPE], dv_mla], axis=2)
    dzkr = jnp.sum(dk_raw[:, :, MLA_NOPE:], axis=0)
    comm.grads({'w_q_up': _mm_stack_tn_right("mla_dwq", dq_raw, cq, BF16),
                'w_kv_up': _mm_stack_tn_left("mla_dwkv", ckv, dkv, BF16)})
    dcq = _mm_stack_sum("mla_dcq", dq_raw, wt['w_q_up'], F32)
    dckv = _mm_stack_nt_sum("mla_dckv", dkv, wt['w_kv_up'], F32)
    dzq, gs['q_a_norm'] = _rms_bwd("mla_q_a_drms", zs['zq'], sm['q_a_norm'], dcq)
    dzkv, gs['kv_a_norm'] = _rms_bwd("mla_kv_a_drms", zs['zkv'], sm['kv_a_norm'], dckv)

    dzs = {'zq': dzq, 'zkv': dzkv, 'gq': dgq, 'gk': dgk, 'gv': dgv, 'zr': dzr,
           'zkr': jnp.pad(dzkr, ((0, 0), (0, LANES - MLA_ROPE))), 'zg': dzg}
    dz = jnp.concatenate([dzs[name].astype(BF16) for name, _ in IN_PAD], axis=1)
    dw_in = _mm2("mix_dwin", h2, dz, TN, BF16, tm=1024, tn=2048, tk=512)
    dw_in_ref = jnp.concatenate([dw_in[:, PAD_OFF[name]:PAD_OFF[name] + n] for name, n in IN_REF], axis=1)
    comm.grads({'w_in': _stack_cols(dw_in_ref)})
    dh2 = _mm2("mix_dh", dz, w_in, NT, F32)
    comm.advance()
    dx1, dx1_narrow, gs['mix_norm'] = _rms_bwd("mix_drms", x1, sm['mix_norm'], dh2, res=dx2, narrow=True)

    dgu = _ffn_bwd_weights("ffn1", dx1_narrow, ffn1_saved, wt['ffn1_w_down'], comm)
    grad_x, gs['ffn1_norm'] = _ffn_bwd_input("ffn1", dx1, ffn1_saved, dgu, sm['ffn1_norm'],
                                             wt['ffn1_w_gate'], wt['ffn1_w_up'], comm)
    return loss_lanes, grad_x, gs


def _pad_lanes(v):
    n = v.shape[1]
    return jnp.pad(v, ((0, 0), (0, -n % LANES)))


def _pack_small(vals):
    return jnp.concatenate([_pad_lanes(vals[n]) for n in SMALL], axis=1)


def _unpack_small(packed, like):
    out, off = {}, 0
    for n in SMALL:
        size = like[n].shape[1]
        out[n] = packed[:, off:off + size]
        off += size + (-size % LANES)
    return out


def kernel(x, mem, positions, ffn1_norm, ffn1_w_gate, ffn1_w_up, ffn1_w_down, mix_norm, w_in, q_a_norm, w_q_up, kv_a_norm, w_kv_up, mla_q_norm, mla_k_norm, gla_w_gate2, gla_b_gate, gla_out_norm, w_out, mem_attn_norm, mem_norm, mem_w_q, mem_w_k, mem_w_v, mem_w_o, mem_q_norm, mem_k_norm, ffn2_norm, ffn2_w_gate, ffn2_w_up, ffn2_w_down, loss_target, m_ffn1_norm, m_ffn1_w_gate, m_ffn1_w_up, m_ffn1_w_down, m_mix_norm, m_w_in, m_q_a_norm, m_w_q_up, m_kv_a_norm, m_w_kv_up, m_mla_q_norm, m_mla_k_norm, m_gla_w_gate2, m_gla_b_gate, m_gla_out_norm, m_w_out, m_mem_attn_norm, m_mem_norm, m_mem_w_q, m_mem_w_k, m_mem_w_v, m_mem_w_o, m_mem_q_norm, m_mem_k_norm, m_ffn2_norm, m_ffn2_w_gate, m_ffn2_w_up, m_ffn2_w_down, v_ffn1_norm, v_ffn1_w_gate, v_ffn1_w_up, v_ffn1_w_down, v_mix_norm, v_w_in, v_q_a_norm, v_w_q_up, v_kv_a_norm, v_w_kv_up, v_mla_q_norm, v_mla_k_norm, v_gla_w_gate2, v_gla_b_gate, v_gla_out_norm, v_w_out, v_mem_attn_norm, v_mem_norm, v_mem_w_q, v_mem_w_k, v_mem_w_v, v_mem_w_o, v_mem_q_norm, v_mem_k_norm, v_ffn2_norm, v_ffn2_w_gate, v_ffn2_w_up, v_ffn2_w_down):
    inp = dict(locals())
    x, mem, positions, target = inp['x'][0], inp['mem'][0], inp['positions'][0], inp['loss_target'][0]
    sm = {n: inp[n] for n in SMALL}
    out = {}

    def stored(key):
        name = key[2:] if key[:2] in ('m_', 'v_') else key
        return inp[key][0].T if name in TRANSPOSED else inp[key][0]

    def as_given(name, r):
        return r.T[None] if name in TRANSPOSED else r[None]

    class Comm:
        def __init__(self):
            self.gathers = {0: self.start(0)}
            self.forwards, self.exchanges, self.pairs = {}, [], []

        def start(self, k):
            return _gather_start(f"gather_start_{k}", [stored(n).astype(BF16) for n in GATHERS[k]])

        def forward(self, k):
            if k not in self.forwards:
                self.forwards[k] = _gather_mid(f"gather_mid_{k}", self.gathers[k])
                self.gathers.update({nxt: self.start(nxt) for nxt in NEXT_GATHERS.get(k, [])})

        def weights(self, k):
            self.forward(k)
            if k in EARLY_FORWARD:
                self.forward(EARLY_FORWARD[k])
            return dict(zip(GATHERS[k], _gather_end(f"gather_end_{k}", self.forwards[k])))

        def grads(self, stacks):
            names = list(stacks)
            if names[0] in TWO_STAGE:
                (n,) = names
                self.pairs.append((n, _stage_start("pair_start_" + n, _pair_copies, stacks[n], 2, (N_CHIP,) * 3)))
            else:
                self.exchanges.append((names, _exchange_start("exchange_start_" + names[0], [stacks[n] for n in names])))

        def advance(self):
            for n, state in self.pairs:
                own, land = _stage_wait("pair_wait_" + n, _pair_copies, state)
                pairs = _pair_sum("pair_sum_" + n, own, land)
                self.exchanges.append(([n], _stage_start("chip_start_" + n, _chip_copies, pairs, 1, (N_CHIP - 1, N_CHIP - 1, 1))))
            self.pairs = []

        def update(self, count):
            todo, self.exchanges = self.exchanges[:count], self.exchanges[count:]
            for names, state in todo:
                if names[0] in TWO_STAGE:
                    parts = _stage_wait("chip_wait_" + names[0], _chip_copies, state)
                else:
                    parts = _exchange_wait("exchange_wait_" + names[0], state)
                for n, p in zip(names, parts):
                    res = _adamw("adamw_" + n, stored(n), stored('m_' + n), stored('v_' + n), p)
                    for kind, r in zip(('grad_', 'delta_', 'new_m_', 'new_v_'), res):
                        out[kind + n] = as_given(n, r)

    _Chain.last = None
    comm = Comm()
    loss_lanes, grad_x, gs = _local_step(x, mem, positions, sm, comm, target)
    out['loss'] = lax.psum(jnp.sum(loss_lanes), ("x", "y", "c"))
    out['grad_x'] = grad_x[None]

    comm.update(len(comm.exchanges) - 3)
    small_parts = _all_gather("gather_small", [_pack_small(gs)])[0]
    res = _adamw("adamw_small", _pack_small(sm), _pack_small({n: inp['m_' + n] for n in SMALL}),
                 _pack_small({n: inp['v_' + n] for n in SMALL}), small_parts)
    for kind, r in zip(('grad_', 'delta_', 'new_m_', 'new_v_'), res):
        for n, val in _unpack_small(r, sm).items():
            out[kind + n] = val

    comm.update(3)

    names = ['loss', 'grad_x'] + [k + n for k in ('grad_', 'delta_', 'new_m_', 'new_v_') for n in WEIGHTS]
    return tuple(out[n] for n in names)
```

```python
import functools
import math

import jax
import jax.numpy as jnp
from jax import lax
from jax.experimental import pallas as pl
from jax.experimental.pallas import tpu as pltpu

F32 = jnp.float32
BF16 = jnp.bfloat16

N_DEV = 8
EPS = 1e-6
CHUNK = 64
MLA_HEADS, MLA_NOPE, MLA_ROPE, MLA_V = 8, 128, 64, 128
MLA_QK = MLA_NOPE + MLA_ROPE
MLA_Q_RANK, MLA_KV_RANK = 512, 256
ROPE_THETA = 10000.0
GLA_HEADS, GLA_DK, GLA_DV, GLA_GATE_RANK = 4, 128, 256, 16
GLA_TAU = 16.0
MEM_HEADS, MEM_HEAD_DIM = 4, 128
ADAM_LR, ADAM_B1, ADAM_B2, ADAM_EPS, ADAM_WD, ADAM_STEP = 0.001, 0.9, 0.999, 1e-08, 0.01, 10

V7X_VMEM_BYTES = 64 * 1024 * 1024
LANES = 128

NN = (((1,), (0,)), ((), ()))
NT = (((1,), (1,)), ((), ()))
TN = (((0,), (0,)), ((), ()))

WEIGHTS = ['ffn1_norm', 'ffn1_w_gate', 'ffn1_w_up', 'ffn1_w_down', 'mix_norm', 'w_in', 'q_a_norm', 'w_q_up',
           'kv_a_norm', 'w_kv_up', 'mla_q_norm', 'mla_k_norm', 'gla_w_gate2', 'gla_b_gate', 'gla_out_norm', 'w_out',
           'mem_attn_norm', 'mem_norm', 'mem_w_q', 'mem_w_k', 'mem_w_v', 'mem_w_o', 'mem_q_norm', 'mem_k_norm',
           'ffn2_norm', 'ffn2_w_gate', 'ffn2_w_up', 'ffn2_w_down']
SMALL = ['ffn1_norm', 'mix_norm', 'q_a_norm', 'kv_a_norm', 'mla_q_norm', 'mla_k_norm', 'gla_b_gate', 'gla_out_norm',
         'mem_attn_norm', 'mem_norm', 'mem_q_norm', 'mem_k_norm', 'ffn2_norm']
MIXER = ['w_in', 'w_q_up', 'w_kv_up', 'gla_w_gate2', 'w_out']
MEMORY = ['mem_w_q', 'mem_w_k', 'mem_w_v', 'mem_w_o']
TRANSPOSED = ['ffn1_w_gate', 'ffn1_w_up', 'ffn2_w_gate', 'ffn2_w_up', 'w_q_up']
TWO_STAGE = ['w_in', 'ffn1_w_down', 'ffn1_w_gate', 'ffn1_w_up']
GATHERS = [['ffn1_w_gate'], ['ffn1_w_up'], ['ffn1_w_down'], MIXER, MEMORY, ['ffn2_w_gate', 'ffn2_w_up', 'ffn2_w_down']]
NEXT_GATHERS = {0: [1], 1: [2], 2: [3], 3: [4, 5]}
EARLY_FORWARD = {4: 5}

IN_REF = [('zq', 512), ('zkv', 256), ('zkr', 64), ('gq', 512), ('gk', 512), ('gv', 1024), ('zg', 16), ('zr', 1024)]
IN_PAD = [('zq', 512), ('zkv', 256), ('gq', 512), ('gk', 512), ('gv', 1024), ('zr', 1024), ('zkr', 128), ('zg', 128)]
IN_WIDTH = sum(n for _, n in IN_REF)
IN_PAD_WIDTH = sum(n for _, n in IN_PAD)


def _offsets(layout):
    out, off = {}, 0
    for name, n in layout:
        out[name] = off
        off += n
    return out


REF_OFF, PAD_OFF = _offsets(IN_REF), _offsets(IN_PAD)
REF_SIZE = dict(IN_REF)


def _tile(n, pref):
    return pref if n % pref == 0 else n


def _block_bytes(blk, dtype):
    dims = [d for d in blk if d is not None]
    if len(dims) >= 1:
        dims[-1] = -(-dims[-1] // LANES) * LANES
    return math.prod(dims) * jnp.dtype(dtype).itemsize


def _vmem_limit(pipelined_bytes, resident_bytes=0):
    need = 2 * pipelined_bytes + resident_bytes + (8 << 20)
    return int(min(max(need, 32 << 20), V7X_VMEM_BYTES - (6 << 20)))


class _Chain:
    last = None


def _chained(body, *, in_specs, link=0, **kwargs):
    def call(*operands):
        dep = _Chain.last
        if dep is not None and any(o is dep for o in operands):
            dep = None
        if dep is None:
            res = pl.pallas_call(body, in_specs=in_specs, **kwargs)(*operands)
        else:
            n = len(operands)

            def chained_body(*refs):
                body(*refs[:n], *refs[n + 1:])

            res = pl.pallas_call(chained_body, in_specs=list(in_specs) + [pl.BlockSpec(memory_space=pl.ANY)],
                                 **kwargs)(*operands, dep)
        _Chain.last = res[link] if isinstance(res, (list, tuple)) else res
        return res

    return call


def _row_tile(rows, width, elements=1 << 18):
    lanes = -(-width // LANES) * LANES
    return _tile(rows, max(256, 1 << int(math.log2(max(1, elements // lanes)))))


def _rowwise_params(*semantics):
    return pltpu.CompilerParams(dimension_semantics=semantics, vmem_limit_bytes=48 << 20)


def _mm(name, a, b, grid, a_spec, b_spec, o_spec, out_shape, out_dtype, dims, scale=None, res=None, fuse=None):
    nk = grid[2]
    o_blk, o_map = o_spec
    acc_shape = tuple(d for d in o_blk if d is not None)
    extras = [res] if res is not None else (list(fuse[1]) if fuse else [])
    n_out = 2 if fuse else 1
    a_list, b_list = (list(a), list(b)) if isinstance(a, (tuple, list)) else ([a], [b])
    a_specs = a_spec if isinstance(a_spec, list) else [a_spec] * len(a_list)
    b_specs = b_spec if isinstance(b_spec, list) else [b_spec] * len(b_list)
    n_in = 2 * len(a_list)

    def body(*refs):
        a_refs, b_refs = refs[:n_in // 2], refs[n_in // 2:n_in]
        extra_refs = refs[n_in:n_in + len(extras)]
        out_refs = refs[n_in + len(extras):n_in + len(extras) + n_out]
        rest = refs[n_in + len(extras) + n_out:]
        r_ref = extra_refs[0] if res is not None else None
        o_ref = out_refs[0]

        def product():
            return sum(lax.dot_general(a_ref[...].astype(BF16), b_ref[...].astype(BF16), dims, preferred_element_type=F32)
                       for a_ref, b_ref in zip(a_refs, b_refs))

        def finish(r):
            if scale is not None:
                r = r * scale
            if r_ref is not None:
                r = r + r_ref[...].astype(F32)
            if fuse:
                for ref, val in zip(out_refs, fuse[0](r, *[e[...].astype(F32) for e in extra_refs])):
                    ref[...] = val.astype(ref.dtype)
            else:
                o_ref[...] = r.astype(o_ref.dtype)

        if nk == 1:
            finish(product())
        else:
            acc = rest[0]
            k = pl.program_id(2)

            @pl.when(k == 0)
            def _():
                acc[...] = product()

            @pl.when(k > 0)
            def _():
                acc[...] += product()

            @pl.when(k == nk - 1)
            def _():
                finish(acc[...])

    in_specs = [pl.BlockSpec(*spec) for spec in a_specs + b_specs]
    operands = a_list + b_list
    piped = sum(_block_bytes(spec[0], v.dtype) for spec, v in zip(a_specs + b_specs, operands))
    piped += _block_bytes(o_blk, out_dtype)
    for extra in extras:
        in_specs.append(pl.BlockSpec(o_blk, o_map))
        operands.append(extra)
        piped += _block_bytes(o_blk, extra.dtype)
    piped += (n_out - 1) * _block_bytes(o_blk, out_dtype)
    scratch = [pltpu.VMEM(acc_shape, F32)] if nk > 1 else []
    out_spec, out_sds = pl.BlockSpec(o_blk, o_map), jax.ShapeDtypeStruct(out_shape, out_dtype)
    return _chained(
        body, name=name, grid=grid, in_specs=in_specs, out_specs=(out_spec,) * n_out if fuse else out_spec,
        out_shape=(out_sds,) * n_out if fuse else out_sds, scratch_shapes=scratch,
        compiler_params=pltpu.CompilerParams(
            dimension_semantics=("parallel", "parallel", "arbitrary"),
            vmem_limit_bytes=_vmem_limit(piped, 3 * _block_bytes(acc_shape, F32))),
    )(*operands)


def _mm2(name, a, b, dims, out_dtype, tm=512, tn=1024, tk=2048, scale=None, res=None):
    if dims is NN:
        (m, kk), n = a.shape, b.shape[1]
    elif dims is NT:
        (m, kk), n = a.shape, b.shape[0]
    else:
        (kk, m), n = a.shape, b.shape[1]
    tm, tn, tk = _tile(m, tm), _tile(n, tn), _tile(kk, tk)
    a_spec = ((tk, tm), lambda i, j, k: (k, i)) if dims is TN else ((tm, tk), lambda i, j, k: (i, k))
    b_spec = ((tn, tk), lambda i, j, k: (j, k)) if dims is NT else ((tk, tn), lambda i, j, k: (k, j))
    return _mm(name, a, b, (m // tm, n // tn, kk // tk), a_spec, b_spec, ((tm, tn), lambda i, j, k: (i, j)),
               (m, n), out_dtype, dims, scale, res)


def _mm_stack_out(name, a, b, out_dtype, tm=512, tk=2048):
    (m, kk), (nj, _, n) = a.shape, b.shape
    tm, tk = _tile(m, tm), _tile(kk, tk)
    return _mm(name, a, b, (nj, m // tm, kk // tk), ((tm, tk), lambda j, i, k: (i, k)),
               ((None, tk, n), lambda j, i, k: (j, k, 0)), ((None, tm, n), lambda j, i, k: (j, i, 0)),
               (nj, m, n), out_dtype, NN)


def _mm_stack_nt_out(name, a, b, out_dtype, scale=None, fuse=None, tm=512, tk=2048):
    (m, kk), (nj, n, _) = a.shape, b.shape
    tm, tk = _tile(m, tm), _tile(kk, tk)
    return _mm(name, a, b, (nj, m // tm, kk // tk), ((tm, tk), lambda j, i, k: (i, k)),
               ((None, n, tk), lambda j, i, k: (j, 0, k)), ((None, tm, n), lambda j, i, k: (j, i, 0)),
               (nj, m, n), out_dtype, NT, scale, fuse=fuse)


def _mm_stack_sum(name, a, b, out_dtype, scale=None, res=None, tm=512, tn=1024):
    (nj, m, f), n = (a[0] if isinstance(a, tuple) else a).shape, (b[0] if isinstance(b, tuple) else b).shape[2]
    tm, tn = _tile(m, tm), _tile(n, tn)
    a_spec = ((None, tm, f), lambda i, j, k: (k, i, 0))
    b_spec = ((None, f, tn), lambda i, j, k: (k, 0, j))
    if not isinstance(a, tuple) and nj % 2 == 0:
        nj, a, b = nj // 2, (a, a), (b, b)
        a_spec = [((None, tm, f), lambda i, j, k: (2 * k, i, 0)), ((None, tm, f), lambda i, j, k: (2 * k + 1, i, 0))]
        b_spec = [((None, f, tn), lambda i, j, k: (2 * k, 0, j)), ((None, f, tn), lambda i, j, k: (2 * k + 1, 0, j))]
    return _mm(name, a, b, (m // tm, n // tn, nj), a_spec, b_spec, ((tm, tn), lambda i, j, k: (i, j)),
               (m, n), out_dtype, NN, scale, res)


def _mm_stack_nt_sum(name, a, b, out_dtype, res=None, tm=512, tn=1024):
    (nj, m, f), n = a.shape, b.shape[1]
    tm, tn = _tile(m, tm), _tile(n, tn)
    return _mm(name, a, b, (m // tm, n // tn, nj), ((None, tm, f), lambda i, j, k: (k, i, 0)),
               ((None, tn, f), lambda i, j, k: (k, j, 0)), ((tm, tn), lambda i, j, k: (i, j)),
               (m, n), out_dtype, NT, None, res)


def _mm_stack_tn_left(name, a, b, out_dtype, tm=1024, tk=512):
    (m, kp), (nj, _, n) = a.shape, b.shape
    tm, tk = _tile(kp, tm), _tile(m, tk)
    return _mm(name, a, b, (nj, kp // tm, m // tk), ((tk, tm), lambda j, i, k: (k, i)),
               ((None, tk, n), lambda j, i, k: (j, k, 0)), ((None, tm, n), lambda j, i, k: (j, i, 0)),
               (nj, kp, n), out_dtype, TN)


def _mm_stack_tn_right(name, a, b, out_dtype, scale=None, tn=2048, tk=512):
    (nj, m, f), n = a.shape, b.shape[1]
    tn, tk = _tile(n, tn), _tile(m, tk)
    return _mm(name, a, b, (nj, n // tn, m // tk), ((None, tk, f), lambda j, i, k: (j, k, 0)),
               ((tk, tn), lambda j, i, k: (k, i)), ((None, f, tn), lambda j, i, k: (j, 0, i)),
               (nj, f, n), out_dtype, TN, scale)


def _split3(x):
    hi = x.astype(BF16)
    r1 = x - hi.astype(F32)
    mid = r1.astype(BF16)
    lo = (r1 - mid.astype(F32)).astype(BF16)
    return hi, mid, lo


def _swap_halves(x, swap):
    return sum(jnp.dot(part, swap, preferred_element_type=F32) for part in _split3(x))


def _rope_specs(rope, tm):
    cos_t, _, swap = rope
    nt = cos_t.shape[0] // tm
    tab = pl.BlockSpec((tm, cos_t.shape[1]), lambda i, c: (i % nt, 0))
    return [tab, tab, pl.BlockSpec(swap.shape, lambda i, c: (0, 0))]


def _rms_fwd(name, x, g, out_dtype, rope=None):
    rows, cols = x.shape
    d = g.shape[1]
    tm = _row_tile(rows if rope is None else rope[0].shape[0], d)

    def body(x_ref, g_ref, *refs):
        xf = x_ref[...].astype(F32)
        r = lax.rsqrt(jnp.mean(xf * xf, axis=-1, keepdims=True) + EPS)
        y = xf * r * g_ref[...]
        if rope is not None:
            c_ref, s_ref, p_ref = refs[:3]
            y = y * c_ref[...] + _swap_halves(y, p_ref[...]) * s_ref[...]
        refs[-1][...] = y.astype(refs[-1].dtype)

    return _chained(
        body, name=name, grid=(rows // tm, cols // d),
        in_specs=[pl.BlockSpec((tm, d), lambda i, c: (i, c)), pl.BlockSpec((1, d), lambda i, c: (0, 0))]
        + (_rope_specs(rope, tm) if rope is not None else []),
        out_specs=pl.BlockSpec((tm, d), lambda i, c: (i, c)),
        out_shape=jax.ShapeDtypeStruct((rows, cols), out_dtype),
        compiler_params=_rowwise_params("parallel", "parallel"),
    )(x, g, *(rope or ()))


def _rms_bwd(name, x, g, dy, res=None, rope=None, narrow=False):
    rows, cols = x.shape
    d = g.shape[1]
    tm = _row_tile(rows if rope is None else rope[0].shape[0], d)

    def body(*refs):
        x_ref, g_ref, dy_ref = refs[:3]
        n_out = 3 if narrow else 2
        dx_ref, dg_ref = refs[-n_out], refs[-1]
        r_ref = refs[3] if res is not None else None
        xf = x_ref[...].astype(F32)
        r = lax.rsqrt(jnp.mean(xf * xf, axis=-1, keepdims=True) + EPS)
        xhat = xf * r
        dyf = dy_ref[...].astype(F32)
        if rope is not None:
            c_ref, s_ref, p_ref = refs[-n_out - 3:-n_out]
            dyf = dyf * c_ref[...] + _swap_halves(dyf * s_ref[...], p_ref[...])

        @pl.when((pl.program_id(0) == 0) & (pl.program_id(1) == 0))
        def _():
            dg_ref[...] = jnp.zeros_like(dg_ref)

        dg_ref[...] += jnp.sum(dyf * xhat, axis=0, keepdims=True)
        dxh = dyf * g_ref[...]
        dx = r * (dxh - xhat * jnp.mean(dxh * xhat, axis=-1, keepdims=True))
        if r_ref is not None:
            dx = dx + r_ref[...].astype(F32)
        dx_ref[...] = dx
        if narrow:
            refs[-2][...] = dx.astype(BF16)

    blk = pl.BlockSpec((tm, d), lambda i, c: (i, c))
    in_specs = [blk, pl.BlockSpec((1, d), lambda i, c: (0, 0)), blk]
    operands = [x, g, dy]
    if res is not None:
        in_specs.append(blk)
        operands.append(res)
    if rope is not None:
        in_specs += _rope_specs(rope, tm)
        operands += list(rope)
    wide = [(blk, jax.ShapeDtypeStruct((rows, cols), F32))] + ([(blk, jax.ShapeDtypeStruct((rows, cols), BF16))] if narrow else [])
    outs = wide + [(pl.BlockSpec((1, d), lambda i, c: (0, 0)), jax.ShapeDtypeStruct((1, d), F32))]
    return _chained(
        body, name=name, grid=(rows // tm, cols // d), in_specs=in_specs,
        out_specs=tuple(spec for spec, _ in outs), out_shape=tuple(sds for _, sds in outs),
        compiler_params=_rowwise_params("arbitrary", "arbitrary"),
    )(*operands)


def _swiglu_fwd(name, g, u, out_dtype, tm=256):
    rows, cols = g.shape
    tm = _tile(rows, tm)

    def body(g_ref, u_ref, o_ref):
        gf = g_ref[...].astype(F32)
        o_ref[...] = (gf * jax.nn.sigmoid(gf) * u_ref[...].astype(F32)).astype(o_ref.dtype)

    blk = pl.BlockSpec((tm, cols), lambda i: (i, 0))
    return _chained(
        body, name=name, grid=(rows // tm,), in_specs=[blk, blk], out_specs=blk,
        out_shape=jax.ShapeDtypeStruct((rows, cols), out_dtype),
        compiler_params=_rowwise_params("parallel"),
    )(g, u)


def _swiglu_bwd(name, da, g, u, out_dtype, tm=256):
    rows, cols = g.shape
    tm = _tile(rows, tm)

    def body(da_ref, g_ref, u_ref, dg_ref, du_ref):
        gf = g_ref[...].astype(F32)
        daf = da_ref[...].astype(F32)
        sig = jax.nn.sigmoid(gf)
        du_ref[...] = (daf * gf * sig).astype(du_ref.dtype)
        dg_ref[...] = (daf * u_ref[...].astype(F32) * sig * (1.0 + gf * (1.0 - sig))).astype(dg_ref.dtype)

    blk = pl.BlockSpec((tm, cols), lambda i: (i, 0))
    sds = jax.ShapeDtypeStruct((rows, cols), out_dtype)
    return _chained(
        body, name=name, grid=(rows // tm,), in_specs=[blk, blk, blk], out_specs=(blk, blk), out_shape=(sds, sds),
        compiler_params=_rowwise_params("parallel"),
    )(da, g, u)


def _gate_fwd(name, pre, bias, tm=256):
    rows, cols = pre.shape
    tm = _tile(rows, tm)

    def body(p_ref, b_ref, o_ref):
        z = p_ref[...] + b_ref[...]
        o_ref[...] = (jnp.minimum(z, 0.0) - jnp.log(1.0 + jnp.exp(-jnp.abs(z)))) * (1.0 / GLA_TAU)

    blk = pl.BlockSpec((tm, cols), lambda i: (i, 0))
    return _chained(
        body, name=name, grid=(rows // tm,), in_specs=[blk, pl.BlockSpec((1, cols), lambda i: (0, 0))], out_specs=blk,
        out_shape=jax.ShapeDtypeStruct((rows, cols), F32),
        compiler_params=_rowwise_params("parallel"),
    )(pre, bias)


def _gate_bwd(name, pre, bias, dla, tm=256):
    rows, cols = pre.shape
    tm = _tile(rows, tm)

    def body(p_ref, b_ref, d_ref, dp_ref, db_ref):
        z = p_ref[...] + b_ref[...]
        dp = d_ref[...] * (1.0 / GLA_TAU) / (1.0 + jnp.exp(z))
        dp_ref[...] = dp

        @pl.when(pl.program_id(0) == 0)
        def _():
            db_ref[...] = jnp.zeros_like(db_ref)

        db_ref[...] += jnp.sum(dp, axis=0, keepdims=True)

    blk = pl.BlockSpec((tm, cols), lambda i: (i, 0))
    row = pl.BlockSpec((1, cols), lambda i: (0, 0))
    return _chained(
        body, name=name, grid=(rows // tm,), in_specs=[blk, row, blk], out_specs=(blk, row),
        out_shape=(jax.ShapeDtypeStruct((rows, cols), F32), jax.ShapeDtypeStruct((1, cols), F32)),
        compiler_params=_rowwise_params("arbitrary"),
    )(pre, bias, dla)


def _loss(name, y, target, tm=256):
    rows, d = y.shape
    tm = _tile(rows, tm)

    def body(y_ref, t_ref, dy_ref, dyb_ref, l_ref):
        err = y_ref[...] - t_ref[...]
        dy_ref[...] = err * (1.0 / d)
        dyb_ref[...] = (err * (1.0 / d)).astype(BF16)

        @pl.when(pl.program_id(0) == 0)
        def _():
            l_ref[...] = jnp.zeros_like(l_ref)

        sq = (err * err).reshape(tm // 8, 8, d)
        l_ref[...] += jnp.sum(sq, axis=0) * (0.5 / d)

    blk = pl.BlockSpec((tm, d), lambda i: (i, 0))
    return _chained(
        body, name=name, grid=(rows // tm,), in_specs=[blk, blk],
        out_specs=(blk, blk, pl.BlockSpec((8, d), lambda i: (0, 0))),
        out_shape=(jax.ShapeDtypeStruct((rows, d), F32), jax.ShapeDtypeStruct((rows, d), BF16),
                   jax.ShapeDtypeStruct((8, d), F32)),
        compiler_params=_rowwise_params("arbitrary"),
    )(y, target)


def _per_query_block(causal, nq, tq, tk, inner):
    if not causal:
        inner(0, tk)
        return
    for block in range(nq):
        pl.when(pl.program_id(1) == block)(functools.partial(inner, block * tq, (block + 1) * tq))


def _scores(q, k, scale, causal, q0):
    s = lax.dot_general(q, k, NT, preferred_element_type=F32) * scale
    if causal:
        qc = (q0 + lax.broadcasted_iota(jnp.int32, s.shape, 0)) // CHUNK
        kc = lax.broadcasted_iota(jnp.int32, s.shape, 1) // CHUNK
        s = jnp.where(kc <= qc, s, -1e30)
    e = jnp.exp(s - jnp.max(s, axis=-1, keepdims=True))
    return e, jnp.sum(e, axis=-1, keepdims=True)


def _attn_fwd(name, q, k, v, scale, causal, tq=256):
    nh, t, dk = q.shape
    tk, dv = k.shape[1], v.shape[2]
    tq = _tile(t, tq)

    def body(q_ref, k_ref, v_ref, o_ref):
        def inner(q0, kv):
            e, l = _scores(q_ref[...], k_ref[0:kv, :], scale, causal, q0)
            o = jnp.dot(e.astype(BF16), v_ref[0:kv, :], preferred_element_type=F32)
            o_ref[...] = (o / l).astype(o_ref.dtype)

        _per_query_block(causal, t // tq, tq, tk, inner)

    return _chained(
        body, name=name, grid=(nh, t // tq),
        in_specs=[pl.BlockSpec((None, tq, dk), lambda h, i: (h, i, 0)), pl.BlockSpec((None, tk, dk), lambda h, i: (h, 0, 0)),
                  pl.BlockSpec((None, tk, dv), lambda h, i: (h, 0, 0))],
        out_specs=pl.BlockSpec((tq, dv), lambda h, i: (i, h)),
        out_shape=jax.ShapeDtypeStruct((t, nh * dv), BF16),
        compiler_params=pltpu.CompilerParams(dimension_semantics=("parallel", "parallel"),
                                             vmem_limit_bytes=_vmem_limit(0, 6 * tq * tk * 4)),
    )(q, k, v)


def _attn_bwd(name, q, k, v, do, scale, causal, tq=256):
    nh, t, dk = q.shape
    tk, dv = k.shape[1], v.shape[2]
    tq = _tile(t, tq)

    def body(q_ref, k_ref, v_ref, do_ref, dq_ref, dk_ref, dv_ref):
        @pl.when(pl.program_id(1) == 0)
        def _():
            dk_ref[...] = jnp.zeros_like(dk_ref)
            dv_ref[...] = jnp.zeros_like(dv_ref)

        def inner(q0, kv):
            qb, kb = q_ref[...], k_ref[0:kv, :]
            e, l = _scores(qb, kb, scale, causal, q0)
            p = e / l
            dob = do_ref[...].astype(BF16)
            dp = lax.dot_general(dob, v_ref[0:kv, :], NT, preferred_element_type=F32)
            ds = (p * (dp - jnp.sum(p * dp, axis=-1, keepdims=True)) * scale).astype(BF16)
            dq_ref[...] = jnp.dot(ds, kb, preferred_element_type=F32)
            dk_ref[0:kv, :] += lax.dot_general(ds, qb, TN, preferred_element_type=F32)
            dv_ref[0:kv, :] += lax.dot_general(p.astype(BF16), dob, TN, preferred_element_type=F32)

        _per_query_block(causal, t // tq, tq, tk, inner)

    return _chained(
        body, name=name, grid=(nh, t // tq),
        in_specs=[pl.BlockSpec((None, tq, dk), lambda h, i: (h, i, 0)), pl.BlockSpec((None, tk, dk), lambda h, i: (h, 0, 0)),
                  pl.BlockSpec((None, tk, dv), lambda h, i: (h, 0, 0)), pl.BlockSpec((tq, dv), lambda h, i: (i, h))],
        out_specs=(pl.BlockSpec((None, tq, dk), lambda h, i: (h, i, 0)), pl.BlockSpec((None, tk, dk), lambda h, i: (h, 0, 0)),
                   pl.BlockSpec((None, tk, dv), lambda h, i: (h, 0, 0))),
        out_shape=(jax.ShapeDtypeStruct((nh, t, dk), F32), jax.ShapeDtypeStruct((nh, tk, dk), F32),
                   jax.ShapeDtypeStruct((nh, tk, dv), F32)),
        compiler_params=pltpu.CompilerParams(dimension_semantics=("parallel", "arbitrary"),
                                             vmem_limit_bytes=_vmem_limit(0, 10 * tq * tk * 4)),
    )(q, k, v, do)


def _tri(lower):
    r = lax.broadcasted_iota(jnp.int32, (CHUNK, CHUNK), 0)
    c = lax.broadcasted_iota(jnp.int32, (CHUNK, CHUNK), 1)
    return jnp.where((c <= r) if lower else (c >= r), 1.0, 0.0).astype(BF16)


def _tri_dot(tri, x):
    return sum(jnp.dot(tri, part, preferred_element_type=F32) for part in _split3(x))


def _gla_fwd(name, q, k, v, la, nh):
    t = q.shape[0]
    dk, dv = q.shape[1] // nh, v.shape[1] // nh
    nc = t // CHUNK

    def body(q_ref, k_ref, v_ref, g_ref, o_ref, st_ref, state):
        @pl.when(pl.program_id(0) == 0)
        def _():
            state[...] = jnp.zeros_like(state)

        g = g_ref[...]
        b = _tri_dot(_tri(True), g)
        b_end = jnp.sum(g, axis=0, keepdims=True)
        k_dec = (k_ref[...] * jnp.exp(b_end - b)).astype(BF16)
        decay = jnp.exp(b_end)
        qc = (q_ref[...] * (dk ** -0.5)).astype(BF16)
        vb = v_ref[...].astype(BF16)
        for h in range(nh):
            ks, vs = slice(h * dk, (h + 1) * dk), slice(h * dv, (h + 1) * dv)
            u_t = lax.dot_general(vb[:, vs], k_dec[:, ks], TN, preferred_element_type=F32)
            new = state[h] * decay[:, ks] + u_t
            state[h] = new
            st_ref[h] = new
            o_ref[:, vs] = lax.dot_general(qc[:, ks], new.astype(BF16), NT, preferred_element_type=F32)

    kblk = pl.BlockSpec((CHUNK, nh * dk), lambda n: (n, 0))
    vblk = pl.BlockSpec((CHUNK, nh * dv), lambda n: (n, 0))
    return _chained(
        body, name=name, grid=(nc,), in_specs=[kblk, kblk, vblk, kblk],
        out_specs=(vblk, pl.BlockSpec((nh, None, dv, dk), lambda n: (0, n, 0, 0))),
        out_shape=(jax.ShapeDtypeStruct((t, nh * dv), F32), jax.ShapeDtypeStruct((nh, nc, dv, dk), F32)),
        scratch_shapes=[pltpu.VMEM((nh, dv, dk), F32)],
        compiler_params=_rowwise_params("arbitrary"),
    )(q, k, v, la)


def _gla_bwd(name, q, k, v, la, states, do, nh):
    t = q.shape[0]
    dk, dv = q.shape[1] // nh, v.shape[1] // nh
    nc = t // CHUNK
    scale = dk ** -0.5

    def body(q_ref, k_ref, v_ref, g_ref, do_ref, st_ref, sp_ref, dq_ref, dk_ref, dv_ref, dg_ref, carry):
        i = pl.program_id(0)

        @pl.when(i == 0)
        def _():
            carry[...] = jnp.zeros_like(carry)

        g = g_ref[...]
        b = _tri_dot(_tri(True), g)
        b_end = jnp.sum(g, axis=0, keepdims=True)
        w = jnp.exp(b_end - b)
        decay = jnp.exp(b_end)
        k_dec = k_ref[...] * w
        k_decb = k_dec.astype(BF16)
        qc = (q_ref[...] * scale).astype(BF16)
        dob = do_ref[...].astype(BF16)
        vb = v_ref[...].astype(BF16)
        dk_dec, ddecay = [], []
        for h in range(nh):
            ks, vs = slice(h * dk, (h + 1) * dk), slice(h * dv, (h + 1) * dv)
            dq_ref[:, ks] = jnp.dot(dob[:, vs], st_ref[h].astype(BF16), preferred_element_type=F32) * scale
            g_t = carry[h] + lax.dot_general(dob[:, vs], qc[:, ks], TN, preferred_element_type=F32)
            g_tb = g_t.astype(BF16)
            dk_dec.append(jnp.dot(vb[:, vs], g_tb, preferred_element_type=F32))
            dv_ref[:, vs] = lax.dot_general(k_decb[:, ks], g_tb, NT, preferred_element_type=F32)
            prev = jnp.where(i < nc - 1, sp_ref[h], 0.0)
            ddecay.append(jnp.sum(g_t * prev, axis=0, keepdims=True))
            carry[h] = g_t * decay[:, ks]
        dk_dec = jnp.concatenate(dk_dec, axis=1)
        dk_ref[...] = dk_dec * w
        e = dk_dec * k_dec
        db_end = jnp.sum(e, axis=0, keepdims=True) + jnp.concatenate(ddecay, axis=1) * decay
        dg_ref[...] = _tri_dot(_tri(False), -e) + db_end

    kblk = pl.BlockSpec((CHUNK, nh * dk), lambda i: (nc - 1 - i, 0))
    vblk = pl.BlockSpec((CHUNK, nh * dv), lambda i: (nc - 1 - i, 0))
    ksds = jax.ShapeDtypeStruct((t, nh * dk), F32)
    return _chained(
        body, name=name, grid=(nc,),
        in_specs=[kblk, kblk, vblk, kblk, vblk,
                  pl.BlockSpec((nh, None, dv, dk), lambda i: (0, nc - 1 - i, 0, 0)),
                  pl.BlockSpec((nh, None, dv, dk), lambda i: (0, jnp.maximum(nc - 2 - i, 0), 0, 0))],
        out_specs=(kblk, kblk, vblk, kblk),
        out_shape=(ksds, ksds, jax.ShapeDtypeStruct((t, nh * dv), F32), ksds),
        scratch_shapes=[pltpu.VMEM((nh, dv, dk), F32)],
        compiler_params=_rowwise_params("arbitrary"),
    )(q, k, v, la, do, states, states)


def _adamw(name, w, m, v, parts):
    rows, cols = w.shape
    tm = 1 << int(math.log2(max(8, (1 << 18) // (-(-cols // LANES) * LANES))))
    while rows % tm and tm > 8:
        tm //= 2
    tm = _tile(rows, tm)

    def body(w_ref, m_ref, v_ref, p_ref, g_ref, d_ref, nm_ref, nv_ref):
        g = p_ref[0].astype(F32)
        for s in range(1, parts.shape[0]):
            g = g + p_ref[s].astype(F32)
        m_new = ADAM_B1 * m_ref[...] + (1.0 - ADAM_B1) * g
        v_new = ADAM_B2 * v_ref[...] + (1.0 - ADAM_B2) * jnp.square(g)
        m_hat = m_new / (1.0 - ADAM_B1 ** ADAM_STEP)
        v_hat = v_new / (1.0 - ADAM_B2 ** ADAM_STEP)
        g_ref[...] = g
        d_ref[...] = -ADAM_LR * (m_hat / (jnp.sqrt(v_hat) + ADAM_EPS) + ADAM_WD * w_ref[...])
        nm_ref[...] = m_new
        nv_ref[...] = v_new

    blk = pl.BlockSpec((tm, cols), lambda i: (i, 0))
    sds = jax.ShapeDtypeStruct((rows, cols), F32)
    return _chained(
        body, name=name, grid=(rows // tm,),
        in_specs=[blk, blk, blk, pl.BlockSpec((parts.shape[0], tm, cols), lambda i: (0, i, 0))],
        out_specs=(blk, blk, blk, blk), out_shape=(sds, sds, sds, sds),
        compiler_params=_rowwise_params("parallel"),
    )(w, m, v, parts)


HBM = pl.BlockSpec(memory_space=pl.ANY)
MESH = pl.DeviceIdType.MESH


def _all_gather(name, shards):
    n = len(shards)

    def body(*refs):
        ins, outs = refs[:n], refs[n:2 * n]
        send_sems, recv_sems, local_sems = refs[2 * n:]
        x, y, c = lax.axis_index("x"), lax.axis_index("y"), lax.axis_index("c")
        me, sibling = (x, y, c), (x, y, 1 - c)
        chips = [(1 - x, y), (x, 1 - y), (1 - x, 1 - y)]

        def copy(w, k, block, to, src=None):
            dst = outs[w].at[4 * block[0] + 2 * block[1] + block[2]]
            return pltpu.make_async_remote_copy(
                src_ref=dst if src is None else src, dst_ref=dst, send_sem=send_sems.at[7 * w + k],
                recv_sem=recv_sems.at[7 * w + k], device_id=to, device_id_type=MESH)

        mine, first, passed = [], [], []
        for w in range(n):
            cp = pltpu.make_async_copy(ins[w], outs[w].at[4 * x + 2 * y + c], local_sems.at[w])
            cp.start()
            mine.append(cp)
            first.append(copy(w, 0, me, sibling, src=ins[w]))
            first += [copy(w, 1 + j, me, (*chip, c), src=ins[w]) for j, chip in enumerate(chips)]
        for cp in first:
            cp.start()
        for w in range(n):
            for j, chip in enumerate(chips):
                copy(w, 1 + j, (*chip, c), me).wait_recv()
                cp = copy(w, 4 + j, (*chip, c), sibling)
                cp.start()
                passed.append(cp)
        for w in range(n):
            copy(w, 0, sibling, me).wait_recv()
            for j, chip in enumerate(chips):
                copy(w, 4 + j, (*chip, 1 - c), me).wait_recv()
        for cp in first + passed:
            cp.wait_send()
        for cp in mine:
            cp.wait()

    return _chained(
        body, name=name, in_specs=[HBM] * n, out_specs=[HBM] * n,
        out_shape=[jax.ShapeDtypeStruct((N_DEV,) + s.shape, s.dtype) for s in shards],
        scratch_shapes=[pltpu.SemaphoreType.DMA((7 * n,)), pltpu.SemaphoreType.DMA((7 * n,)),
                        pltpu.SemaphoreType.DMA((n,))],
    )(*shards)


HBM_SPEC = pl.BlockSpec(memory_space=pltpu.HBM)
SEM_SPEC = pl.BlockSpec(memory_space=pltpu.SEMAPHORE)
EFFECT = pltpu.SideEffectType.DATAFLOW_SIDE_EFFECTING
TOKEN = jax.ShapeDtypeStruct((8, LANES), F32)


def _in_hbm(a):
    return pltpu.with_memory_space_constraint(a, pltpu.HBM)


def _place():
    x, y, c = lax.axis_index("x"), lax.axis_index("y"), lax.axis_index("c")
    chips = [(1 - x, y), (x, 1 - y), (1 - x, 1 - y)]
    return x, y, c, chips


def _block_of(px, py, pc):
    return 4 * px + 2 * py + pc


def _gather_copies(ins, outs, sems, w):
    send_sems, recv_sems = sems
    x, y, c, chips = _place()
    peers = [(x, y, 1 - c)] + [(*chip, c) for chip in chips]

    def copy(k, block, to, src):
        dst = outs[w].at[_block_of(*block)]
        return pltpu.make_async_remote_copy(src_ref=dst if src is None else src, dst_ref=dst, send_sem=send_sems.at[4 * w + k],
                                            recv_sem=recv_sems.at[4 * w + k], device_id=to, device_id_type=MESH)

    sends = [copy(k, (x, y, c), peer, ins[w]) for k, peer in enumerate(peers)]
    recvs = [copy(k, peer, (x, y, c), None) for k, peer in enumerate(peers)]
    return sends, recvs


def _forward_copies(outs, sems, w):
    send_sems, recv_sems = sems
    x, y, c, chips = _place()

    def copy(j, block, to):
        dst = outs[w].at[_block_of(*block)]
        return pltpu.make_async_remote_copy(src_ref=dst, dst_ref=dst, send_sem=send_sems.at[3 * w + j],
                                            recv_sem=recv_sems.at[3 * w + j], device_id=to, device_id_type=MESH)

    sends = [copy(j, (*chip, c), (x, y, 1 - c)) for j, chip in enumerate(chips)]
    recvs = [copy(j, (*chip, 1 - c), (x, y, c)) for j, chip in enumerate(chips)]
    return sends, recvs


def _gather_start(name, shards):
    n = len(shards)

    def body(*refs):
        ins, outs, sems = refs[:n], refs[n:2 * n], refs[2 * n:2 * n + 2]
        for w in range(n):
            for cp in _gather_copies(ins, outs, sems, w)[0]:
                cp.start()
        refs[-1][...] = jnp.zeros_like(refs[-1])

    mine = _block_of(lax.axis_index("x"), lax.axis_index("y"), lax.axis_index("c"))
    lands = [lax.dynamic_update_slice(lax.empty((N_DEV,) + s.shape, s.dtype), s[None], (mine,) + (0,) * s.ndim) for s in shards]
    res = _chained(
        body, name=name, link=-1, in_specs=[HBM_SPEC] * (2 * n),
        out_specs=[SEM_SPEC] * 2 + [HBM_SPEC] * (2 * n) + [pl.BlockSpec(memory_space=pltpu.VMEM)],
        out_shape=[pltpu.SemaphoreType.DMA((4 * n,)), pltpu.SemaphoreType.DMA((4 * n,))]
        + [pltpu.HBM(s.shape, s.dtype) for s in shards] + [pltpu.HBM(l.shape, l.dtype) for l in lands] + [TOKEN],
        input_output_aliases={i: 2 + i for i in range(2 * n)},
        compiler_params=pltpu.CompilerParams(has_side_effects=EFFECT),
    )(*[_in_hbm(s) for s in shards], *[_in_hbm(l) for l in lands])
    return n, res[:2], res[2:2 + n], res[2 + n:2 + 2 * n]


def _gather_mid(name, state):
    n, sems, shards, lands = state

    def body(*refs):
        ins, outs, sems_in = refs[:n], refs[n:2 * n], refs[2 * n:2 * n + 2]
        sems_out = refs[2 * n + 2:2 * n + 4]
        for w in range(n):
            sends, recvs = _gather_copies(ins, outs, sems_in, w)
            for cp in sends:
                cp.wait_send()
            for cp in recvs:
                cp.wait_recv()
            for cp in _forward_copies(outs, sems_out, w)[0]:
                cp.start()
        refs[-1][...] = jnp.zeros_like(refs[-1])

    res = _chained(
        body, name=name, link=-1, in_specs=[HBM_SPEC] * (2 * n) + [SEM_SPEC] * 2,
        out_specs=[SEM_SPEC] * 2 + [HBM_SPEC] * n + [pl.BlockSpec(memory_space=pltpu.VMEM)],
        out_shape=[pltpu.SemaphoreType.DMA((3 * n,)), pltpu.SemaphoreType.DMA((3 * n,))]
        + [pltpu.HBM(l.shape, l.dtype) for l in lands] + [TOKEN],
        input_output_aliases={n + i: 2 + i for i in range(n)},
        compiler_params=pltpu.CompilerParams(has_side_effects=EFFECT),
    )(*shards, *lands, *sems)
    return n, res[:2], res[2:2 + n]


def _gather_end(name, state):
    n, sems, lands = state

    def body(*refs):
        outs, sems_in = refs[:n], refs[n:n + 2]
        for w in range(n):
            sends, recvs = _forward_copies(outs, sems_in, w)
            for cp in sends:
                cp.wait_send()
            for cp in recvs:
                cp.wait_recv()

    return _chained(
        body, name=name, in_specs=[HBM_SPEC] * n + [SEM_SPEC] * 2, out_specs=[HBM_SPEC] * n,
        out_shape=[pltpu.HBM(l.shape, l.dtype) for l in lands], input_output_aliases={i: i for i in range(n)},
        compiler_params=pltpu.CompilerParams(has_side_effects=EFFECT),
    )(*lands, *sems)


def _exchange_copies(ins, outs, sems, w):
    send_sems, recv_sems, local_sems = sems
    x, y, c, _ = _place()
    mine = _block_of(x, y, c)
    local = pltpu.make_async_copy(ins[w].at[mine], outs[w].at[mine], local_sems.at[w])
    remote = []
    for k in range(1, N_DEV):
        px, py, pc = x ^ (k >> 2), y ^ ((k >> 1) & 1), c ^ (k & 1)
        remote.append(pltpu.make_async_remote_copy(
            src_ref=ins[w].at[_block_of(px, py, pc)], dst_ref=outs[w].at[mine], send_sem=send_sems.at[7 * w + k - 1],
            recv_sem=recv_sems.at[7 * w + k - 1], device_id=(px, py, pc), device_id_type=MESH))
    return local, remote


def _exchange_start(name, stacks):
    n = len(stacks)

    def body(*refs):
        ins, outs, sems = refs[:n], refs[n:2 * n], refs[2 * n:2 * n + 3]
        for w in range(n):
            local, remote = _exchange_copies(ins, outs, sems, w)
            local.start()
            for cp in remote:
                cp.start()
        refs[-1][...] = jnp.zeros_like(refs[-1])

    lands = [lax.empty(s.shape, s.dtype) for s in stacks]
    if any(s is _Chain.last for s in stacks):
        _Chain.last = None
    res = _chained(
        body, name=name, link=-1, in_specs=[HBM_SPEC] * (2 * n),
        out_specs=[SEM_SPEC] * 3 + [HBM_SPEC] * (2 * n) + [pl.BlockSpec(memory_space=pltpu.VMEM)],
        out_shape=[pltpu.SemaphoreType.DMA((7 * n,)), pltpu.SemaphoreType.DMA((7 * n,)), pltpu.SemaphoreType.DMA((n,))]
        + [pltpu.HBM(s.shape, s.dtype) for s in stacks] * 2 + [TOKEN],
        input_output_aliases={i: 3 + i for i in range(2 * n)},
        compiler_params=pltpu.CompilerParams(has_side_effects=EFFECT),
    )(*[_in_hbm(s) for s in stacks], *[_in_hbm(l) for l in lands])
    return n, res[:3], res[3:3 + n], res[3 + n:3 + 2 * n]


def _exchange_wait(name, state):
    n, sems, stacks, lands = state

    def body(*refs):
        ins, outs, sems_in = refs[:n], refs[n:2 * n], refs[2 * n:2 * n + 3]
        for w in range(n):
            local, remote = _exchange_copies(ins, outs, sems_in, w)
            local.wait()
            for cp in remote:
                cp.wait_send()
                cp.wait_recv()

    return _chained(
        body, name=name, in_specs=[HBM_SPEC] * (2 * n) + [SEM_SPEC] * 3, out_specs=[HBM_SPEC] * n,
        out_shape=[pltpu.HBM(l.shape, l.dtype) for l in lands], input_output_aliases={n + i: i for i in range(n)},
        compiler_params=pltpu.CompilerParams(has_side_effects=EFFECT),
    )(*stacks, *lands, *sems)


N_CHIP = N_DEV // 2


def _pair_copies(stack, land, sems):
    send_sems, recv_sems, _ = sems
    x, y, c, _ = _place()
    remote = [pltpu.make_async_remote_copy(src_ref=stack.at[2 * k + 1 - c], dst_ref=land.at[k], send_sem=send_sems.at[k],
                                           recv_sem=recv_sems.at[k], device_id=(x, y, 1 - c), device_id_type=MESH)
              for k in range(N_CHIP)]
    return [], remote


def _chip_copies(pairs, land, sems):
    send_sems, recv_sems, local_sems = sems
    x, y, c, _ = _place()
    mine = 2 * x + y
    local = pltpu.make_async_copy(pairs.at[mine], land.at[mine], local_sems.at[0])
    remote = []
    for m in range(1, N_CHIP):
        px, py = x ^ (m >> 1), y ^ (m & 1)
        remote.append(pltpu.make_async_remote_copy(
            src_ref=pairs.at[2 * px + py], dst_ref=land.at[mine], send_sem=send_sems.at[m - 1],
            recv_sem=recv_sems.at[m - 1], device_id=(px, py, c), device_id_type=MESH))
    return local, remote


def _stage_start(name, copies_of, src_arr, n_land, n_sems):
    half = (N_CHIP,) + src_arr.shape[1:]

    def body(*refs):
        local, remote = copies_of(refs[0], *refs[1:1 + n_land], refs[1 + n_land:4 + n_land])
        for cp in remote + (local if isinstance(local, list) else [local]):
            cp.start()
        refs[-1][...] = jnp.zeros_like(refs[-1])

    if src_arr is _Chain.last:
        _Chain.last = None
    lands = [lax.empty(half, src_arr.dtype) for _ in range(n_land)]
    res = _chained(
        body, name=name, link=-1, in_specs=[HBM_SPEC] * (1 + n_land),
        out_specs=[SEM_SPEC] * 3 + [HBM_SPEC] * (1 + n_land) + [pl.BlockSpec(memory_space=pltpu.VMEM)],
        out_shape=[pltpu.SemaphoreType.DMA((n,)) for n in n_sems] + [pltpu.HBM(src_arr.shape, src_arr.dtype)]
        + [pltpu.HBM(half, src_arr.dtype)] * n_land + [TOKEN],
        input_output_aliases={i: 3 + i for i in range(1 + n_land)},
        compiler_params=pltpu.CompilerParams(has_side_effects=EFFECT),
    )(_in_hbm(src_arr), *[_in_hbm(l) for l in lands])
    return res[:3], res[3], res[4:4 + n_land]


def _stage_wait(name, copies_of, state):
    sems, src_arr, lands = state
    n_land = len(lands)

    def body(*refs):
        local, remote = copies_of(refs[0], *refs[1:1 + n_land], refs[1 + n_land:4 + n_land])
        for cp in (local if isinstance(local, list) else [local]):
            cp.wait()
        for cp in remote:
            cp.wait_send()
            cp.wait_recv()

    return _chained(
        body, name=name, in_specs=[HBM_SPEC] * (1 + n_land) + [SEM_SPEC] * 3, out_specs=[HBM_SPEC] * n_land,
        out_shape=[pltpu.HBM(l.shape, l.dtype) for l in lands],
        input_output_aliases={1 + i: i for i in range(n_land)},
        compiler_params=pltpu.CompilerParams(has_side_effects=EFFECT),
    )(src_arr, *lands, *sems)


def _pair_sum(name, stack, land, tm=64):
    n, rows, cols = land.shape
    tm = _tile(rows, tm)

    def body(c_ref, a_ref, b_ref, o_ref):
        o_ref[...] = (a_ref[...].astype(F32) + b_ref[...].astype(F32)).astype(o_ref.dtype)

    core = lax.axis_index("c").astype(jnp.int32).reshape(1)
    blk = pl.BlockSpec((n, tm, cols), lambda i, c_ref: (0, i, 0))
    mine = pl.BlockSpec((n, None, tm, cols), lambda i, c_ref: (0, c_ref[0], i, 0))
    _Chain.last = pl.pallas_call(
        body, name=name,
        grid_spec=pltpu.PrefetchScalarGridSpec(num_scalar_prefetch=1, grid=(rows // tm,), in_specs=[mine, blk], out_specs=blk),
        out_shape=jax.ShapeDtypeStruct(land.shape, land.dtype), compiler_params=_rowwise_params("parallel"),
    )(core, stack.reshape(n, 2, rows, cols), land)
    return _Chain.last


def _unstack_cols(w):
    return w.transpose(1, 0, 2).reshape(w.shape[1], N_DEV * w.shape[2])


def _stack_cols(w):
    return w.reshape(w.shape[0], N_DEV, w.shape[1] // N_DEV).transpose(1, 0, 2)


def _rope_tables(positions):
    half = MLA_ROPE // 2
    inv_freq = ROPE_THETA ** (-jnp.arange(half, dtype=F32) / half)
    ang = positions.astype(F32)[:, None] * inv_freq
    cos, sin = jnp.cos(ang), jnp.sin(ang)
    t = positions.shape[0]
    cos_t = jnp.concatenate([jnp.ones((t, MLA_NOPE), F32), cos, cos], axis=1)
    sin_t = jnp.concatenate([jnp.zeros((t, MLA_NOPE), F32), -sin, sin], axis=1)
    idx = jnp.arange(MLA_QK)
    partner = jnp.where(idx < MLA_NOPE, -1, jnp.where(idx < MLA_NOPE + half, idx + half, idx - half))
    swap = (idx[:, None] == partner[None, :]).astype(BF16)
    return cos_t, sin_t, swap


def _ffn_fwd(tag, x, gain, wt):
    h = _rms_fwd(tag + "_rms", x, gain, BF16)
    g = _mm_stack_nt_out(tag + "_gate", h, wt[tag + '_w_gate'], BF16)
    u, a = _mm_stack_nt_out(tag + "_up", h, wt[tag + '_w_up'], BF16,
                            fuse=(lambda u_blk, g_blk: (u_blk, g_blk * jax.nn.sigmoid(g_blk) * u_blk), [g]))
    y = _mm_stack_sum(tag + "_down", a, wt[tag + '_w_down'], F32, scale=0.5, res=x)
    return y, (x, h, g, u, a)


def _ffn_bwd_weights(tag, dy, saved, wd, comm):
    x, h, g, u, a = saved
    comm.grads({tag + '_w_down': _mm_stack_tn_right(tag + "_dwd", a, dy, BF16, scale=0.5)})

    def dact(da, g_blk, u_blk):
        sig = jax.nn.sigmoid(g_blk)
        return da * u_blk * sig * (1.0 + g_blk * (1.0 - sig)), da * g_blk * sig

    dg, du = _mm_stack_nt_out(tag + "_da", dy, wd, BF16, scale=0.5, fuse=(dact, [g, u]))
    comm.advance()
    comm.grads({tag + '_w_gate': _mm_stack_tn_right(tag + "_dwg", dg, h, BF16)})
    dwu = _mm_stack_tn_right(tag + "_dwu", du, h, BF16)
    comm.advance()
    comm.grads({tag + '_w_up': dwu})
    return dg, du


def _ffn_bwd_input(tag, dy, saved, dgu, gain, wg, wu, comm):
    dg, du = dgu
    dh = _mm_stack_sum(tag + "_dh", (dg, du), (wg, wu), F32)
    comm.advance()
    return _rms_bwd(tag + "_drms", saved[0], gain, dh, res=dy)


def _local_step(x, mem, positions, sm, comm, target):
    t, d = x.shape
    nm = mem.shape[0]
    gs, gw = {}, {}

    class Weights(dict):
        def __missing__(self, name):
            self.update(comm.weights(next(k for k, names in enumerate(GATHERS) if name in names)))
            return self[name]

    wt = Weights()

    x1, ffn1_saved = _ffn_fwd("ffn1", x, sm['ffn1_norm'], wt)

    h2 = _rms_fwd("mix_rms", x1, sm['mix_norm'], BF16)
    w_in_ref = _unstack_cols(wt['w_in'])
    pieces = []
    for name, n in IN_PAD:
        piece = w_in_ref[:, REF_OFF[name]:REF_OFF[name] + REF_SIZE[name]]
        if n != REF_SIZE[name]:
            piece = jnp.pad(piece, ((0, 0), (0, n - REF_SIZE[name])))
        pieces.append(piece)
    w_in = jnp.concatenate(pieces, axis=1)
    z = _mm2("mix_in", h2, w_in, NN, F32)
    zs = {name: z[:, PAD_OFF[name]:PAD_OFF[name] + n] for name, n in IN_PAD}

    cq = _rms_fwd("mla_q_a_rms", zs['zq'], sm['q_a_norm'], BF16)
    q_raw = _mm_stack_nt_out("mla_q_up", cq, wt['w_q_up'], F32)
    ckv = _rms_fwd("mla_kv_a_rms", zs['zkv'], sm['kv_a_norm'], BF16)
    kv = _mm_stack_out("mla_kv_up", ckv, wt['w_kv_up'], F32)
    zkr = zs['zkr'][:, :MLA_ROPE]
    k_raw = jnp.concatenate([kv[:, :, :MLA_NOPE], jnp.broadcast_to(zkr[None], (MLA_HEADS, t, MLA_ROPE))], axis=2)
    v_mla = kv[:, :, MLA_NOPE:].astype(BF16)
    cos_t, sin_t, swap = _rope_tables(positions)
    q_raw2, k_raw2 = q_raw.reshape(MLA_HEADS * t, MLA_QK), k_raw.reshape(MLA_HEADS * t, MLA_QK)
    rope = (cos_t, sin_t, swap)
    qf = _rms_fwd("mla_q_rms", q_raw2, sm['mla_q_norm'], BF16, rope=rope).reshape(MLA_HEADS, t, MLA_QK)
    kf = _rms_fwd("mla_k_rms", k_raw2, sm['mla_k_norm'], BF16, rope=rope).reshape(MLA_HEADS, t, MLA_QK)
    o_mla = _attn_fwd("mla_attn", qf, kf, v_mla, MLA_QK ** -0.5, True)

    w_g2 = jnp.pad(_unstack_cols(wt['gla_w_gate2']), ((0, LANES - GLA_GATE_RANK), (0, 0)))
    pre = _mm2("gla_gate_pre", zs['zg'], w_g2, NN, F32)
    log_a = _gate_fwd("gla_gate", pre, sm['gla_b_gate'])
    o_gla_raw, states = _gla_fwd("gla_scan", zs['gq'], zs['gk'], zs['gv'], log_a, GLA_HEADS)
    o_gla_n = _rms_fwd("gla_out_rms", o_gla_raw, sm['gla_out_norm'], F32)
    o_gla = _swiglu_fwd("gla_out_gate", zs['zr'], o_gla_n, BF16)

    cat = jnp.concatenate([o_mla, o_gla], axis=1)
    w_out = wt['w_out'].reshape(d, d)
    x2 = _mm2("mix_out", cat, w_out, NN, F32, res=x1)

    w_mq, w_mk, w_mv = (wt[n].reshape(d, MEM_HEADS * MEM_HEAD_DIM) for n in ('mem_w_q', 'mem_w_k', 'mem_w_v'))
    hq = _rms_fwd("mem_attn_rms", x2, sm['mem_attn_norm'], BF16)
    hm = _rms_fwd("mem_rms", mem, sm['mem_norm'], BF16)

    def heads_out(name, a, b, out_dtype):
        m, kk = a.shape
        tm = _tile(m, 512)
        return _mm(name, a, b, (m // tm, MEM_HEADS, 1), ((tm, kk), lambda i, h, k: (i, 0)),
                   ((kk, MEM_HEAD_DIM), lambda i, h, k: (0, h)), ((None, tm, MEM_HEAD_DIM), lambda i, h, k: (h, i, 0)),
                   (MEM_HEADS, m, MEM_HEAD_DIM), out_dtype, NN)

    mq_raw = heads_out("mem_q", hq, w_mq, F32)
    mk_raw = heads_out("mem_k", hm, w_mk, F32)
    mv = heads_out("mem_v", hm, w_mv, BF16)
    mq = _rms_fwd("mem_q_rms", mq_raw.reshape(MEM_HEADS * t, MEM_HEAD_DIM), sm['mem_q_norm'], BF16)
    mk = _rms_fwd("mem_k_rms", mk_raw.reshape(MEM_HEADS * nm, MEM_HEAD_DIM), sm['mem_k_norm'], BF16)
    mq, mk = mq.reshape(MEM_HEADS, t, MEM_HEAD_DIM), mk.reshape(MEM_HEADS, nm, MEM_HEAD_DIM)
    o_mem = _attn_fwd("mem_attn", mq, mk, mv, MEM_HEAD_DIM ** -0.5, False)
    w_mo = wt['mem_w_o']
    mo_cols = w_mo.shape[2]
    tm = _tile(t, 512)
    x3 = _mm("mem_out", o_mem, w_mo, (t // tm, N_DEV, 1), ((tm, o_mem.shape[1]), lambda i, j, k: (i, 0)),
             ((None, o_mem.shape[1], mo_cols), lambda i, j, k: (j, 0, 0)), ((tm, mo_cols), lambda i, j, k: (i, j)),
             (t, d), F32, NN, res=x2)

    y, ffn2_saved = _ffn_fwd("ffn2", x3, sm['ffn2_norm'], wt)
    dy, dy_narrow, loss_lanes = _loss("loss", y, target)

    dgu = _ffn_bwd_weights("ffn2", dy_narrow, ffn2_saved, wt['ffn2_w_down'], comm)
    dx3, gs['ffn2_norm'] = _ffn_bwd_input("ffn2", dy, ffn2_saved, dgu, sm['ffn2_norm'],
                                          wt['ffn2_w_gate'], wt['ffn2_w_up'], comm)

    do_mem = _mm("mem_do", dx3, w_mo, (t // tm, MEM_HEADS, N_DEV), ((tm, mo_cols), lambda i, h, k: (i, k)),
                 ((None, MEM_HEAD_DIM, mo_cols), lambda i, h, k: (k, h, 0)), ((tm, MEM_HEAD_DIM), lambda i, h, k: (i, h)),
                 (t, MEM_HEADS * MEM_HEAD_DIM), BF16, NT)
    tk = _tile(t, 512)
    gw['mem_w_o'] = _mm("mem_dwo", o_mem, dx3, (N_DEV, 1, t // tk), ((tk, o_mem.shape[1]), lambda j, i, k: (k, 0)),
                        ((tk, mo_cols), lambda j, i, k: (k, j)), ((None, o_mem.shape[1], mo_cols), lambda j, i, k: (j, 0, 0)),
                        w_mo.shape, BF16, TN)
    dmq, dmk, dmv = _attn_bwd("mem_dattn", mq, mk, mv, do_mem, MEM_HEAD_DIM ** -0.5, False)
    dmq_raw, gs['mem_q_norm'] = _rms_bwd("mem_q_drms", mq_raw.reshape(MEM_HEADS * t, MEM_HEAD_DIM), sm['mem_q_norm'],
                                         dmq.reshape(MEM_HEADS * t, MEM_HEAD_DIM))
    dmk_raw, gs['mem_k_norm'] = _rms_bwd("mem_k_drms", mk_raw.reshape(MEM_HEADS * nm, MEM_HEAD_DIM), sm['mem_k_norm'],
                                         dmk.reshape(MEM_HEADS * nm, MEM_HEAD_DIM))
    dmq_raw = dmq_raw.reshape(MEM_HEADS, t, MEM_HEAD_DIM)
    dmk_raw = dmk_raw.reshape(MEM_HEADS, nm, MEM_HEAD_DIM)

    def heads_in_nt(name, a, b, res=None):
        m, n = a.shape[1], b.shape[0]
        tm_, tn_ = _tile(m, 512), _tile(n, 1024)
        return _mm(name, a, b, (m // tm_, n // tn_, MEM_HEADS), ((None, tm_, MEM_HEAD_DIM), lambda i, j, k: (k, i, 0)),
                   ((tn_, MEM_HEAD_DIM), lambda i, j, k: (j, k)), ((tm_, tn_), lambda i, j, k: (i, j)), (m, n), F32, NT,
                   None, res)

    def heads_tn(name, a, b):
        m, kp = a.shape
        tm_, tk_ = _tile(kp, 1024), _tile(m, 512)
        return _mm(name, a, b, (kp // tm_, MEM_HEADS, m // tk_), ((tk_, tm_), lambda i, h, k: (k, i)),
                   ((None, tk_, MEM_HEAD_DIM), lambda i, h, k: (h, k, 0)), ((tm_, MEM_HEAD_DIM), lambda i, h, k: (i, h)),
                   (kp, MEM_HEADS * MEM_HEAD_DIM), BF16, TN)

    dhq = heads_in_nt("mem_dhq", dmq_raw, w_mq)
    gw['mem_w_q'] = heads_tn("mem_dwq", hq, dmq_raw).reshape(wt['mem_w_q'].shape)
    dhm = heads_in_nt("mem_dhm_k", dmk_raw, w_mk)
    dhm = heads_in_nt("mem_dhm_v", dmv, w_mv, res=dhm)
    gw['mem_w_k'] = heads_tn("mem_dwk", hm, dmk_raw).reshape(wt['mem_w_k'].shape)
    gw['mem_w_v'] = heads_tn("mem_dwv", hm, dmv).reshape(wt['mem_w_v'].shape)
    _, gs['mem_norm'] = _rms_bwd("mem_drms", mem, sm['mem_norm'], dhm)
    comm.grads({n: gw[n] for n in MEMORY})
    dx2, gs['mem_attn_norm'] = _rms_bwd("mem_attn_drms", x2, sm['mem_attn_norm'], dhq, res=dx3)

    dcat = _mm2("mix_dcat", dx2, w_out, NT, F32)
    comm.grads({'w_out': _mm2("mix_dwout", cat, dx2, TN, BF16, tm=1024, tn=2048, tk=512).reshape(wt['w_out'].shape)})
    do_mla, do_gla = dcat[:, :MLA_HEADS * MLA_V], dcat[:, MLA_HEADS * MLA_V:]

    dzr, dgn = _swiglu_bwd("gla_out_dgate", do_gla, zs['zr'], o_gla_n, F32)
    do_gla_raw, gs['gla_out_norm'] = _rms_bwd("gla_out_drms", o_gla_raw, sm['gla_out_norm'], dgn)
    dgq, dgk, dgv, dlog_a = _gla_bwd("gla_dscan", zs['gq'], zs['gk'], zs['gv'], log_a, states, do_gla_raw, GLA_HEADS)
    dpre, gs['gla_b_gate'] = _gate_bwd("gla_dgate", pre, sm['gla_b_gate'], dlog_a)
    dw_g2 = _mm2("gla_dwgate", zs['zg'], dpre, TN, BF16, tk=512)
    comm.grads({'gla_w_gate2': _stack_cols(dw_g2[:GLA_GATE_RANK])})
    dzg = _mm2("gla_dzg", dpre, w_g2, NT, F32)

    dqf, dkf, dv_mla = _attn_bwd("mla_dattn", qf, kf, v_mla, do_mla, MLA_QK ** -0.5, True)
    dq_raw, gs['mla_q_norm'] = _rms_bwd("mla_q_drms", q_raw2, sm['mla_q_norm'], dqf.reshape(MLA_HEADS * t, MLA_QK), rope=rope)
    dk_raw, gs['mla_k_norm'] = _rms_bwd("mla_k_drms", k_raw2, sm['mla_k_norm'], dkf.reshape(MLA_HEADS * t, MLA_QK), rope=rope)
    dq_raw = dq_raw.reshape(MLA_HEADS, t, MLA_QK)
    dk_raw = dk_raw.reshape(MLA_HEADS, t, MLA_QK)
    dkv = jnp.concatenate([dk_raw[:, :, :MLA_NOPE], dv_mla], axis=2)
    dzkr = jnp.sum(dk_raw[:, :, MLA_NOPE:], axis=0)
    comm.grads({'w_q_up': _mm_stack_tn_right("mla_dwq", dq_raw, cq, BF16),
                'w_kv_up': _mm_stack_tn_left("mla_dwkv", ckv, dkv, BF16)})
    dcq = _mm_stack_sum("mla_dcq", dq_raw, wt['w_q_up'], F32)
    dckv = _mm_stack_nt_sum("mla_dckv", dkv, wt['w_kv_up'], F32)
    dzq, gs['q_a_norm'] = _rms_bwd("mla_q_a_drms", zs['zq'], sm['q_a_norm'], dcq)
    dzkv, gs['kv_a_norm'] = _rms_bwd("mla_kv_a_drms", zs['zkv'], sm['kv_a_norm'], dckv)

    dzs = {'zq': dzq, 'zkv': dzkv, 'gq': dgq, 'gk': dgk, 'gv': dgv, 'zr': dzr,
           'zkr': jnp.pad(dzkr, ((0, 0), (0, LANES - MLA_ROPE))), 'zg': dzg}
    dz = jnp.concatenate([dzs[name].astype(BF16) for name, _ in IN_PAD], axis=1)
    dw_in = _mm2("mix_dwin", h2, dz, TN, BF16, tm=1024, tn=2048, tk=512)
    dw_in_ref = jnp.concatenate([dw_in[:, PAD_OFF[name]:PAD_OFF[name] + n] for name, n in IN_REF], axis=1)
    comm.grads({'w_in': _stack_cols(dw_in_ref)})
    dh2 = _mm2("mix_dh", dz, w_in, NT, F32)
    comm.advance()
    dx1, dx1_narrow, gs['mix_norm'] = _rms_bwd("mix_drms", x1, sm['mix_norm'], dh2, res=dx2, narrow=True)

    dgu = _ffn_bwd_weights("ffn1", dx1_narrow, ffn1_saved, wt['ffn1_w_down'], comm)
    grad_x, gs['ffn1_norm'] = _ffn_bwd_input("ffn1", dx1, ffn1_saved, dgu, sm['ffn1_norm'],
                                             wt['ffn1_w_gate'], wt['ffn1_w_up'], comm)
    return loss_lanes, grad_x, gs


def _pad_lanes(v):
    n = v.shape[1]
    return jnp.pad(v, ((0, 0), (0, -n % LANES)))


def _pack_small(vals):
    return jnp.concatenate([_pad_lanes(vals[n]) for n in SMALL], axis=1)


def _unpack_small(packed, like):
    out, off = {}, 0
    for n in SMALL:
        size = like[n].shape[1]
        out[n] = packed[:, off:off + size]
        off += size + (-size % LANES)
    return out


def kernel(x, mem, positions, ffn1_norm, ffn1_w_gate, ffn1_w_up, ffn1_w_down, mix_norm, w_in, q_a_norm, w_q_up, kv_a_norm, w_kv_up, mla_q_norm, mla_k_norm, gla_w_gate2, gla_b_gate, gla_out_norm, w_out, mem_attn_norm, mem_norm, mem_w_q, mem_w_k, mem_w_v, mem_w_o, mem_q_norm, mem_k_norm, ffn2_norm, ffn2_w_gate, ffn2_w_up, ffn2_w_down, loss_target, m_ffn1_norm, m_ffn1_w_gate, m_ffn1_w_up, m_ffn1_w_down, m_mix_norm, m_w_in, m_q_a_norm, m_w_q_up, m_kv_a_norm, m_w_kv_up, m_mla_q_norm, m_mla_k_norm, m_gla_w_gate2, m_gla_b_gate, m_gla_out_norm, m_w_out, m_mem_attn_norm, m_mem_norm, m_mem_w_q, m_mem_w_k, m_mem_w_v, m_mem_w_o, m_mem_q_norm, m_mem_k_norm, m_ffn2_norm, m_ffn2_w_gate, m_ffn2_w_up, m_ffn2_w_down, v_ffn1_norm, v_ffn1_w_gate, v_ffn1_w_up, v_ffn1_w_down, v_mix_norm, v_w_in, v_q_a_norm, v_w_q_up, v_kv_a_norm, v_w_kv_up, v_mla_q_norm, v_mla_k_norm, v_gla_w_gate2, v_gla_b_gate, v_gla_out_norm, v_w_out, v_mem_attn_norm, v_mem_norm, v_mem_w_q, v_mem_w_k, v_mem_w_v, v_mem_w_o, v_mem_q_norm, v_mem_k_norm, v_ffn2_norm, v_ffn2_w_gate, v_ffn2_w_up, v_ffn2_w_down):
    inp = dict(locals())
    x, mem, positions, target = inp['x'][0], inp['mem'][0], inp['positions'][0], inp['loss_target'][0]
    sm = {n: inp[n] for n in SMALL}
    out = {}

    def stored(key):
        name = key[2:] if key[:2] in ('m_', 'v_') else key
        return inp[key][0].T if name in TRANSPOSED else inp[key][0]

    def as_given(name, r):
        return r.T[None] if name in TRANSPOSED else r[None]

    class Comm:
        def __init__(self):
            self.gathers = {0: self.start(0)}
            self.forwards, self.exchanges, self.pairs = {}, [], []

        def start(self, k):
            return _gather_start(f"gather_start_{k}", [stored(n).astype(BF16) for n in GATHERS[k]])

        def forward(self, k):
            if k not in self.forwards:
                self.forwards[k] = _gather_mid(f"gather_mid_{k}", self.gathers[k])
                self.gathers.update({nxt: self.start(nxt) for nxt in NEXT_GATHERS.get(k, [])})

        def weights(self, k):
            self.forward(k)
            if k in EARLY_FORWARD:
                self.forward(EARLY_FORWARD[k])
            return dict(zip(GATHERS[k], _gather_end(f"gather_end_{k}", self.forwards[k])))

        def grads(self, stacks):
            names = list(stacks)
            if names[0] in TWO_STAGE:
                (n,) = names
                self.pairs.append((n, _stage_start("pair_start_" + n, _pair_copies, stacks[n], 1, (N_CHIP, N_CHIP, 1))))
            else:
                self.exchanges.append((names, _exchange_start("exchange_start_" + names[0], [stacks[n] for n in names])))

        def advance(self):
            for n, state in self.pairs:
                (land,) = _stage_wait("pair_wait_" + n, _pair_copies, state)
                pairs = _pair_sum("pair_sum_" + n, state[1], land)
                self.exchanges.append(([n], _stage_start("chip_start_" + n, _chip_copies, pairs, 1, (N_CHIP - 1, N_CHIP - 1, 1))))
            self.pairs = []

        def update(self, count):
            todo, self.exchanges = self.exchanges[:count], self.exchanges[count:]
            for names, state in todo:
                if names[0] in TWO_STAGE:
                    parts = _stage_wait("chip_wait_" + names[0], _chip_copies, state)
                else:
                    parts = _exchange_wait("exchange_wait_" + names[0], state)
                for n, p in zip(names, parts):
                    res = _adamw("adamw_" + n, stored(n), stored('m_' + n), stored('v_' + n), p)
                    for kind, r in zip(('grad_', 'delta_', 'new_m_', 'new_v_'), res):
                        out[kind + n] = as_given(n, r)

    _Chain.last = None
    comm = Comm()
    loss_lanes, grad_x, gs = _local_step(x, mem, positions, sm, comm, target)
    out['loss'] = lax.psum(jnp.sum(loss_lanes), ("x", "y", "c"))
    out['grad_x'] = grad_x[None]

    comm.update(len(comm.exchanges) - 3)
    small_parts = _all_gather("gather_small", [_pack_small(gs)])[0]
    res = _adamw("adamw_small", _pack_small(sm), _pack_small({n: inp['m_' + n] for n in SMALL}),
                 _pack_small({n: inp['v_' + n] for n in SMALL}), small_parts)
    for kind, r in zip(('grad_', 'delta_', 'new_m_', 'new_v_'), res):
        for n, val in _unpack_small(r, sm).items():
            out[kind + n] = val

    comm.update(3)

    names = ['loss', 'grad_x'] + [k + n for k in ('grad_', 'delta_', 'new_m_', 'new_v_') for n in WEIGHTS]
    return tuple(out[n] for n in names)
```

```python
import functools
import math

import jax
import jax.numpy as jnp
from jax import lax
from jax.experimental import pallas as pl
from jax.experimental.pallas import tpu as pltpu

F32 = jnp.float32
BF16 = jnp.bfloat16

N_DEV = 8
EPS = 1e-6
CHUNK = 64
MLA_HEADS, MLA_NOPE, MLA_ROPE, MLA_V = 8, 128, 64, 128
MLA_QK = MLA_NOPE + MLA_ROPE
MLA_Q_RANK, MLA_KV_RANK = 512, 256
ROPE_THETA = 10000.0
GLA_HEADS, GLA_DK, GLA_DV, GLA_GATE_RANK = 4, 128, 256, 16
GLA_TAU = 16.0
MEM_HEADS, MEM_HEAD_DIM = 4, 128
ADAM_LR, ADAM_B1, ADAM_B2, ADAM_EPS, ADAM_WD, ADAM_STEP = 0.001, 0.9, 0.999, 1e-08, 0.01, 10

V7X_VMEM_BYTES = 64 * 1024 * 1024
LANES = 128

NN = (((1,), (0,)), ((), ()))
NT = (((1,), (1,)), ((), ()))
TN = (((0,), (0,)), ((), ()))

WEIGHTS = ['ffn1_norm', 'ffn1_w_gate', 'ffn1_w_up', 'ffn1_w_down', 'mix_norm', 'w_in', 'q_a_norm', 'w_q_up',
           'kv_a_norm', 'w_kv_up', 'mla_q_norm', 'mla_k_norm', 'gla_w_gate2', 'gla_b_gate', 'gla_out_norm', 'w_out',
           'mem_attn_norm', 'mem_norm', 'mem_w_q', 'mem_w_k', 'mem_w_v', 'mem_w_o', 'mem_q_norm', 'mem_k_norm',
           'ffn2_norm', 'ffn2_w_gate', 'ffn2_w_up', 'ffn2_w_down']
SMALL = ['ffn1_norm', 'mix_norm', 'q_a_norm', 'kv_a_norm', 'mla_q_norm', 'mla_k_norm', 'gla_b_gate', 'gla_out_norm',
         'mem_attn_norm', 'mem_norm', 'mem_q_norm', 'mem_k_norm', 'ffn2_norm']
MIXER = ['w_in', 'w_q_up', 'w_kv_up', 'gla_w_gate2', 'w_out']
MEMORY = ['mem_w_q', 'mem_w_k', 'mem_w_v', 'mem_w_o']
TRANSPOSED = ['ffn1_w_gate', 'ffn1_w_up', 'ffn2_w_gate', 'ffn2_w_up', 'w_q_up']
TWO_STAGE = ['w_in', 'ffn1_w_down', 'ffn1_w_gate', 'ffn1_w_up']
GATHERS = [['ffn1_w_gate'], ['ffn1_w_up'], ['ffn1_w_down'], MIXER, MEMORY, ['ffn2_w_gate', 'ffn2_w_up', 'ffn2_w_down']]
NEXT_GATHERS = {0: [1], 1: [2], 2: [3], 3: [4, 5]}
EARLY_FORWARD = {4: 5}

IN_REF = [('zq', 512), ('zkv', 256), ('zkr', 64), ('gq', 512), ('gk', 512), ('gv', 1024), ('zg', 16), ('zr', 1024)]
IN_PAD = [('zq', 512), ('zkv', 256), ('gq', 512), ('gk', 512), ('gv', 1024), ('zr', 1024), ('zkr', 128), ('zg', 128)]
IN_WIDTH = sum(n for _, n in IN_REF)
IN_PAD_WIDTH = sum(n for _, n in IN_PAD)


def _offsets(layout):
    out, off = {}, 0
    for name, n in layout:
        out[name] = off
        off += n
    return out


REF_OFF, PAD_OFF = _offsets(IN_REF), _offsets(IN_PAD)
REF_SIZE = dict(IN_REF)


def _tile(n, pref):
    return pref if n % pref == 0 else n


def _block_bytes(blk, dtype):
    dims = [d for d in blk if d is not None]
    if len(dims) >= 1:
        dims[-1] = -(-dims[-1] // LANES) * LANES
    return math.prod(dims) * jnp.dtype(dtype).itemsize


def _vmem_limit(pipelined_bytes, resident_bytes=0):
    need = 2 * pipelined_bytes + resident_bytes + (8 << 20)
    return int(min(max(need, 32 << 20), V7X_VMEM_BYTES - (6 << 20)))


class _Chain:
    last = None


def _chained(body, *, in_specs, link=0, **kwargs):
    def call(*operands):
        dep = _Chain.last
        if dep is not None and any(o is dep for o in operands):
            dep = None
        if dep is None:
            res = pl.pallas_call(body, in_specs=in_specs, **kwargs)(*operands)
        else:
            n = len(operands)

            def chained_body(*refs):
                body(*refs[:n], *refs[n + 1:])

            res = pl.pallas_call(chained_body, in_specs=list(in_specs) + [pl.BlockSpec(memory_space=pl.ANY)],
                                 **kwargs)(*operands, dep)
        _Chain.last = res[link] if isinstance(res, (list, tuple)) else res
        return res

    return call


def _row_tile(rows, width, elements=1 << 18):
    lanes = -(-width // LANES) * LANES
    return _tile(rows, max(256, 1 << int(math.log2(max(1, elements // lanes)))))


def _rowwise_params(*semantics):
    return pltpu.CompilerParams(dimension_semantics=semantics, vmem_limit_bytes=48 << 20)


def _mm(name, a, b, grid, a_spec, b_spec, o_spec, out_shape, out_dtype, dims, scale=None, res=None, fuse=None):
    nk = grid[2]
    o_blk, o_map = o_spec
    acc_shape = tuple(d for d in o_blk if d is not None)
    extras = [res] if res is not None else (list(fuse[1]) if fuse else [])
    n_out = 2 if fuse else 1
    a_list, b_list = (list(a), list(b)) if isinstance(a, (tuple, list)) else ([a], [b])
    a_specs = a_spec if isinstance(a_spec, list) else [a_spec] * len(a_list)
    b_specs = b_spec if isinstance(b_spec, list) else [b_spec] * len(b_list)
    n_in = 2 * len(a_list)

    def body(*refs):
        a_refs, b_refs = refs[:n_in // 2], refs[n_in // 2:n_in]
        extra_refs = refs[n_in:n_in + len(extras)]
        out_refs = refs[n_in + len(extras):n_in + len(extras) + n_out]
        rest = refs[n_in + len(extras) + n_out:]
        r_ref = extra_refs[0] if res is not None else None
        o_ref = out_refs[0]

        def product():
            return sum(lax.dot_general(a_ref[...].astype(BF16), b_ref[...].astype(BF16), dims, preferred_element_type=F32)
                       for a_ref, b_ref in zip(a_refs, b_refs))

        def finish(r):
            if scale is not None:
                r = r * scale
            if r_ref is not None:
                r = r + r_ref[...].astype(F32)
            if fuse:
                for ref, val in zip(out_refs, fuse[0](r, *[e[...].astype(F32) for e in extra_refs])):
                    ref[...] = val.astype(ref.dtype)
            else:
                o_ref[...] = r.astype(o_ref.dtype)

        if nk == 1:
            finish(product())
        else:
            acc = rest[0]
            k = pl.program_id(2)

            @pl.when(k == 0)
            def _():
                acc[...] = product()

            @pl.when(k > 0)
            def _():
                acc[...] += product()

            @pl.when(k == nk - 1)
            def _():
                finish(acc[...])

    in_specs = [pl.BlockSpec(*spec) for spec in a_specs + b_specs]
    operands = a_list + b_list
    piped = sum(_block_bytes(spec[0], v.dtype) for spec, v in zip(a_specs + b_specs, operands))
    piped += _block_bytes(o_blk, out_dtype)
    for extra in extras:
        in_specs.append(pl.BlockSpec(o_blk, o_map))
        operands.append(extra)
        piped += _block_bytes(o_blk, extra.dtype)
    piped += (n_out - 1) * _block_bytes(o_blk, out_dtype)
    scratch = [pltpu.VMEM(acc_shape, F32)] if nk > 1 else []
    out_spec, out_sds = pl.BlockSpec(o_blk, o_map), jax.ShapeDtypeStruct(out_shape, out_dtype)
    return _chained(
        body, name=name, grid=grid, in_specs=in_specs, out_specs=(out_spec,) * n_out if fuse else out_spec,
        out_shape=(out_sds,) * n_out if fuse else out_sds, scratch_shapes=scratch,
        compiler_params=pltpu.CompilerParams(
            dimension_semantics=("parallel", "parallel", "arbitrary"),
            vmem_limit_bytes=_vmem_limit(piped, 3 * _block_bytes(acc_shape, F32))),
    )(*operands)


def _mm2(name, a, b, dims, out_dtype, tm=512, tn=1024, tk=2048, scale=None, res=None):
    if dims is NN:
        (m, kk), n = a.shape, b.shape[1]
    elif dims is NT:
        (m, kk), n = a.shape, b.shape[0]
    else:
        (kk, m), n = a.shape, b.shape[1]
    tm, tn, tk = _tile(m, tm), _tile(n, tn), _tile(kk, tk)
    a_spec = ((tk, tm), lambda i, j, k: (k, i)) if dims is TN else ((tm, tk), lambda i, j, k: (i, k))
    b_spec = ((tn, tk), lambda i, j, k: (j, k)) if dims is NT else ((tk, tn), lambda i, j, k: (k, j))
    return _mm(name, a, b, (m // tm, n // tn, kk // tk), a_spec, b_spec, ((tm, tn), lambda i, j, k: (i, j)),
               (m, n), out_dtype, dims, scale, res)


def _mm_stack_out(name, a, b, out_dtype, tm=512, tk=2048):
    (m, kk), (nj, _, n) = a.shape, b.shape
    tm, tk = _tile(m, tm), _tile(kk, tk)
    return _mm(name, a, b, (nj, m // tm, kk // tk), ((tm, tk), lambda j, i, k: (i, k)),
               ((None, tk, n), lambda j, i, k: (j, k, 0)), ((None, tm, n), lambda j, i, k: (j, i, 0)),
               (nj, m, n), out_dtype, NN)


def _mm_stack_nt_out(name, a, b, out_dtype, scale=None, fuse=None, tm=1024, tk=2048):
    (m, kk), (nj, n, _) = a.shape, b.shape
    tm, tk = _tile(m, tm), _tile(kk, tk)
    return _mm(name, a, b, (nj, m // tm, kk // tk), ((tm, tk), lambda j, i, k: (i, k)),
               ((None, n, tk), lambda j, i, k: (j, 0, k)), ((None, tm, n), lambda j, i, k: (j, i, 0)),
               (nj, m, n), out_dtype, NT, scale, fuse=fuse)


def _mm_stack_sum(name, a, b, out_dtype, scale=None, res=None, tm=512, tn=1024):
    (nj, m, f), n = (a[0] if isinstance(a, tuple) else a).shape, (b[0] if isinstance(b, tuple) else b).shape[2]
    tm, tn = _tile(m, tm), _tile(n, tn)
    pairs = list(zip(a, b)) if isinstance(a, tuple) else [(a, b)]
    step = 2 if nj % 2 == 0 else 1
    a_list, b_list, a_spec, b_spec = [], [], [], []
    for a_k, b_k in pairs:
        for s in range(step):
            a_list.append(a_k)
            b_list.append(b_k)
            a_spec.append(((None, tm, f), lambda i, j, k, s=s: (step * k + s, i, 0)))
            b_spec.append(((None, f, tn), lambda i, j, k, s=s: (step * k + s, 0, j)))
    return _mm(name, a_list, b_list, (m // tm, n // tn, nj // step), a_spec, b_spec, ((tm, tn), lambda i, j, k: (i, j)),
               (m, n), out_dtype, NN, scale, res)


def _mm_stack_nt_sum(name, a, b, out_dtype, res=None, tm=512, tn=1024):
    (nj, m, f), n = a.shape, b.shape[1]
    tm, tn = _tile(m, tm), _tile(n, tn)
    return _mm(name, a, b, (m // tm, n // tn, nj), ((None, tm, f), lambda i, j, k: (k, i, 0)),
               ((None, tn, f), lambda i, j, k: (k, j, 0)), ((tm, tn), lambda i, j, k: (i, j)),
               (m, n), out_dtype, NT, None, res)


def _mm_stack_tn_left(name, a, b, out_dtype, tm=1024, tk=512):
    (m, kp), (nj, _, n) = a.shape, b.shape
    tm, tk = _tile(kp, tm), _tile(m, tk)
    return _mm(name, a, b, (nj, kp // tm, m // tk), ((tk, tm), lambda j, i, k: (k, i)),
               ((None, tk, n), lambda j, i, k: (j, k, 0)), ((None, tm, n), lambda j, i, k: (j, i, 0)),
               (nj, kp, n), out_dtype, TN)


def _mm_stack_tn_right(name, a, b, out_dtype, scale=None, tn=2048, tk=1024):
    (nj, m, f), n = a.shape, b.shape[1]
    tn, tk = _tile(n, tn), _tile(m, tk)
    return _mm(name, a, b, (nj, n // tn, m // tk), ((None, tk, f), lambda j, i, k: (j, k, 0)),
               ((tk, tn), lambda j, i, k: (k, i)), ((None, f, tn), lambda j, i, k: (j, 0, i)),
               (nj, f, n), out_dtype, TN, scale)


def _split3(x):
    hi = x.astype(BF16)
    r1 = x - hi.astype(F32)
    mid = r1.astype(BF16)
    lo = (r1 - mid.astype(F32)).astype(BF16)
    return hi, mid, lo


def _swap_halves(x, swap):
    return sum(jnp.dot(part, swap, preferred_element_type=F32) for part in _split3(x))


def _rope_specs(rope, tm):
    cos_t, _, swap = rope
    nt = cos_t.shape[0] // tm
    tab = pl.BlockSpec((tm, cos_t.shape[1]), lambda i, c: (i % nt, 0))
    return [tab, tab, pl.BlockSpec(swap.shape, lambda i, c: (0, 0))]


def _rms_fwd(name, x, g, out_dtype, rope=None):
    rows, cols = x.shape
    d = g.shape[1]
    tm = _row_tile(rows if rope is None else rope[0].shape[0], d)

    def body(x_ref, g_ref, *refs):
        xf = x_ref[...].astype(F32)
        r = lax.rsqrt(jnp.mean(xf * xf, axis=-1, keepdims=True) + EPS)
        y = xf * r * g_ref[...]
        if rope is not None:
            c_ref, s_ref, p_ref = refs[:3]
            y = y * c_ref[...] + _swap_halves(y, p_ref[...]) * s_ref[...]
        refs[-1][...] = y.astype(refs[-1].dtype)

    return _chained(
        body, name=name, grid=(rows // tm, cols // d),
        in_specs=[pl.BlockSpec((tm, d), lambda i, c: (i, c)), pl.BlockSpec((1, d), lambda i, c: (0, 0))]
        + (_rope_specs(rope, tm) if rope is not None else []),
        out_specs=pl.BlockSpec((tm, d), lambda i, c: (i, c)),
        out_shape=jax.ShapeDtypeStruct((rows, cols), out_dtype),
        compiler_params=_rowwise_params("parallel", "parallel"),
    )(x, g, *(rope or ()))


def _rms_bwd(name, x, g, dy, res=None, rope=None, narrow=False):
    rows, cols = x.shape
    d = g.shape[1]
    tm = _row_tile(rows if rope is None else rope[0].shape[0], d)

    def body(*refs):
        x_ref, g_ref, dy_ref = refs[:3]
        n_out = 3 if narrow else 2
        dx_ref, dg_ref = refs[-n_out], refs[-1]
        r_ref = refs[3] if res is not None else None
        xf = x_ref[...].astype(F32)
        r = lax.rsqrt(jnp.mean(xf * xf, axis=-1, keepdims=True) + EPS)
        xhat = xf * r
        dyf = dy_ref[...].astype(F32)
        if rope is not None:
            c_ref, s_ref, p_ref = refs[-n_out - 3:-n_out]
            dyf = dyf * c_ref[...] + _swap_halves(dyf * s_ref[...], p_ref[...])

        @pl.when((pl.program_id(0) == 0) & (pl.program_id(1) == 0))
        def _():
            dg_ref[...] = jnp.zeros_like(dg_ref)

        dg_ref[...] += jnp.sum(dyf * xhat, axis=0, keepdims=True)
        dxh = dyf * g_ref[...]
        dx = r * (dxh - xhat * jnp.mean(dxh * xhat, axis=-1, keepdims=True))
        if r_ref is not None:
            dx = dx + r_ref[...].astype(F32)
        dx_ref[...] = dx
        if narrow:
            refs[-2][...] = dx.astype(BF16)

    blk = pl.BlockSpec((tm, d), lambda i, c: (i, c))
    in_specs = [blk, pl.BlockSpec((1, d), lambda i, c: (0, 0)), blk]
    operands = [x, g, dy]
    if res is not None:
        in_specs.append(blk)
        operands.append(res)
    if rope is not None:
        in_specs += _rope_specs(rope, tm)
        operands += list(rope)
    wide = [(blk, jax.ShapeDtypeStruct((rows, cols), F32))] + ([(blk, jax.ShapeDtypeStruct((rows, cols), BF16))] if narrow else [])
    outs = wide + [(pl.BlockSpec((1, d), lambda i, c: (0, 0)), jax.ShapeDtypeStruct((1, d), F32))]
    return _chained(
        body, name=name, grid=(rows // tm, cols // d), in_specs=in_specs,
        out_specs=tuple(spec for spec, _ in outs), out_shape=tuple(sds for _, sds in outs),
        compiler_params=_rowwise_params("arbitrary", "arbitrary"),
    )(*operands)


def _swiglu_fwd(name, g, u, out_dtype, tm=256):
    rows, cols = g.shape
    tm = _tile(rows, tm)

    def body(g_ref, u_ref, o_ref):
        gf = g_ref[...].astype(F32)
        o_ref[...] = (gf * jax.nn.sigmoid(gf) * u_ref[...].astype(F32)).astype(o_ref.dtype)

    blk = pl.BlockSpec((tm, cols), lambda i: (i, 0))
    return _chained(
        body, name=name, grid=(rows // tm,), in_specs=[blk, blk], out_specs=blk,
        out_shape=jax.ShapeDtypeStruct((rows, cols), out_dtype),
        compiler_params=_rowwise_params("parallel"),
    )(g, u)


def _swiglu_bwd(name, da, g, u, out_dtype, tm=256):
    rows, cols = g.shape
    tm = _tile(rows, tm)

    def body(da_ref, g_ref, u_ref, dg_ref, du_ref):
        gf = g_ref[...].astype(F32)
        daf = da_ref[...].astype(F32)
        sig = jax.nn.sigmoid(gf)
        du_ref[...] = (daf * gf * sig).astype(du_ref.dtype)
        dg_ref[...] = (daf * u_ref[...].astype(F32) * sig * (1.0 + gf * (1.0 - sig))).astype(dg_ref.dtype)

    blk = pl.BlockSpec((tm, cols), lambda i: (i, 0))
    sds = jax.ShapeDtypeStruct((rows, cols), out_dtype)
    return _chained(
        body, name=name, grid=(rows // tm,), in_specs=[blk, blk, blk], out_specs=(blk, blk), out_shape=(sds, sds),
        compiler_params=_rowwise_params("parallel"),
    )(da, g, u)


def _gate_fwd(name, pre, bias, tm=256):
    rows, cols = pre.shape
    tm = _tile(rows, tm)

    def body(p_ref, b_ref, o_ref):
        z = p_ref[...] + b_ref[...]
        o_ref[...] = (jnp.minimum(z, 0.0) - jnp.log(1.0 + jnp.exp(-jnp.abs(z)))) * (1.0 / GLA_TAU)

    blk = pl.BlockSpec((tm, cols), lambda i: (i, 0))
    return _chained(
        body, name=name, grid=(rows // tm,), in_specs=[blk, pl.BlockSpec((1, cols), lambda i: (0, 0))], out_specs=blk,
        out_shape=jax.ShapeDtypeStruct((rows, cols), F32),
        compiler_params=_rowwise_params("parallel"),
    )(pre, bias)


def _gate_bwd(name, pre, bias, dla, tm=256):
    rows, cols = pre.shape
    tm = _tile(rows, tm)

    def body(p_ref, b_ref, d_ref, dp_ref, db_ref):
        z = p_ref[...] + b_ref[...]
        dp = d_ref[...] * (1.0 / GLA_TAU) / (1.0 + jnp.exp(z))
        dp_ref[...] = dp

        @pl.when(pl.program_id(0) == 0)
        def _():
            db_ref[...] = jnp.zeros_like(db_ref)

        db_ref[...] += jnp.sum(dp, axis=0, keepdims=True)

    blk = pl.BlockSpec((tm, cols), lambda i: (i, 0))
    row = pl.BlockSpec((1, cols), lambda i: (0, 0))
    return _chained(
        body, name=name, grid=(rows // tm,), in_specs=[blk, row, blk], out_specs=(blk, row),
        out_shape=(jax.ShapeDtypeStruct((rows, cols), F32), jax.ShapeDtypeStruct((1, cols), F32)),
        compiler_params=_rowwise_params("arbitrary"),
    )(pre, bias, dla)


def _loss(name, y, target, tm=256):
    rows, d = y.shape
    tm = _tile(rows, tm)

    def body(y_ref, t_ref, dy_ref, dyb_ref, l_ref):
        err = y_ref[...] - t_ref[...]
        dy_ref[...] = err * (1.0 / d)
        dyb_ref[...] = (err * (1.0 / d)).astype(BF16)

        @pl.when(pl.program_id(0) == 0)
        def _():
            l_ref[...] = jnp.zeros_like(l_ref)

        sq = (err * err).reshape(tm // 8, 8, d)
        l_ref[...] += jnp.sum(sq, axis=0) * (0.5 / d)

    blk = pl.BlockSpec((tm, d), lambda i: (i, 0))
    return _chained(
        body, name=name, grid=(rows // tm,), in_specs=[blk, blk],
        out_specs=(blk, blk, pl.BlockSpec((8, d), lambda i: (0, 0))),
        out_shape=(jax.ShapeDtypeStruct((rows, d), F32), jax.ShapeDtypeStruct((rows, d), BF16),
                   jax.ShapeDtypeStruct((8, d), F32)),
        compiler_params=_rowwise_params("arbitrary"),
    )(y, target)


def _per_query_block(causal, nq, tq, tk, inner):
    if not causal:
        inner(0, tk)
        return
    for block in range(nq):
        pl.when(pl.program_id(1) == block)(functools.partial(inner, block * tq, (block + 1) * tq))


def _scores(q, k, scale, causal, q0):
    s = lax.dot_general(q, k, NT, preferred_element_type=F32) * scale
    if causal:
        qc = (q0 + lax.broadcasted_iota(jnp.int32, s.shape, 0)) // CHUNK
        kc = lax.broadcasted_iota(jnp.int32, s.shape, 1) // CHUNK
        s = jnp.where(kc <= qc, s, -1e30)
    e = jnp.exp(s - jnp.max(s, axis=-1, keepdims=True))
    return e, jnp.sum(e, axis=-1, keepdims=True)


def _attn_fwd(name, q, k, v, scale, causal, tq=256):
    nh, t, dk = q.shape
    tk, dv = k.shape[1], v.shape[2]
    tq = _tile(t, tq)

    def body(q_ref, k_ref, v_ref, o_ref):
        def inner(q0, kv):
            e, l = _scores(q_ref[...], k_ref[0:kv, :], scale, causal, q0)
            o = jnp.dot(e.astype(BF16), v_ref[0:kv, :], preferred_element_type=F32)
            o_ref[...] = (o / l).astype(o_ref.dtype)

        _per_query_block(causal, t // tq, tq, tk, inner)

    return _chained(
        body, name=name, grid=(nh, t // tq),
        in_specs=[pl.BlockSpec((None, tq, dk), lambda h, i: (h, i, 0)), pl.BlockSpec((None, tk, dk), lambda h, i: (h, 0, 0)),
                  pl.BlockSpec((None, tk, dv), lambda h, i: (h, 0, 0))],
        out_specs=pl.BlockSpec((tq, dv), lambda h, i: (i, h)),
        out_shape=jax.ShapeDtypeStruct((t, nh * dv), BF16),
        compiler_params=pltpu.CompilerParams(dimension_semantics=("parallel", "parallel"),
                                             vmem_limit_bytes=_vmem_limit(0, 6 * tq * tk * 4)),
    )(q, k, v)


def _attn_bwd(name, q, k, v, do, scale, causal, tq=256):
    nh, t, dk = q.shape
    tk, dv = k.shape[1], v.shape[2]
    tq = _tile(t, tq)

    def body(q_ref, k_ref, v_ref, do_ref, dq_ref, dk_ref, dv_ref):
        @pl.when(pl.program_id(1) == 0)
        def _():
            dk_ref[...] = jnp.zeros_like(dk_ref)
            dv_ref[...] = jnp.zeros_like(dv_ref)

        def inner(q0, kv):
            qb, kb = q_ref[...], k_ref[0:kv, :]
            e, l = _scores(qb, kb, scale, causal, q0)
            p = e / l
            dob = do_ref[...].astype(BF16)
            dp = lax.dot_general(dob, v_ref[0:kv, :], NT, preferred_element_type=F32)
            ds = (p * (dp - jnp.sum(p * dp, axis=-1, keepdims=True)) * scale).astype(BF16)
            dq_ref[...] = jnp.dot(ds, kb, preferred_element_type=F32)
            dk_ref[0:kv, :] += lax.dot_general(ds, qb, TN, preferred_element_type=F32)
            dv_ref[0:kv, :] += lax.dot_general(p.astype(BF16), dob, TN, preferred_element_type=F32)

        _per_query_block(causal, t // tq, tq, tk, inner)

    return _chained(
        body, name=name, grid=(nh, t // tq),
        in_specs=[pl.BlockSpec((None, tq, dk), lambda h, i: (h, i, 0)), pl.BlockSpec((None, tk, dk), lambda h, i: (h, 0, 0)),
                  pl.BlockSpec((None, tk, dv), lambda h, i: (h, 0, 0)), pl.BlockSpec((tq, dv), lambda h, i: (i, h))],
        out_specs=(pl.BlockSpec((None, tq, dk), lambda h, i: (h, i, 0)), pl.BlockSpec((None, tk, dk), lambda h, i: (h, 0, 0)),
                   pl.BlockSpec((None, tk, dv), lambda h, i: (h, 0, 0))),
        out_shape=(jax.ShapeDtypeStruct((nh, t, dk), F32), jax.ShapeDtypeStruct((nh, tk, dk), F32),
                   jax.ShapeDtypeStruct((nh, tk, dv), F32)),
        compiler_params=pltpu.CompilerParams(dimension_semantics=("parallel", "arbitrary"),
                                             vmem_limit_bytes=_vmem_limit(0, 10 * tq * tk * 4)),
    )(q, k, v, do)


def _tri(lower):
    r = lax.broadcasted_iota(jnp.int32, (CHUNK, CHUNK), 0)
    c = lax.broadcasted_iota(jnp.int32, (CHUNK, CHUNK), 1)
    return jnp.where((c <= r) if lower else (c >= r), 1.0, 0.0).astype(BF16)


def _tri_dot(tri, x):
    return sum(jnp.dot(tri, part, preferred_element_type=F32) for part in _split3(x))


def _gla_fwd(name, q, k, v, la, nh):
    t = q.shape[0]
    dk, dv = q.shape[1] // nh, v.shape[1] // nh
    nc = t // CHUNK

    def body(q_ref, k_ref, v_ref, g_ref, o_ref, st_ref, state):
        @pl.when(pl.program_id(0) == 0)
        def _():
            state[...] = jnp.zeros_like(state)

        g = g_ref[...]
        b = _tri_dot(_tri(True), g)
        b_end = jnp.sum(g, axis=0, keepdims=True)
        k_dec = (k_ref[...] * jnp.exp(b_end - b)).astype(BF16)
        decay = jnp.exp(b_end)
        qc = (q_ref[...] * (dk ** -0.5)).astype(BF16)
        vb = v_ref[...].astype(BF16)
        for h in range(nh):
            ks, vs = slice(h * dk, (h + 1) * dk), slice(h * dv, (h + 1) * dv)
            u_t = lax.dot_general(vb[:, vs], k_dec[:, ks], TN, preferred_element_type=F32)
            new = state[h] * decay[:, ks] + u_t
            state[h] = new
            st_ref[h] = new
            o_ref[:, vs] = lax.dot_general(qc[:, ks], new.astype(BF16), NT, preferred_element_type=F32)

    kblk = pl.BlockSpec((CHUNK, nh * dk), lambda n: (n, 0))
    vblk = pl.BlockSpec((CHUNK, nh * dv), lambda n: (n, 0))
    return _chained(
        body, name=name, grid=(nc,), in_specs=[kblk, kblk, vblk, kblk],
        out_specs=(vblk, pl.BlockSpec((nh, None, dv, dk), lambda n: (0, n, 0, 0))),
        out_shape=(jax.ShapeDtypeStruct((t, nh * dv), F32), jax.ShapeDtypeStruct((nh, nc, dv, dk), F32)),
        scratch_shapes=[pltpu.VMEM((nh, dv, dk), F32)],
        compiler_params=_rowwise_params("arbitrary"),
    )(q, k, v, la)


def _gla_bwd(name, q, k, v, la, states, do, nh):
    t = q.shape[0]
    dk, dv = q.shape[1] // nh, v.shape[1] // nh
    nc = t // CHUNK
    scale = dk ** -0.5

    def body(q_ref, k_ref, v_ref, g_ref, do_ref, st_ref, sp_ref, dq_ref, dk_ref, dv_ref, dg_ref, carry):
        i = pl.program_id(0)

        @pl.when(i == 0)
        def _():
            carry[...] = jnp.zeros_like(carry)

        g = g_ref[...]
        b = _tri_dot(_tri(True), g)
        b_end = jnp.sum(g, axis=0, keepdims=True)
        w = jnp.exp(b_end - b)
        decay = jnp.exp(b_end)
        k_dec = k_ref[...] * w
        k_decb = k_dec.astype(BF16)
        qc = (q_ref[...] * scale).astype(BF16)
        dob = do_ref[...].astype(BF16)
        vb = v_ref[...].astype(BF16)
        dk_dec, ddecay = [], []
        for h in range(nh):
            ks, vs = slice(h * dk, (h + 1) * dk), slice(h * dv, (h + 1) * dv)
            dq_ref[:, ks] = jnp.dot(dob[:, vs], st_ref[h].astype(BF16), preferred_element_type=F32) * scale
            g_t = carry[h] + lax.dot_general(dob[:, vs], qc[:, ks], TN, preferred_element_type=F32)
            g_tb = g_t.astype(BF16)
            dk_dec.append(jnp.dot(vb[:, vs], g_tb, preferred_element_type=F32))
            dv_ref[:, vs] = lax.dot_general(k_decb[:, ks], g_tb, NT, preferred_element_type=F32)
            prev = jnp.where(i < nc - 1, sp_ref[h], 0.0)
            ddecay.append(jnp.sum(g_t * prev, axis=0, keepdims=True))
            carry[h] = g_t * decay[:, ks]
        dk_dec = jnp.concatenate(dk_dec, axis=1)
        dk_ref[...] = dk_dec * w
        e = dk_dec * k_dec
        db_end = jnp.sum(e, axis=0, keepdims=True) + jnp.concatenate(ddecay, axis=1) * decay
        dg_ref[...] = _tri_dot(_tri(False), -e) + db_end

    kblk = pl.BlockSpec((CHUNK, nh * dk), lambda i: (nc - 1 - i, 0))
    vblk = pl.BlockSpec((CHUNK, nh * dv), lambda i: (nc - 1 - i, 0))
    ksds = jax.ShapeDtypeStruct((t, nh * dk), F32)
    return _chained(
        body, name=name, grid=(nc,),
        in_specs=[kblk, kblk, vblk, kblk, vblk,
                  pl.BlockSpec((nh, None, dv, dk), lambda i: (0, nc - 1 - i, 0, 0)),
                  pl.BlockSpec((nh, None, dv, dk), lambda i: (0, jnp.maximum(nc - 2 - i, 0), 0, 0))],
        out_specs=(kblk, kblk, vblk, kblk),
        out_shape=(ksds, ksds, jax.ShapeDtypeStruct((t, nh * dv), F32), ksds),
        scratch_shapes=[pltpu.VMEM((nh, dv, dk), F32)],
        compiler_params=_rowwise_params("arbitrary"),
    )(q, k, v, la, do, states, states)


def _adamw(name, w, m, v, parts):
    rows, cols = w.shape
    tm = 1 << int(math.log2(max(8, (1 << 18) // (-(-cols // LANES) * LANES))))
    while rows % tm and tm > 8:
        tm //= 2
    tm = _tile(rows, tm)

    def body(w_ref, m_ref, v_ref, p_ref, g_ref, d_ref, nm_ref, nv_ref):
        g = p_ref[0].astype(F32)
        for s in range(1, parts.shape[0]):
            g = g + p_ref[s].astype(F32)
        m_new = ADAM_B1 * m_ref[...] + (1.0 - ADAM_B1) * g
        v_new = ADAM_B2 * v_ref[...] + (1.0 - ADAM_B2) * jnp.square(g)
        m_hat = m_new / (1.0 - ADAM_B1 ** ADAM_STEP)
        v_hat = v_new / (1.0 - ADAM_B2 ** ADAM_STEP)
        g_ref[...] = g
        d_ref[...] = -ADAM_LR * (m_hat / (jnp.sqrt(v_hat) + ADAM_EPS) + ADAM_WD * w_ref[...])
        nm_ref[...] = m_new
        nv_ref[...] = v_new

    blk = pl.BlockSpec((tm, cols), lambda i: (i, 0))
    sds = jax.ShapeDtypeStruct((rows, cols), F32)
    return _chained(
        body, name=name, grid=(rows // tm,),
        in_specs=[blk, blk, blk, pl.BlockSpec((parts.shape[0], tm, cols), lambda i: (0, i, 0))],
        out_specs=(blk, blk, blk, blk), out_shape=(sds, sds, sds, sds),
        compiler_params=_rowwise_params("parallel"),
    )(w, m, v, parts)


HBM = pl.BlockSpec(memory_space=pl.ANY)
MESH = pl.DeviceIdType.MESH


def _all_gather(name, shards):
    n = len(shards)

    def body(*refs):
        ins, outs = refs[:n], refs[n:2 * n]
        send_sems, recv_sems, local_sems = refs[2 * n:]
        x, y, c = lax.axis_index("x"), lax.axis_index("y"), lax.axis_index("c")
        me, sibling = (x, y, c), (x, y, 1 - c)
        chips = [(1 - x, y), (x, 1 - y), (1 - x, 1 - y)]

        def copy(w, k, block, to, src=None):
            dst = outs[w].at[4 * block[0] + 2 * block[1] + block[2]]
            return pltpu.make_async_remote_copy(
                src_ref=dst if src is None else src, dst_ref=dst, send_sem=send_sems.at[7 * w + k],
                recv_sem=recv_sems.at[7 * w + k], device_id=to, device_id_type=MESH)

        mine, first, passed = [], [], []
        for w in range(n):
            cp = pltpu.make_async_copy(ins[w], outs[w].at[4 * x + 2 * y + c], local_sems.at[w])
            cp.start()
            mine.append(cp)
            first.append(copy(w, 0, me, sibling, src=ins[w]))
            first += [copy(w, 1 + j, me, (*chip, c), src=ins[w]) for j, chip in enumerate(chips)]
        for cp in first:
            cp.start()
        for w in range(n):
            for j, chip in enumerate(chips):
                copy(w, 1 + j, (*chip, c), me).wait_recv()
                cp = copy(w, 4 + j, (*chip, c), sibling)
                cp.start()
                passed.append(cp)
        for w in range(n):
            copy(w, 0, sibling, me).wait_recv()
            for j, chip in enumerate(chips):
                copy(w, 4 + j, (*chip, 1 - c), me).wait_recv()
        for cp in first + passed:
            cp.wait_send()
        for cp in mine:
            cp.wait()

    return _chained(
        body, name=name, in_specs=[HBM] * n, out_specs=[HBM] * n,
        out_shape=[jax.ShapeDtypeStruct((N_DEV,) + s.shape, s.dtype) for s in shards],
        scratch_shapes=[pltpu.SemaphoreType.DMA((7 * n,)), pltpu.SemaphoreType.DMA((7 * n,)),
                        pltpu.SemaphoreType.DMA((n,))],
    )(*shards)


HBM_SPEC = pl.BlockSpec(memory_space=pltpu.HBM)
SEM_SPEC = pl.BlockSpec(memory_space=pltpu.SEMAPHORE)
EFFECT = pltpu.SideEffectType.DATAFLOW_SIDE_EFFECTING
TOKEN = jax.ShapeDtypeStruct((8, LANES), F32)


def _in_hbm(a):
    return pltpu.with_memory_space_constraint(a, pltpu.HBM)


def _place():
    x, y, c = lax.axis_index("x"), lax.axis_index("y"), lax.axis_index("c")
    chips = [(1 - x, y), (x, 1 - y), (1 - x, 1 - y)]
    return x, y, c, chips


def _block_of(px, py, pc):
    return 4 * px + 2 * py + pc


def _gather_copies(ins, outs, sems, w):
    send_sems, recv_sems, local_sems = sems
    x, y, c, chips = _place()
    peers = [(x, y, 1 - c)] + [(*chip, c) for chip in chips]

    def copy(k, block, to, src):
        dst = outs[w].at[_block_of(*block)]
        return pltpu.make_async_remote_copy(src_ref=dst if src is None else src, dst_ref=dst, send_sem=send_sems.at[4 * w + k],
                                            recv_sem=recv_sems.at[4 * w + k], device_id=to, device_id_type=MESH)

    local = pltpu.make_async_copy(ins[w], outs[w].at[_block_of(x, y, c)], local_sems.at[w])
    sends = [copy(k, (x, y, c), peer, ins[w]) for k, peer in enumerate(peers)]
    recvs = [copy(k, peer, (x, y, c), None) for k, peer in enumerate(peers)]
    return local, sends, recvs


def _forward_copies(outs, sems, w):
    send_sems, recv_sems = sems
    x, y, c, chips = _place()

    def copy(j, block, to):
        dst = outs[w].at[_block_of(*block)]
        return pltpu.make_async_remote_copy(src_ref=dst, dst_ref=dst, send_sem=send_sems.at[3 * w + j],
                                            recv_sem=recv_sems.at[3 * w + j], device_id=to, device_id_type=MESH)

    sends = [copy(j, (*chip, c), (x, y, 1 - c)) for j, chip in enumerate(chips)]
    recvs = [copy(j, (*chip, 1 - c), (x, y, c)) for j, chip in enumerate(chips)]
    return sends, recvs


def _gather_start(name, shards):
    n = len(shards)

    def body(*refs):
        ins, outs, sems = refs[:n], refs[n:2 * n], refs[2 * n:2 * n + 3]
        for w in range(n):
            local, sends, _ = _gather_copies(ins, outs, sems, w)
            for cp in sends + [local]:
                cp.start()
        refs[-1][...] = jnp.zeros_like(refs[-1])

    lands = [lax.empty((N_DEV,) + s.shape, s.dtype) for s in shards]
    res = _chained(
        body, name=name, link=-1, in_specs=[HBM_SPEC] * (2 * n),
        out_specs=[SEM_SPEC] * 3 + [HBM_SPEC] * (2 * n) + [pl.BlockSpec(memory_space=pltpu.VMEM)],
        out_shape=[pltpu.SemaphoreType.DMA((4 * n,)), pltpu.SemaphoreType.DMA((4 * n,)), pltpu.SemaphoreType.DMA((n,))]
        + [pltpu.HBM(s.shape, s.dtype) for s in shards] + [pltpu.HBM(l.shape, l.dtype) for l in lands] + [TOKEN],
        input_output_aliases={i: 3 + i for i in range(2 * n)},
        compiler_params=pltpu.CompilerParams(has_side_effects=EFFECT),
    )(*[_in_hbm(s) for s in shards], *[_in_hbm(l) for l in lands])
    return n, res[:3], res[3:3 + n], res[3 + n:3 + 2 * n]


def _gather_mid(name, state):
    n, sems, shards, lands = state

    def body(*refs):
        ins, outs, sems_in = refs[:n], refs[n:2 * n], refs[2 * n:2 * n + 3]
        sems_out = refs[2 * n + 3:2 * n + 5]
        for w in range(n):
            local, sends, recvs = _gather_copies(ins, outs, sems_in, w)
            local.wait()
            for cp in sends:
                cp.wait_send()
            for cp in recvs:
                cp.wait_recv()
            for cp in _forward_copies(outs, sems_out, w)[0]:
                cp.start()
        refs[-1][...] = jnp.zeros_like(refs[-1])

    res = _chained(
        body, name=name, link=-1, in_specs=[HBM_SPEC] * (2 * n) + [SEM_SPEC] * 3,
        out_specs=[SEM_SPEC] * 2 + [HBM_SPEC] * n + [pl.BlockSpec(memory_space=pltpu.VMEM)],
        out_shape=[pltpu.SemaphoreType.DMA((3 * n,)), pltpu.SemaphoreType.DMA((3 * n,))]
        + [pltpu.HBM(l.shape, l.dtype) for l in lands] + [TOKEN],
        input_output_aliases={n + i: 2 + i for i in range(n)},
        compiler_params=pltpu.CompilerParams(has_side_effects=EFFECT),
    )(*shards, *lands, *sems)
    return n, res[:2], res[2:2 + n]


def _gather_end(name, state):
    n, sems, lands = state

    def body(*refs):
        outs, sems_in = refs[:n], refs[n:n + 2]
        for w in range(n):
            sends, recvs = _forward_copies(outs, sems_in, w)
            for cp in sends:
                cp.wait_send()
            for cp in recvs:
                cp.wait_recv()

    return _chained(
        body, name=name, in_specs=[HBM_SPEC] * n + [SEM_SPEC] * 2, out_specs=[HBM_SPEC] * n,
        out_shape=[pltpu.HBM(l.shape, l.dtype) for l in lands], input_output_aliases={i: i for i in range(n)},
        compiler_params=pltpu.CompilerParams(has_side_effects=EFFECT),
    )(*lands, *sems)


def _exchange_copies(ins, outs, sems, w):
    send_sems, recv_sems, local_sems = sems
    x, y, c, _ = _place()
    mine = _block_of(x, y, c)
    local = pltpu.make_async_copy(ins[w].at[mine], outs[w].at[mine], local_sems.at[w])
    remote = []
    for k in range(1, N_DEV):
        px, py, pc = x ^ (k >> 2), y ^ ((k >> 1) & 1), c ^ (k & 1)
        remote.append(pltpu.make_async_remote_copy(
            src_ref=ins[w].at[_block_of(px, py, pc)], dst_ref=outs[w].at[mine], send_sem=send_sems.at[7 * w + k - 1],
            recv_sem=recv_sems.at[7 * w + k - 1], device_id=(px, py, pc), device_id_type=MESH))
    return local, remote


def _exchange_start(name, stacks):
    n = len(stacks)

    def body(*refs):
        ins, outs, sems = refs[:n], refs[n:2 * n], refs[2 * n:2 * n + 3]
        for w in range(n):
            local, remote = _exchange_copies(ins, outs, sems, w)
            for cp in remote + [local]:
                cp.start()
        refs[-1][...] = jnp.zeros_like(refs[-1])

    lands = [lax.empty(s.shape, s.dtype) for s in stacks]
    if any(s is _Chain.last for s in stacks):
        _Chain.last = None
    res = _chained(
        body, name=name, link=-1, in_specs=[HBM_SPEC] * (2 * n),
        out_specs=[SEM_SPEC] * 3 + [HBM_SPEC] * (2 * n) + [pl.BlockSpec(memory_space=pltpu.VMEM)],
        out_shape=[pltpu.SemaphoreType.DMA((7 * n,)), pltpu.SemaphoreType.DMA((7 * n,)), pltpu.SemaphoreType.DMA((n,))]
        + [pltpu.HBM(s.shape, s.dtype) for s in stacks] * 2 + [TOKEN],
        input_output_aliases={i: 3 + i for i in range(2 * n)},
        compiler_params=pltpu.CompilerParams(has_side_effects=EFFECT),
    )(*[_in_hbm(s) for s in stacks], *[_in_hbm(l) for l in lands])
    return n, res[:3], res[3:3 + n], res[3 + n:3 + 2 * n]


def _exchange_wait(name, state):
    n, sems, stacks, lands = state

    def body(*refs):
        ins, outs, sems_in = refs[:n], refs[n:2 * n], refs[2 * n:2 * n + 3]
        for w in range(n):
            local, remote = _exchange_copies(ins, outs, sems_in, w)
            local.wait()
            for cp in remote:
                cp.wait_send()
                cp.wait_recv()

    return _chained(
        body, name=name, in_specs=[HBM_SPEC] * (2 * n) + [SEM_SPEC] * 3, out_specs=[HBM_SPEC] * n,
        out_shape=[pltpu.HBM(l.shape, l.dtype) for l in lands], input_output_aliases={n + i: i for i in range(n)},
        compiler_params=pltpu.CompilerParams(has_side_effects=EFFECT),
    )(*stacks, *lands, *sems)


N_CHIP = N_DEV // 2


def _pair_copies(stack, land, sems):
    send_sems, recv_sems, _ = sems
    x, y, c, _ = _place()
    remote = [pltpu.make_async_remote_copy(src_ref=stack.at[2 * k + 1 - c], dst_ref=land.at[k], send_sem=send_sems.at[k],
                                           recv_sem=recv_sems.at[k], device_id=(x, y, 1 - c), device_id_type=MESH)
              for k in range(N_CHIP)]
    return [], remote


def _chip_copies(pairs, land, sems):
    send_sems, recv_sems, local_sems = sems
    x, y, c, _ = _place()
    mine = 2 * x + y
    local = pltpu.make_async_copy(pairs.at[mine], land.at[mine], local_sems.at[0])
    remote = []
    for m in range(1, N_CHIP):
        px, py = x ^ (m >> 1), y ^ (m & 1)
        remote.append(pltpu.make_async_remote_copy(
            src_ref=pairs.at[2 * px + py], dst_ref=land.at[mine], send_sem=send_sems.at[m - 1],
            recv_sem=recv_sems.at[m - 1], device_id=(px, py, c), device_id_type=MESH))
    return local, remote


def _stage_start(name, copies_of, src_arr, n_land, n_sems):
    half = (N_CHIP,) + src_arr.shape[1:]

    def body(*refs):
        local, remote = copies_of(refs[0], *refs[1:1 + n_land], refs[1 + n_land:4 + n_land])
        for cp in remote + (local if isinstance(local, list) else [local]):
            cp.start()
        refs[-1][...] = jnp.zeros_like(refs[-1])

    if src_arr is _Chain.last:
        _Chain.last = None
    lands = [lax.empty(half, src_arr.dtype) for _ in range(n_land)]
    res = _chained(
        body, name=name, link=-1, in_specs=[HBM_SPEC] * (1 + n_land),
        out_specs=[SEM_SPEC] * 3 + [HBM_SPEC] * (1 + n_land) + [pl.BlockSpec(memory_space=pltpu.VMEM)],
        out_shape=[pltpu.SemaphoreType.DMA((n,)) for n in n_sems] + [pltpu.HBM(src_arr.shape, src_arr.dtype)]
        + [pltpu.HBM(half, src_arr.dtype)] * n_land + [TOKEN],
        input_output_aliases={i: 3 + i for i in range(1 + n_land)},
        compiler_params=pltpu.CompilerParams(has_side_effects=EFFECT),
    )(_in_hbm(src_arr), *[_in_hbm(l) for l in lands])
    return res[:3], res[3], res[4:4 + n_land]


def _stage_wait(name, copies_of, state):
    sems, src_arr, lands = state
    n_land = len(lands)

    def body(*refs):
        local, remote = copies_of(refs[0], *refs[1:1 + n_land], refs[1 + n_land:4 + n_land])
        for cp in (local if isinstance(local, list) else [local]):
            cp.wait()
        for cp in remote:
            cp.wait_send()
            cp.wait_recv()

    return _chained(
        body, name=name, in_specs=[HBM_SPEC] * (1 + n_land) + [SEM_SPEC] * 3, out_specs=[HBM_SPEC] * n_land,
        out_shape=[pltpu.HBM(l.shape, l.dtype) for l in lands],
        input_output_aliases={1 + i: i for i in range(n_land)},
        compiler_params=pltpu.CompilerParams(has_side_effects=EFFECT),
    )(src_arr, *lands, *sems)


def _pair_sum(name, stack, land, tm=64):
    n, rows, cols = land.shape
    tm = _tile(rows, tm)

    def body(c_ref, a_ref, b_ref, o_ref):
        o_ref[...] = (a_ref[...].astype(F32) + b_ref[...].astype(F32)).astype(o_ref.dtype)

    core = lax.axis_index("c").astype(jnp.int32).reshape(1)
    blk = pl.BlockSpec((n, tm, cols), lambda i, c_ref: (0, i, 0))
    mine = pl.BlockSpec((n, None, tm, cols), lambda i, c_ref: (0, c_ref[0], i, 0))
    _Chain.last = pl.pallas_call(
        body, name=name,
        grid_spec=pltpu.PrefetchScalarGridSpec(num_scalar_prefetch=1, grid=(rows // tm,), in_specs=[mine, blk], out_specs=blk),
        out_shape=jax.ShapeDtypeStruct(land.shape, land.dtype), compiler_params=_rowwise_params("parallel"),
    )(core, stack.reshape(n, 2, rows, cols), land)
    return _Chain.last


def _unstack_cols(w):
    return w.transpose(1, 0, 2).reshape(w.shape[1], N_DEV * w.shape[2])


def _stack_cols(w):
    return w.reshape(w.shape[0], N_DEV, w.shape[1] // N_DEV).transpose(1, 0, 2)


def _rope_tables(positions):
    half = MLA_ROPE // 2
    inv_freq = ROPE_THETA ** (-jnp.arange(half, dtype=F32) / half)
    ang = positions.astype(F32)[:, None] * inv_freq
    cos, sin = jnp.cos(ang), jnp.sin(ang)
    t = positions.shape[0]
    cos_t = jnp.concatenate([jnp.ones((t, MLA_NOPE), F32), cos, cos], axis=1)
    sin_t = jnp.concatenate([jnp.zeros((t, MLA_NOPE), F32), -sin, sin], axis=1)
    idx = jnp.arange(MLA_QK)
    partner = jnp.where(idx < MLA_NOPE, -1, jnp.where(idx < MLA_NOPE + half, idx + half, idx - half))
    swap = (idx[:, None] == partner[None, :]).astype(BF16)
    return cos_t, sin_t, swap


def _ffn_fwd(tag, x, gain, wt):
    h = _rms_fwd(tag + "_rms", x, gain, BF16)
    g = _mm_stack_nt_out(tag + "_gate", h, wt[tag + '_w_gate'], BF16)
    u, a = _mm_stack_nt_out(tag + "_up", h, wt[tag + '_w_up'], BF16,
                            fuse=(lambda u_blk, g_blk: (u_blk, g_blk * jax.nn.sigmoid(g_blk) * u_blk), [g]))
    y = _mm_stack_sum(tag + "_down", a, wt[tag + '_w_down'], F32, scale=0.5, res=x, tm=1024)
    return y, (x, h, g, u, a)


def _ffn_bwd_weights(tag, dy, saved, wd, comm):
    x, h, g, u, a = saved
    comm.grads({tag + '_w_down': _mm_stack_tn_right(tag + "_dwd", a, dy, BF16, scale=0.5)})

    def dact(da, g_blk, u_blk):
        sig = jax.nn.sigmoid(g_blk)
        return da * u_blk * sig * (1.0 + g_blk * (1.0 - sig)), da * g_blk * sig

    dg, du = _mm_stack_nt_out(tag + "_da", dy, wd, BF16, scale=0.5, fuse=(dact, [g, u]))
    comm.advance()
    comm.grads({tag + '_w_gate': _mm_stack_tn_right(tag + "_dwg", dg, h, BF16)})
    dwu = _mm_stack_tn_right(tag + "_dwu", du, h, BF16)
    comm.advance()
    comm.grads({tag + '_w_up': dwu})
    return dg, du


def _ffn_bwd_input(tag, dy, saved, dgu, gain, wg, wu, comm):
    dg, du = dgu
    dh = _mm_stack_sum(tag + "_dh", (dg, du), (wg, wu), F32)
    comm.advance()
    return _rms_bwd(tag + "_drms", saved[0], gain, dh, res=dy)


def _local_step(x, mem, positions, sm, comm, target):
    t, d = x.shape
    nm = mem.shape[0]
    gs, gw = {}, {}

    class Weights(dict):
        def __missing__(self, name):
            self.update(comm.weights(next(k for k, names in enumerate(GATHERS) if name in names)))
            return self[name]

    wt = Weights()

    x1, ffn1_saved = _ffn_fwd("ffn1", x, sm['ffn1_norm'], wt)

    h2 = _rms_fwd("mix_rms", x1, sm['mix_norm'], BF16)
    w_in_ref = _unstack_cols(wt['w_in'])
    pieces = []
    for name, n in IN_PAD:
        piece = w_in_ref[:, REF_OFF[name]:REF_OFF[name] + REF_SIZE[name]]
        if n != REF_SIZE[name]:
            piece = jnp.pad(piece, ((0, 0), (0, n - REF_SIZE[name])))
        pieces.append(piece)
    w_in = jnp.concatenate(pieces, axis=1)
    z = _mm2("mix_in", h2, w_in, NN, F32)
    zs = {name: z[:, PAD_OFF[name]:PAD_OFF[name] + n] for name, n in IN_PAD}

    cq = _rms_fwd("mla_q_a_rms", zs['zq'], sm['q_a_norm'], BF16)
    q_raw = _mm_stack_nt_out("mla_q_up", cq, wt['w_q_up'], F32)
    ckv = _rms_fwd("mla_kv_a_rms", zs['zkv'], sm['kv_a_norm'], BF16)
    kv = _mm_stack_out("mla_kv_up", ckv, wt['w_kv_up'], F32)
    zkr = zs['zkr'][:, :MLA_ROPE]
    k_raw = jnp.concatenate([kv[:, :, :MLA_NOPE], jnp.broadcast_to(zkr[None], (MLA_HEADS, t, MLA_ROPE))], axis=2)
    v_mla = kv[:, :, MLA_NOPE:].astype(BF16)
    cos_t, sin_t, swap = _rope_tables(positions)
    q_raw2, k_raw2 = q_raw.reshape(MLA_HEADS * t, MLA_QK), k_raw.reshape(MLA_HEADS * t, MLA_QK)
    rope = (cos_t, sin_t, swap)
    qf = _rms_fwd("mla_q_rms", q_raw2, sm['mla_q_norm'], BF16, rope=rope).reshape(MLA_HEADS, t, MLA_QK)
    kf = _rms_fwd("mla_k_rms", k_raw2, sm['mla_k_norm'], BF16, rope=rope).reshape(MLA_HEADS, t, MLA_QK)
    o_mla = _attn_fwd("mla_attn", qf, kf, v_mla, MLA_QK ** -0.5, True)

    w_g2 = jnp.pad(_unstack_cols(wt['gla_w_gate2']), ((0, LANES - GLA_GATE_RANK), (0, 0)))
    pre = _mm2("gla_gate_pre", zs['zg'], w_g2, NN, F32)
    log_a = _gate_fwd("gla_gate", pre, sm['gla_b_gate'])
    o_gla_raw, states = _gla_fwd("gla_scan", zs['gq'], zs['gk'], zs['gv'], log_a, GLA_HEADS)
    o_gla_n = _rms_fwd("gla_out_rms", o_gla_raw, sm['gla_out_norm'], F32)
    o_gla = _swiglu_fwd("gla_out_gate", zs['zr'], o_gla_n, BF16)

    cat = jnp.concatenate([o_mla, o_gla], axis=1)
    w_out = wt['w_out'].reshape(d, d)
    x2 = _mm2("mix_out", cat, w_out, NN, F32, res=x1)

    w_mq, w_mk, w_mv = (wt[n].reshape(d, MEM_HEADS * MEM_HEAD_DIM) for n in ('mem_w_q', 'mem_w_k', 'mem_w_v'))
    hq = _rms_fwd("mem_attn_rms", x2, sm['mem_attn_norm'], BF16)
    hm = _rms_fwd("mem_rms", mem, sm['mem_norm'], BF16)

    def heads_out(name, a, b, out_dtype):
        m, kk = a.shape
        tm = _tile(m, 512)
        return _mm(name, a, b, (m // tm, MEM_HEADS, 1), ((tm, kk), lambda i, h, k: (i, 0)),
                   ((kk, MEM_HEAD_DIM), lambda i, h, k: (0, h)), ((None, tm, MEM_HEAD_DIM), lambda i, h, k: (h, i, 0)),
                   (MEM_HEADS, m, MEM_HEAD_DIM), out_dtype, NN)

    mq_raw = heads_out("mem_q", hq, w_mq, F32)
    mk_raw = heads_out("mem_k", hm, w_mk, F32)
    mv = heads_out("mem_v", hm, w_mv, BF16)
    mq = _rms_fwd("mem_q_rms", mq_raw.reshape(MEM_HEADS * t, MEM_HEAD_DIM), sm['mem_q_norm'], BF16)
    mk = _rms_fwd("mem_k_rms", mk_raw.reshape(MEM_HEADS * nm, MEM_HEAD_DIM), sm['mem_k_norm'], BF16)
    mq, mk = mq.reshape(MEM_HEADS, t, MEM_HEAD_DIM), mk.reshape(MEM_HEADS, nm, MEM_HEAD_DIM)
    o_mem = _attn_fwd("mem_attn", mq, mk, mv, MEM_HEAD_DIM ** -0.5, False)
    w_mo = wt['mem_w_o']
    mo_cols = w_mo.shape[2]
    tm = _tile(t, 512)
    x3 = _mm("mem_out", o_mem, w_mo, (t // tm, N_DEV, 1), ((tm, o_mem.shape[1]), lambda i, j, k: (i, 0)),
             ((None, o_mem.shape[1], mo_cols), lambda i, j, k: (j, 0, 0)), ((tm, mo_cols), lambda i, j, k: (i, j)),
             (t, d), F32, NN, res=x2)

    y, ffn2_saved = _ffn_fwd("ffn2", x3, sm['ffn2_norm'], wt)
    dy, dy_narrow, loss_lanes = _loss("loss", y, target)

    dgu = _ffn_bwd_weights("ffn2", dy_narrow, ffn2_saved, wt['ffn2_w_down'], comm)
    dx3, gs['ffn2_norm'] = _ffn_bwd_input("ffn2", dy, ffn2_saved, dgu, sm['ffn2_norm'],
                                          wt['ffn2_w_gate'], wt['ffn2_w_up'], comm)

    do_mem = _mm("mem_do", dx3, w_mo, (t // tm, MEM_HEADS, N_DEV), ((tm, mo_cols), lambda i, h, k: (i, k)),
                 ((None, MEM_HEAD_DIM, mo_cols), lambda i, h, k: (k, h, 0)), ((tm, MEM_HEAD_DIM), lambda i, h, k: (i, h)),
                 (t, MEM_HEADS * MEM_HEAD_DIM), BF16, NT)
    tk = _tile(t, 512)
    gw['mem_w_o'] = _mm("mem_dwo", o_mem, dx3, (N_DEV, 1, t // tk), ((tk, o_mem.shape[1]), lambda j, i, k: (k, 0)),
                        ((tk, mo_cols), lambda j, i, k: (k, j)), ((None, o_mem.shape[1], mo_cols), lambda j, i, k: (j, 0, 0)),
                        w_mo.shape, BF16, TN)
    dmq, dmk, dmv = _attn_bwd("mem_dattn", mq, mk, mv, do_mem, MEM_HEAD_DIM ** -0.5, False)
    dmq_raw, gs['mem_q_norm'] = _rms_bwd("mem_q_drms", mq_raw.reshape(MEM_HEADS * t, MEM_HEAD_DIM), sm['mem_q_norm'],
                                         dmq.reshape(MEM_HEADS * t, MEM_HEAD_DIM))
    dmk_raw, gs['mem_k_norm'] = _rms_bwd("mem_k_drms", mk_raw.reshape(MEM_HEADS * nm, MEM_HEAD_DIM), sm['mem_k_norm'],
                                         dmk.reshape(MEM_HEADS * nm, MEM_HEAD_DIM))
    dmq_raw = dmq_raw.reshape(MEM_HEADS, t, MEM_HEAD_DIM)
    dmk_raw = dmk_raw.reshape(MEM_HEADS, nm, MEM_HEAD_DIM)

    def heads_in_nt(name, a, b, res=None):
        m, n = a.shape[1], b.shape[0]
        tm_, tn_ = _tile(m, 512), _tile(n, 1024)
        return _mm(name, a, b, (m // tm_, n // tn_, MEM_HEADS), ((None, tm_, MEM_HEAD_DIM), lambda i, j, k: (k, i, 0)),
                   ((tn_, MEM_HEAD_DIM), lambda i, j, k: (j, k)), ((tm_, tn_), lambda i, j, k: (i, j)), (m, n), F32, NT,
                   None, res)

    def heads_tn(name, a, b):
        m, kp = a.shape
        tm_, tk_ = _tile(kp, 1024), _tile(m, 512)
        return _mm(name, a, b, (kp // tm_, MEM_HEADS, m // tk_), ((tk_, tm_), lambda i, h, k: (k, i)),
                   ((None, tk_, MEM_HEAD_DIM), lambda i, h, k: (h, k, 0)), ((tm_, MEM_HEAD_DIM), lambda i, h, k: (i, h)),
                   (kp, MEM_HEADS * MEM_HEAD_DIM), BF16, TN)

    dhq = heads_in_nt("mem_dhq", dmq_raw, w_mq)
    gw['mem_w_q'] = heads_tn("mem_dwq", hq, dmq_raw).reshape(wt['mem_w_q'].shape)
    dhm = heads_in_nt("mem_dhm_k", dmk_raw, w_mk)
    dhm = heads_in_nt("mem_dhm_v", dmv, w_mv, res=dhm)
    gw['mem_w_k'] = heads_tn("mem_dwk", hm, dmk_raw).reshape(wt['mem_w_k'].shape)
    gw['mem_w_v'] = heads_tn("mem_dwv", hm, dmv).reshape(wt['mem_w_v'].shape)
    _, gs['mem_norm'] = _rms_bwd("mem_drms", mem, sm['mem_norm'], dhm)
    comm.grads({n: gw[n] for n in MEMORY})
    dx2, gs['mem_attn_norm'] = _rms_bwd("mem_attn_drms", x2, sm['mem_attn_norm'], dhq, res=dx3)

    dcat = _mm2("mix_dcat", dx2, w_out, NT, F32)
    comm.grads({'w_out': _mm2("mix_dwout", cat, dx2, TN, BF16, tm=1024, tn=2048, tk=512).reshape(wt['w_out'].shape)})
    do_mla, do_gla = dcat[:, :MLA_HEADS * MLA_V], dcat[:, MLA_HEADS * MLA_V:]

    dzr, dgn = _swiglu_bwd("gla_out_dgate", do_gla, zs['zr'], o_gla_n, F32)
    do_gla_raw, gs['gla_out_norm'] = _rms_bwd("gla_out_drms", o_gla_raw, sm['gla_out_norm'], dgn)
    dgq, dgk, dgv, dlog_a = _gla_bwd("gla_dscan", zs['gq'], zs['gk'], zs['gv'], log_a, states, do_gla_raw, GLA_HEADS)
    dpre, gs['gla_b_gate'] = _gate_bwd("gla_dgate", pre, sm['gla_b_gate'], dlog_a)
    dw_g2 = _mm2("gla_dwgate", zs['zg'], dpre, TN, BF16, tk=512)
    comm.grads({'gla_w_gate2': _stack_cols(dw_g2[:GLA_GATE_RANK])})
    dzg = _mm2("gla_dzg", dpre, w_g2, NT, F32)

    dqf, dkf, dv_mla = _attn_bwd("mla_dattn", qf, kf, v_mla, do_mla, MLA_QK ** -0.5, True)
    dq_raw, gs['mla_q_norm'] = _rms_bwd("mla_q_drms", q_raw2, sm['mla_q_norm'], dqf.reshape(MLA_HEADS * t, MLA_QK), rope=rope)
    dk_raw, gs['mla_k_norm'] = _rms_bwd("mla_k_drms", k_raw2, sm['mla_k_norm'], dkf.reshape(MLA_HEADS * t, MLA_QK), rope=rope)
    dq_raw = dq_raw.reshape(MLA_HEADS, t, MLA_QK)
    dk_raw = dk_raw.reshape(MLA_HEADS, t, MLA_QK)
    dkv = jnp.concatenate([dk_raw[:, :, :MLA_NOPE], dv_mla], axis=2)
    dzkr = jnp.sum(dk_raw[:, :, MLA_NOPE:], axis=0)
    comm.grads({'w_q_up': _mm_stack_tn_right("mla_dwq", dq_raw, cq, BF16),
                'w_kv_up': _mm_stack_tn_left("mla_dwkv", ckv, dkv, BF16)})
    dcq = _mm_stack_sum("mla_dcq", dq_raw, wt['w_q_up'], F32)
    dckv = _mm_stack_nt_sum("mla_dckv", dkv, wt['w_kv_up'], F32)
    dzq, gs['q_a_norm'] = _rms_bwd("mla_q_a_drms", zs['zq'], sm['q_a_norm'], dcq)
    dzkv, gs['kv_a_norm'] = _rms_bwd("mla_kv_a_drms", zs['zkv'], sm['kv_a_norm'], dckv)

    dzs = {'zq': dzq, 'zkv': dzkv, 'gq': dgq, 'gk': dgk, 'gv': dgv, 'zr': dzr,
           'zkr': jnp.pad(dzkr, ((0, 0), (0, LANES - MLA_ROPE))), 'zg': dzg}
    dz = jnp.concatenate([dzs[name].astype(BF16) for name, _ in IN_PAD], axis=1)
    dw_in = _mm2("mix_dwin", h2, dz, TN, BF16, tm=1024, tn=2048, tk=512)
    dw_in_ref = jnp.concatenate([dw_in[:, PAD_OFF[name]:PAD_OFF[name] + n] for name, n in IN_REF], axis=1)
    comm.grads({'w_in': _stack_cols(dw_in_ref)})
    dh2 = _mm2("mix_dh", dz, w_in, NT, F32)
    comm.advance()
    dx1, dx1_narrow, gs['mix_norm'] = _rms_bwd("mix_drms", x1, sm['mix_norm'], dh2, res=dx2, narrow=True)

    dgu = _ffn_bwd_weights("ffn1", dx1_narrow, ffn1_saved, wt['ffn1_w_down'], comm)
    grad_x, gs['ffn1_norm'] = _ffn_bwd_input("ffn1", dx1, ffn1_saved, dgu, sm['ffn1_norm'],
                                             wt['ffn1_w_gate'], wt['ffn1_w_up'], comm)
    return loss_lanes, grad_x, gs


def _pad_lanes(v):
    n = v.shape[1]
    return jnp.pad(v, ((0, 0), (0, -n % LANES)))


def _pack_small(vals):
    return jnp.concatenate([_pad_lanes(vals[n]) for n in SMALL], axis=1)


def _unpack_small(packed, like):
    out, off = {}, 0
    for n in SMALL:
        size = like[n].shape[1]
        out[n] = packed[:, off:off + size]
        off += size + (-size % LANES)
    return out


def kernel(x, mem, positions, ffn1_norm, ffn1_w_gate, ffn1_w_up, ffn1_w_down, mix_norm, w_in, q_a_norm, w_q_up, kv_a_norm, w_kv_up, mla_q_norm, mla_k_norm, gla_w_gate2, gla_b_gate, gla_out_norm, w_out, mem_attn_norm, mem_norm, mem_w_q, mem_w_k, mem_w_v, mem_w_o, mem_q_norm, mem_k_norm, ffn2_norm, ffn2_w_gate, ffn2_w_up, ffn2_w_down, loss_target, m_ffn1_norm, m_ffn1_w_gate, m_ffn1_w_up, m_ffn1_w_down, m_mix_norm, m_w_in, m_q_a_norm, m_w_q_up, m_kv_a_norm, m_w_kv_up, m_mla_q_norm, m_mla_k_norm, m_gla_w_gate2, m_gla_b_gate, m_gla_out_norm, m_w_out, m_mem_attn_norm, m_mem_norm, m_mem_w_q, m_mem_w_k, m_mem_w_v, m_mem_w_o, m_mem_q_norm, m_mem_k_norm, m_ffn2_norm, m_ffn2_w_gate, m_ffn2_w_up, m_ffn2_w_down, v_ffn1_norm, v_ffn1_w_gate, v_ffn1_w_up, v_ffn1_w_down, v_mix_norm, v_w_in, v_q_a_norm, v_w_q_up, v_kv_a_norm, v_w_kv_up, v_mla_q_norm, v_mla_k_norm, v_gla_w_gate2, v_gla_b_gate, v_gla_out_norm, v_w_out, v_mem_attn_norm, v_mem_norm, v_mem_w_q, v_mem_w_k, v_mem_w_v, v_mem_w_o, v_mem_q_norm, v_mem_k_norm, v_ffn2_norm, v_ffn2_w_gate, v_ffn2_w_up, v_ffn2_w_down):
    inp = dict(locals())
    x, mem, positions, target = inp['x'][0], inp['mem'][0], inp['positions'][0], inp['loss_target'][0]
    sm = {n: inp[n] for n in SMALL}
    out = {}

    def stored(key):
        name = key[2:] if key[:2] in ('m_', 'v_') else key
        return inp[key][0].T if name in TRANSPOSED else inp[key][0]

    def as_given(name, r):
        return r.T[None] if name in TRANSPOSED else r[None]

    class Comm:
        def __init__(self):
            self.gathers = {0: self.start(0)}
            self.forwards, self.exchanges, self.pairs = {}, [], []

        def start(self, k):
            return _gather_start(f"gather_start_{k}", [stored(n).astype(BF16) for n in GATHERS[k]])

        def forward(self, k):
            if k not in self.forwards:
                self.forwards[k] = _gather_mid(f"gather_mid_{k}", self.gathers[k])
                self.gathers.update({nxt: self.start(nxt) for nxt in NEXT_GATHERS.get(k, [])})

        def weights(self, k):
            self.forward(k)
            if k in EARLY_FORWARD:
                self.forward(EARLY_FORWARD[k])
            return dict(zip(GATHERS[k], _gather_end(f"gather_end_{k}", self.forwards[k])))

        def grads(self, stacks):
            names = list(stacks)
            if names[0] in TWO_STAGE:
                (n,) = names
                self.pairs.append((n, _stage_start("pair_start_" + n, _pair_copies, stacks[n], 1, (N_CHIP, N_CHIP, 1))))
            else:
                self.exchanges.append((names, _exchange_start("exchange_start_" + names[0], [stacks[n] for n in names])))

        def advance(self):
            for n, state in self.pairs:
                (land,) = _stage_wait("pair_wait_" + n, _pair_copies, state)
                pairs = _pair_sum("pair_sum_" + n, state[1], land)
                self.exchanges.append(([n], _stage_start("chip_start_" + n, _chip_copies, pairs, 1, (N_CHIP - 1, N_CHIP - 1, 1))))
            self.pairs = []

        def update(self, count):
            todo, self.exchanges = self.exchanges[:count], self.exchanges[count:]
            for names, state in todo:
                if names[0] in TWO_STAGE:
                    parts = _stage_wait("chip_wait_" + names[0], _chip_copies, state)
                else:
                    parts = _exchange_wait("exchange_wait_" + names[0], state)
                for n, p in zip(names, parts):
                    res = _adamw("adamw_" + n, stored(n), stored('m_' + n), stored('v_' + n), p)
                    for kind, r in zip(('grad_', 'delta_', 'new_m_', 'new_v_'), res):
                        out[kind + n] = as_given(n, r)

    _Chain.last = None
    comm = Comm()
    loss_lanes, grad_x, gs = _local_step(x, mem, positions, sm, comm, target)
    out['loss'] = lax.psum(jnp.sum(loss_lanes), ("x", "y", "c"))
    out['grad_x'] = grad_x[None]

    comm.update(len(comm.exchanges) - 3)
    small_parts = _all_gather("gather_small", [_pack_small(gs)])[0]
    res = _adamw("adamw_small", _pack_small(sm), _pack_small({n: inp['m_' + n] for n in SMALL}),
                 _pack_small({n: inp['v_' + n] for n in SMALL}), small_parts)
    for kind, r in zip(('grad_', 'delta_', 'new_m_', 'new_v_'), res):
        for n, val in _unpack_small(r, sm).items():
            out[kind + n] = val

    comm.update(3)

    names = ['loss', 'grad_x'] + [k + n for k in ('grad_', 'delta_', 'new_m_', 'new_v_') for n in WEIGHTS]
    return tuple(out[n] for n in names)
```

```python
import functools
import math

import jax
import jax.numpy as jnp
from jax import lax
from jax.experimental import pallas as pl
from jax.experimental.pallas import tpu as pltpu

F32 = jnp.float32
BF16 = jnp.bfloat16

N_DEV = 8
EPS = 1e-6
CHUNK = 64
MLA_HEADS, MLA_NOPE, MLA_ROPE, MLA_V = 8, 128, 64, 128
MLA_QK = MLA_NOPE + MLA_ROPE
MLA_Q_RANK, MLA_KV_RANK = 512, 256
ROPE_THETA = 10000.0
GLA_HEADS, GLA_DK, GLA_DV, GLA_GATE_RANK = 4, 128, 256, 16
GLA_TAU = 16.0
MEM_HEADS, MEM_HEAD_DIM = 4, 128
ADAM_LR, ADAM_B1, ADAM_B2, ADAM_EPS, ADAM_WD, ADAM_STEP = 0.001, 0.9, 0.999, 1e-08, 0.01, 10

V7X_VMEM_BYTES = 64 * 1024 * 1024
LANES = 128

NN = (((1,), (0,)), ((), ()))
NT = (((1,), (1,)), ((), ()))
TN = (((0,), (0,)), ((), ()))

WEIGHTS = ['ffn1_norm', 'ffn1_w_gate', 'ffn1_w_up', 'ffn1_w_down', 'mix_norm', 'w_in', 'q_a_norm', 'w_q_up',
           'kv_a_norm', 'w_kv_up', 'mla_q_norm', 'mla_k_norm', 'gla_w_gate2', 'gla_b_gate', 'gla_out_norm', 'w_out',
           'mem_attn_norm', 'mem_norm', 'mem_w_q', 'mem_w_k', 'mem_w_v', 'mem_w_o', 'mem_q_norm', 'mem_k_norm',
           'ffn2_norm', 'ffn2_w_gate', 'ffn2_w_up', 'ffn2_w_down']
SMALL = ['ffn1_norm', 'mix_norm', 'q_a_norm', 'kv_a_norm', 'mla_q_norm', 'mla_k_norm', 'gla_b_gate', 'gla_out_norm',
         'mem_attn_norm', 'mem_norm', 'mem_q_norm', 'mem_k_norm', 'ffn2_norm']
MIXER = ['w_in', 'w_q_up', 'w_kv_up', 'gla_w_gate2', 'w_out']
MEMORY = ['mem_w_q', 'mem_w_k', 'mem_w_v', 'mem_w_o']
TRANSPOSED = ['ffn1_w_gate', 'ffn1_w_up', 'ffn2_w_gate', 'ffn2_w_up', 'w_q_up']
TWO_STAGE = ['w_in', 'ffn1_w_down', 'ffn1_w_gate', 'ffn1_w_up']
GATHERS = [['ffn1_w_gate'], ['ffn1_w_up'], ['ffn1_w_down'], MIXER[:1], MIXER[1:], MEMORY,
           ['ffn2_w_gate', 'ffn2_w_up', 'ffn2_w_down']]
NEXT_GATHERS = {0: [1], 1: [2], 2: [3], 3: [4], 4: [5, 6]}
EARLY_FORWARD = {5: 6}

IN_REF = [('zq', 512), ('zkv', 256), ('zkr', 64), ('gq', 512), ('gk', 512), ('gv', 1024), ('zg', 16), ('zr', 1024)]
IN_PAD = [('zq', 512), ('zkv', 256), ('gq', 512), ('gk', 512), ('gv', 1024), ('zr', 1024), ('zkr', 128), ('zg', 128)]
IN_WIDTH = sum(n for _, n in IN_REF)
IN_PAD_WIDTH = sum(n for _, n in IN_PAD)


def _offsets(layout):
    out, off = {}, 0
    for name, n in layout:
        out[name] = off
        off += n
    return out


REF_OFF, PAD_OFF = _offsets(IN_REF), _offsets(IN_PAD)
REF_SIZE = dict(IN_REF)


def _tile(n, pref):
    return pref if n % pref == 0 else n


def _block_bytes(blk, dtype):
    dims = [d for d in blk if d is not None]
    if len(dims) >= 1:
        dims[-1] = -(-dims[-1] // LANES) * LANES
    return math.prod(dims) * jnp.dtype(dtype).itemsize


def _vmem_limit(pipelined_bytes, resident_bytes=0):
    need = 2 * pipelined_bytes + resident_bytes + (8 << 20)
    return int(min(max(need, 32 << 20), V7X_VMEM_BYTES - (6 << 20)))


class _Chain:
    last = None


def _chained(body, *, in_specs, link=0, **kwargs):
    def call(*operands):
        dep = _Chain.last
        if dep is not None and any(o is dep for o in operands):
            dep = None
        if dep is None:
            res = pl.pallas_call(body, in_specs=in_specs, **kwargs)(*operands)
        else:
            n = len(operands)

            def chained_body(*refs):
                body(*refs[:n], *refs[n + 1:])

            res = pl.pallas_call(chained_body, in_specs=list(in_specs) + [pl.BlockSpec(memory_space=pl.ANY)],
                                 **kwargs)(*operands, dep)
        _Chain.last = res[link] if isinstance(res, (list, tuple)) else res
        return res

    return call


def _row_tile(rows, width, elements=1 << 18):
    lanes = -(-width // LANES) * LANES
    return _tile(rows, max(256, 1 << int(math.log2(max(1, elements // lanes)))))


def _streamed(*arrays):
    return [pltpu.with_memory_space_constraint(a, pltpu.HBM) for a in arrays]


def _rowwise_params(*semantics):
    return pltpu.CompilerParams(dimension_semantics=semantics, vmem_limit_bytes=48 << 20)


def _mm(name, a, b, grid, a_spec, b_spec, o_spec, out_shape, out_dtype, dims, scale=None, res=None, fuse=None):
    nk = grid[2]
    o_blk, o_map = o_spec
    acc_shape = tuple(d for d in o_blk if d is not None)
    extras = [res] if res is not None else (list(fuse[1]) if fuse else [])
    n_out = 2 if fuse else 1
    a_list, b_list = (list(a), list(b)) if isinstance(a, (tuple, list)) else ([a], [b])
    a_specs = a_spec if isinstance(a_spec, list) else [a_spec] * len(a_list)
    b_specs = b_spec if isinstance(b_spec, list) else [b_spec] * len(b_list)
    n_in = 2 * len(a_list)

    def body(*refs):
        a_refs, b_refs = refs[:n_in // 2], refs[n_in // 2:n_in]
        extra_refs = refs[n_in:n_in + len(extras)]
        out_refs = refs[n_in + len(extras):n_in + len(extras) + n_out]
        rest = refs[n_in + len(extras) + n_out:]
        r_ref = extra_refs[0] if res is not None else None
        o_ref = out_refs[0]

        def product():
            return sum(lax.dot_general(a_ref[...].astype(BF16), b_ref[...].astype(BF16), dims, preferred_element_type=F32)
                       for a_ref, b_ref in zip(a_refs, b_refs))

        def finish(r):
            if scale is not None:
                r = r * scale
            if r_ref is not None:
                r = r + r_ref[...].astype(F32)
            if fuse:
                for ref, val in zip(out_refs, fuse[0](r, *[e[...].astype(F32) for e in extra_refs])):
                    ref[...] = val.astype(ref.dtype)
            else:
                o_ref[...] = r.astype(o_ref.dtype)

        if nk == 1:
            finish(product())
        else:
            acc = rest[0]
            k = pl.program_id(2)

            @pl.when(k == 0)
            def _():
                acc[...] = product()

            @pl.when(k > 0)
            def _():
                acc[...] += product()

            @pl.when(k == nk - 1)
            def _():
                finish(acc[...])

    in_specs = [pl.BlockSpec(*spec) for spec in a_specs + b_specs]
    operands = a_list + b_list
    piped = sum(_block_bytes(spec[0], v.dtype) for spec, v in zip(a_specs + b_specs, operands))
    piped += _block_bytes(o_blk, out_dtype)
    for extra in extras:
        in_specs.append(pl.BlockSpec(o_blk, o_map))
        operands.append(extra)
        piped += _block_bytes(o_blk, extra.dtype)
    piped += (n_out - 1) * _block_bytes(o_blk, out_dtype)
    scratch = [pltpu.VMEM(acc_shape, F32)] if nk > 1 else []
    out_spec, out_sds = pl.BlockSpec(o_blk, o_map), jax.ShapeDtypeStruct(out_shape, out_dtype)
    return _chained(
        body, name=name, grid=grid, in_specs=in_specs, out_specs=(out_spec,) * n_out if fuse else out_spec,
        out_shape=(out_sds,) * n_out if fuse else out_sds, scratch_shapes=scratch,
        compiler_params=pltpu.CompilerParams(
            dimension_semantics=("parallel", "parallel", "arbitrary"),
            vmem_limit_bytes=_vmem_limit(piped, 3 * _block_bytes(acc_shape, F32))),
    )(*operands)


def _mm2(name, a, b, dims, out_dtype, tm=512, tn=1024, tk=2048, scale=None, res=None):
    if dims is NN:
        (m, kk), n = a.shape, b.shape[1]
    elif dims is NT:
        (m, kk), n = a.shape, b.shape[0]
    else:
        (kk, m), n = a.shape, b.shape[1]
    tm, tn, tk = _tile(m, tm), _tile(n, tn), _tile(kk, tk)
    a_spec = ((tk, tm), lambda i, j, k: (k, i)) if dims is TN else ((tm, tk), lambda i, j, k: (i, k))
    b_spec = ((tn, tk), lambda i, j, k: (j, k)) if dims is NT else ((tk, tn), lambda i, j, k: (k, j))
    return _mm(name, a, b, (m // tm, n // tn, kk // tk), a_spec, b_spec, ((tm, tn), lambda i, j, k: (i, j)),
               (m, n), out_dtype, dims, scale, res)


def _mm_stack_out(name, a, b, out_dtype, tm=512, tk=2048):
    (m, kk), (nj, _, n) = a.shape, b.shape
    tm, tk = _tile(m, tm), _tile(kk, tk)
    return _mm(name, a, b, (nj, m // tm, kk // tk), ((tm, tk), lambda j, i, k: (i, k)),
               ((None, tk, n), lambda j, i, k: (j, k, 0)), ((None, tm, n), lambda j, i, k: (j, i, 0)),
               (nj, m, n), out_dtype, NN)


def _mm_stack_nt_out(name, a, b, out_dtype, scale=None, fuse=None, tm=1024, tk=2048):
    (m, kk), (nj, n, _) = a.shape, b.shape
    tm, tk = _tile(m, tm), _tile(kk, tk)
    return _mm(name, a, b, (nj, m // tm, kk // tk), ((tm, tk), lambda j, i, k: (i, k)),
               ((None, n, tk), lambda j, i, k: (j, 0, k)), ((None, tm, n), lambda j, i, k: (j, i, 0)),
               (nj, m, n), out_dtype, NT, scale, fuse=fuse)


def _mm_stack_sum(name, a, b, out_dtype, scale=None, res=None, tm=512, tn=1024):
    (nj, m, f), n = (a[0] if isinstance(a, tuple) else a).shape, (b[0] if isinstance(b, tuple) else b).shape[2]
    tm, tn = _tile(m, tm), _tile(n, tn)
    pairs = list(zip(a, b)) if isinstance(a, tuple) else [(a, b)]
    step = 2 if nj % 2 == 0 else 1
    a_list, b_list, a_spec, b_spec = [], [], [], []
    for a_k, b_k in pairs:
        for s in range(step):
            a_list.append(a_k)
            b_list.append(b_k)
            a_spec.append(((None, tm, f), lambda i, j, k, s=s: (step * k + s, i, 0)))
            b_spec.append(((None, f, tn), lambda i, j, k, s=s: (step * k + s, 0, j)))
    return _mm(name, a_list, b_list, (m // tm, n // tn, nj // step), a_spec, b_spec, ((tm, tn), lambda i, j, k: (i, j)),
               (m, n), out_dtype, NN, scale, res)


def _mm_stack_nt_sum(name, a, b, out_dtype, res=None, tm=512, tn=1024):
    (nj, m, f), n = a.shape, b.shape[1]
    tm, tn = _tile(m, tm), _tile(n, tn)
    return _mm(name, a, b, (m // tm, n // tn, nj), ((None, tm, f), lambda i, j, k: (k, i, 0)),
               ((None, tn, f), lambda i, j, k: (k, j, 0)), ((tm, tn), lambda i, j, k: (i, j)),
               (m, n), out_dtype, NT, None, res)


def _mm_stack_tn_left(name, a, b, out_dtype, tm=1024, tk=512):
    (m, kp), (nj, _, n) = a.shape, b.shape
    tm, tk = _tile(kp, tm), _tile(m, tk)
    return _mm(name, a, b, (nj, kp // tm, m // tk), ((tk, tm), lambda j, i, k: (k, i)),
               ((None, tk, n), lambda j, i, k: (j, k, 0)), ((None, tm, n), lambda j, i, k: (j, i, 0)),
               (nj, kp, n), out_dtype, TN)


def _mm_stack_tn_right(name, a, b, out_dtype, scale=None, tn=2048, tk=1024):
    (nj, m, f), n = a.shape, b.shape[1]
    tn, tk = _tile(n, tn), _tile(m, tk)
    return _mm(name, a, b, (nj, n // tn, m // tk), ((None, tk, f), lambda j, i, k: (j, k, 0)),
               ((tk, tn), lambda j, i, k: (k, i)), ((None, f, tn), lambda j, i, k: (j, 0, i)),
               (nj, f, n), out_dtype, TN, scale)


def _split3(x):
    hi = x.astype(BF16)
    r1 = x - hi.astype(F32)
    mid = r1.astype(BF16)
    lo = (r1 - mid.astype(F32)).astype(BF16)
    return hi, mid, lo


def _swap_halves(x, swap):
    return sum(jnp.dot(part, swap, preferred_element_type=F32) for part in _split3(x))


def _rope_specs(rope, tm):
    cos_t, _, swap = rope
    nt = cos_t.shape[0] // tm
    tab = pl.BlockSpec((tm, cos_t.shape[1]), lambda i, c: (i % nt, 0))
    return [tab, tab, pl.BlockSpec(swap.shape, lambda i, c: (0, 0))]


def _rms_fwd(name, x, g, out_dtype, rope=None):
    rows, cols = x.shape
    d = g.shape[1]
    tm = _row_tile(rows if rope is None else rope[0].shape[0], d)

    def body(x_ref, g_ref, *refs):
        xf = x_ref[...].astype(F32)
        r = lax.rsqrt(jnp.mean(xf * xf, axis=-1, keepdims=True) + EPS)
        y = xf * r * g_ref[...]
        if rope is not None:
            c_ref, s_ref, p_ref = refs[:3]
            y = y * c_ref[...] + _swap_halves(y, p_ref[...]) * s_ref[...]
        refs[-1][...] = y.astype(refs[-1].dtype)

    return _chained(
        body, name=name, grid=(rows // tm, cols // d),
        in_specs=[pl.BlockSpec((tm, d), lambda i, c: (i, c)), pl.BlockSpec((1, d), lambda i, c: (0, 0))]
        + (_rope_specs(rope, tm) if rope is not None else []),
        out_specs=pl.BlockSpec((tm, d), lambda i, c: (i, c)),
        out_shape=jax.ShapeDtypeStruct((rows, cols), out_dtype),
        compiler_params=_rowwise_params("parallel", "parallel"),
    )(*_streamed(x), g, *(rope or ()))


def _rms_bwd(name, x, g, dy, res=None, rope=None, narrow=False):
    rows, cols = x.shape
    d = g.shape[1]
    tm = _row_tile(rows if rope is None else rope[0].shape[0], d)

    def body(*refs):
        x_ref, g_ref, dy_ref = refs[:3]
        n_out = 3 if narrow else 2
        dx_ref, dg_ref = refs[-n_out], refs[-1]
        r_ref = refs[3] if res is not None else None
        xf = x_ref[...].astype(F32)
        r = lax.rsqrt(jnp.mean(xf * xf, axis=-1, keepdims=True) + EPS)
        xhat = xf * r
        dyf = dy_ref[...].astype(F32)
        if rope is not None:
            c_ref, s_ref, p_ref = refs[-n_out - 3:-n_out]
            dyf = dyf * c_ref[...] + _swap_halves(dyf * s_ref[...], p_ref[...])

        @pl.when((pl.program_id(0) == 0) & (pl.program_id(1) == 0))
        def _():
            dg_ref[...] = jnp.zeros_like(dg_ref)

        dg_ref[...] += jnp.sum(dyf * xhat, axis=0, keepdims=True)
        dxh = dyf * g_ref[...]
        dx = r * (dxh - xhat * jnp.mean(dxh * xhat, axis=-1, keepdims=True))
        if r_ref is not None:
            dx = dx + r_ref[...].astype(F32)
        dx_ref[...] = dx
        if narrow:
            refs[-2][...] = dx.astype(BF16)

    blk = pl.BlockSpec((tm, d), lambda i, c: (i, c))
    in_specs = [blk, pl.BlockSpec((1, d), lambda i, c: (0, 0)), blk]
    operands = [x, g, dy]
    if res is not None:
        in_specs.append(blk)
        operands.append(res)
    if rope is not None:
        in_specs += _rope_specs(rope, tm)
        operands += list(rope)
    wide = [(blk, jax.ShapeDtypeStruct((rows, cols), F32))] + ([(blk, jax.ShapeDtypeStruct((rows, cols), BF16))] if narrow else [])
    outs = wide + [(pl.BlockSpec((1, d), lambda i, c: (0, 0)), jax.ShapeDtypeStruct((1, d), F32))]
    return _chained(
        body, name=name, grid=(rows // tm, cols // d), in_specs=in_specs,
        out_specs=tuple(spec for spec, _ in outs), out_shape=tuple(sds for _, sds in outs),
        compiler_params=_rowwise_params("arbitrary", "arbitrary"),
    )(*_streamed(x), g, *_streamed(*operands[2:4 if res is not None else 3]), *operands[4 if res is not None else 3:])


def _swiglu_fwd(name, g, u, out_dtype, tm=256):
    rows, cols = g.shape
    tm = _tile(rows, tm)

    def body(g_ref, u_ref, o_ref):
        gf = g_ref[...].astype(F32)
        o_ref[...] = (gf * jax.nn.sigmoid(gf) * u_ref[...].astype(F32)).astype(o_ref.dtype)

    blk = pl.BlockSpec((tm, cols), lambda i: (i, 0))
    return _chained(
        body, name=name, grid=(rows // tm,), in_specs=[blk, blk], out_specs=blk,
        out_shape=jax.ShapeDtypeStruct((rows, cols), out_dtype),
        compiler_params=_rowwise_params("parallel"),
    )(*_streamed(g, u))


def _swiglu_bwd(name, da, g, u, out_dtype, tm=256):
    rows, cols = g.shape
    tm = _tile(rows, tm)

    def body(da_ref, g_ref, u_ref, dg_ref, du_ref):
        gf = g_ref[...].astype(F32)
        daf = da_ref[...].astype(F32)
        sig = jax.nn.sigmoid(gf)
        du_ref[...] = (daf * gf * sig).astype(du_ref.dtype)
        dg_ref[...] = (daf * u_ref[...].astype(F32) * sig * (1.0 + gf * (1.0 - sig))).astype(dg_ref.dtype)

    blk = pl.BlockSpec((tm, cols), lambda i: (i, 0))
    sds = jax.ShapeDtypeStruct((rows, cols), out_dtype)
    return _chained(
        body, name=name, grid=(rows // tm,), in_specs=[blk, blk, blk], out_specs=(blk, blk), out_shape=(sds, sds),
        compiler_params=_rowwise_params("parallel"),
    )(*_streamed(da, g, u))


def _gate_fwd(name, pre, bias, tm=256):
    rows, cols = pre.shape
    tm = _tile(rows, tm)

    def body(p_ref, b_ref, o_ref):
        z = p_ref[...] + b_ref[...]
        o_ref[...] = (jnp.minimum(z, 0.0) - jnp.log(1.0 + jnp.exp(-jnp.abs(z)))) * (1.0 / GLA_TAU)

    blk = pl.BlockSpec((tm, cols), lambda i: (i, 0))
    return _chained(
        body, name=name, grid=(rows // tm,), in_specs=[blk, pl.BlockSpec((1, cols), lambda i: (0, 0))], out_specs=blk,
        out_shape=jax.ShapeDtypeStruct((rows, cols), F32),
        compiler_params=_rowwise_params("parallel"),
    )(pre, bias)


def _gate_bwd(name, pre, bias, dla, tm=256):
    rows, cols = pre.shape
    tm = _tile(rows, tm)

    def body(p_ref, b_ref, d_ref, dp_ref, db_ref):
        z = p_ref[...] + b_ref[...]
        dp = d_ref[...] * (1.0 / GLA_TAU) / (1.0 + jnp.exp(z))
        dp_ref[...] = dp

        @pl.when(pl.program_id(0) == 0)
        def _():
            db_ref[...] = jnp.zeros_like(db_ref)

        db_ref[...] += jnp.sum(dp, axis=0, keepdims=True)

    blk = pl.BlockSpec((tm, cols), lambda i: (i, 0))
    row = pl.BlockSpec((1, cols), lambda i: (0, 0))
    return _chained(
        body, name=name, grid=(rows // tm,), in_specs=[blk, row, blk], out_specs=(blk, row),
        out_shape=(jax.ShapeDtypeStruct((rows, cols), F32), jax.ShapeDtypeStruct((1, cols), F32)),
        compiler_params=_rowwise_params("arbitrary"),
    )(*_streamed(pre), bias, *_streamed(dla))


def _loss(name, y, target, tm=256):
    rows, d = y.shape
    tm = _tile(rows, tm)

    def body(y_ref, t_ref, dy_ref, dyb_ref, l_ref):
        err = y_ref[...] - t_ref[...]
        dy_ref[...] = err * (1.0 / d)
        dyb_ref[...] = (err * (1.0 / d)).astype(BF16)

        @pl.when(pl.program_id(0) == 0)
        def _():
            l_ref[...] = jnp.zeros_like(l_ref)

        sq = (err * err).reshape(tm // 8, 8, d)
        l_ref[...] += jnp.sum(sq, axis=0) * (0.5 / d)

    blk = pl.BlockSpec((tm, d), lambda i: (i, 0))
    return _chained(
        body, name=name, grid=(rows // tm,), in_specs=[blk, blk],
        out_specs=(blk, blk, pl.BlockSpec((8, d), lambda i: (0, 0))),
        out_shape=(jax.ShapeDtypeStruct((rows, d), F32), jax.ShapeDtypeStruct((rows, d), BF16),
                   jax.ShapeDtypeStruct((8, d), F32)),
        compiler_params=_rowwise_params("arbitrary"),
    )(*_streamed(y, target))


def _per_query_block(causal, nq, tq, tk, inner):
    if not causal:
        inner(0, tk)
        return
    for block in range(nq):
        pl.when(pl.program_id(1) == block)(functools.partial(inner, block * tq, (block + 1) * tq))


def _scores(q, k, scale, causal, q0):
    s = lax.dot_general(q, k, NT, preferred_element_type=F32) * scale
    if causal:
        qc = (q0 + lax.broadcasted_iota(jnp.int32, s.shape, 0)) // CHUNK
        kc = lax.broadcasted_iota(jnp.int32, s.shape, 1) // CHUNK
        s = jnp.where(kc <= qc, s, -1e30)
    e = jnp.exp(s - jnp.max(s, axis=-1, keepdims=True))
    return e, jnp.sum(e, axis=-1, keepdims=True)


def _attn_fwd(name, q, k, v, scale, causal, tq=256):
    nh, t, dk = q.shape
    tk, dv = k.shape[1], v.shape[2]
    tq = _tile(t, tq)

    def body(q_ref, k_ref, v_ref, o_ref):
        def inner(q0, kv):
            e, l = _scores(q_ref[...], k_ref[0:kv, :], scale, causal, q0)
            o = jnp.dot(e.astype(BF16), v_ref[0:kv, :], preferred_element_type=F32)
            o_ref[...] = (o / l).astype(o_ref.dtype)

        _per_query_block(causal, t // tq, tq, tk, inner)

    return _chained(
        body, name=name, grid=(nh, t // tq),
        in_specs=[pl.BlockSpec((None, tq, dk), lambda h, i: (h, i, 0)), pl.BlockSpec((None, tk, dk), lambda h, i: (h, 0, 0)),
                  pl.BlockSpec((None, tk, dv), lambda h, i: (h, 0, 0))],
        out_specs=pl.BlockSpec((tq, dv), lambda h, i: (i, h)),
        out_shape=jax.ShapeDtypeStruct((t, nh * dv), BF16),
        compiler_params=pltpu.CompilerParams(dimension_semantics=("parallel", "parallel"),
                                             vmem_limit_bytes=_vmem_limit(0, 6 * tq * tk * 4)),
    )(q, k, v)


def _attn_bwd(name, q, k, v, do, scale, causal, tq=256):
    nh, t, dk = q.shape
    tk, dv = k.shape[1], v.shape[2]
    tq = _tile(t, tq)

    def body(q_ref, k_ref, v_ref, do_ref, dq_ref, dk_ref, dv_ref):
        @pl.when(pl.program_id(1) == 0)
        def _():
            dk_ref[...] = jnp.zeros_like(dk_ref)
            dv_ref[...] = jnp.zeros_like(dv_ref)

        def inner(q0, kv):
            qb, kb = q_ref[...], k_ref[0:kv, :]
            e, l = _scores(qb, kb, scale, causal, q0)
            p = e / l
            dob = do_ref[...].astype(BF16)
            dp = lax.dot_general(dob, v_ref[0:kv, :], NT, preferred_element_type=F32)
            ds = (p * (dp - jnp.sum(p * dp, axis=-1, keepdims=True)) * scale).astype(BF16)
            dq_ref[...] = jnp.dot(ds, kb, preferred_element_type=F32)
            dk_ref[0:kv, :] += lax.dot_general(ds, qb, TN, preferred_element_type=F32)
            dv_ref[0:kv, :] += lax.dot_general(p.astype(BF16), dob, TN, preferred_element_type=F32)

        _per_query_block(causal, t // tq, tq, tk, inner)

    return _chained(
        body, name=name, grid=(nh, t // tq),
        in_specs=[pl.BlockSpec((None, tq, dk), lambda h, i: (h, i, 0)), pl.BlockSpec((None, tk, dk), lambda h, i: (h, 0, 0)),
                  pl.BlockSpec((None, tk, dv), lambda h, i: (h, 0, 0)), pl.BlockSpec((tq, dv), lambda h, i: (i, h))],
        out_specs=(pl.BlockSpec((None, tq, dk), lambda h, i: (h, i, 0)), pl.BlockSpec((None, tk, dk), lambda h, i: (h, 0, 0)),
                   pl.BlockSpec((None, tk, dv), lambda h, i: (h, 0, 0))),
        out_shape=(jax.ShapeDtypeStruct((nh, t, dk), F32), jax.ShapeDtypeStruct((nh, tk, dk), F32),
                   jax.ShapeDtypeStruct((nh, tk, dv), F32)),
        compiler_params=pltpu.CompilerParams(dimension_semantics=("parallel", "arbitrary"),
                                             vmem_limit_bytes=_vmem_limit(0, 10 * tq * tk * 4)),
    )(q, k, v, do)


def _tri(lower):
    r = lax.broadcasted_iota(jnp.int32, (CHUNK, CHUNK), 0)
    c = lax.broadcasted_iota(jnp.int32, (CHUNK, CHUNK), 1)
    return jnp.where((c <= r) if lower else (c >= r), 1.0, 0.0).astype(BF16)


def _tri_dot(tri, x):
    return sum(jnp.dot(tri, part, preferred_element_type=F32) for part in _split3(x))


def _gla_fwd(name, q, k, v, la, nh):
    t = q.shape[0]
    dk, dv = q.shape[1] // nh, v.shape[1] // nh
    nc = t // CHUNK

    def body(q_ref, k_ref, v_ref, g_ref, o_ref, st_ref, state):
        @pl.when(pl.program_id(0) == 0)
        def _():
            state[...] = jnp.zeros_like(state)

        g = g_ref[...]
        b = _tri_dot(_tri(True), g)
        b_end = jnp.sum(g, axis=0, keepdims=True)
        k_dec = (k_ref[...] * jnp.exp(b_end - b)).astype(BF16)
        decay = jnp.exp(b_end)
        qc = (q_ref[...] * (dk ** -0.5)).astype(BF16)
        vb = v_ref[...].astype(BF16)
        for h in range(nh):
            ks, vs = slice(h * dk, (h + 1) * dk), slice(h * dv, (h + 1) * dv)
            u_t = lax.dot_general(vb[:, vs], k_dec[:, ks], TN, preferred_element_type=F32)
            new = state[h] * decay[:, ks] + u_t
            state[h] = new
            st_ref[h] = new
            o_ref[:, vs] = lax.dot_general(qc[:, ks], new.astype(BF16), NT, preferred_element_type=F32)

    kblk = pl.BlockSpec((CHUNK, nh * dk), lambda n: (n, 0))
    vblk = pl.BlockSpec((CHUNK, nh * dv), lambda n: (n, 0))
    return _chained(
        body, name=name, grid=(nc,), in_specs=[kblk, kblk, vblk, kblk],
        out_specs=(vblk, pl.BlockSpec((nh, None, dv, dk), lambda n: (0, n, 0, 0))),
        out_shape=(jax.ShapeDtypeStruct((t, nh * dv), F32), jax.ShapeDtypeStruct((nh, nc, dv, dk), F32)),
        scratch_shapes=[pltpu.VMEM((nh, dv, dk), F32)],
        compiler_params=_rowwise_params("arbitrary"),
    )(q, k, v, la)


def _gla_bwd(name, q, k, v, la, states, do, nh):
    t = q.shape[0]
    dk, dv = q.shape[1] // nh, v.shape[1] // nh
    nc = t // CHUNK
    scale = dk ** -0.5

    def body(q_ref, k_ref, v_ref, g_ref, do_ref, st_ref, sp_ref, dq_ref, dk_ref, dv_ref, dg_ref, carry):
        i = pl.program_id(0)

        @pl.when(i == 0)
        def _():
            carry[...] = jnp.zeros_like(carry)

        g = g_ref[...]
        b = _tri_dot(_tri(True), g)
        b_end = jnp.sum(g, axis=0, keepdims=True)
        w = jnp.exp(b_end - b)
        decay = jnp.exp(b_end)
        k_dec = k_ref[...] * w
        k_decb = k_dec.astype(BF16)
        qc = (q_ref[...] * scale).astype(BF16)
        dob = do_ref[...].astype(BF16)
        vb = v_ref[...].astype(BF16)
        dk_dec, ddecay = [], []
        for h in range(nh):
            ks, vs = slice(h * dk, (h + 1) * dk), slice(h * dv, (h + 1) * dv)
            dq_ref[:, ks] = jnp.dot(dob[:, vs], st_ref[h].astype(BF16), preferred_element_type=F32) * scale
            g_t = carry[h] + lax.dot_general(dob[:, vs], qc[:, ks], TN, preferred_element_type=F32)
            g_tb = g_t.astype(BF16)
            dk_dec.append(jnp.dot(vb[:, vs], g_tb, preferred_element_type=F32))
            dv_ref[:, vs] = lax.dot_general(k_decb[:, ks], g_tb, NT, preferred_element_type=F32)
            prev = jnp.where(i < nc - 1, sp_ref[h], 0.0)
            ddecay.append(jnp.sum(g_t * prev, axis=0, keepdims=True))
            carry[h] = g_t * decay[:, ks]
        dk_dec = jnp.concatenate(dk_dec, axis=1)
        dk_ref[...] = dk_dec * w
        e = dk_dec * k_dec
        db_end = jnp.sum(e, axis=0, keepdims=True) + jnp.concatenate(ddecay, axis=1) * decay
        dg_ref[...] = _tri_dot(_tri(False), -e) + db_end

    kblk = pl.BlockSpec((CHUNK, nh * dk), lambda i: (nc - 1 - i, 0))
    vblk = pl.BlockSpec((CHUNK, nh * dv), lambda i: (nc - 1 - i, 0))
    ksds = jax.ShapeDtypeStruct((t, nh * dk), F32)
    return _chained(
        body, name=name, grid=(nc,),
        in_specs=[kblk, kblk, vblk, kblk, vblk,
                  pl.BlockSpec((nh, None, dv, dk), lambda i: (0, nc - 1 - i, 0, 0)),
                  pl.BlockSpec((nh, None, dv, dk), lambda i: (0, jnp.maximum(nc - 2 - i, 0), 0, 0))],
        out_specs=(kblk, kblk, vblk, kblk),
        out_shape=(ksds, ksds, jax.ShapeDtypeStruct((t, nh * dv), F32), ksds),
        scratch_shapes=[pltpu.VMEM((nh, dv, dk), F32)],
        compiler_params=_rowwise_params("arbitrary"),
    )(q, k, v, la, do, states, states)


def _adamw(name, w, m, v, parts):
    rows, cols = w.shape
    tm = 1 << int(math.log2(max(8, (1 << 18) // (-(-cols // LANES) * LANES))))
    while rows % tm and tm > 8:
        tm //= 2
    tm = _tile(rows, tm)

    def body(w_ref, m_ref, v_ref, p_ref, g_ref, d_ref, nm_ref, nv_ref):
        g = p_ref[0].astype(F32)
        for s in range(1, parts.shape[0]):
            g = g + p_ref[s].astype(F32)
        m_new = ADAM_B1 * m_ref[...] + (1.0 - ADAM_B1) * g
        v_new = ADAM_B2 * v_ref[...] + (1.0 - ADAM_B2) * jnp.square(g)
        m_hat = m_new / (1.0 - ADAM_B1 ** ADAM_STEP)
        v_hat = v_new / (1.0 - ADAM_B2 ** ADAM_STEP)
        g_ref[...] = g
        d_ref[...] = -ADAM_LR * (m_hat / (jnp.sqrt(v_hat) + ADAM_EPS) + ADAM_WD * w_ref[...])
        nm_ref[...] = m_new
        nv_ref[...] = v_new

    blk = pl.BlockSpec((tm, cols), lambda i: (i, 0))
    sds = jax.ShapeDtypeStruct((rows, cols), F32)
    return _chained(
        body, name=name, grid=(rows // tm,),
        in_specs=[blk, blk, blk, pl.BlockSpec((parts.shape[0], tm, cols), lambda i: (0, i, 0))],
        out_specs=(blk, blk, blk, blk), out_shape=(sds, sds, sds, sds),
        compiler_params=_rowwise_params("parallel"),
    )(*_streamed(w, m, v, parts))


HBM = pl.BlockSpec(memory_space=pl.ANY)
MESH = pl.DeviceIdType.MESH


def _all_gather(name, shards):
    n = len(shards)

    def body(*refs):
        ins, outs = refs[:n], refs[n:2 * n]
        send_sems, recv_sems, local_sems = refs[2 * n:]
        x, y, c = lax.axis_index("x"), lax.axis_index("y"), lax.axis_index("c")
        me, sibling = (x, y, c), (x, y, 1 - c)
        chips = [(1 - x, y), (x, 1 - y), (1 - x, 1 - y)]

        def copy(w, k, block, to, src=None):
            dst = outs[w].at[4 * block[0] + 2 * block[1] + block[2]]
            return pltpu.make_async_remote_copy(
                src_ref=dst if src is None else src, dst_ref=dst, send_sem=send_sems.at[7 * w + k],
                recv_sem=recv_sems.at[7 * w + k], device_id=to, device_id_type=MESH)

        mine, first, passed = [], [], []
        for w in range(n):
            cp = pltpu.make_async_copy(ins[w], outs[w].at[4 * x + 2 * y + c], local_sems.at[w])
            cp.start()
            mine.append(cp)
            first.append(copy(w, 0, me, sibling, src=ins[w]))
            first += [copy(w, 1 + j, me, (*chip, c), src=ins[w]) for j, chip in enumerate(chips)]
        for cp in first:
            cp.start()
        for w in range(n):
            for j, chip in enumerate(chips):
                copy(w, 1 + j, (*chip, c), me).wait_recv()
                cp = copy(w, 4 + j, (*chip, c), sibling)
                cp.start()
                passed.append(cp)
        for w in range(n):
            copy(w, 0, sibling, me).wait_recv()
            for j, chip in enumerate(chips):
                copy(w, 4 + j, (*chip, 1 - c), me).wait_recv()
        for cp in first + passed:
            cp.wait_send()
        for cp in mine:
            cp.wait()

    return _chained(
        body, name=name, in_specs=[HBM] * n, out_specs=[HBM] * n,
        out_shape=[jax.ShapeDtypeStruct((N_DEV,) + s.shape, s.dtype) for s in shards],
        scratch_shapes=[pltpu.SemaphoreType.DMA((7 * n,)), pltpu.SemaphoreType.DMA((7 * n,)),
                        pltpu.SemaphoreType.DMA((n,))],
    )(*shards)


HBM_SPEC = pl.BlockSpec(memory_space=pltpu.HBM)
SEM_SPEC = pl.BlockSpec(memory_space=pltpu.SEMAPHORE)
EFFECT = pltpu.SideEffectType.DATAFLOW_SIDE_EFFECTING
TOKEN = jax.ShapeDtypeStruct((8, LANES), F32)


def _in_hbm(a):
    return pltpu.with_memory_space_constraint(a, pltpu.HBM)


def _place():
    x, y, c = lax.axis_index("x"), lax.axis_index("y"), lax.axis_index("c")
    chips = [(1 - x, y), (x, 1 - y), (1 - x, 1 - y)]
    return x, y, c, chips


def _block_of(px, py, pc):
    return 4 * px + 2 * py + pc


def _gather_copies(ins, outs, sems, w):
    send_sems, recv_sems, local_sems = sems
    x, y, c, chips = _place()
    peers = [(x, y, 1 - c)] + [(*chip, c) for chip in chips]

    def copy(k, block, to, src):
        dst = outs[w].at[_block_of(*block)]
        return pltpu.make_async_remote_copy(src_ref=dst if src is None else src, dst_ref=dst, send_sem=send_sems.at[4 * w + k],
                                            recv_sem=recv_sems.at[4 * w + k], device_id=to, device_id_type=MESH)

    local = pltpu.make_async_copy(ins[w], outs[w].at[_block_of(x, y, c)], local_sems.at[w])
    sends = [copy(k, (x, y, c), peer, ins[w]) for k, peer in enumerate(peers)]
    recvs = [copy(k, peer, (x, y, c), None) for k, peer in enumerate(peers)]
    return local, sends, recvs


def _forward_copies(outs, sems, w):
    send_sems, recv_sems = sems
    x, y, c, chips = _place()

    def copy(j, block, to):
        dst = outs[w].at[_block_of(*block)]
        return pltpu.make_async_remote_copy(src_ref=dst, dst_ref=dst, send_sem=send_sems.at[3 * w + j],
                                            recv_sem=recv_sems.at[3 * w + j], device_id=to, device_id_type=MESH)

    sends = [copy(j, (*chip, c), (x, y, 1 - c)) for j, chip in enumerate(chips)]
    recvs = [copy(j, (*chip, 1 - c), (x, y, c)) for j, chip in enumerate(chips)]
    return sends, recvs


def _gather_start(name, shards):
    n = len(shards)

    def body(*refs):
        ins, outs, sems = refs[:n], refs[n:2 * n], refs[2 * n:2 * n + 3]
        for w in range(n):
            local, sends, _ = _gather_copies(ins, outs, sems, w)
            for cp in sends + [local]:
                cp.start()
        refs[-1][...] = jnp.zeros_like(refs[-1])

    lands = [lax.empty((N_DEV,) + s.shape, s.dtype) for s in shards]
    res = _chained(
        body, name=name, link=-1, in_specs=[HBM_SPEC] * (2 * n),
        out_specs=[SEM_SPEC] * 3 + [HBM_SPEC] * (2 * n) + [pl.BlockSpec(memory_space=pltpu.VMEM)],
        out_shape=[pltpu.SemaphoreType.DMA((4 * n,)), pltpu.SemaphoreType.DMA((4 * n,)), pltpu.SemaphoreType.DMA((n,))]
        + [pltpu.HBM(s.shape, s.dtype) for s in shards] + [pltpu.HBM(l.shape, l.dtype) for l in lands] + [TOKEN],
        input_output_aliases={i: 3 + i for i in range(2 * n)},
        compiler_params=pltpu.CompilerParams(has_side_effects=EFFECT),
    )(*[_in_hbm(s) for s in shards], *[_in_hbm(l) for l in lands])
    return n, res[:3], res[3:3 + n], res[3 + n:3 + 2 * n]


def _gather_mid(name, state):
    n, sems, shards, lands = state

    def body(*refs):
        ins, outs, sems_in = refs[:n], refs[n:2 * n], refs[2 * n:2 * n + 3]
        sems_out = refs[2 * n + 3:2 * n + 5]
        for w in range(n):
            local, sends, recvs = _gather_copies(ins, outs, sems_in, w)
            local.wait()
            for cp in sends:
                cp.wait_send()
            for cp in recvs:
                cp.wait_recv()
            for cp in _forward_copies(outs, sems_out, w)[0]:
                cp.start()
        refs[-1][...] = jnp.zeros_like(refs[-1])

    res = _chained(
        body, name=name, link=-1, in_specs=[HBM_SPEC] * (2 * n) + [SEM_SPEC] * 3,
        out_specs=[SEM_SPEC] * 2 + [HBM_SPEC] * n + [pl.BlockSpec(memory_space=pltpu.VMEM)],
        out_shape=[pltpu.SemaphoreType.DMA((3 * n,)), pltpu.SemaphoreType.DMA((3 * n,))]
        + [pltpu.HBM(l.shape, l.dtype) for l in lands] + [TOKEN],
        input_output_aliases={n + i: 2 + i for i in range(n)},
        compiler_params=pltpu.CompilerParams(has_side_effects=EFFECT),
    )(*shards, *lands, *sems)
    return n, res[:2], res[2:2 + n]


def _gather_end(name, state):
    n, sems, lands = state

    def body(*refs):
        outs, sems_in = refs[:n], refs[n:n + 2]
        for w in range(n):
            sends, recvs = _forward_copies(outs, sems_in, w)
            for cp in sends:
                cp.wait_send()
            for cp in recvs:
                cp.wait_recv()

    return _chained(
        body, name=name, in_specs=[HBM_SPEC] * n + [SEM_SPEC] * 2, out_specs=[HBM_SPEC] * n,
        out_shape=[pltpu.HBM(l.shape, l.dtype) for l in lands], input_output_aliases={i: i for i in range(n)},
        compiler_params=pltpu.CompilerParams(has_side_effects=EFFECT),
    )(*lands, *sems)


def _exchange_copies(ins, outs, sems, w):
    send_sems, recv_sems, local_sems = sems
    x, y, c, _ = _place()
    mine = _block_of(x, y, c)
    local = pltpu.make_async_copy(ins[w].at[mine], outs[w].at[mine], local_sems.at[w])
    remote = []
    for k in range(1, N_DEV):
        px, py, pc = x ^ (k >> 2), y ^ ((k >> 1) & 1), c ^ (k & 1)
        remote.append(pltpu.make_async_remote_copy(
            src_ref=ins[w].at[_block_of(px, py, pc)], dst_ref=outs[w].at[mine], send_sem=send_sems.at[7 * w + k - 1],
            recv_sem=recv_sems.at[7 * w + k - 1], device_id=(px, py, pc), device_id_type=MESH))
    return local, remote


def _exchange_start(name, stacks):
    n = len(stacks)

    def body(*refs):
        ins, outs, sems = refs[:n], refs[n:2 * n], refs[2 * n:2 * n + 3]
        for w in range(n):
            local, remote = _exchange_copies(ins, outs, sems, w)
            for cp in remote + [local]:
                cp.start()
        refs[-1][...] = jnp.zeros_like(refs[-1])

    lands = [lax.empty(s.shape, s.dtype) for s in stacks]
    if any(s is _Chain.last for s in stacks):
        _Chain.last = None
    res = _chained(
        body, name=name, link=-1, in_specs=[HBM_SPEC] * (2 * n),
        out_specs=[SEM_SPEC] * 3 + [HBM_SPEC] * (2 * n) + [pl.BlockSpec(memory_space=pltpu.VMEM)],
        out_shape=[pltpu.SemaphoreType.DMA((7 * n,)), pltpu.SemaphoreType.DMA((7 * n,)), pltpu.SemaphoreType.DMA((n,))]
        + [pltpu.HBM(s.shape, s.dtype) for s in stacks] * 2 + [TOKEN],
        input_output_aliases={i: 3 + i for i in range(2 * n)},
        compiler_params=pltpu.CompilerParams(has_side_effects=EFFECT),
    )(*[_in_hbm(s) for s in stacks], *[_in_hbm(l) for l in lands])
    return n, res[:3], res[3:3 + n], res[3 + n:3 + 2 * n]


def _exchange_wait(name, state):
    n, sems, stacks, lands = state

    def body(*refs):
        ins, outs, sems_in = refs[:n], refs[n:2 * n], refs[2 * n:2 * n + 3]
        for w in range(n):
            local, remote = _exchange_copies(ins, outs, sems_in, w)
            local.wait()
            for cp in remote:
                cp.wait_send()
                cp.wait_recv()

    return _chained(
        body, name=name, in_specs=[HBM_SPEC] * (2 * n) + [SEM_SPEC] * 3, out_specs=[HBM_SPEC] * n,
        out_shape=[pltpu.HBM(l.shape, l.dtype) for l in lands], input_output_aliases={n + i: i for i in range(n)},
        compiler_params=pltpu.CompilerParams(has_side_effects=EFFECT),
    )(*stacks, *lands, *sems)


N_CHIP = N_DEV // 2


def _pair_copies(stack, land, sems):
    send_sems, recv_sems, _ = sems
    x, y, c, _ = _place()
    remote = [pltpu.make_async_remote_copy(src_ref=stack.at[2 * k + 1 - c], dst_ref=land.at[k], send_sem=send_sems.at[k],
                                           recv_sem=recv_sems.at[k], device_id=(x, y, 1 - c), device_id_type=MESH)
              for k in range(N_CHIP)]
    return [], remote


def _chip_copies(pairs, land, sems):
    send_sems, recv_sems, local_sems = sems
    x, y, c, _ = _place()
    mine = 2 * x + y
    local = pltpu.make_async_copy(pairs.at[mine], land.at[mine], local_sems.at[0])
    remote = []
    for m in range(1, N_CHIP):
        px, py = x ^ (m >> 1), y ^ (m & 1)
        remote.append(pltpu.make_async_remote_copy(
            src_ref=pairs.at[2 * px + py], dst_ref=land.at[mine], send_sem=send_sems.at[m - 1],
            recv_sem=recv_sems.at[m - 1], device_id=(px, py, c), device_id_type=MESH))
    return local, remote


def _stage_start(name, copies_of, src_arr, n_land, n_sems):
    half = (N_CHIP,) + src_arr.shape[1:]

    def body(*refs):
        local, remote = copies_of(refs[0], *refs[1:1 + n_land], refs[1 + n_land:4 + n_land])
        for cp in remote + (local if isinstance(local, list) else [local]):
            cp.start()
        refs[-1][...] = jnp.zeros_like(refs[-1])

    if src_arr is _Chain.last:
        _Chain.last = None
    lands = [lax.empty(half, src_arr.dtype) for _ in range(n_land)]
    res = _chained(
        body, name=name, link=-1, in_specs=[HBM_SPEC] * (1 + n_land),
        out_specs=[SEM_SPEC] * 3 + [HBM_SPEC] * (1 + n_land) + [pl.BlockSpec(memory_space=pltpu.VMEM)],
        out_shape=[pltpu.SemaphoreType.DMA((n,)) for n in n_sems] + [pltpu.HBM(src_arr.shape, src_arr.dtype)]
        + [pltpu.HBM(half, src_arr.dtype)] * n_land + [TOKEN],
        input_output_aliases={i: 3 + i for i in range(1 + n_land)},
        compiler_params=pltpu.CompilerParams(has_side_effects=EFFECT),
    )(_in_hbm(src_arr), *[_in_hbm(l) for l in lands])
    return res[:3], res[3], res[4:4 + n_land]


def _stage_wait(name, copies_of, state):
    sems, src_arr, lands = state
    n_land = len(lands)

    def body(*refs):
        local, remote = copies_of(refs[0], *refs[1:1 + n_land], refs[1 + n_land:4 + n_land])
        for cp in (local if isinstance(local, list) else [local]):
            cp.wait()
        for cp in remote:
            cp.wait_send()
            cp.wait_recv()

    return _chained(
        body, name=name, in_specs=[HBM_SPEC] * (1 + n_land) + [SEM_SPEC] * 3, out_specs=[HBM_SPEC] * n_land,
        out_shape=[pltpu.HBM(l.shape, l.dtype) for l in lands],
        input_output_aliases={1 + i: i for i in range(n_land)},
        compiler_params=pltpu.CompilerParams(has_side_effects=EFFECT),
    )(src_arr, *lands, *sems)


def _pair_sum(name, stack, land, tm=64):
    n, rows, cols = land.shape
    tm = _tile(rows, tm)

    def body(c_ref, a_ref, b_ref, o_ref):
        o_ref[...] = (a_ref[...].astype(F32) + b_ref[...].astype(F32)).astype(o_ref.dtype)

    core = lax.axis_index("c").astype(jnp.int32).reshape(1)
    blk = pl.BlockSpec((n, tm, cols), lambda i, c_ref: (0, i, 0))
    mine = pl.BlockSpec((n, None, tm, cols), lambda i, c_ref: (0, c_ref[0], i, 0))
    _Chain.last = pl.pallas_call(
        body, name=name,
        grid_spec=pltpu.PrefetchScalarGridSpec(num_scalar_prefetch=1, grid=(rows // tm,), in_specs=[mine, blk], out_specs=blk),
        out_shape=jax.ShapeDtypeStruct(land.shape, land.dtype), compiler_params=_rowwise_params("parallel"),
    )(core, stack.reshape(n, 2, rows, cols), land)
    return _Chain.last


def _unstack_cols(w):
    return w.transpose(1, 0, 2).reshape(w.shape[1], N_DEV * w.shape[2])


def _stack_cols(w):
    return w.reshape(w.shape[0], N_DEV, w.shape[1] // N_DEV).transpose(1, 0, 2)


def _rope_tables(positions):
    half = MLA_ROPE // 2
    inv_freq = ROPE_THETA ** (-jnp.arange(half, dtype=F32) / half)
    ang = positions.astype(F32)[:, None] * inv_freq
    cos, sin = jnp.cos(ang), jnp.sin(ang)
    t = positions.shape[0]
    cos_t = jnp.concatenate([jnp.ones((t, MLA_NOPE), F32), cos, cos], axis=1)
    sin_t = jnp.concatenate([jnp.zeros((t, MLA_NOPE), F32), -sin, sin], axis=1)
    idx = jnp.arange(MLA_QK)
    partner = jnp.where(idx < MLA_NOPE, -1, jnp.where(idx < MLA_NOPE + half, idx + half, idx - half))
    swap = (idx[:, None] == partner[None, :]).astype(BF16)
    return cos_t, sin_t, swap


def _ffn_fwd(tag, x, gain, wt):
    h = _rms_fwd(tag + "_rms", x, gain, BF16)
    g = _mm_stack_nt_out(tag + "_gate", h, wt[tag + '_w_gate'], BF16)
    u, a = _mm_stack_nt_out(tag + "_up", h, wt[tag + '_w_up'], BF16,
                            fuse=(lambda u_blk, g_blk: (u_blk, g_blk * jax.nn.sigmoid(g_blk) * u_blk), [g]))
    y = _mm_stack_sum(tag + "_down", a, wt[tag + '_w_down'], F32, scale=0.5, res=x, tm=1024)
    return y, (x, h, g, u, a)


def _ffn_bwd_weights(tag, dy, saved, wd, comm):
    x, h, g, u, a = saved
    comm.grads({tag + '_w_down': _mm_stack_tn_right(tag + "_dwd", a, dy, BF16, scale=0.5)})

    def dact(da, g_blk, u_blk):
        sig = jax.nn.sigmoid(g_blk)
        return da * u_blk * sig * (1.0 + g_blk * (1.0 - sig)), da * g_blk * sig

    dg, du = _mm_stack_nt_out(tag + "_da", dy, wd, BF16, scale=0.5, fuse=(dact, [g, u]))
    comm.advance()
    comm.grads({tag + '_w_gate': _mm_stack_tn_right(tag + "_dwg", dg, h, BF16)})
    dwu = _mm_stack_tn_right(tag + "_dwu", du, h, BF16)
    comm.advance()
    comm.grads({tag + '_w_up': dwu})
    return dg, du


def _ffn_bwd_input(tag, dy, saved, dgu, gain, wg, wu, comm):
    dg, du = dgu
    dh = _mm_stack_sum(tag + "_dh", (dg, du), (wg, wu), F32)
    comm.advance()
    return _rms_bwd(tag + "_drms", saved[0], gain, dh, res=dy)


def _local_step(x, mem, positions, sm, comm, target):
    t, d = x.shape
    nm = mem.shape[0]
    gs, gw = {}, {}

    class Weights(dict):
        def __missing__(self, name):
            self.update(comm.weights(next(k for k, names in enumerate(GATHERS) if name in names)))
            return self[name]

    wt = Weights()

    x1, ffn1_saved = _ffn_fwd("ffn1", x, sm['ffn1_norm'], wt)

    h2 = _rms_fwd("mix_rms", x1, sm['mix_norm'], BF16)
    w_in_ref = _unstack_cols(wt['w_in'])
    pieces = []
    for name, n in IN_PAD:
        piece = w_in_ref[:, REF_OFF[name]:REF_OFF[name] + REF_SIZE[name]]
        if n != REF_SIZE[name]:
            piece = jnp.pad(piece, ((0, 0), (0, n - REF_SIZE[name])))
        pieces.append(piece)
    w_in = jnp.concatenate(pieces, axis=1)
    z = _mm2("mix_in", h2, w_in, NN, F32)
    zs = {name: z[:, PAD_OFF[name]:PAD_OFF[name] + n] for name, n in IN_PAD}

    cq = _rms_fwd("mla_q_a_rms", zs['zq'], sm['q_a_norm'], BF16)
    q_raw = _mm_stack_nt_out("mla_q_up", cq, wt['w_q_up'], F32)
    ckv = _rms_fwd("mla_kv_a_rms", zs['zkv'], sm['kv_a_norm'], BF16)
    kv = _mm_stack_out("mla_kv_up", ckv, wt['w_kv_up'], F32)
    zkr = zs['zkr'][:, :MLA_ROPE]
    k_raw = jnp.concatenate([kv[:, :, :MLA_NOPE], jnp.broadcast_to(zkr[None], (MLA_HEADS, t, MLA_ROPE))], axis=2)
    v_mla = kv[:, :, MLA_NOPE:].astype(BF16)
    cos_t, sin_t, swap = _rope_tables(positions)
    q_raw2, k_raw2 = q_raw.reshape(MLA_HEADS * t, MLA_QK), k_raw.reshape(MLA_HEADS * t, MLA_QK)
    rope = (cos_t, sin_t, swap)
    qf = _rms_fwd("mla_q_rms", q_raw2, sm['mla_q_norm'], BF16, rope=rope).reshape(MLA_HEADS, t, MLA_QK)
    kf = _rms_fwd("mla_k_rms", k_raw2, sm['mla_k_norm'], BF16, rope=rope).reshape(MLA_HEADS, t, MLA_QK)
    o_mla = _attn_fwd("mla_attn", qf, kf, v_mla, MLA_QK ** -0.5, True)

    w_g2 = jnp.pad(_unstack_cols(wt['gla_w_gate2']), ((0, LANES - GLA_GATE_RANK), (0, 0)))
    pre = _mm2("gla_gate_pre", zs['zg'], w_g2, NN, F32)
    log_a = _gate_fwd("gla_gate", pre, sm['gla_b_gate'])
    o_gla_raw, states = _gla_fwd("gla_scan", zs['gq'], zs['gk'], zs['gv'], log_a, GLA_HEADS)
    o_gla_n = _rms_fwd("gla_out_rms", o_gla_raw, sm['gla_out_norm'], F32)
    o_gla = _swiglu_fwd("gla_out_gate", zs['zr'], o_gla_n, BF16)

    cat = jnp.concatenate([o_mla, o_gla], axis=1)
    w_out = wt['w_out'].reshape(d, d)
    x2 = _mm2("mix_out", cat, w_out, NN, F32, res=x1)

    w_mq, w_mk, w_mv = (wt[n].reshape(d, MEM_HEADS * MEM_HEAD_DIM) for n in ('mem_w_q', 'mem_w_k', 'mem_w_v'))
    hq = _rms_fwd("mem_attn_rms", x2, sm['mem_attn_norm'], BF16)
    hm = _rms_fwd("mem_rms", mem, sm['mem_norm'], BF16)

    def heads_out(name, a, b, out_dtype):
        m, kk = a.shape
        tm = _tile(m, 512)
        return _mm(name, a, b, (m // tm, MEM_HEADS, 1), ((tm, kk), lambda i, h, k: (i, 0)),
                   ((kk, MEM_HEAD_DIM), lambda i, h, k: (0, h)), ((None, tm, MEM_HEAD_DIM), lambda i, h, k: (h, i, 0)),
                   (MEM_HEADS, m, MEM_HEAD_DIM), out_dtype, NN)

    mq_raw = heads_out("mem_q", hq, w_mq, F32)
    mk_raw = heads_out("mem_k", hm, w_mk, F32)
    mv = heads_out("mem_v", hm, w_mv, BF16)
    mq = _rms_fwd("mem_q_rms", mq_raw.reshape(MEM_HEADS * t, MEM_HEAD_DIM), sm['mem_q_norm'], BF16)
    mk = _rms_fwd("mem_k_rms", mk_raw.reshape(MEM_HEADS * nm, MEM_HEAD_DIM), sm['mem_k_norm'], BF16)
    mq, mk = mq.reshape(MEM_HEADS, t, MEM_HEAD_DIM), mk.reshape(MEM_HEADS, nm, MEM_HEAD_DIM)
    o_mem = _attn_fwd("mem_attn", mq, mk, mv, MEM_HEAD_DIM ** -0.5, False)
    w_mo = wt['mem_w_o']
    mo_cols = w_mo.shape[2]
    tm = _tile(t, 512)
    x3 = _mm("mem_out", o_mem, w_mo, (t // tm, N_DEV, 1), ((tm, o_mem.shape[1]), lambda i, j, k: (i, 0)),
             ((None, o_mem.shape[1], mo_cols), lambda i, j, k: (j, 0, 0)), ((tm, mo_cols), lambda i, j, k: (i, j)),
             (t, d), F32, NN, res=x2)

    y, ffn2_saved = _ffn_fwd("ffn2", x3, sm['ffn2_norm'], wt)
    dy, dy_narrow, loss_lanes = _loss("loss", y, target)

    dgu = _ffn_bwd_weights("ffn2", dy_narrow, ffn2_saved, wt['ffn2_w_down'], comm)
    dx3, gs['ffn2_norm'] = _ffn_bwd_input("ffn2", dy, ffn2_saved, dgu, sm['ffn2_norm'],
                                          wt['ffn2_w_gate'], wt['ffn2_w_up'], comm)

    do_mem = _mm("mem_do", dx3, w_mo, (t // tm, MEM_HEADS, N_DEV), ((tm, mo_cols), lambda i, h, k: (i, k)),
                 ((None, MEM_HEAD_DIM, mo_cols), lambda i, h, k: (k, h, 0)), ((tm, MEM_HEAD_DIM), lambda i, h, k: (i, h)),
                 (t, MEM_HEADS * MEM_HEAD_DIM), BF16, NT)
    tk = _tile(t, 512)
    gw['mem_w_o'] = _mm("mem_dwo", o_mem, dx3, (N_DEV, 1, t // tk), ((tk, o_mem.shape[1]), lambda j, i, k: (k, 0)),
                        ((tk, mo_cols), lambda j, i, k: (k, j)), ((None, o_mem.shape[1], mo_cols), lambda j, i, k: (j, 0, 0)),
                        w_mo.shape, BF16, TN)
    dmq, dmk, dmv = _attn_bwd("mem_dattn", mq, mk, mv, do_mem, MEM_HEAD_DIM ** -0.5, False)
    dmq_raw, gs['mem_q_norm'] = _rms_bwd("mem_q_drms", mq_raw.reshape(MEM_HEADS * t, MEM_HEAD_DIM), sm['mem_q_norm'],
                                         dmq.reshape(MEM_HEADS * t, MEM_HEAD_DIM))
    dmk_raw, gs['mem_k_norm'] = _rms_bwd("mem_k_drms", mk_raw.reshape(MEM_HEADS * nm, MEM_HEAD_DIM), sm['mem_k_norm'],
                                         dmk.reshape(MEM_HEADS * nm, MEM_HEAD_DIM))
    dmq_raw = dmq_raw.reshape(MEM_HEADS, t, MEM_HEAD_DIM)
    dmk_raw = dmk_raw.reshape(MEM_HEADS, nm, MEM_HEAD_DIM)

    def heads_in_nt(name, a, b, res=None):
        m, n = a.shape[1], b.shape[0]
        tm_, tn_ = _tile(m, 512), _tile(n, 1024)
        return _mm(name, a, b, (m // tm_, n // tn_, MEM_HEADS), ((None, tm_, MEM_HEAD_DIM), lambda i, j, k: (k, i, 0)),
                   ((tn_, MEM_HEAD_DIM), lambda i, j, k: (j, k)), ((tm_, tn_), lambda i, j, k: (i, j)), (m, n), F32, NT,
                   None, res)

    def heads_tn(name, a, b):
        m, kp = a.shape
        tm_, tk_ = _tile(kp, 1024), _tile(m, 512)
        return _mm(name, a, b, (kp // tm_, MEM_HEADS, m // tk_), ((tk_, tm_), lambda i, h, k: (k, i)),
                   ((None, tk_, MEM_HEAD_DIM), lambda i, h, k: (h, k, 0)), ((tm_, MEM_HEAD_DIM), lambda i, h, k: (i, h)),
                   (kp, MEM_HEADS * MEM_HEAD_DIM), BF16, TN)

    dhq = heads_in_nt("mem_dhq", dmq_raw, w_mq)
    gw['mem_w_q'] = heads_tn("mem_dwq", hq, dmq_raw).reshape(wt['mem_w_q'].shape)
    dhm = heads_in_nt("mem_dhm_k", dmk_raw, w_mk)
    dhm = heads_in_nt("mem_dhm_v", dmv, w_mv, res=dhm)
    gw['mem_w_k'] = heads_tn("mem_dwk", hm, dmk_raw).reshape(wt['mem_w_k'].shape)
    gw['mem_w_v'] = heads_tn("mem_dwv", hm, dmv).reshape(wt['mem_w_v'].shape)
    _, gs['mem_norm'] = _rms_bwd("mem_drms", mem, sm['mem_norm'], dhm)
    comm.grads({n: gw[n] for n in MEMORY})
    dx2, gs['mem_attn_norm'] = _rms_bwd("mem_attn_drms", x2, sm['mem_attn_norm'], dhq, res=dx3)

    dcat = _mm2("mix_dcat", dx2, w_out, NT, F32)
    comm.grads({'w_out': _mm2("mix_dwout", cat, dx2, TN, BF16, tm=1024, tn=2048, tk=512).reshape(wt['w_out'].shape)})
    do_mla, do_gla = dcat[:, :MLA_HEADS * MLA_V], dcat[:, MLA_HEADS * MLA_V:]

    dzr, dgn = _swiglu_bwd("gla_out_dgate", do_gla, zs['zr'], o_gla_n, F32)
    do_gla_raw, gs['gla_out_norm'] = _rms_bwd("gla_out_drms", o_gla_raw, sm['gla_out_norm'], dgn)
    dgq, dgk, dgv, dlog_a = _gla_bwd("gla_dscan", zs['gq'], zs['gk'], zs['gv'], log_a, states, do_gla_raw, GLA_HEADS)
    dpre, gs['gla_b_gate'] = _gate_bwd("gla_dgate", pre, sm['gla_b_gate'], dlog_a)
    dw_g2 = _mm2("gla_dwgate", zs['zg'], dpre, TN, BF16, tk=512)
    comm.grads({'gla_w_gate2': _stack_cols(dw_g2[:GLA_GATE_RANK])})
    dzg = _mm2("gla_dzg", dpre, w_g2, NT, F32)

    dqf, dkf, dv_mla = _attn_bwd("mla_dattn", qf, kf, v_mla, do_mla, MLA_QK ** -0.5, True)
    dq_raw, gs['mla_q_norm'] = _rms_bwd("mla_q_drms", q_raw2, sm['mla_q_norm'], dqf.reshape(MLA_HEADS * t, MLA_QK), rope=rope)
    dk_raw, gs['mla_k_norm'] = _rms_bwd("mla_k_drms", k_raw2, sm['mla_k_norm'], dkf.reshape(MLA_HEADS * t, MLA_QK), rope=rope)
    dq_raw = dq_raw.reshape(MLA_HEADS, t, MLA_QK)
    dk_raw = dk_raw.reshape(MLA_HEADS, t, MLA_QK)
    dkv = jnp.concatenate([dk_raw[:, :, :MLA_NOPE], dv_mla], axis=2)
    dzkr = jnp.sum(dk_raw[:, :, MLA_NOPE:], axis=0)
    comm.grads({'w_q_up': _mm_stack_tn_right("mla_dwq", dq_raw, cq, BF16),
                'w_kv_up': _mm_stack_tn_left("mla_dwkv", ckv, dkv, BF16)})
    dcq = _mm_stack_sum("mla_dcq", dq_raw, wt['w_q_up'], F32)
    dckv = _mm_stack_nt_sum("mla_dckv", dkv, wt['w_kv_up'], F32)
    dzq, gs['q_a_norm'] = _rms_bwd("mla_q_a_drms", zs['zq'], sm['q_a_norm'], dcq)
    dzkv, gs['kv_a_norm'] = _rms_bwd("mla_kv_a_drms", zs['zkv'], sm['kv_a_norm'], dckv)

    dzs = {'zq': dzq, 'zkv': dzkv, 'gq': dgq, 'gk': dgk, 'gv': dgv, 'zr': dzr,
           'zkr': jnp.pad(dzkr, ((0, 0), (0, LANES - MLA_ROPE))), 'zg': dzg}
    dz = jnp.concatenate([dzs[name].astype(BF16) for name, _ in IN_PAD], axis=1)
    dw_in = _mm2("mix_dwin", h2, dz, TN, BF16, tm=1024, tn=2048, tk=512)
    dw_in_ref = jnp.concatenate([dw_in[:, PAD_OFF[name]:PAD_OFF[name] + n] for name, n in IN_REF], axis=1)
    comm.grads({'w_in': _stack_cols(dw_in_ref)})
    dh2 = _mm2("mix_dh", dz, w_in, NT, F32)
    comm.advance()
    dx1, dx1_narrow, gs['mix_norm'] = _rms_bwd("mix_drms", x1, sm['mix_norm'], dh2, res=dx2, narrow=True)

    dgu = _ffn_bwd_weights("ffn1", dx1_narrow, ffn1_saved, wt['ffn1_w_down'], comm)
    grad_x, gs['ffn1_norm'] = _ffn_bwd_input("ffn1", dx1, ffn1_saved, dgu, sm['ffn1_norm'],
                                             wt['ffn1_w_gate'], wt['ffn1_w_up'], comm)
    return loss_lanes, grad_x, gs


def _pad_lanes(v):
    n = v.shape[1]
    return jnp.pad(v, ((0, 0), (0, -n % LANES)))


def _pack_small(vals):
    return jnp.concatenate([_pad_lanes(vals[n]) for n in SMALL], axis=1)


def _unpack_small(packed, like):
    out, off = {}, 0
    for n in SMALL:
        size = like[n].shape[1]
        out[n] = packed[:, off:off + size]
        off += size + (-size % LANES)
    return out


def kernel(x, mem, positions, ffn1_norm, ffn1_w_gate, ffn1_w_up, ffn1_w_down, mix_norm, w_in, q_a_norm, w_q_up, kv_a_norm, w_kv_up, mla_q_norm, mla_k_norm, gla_w_gate2, gla_b_gate, gla_out_norm, w_out, mem_attn_norm, mem_norm, mem_w_q, mem_w_k, mem_w_v, mem_w_o, mem_q_norm, mem_k_norm, ffn2_norm, ffn2_w_gate, ffn2_w_up, ffn2_w_down, loss_target, m_ffn1_norm, m_ffn1_w_gate, m_ffn1_w_up, m_ffn1_w_down, m_mix_norm, m_w_in, m_q_a_norm, m_w_q_up, m_kv_a_norm, m_w_kv_up, m_mla_q_norm, m_mla_k_norm, m_gla_w_gate2, m_gla_b_gate, m_gla_out_norm, m_w_out, m_mem_attn_norm, m_mem_norm, m_mem_w_q, m_mem_w_k, m_mem_w_v, m_mem_w_o, m_mem_q_norm, m_mem_k_norm, m_ffn2_norm, m_ffn2_w_gate, m_ffn2_w_up, m_ffn2_w_down, v_ffn1_norm, v_ffn1_w_gate, v_ffn1_w_up, v_ffn1_w_down, v_mix_norm, v_w_in, v_q_a_norm, v_w_q_up, v_kv_a_norm, v_w_kv_up, v_mla_q_norm, v_mla_k_norm, v_gla_w_gate2, v_gla_b_gate, v_gla_out_norm, v_w_out, v_mem_attn_norm, v_mem_norm, v_mem_w_q, v_mem_w_k, v_mem_w_v, v_mem_w_o, v_mem_q_norm, v_mem_k_norm, v_ffn2_norm, v_ffn2_w_gate, v_ffn2_w_up, v_ffn2_w_down):
    inp = dict(locals())
    x, mem, positions, target = inp['x'][0], inp['mem'][0], inp['positions'][0], inp['loss_target'][0]
    sm = {n: inp[n] for n in SMALL}
    out = {}

    def stored(key):
        name = key[2:] if key[:2] in ('m_', 'v_') else key
        return inp[key][0].T if name in TRANSPOSED else inp[key][0]

    def as_given(name, r):
        return r.T[None] if name in TRANSPOSED else r[None]

    class Comm:
        def __init__(self):
            self.gathers = {0: self.start(0)}
            self.forwards, self.exchanges, self.pairs = {}, [], []

        def start(self, k):
            return _gather_start(f"gather_start_{k}", [stored(n).astype(BF16) for n in GATHERS[k]])

        def forward(self, k):
            if k not in self.forwards:
                self.forwards[k] = _gather_mid(f"gather_mid_{k}", self.gathers[k])
                self.gathers.update({nxt: self.start(nxt) for nxt in NEXT_GATHERS.get(k, [])})

        def weights(self, k):
            self.forward(k)
            if k in EARLY_FORWARD:
                self.forward(EARLY_FORWARD[k])
            return dict(zip(GATHERS[k], _gather_end(f"gather_end_{k}", self.forwards[k])))

        def grads(self, stacks):
            names = list(stacks)
            if names[0] in TWO_STAGE:
                (n,) = names
                self.pairs.append((n, _stage_start("pair_start_" + n, _pair_copies, stacks[n], 1, (N_CHIP, N_CHIP, 1))))
            else:
                self.exchanges.append((names, _exchange_start("exchange_start_" + names[0], [stacks[n] for n in names])))

        def advance(self):
            for n, state in self.pairs:
                (land,) = _stage_wait("pair_wait_" + n, _pair_copies, state)
                pairs = _pair_sum("pair_sum_" + n, state[1], land)
                self.exchanges.append(([n], _stage_start("chip_start_" + n, _chip_copies, pairs, 1, (N_CHIP - 1, N_CHIP - 1, 1))))
            self.pairs = []

        def update(self, count):
            todo, self.exchanges = self.exchanges[:count], self.exchanges[count:]
            for names, state in todo:
                if names[0] in TWO_STAGE:
                    parts = _stage_wait("chip_wait_" + names[0], _chip_copies, state)
                else:
                    parts = _exchange_wait("exchange_wait_" + names[0], state)
                for n, p in zip(names, parts):
                    res = _adamw("adamw_" + n, stored(n), stored('m_' + n), stored('v_' + n), p)
                    for kind, r in zip(('grad_', 'delta_', 'new_m_', 'new_v_'), res):
                        out[kind + n] = as_given(n, r)

    _Chain.last = None
    comm = Comm()
    loss_lanes, grad_x, gs = _local_step(x, mem, positions, sm, comm, target)
    out['loss'] = lax.psum(jnp.sum(loss_lanes), ("x", "y", "c"))
    out['grad_x'] = grad_x[None]

    comm.update(len(comm.exchanges) - 3)
    small_parts = _all_gather("gather_small", [_pack_small(gs)])[0]
    res = _adamw("adamw_small", _pack_small(sm), _pack_small({n: inp['m_' + n] for n in SMALL}),
                 _pack_small({n: inp['v_' + n] for n in SMALL}), small_parts)
    for kind, r in zip(('grad_', 'delta_', 'new_m_', 'new_v_'), res):
        for n, val in _unpack_small(r, sm).items():
            out[kind + n] = val

    comm.update(3)

    names = ['loss', 'grad_x'] + [k + n for k in ('grad_', 'delta_', 'new_m_', 'new_v_') for n in WEIGHTS]
    return tuple(out[n] for n in names)
```

```python
import functools
import math

import jax
import jax.numpy as jnp
from jax import lax
from jax.experimental import pallas as pl
from jax.experimental.pallas import tpu as pltpu

F32 = jnp.float32
BF16 = jnp.bfloat16

N_DEV = 8
EPS = 1e-6
CHUNK = 64
MLA_HEADS, MLA_NOPE, MLA_ROPE, MLA_V = 8, 128, 64, 128
MLA_QK = MLA_NOPE + MLA_ROPE
MLA_Q_RANK, MLA_KV_RANK = 512, 256
ROPE_THETA = 10000.0
GLA_HEADS, GLA_DK, GLA_DV, GLA_GATE_RANK = 4, 128, 256, 16
GLA_TAU = 16.0
MEM_HEADS, MEM_HEAD_DIM = 4, 128
ADAM_LR, ADAM_B1, ADAM_B2, ADAM_EPS, ADAM_WD, ADAM_STEP = 0.001, 0.9, 0.999, 1e-08, 0.01, 10

V7X_VMEM_BYTES = 64 * 1024 * 1024
LANES = 128

NN = (((1,), (0,)), ((), ()))
NT = (((1,), (1,)), ((), ()))
TN = (((0,), (0,)), ((), ()))

WEIGHTS = ['ffn1_norm', 'ffn1_w_gate', 'ffn1_w_up', 'ffn1_w_down', 'mix_norm', 'w_in', 'q_a_norm', 'w_q_up',
           'kv_a_norm', 'w_kv_up', 'mla_q_norm', 'mla_k_norm', 'gla_w_gate2', 'gla_b_gate', 'gla_out_norm', 'w_out',
           'mem_attn_norm', 'mem_norm', 'mem_w_q', 'mem_w_k', 'mem_w_v', 'mem_w_o', 'mem_q_norm', 'mem_k_norm',
           'ffn2_norm', 'ffn2_w_gate', 'ffn2_w_up', 'ffn2_w_down']
SMALL = ['ffn1_norm', 'mix_norm', 'q_a_norm', 'kv_a_norm', 'mla_q_norm', 'mla_k_norm', 'gla_b_gate', 'gla_out_norm',
         'mem_attn_norm', 'mem_norm', 'mem_q_norm', 'mem_k_norm', 'ffn2_norm']
MIXER = ['w_in', 'w_q_up', 'w_kv_up', 'gla_w_gate2', 'w_out']
MEMORY = ['mem_w_q', 'mem_w_k', 'mem_w_v', 'mem_w_o']
TRANSPOSED = ['ffn1_w_gate', 'ffn1_w_up', 'ffn2_w_gate', 'ffn2_w_up', 'w_q_up']
TWO_STAGE = ['w_in', 'ffn1_w_down', 'ffn1_w_gate', 'ffn1_w_up']
GATHERS = [['ffn1_w_gate'], ['ffn1_w_up'], ['ffn1_w_down'], MIXER[:1], MIXER[1:], MEMORY,
           ['ffn2_w_gate', 'ffn2_w_up', 'ffn2_w_down']]
NEXT_GATHERS = {0: [1], 1: [2], 2: [3], 3: [4, 5, 6]}
EARLY_FORWARD = {5: 6}

IN_REF = [('zq', 512), ('zkv', 256), ('zkr', 64), ('gq', 512), ('gk', 512), ('gv', 1024), ('zg', 16), ('zr', 1024)]
IN_PAD = [('zq', 512), ('zkv', 256), ('gq', 512), ('gk', 512), ('gv', 1024), ('zr', 1024), ('zkr', 128), ('zg', 128)]
IN_WIDTH = sum(n for _, n in IN_REF)
IN_PAD_WIDTH = sum(n for _, n in IN_PAD)


def _offsets(layout):
    out, off = {}, 0
    for name, n in layout:
        out[name] = off
        off += n
    return out


REF_OFF, PAD_OFF = _offsets(IN_REF), _offsets(IN_PAD)
REF_SIZE = dict(IN_REF)


def _tile(n, pref):
    return pref if n % pref == 0 else n


def _block_bytes(blk, dtype):
    dims = [d for d in blk if d is not None]
    if len(dims) >= 1:
        dims[-1] = -(-dims[-1] // LANES) * LANES
    return math.prod(dims) * jnp.dtype(dtype).itemsize


def _vmem_limit(pipelined_bytes, resident_bytes=0):
    need = 2 * pipelined_bytes + resident_bytes + (8 << 20)
    return int(min(max(need, 32 << 20), V7X_VMEM_BYTES - (6 << 20)))


class _Chain:
    last = None


def _chained(body, *, in_specs, link=0, **kwargs):
    def call(*operands):
        dep = _Chain.last
        if dep is not None and any(o is dep for o in operands):
            dep = None
        if dep is None:
            res = pl.pallas_call(body, in_specs=in_specs, **kwargs)(*operands)
        else:
            n = len(operands)

            def chained_body(*refs):
                body(*refs[:n], *refs[n + 1:])

            res = pl.pallas_call(chained_body, in_specs=list(in_specs) + [pl.BlockSpec(memory_space=pl.ANY)],
                                 **kwargs)(*operands, dep)
        _Chain.last = res[link] if isinstance(res, (list, tuple)) else res
        return res

    return call


def _row_tile(rows, width, elements=1 << 18):
    lanes = -(-width // LANES) * LANES
    return _tile(rows, max(256, 1 << int(math.log2(max(1, elements // lanes)))))


def _streamed(*arrays):
    return [pltpu.with_memory_space_constraint(a, pltpu.HBM) for a in arrays]


def _rowwise_params(*semantics):
    return pltpu.CompilerParams(dimension_semantics=semantics, vmem_limit_bytes=48 << 20)


def _mm(name, a, b, grid, a_spec, b_spec, o_spec, out_shape, out_dtype, dims, scale=None, res=None, fuse=None):
    nk = grid[2]
    o_blk, o_map = o_spec
    acc_shape = tuple(d for d in o_blk if d is not None)
    extras = [res] if res is not None else (list(fuse[1]) if fuse else [])
    n_out = 2 if fuse else 1
    a_list, b_list = (list(a), list(b)) if isinstance(a, (tuple, list)) else ([a], [b])
    a_specs = a_spec if isinstance(a_spec, list) else [a_spec] * len(a_list)
    b_specs = b_spec if isinstance(b_spec, list) else [b_spec] * len(b_list)
    n_in = 2 * len(a_list)

    def body(*refs):
        a_refs, b_refs = refs[:n_in // 2], refs[n_in // 2:n_in]
        extra_refs = refs[n_in:n_in + len(extras)]
        out_refs = refs[n_in + len(extras):n_in + len(extras) + n_out]
        rest = refs[n_in + len(extras) + n_out:]
        r_ref = extra_refs[0] if res is not None else None
        o_ref = out_refs[0]

        def product():
            return sum(lax.dot_general(a_ref[...].astype(BF16), b_ref[...].astype(BF16), dims, preferred_element_type=F32)
                       for a_ref, b_ref in zip(a_refs, b_refs))

        def finish(r):
            if scale is not None:
                r = r * scale
            if r_ref is not None:
                r = r + r_ref[...].astype(F32)
            if fuse:
                for ref, val in zip(out_refs, fuse[0](r, *[e[...].astype(F32) for e in extra_refs])):
                    ref[...] = val.astype(ref.dtype)
            else:
                o_ref[...] = r.astype(o_ref.dtype)

        if nk == 1:
            finish(product())
        else:
            acc = rest[0]
            k = pl.program_id(2)

            @pl.when(k == 0)
            def _():
                acc[...] = product()

            @pl.when(k > 0)
            def _():
                acc[...] += product()

            @pl.when(k == nk - 1)
            def _():
                finish(acc[...])

    in_specs = [pl.BlockSpec(*spec) for spec in a_specs + b_specs]
    operands = a_list + b_list
    piped = sum(_block_bytes(spec[0], v.dtype) for spec, v in zip(a_specs + b_specs, operands))
    piped += _block_bytes(o_blk, out_dtype)
    for extra in extras:
        in_specs.append(pl.BlockSpec(o_blk, o_map))
        operands.append(extra)
        piped += _block_bytes(o_blk, extra.dtype)
    piped += (n_out - 1) * _block_bytes(o_blk, out_dtype)
    scratch = [pltpu.VMEM(acc_shape, F32)] if nk > 1 else []
    out_spec, out_sds = pl.BlockSpec(o_blk, o_map), jax.ShapeDtypeStruct(out_shape, out_dtype)
    return _chained(
        body, name=name, grid=grid, in_specs=in_specs, out_specs=(out_spec,) * n_out if fuse else out_spec,
        out_shape=(out_sds,) * n_out if fuse else out_sds, scratch_shapes=scratch,
        compiler_params=pltpu.CompilerParams(
            dimension_semantics=("parallel", "parallel", "arbitrary"),
            vmem_limit_bytes=_vmem_limit(piped, 3 * _block_bytes(acc_shape, F32))),
    )(*operands)


def _mm2(name, a, b, dims, out_dtype, tm=512, tn=1024, tk=2048, scale=None, res=None):
    if dims is NN:
        (m, kk), n = a.shape, b.shape[1]
    elif dims is NT:
        (m, kk), n = a.shape, b.shape[0]
    else:
        (kk, m), n = a.shape, b.shape[1]
    tm, tn, tk = _tile(m, tm), _tile(n, tn), _tile(kk, tk)
    a_spec = ((tk, tm), lambda i, j, k: (k, i)) if dims is TN else ((tm, tk), lambda i, j, k: (i, k))
    b_spec = ((tn, tk), lambda i, j, k: (j, k)) if dims is NT else ((tk, tn), lambda i, j, k: (k, j))
    return _mm(name, a, b, (m // tm, n // tn, kk // tk), a_spec, b_spec, ((tm, tn), lambda i, j, k: (i, j)),
               (m, n), out_dtype, dims, scale, res)


def _mm_stack_out(name, a, b, out_dtype, tm=512, tk=2048):
    (m, kk), (nj, _, n) = a.shape, b.shape
    tm, tk = _tile(m, tm), _tile(kk, tk)
    return _mm(name, a, b, (nj, m // tm, kk // tk), ((tm, tk), lambda j, i, k: (i, k)),
               ((None, tk, n), lambda j, i, k: (j, k, 0)), ((None, tm, n), lambda j, i, k: (j, i, 0)),
               (nj, m, n), out_dtype, NN)


def _mm_stack_nt_out(name, a, b, out_dtype, scale=None, fuse=None, tm=1024, tk=2048):
    (m, kk), (nj, n, _) = a.shape, b.shape
    tm, tk = _tile(m, tm), _tile(kk, tk)
    return _mm(name, a, b, (nj, m // tm, kk // tk), ((tm, tk), lambda j, i, k: (i, k)),
               ((None, n, tk), lambda j, i, k: (j, 0, k)), ((None, tm, n), lambda j, i, k: (j, i, 0)),
               (nj, m, n), out_dtype, NT, scale, fuse=fuse)


def _mm_stack_sum(name, a, b, out_dtype, scale=None, res=None, tm=512, tn=1024):
    (nj, m, f), n = (a[0] if isinstance(a, tuple) else a).shape, (b[0] if isinstance(b, tuple) else b).shape[2]
    tm, tn = _tile(m, tm), _tile(n, tn)
    pairs = list(zip(a, b)) if isinstance(a, tuple) else [(a, b)]
    step = 2 if nj % 2 == 0 else 1
    a_list, b_list, a_spec, b_spec = [], [], [], []
    for a_k, b_k in pairs:
        for s in range(step):
            a_list.append(a_k)
            b_list.append(b_k)
            a_spec.append(((None, tm, f), lambda i, j, k, s=s: (step * k + s, i, 0)))
            b_spec.append(((None, f, tn), lambda i, j, k, s=s: (step * k + s, 0, j)))
    return _mm(name, a_list, b_list, (m // tm, n // tn, nj // step), a_spec, b_spec, ((tm, tn), lambda i, j, k: (i, j)),
               (m, n), out_dtype, NN, scale, res)


def _mm_stack_nt_sum(name, a, b, out_dtype, res=None, tm=512, tn=1024):
    (nj, m, f), n = a.shape, b.shape[1]
    tm, tn = _tile(m, tm), _tile(n, tn)
    return _mm(name, a, b, (m // tm, n // tn, nj), ((None, tm, f), lambda i, j, k: (k, i, 0)),
               ((None, tn, f), lambda i, j, k: (k, j, 0)), ((tm, tn), lambda i, j, k: (i, j)),
               (m, n), out_dtype, NT, None, res)


def _mm_stack_tn_left(name, a, b, out_dtype, tm=1024, tk=512):
    (m, kp), (nj, _, n) = a.shape, b.shape
    tm, tk = _tile(kp, tm), _tile(m, tk)
    return _mm(name, a, b, (nj, kp // tm, m // tk), ((tk, tm), lambda j, i, k: (k, i)),
               ((None, tk, n), lambda j, i, k: (j, k, 0)), ((None, tm, n), lambda j, i, k: (j, i, 0)),
               (nj, kp, n), out_dtype, TN)


def _mm_stack_tn_right(name, a, b, out_dtype, scale=None, tn=2048, tk=1024):
    (nj, m, f), n = a.shape, b.shape[1]
    tn, tk = _tile(n, tn), _tile(m, tk)
    return _mm(name, a, b, (nj, n // tn, m // tk), ((None, tk, f), lambda j, i, k: (j, k, 0)),
               ((tk, tn), lambda j, i, k: (k, i)), ((None, f, tn), lambda j, i, k: (j, 0, i)),
               (nj, f, n), out_dtype, TN, scale)


def _split3(x):
    hi = x.astype(BF16)
    r1 = x - hi.astype(F32)
    mid = r1.astype(BF16)
    lo = (r1 - mid.astype(F32)).astype(BF16)
    return hi, mid, lo


def _swap_halves(x, swap):
    return sum(jnp.dot(part, swap, preferred_element_type=F32) for part in _split3(x))


def _rope_specs(rope, tm):
    cos_t, _, swap = rope
    nt = cos_t.shape[0] // tm
    tab = pl.BlockSpec((tm, cos_t.shape[1]), lambda i, c: (i % nt, 0))
    return [tab, tab, pl.BlockSpec(swap.shape, lambda i, c: (0, 0))]


def _rms_fwd(name, x, g, out_dtype, rope=None):
    rows, cols = x.shape
    d = g.shape[1]
    tm = _row_tile(rows if rope is None else rope[0].shape[0], d)

    def body(x_ref, g_ref, *refs):
        xf = x_ref[...].astype(F32)
        r = lax.rsqrt(jnp.mean(xf * xf, axis=-1, keepdims=True) + EPS)
        y = xf * r * g_ref[...]
        if rope is not None:
            c_ref, s_ref, p_ref = refs[:3]
            y = y * c_ref[...] + _swap_halves(y, p_ref[...]) * s_ref[...]
        refs[-1][...] = y.astype(refs[-1].dtype)

    return _chained(
        body, name=name, grid=(rows // tm, cols // d),
        in_specs=[pl.BlockSpec((tm, d), lambda i, c: (i, c)), pl.BlockSpec((1, d), lambda i, c: (0, 0))]
        + (_rope_specs(rope, tm) if rope is not None else []),
        out_specs=pl.BlockSpec((tm, d), lambda i, c: (i, c)),
        out_shape=jax.ShapeDtypeStruct((rows, cols), out_dtype),
        compiler_params=_rowwise_params("parallel", "parallel"),
    )(*_streamed(x), g, *(rope or ()))


def _rms_bwd(name, x, g, dy, res=None, rope=None, narrow=False):
    rows, cols = x.shape
    d = g.shape[1]
    tm = _row_tile(rows if rope is None else rope[0].shape[0], d)

    def body(*refs):
        x_ref, g_ref, dy_ref = refs[:3]
        n_out = 3 if narrow else 2
        dx_ref, dg_ref = refs[-n_out], refs[-1]
        r_ref = refs[3] if res is not None else None
        xf = x_ref[...].astype(F32)
        r = lax.rsqrt(jnp.mean(xf * xf, axis=-1, keepdims=True) + EPS)
        xhat = xf * r
        dyf = dy_ref[...].astype(F32)
        if rope is not None:
            c_ref, s_ref, p_ref = refs[-n_out - 3:-n_out]
            dyf = dyf * c_ref[...] + _swap_halves(dyf * s_ref[...], p_ref[...])

        @pl.when((pl.program_id(0) == 0) & (pl.program_id(1) == 0))
        def _():
            dg_ref[...] = jnp.zeros_like(dg_ref)

        dg_ref[...] += jnp.sum(dyf * xhat, axis=0, keepdims=True)
        dxh = dyf * g_ref[...]
        dx = r * (dxh - xhat * jnp.mean(dxh * xhat, axis=-1, keepdims=True))
        if r_ref is not None:
            dx = dx + r_ref[...].astype(F32)
        dx_ref[...] = dx
        if narrow:
            refs[-2][...] = dx.astype(BF16)

    blk = pl.BlockSpec((tm, d), lambda i, c: (i, c))
    in_specs = [blk, pl.BlockSpec((1, d), lambda i, c: (0, 0)), blk]
    operands = [x, g, dy]
    if res is not None:
        in_specs.append(blk)
        operands.append(res)
    if rope is not None:
        in_specs += _rope_specs(rope, tm)
        operands += list(rope)
    wide = [(blk, jax.ShapeDtypeStruct((rows, cols), F32))] + ([(blk, jax.ShapeDtypeStruct((rows, cols), BF16))] if narrow else [])
    outs = wide + [(pl.BlockSpec((1, d), lambda i, c: (0, 0)), jax.ShapeDtypeStruct((1, d), F32))]
    return _chained(
        body, name=name, grid=(rows // tm, cols // d), in_specs=in_specs,
        out_specs=tuple(spec for spec, _ in outs), out_shape=tuple(sds for _, sds in outs),
        compiler_params=_rowwise_params("arbitrary", "arbitrary"),
    )(*_streamed(x), g, *_streamed(*operands[2:4 if res is not None else 3]), *operands[4 if res is not None else 3:])


def _swiglu_fwd(name, g, u, out_dtype, tm=256):
    rows, cols = g.shape
    tm = _tile(rows, tm)

    def body(g_ref, u_ref, o_ref):
        gf = g_ref[...].astype(F32)
        o_ref[...] = (gf * jax.nn.sigmoid(gf) * u_ref[...].astype(F32)).astype(o_ref.dtype)

    blk = pl.BlockSpec((tm, cols), lambda i: (i, 0))
    return _chained(
        body, name=name, grid=(rows // tm,), in_specs=[blk, blk], out_specs=blk,
        out_shape=jax.ShapeDtypeStruct((rows, cols), out_dtype),
        compiler_params=_rowwise_params("parallel"),
    )(*_streamed(g, u))


def _swiglu_bwd(name, da, g, u, out_dtype, tm=256):
    rows, cols = g.shape
    tm = _tile(rows, tm)

    def body(da_ref, g_ref, u_ref, dg_ref, du_ref):
        gf = g_ref[...].astype(F32)
        daf = da_ref[...].astype(F32)
        sig = jax.nn.sigmoid(gf)
        du_ref[...] = (daf * gf * sig).astype(du_ref.dtype)
        dg_ref[...] = (daf * u_ref[...].astype(F32) * sig * (1.0 + gf * (1.0 - sig))).astype(dg_ref.dtype)

    blk = pl.BlockSpec((tm, cols), lambda i: (i, 0))
    sds = jax.ShapeDtypeStruct((rows, cols), out_dtype)
    return _chained(
        body, name=name, grid=(rows // tm,), in_specs=[blk, blk, blk], out_specs=(blk, blk), out_shape=(sds, sds),
        compiler_params=_rowwise_params("parallel"),
    )(*_streamed(da, g, u))


def _gate_fwd(name, pre, bias, tm=256):
    rows, cols = pre.shape
    tm = _tile(rows, tm)

    def body(p_ref, b_ref, o_ref):
        z = p_ref[...] + b_ref[...]
        o_ref[...] = (jnp.minimum(z, 0.0) - jnp.log(1.0 + jnp.exp(-jnp.abs(z)))) * (1.0 / GLA_TAU)

    blk = pl.BlockSpec((tm, cols), lambda i: (i, 0))
    return _chained(
        body, name=name, grid=(rows // tm,), in_specs=[blk, pl.BlockSpec((1, cols), lambda i: (0, 0))], out_specs=blk,
        out_shape=jax.ShapeDtypeStruct((rows, cols), F32),
        compiler_params=_rowwise_params("parallel"),
    )(pre, bias)


def _gate_bwd(name, pre, bias, dla, tm=256):
    rows, cols = pre.shape
    tm = _tile(rows, tm)

    def body(p_ref, b_ref, d_ref, dp_ref, db_ref):
        z = p_ref[...] + b_ref[...]
        dp = d_ref[...] * (1.0 / GLA_TAU) / (1.0 + jnp.exp(z))
        dp_ref[...] = dp

        @pl.when(pl.program_id(0) == 0)
        def _():
            db_ref[...] = jnp.zeros_like(db_ref)

        db_ref[...] += jnp.sum(dp, axis=0, keepdims=True)

    blk = pl.BlockSpec((tm, cols), lambda i: (i, 0))
    row = pl.BlockSpec((1, cols), lambda i: (0, 0))
    return _chained(
        body, name=name, grid=(rows // tm,), in_specs=[blk, row, blk], out_specs=(blk, row),
        out_shape=(jax.ShapeDtypeStruct((rows, cols), F32), jax.ShapeDtypeStruct((1, cols), F32)),
        compiler_params=_rowwise_params("arbitrary"),
    )(*_streamed(pre), bias, *_streamed(dla))


def _loss(name, y, target, tm=256):
    rows, d = y.shape
    tm = _tile(rows, tm)

    def body(y_ref, t_ref, dy_ref, dyb_ref, l_ref):
        err = y_ref[...] - t_ref[...]
        dy_ref[...] = err * (1.0 / d)
        dyb_ref[...] = (err * (1.0 / d)).astype(BF16)

        @pl.when(pl.program_id(0) == 0)
        def _():
            l_ref[...] = jnp.zeros_like(l_ref)

        sq = (err * err).reshape(tm // 8, 8, d)
        l_ref[...] += jnp.sum(sq, axis=0) * (0.5 / d)

    blk = pl.BlockSpec((tm, d), lambda i: (i, 0))
    return _chained(
        body, name=name, grid=(rows // tm,), in_specs=[blk, blk],
        out_specs=(blk, blk, pl.BlockSpec((8, d), lambda i: (0, 0))),
        out_shape=(jax.ShapeDtypeStruct((rows, d), F32), jax.ShapeDtypeStruct((rows, d), BF16),
                   jax.ShapeDtypeStruct((8, d), F32)),
        compiler_params=_rowwise_params("arbitrary"),
    )(*_streamed(y, target))


def _per_query_block(causal, nq, tq, tk, inner):
    if not causal:
        inner(0, tk)
        return
    for block in range(nq):
        pl.when(pl.program_id(1) == block)(functools.partial(inner, block * tq, (block + 1) * tq))


def _scores(q, k, scale, causal, q0):
    s = lax.dot_general(q, k, NT, preferred_element_type=F32) * scale
    if causal:
        tq, kv = s.shape
        own = s[:, kv - tq:]
        qc = lax.broadcasted_iota(jnp.int32, own.shape, 0) // CHUNK
        kc = lax.broadcasted_iota(jnp.int32, own.shape, 1) // CHUNK
        own = jnp.where(kc <= qc, own, -1e30)
        s = own if kv == tq else jnp.concatenate([s[:, :kv - tq], own], axis=1)
    e = jnp.exp(s - jnp.max(s, axis=-1, keepdims=True))
    return e, jnp.sum(e, axis=-1, keepdims=True)


def _attn_fwd(name, q, k, v, scale, causal, tq=256):
    nh, t, dk = q.shape
    tk, dv = k.shape[1], v.shape[2]
    tq = _tile(t, tq)

    def body(q_ref, k_ref, v_ref, o_ref):
        def inner(q0, kv):
            e, l = _scores(q_ref[...], k_ref[0:kv, :], scale, causal, q0)
            o = jnp.dot(e.astype(BF16), v_ref[0:kv, :], preferred_element_type=F32)
            o_ref[...] = (o / l).astype(o_ref.dtype)

        _per_query_block(causal, t // tq, tq, tk, inner)

    return _chained(
        body, name=name, grid=(nh, t // tq),
        in_specs=[pl.BlockSpec((None, tq, dk), lambda h, i: (h, i, 0)), pl.BlockSpec((None, tk, dk), lambda h, i: (h, 0, 0)),
                  pl.BlockSpec((None, tk, dv), lambda h, i: (h, 0, 0))],
        out_specs=pl.BlockSpec((tq, dv), lambda h, i: (i, h)),
        out_shape=jax.ShapeDtypeStruct((t, nh * dv), BF16),
        compiler_params=pltpu.CompilerParams(dimension_semantics=("parallel", "parallel"),
                                             vmem_limit_bytes=_vmem_limit(0, 6 * tq * tk * 4)),
    )(q, k, v)


def _attn_bwd(name, q, k, v, do, scale, causal, tq=256):
    nh, t, dk = q.shape
    tk, dv = k.shape[1], v.shape[2]
    tq = _tile(t, tq)

    def body(q_ref, k_ref, v_ref, do_ref, dq_ref, dk_ref, dv_ref):
        @pl.when(pl.program_id(1) == 0)
        def _():
            dk_ref[...] = jnp.zeros_like(dk_ref)
            dv_ref[...] = jnp.zeros_like(dv_ref)

        def inner(q0, kv):
            qb, kb = q_ref[...], k_ref[0:kv, :]
            e, l = _scores(qb, kb, scale, causal, q0)
            p = e * (1.0 / l)
            dob = do_ref[...].astype(BF16)
            dp = lax.dot_general(dob, v_ref[0:kv, :], NT, preferred_element_type=F32)
            ds = (p * (dp - jnp.sum(p * dp, axis=-1, keepdims=True)) * scale).astype(BF16)
            dq_ref[...] = jnp.dot(ds, kb, preferred_element_type=F32)
            dk_ref[0:kv, :] += lax.dot_general(ds, qb, TN, preferred_element_type=F32)
            dv_ref[0:kv, :] += lax.dot_general(p.astype(BF16), dob, TN, preferred_element_type=F32)

        _per_query_block(causal, t // tq, tq, tk, inner)

    return _chained(
        body, name=name, grid=(nh, t // tq),
        in_specs=[pl.BlockSpec((None, tq, dk), lambda h, i: (h, i, 0)), pl.BlockSpec((None, tk, dk), lambda h, i: (h, 0, 0)),
                  pl.BlockSpec((None, tk, dv), lambda h, i: (h, 0, 0)), pl.BlockSpec((tq, dv), lambda h, i: (i, h))],
        out_specs=(pl.BlockSpec((None, tq, dk), lambda h, i: (h, i, 0)), pl.BlockSpec((None, tk, dk), lambda h, i: (h, 0, 0)),
                   pl.BlockSpec((None, tk, dv), lambda h, i: (h, 0, 0))),
        out_shape=(jax.ShapeDtypeStruct((nh, t, dk), F32), jax.ShapeDtypeStruct((nh, tk, dk), F32),
                   jax.ShapeDtypeStruct((nh, tk, dv), F32)),
        compiler_params=pltpu.CompilerParams(dimension_semantics=("parallel", "arbitrary"),
                                             vmem_limit_bytes=_vmem_limit(0, 10 * tq * tk * 4)),
    )(q, k, v, do)


def _tri(lower):
    r = lax.broadcasted_iota(jnp.int32, (CHUNK, CHUNK), 0)
    c = lax.broadcasted_iota(jnp.int32, (CHUNK, CHUNK), 1)
    return jnp.where((c <= r) if lower else (c >= r), 1.0, 0.0).astype(BF16)


def _tri_dot(tri, x):
    return sum(jnp.dot(tri, part, preferred_element_type=F32) for part in _split3(x))


def _gla_fwd(name, q, k, v, la, nh):
    t = q.shape[0]
    dk, dv = q.shape[1] // nh, v.shape[1] // nh
    nc = t // CHUNK

    def body(q_ref, k_ref, v_ref, g_ref, o_ref, st_ref, state):
        @pl.when(pl.program_id(0) == 0)
        def _():
            state[...] = jnp.zeros_like(state)

        g = g_ref[...]
        b = _tri_dot(_tri(True), g)
        b_end = jnp.sum(g, axis=0, keepdims=True)
        k_dec = (k_ref[...] * jnp.exp(b_end - b)).astype(BF16)
        decay = jnp.exp(b_end)
        qc = (q_ref[...] * (dk ** -0.5)).astype(BF16)
        vb = v_ref[...].astype(BF16)
        for h in range(nh):
            ks, vs = slice(h * dk, (h + 1) * dk), slice(h * dv, (h + 1) * dv)
            u_t = lax.dot_general(vb[:, vs], k_dec[:, ks], TN, preferred_element_type=F32)
            new = state[h] * decay[:, ks] + u_t
            state[h] = new
            st_ref[h] = new
            o_ref[:, vs] = lax.dot_general(qc[:, ks], new.astype(BF16), NT, preferred_element_type=F32)

    kblk = pl.BlockSpec((CHUNK, nh * dk), lambda n: (n, 0))
    vblk = pl.BlockSpec((CHUNK, nh * dv), lambda n: (n, 0))
    return _chained(
        body, name=name, grid=(nc,), in_specs=[kblk, kblk, vblk, kblk],
        out_specs=(vblk, pl.BlockSpec((nh, None, dv, dk), lambda n: (0, n, 0, 0))),
        out_shape=(jax.ShapeDtypeStruct((t, nh * dv), F32), jax.ShapeDtypeStruct((nh, nc, dv, dk), F32)),
        scratch_shapes=[pltpu.VMEM((nh, dv, dk), F32)],
        compiler_params=_rowwise_params("arbitrary"),
    )(q, k, v, la)


def _gla_bwd(name, q, k, v, la, states, do, nh):
    t = q.shape[0]
    dk, dv = q.shape[1] // nh, v.shape[1] // nh
    nc = t // CHUNK
    scale = dk ** -0.5

    def body(q_ref, k_ref, v_ref, g_ref, do_ref, st_ref, sp_ref, dq_ref, dk_ref, dv_ref, dg_ref, carry):
        i = pl.program_id(0)

        @pl.when(i == 0)
        def _():
            carry[...] = jnp.zeros_like(carry)

        g = g_ref[...]
        b = _tri_dot(_tri(True), g)
        b_end = jnp.sum(g, axis=0, keepdims=True)
        w = jnp.exp(b_end - b)
        decay = jnp.exp(b_end)
        k_dec = k_ref[...] * w
        k_decb = k_dec.astype(BF16)
        qc = (q_ref[...] * scale).astype(BF16)
        dob = do_ref[...].astype(BF16)
        vb = v_ref[...].astype(BF16)
        dk_dec, ddecay = [], []
        for h in range(nh):
            ks, vs = slice(h * dk, (h + 1) * dk), slice(h * dv, (h + 1) * dv)
            dq_ref[:, ks] = jnp.dot(dob[:, vs], st_ref[h].astype(BF16), preferred_element_type=F32) * scale
            g_t = carry[h] + lax.dot_general(dob[:, vs], qc[:, ks], TN, preferred_element_type=F32)
            g_tb = g_t.astype(BF16)
            dk_dec.append(jnp.dot(vb[:, vs], g_tb, preferred_element_type=F32))
            dv_ref[:, vs] = lax.dot_general(k_decb[:, ks], g_tb, NT, preferred_element_type=F32)
            prev = jnp.where(i < nc - 1, sp_ref[h], 0.0)
            ddecay.append(jnp.sum(g_t * prev, axis=0, keepdims=True))
            carry[h] = g_t * decay[:, ks]
        dk_dec = jnp.concatenate(dk_dec, axis=1)
        dk_ref[...] = dk_dec * w
        e = dk_dec * k_dec
        db_end = jnp.sum(e, axis=0, keepdims=True) + jnp.concatenate(ddecay, axis=1) * decay
        dg_ref[...] = _tri_dot(_tri(False), -e) + db_end

    kblk = pl.BlockSpec((CHUNK, nh * dk), lambda i: (nc - 1 - i, 0))
    vblk = pl.BlockSpec((CHUNK, nh * dv), lambda i: (nc - 1 - i, 0))
    ksds = jax.ShapeDtypeStruct((t, nh * dk), F32)
    return _chained(
        body, name=name, grid=(nc,),
        in_specs=[kblk, kblk, vblk, kblk, vblk,
                  pl.BlockSpec((nh, None, dv, dk), lambda i: (0, nc - 1 - i, 0, 0)),
                  pl.BlockSpec((nh, None, dv, dk), lambda i: (0, jnp.maximum(nc - 2 - i, 0), 0, 0))],
        out_specs=(kblk, kblk, vblk, kblk),
        out_shape=(ksds, ksds, jax.ShapeDtypeStruct((t, nh * dv), F32), ksds),
        scratch_shapes=[pltpu.VMEM((nh, dv, dk), F32)],
        compiler_params=_rowwise_params("arbitrary"),
    )(q, k, v, la, do, states, states)


def _adamw(name, w, m, v, parts):
    rows, cols = w.shape
    tm = 1 << int(math.log2(max(8, (1 << 18) // (-(-cols // LANES) * LANES))))
    while rows % tm and tm > 8:
        tm //= 2
    tm = _tile(rows, tm)

    def body(w_ref, m_ref, v_ref, p_ref, g_ref, d_ref, nm_ref, nv_ref):
        g = p_ref[0].astype(F32)
        for s in range(1, parts.shape[0]):
            g = g + p_ref[s].astype(F32)
        m_new = ADAM_B1 * m_ref[...] + (1.0 - ADAM_B1) * g
        v_new = ADAM_B2 * v_ref[...] + (1.0 - ADAM_B2) * jnp.square(g)
        m_hat = m_new / (1.0 - ADAM_B1 ** ADAM_STEP)
        v_hat = v_new / (1.0 - ADAM_B2 ** ADAM_STEP)
        g_ref[...] = g
        d_ref[...] = -ADAM_LR * (m_hat / (jnp.sqrt(v_hat) + ADAM_EPS) + ADAM_WD * w_ref[...])
        nm_ref[...] = m_new
        nv_ref[...] = v_new

    blk = pl.BlockSpec((tm, cols), lambda i: (i, 0))
    sds = jax.ShapeDtypeStruct((rows, cols), F32)
    return _chained(
        body, name=name, grid=(rows // tm,),
        in_specs=[blk, blk, blk, pl.BlockSpec((parts.shape[0], tm, cols), lambda i: (0, i, 0))],
        out_specs=(blk, blk, blk, blk), out_shape=(sds, sds, sds, sds),
        compiler_params=_rowwise_params("parallel"),
    )(*_streamed(w, m, v, parts))


HBM = pl.BlockSpec(memory_space=pl.ANY)
MESH = pl.DeviceIdType.MESH


def _all_gather(name, shards):
    n = len(shards)

    def body(*refs):
        ins, outs = refs[:n], refs[n:2 * n]
        send_sems, recv_sems, local_sems = refs[2 * n:]
        x, y, c = lax.axis_index("x"), lax.axis_index("y"), lax.axis_index("c")
        me, sibling = (x, y, c), (x, y, 1 - c)
        chips = [(1 - x, y), (x, 1 - y), (1 - x, 1 - y)]

        def copy(w, k, block, to, src=None):
            dst = outs[w].at[4 * block[0] + 2 * block[1] + block[2]]
            return pltpu.make_async_remote_copy(
                src_ref=dst if src is None else src, dst_ref=dst, send_sem=send_sems.at[7 * w + k],
                recv_sem=recv_sems.at[7 * w + k], device_id=to, device_id_type=MESH)

        mine, first, passed = [], [], []
        for w in range(n):
            cp = pltpu.make_async_copy(ins[w], outs[w].at[4 * x + 2 * y + c], local_sems.at[w])
            cp.start()
            mine.append(cp)
            first.append(copy(w, 0, me, sibling, src=ins[w]))
            first += [copy(w, 1 + j, me, (*chip, c), src=ins[w]) for j, chip in enumerate(chips)]
        for cp in first:
            cp.start()
        for w in range(n):
            for j, chip in enumerate(chips):
                copy(w, 1 + j, (*chip, c), me).wait_recv()
                cp = copy(w, 4 + j, (*chip, c), sibling)
                cp.start()
                passed.append(cp)
        for w in range(n):
            copy(w, 0, sibling, me).wait_recv()
            for j, chip in enumerate(chips):
                copy(w, 4 + j, (*chip, 1 - c), me).wait_recv()
        for cp in first + passed:
            cp.wait_send()
        for cp in mine:
            cp.wait()

    return _chained(
        body, name=name, in_specs=[HBM] * n, out_specs=[HBM] * n,
        out_shape=[jax.ShapeDtypeStruct((N_DEV,) + s.shape, s.dtype) for s in shards],
        scratch_shapes=[pltpu.SemaphoreType.DMA((7 * n,)), pltpu.SemaphoreType.DMA((7 * n,)),
                        pltpu.SemaphoreType.DMA((n,))],
    )(*shards)


HBM_SPEC = pl.BlockSpec(memory_space=pltpu.HBM)
SEM_SPEC = pl.BlockSpec(memory_space=pltpu.SEMAPHORE)
EFFECT = pltpu.SideEffectType.DATAFLOW_SIDE_EFFECTING
TOKEN = jax.ShapeDtypeStruct((8, LANES), F32)


def _in_hbm(a):
    return pltpu.with_memory_space_constraint(a, pltpu.HBM)


def _place():
    x, y, c = lax.axis_index("x"), lax.axis_index("y"), lax.axis_index("c")
    chips = [(1 - x, y), (x, 1 - y), (1 - x, 1 - y)]
    return x, y, c, chips


def _block_of(px, py, pc):
    return 4 * px + 2 * py + pc


def _gather_copies(ins, outs, sems, w):
    send_sems, recv_sems, local_sems = sems
    x, y, c, chips = _place()
    peers = [(x, y, 1 - c)] + [(*chip, c) for chip in chips]

    def copy(k, block, to, src):
        dst = outs[w].at[_block_of(*block)]
        return pltpu.make_async_remote_copy(src_ref=dst if src is None else src, dst_ref=dst, send_sem=send_sems.at[4 * w + k],
                                            recv_sem=recv_sems.at[4 * w + k], device_id=to, device_id_type=MESH)

    local = pltpu.make_async_copy(ins[w], outs[w].at[_block_of(x, y, c)], local_sems.at[w])
    sends = [copy(k, (x, y, c), peer, ins[w]) for k, peer in enumerate(peers)]
    recvs = [copy(k, peer, (x, y, c), None) for k, peer in enumerate(peers)]
    return local, sends, recvs


def _forward_copies(outs, sems, w):
    send_sems, recv_sems = sems
    x, y, c, chips = _place()

    def copy(j, block, to):
        dst = outs[w].at[_block_of(*block)]
        return pltpu.make_async_remote_copy(src_ref=dst, dst_ref=dst, send_sem=send_sems.at[3 * w + j],
                                            recv_sem=recv_sems.at[3 * w + j], device_id=to, device_id_type=MESH)

    sends = [copy(j, (*chip, c), (x, y, 1 - c)) for j, chip in enumerate(chips)]
    recvs = [copy(j, (*chip, 1 - c), (x, y, c)) for j, chip in enumerate(chips)]
    return sends, recvs


def _gather_start(name, shards):
    n = len(shards)

    def body(*refs):
        ins, outs, sems = refs[:n], refs[n:2 * n], refs[2 * n:2 * n + 3]
        for w in range(n):
            local, sends, _ = _gather_copies(ins, outs, sems, w)
            for cp in sends + [local]:
                cp.start()
        refs[-1][...] = jnp.zeros_like(refs[-1])

    lands = [lax.empty((N_DEV,) + s.shape, s.dtype) for s in shards]
    res = _chained(
        body, name=name, link=-1, in_specs=[HBM_SPEC] * (2 * n),
        out_specs=[SEM_SPEC] * 3 + [HBM_SPEC] * (2 * n) + [pl.BlockSpec(memory_space=pltpu.VMEM)],
        out_shape=[pltpu.SemaphoreType.DMA((4 * n,)), pltpu.SemaphoreType.DMA((4 * n,)), pltpu.SemaphoreType.DMA((n,))]
        + [pltpu.HBM(s.shape, s.dtype) for s in shards] + [pltpu.HBM(l.shape, l.dtype) for l in lands] + [TOKEN],
        input_output_aliases={i: 3 + i for i in range(2 * n)},
        compiler_params=pltpu.CompilerParams(has_side_effects=EFFECT),
    )(*[_in_hbm(s) for s in shards], *[_in_hbm(l) for l in lands])
    return n, res[:3], res[3:3 + n], res[3 + n:3 + 2 * n]


def _gather_mid(name, state):
    n, sems, shards, lands = state

    def body(*refs):
        ins, outs, sems_in = refs[:n], refs[n:2 * n], refs[2 * n:2 * n + 3]
        sems_out = refs[2 * n + 3:2 * n + 5]
        for w in range(n):
            local, sends, recvs = _gather_copies(ins, outs, sems_in, w)
            local.wait()
            for cp in sends:
                cp.wait_send()
            for cp in recvs:
                cp.wait_recv()
            for cp in _forward_copies(outs, sems_out, w)[0]:
                cp.start()
        refs[-1][...] = jnp.zeros_like(refs[-1])

    res = _chained(
        body, name=name, link=-1, in_specs=[HBM_SPEC] * (2 * n) + [SEM_SPEC] * 3,
        out_specs=[SEM_SPEC] * 2 + [HBM_SPEC] * n + [pl.BlockSpec(memory_space=pltpu.VMEM)],
        out_shape=[pltpu.SemaphoreType.DMA((3 * n,)), pltpu.SemaphoreType.DMA((3 * n,))]
        + [pltpu.HBM(l.shape, l.dtype) for l in lands] + [TOKEN],
        input_output_aliases={n + i: 2 + i for i in range(n)},
        compiler_params=pltpu.CompilerParams(has_side_effects=EFFECT),
    )(*shards, *lands, *sems)
    return n, res[:2], res[2:2 + n]


def _gather_end(name, state):
    n, sems, lands = state

    def body(*refs):
        outs, sems_in = refs[:n], refs[n:n + 2]
        for w in range(n):
            sends, recvs = _forward_copies(outs, sems_in, w)
            for cp in sends:
                cp.wait_send()
            for cp in recvs:
                cp.wait_recv()

    return _chained(
        body, name=name, in_specs=[HBM_SPEC] * n + [SEM_SPEC] * 2, out_specs=[HBM_SPEC] * n,
        out_shape=[pltpu.HBM(l.shape, l.dtype) for l in lands], input_output_aliases={i: i for i in range(n)},
        compiler_params=pltpu.CompilerParams(has_side_effects=EFFECT),
    )(*lands, *sems)


def _exchange_copies(ins, outs, sems, w):
    send_sems, recv_sems, local_sems = sems
    x, y, c, _ = _place()
    mine = _block_of(x, y, c)
    local = pltpu.make_async_copy(ins[w].at[mine], outs[w].at[mine], local_sems.at[w])
    remote = []
    for k in range(1, N_DEV):
        px, py, pc = x ^ (k >> 2), y ^ ((k >> 1) & 1), c ^ (k & 1)
        remote.append(pltpu.make_async_remote_copy(
            src_ref=ins[w].at[_block_of(px, py, pc)], dst_ref=outs[w].at[mine], send_sem=send_sems.at[7 * w + k - 1],
            recv_sem=recv_sems.at[7 * w + k - 1], device_id=(px, py, pc), device_id_type=MESH))
    return local, remote


def _exchange_start(name, stacks):
    n = len(stacks)

    def body(*refs):
        ins, outs, sems = refs[:n], refs[n:2 * n], refs[2 * n:2 * n + 3]
        for w in range(n):
            local, remote = _exchange_copies(ins, outs, sems, w)
            for cp in remote + [local]:
                cp.start()
        refs[-1][...] = jnp.zeros_like(refs[-1])

    lands = [lax.empty(s.shape, s.dtype) for s in stacks]
    if any(s is _Chain.last for s in stacks):
        _Chain.last = None
    res = _chained(
        body, name=name, link=-1, in_specs=[HBM_SPEC] * (2 * n),
        out_specs=[SEM_SPEC] * 3 + [HBM_SPEC] * (2 * n) + [pl.BlockSpec(memory_space=pltpu.VMEM)],
        out_shape=[pltpu.SemaphoreType.DMA((7 * n,)), pltpu.SemaphoreType.DMA((7 * n,)), pltpu.SemaphoreType.DMA((n,))]
        + [pltpu.HBM(s.shape, s.dtype) for s in stacks] * 2 + [TOKEN],
        input_output_aliases={i: 3 + i for i in range(2 * n)},
        compiler_params=pltpu.CompilerParams(has_side_effects=EFFECT),
    )(*[_in_hbm(s) for s in stacks], *[_in_hbm(l) for l in lands])
    return n, res[:3], res[3:3 + n], res[3 + n:3 + 2 * n]


def _exchange_wait(name, state):
    n, sems, stacks, lands = state

    def body(*refs):
        ins, outs, sems_in = refs[:n], refs[n:2 * n], refs[2 * n:2 * n + 3]
        for w in range(n):
            local, remote = _exchange_copies(ins, outs, sems_in, w)
            local.wait()
            for cp in remote:
                cp.wait_send()
                cp.wait_recv()

    return _chained(
        body, name=name, in_specs=[HBM_SPEC] * (2 * n) + [SEM_SPEC] * 3, out_specs=[HBM_SPEC] * n,
        out_shape=[pltpu.HBM(l.shape, l.dtype) for l in lands], input_output_aliases={n + i: i for i in range(n)},
        compiler_params=pltpu.CompilerParams(has_side_effects=EFFECT),
    )(*stacks, *lands, *sems)


N_CHIP = N_DEV // 2


def _pair_copies(stack, land, sems):
    send_sems, recv_sems, _ = sems
    x, y, c, _ = _place()
    remote = [pltpu.make_async_remote_copy(src_ref=stack.at[2 * k + 1 - c], dst_ref=land.at[k], send_sem=send_sems.at[k],
                                           recv_sem=recv_sems.at[k], device_id=(x, y, 1 - c), device_id_type=MESH)
              for k in range(N_CHIP)]
    return [], remote


def _chip_copies(pairs, land, sems):
    send_sems, recv_sems, local_sems = sems
    x, y, c, _ = _place()
    mine = 2 * x + y
    local = pltpu.make_async_copy(pairs.at[mine], land.at[mine], local_sems.at[0])
    remote = []
    for m in range(1, N_CHIP):
        px, py = x ^ (m >> 1), y ^ (m & 1)
        remote.append(pltpu.make_async_remote_copy(
            src_ref=pairs.at[2 * px + py], dst_ref=land.at[mine], send_sem=send_sems.at[m - 1],
            recv_sem=recv_sems.at[m - 1], device_id=(px, py, c), device_id_type=MESH))
    return local, remote


def _stage_start(name, copies_of, src_arr, n_land, n_sems):
    half = (N_CHIP,) + src_arr.shape[1:]

    def body(*refs):
        local, remote = copies_of(refs[0], *refs[1:1 + n_land], refs[1 + n_land:4 + n_land])
        for cp in remote + (local if isinstance(local, list) else [local]):
            cp.start()
        refs[-1][...] = jnp.zeros_like(refs[-1])

    if src_arr is _Chain.last:
        _Chain.last = None
    lands = [lax.empty(half, src_arr.dtype) for _ in range(n_land)]
    res = _chained(
        body, name=name, link=-1, in_specs=[HBM_SPEC] * (1 + n_land),
        out_specs=[SEM_SPEC] * 3 + [HBM_SPEC] * (1 + n_land) + [pl.BlockSpec(memory_space=pltpu.VMEM)],
        out_shape=[pltpu.SemaphoreType.DMA((n,)) for n in n_sems] + [pltpu.HBM(src_arr.shape, src_arr.dtype)]
        + [pltpu.HBM(half, src_arr.dtype)] * n_land + [TOKEN],
        input_output_aliases={i: 3 + i for i in range(1 + n_land)},
        compiler_params=pltpu.CompilerParams(has_side_effects=EFFECT),
    )(_in_hbm(src_arr), *[_in_hbm(l) for l in lands])
    return res[:3], res[3], res[4:4 + n_land]


def _stage_wait(name, copies_of, state):
    sems, src_arr, lands = state
    n_land = len(lands)

    def body(*refs):
        local, remote = copies_of(refs[0], *refs[1:1 + n_land], refs[1 + n_land:4 + n_land])
        for cp in (local if isinstance(local, list) else [local]):
            cp.wait()
        for cp in remote:
            cp.wait_send()
            cp.wait_recv()

    return _chained(
        body, name=name, in_specs=[HBM_SPEC] * (1 + n_land) + [SEM_SPEC] * 3, out_specs=[HBM_SPEC] * n_land,
        out_shape=[pltpu.HBM(l.shape, l.dtype) for l in lands],
        input_output_aliases={1 + i: i for i in range(n_land)},
        compiler_params=pltpu.CompilerParams(has_side_effects=EFFECT),
    )(src_arr, *lands, *sems)


def _pair_sum(name, stack, land, tm=64):
    n, rows, cols = land.shape
    tm = _tile(rows, tm)

    def body(c_ref, a_ref, b_ref, o_ref):
        o_ref[...] = (a_ref[...].astype(F32) + b_ref[...].astype(F32)).astype(o_ref.dtype)

    core = lax.axis_index("c").astype(jnp.int32).reshape(1)
    blk = pl.BlockSpec((n, tm, cols), lambda i, c_ref: (0, i, 0))
    mine = pl.BlockSpec((n, None, tm, cols), lambda i, c_ref: (0, c_ref[0], i, 0))
    _Chain.last = pl.pallas_call(
        body, name=name,
        grid_spec=pltpu.PrefetchScalarGridSpec(num_scalar_prefetch=1, grid=(rows // tm,), in_specs=[mine, blk], out_specs=blk),
        out_shape=jax.ShapeDtypeStruct(land.shape, land.dtype), compiler_params=_rowwise_params("parallel"),
    )(core, stack.reshape(n, 2, rows, cols), land)
    return _Chain.last


def _unstack_cols(w):
    return w.transpose(1, 0, 2).reshape(w.shape[1], N_DEV * w.shape[2])


def _stack_cols(w):
    return w.reshape(w.shape[0], N_DEV, w.shape[1] // N_DEV).transpose(1, 0, 2)


def _rope_tables(positions):
    half = MLA_ROPE // 2
    inv_freq = ROPE_THETA ** (-jnp.arange(half, dtype=F32) / half)
    ang = positions.astype(F32)[:, None] * inv_freq
    cos, sin = jnp.cos(ang), jnp.sin(ang)
    t = positions.shape[0]
    cos_t = jnp.concatenate([jnp.ones((t, MLA_NOPE), F32), cos, cos], axis=1)
    sin_t = jnp.concatenate([jnp.zeros((t, MLA_NOPE), F32), -sin, sin], axis=1)
    idx = jnp.arange(MLA_QK)
    partner = jnp.where(idx < MLA_NOPE, -1, jnp.where(idx < MLA_NOPE + half, idx + half, idx - half))
    swap = (idx[:, None] == partner[None, :]).astype(BF16)
    return cos_t, sin_t, swap


def _ffn_fwd(tag, x, gain, wt):
    h = _rms_fwd(tag + "_rms", x, gain, BF16)
    g = _mm_stack_nt_out(tag + "_gate", h, wt[tag + '_w_gate'], BF16)
    u, a = _mm_stack_nt_out(tag + "_up", h, wt[tag + '_w_up'], BF16,
                            fuse=(lambda u_blk, g_blk: (u_blk, g_blk * jax.nn.sigmoid(g_blk) * u_blk), [g]))
    y = _mm_stack_sum(tag + "_down", a, wt[tag + '_w_down'], F32, scale=0.5, res=x, tm=1024)
    return y, (x, h, g, u, a)


def _ffn_bwd_weights(tag, dy, saved, wd, comm):
    x, h, g, u, a = saved
    comm.grads({tag + '_w_down': _mm_stack_tn_right(tag + "_dwd", a, dy, BF16, scale=0.5)})

    def dact(da, g_blk, u_blk):
        sig = jax.nn.sigmoid(g_blk)
        return da * u_blk * sig * (1.0 + g_blk * (1.0 - sig)), da * g_blk * sig

    dg, du = _mm_stack_nt_out(tag + "_da", dy, wd, BF16, scale=0.5, fuse=(dact, [g, u]))
    comm.advance()
    comm.grads({tag + '_w_gate': _mm_stack_tn_right(tag + "_dwg", dg, h, BF16)})
    dwu = _mm_stack_tn_right(tag + "_dwu", du, h, BF16)
    comm.advance()
    comm.grads({tag + '_w_up': dwu})
    return dg, du


def _ffn_bwd_input(tag, dy, saved, dgu, gain, wg, wu, comm):
    dg, du = dgu
    dh = _mm_stack_sum(tag + "_dh", (dg, du), (wg, wu), F32)
    comm.advance()
    return _rms_bwd(tag + "_drms", saved[0], gain, dh, res=dy)


def _local_step(x, mem, positions, sm, comm, target):
    t, d = x.shape
    nm = mem.shape[0]
    gs, gw = {}, {}

    class Weights(dict):
        def __missing__(self, name):
            self.update(comm.weights(next(k for k, names in enumerate(GATHERS) if name in names)))
            return self[name]

    wt = Weights()

    x1, ffn1_saved = _ffn_fwd("ffn1", x, sm['ffn1_norm'], wt)

    h2 = _rms_fwd("mix_rms", x1, sm['mix_norm'], BF16)
    w_in_ref = _unstack_cols(wt['w_in'])
    pieces = []
    for name, n in IN_PAD:
        piece = w_in_ref[:, REF_OFF[name]:REF_OFF[name] + REF_SIZE[name]]
        if n != REF_SIZE[name]:
            piece = jnp.pad(piece, ((0, 0), (0, n - REF_SIZE[name])))
        pieces.append(piece)
    w_in = jnp.concatenate(pieces, axis=1)
    z = _mm2("mix_in", h2, w_in, NN, F32)
    zs = {name: z[:, PAD_OFF[name]:PAD_OFF[name] + n] for name, n in IN_PAD}

    cq = _rms_fwd("mla_q_a_rms", zs['zq'], sm['q_a_norm'], BF16)
    q_raw = _mm_stack_nt_out("mla_q_up", cq, wt['w_q_up'], F32)
    ckv = _rms_fwd("mla_kv_a_rms", zs['zkv'], sm['kv_a_norm'], BF16)
    kv = _mm_stack_out("mla_kv_up", ckv, wt['w_kv_up'], F32)
    zkr = zs['zkr'][:, :MLA_ROPE]
    k_raw = jnp.concatenate([kv[:, :, :MLA_NOPE], jnp.broadcast_to(zkr[None], (MLA_HEADS, t, MLA_ROPE))], axis=2)
    v_mla = kv[:, :, MLA_NOPE:].astype(BF16)
    cos_t, sin_t, swap = _rope_tables(positions)
    q_raw2, k_raw2 = q_raw.reshape(MLA_HEADS * t, MLA_QK), k_raw.reshape(MLA_HEADS * t, MLA_QK)
    rope = (cos_t, sin_t, swap)
    qf = _rms_fwd("mla_q_rms", q_raw2, sm['mla_q_norm'], BF16, rope=rope).reshape(MLA_HEADS, t, MLA_QK)
    kf = _rms_fwd("mla_k_rms", k_raw2, sm['mla_k_norm'], BF16, rope=rope).reshape(MLA_HEADS, t, MLA_QK)
    o_mla = _attn_fwd("mla_attn", qf, kf, v_mla, MLA_QK ** -0.5, True)

    w_g2 = jnp.pad(_unstack_cols(wt['gla_w_gate2']), ((0, LANES - GLA_GATE_RANK), (0, 0)))
    pre = _mm2("gla_gate_pre", zs['zg'], w_g2, NN, F32)
    log_a = _gate_fwd("gla_gate", pre, sm['gla_b_gate'])
    o_gla_raw, states = _gla_fwd("gla_scan", zs['gq'], zs['gk'], zs['gv'], log_a, GLA_HEADS)
    o_gla_n = _rms_fwd("gla_out_rms", o_gla_raw, sm['gla_out_norm'], F32)
    o_gla = _swiglu_fwd("gla_out_gate", zs['zr'], o_gla_n, BF16)

    cat = jnp.concatenate([o_mla, o_gla], axis=1)
    w_out = wt['w_out'].reshape(d, d)
    x2 = _mm2("mix_out", cat, w_out, NN, F32, res=x1)

    w_mq, w_mk, w_mv = (wt[n].reshape(d, MEM_HEADS * MEM_HEAD_DIM) for n in ('mem_w_q', 'mem_w_k', 'mem_w_v'))
    hq = _rms_fwd("mem_attn_rms", x2, sm['mem_attn_norm'], BF16)
    hm = _rms_fwd("mem_rms", mem, sm['mem_norm'], BF16)

    def heads_out(name, a, b, out_dtype):
        m, kk = a.shape
        tm = _tile(m, 512)
        return _mm(name, a, b, (m // tm, MEM_HEADS, 1), ((tm, kk), lambda i, h, k: (i, 0)),
                   ((kk, MEM_HEAD_DIM), lambda i, h, k: (0, h)), ((None, tm, MEM_HEAD_DIM), lambda i, h, k: (h, i, 0)),
                   (MEM_HEADS, m, MEM_HEAD_DIM), out_dtype, NN)

    mq_raw = heads_out("mem_q", hq, w_mq, F32)
    mk_raw = heads_out("mem_k", hm, w_mk, F32)
    mv = heads_out("mem_v", hm, w_mv, BF16)
    mq = _rms_fwd("mem_q_rms", mq_raw.reshape(MEM_HEADS * t, MEM_HEAD_DIM), sm['mem_q_norm'], BF16)
    mk = _rms_fwd("mem_k_rms", mk_raw.reshape(MEM_HEADS * nm, MEM_HEAD_DIM), sm['mem_k_norm'], BF16)
    mq, mk = mq.reshape(MEM_HEADS, t, MEM_HEAD_DIM), mk.reshape(MEM_HEADS, nm, MEM_HEAD_DIM)
    o_mem = _attn_fwd("mem_attn", mq, mk, mv, MEM_HEAD_DIM ** -0.5, False)
    w_mo = wt['mem_w_o']
    mo_cols = w_mo.shape[2]
    tm = _tile(t, 512)
    x3 = _mm("mem_out", o_mem, w_mo, (t // tm, N_DEV, 1), ((tm, o_mem.shape[1]), lambda i, j, k: (i, 0)),
             ((None, o_mem.shape[1], mo_cols), lambda i, j, k: (j, 0, 0)), ((tm, mo_cols), lambda i, j, k: (i, j)),
             (t, d), F32, NN, res=x2)

    y, ffn2_saved = _ffn_fwd("ffn2", x3, sm['ffn2_norm'], wt)
    dy, dy_narrow, loss_lanes = _loss("loss", y, target)

    dgu = _ffn_bwd_weights("ffn2", dy_narrow, ffn2_saved, wt['ffn2_w_down'], comm)
    dx3, gs['ffn2_norm'] = _ffn_bwd_input("ffn2", dy, ffn2_saved, dgu, sm['ffn2_norm'],
                                          wt['ffn2_w_gate'], wt['ffn2_w_up'], comm)

    do_mem = _mm("mem_do", dx3, w_mo, (t // tm, 1, N_DEV), ((tm, mo_cols), lambda i, h, k: (i, k)),
                 ((None, w_mo.shape[1], mo_cols), lambda i, h, k: (k, 0, 0)), ((tm, w_mo.shape[1]), lambda i, h, k: (i, 0)),
                 (t, MEM_HEADS * MEM_HEAD_DIM), BF16, NT)
    tk = _tile(t, 512)
    gw['mem_w_o'] = _mm("mem_dwo", o_mem, dx3, (N_DEV, 1, t // tk), ((tk, o_mem.shape[1]), lambda j, i, k: (k, 0)),
                        ((tk, mo_cols), lambda j, i, k: (k, j)), ((None, o_mem.shape[1], mo_cols), lambda j, i, k: (j, 0, 0)),
                        w_mo.shape, BF16, TN)
    dmq, dmk, dmv = _attn_bwd("mem_dattn", mq, mk, mv, do_mem, MEM_HEAD_DIM ** -0.5, False)
    dmq_raw, gs['mem_q_norm'] = _rms_bwd("mem_q_drms", mq_raw.reshape(MEM_HEADS * t, MEM_HEAD_DIM), sm['mem_q_norm'],
                                         dmq.reshape(MEM_HEADS * t, MEM_HEAD_DIM))
    dmk_raw, gs['mem_k_norm'] = _rms_bwd("mem_k_drms", mk_raw.reshape(MEM_HEADS * nm, MEM_HEAD_DIM), sm['mem_k_norm'],
                                         dmk.reshape(MEM_HEADS * nm, MEM_HEAD_DIM))
    dmq_raw = dmq_raw.reshape(MEM_HEADS, t, MEM_HEAD_DIM)
    dmk_raw = dmk_raw.reshape(MEM_HEADS, nm, MEM_HEAD_DIM)

    def heads_in_nt(name, a, b, res=None):
        m, n = a.shape[1], b.shape[0]
        tm_, tn_ = _tile(m, 512), _tile(n, 1024)
        return _mm(name, a, b, (m // tm_, n // tn_, MEM_HEADS), ((None, tm_, MEM_HEAD_DIM), lambda i, j, k: (k, i, 0)),
                   ((tn_, MEM_HEAD_DIM), lambda i, j, k: (j, k)), ((tm_, tn_), lambda i, j, k: (i, j)), (m, n), F32, NT,
                   None, res)

    def heads_tn(name, a, b):
        m, kp = a.shape
        tm_, tk_ = _tile(kp, 1024), _tile(m, 512)
        return _mm(name, a, b, (kp // tm_, MEM_HEADS, m // tk_), ((tk_, tm_), lambda i, h, k: (k, i)),
                   ((None, tk_, MEM_HEAD_DIM), lambda i, h, k: (h, k, 0)), ((tm_, MEM_HEAD_DIM), lambda i, h, k: (i, h)),
                   (kp, MEM_HEADS * MEM_HEAD_DIM), BF16, TN)

    dhq = heads_in_nt("mem_dhq", dmq_raw, w_mq)
    gw['mem_w_q'] = heads_tn("mem_dwq", hq, dmq_raw).reshape(wt['mem_w_q'].shape)
    dhm = heads_in_nt("mem_dhm_k", dmk_raw, w_mk)
    dhm = heads_in_nt("mem_dhm_v", dmv, w_mv, res=dhm)
    gw['mem_w_k'] = heads_tn("mem_dwk", hm, dmk_raw).reshape(wt['mem_w_k'].shape)
    gw['mem_w_v'] = heads_tn("mem_dwv", hm, dmv).reshape(wt['mem_w_v'].shape)
    _, gs['mem_norm'] = _rms_bwd("mem_drms", mem, sm['mem_norm'], dhm)
    comm.grads({n: gw[n] for n in MEMORY})
    dx2, gs['mem_attn_norm'] = _rms_bwd("mem_attn_drms", x2, sm['mem_attn_norm'], dhq, res=dx3)

    dcat = _mm2("mix_dcat", dx2, w_out, NT, F32)
    comm.grads({'w_out': _mm2("mix_dwout", cat, dx2, TN, BF16, tm=1024, tn=2048, tk=512).reshape(wt['w_out'].shape)})
    do_mla, do_gla = dcat[:, :MLA_HEADS * MLA_V], dcat[:, MLA_HEADS * MLA_V:]

    dzr, dgn = _swiglu_bwd("gla_out_dgate", do_gla, zs['zr'], o_gla_n, F32)
    do_gla_raw, gs['gla_out_norm'] = _rms_bwd("gla_out_drms", o_gla_raw, sm['gla_out_norm'], dgn)
    dgq, dgk, dgv, dlog_a = _gla_bwd("gla_dscan", zs['gq'], zs['gk'], zs['gv'], log_a, states, do_gla_raw, GLA_HEADS)
    dpre, gs['gla_b_gate'] = _gate_bwd("gla_dgate", pre, sm['gla_b_gate'], dlog_a)
    dw_g2 = _mm2("gla_dwgate", zs['zg'], dpre, TN, BF16, tk=512)
    comm.grads({'gla_w_gate2': _stack_cols(dw_g2[:GLA_GATE_RANK])})
    dzg = _mm2("gla_dzg", dpre, w_g2, NT, F32)

    dqf, dkf, dv_mla = _attn_bwd("mla_dattn", qf, kf, v_mla, do_mla, MLA_QK ** -0.5, True)
    dq_raw, gs['mla_q_norm'] = _rms_bwd("mla_q_drms", q_raw2, sm['mla_q_norm'], dqf.reshape(MLA_HEADS * t, MLA_QK), rope=rope)
    dk_raw, gs['mla_k_norm'] = _rms_bwd("mla_k_drms", k_raw2, sm['mla_k_norm'], dkf.reshape(MLA_HEADS * t, MLA_QK), rope=rope)
    dq_raw = dq_raw.reshape(MLA_HEADS, t, MLA_QK)
    dk_raw = dk_raw.reshape(MLA_HEADS, t, MLA_QK)
    dkv = jnp.concatenate([dk_raw[:, :, :MLA_NOPE], dv_mla], axis=2)
    dzkr = jnp.sum(dk_raw[:, :, MLA_NOPE:], axis=0)
    comm.grads({'w_q_up': _mm_stack_tn_right("mla_dwq", dq_raw, cq, BF16),
                'w_kv_up': _mm_stack_tn_left("mla_dwkv", ckv, dkv, BF16)})
    dcq = _mm_stack_sum("mla_dcq", dq_raw, wt['w_q_up'], F32)
    dckv = _mm_stack_nt_sum("mla_dckv", dkv, wt['w_kv_up'], F32)
    dzq, gs['q_a_norm'] = _rms_bwd("mla_q_a_drms", zs['zq'], sm['q_a_norm'], dcq)
    dzkv, gs['kv_a_norm'] = _rms_bwd("mla_kv_a_drms", zs['zkv'], sm['kv_a_norm'], dckv)

    dzs = {'zq': dzq, 'zkv': dzkv, 'gq': dgq, 'gk': dgk, 'gv': dgv, 'zr': dzr,
           'zkr': jnp.pad(dzkr, ((0, 0), (0, LANES - MLA_ROPE))), 'zg': dzg}
    dz = jnp.concatenate([dzs[name].astype(BF16) for name, _ in IN_PAD], axis=1)
    dw_in = _mm2("mix_dwin", h2, dz, TN, BF16, tm=1024, tn=2048, tk=512)
    dw_in_ref = jnp.concatenate([dw_in[:, PAD_OFF[name]:PAD_OFF[name] + n] for name, n in IN_REF], axis=1)
    comm.grads({'w_in': _stack_cols(dw_in_ref)})
    dh2 = _mm2("mix_dh", dz, w_in, NT, F32)
    comm.advance()
    dx1, dx1_narrow, gs['mix_norm'] = _rms_bwd("mix_drms", x1, sm['mix_norm'], dh2, res=dx2, narrow=True)

    dgu = _ffn_bwd_weights("ffn1", dx1_narrow, ffn1_saved, wt['ffn1_w_down'], comm)
    grad_x, gs['ffn1_norm'] = _ffn_bwd_input("ffn1", dx1, ffn1_saved, dgu, sm['ffn1_norm'],
                                             wt['ffn1_w_gate'], wt['ffn1_w_up'], comm)
    return loss_lanes, grad_x, gs


def _pad_lanes(v):
    n = v.shape[1]
    return jnp.pad(v, ((0, 0), (0, -n % LANES)))


def _pack_small(vals):
    return jnp.concatenate([_pad_lanes(vals[n]) for n in SMALL], axis=1)


def _unpack_small(packed, like):
    out, off = {}, 0
    for n in SMALL:
        size = like[n].shape[1]
        out[n] = packed[:, off:off + size]
        off += size + (-size % LANES)
    return out


def kernel(x, mem, positions, ffn1_norm, ffn1_w_gate, ffn1_w_up, ffn1_w_down, mix_norm, w_in, q_a_norm, w_q_up, kv_a_norm, w_kv_up, mla_q_norm, mla_k_norm, gla_w_gate2, gla_b_gate, gla_out_norm, w_out, mem_attn_norm, mem_norm, mem_w_q, mem_w_k, mem_w_v, mem_w_o, mem_q_norm, mem_k_norm, ffn2_norm, ffn2_w_gate, ffn2_w_up, ffn2_w_down, loss_target, m_ffn1_norm, m_ffn1_w_gate, m_ffn1_w_up, m_ffn1_w_down, m_mix_norm, m_w_in, m_q_a_norm, m_w_q_up, m_kv_a_norm, m_w_kv_up, m_mla_q_norm, m_mla_k_norm, m_gla_w_gate2, m_gla_b_gate, m_gla_out_norm, m_w_out, m_mem_attn_norm, m_mem_norm, m_mem_w_q, m_mem_w_k, m_mem_w_v, m_mem_w_o, m_mem_q_norm, m_mem_k_norm, m_ffn2_norm, m_ffn2_w_gate, m_ffn2_w_up, m_ffn2_w_down, v_ffn1_norm, v_ffn1_w_gate, v_ffn1_w_up, v_ffn1_w_down, v_mix_norm, v_w_in, v_q_a_norm, v_w_q_up, v_kv_a_norm, v_w_kv_up, v_mla_q_norm, v_mla_k_norm, v_gla_w_gate2, v_gla_b_gate, v_gla_out_norm, v_w_out, v_mem_attn_norm, v_mem_norm, v_mem_w_q, v_mem_w_k, v_mem_w_v, v_mem_w_o, v_mem_q_norm, v_mem_k_norm, v_ffn2_norm, v_ffn2_w_gate, v_ffn2_w_up, v_ffn2_w_down):
    inp = dict(locals())
    x, mem, positions, target = inp['x'][0], inp['mem'][0], inp['positions'][0], inp['loss_target'][0]
    sm = {n: inp[n] for n in SMALL}
    out = {}

    def stored(key):
        name = key[2:] if key[:2] in ('m_', 'v_') else key
        return inp[key][0].T if name in TRANSPOSED else inp[key][0]

    def as_given(name, r):
        return r.T[None] if name in TRANSPOSED else r[None]

    class Comm:
        def __init__(self):
            self.gathers = {0: self.start(0)}
            self.forwards, self.exchanges, self.pairs = {}, [], []

        def start(self, k):
            return _gather_start(f"gather_start_{k}", [stored(n).astype(BF16) for n in GATHERS[k]])

        def forward(self, k):
            if k not in self.forwards:
                self.forwards[k] = _gather_mid(f"gather_mid_{k}", self.gathers[k])
                self.gathers.update({nxt: self.start(nxt) for nxt in NEXT_GATHERS.get(k, [])})

        def weights(self, k):
            self.forward(k)
            if k in EARLY_FORWARD:
                self.forward(EARLY_FORWARD[k])
            return dict(zip(GATHERS[k], _gather_end(f"gather_end_{k}", self.forwards[k])))

        def grads(self, stacks):
            names = list(stacks)
            if names[0] in TWO_STAGE:
                (n,) = names
                self.pairs.append((n, _stage_start("pair_start_" + n, _pair_copies, stacks[n], 1, (N_CHIP, N_CHIP, 1))))
            else:
                self.exchanges.append((names, _exchange_start("exchange_start_" + names[0], [stacks[n] for n in names])))

        def advance(self):
            for n, state in self.pairs:
                (land,) = _stage_wait("pair_wait_" + n, _pair_copies, state)
                pairs = _pair_sum("pair_sum_" + n, state[1], land)
                self.exchanges.append(([n], _stage_start("chip_start_" + n, _chip_copies, pairs, 1, (N_CHIP - 1, N_CHIP - 1, 1))))
            self.pairs = []

        def update(self, count):
            todo, self.exchanges = self.exchanges[:count], self.exchanges[count:]
            for names, state in todo:
                if names[0] in TWO_STAGE:
                    parts = _stage_wait("chip_wait_" + names[0], _chip_copies, state)
                else:
                    parts = _exchange_wait("exchange_wait_" + names[0], state)
                for n, p in zip(names, parts):
                    res = _adamw("adamw_" + n, stored(n), stored('m_' + n), stored('v_' + n), p)
                    for kind, r in zip(('grad_', 'delta_', 'new_m_', 'new_v_'), res):
                        out[kind + n] = as_given(n, r)

    _Chain.last = None
    comm = Comm()
    loss_lanes, grad_x, gs = _local_step(x, mem, positions, sm, comm, target)
    out['loss'] = lax.psum(jnp.sum(loss_lanes), ("x", "y", "c"))
    out['grad_x'] = grad_x[None]

    comm.update(len(comm.exchanges) - 3)
    small_parts = _all_gather("gather_small", [_pack_small(gs)])[0]
    res = _adamw("adamw_small", _pack_small(sm), _pack_small({n: inp['m_' + n] for n in SMALL}),
                 _pack_small({n: inp['v_' + n] for n in SMALL}), small_parts)
    for kind, r in zip(('grad_', 'delta_', 'new_m_', 'new_v_'), res):
        for n, val in _unpack_small(r, sm).items():
            out[kind + n] = val

    comm.update(3)

    names = ['loss', 'grad_x'] + [k + n for k in ('grad_', 'delta_', 'new_m_', 'new_v_') for n in WEIGHTS]
    return tuple(out[n] for n in names)
```

```python
import functools
import math

import jax
import jax.numpy as jnp
from jax import lax
from jax.experimental import pallas as pl
from jax.experimental.pallas import tpu as pltpu

F32 = jnp.float32
BF16 = jnp.bfloat16

N_DEV = 8
EPS = 1e-6
CHUNK = 64
MLA_HEADS, MLA_NOPE, MLA_ROPE, MLA_V = 8, 128, 64, 128
MLA_QK = MLA_NOPE + MLA_ROPE
MLA_Q_RANK, MLA_KV_RANK = 512, 256
ROPE_THETA = 10000.0
GLA_HEADS, GLA_DK, GLA_DV, GLA_GATE_RANK = 4, 128, 256, 16
GLA_TAU = 16.0
MEM_HEADS, MEM_HEAD_DIM = 4, 128
ADAM_LR, ADAM_B1, ADAM_B2, ADAM_EPS, ADAM_WD, ADAM_STEP = 0.001, 0.9, 0.999, 1e-08, 0.01, 10

V7X_VMEM_BYTES = 64 * 1024 * 1024
LANES = 128

NN = (((1,), (0,)), ((), ()))
NT = (((1,), (1,)), ((), ()))
TN = (((0,), (0,)), ((), ()))

WEIGHTS = ['ffn1_norm', 'ffn1_w_gate', 'ffn1_w_up', 'ffn1_w_down', 'mix_norm', 'w_in', 'q_a_norm', 'w_q_up',
           'kv_a_norm', 'w_kv_up', 'mla_q_norm', 'mla_k_norm', 'gla_w_gate2', 'gla_b_gate', 'gla_out_norm', 'w_out',
           'mem_attn_norm', 'mem_norm', 'mem_w_q', 'mem_w_k', 'mem_w_v', 'mem_w_o', 'mem_q_norm', 'mem_k_norm',
           'ffn2_norm', 'ffn2_w_gate', 'ffn2_w_up', 'ffn2_w_down']
SMALL = ['ffn1_norm', 'mix_norm', 'q_a_norm', 'kv_a_norm', 'mla_q_norm', 'mla_k_norm', 'gla_b_gate', 'gla_out_norm',
         'mem_attn_norm', 'mem_norm', 'mem_q_norm', 'mem_k_norm', 'ffn2_norm']
MIXER = ['w_in', 'w_q_up', 'w_kv_up', 'gla_w_gate2', 'w_out']
MEMORY = ['mem_w_q', 'mem_w_k', 'mem_w_v', 'mem_w_o']
TRANSPOSED = ['ffn1_w_gate', 'ffn1_w_up', 'ffn2_w_gate', 'ffn2_w_up', 'w_q_up']
TWO_STAGE = ['w_in', 'ffn1_w_down', 'ffn1_w_gate', 'ffn1_w_up']
GATHERS = [['ffn1_w_gate'], ['ffn1_w_up'], ['ffn1_w_down'], MIXER[:1], MIXER[1:], MEMORY,
           ['ffn2_w_gate', 'ffn2_w_up', 'ffn2_w_down']]
NEXT_GATHERS = {0: [1], 1: [2], 2: [3], 3: [4, 5, 6]}
EARLY_FORWARD = {5: 6}

IN_REF = [('zq', 512), ('zkv', 256), ('zkr', 64), ('gq', 512), ('gk', 512), ('gv', 1024), ('zg', 16), ('zr', 1024)]
IN_PAD = [('zq', 512), ('zkv', 256), ('gq', 512), ('gk', 512), ('gv', 1024), ('zr', 1024), ('zkr', 128), ('zg', 128)]
IN_WIDTH = sum(n for _, n in IN_REF)
IN_PAD_WIDTH = sum(n for _, n in IN_PAD)


def _offsets(layout):
    out, off = {}, 0
    for name, n in layout:
        out[name] = off
        off += n
    return out


REF_OFF, PAD_OFF = _offsets(IN_REF), _offsets(IN_PAD)
REF_SIZE = dict(IN_REF)


def _tile(n, pref):
    return pref if n % pref == 0 else n


def _block_bytes(blk, dtype):
    dims = [d for d in blk if d is not None]
    if len(dims) >= 1:
        dims[-1] = -(-dims[-1] // LANES) * LANES
    return math.prod(dims) * jnp.dtype(dtype).itemsize


def _vmem_limit(pipelined_bytes, resident_bytes=0):
    need = 2 * pipelined_bytes + resident_bytes + (8 << 20)
    return int(min(max(need, 32 << 20), V7X_VMEM_BYTES - (6 << 20)))


class _Chain:
    last = None


def _chained(body, *, in_specs, link=0, **kwargs):
    def call(*operands):
        dep = _Chain.last
        if dep is not None and any(o is dep for o in operands):
            dep = None
        if dep is None:
            res = pl.pallas_call(body, in_specs=in_specs, **kwargs)(*operands)
        else:
            n = len(operands)

            def chained_body(*refs):
                body(*refs[:n], *refs[n + 1:])

            res = pl.pallas_call(chained_body, in_specs=list(in_specs) + [pl.BlockSpec(memory_space=pl.ANY)],
                                 **kwargs)(*operands, dep)
        _Chain.last = res[link] if isinstance(res, (list, tuple)) else res
        return res

    return call


def _row_tile(rows, width, elements=1 << 18):
    lanes = -(-width // LANES) * LANES
    return _tile(rows, max(256, 1 << int(math.log2(max(1, elements // lanes)))))


def _streamed(*arrays):
    return [pltpu.with_memory_space_constraint(a, pltpu.HBM) for a in arrays]


def _rowwise_params(*semantics):
    return pltpu.CompilerParams(dimension_semantics=semantics, vmem_limit_bytes=48 << 20)


def _mm(name, a, b, grid, a_spec, b_spec, o_spec, out_shape, out_dtype, dims, scale=None, res=None, fuse=None):
    nk = grid[2]
    o_blk, o_map = o_spec
    acc_shape = tuple(d for d in o_blk if d is not None)
    extras = [res] if res is not None else (list(fuse[1]) if fuse else [])
    n_out = 2 if fuse else 1
    a_list, b_list = (list(a), list(b)) if isinstance(a, (tuple, list)) else ([a], [b])
    a_specs = a_spec if isinstance(a_spec, list) else [a_spec] * len(a_list)
    b_specs = b_spec if isinstance(b_spec, list) else [b_spec] * len(b_list)
    n_in = 2 * len(a_list)

    def body(*refs):
        a_refs, b_refs = refs[:n_in // 2], refs[n_in // 2:n_in]
        extra_refs = refs[n_in:n_in + len(extras)]
        out_refs = refs[n_in + len(extras):n_in + len(extras) + n_out]
        rest = refs[n_in + len(extras) + n_out:]
        r_ref = extra_refs[0] if res is not None else None
        o_ref = out_refs[0]

        def product():
            return sum(lax.dot_general(a_ref[...].astype(BF16), b_ref[...].astype(BF16), dims, preferred_element_type=F32)
                       for a_ref, b_ref in zip(a_refs, b_refs))

        def finish(r):
            if scale is not None:
                r = r * scale
            if r_ref is not None:
                r = r + r_ref[...].astype(F32)
            if fuse:
                for ref, val in zip(out_refs, fuse[0](r, *[e[...].astype(F32) for e in extra_refs])):
                    ref[...] = val.astype(ref.dtype)
            else:
                o_ref[...] = r.astype(o_ref.dtype)

        if nk == 1:
            finish(product())
        else:
            acc = rest[0]
            k = pl.program_id(2)

            @pl.when(k == 0)
            def _():
                acc[...] = product()

            @pl.when(k > 0)
            def _():
                acc[...] += product()

            @pl.when(k == nk - 1)
            def _():
                finish(acc[...])

    in_specs = [pl.BlockSpec(*spec) for spec in a_specs + b_specs]
    operands = a_list + b_list
    piped = sum(_block_bytes(spec[0], v.dtype) for spec, v in zip(a_specs + b_specs, operands))
    piped += _block_bytes(o_blk, out_dtype)
    for extra in extras:
        in_specs.append(pl.BlockSpec(o_blk, o_map))
        operands.append(extra)
        piped += _block_bytes(o_blk, extra.dtype)
    piped += (n_out - 1) * _block_bytes(o_blk, out_dtype)
    scratch = [pltpu.VMEM(acc_shape, F32)] if nk > 1 else []
    out_spec, out_sds = pl.BlockSpec(o_blk, o_map), jax.ShapeDtypeStruct(out_shape, out_dtype)
    return _chained(
        body, name=name, grid=grid, in_specs=in_specs, out_specs=(out_spec,) * n_out if fuse else out_spec,
        out_shape=(out_sds,) * n_out if fuse else out_sds, scratch_shapes=scratch,
        compiler_params=pltpu.CompilerParams(
            dimension_semantics=("parallel", "parallel", "arbitrary"),
            vmem_limit_bytes=_vmem_limit(piped, 3 * _block_bytes(acc_shape, F32))),
    )(*operands)


def _mm2(name, a, b, dims, out_dtype, tm=512, tn=1024, tk=2048, scale=None, res=None):
    if dims is NN:
        (m, kk), n = a.shape, b.shape[1]
    elif dims is NT:
        (m, kk), n = a.shape, b.shape[0]
    else:
        (kk, m), n = a.shape, b.shape[1]
    tm, tn, tk = _tile(m, tm), _tile(n, tn), _tile(kk, tk)
    a_spec = ((tk, tm), lambda i, j, k: (k, i)) if dims is TN else ((tm, tk), lambda i, j, k: (i, k))
    b_spec = ((tn, tk), lambda i, j, k: (j, k)) if dims is NT else ((tk, tn), lambda i, j, k: (k, j))
    return _mm(name, a, b, (m // tm, n // tn, kk // tk), a_spec, b_spec, ((tm, tn), lambda i, j, k: (i, j)),
               (m, n), out_dtype, dims, scale, res)


def _mm_stack_out(name, a, b, out_dtype, tm=512, tk=2048):
    (m, kk), (nj, _, n) = a.shape, b.shape
    tm, tk = _tile(m, tm), _tile(kk, tk)
    return _mm(name, a, b, (nj, m // tm, kk // tk), ((tm, tk), lambda j, i, k: (i, k)),
               ((None, tk, n), lambda j, i, k: (j, k, 0)), ((None, tm, n), lambda j, i, k: (j, i, 0)),
               (nj, m, n), out_dtype, NN)


def _mm_stack_nt_out(name, a, b, out_dtype, scale=None, fuse=None, tm=1024, tk=2048):
    (m, kk), (nj, n, _) = a.shape, b.shape
    tm, tk = _tile(m, tm), _tile(kk, tk)
    return _mm(name, a, b, (nj, m // tm, kk // tk), ((tm, tk), lambda j, i, k: (i, k)),
               ((None, n, tk), lambda j, i, k: (j, 0, k)), ((None, tm, n), lambda j, i, k: (j, i, 0)),
               (nj, m, n), out_dtype, NT, scale, fuse=fuse)


def _mm_stack_sum(name, a, b, out_dtype, scale=None, res=None, tm=512, tn=1024):
    (nj, m, f), n = (a[0] if isinstance(a, tuple) else a).shape, (b[0] if isinstance(b, tuple) else b).shape[2]
    tm, tn = _tile(m, tm), _tile(n, tn)
    pairs = list(zip(a, b)) if isinstance(a, tuple) else [(a, b)]
    step = 2 if nj % 2 == 0 else 1
    a_list, b_list, a_spec, b_spec = [], [], [], []
    for a_k, b_k in pairs:
        for s in range(step):
            a_list.append(a_k)
            b_list.append(b_k)
            a_spec.append(((None, tm, f), lambda i, j, k, s=s: (step * k + s, i, 0)))
            b_spec.append(((None, f, tn), lambda i, j, k, s=s: (step * k + s, 0, j)))
    return _mm(name, a_list, b_list, (m // tm, n // tn, nj // step), a_spec, b_spec, ((tm, tn), lambda i, j, k: (i, j)),
               (m, n), out_dtype, NN, scale, res)


def _mm_stack_nt_sum(name, a, b, out_dtype, res=None, tm=512, tn=1024):
    (nj, m, f), n = a.shape, b.shape[1]
    tm, tn = _tile(m, tm), _tile(n, tn)
    return _mm(name, a, b, (m // tm, n // tn, nj), ((None, tm, f), lambda i, j, k: (k, i, 0)),
               ((None, tn, f), lambda i, j, k: (k, j, 0)), ((tm, tn), lambda i, j, k: (i, j)),
               (m, n), out_dtype, NT, None, res)


def _mm_stack_tn_left(name, a, b, out_dtype, tm=1024, tk=512):
    (m, kp), (nj, _, n) = a.shape, b.shape
    tm, tk = _tile(kp, tm), _tile(m, tk)
    return _mm(name, a, b, (nj, kp // tm, m // tk), ((tk, tm), lambda j, i, k: (k, i)),
               ((None, tk, n), lambda j, i, k: (j, k, 0)), ((None, tm, n), lambda j, i, k: (j, i, 0)),
               (nj, kp, n), out_dtype, TN)


def _mm_stack_tn_right(name, a, b, out_dtype, scale=None, tn=2048, tk=1024):
    (nj, m, f), n = a.shape, b.shape[1]
    tn, tk = _tile(n, tn), _tile(m, tk)
    return _mm(name, a, b, (nj, n // tn, m // tk), ((None, tk, f), lambda j, i, k: (j, k, 0)),
               ((tk, tn), lambda j, i, k: (k, i)), ((None, f, tn), lambda j, i, k: (j, 0, i)),
               (nj, f, n), out_dtype, TN, scale)


def _split3(x):
    hi = x.astype(BF16)
    r1 = x - hi.astype(F32)
    mid = r1.astype(BF16)
    lo = (r1 - mid.astype(F32)).astype(BF16)
    return hi, mid, lo


def _swap_halves(x, swap):
    return sum(jnp.dot(part, swap, preferred_element_type=F32) for part in _split3(x))


def _rope_specs(rope, tm):
    cos_t, _, swap = rope
    nt = cos_t.shape[0] // tm
    tab = pl.BlockSpec((tm, cos_t.shape[1]), lambda i, c: (i % nt, 0))
    return [tab, tab, pl.BlockSpec(swap.shape, lambda i, c: (0, 0))]


def _rms_fwd(name, x, g, out_dtype, rope=None, scale=None):
    rows, cols = x.shape
    d = g.shape[1]
    tm = _row_tile(rows if rope is None else rope[0].shape[0], d)

    def body(x_ref, g_ref, *refs):
        xf = x_ref[...].astype(F32)
        r = lax.rsqrt(jnp.mean(xf * xf, axis=-1, keepdims=True) + EPS)
        y = xf * r * g_ref[...]
        if rope is not None:
            c_ref, s_ref, p_ref = refs[:3]
            y = y * c_ref[...] + _swap_halves(y, p_ref[...]) * s_ref[...]
        if scale is not None:
            y = y * scale
        refs[-1][...] = y.astype(refs[-1].dtype)

    return _chained(
        body, name=name, grid=(rows // tm, cols // d),
        in_specs=[pl.BlockSpec((tm, d), lambda i, c: (i, c)), pl.BlockSpec((1, d), lambda i, c: (0, 0))]
        + (_rope_specs(rope, tm) if rope is not None else []),
        out_specs=pl.BlockSpec((tm, d), lambda i, c: (i, c)),
        out_shape=jax.ShapeDtypeStruct((rows, cols), out_dtype),
        compiler_params=_rowwise_params("parallel", "parallel"),
    )(*_streamed(x), g, *(rope or ()))


def _rms_bwd(name, x, g, dy, res=None, rope=None, narrow=False, scale=None):
    rows, cols = x.shape
    d = g.shape[1]
    tm = _row_tile(rows if rope is None else rope[0].shape[0], d)

    def body(*refs):
        x_ref, g_ref, dy_ref = refs[:3]
        n_out = 3 if narrow else 2
        dx_ref, dg_ref = refs[-n_out], refs[-1]
        r_ref = refs[3] if res is not None else None
        xf = x_ref[...].astype(F32)
        r = lax.rsqrt(jnp.mean(xf * xf, axis=-1, keepdims=True) + EPS)
        xhat = xf * r
        dyf = dy_ref[...].astype(F32)
        if scale is not None:
            dyf = dyf * scale
        if rope is not None:
            c_ref, s_ref, p_ref = refs[-n_out - 3:-n_out]
            dyf = dyf * c_ref[...] + _swap_halves(dyf * s_ref[...], p_ref[...])

        @pl.when((pl.program_id(0) == 0) & (pl.program_id(1) == 0))
        def _():
            dg_ref[...] = jnp.zeros_like(dg_ref)

        dg_ref[...] += jnp.sum(dyf * xhat, axis=0, keepdims=True)
        dxh = dyf * g_ref[...]
        dx = r * (dxh - xhat * jnp.mean(dxh * xhat, axis=-1, keepdims=True))
        if r_ref is not None:
            dx = dx + r_ref[...].astype(F32)
        dx_ref[...] = dx
        if narrow:
            refs[-2][...] = dx.astype(BF16)

    blk = pl.BlockSpec((tm, d), lambda i, c: (i, c))
    in_specs = [blk, pl.BlockSpec((1, d), lambda i, c: (0, 0)), blk]
    operands = [x, g, dy]
    if res is not None:
        in_specs.append(blk)
        operands.append(res)
    if rope is not None:
        in_specs += _rope_specs(rope, tm)
        operands += list(rope)
    wide = [(blk, jax.ShapeDtypeStruct((rows, cols), F32))] + ([(blk, jax.ShapeDtypeStruct((rows, cols), BF16))] if narrow else [])
    outs = wide + [(pl.BlockSpec((1, d), lambda i, c: (0, 0)), jax.ShapeDtypeStruct((1, d), F32))]
    return _chained(
        body, name=name, grid=(rows // tm, cols // d), in_specs=in_specs,
        out_specs=tuple(spec for spec, _ in outs), out_shape=tuple(sds for _, sds in outs),
        compiler_params=_rowwise_params("arbitrary", "arbitrary"),
    )(*_streamed(x), g, *_streamed(*operands[2:4 if res is not None else 3]), *operands[4 if res is not None else 3:])


def _swiglu_fwd(name, g, u, out_dtype, tm=256):
    rows, cols = g.shape
    tm = _tile(rows, tm)

    def body(g_ref, u_ref, o_ref):
        gf = g_ref[...].astype(F32)
        o_ref[...] = (gf * jax.nn.sigmoid(gf) * u_ref[...].astype(F32)).astype(o_ref.dtype)

    blk = pl.BlockSpec((tm, cols), lambda i: (i, 0))
    return _chained(
        body, name=name, grid=(rows // tm,), in_specs=[blk, blk], out_specs=blk,
        out_shape=jax.ShapeDtypeStruct((rows, cols), out_dtype),
        compiler_params=_rowwise_params("parallel"),
    )(*_streamed(g, u))


def _swiglu_bwd(name, da, g, u, out_dtype, tm=256):
    rows, cols = g.shape
    tm = _tile(rows, tm)

    def body(da_ref, g_ref, u_ref, dg_ref, du_ref):
        gf = g_ref[...].astype(F32)
        daf = da_ref[...].astype(F32)
        sig = jax.nn.sigmoid(gf)
        du_ref[...] = (daf * gf * sig).astype(du_ref.dtype)
        dg_ref[...] = (daf * u_ref[...].astype(F32) * sig * (1.0 + gf * (1.0 - sig))).astype(dg_ref.dtype)

    blk = pl.BlockSpec((tm, cols), lambda i: (i, 0))
    sds = jax.ShapeDtypeStruct((rows, cols), out_dtype)
    return _chained(
        body, name=name, grid=(rows // tm,), in_specs=[blk, blk, blk], out_specs=(blk, blk), out_shape=(sds, sds),
        compiler_params=_rowwise_params("parallel"),
    )(*_streamed(da, g, u))


def _gate_fwd(name, pre, bias, tm=256):
    rows, cols = pre.shape
    tm = _tile(rows, tm)

    def body(p_ref, b_ref, o_ref):
        z = p_ref[...] + b_ref[...]
        o_ref[...] = (jnp.minimum(z, 0.0) - jnp.log(1.0 + jnp.exp(-jnp.abs(z)))) * (1.0 / GLA_TAU)

    blk = pl.BlockSpec((tm, cols), lambda i: (i, 0))
    return _chained(
        body, name=name, grid=(rows // tm,), in_specs=[blk, pl.BlockSpec((1, cols), lambda i: (0, 0))], out_specs=blk,
        out_shape=jax.ShapeDtypeStruct((rows, cols), F32),
        compiler_params=_rowwise_params("parallel"),
    )(pre, bias)


def _gate_bwd(name, pre, bias, dla, tm=256):
    rows, cols = pre.shape
    tm = _tile(rows, tm)

    def body(p_ref, b_ref, d_ref, dp_ref, db_ref):
        z = p_ref[...] + b_ref[...]
        dp = d_ref[...] * (1.0 / GLA_TAU) / (1.0 + jnp.exp(z))
        dp_ref[...] = dp

        @pl.when(pl.program_id(0) == 0)
        def _():
            db_ref[...] = jnp.zeros_like(db_ref)

        db_ref[...] += jnp.sum(dp, axis=0, keepdims=True)

    blk = pl.BlockSpec((tm, cols), lambda i: (i, 0))
    row = pl.BlockSpec((1, cols), lambda i: (0, 0))
    return _chained(
        body, name=name, grid=(rows // tm,), in_specs=[blk, row, blk], out_specs=(blk, row),
        out_shape=(jax.ShapeDtypeStruct((rows, cols), F32), jax.ShapeDtypeStruct((1, cols), F32)),
        compiler_params=_rowwise_params("arbitrary"),
    )(*_streamed(pre), bias, *_streamed(dla))


def _loss(name, y, target, tm=256):
    rows, d = y.shape
    tm = _tile(rows, tm)

    def body(y_ref, t_ref, dy_ref, dyb_ref, l_ref):
        err = y_ref[...] - t_ref[...]
        dy_ref[...] = err * (1.0 / d)
        dyb_ref[...] = (err * (1.0 / d)).astype(BF16)

        @pl.when(pl.program_id(0) == 0)
        def _():
            l_ref[...] = jnp.zeros_like(l_ref)

        sq = (err * err).reshape(tm // 8, 8, d)
        l_ref[...] += jnp.sum(sq, axis=0) * (0.5 / d)

    blk = pl.BlockSpec((tm, d), lambda i: (i, 0))
    return _chained(
        body, name=name, grid=(rows // tm,), in_specs=[blk, blk],
        out_specs=(blk, blk, pl.BlockSpec((8, d), lambda i: (0, 0))),
        out_shape=(jax.ShapeDtypeStruct((rows, d), F32), jax.ShapeDtypeStruct((rows, d), BF16),
                   jax.ShapeDtypeStruct((8, d), F32)),
        compiler_params=_rowwise_params("arbitrary"),
    )(*_streamed(y, target))


def _per_query_block(causal, nq, tq, tk, inner):
    if not causal:
        inner(0, tk)
        return
    for block in range(nq):
        pl.when(pl.program_id(1) == block)(functools.partial(inner, block * tq, (block + 1) * tq))


def _scores(q, k, causal):
    s = lax.dot_general(q, k, NT, preferred_element_type=F32)
    if causal:
        tq, kv = s.shape
        own = s[:, kv - tq:]
        qc = lax.broadcasted_iota(jnp.int32, own.shape, 0) // CHUNK
        kc = lax.broadcasted_iota(jnp.int32, own.shape, 1) // CHUNK
        own = jnp.where(kc <= qc, own, -1e30)
        s = own if kv == tq else jnp.concatenate([s[:, :kv - tq], own], axis=1)
    e = jnp.exp(s - jnp.max(s, axis=-1, keepdims=True))
    return e, jnp.sum(e, axis=-1, keepdims=True)


def _attn_fwd(name, q, k, v, causal, tq=256):
    nh, t, dk = q.shape
    tk, dv = k.shape[1], v.shape[2]
    tq = _tile(t, tq)

    def body(q_ref, k_ref, v_ref, o_ref):
        def inner(q0, kv):
            e, l = _scores(q_ref[...], k_ref[0:kv, :], causal)
            o = jnp.dot(e.astype(BF16), v_ref[0:kv, :], preferred_element_type=F32)
            o_ref[...] = (o / l).astype(o_ref.dtype)

        _per_query_block(causal, t // tq, tq, tk, inner)

    return _chained(
        body, name=name, grid=(nh, t // tq),
        in_specs=[pl.BlockSpec((None, tq, dk), lambda h, i: (h, i, 0)), pl.BlockSpec((None, tk, dk), lambda h, i: (h, 0, 0)),
                  pl.BlockSpec((None, tk, dv), lambda h, i: (h, 0, 0))],
        out_specs=pl.BlockSpec((tq, dv), lambda h, i: (i, h)),
        out_shape=jax.ShapeDtypeStruct((t, nh * dv), BF16),
        compiler_params=pltpu.CompilerParams(dimension_semantics=("parallel", "parallel"),
                                             vmem_limit_bytes=_vmem_limit(0, 6 * tq * tk * 4)),
    )(q, k, v)


def _attn_bwd(name, q, k, v, do, causal, tq=256):
    nh, t, dk = q.shape
    tk, dv = k.shape[1], v.shape[2]
    tq = _tile(t, tq)

    def body(q_ref, k_ref, v_ref, do_ref, dq_ref, dk_ref, dv_ref):
        @pl.when(pl.program_id(1) == 0)
        def _():
            dk_ref[...] = jnp.zeros_like(dk_ref)
            dv_ref[...] = jnp.zeros_like(dv_ref)

        def inner(q0, kv):
            qb, kb = q_ref[...], k_ref[0:kv, :]
            e, l = _scores(qb, kb, causal)
            p = e * (1.0 / l)
            dob = do_ref[...].astype(BF16)
            dp = lax.dot_general(dob, v_ref[0:kv, :], NT, preferred_element_type=F32)
            ds = (p * (dp - jnp.sum(p * dp, axis=-1, keepdims=True))).astype(BF16)
            dq_ref[...] = jnp.dot(ds, kb, preferred_element_type=F32)
            dk_ref[0:kv, :] += lax.dot_general(ds, qb, TN, preferred_element_type=F32)
            dv_ref[0:kv, :] += lax.dot_general(p.astype(BF16), dob, TN, preferred_element_type=F32)

        _per_query_block(causal, t // tq, tq, tk, inner)

    return _chained(
        body, name=name, grid=(nh, t // tq),
        in_specs=[pl.BlockSpec((None, tq, dk), lambda h, i: (h, i, 0)), pl.BlockSpec((None, tk, dk), lambda h, i: (h, 0, 0)),
                  pl.BlockSpec((None, tk, dv), lambda h, i: (h, 0, 0)), pl.BlockSpec((tq, dv), lambda h, i: (i, h))],
        out_specs=(pl.BlockSpec((None, tq, dk), lambda h, i: (h, i, 0)), pl.BlockSpec((None, tk, dk), lambda h, i: (h, 0, 0)),
                   pl.BlockSpec((None, tk, dv), lambda h, i: (h, 0, 0))),
        out_shape=(jax.ShapeDtypeStruct((nh, t, dk), F32), jax.ShapeDtypeStruct((nh, tk, dk), F32),
                   jax.ShapeDtypeStruct((nh, tk, dv), F32)),
        compiler_params=pltpu.CompilerParams(dimension_semantics=("parallel", "arbitrary"),
                                             vmem_limit_bytes=_vmem_limit(0, 10 * tq * tk * 4)),
    )(q, k, v, do)


def _tri(lower):
    r = lax.broadcasted_iota(jnp.int32, (CHUNK, CHUNK), 0)
    c = lax.broadcasted_iota(jnp.int32, (CHUNK, CHUNK), 1)
    return jnp.where((c <= r) if lower else (c >= r), 1.0, 0.0).astype(BF16)


def _tri_dot(tri, x):
    return sum(jnp.dot(tri, part, preferred_element_type=F32) for part in _split3(x))


def _gla_fwd(name, q, k, v, la, nh):
    t = q.shape[0]
    dk, dv = q.shape[1] // nh, v.shape[1] // nh
    nc = t // CHUNK

    def body(q_ref, k_ref, v_ref, g_ref, o_ref, st_ref, state):
        @pl.when(pl.program_id(0) == 0)
        def _():
            state[...] = jnp.zeros_like(state)

        g = g_ref[...]
        b = _tri_dot(_tri(True), g)
        b_end = jnp.sum(g, axis=0, keepdims=True)
        k_dec = (k_ref[...] * jnp.exp(b_end - b)).astype(BF16)
        decay = jnp.exp(b_end)
        qc = (q_ref[...] * (dk ** -0.5)).astype(BF16)
        vb = v_ref[...].astype(BF16)
        for h in range(nh):
            ks, vs = slice(h * dk, (h + 1) * dk), slice(h * dv, (h + 1) * dv)
            u_t = lax.dot_general(vb[:, vs], k_dec[:, ks], TN, preferred_element_type=F32)
            new = state[h] * decay[:, ks] + u_t
            state[h] = new
            st_ref[h] = new
            o_ref[:, vs] = lax.dot_general(qc[:, ks], new.astype(BF16), NT, preferred_element_type=F32)

    kblk = pl.BlockSpec((CHUNK, nh * dk), lambda n: (n, 0))
    vblk = pl.BlockSpec((CHUNK, nh * dv), lambda n: (n, 0))
    return _chained(
        body, name=name, grid=(nc,), in_specs=[kblk, kblk, vblk, kblk],
        out_specs=(vblk, pl.BlockSpec((nh, None, dv, dk), lambda n: (0, n, 0, 0))),
        out_shape=(jax.ShapeDtypeStruct((t, nh * dv), F32), jax.ShapeDtypeStruct((nh, nc, dv, dk), F32)),
        scratch_shapes=[pltpu.VMEM((nh, dv, dk), F32)],
        compiler_params=_rowwise_params("arbitrary"),
    )(q, k, v, la)


def _gla_bwd(name, q, k, v, la, states, do, nh):
    t = q.shape[0]
    dk, dv = q.shape[1] // nh, v.shape[1] // nh
    nc = t // CHUNK
    scale = dk ** -0.5

    def body(q_ref, k_ref, v_ref, g_ref, do_ref, st_ref, sp_ref, dq_ref, dk_ref, dv_ref, dg_ref, carry):
        i = pl.program_id(0)

        @pl.when(i == 0)
        def _():
            carry[...] = jnp.zeros_like(carry)

        g = g_ref[...]
        b = _tri_dot(_tri(True), g)
        b_end = jnp.sum(g, axis=0, keepdims=True)
        w = jnp.exp(b_end - b)
        decay = jnp.exp(b_end)
        k_dec = k_ref[...] * w
        k_decb = k_dec.astype(BF16)
        qc = (q_ref[...] * scale).astype(BF16)
        dob = do_ref[...].astype(BF16)
        vb = v_ref[...].astype(BF16)
        dk_dec, ddecay = [], []
        for h in range(nh):
            ks, vs = slice(h * dk, (h + 1) * dk), slice(h * dv, (h + 1) * dv)
            dq_ref[:, ks] = jnp.dot(dob[:, vs], st_ref[h].astype(BF16), preferred_element_type=F32) * scale
            g_t = carry[h] + lax.dot_general(dob[:, vs], qc[:, ks], TN, preferred_element_type=F32)
            g_tb = g_t.astype(BF16)
            dk_dec.append(jnp.dot(vb[:, vs], g_tb, preferred_element_type=F32))
            dv_ref[:, vs] = lax.dot_general(k_decb[:, ks], g_tb, NT, preferred_element_type=F32)
            prev = jnp.where(i < nc - 1, sp_ref[h], 0.0)
            ddecay.append(jnp.sum(g_t * prev, axis=0, keepdims=True))
            carry[h] = g_t * decay[:, ks]
        dk_dec = jnp.concatenate(dk_dec, axis=1)
        dk_ref[...] = dk_dec * w
        e = dk_dec * k_dec
        db_end = jnp.sum(e, axis=0, keepdims=True) + jnp.concatenate(ddecay, axis=1) * decay
        dg_ref[...] = _tri_dot(_tri(False), -e) + db_end

    kblk = pl.BlockSpec((CHUNK, nh * dk), lambda i: (nc - 1 - i, 0))
    vblk = pl.BlockSpec((CHUNK, nh * dv), lambda i: (nc - 1 - i, 0))
    ksds = jax.ShapeDtypeStruct((t, nh * dk), F32)
    return _chained(
        body, name=name, grid=(nc,),
        in_specs=[kblk, kblk, vblk, kblk, vblk,
                  pl.BlockSpec((nh, None, dv, dk), lambda i: (0, nc - 1 - i, 0, 0)),
                  pl.BlockSpec((nh, None, dv, dk), lambda i: (0, jnp.maximum(nc - 2 - i, 0), 0, 0))],
        out_specs=(kblk, kblk, vblk, kblk),
        out_shape=(ksds, ksds, jax.ShapeDtypeStruct((t, nh * dv), F32), ksds),
        scratch_shapes=[pltpu.VMEM((nh, dv, dk), F32)],
        compiler_params=_rowwise_params("arbitrary"),
    )(q, k, v, la, do, states, states)


def _adamw(name, w, m, v, parts):
    rows, cols = w.shape
    tm = 1 << int(math.log2(max(8, (1 << 18) // (-(-cols // LANES) * LANES))))
    while rows % tm and tm > 8:
        tm //= 2
    tm = _tile(rows, tm)

    def body(w_ref, m_ref, v_ref, p_ref, g_ref, d_ref, nm_ref, nv_ref):
        g = p_ref[0].astype(F32)
        for s in range(1, parts.shape[0]):
            g = g + p_ref[s].astype(F32)
        m_new = ADAM_B1 * m_ref[...] + (1.0 - ADAM_B1) * g
        v_new = ADAM_B2 * v_ref[...] + (1.0 - ADAM_B2) * jnp.square(g)
        m_hat = m_new / (1.0 - ADAM_B1 ** ADAM_STEP)
        v_hat = v_new / (1.0 - ADAM_B2 ** ADAM_STEP)
        g_ref[...] = g
        d_ref[...] = -ADAM_LR * (m_hat / (jnp.sqrt(v_hat) + ADAM_EPS) + ADAM_WD * w_ref[...])
        nm_ref[...] = m_new
        nv_ref[...] = v_new

    blk = pl.BlockSpec((tm, cols), lambda i: (i, 0))
    sds = jax.ShapeDtypeStruct((rows, cols), F32)
    return _chained(
        body, name=name, grid=(rows // tm,),
        in_specs=[blk, blk, blk, pl.BlockSpec((parts.shape[0], tm, cols), lambda i: (0, i, 0))],
        out_specs=(blk, blk, blk, blk), out_shape=(sds, sds, sds, sds),
        compiler_params=_rowwise_params("parallel"),
    )(*_streamed(w, m, v, parts))


HBM = pl.BlockSpec(memory_space=pl.ANY)
MESH = pl.DeviceIdType.MESH


def _all_gather(name, shards):
    n = len(shards)

    def body(*refs):
        ins, outs = refs[:n], refs[n:2 * n]
        send_sems, recv_sems, local_sems = refs[2 * n:]
        x, y, c = lax.axis_index("x"), lax.axis_index("y"), lax.axis_index("c")
        me, sibling = (x, y, c), (x, y, 1 - c)
        chips = [(1 - x, y), (x, 1 - y), (1 - x, 1 - y)]

        def copy(w, k, block, to, src=None):
            dst = outs[w].at[4 * block[0] + 2 * block[1] + block[2]]
            return pltpu.make_async_remote_copy(
                src_ref=dst if src is None else src, dst_ref=dst, send_sem=send_sems.at[7 * w + k],
                recv_sem=recv_sems.at[7 * w + k], device_id=to, device_id_type=MESH)

        mine, first, passed = [], [], []
        for w in range(n):
            cp = pltpu.make_async_copy(ins[w], outs[w].at[4 * x + 2 * y + c], local_sems.at[w])
            cp.start()
            mine.append(cp)
            first.append(copy(w, 0, me, sibling, src=ins[w]))
            first += [copy(w, 1 + j, me, (*chip, c), src=ins[w]) for j, chip in enumerate(chips)]
        for cp in first:
            cp.start()
        for w in range(n):
            for j, chip in enumerate(chips):
                copy(w, 1 + j, (*chip, c), me).wait_recv()
                cp = copy(w, 4 + j, (*chip, c), sibling)
                cp.start()
                passed.append(cp)
        for w in range(n):
            copy(w, 0, sibling, me).wait_recv()
            for j, chip in enumerate(chips):
                copy(w, 4 + j, (*chip, 1 - c), me).wait_recv()
        for cp in first + passed:
            cp.wait_send()
        for cp in mine:
            cp.wait()

    return _chained(
        body, name=name, in_specs=[HBM] * n, out_specs=[HBM] * n,
        out_shape=[jax.ShapeDtypeStruct((N_DEV,) + s.shape, s.dtype) for s in shards],
        scratch_shapes=[pltpu.SemaphoreType.DMA((7 * n,)), pltpu.SemaphoreType.DMA((7 * n,)),
                        pltpu.SemaphoreType.DMA((n,))],
    )(*shards)


HBM_SPEC = pl.BlockSpec(memory_space=pltpu.HBM)
SEM_SPEC = pl.BlockSpec(memory_space=pltpu.SEMAPHORE)
EFFECT = pltpu.SideEffectType.DATAFLOW_SIDE_EFFECTING
TOKEN = jax.ShapeDtypeStruct((8, LANES), F32)


def _in_hbm(a):
    return pltpu.with_memory_space_constraint(a, pltpu.HBM)


def _place():
    x, y, c = lax.axis_index("x"), lax.axis_index("y"), lax.axis_index("c")
    chips = [(1 - x, y), (x, 1 - y), (1 - x, 1 - y)]
    return x, y, c, chips


def _block_of(px, py, pc):
    return 4 * px + 2 * py + pc


def _gather_copies(ins, outs, sems, w):
    send_sems, recv_sems, local_sems = sems
    x, y, c, chips = _place()
    peers = [(x, y, 1 - c)] + [(*chip, c) for chip in chips]

    def copy(k, block, to, src):
        dst = outs[w].at[_block_of(*block)]
        return pltpu.make_async_remote_copy(src_ref=dst if src is None else src, dst_ref=dst, send_sem=send_sems.at[4 * w + k],
                                            recv_sem=recv_sems.at[4 * w + k], device_id=to, device_id_type=MESH)

    local = pltpu.make_async_copy(ins[w], outs[w].at[_block_of(x, y, c)], local_sems.at[w])
    sends = [copy(k, (x, y, c), peer, ins[w]) for k, peer in enumerate(peers)]
    recvs = [copy(k, peer, (x, y, c), None) for k, peer in enumerate(peers)]
    return local, sends, recvs


def _forward_copies(outs, sems, w):
    send_sems, recv_sems = sems
    x, y, c, chips = _place()

    def copy(j, block, to):
        dst = outs[w].at[_block_of(*block)]
        return pltpu.make_async_remote_copy(src_ref=dst, dst_ref=dst, send_sem=send_sems.at[3 * w + j],
                                            recv_sem=recv_sems.at[3 * w + j], device_id=to, device_id_type=MESH)

    sends = [copy(j, (*chip, c), (x, y, 1 - c)) for j, chip in enumerate(chips)]
    recvs = [copy(j, (*chip, 1 - c), (x, y, c)) for j, chip in enumerate(chips)]
    return sends, recvs


def _gather_start(name, shards):
    n = len(shards)

    def body(*refs):
        ins, outs, sems = refs[:n], refs[n:2 * n], refs[2 * n:2 * n + 3]
        for w in range(n):
            local, sends, _ = _gather_copies(ins, outs, sems, w)
            for cp in sends + [local]:
                cp.start()
        refs[-1][...] = jnp.zeros_like(refs[-1])

    lands = [lax.empty((N_DEV,) + s.shape, s.dtype) for s in shards]
    res = _chained(
        body, name=name, link=-1, in_specs=[HBM_SPEC] * (2 * n),
        out_specs=[SEM_SPEC] * 3 + [HBM_SPEC] * (2 * n) + [pl.BlockSpec(memory_space=pltpu.VMEM)],
        out_shape=[pltpu.SemaphoreType.DMA((4 * n,)), pltpu.SemaphoreType.DMA((4 * n,)), pltpu.SemaphoreType.DMA((n,))]
        + [pltpu.HBM(s.shape, s.dtype) for s in shards] + [pltpu.HBM(l.shape, l.dtype) for l in lands] + [TOKEN],
        input_output_aliases={i: 3 + i for i in range(2 * n)},
        compiler_params=pltpu.CompilerParams(has_side_effects=EFFECT),
    )(*[_in_hbm(s) for s in shards], *[_in_hbm(l) for l in lands])
    return n, res[:3], res[3:3 + n], res[3 + n:3 + 2 * n]


def _gather_mid(name, state):
    n, sems, shards, lands = state

    def body(*refs):
        ins, outs, sems_in = refs[:n], refs[n:2 * n], refs[2 * n:2 * n + 3]
        sems_out = refs[2 * n + 3:2 * n + 5]
        for w in range(n):
            local, sends, recvs = _gather_copies(ins, outs, sems_in, w)
            local.wait()
            for cp in sends:
                cp.wait_send()
            for cp in recvs:
                cp.wait_recv()
            for cp in _forward_copies(outs, sems_out, w)[0]:
                cp.start()
        refs[-1][...] = jnp.zeros_like(refs[-1])

    res = _chained(
        body, name=name, link=-1, in_specs=[HBM_SPEC] * (2 * n) + [SEM_SPEC] * 3,
        out_specs=[SEM_SPEC] * 2 + [HBM_SPEC] * n + [pl.BlockSpec(memory_space=pltpu.VMEM)],
        out_shape=[pltpu.SemaphoreType.DMA((3 * n,)), pltpu.SemaphoreType.DMA((3 * n,))]
        + [pltpu.HBM(l.shape, l.dtype) for l in lands] + [TOKEN],
        input_output_aliases={n + i: 2 + i for i in range(n)},
        compiler_params=pltpu.CompilerParams(has_side_effects=EFFECT),
    )(*shards, *lands, *sems)
    return n, res[:2], res[2:2 + n]


def _gather_end(name, state):
    n, sems, lands = state

    def body(*refs):
        outs, sems_in = refs[:n], refs[n:n + 2]
        for w in range(n):
            sends, recvs = _forward_copies(outs, sems_in, w)
            for cp in sends:
                cp.wait_send()
            for cp in recvs:
                cp.wait_recv()

    return _chained(
        body, name=name, in_specs=[HBM_SPEC] * n + [SEM_SPEC] * 2, out_specs=[HBM_SPEC] * n,
        out_shape=[pltpu.HBM(l.shape, l.dtype) for l in lands], input_output_aliases={i: i for i in range(n)},
        compiler_params=pltpu.CompilerParams(has_side_effects=EFFECT),
    )(*lands, *sems)


def _exchange_copies(ins, outs, sems, w):
    send_sems, recv_sems, local_sems = sems
    x, y, c, _ = _place()
    mine = _block_of(x, y, c)
    local = pltpu.make_async_copy(ins[w].at[mine], outs[w].at[mine], local_sems.at[w])
    remote = []
    for k in range(1, N_DEV):
        px, py, pc = x ^ (k >> 2), y ^ ((k >> 1) & 1), c ^ (k & 1)
        remote.append(pltpu.make_async_remote_copy(
            src_ref=ins[w].at[_block_of(px, py, pc)], dst_ref=outs[w].at[mine], send_sem=send_sems.at[7 * w + k - 1],
            recv_sem=recv_sems.at[7 * w + k - 1], device_id=(px, py, pc), device_id_type=MESH))
    return local, remote


def _exchange_start(name, stacks):
    n = len(stacks)

    def body(*refs):
        ins, outs, sems = refs[:n], refs[n:2 * n], refs[2 * n:2 * n + 3]
        for w in range(n):
            local, remote = _exchange_copies(ins, outs, sems, w)
            for cp in remote + [local]:
                cp.start()
        refs[-1][...] = jnp.zeros_like(refs[-1])

    lands = [lax.empty(s.shape, s.dtype) for s in stacks]
    if any(s is _Chain.last for s in stacks):
        _Chain.last = None
    res = _chained(
        body, name=name, link=-1, in_specs=[HBM_SPEC] * (2 * n),
        out_specs=[SEM_SPEC] * 3 + [HBM_SPEC] * (2 * n) + [pl.BlockSpec(memory_space=pltpu.VMEM)],
        out_shape=[pltpu.SemaphoreType.DMA((7 * n,)), pltpu.SemaphoreType.DMA((7 * n,)), pltpu.SemaphoreType.DMA((n,))]
        + [pltpu.HBM(s.shape, s.dtype) for s in stacks] * 2 + [TOKEN],
        input_output_aliases={i: 3 + i for i in range(2 * n)},
        compiler_params=pltpu.CompilerParams(has_side_effects=EFFECT),
    )(*[_in_hbm(s) for s in stacks], *[_in_hbm(l) for l in lands])
    return n, res[:3], res[3:3 + n], res[3 + n:3 + 2 * n]


def _exchange_wait(name, state):
    n, sems, stacks, lands = state

    def body(*refs):
        ins, outs, sems_in = refs[:n], refs[n:2 * n], refs[2 * n:2 * n + 3]
        for w in range(n):
            local, remote = _exchange_copies(ins, outs, sems_in, w)
            local.wait()
            for cp in remote:
                cp.wait_send()
                cp.wait_recv()

    return _chained(
        body, name=name, in_specs=[HBM_SPEC] * (2 * n) + [SEM_SPEC] * 3, out_specs=[HBM_SPEC] * n,
        out_shape=[pltpu.HBM(l.shape, l.dtype) for l in lands], input_output_aliases={n + i: i for i in range(n)},
        compiler_params=pltpu.CompilerParams(has_side_effects=EFFECT),
    )(*stacks, *lands, *sems)


N_CHIP = N_DEV // 2


def _pair_copies(stack, land, sems):
    send_sems, recv_sems, _ = sems
    x, y, c, _ = _place()
    remote = [pltpu.make_async_remote_copy(src_ref=stack.at[2 * k + 1 - c], dst_ref=land.at[k], send_sem=send_sems.at[k],
                                           recv_sem=recv_sems.at[k], device_id=(x, y, 1 - c), device_id_type=MESH)
              for k in range(N_CHIP)]
    return [], remote


def _chip_copies(pairs, land, sems):
    send_sems, recv_sems, local_sems = sems
    x, y, c, _ = _place()
    mine = 2 * x + y
    local = pltpu.make_async_copy(pairs.at[mine], land.at[mine], local_sems.at[0])
    remote = []
    for m in range(1, N_CHIP):
        px, py = x ^ (m >> 1), y ^ (m & 1)
        remote.append(pltpu.make_async_remote_copy(
            src_ref=pairs.at[2 * px + py], dst_ref=land.at[mine], send_sem=send_sems.at[m - 1],
            recv_sem=recv_sems.at[m - 1], device_id=(px, py, c), device_id_type=MESH))
    return local, remote


def _stage_start(name, copies_of, src_arr, n_land, n_sems):
    half = (N_CHIP,) + src_arr.shape[1:]

    def body(*refs):
        local, remote = copies_of(refs[0], *refs[1:1 + n_land], refs[1 + n_land:4 + n_land])
        for cp in remote + (local if isinstance(local, list) else [local]):
            cp.start()
        refs[-1][...] = jnp.zeros_like(refs[-1])

    if src_arr is _Chain.last:
        _Chain.last = None
    lands = [lax.empty(half, src_arr.dtype) for _ in range(n_land)]
    res = _chained(
        body, name=name, link=-1, in_specs=[HBM_SPEC] * (1 + n_land),
        out_specs=[SEM_SPEC] * 3 + [HBM_SPEC] * (1 + n_land) + [pl.BlockSpec(memory_space=pltpu.VMEM)],
        out_shape=[pltpu.SemaphoreType.DMA((n,)) for n in n_sems] + [pltpu.HBM(src_arr.shape, src_arr.dtype)]
        + [pltpu.HBM(half, src_arr.dtype)] * n_land + [TOKEN],
        input_output_aliases={i: 3 + i for i in range(1 + n_land)},
        compiler_params=pltpu.CompilerParams(has_side_effects=EFFECT),
    )(_in_hbm(src_arr), *[_in_hbm(l) for l in lands])
    return res[:3], res[3], res[4:4 + n_land]


def _stage_wait(name, copies_of, state):
    sems, src_arr, lands = state
    n_land = len(lands)

    def body(*refs):
        local, remote = copies_of(refs[0], *refs[1:1 + n_land], refs[1 + n_land:4 + n_land])
        for cp in (local if isinstance(local, list) else [local]):
            cp.wait()
        for cp in remote:
            cp.wait_send()
            cp.wait_recv()

    return _chained(
        body, name=name, in_specs=[HBM_SPEC] * (1 + n_land) + [SEM_SPEC] * 3, out_specs=[HBM_SPEC] * n_land,
        out_shape=[pltpu.HBM(l.shape, l.dtype) for l in lands],
        input_output_aliases={1 + i: i for i in range(n_land)},
        compiler_params=pltpu.CompilerParams(has_side_effects=EFFECT),
    )(src_arr, *lands, *sems)


def _pair_sum(name, stack, land, tm=64):
    n, rows, cols = land.shape
    tm = _tile(rows, tm)

    def body(c_ref, a_ref, b_ref, o_ref):
        o_ref[...] = (a_ref[...].astype(F32) + b_ref[...].astype(F32)).astype(o_ref.dtype)

    core = lax.axis_index("c").astype(jnp.int32).reshape(1)
    blk = pl.BlockSpec((n, tm, cols), lambda i, c_ref: (0, i, 0))
    mine = pl.BlockSpec((n, None, tm, cols), lambda i, c_ref: (0, c_ref[0], i, 0))
    _Chain.last = pl.pallas_call(
        body, name=name,
        grid_spec=pltpu.PrefetchScalarGridSpec(num_scalar_prefetch=1, grid=(rows // tm,), in_specs=[mine, blk], out_specs=blk),
        out_shape=jax.ShapeDtypeStruct(land.shape, land.dtype), compiler_params=_rowwise_params("parallel"),
    )(core, stack.reshape(n, 2, rows, cols), land)
    return _Chain.last


def _unstack_cols(w):
    return w.transpose(1, 0, 2).reshape(w.shape[1], N_DEV * w.shape[2])


def _stack_cols(w):
    return w.reshape(w.shape[0], N_DEV, w.shape[1] // N_DEV).transpose(1, 0, 2)


def _rope_tables(positions):
    half = MLA_ROPE // 2
    inv_freq = ROPE_THETA ** (-jnp.arange(half, dtype=F32) / half)
    ang = positions.astype(F32)[:, None] * inv_freq
    cos, sin = jnp.cos(ang), jnp.sin(ang)
    t = positions.shape[0]
    cos_t = jnp.concatenate([jnp.ones((t, MLA_NOPE), F32), cos, cos], axis=1)
    sin_t = jnp.concatenate([jnp.zeros((t, MLA_NOPE), F32), -sin, sin], axis=1)
    idx = jnp.arange(MLA_QK)
    partner = jnp.where(idx < MLA_NOPE, -1, jnp.where(idx < MLA_NOPE + half, idx + half, idx - half))
    swap = (idx[:, None] == partner[None, :]).astype(BF16)
    return cos_t, sin_t, swap


def _ffn_fwd(tag, x, gain, wt):
    h = _rms_fwd(tag + "_rms", x, gain, BF16)
    g = _mm_stack_nt_out(tag + "_gate", h, wt[tag + '_w_gate'], BF16)
    u, a = _mm_stack_nt_out(tag + "_up", h, wt[tag + '_w_up'], BF16,
                            fuse=(lambda u_blk, g_blk: (u_blk, g_blk * jax.nn.sigmoid(g_blk) * u_blk), [g]))
    y = _mm_stack_sum(tag + "_down", a, wt[tag + '_w_down'], F32, scale=0.5, res=x, tm=1024)
    return y, (x, h, g, u, a)


def _ffn_bwd_weights(tag, dy, saved, wd, comm):
    x, h, g, u, a = saved
    comm.grads({tag + '_w_down': _mm_stack_tn_right(tag + "_dwd", a, dy, BF16, scale=0.5)})

    def dact(da, g_blk, u_blk):
        sig = jax.nn.sigmoid(g_blk)
        return da * u_blk * sig * (1.0 + g_blk * (1.0 - sig)), da * g_blk * sig

    dg, du = _mm_stack_nt_out(tag + "_da", dy, wd, BF16, scale=0.5, fuse=(dact, [g, u]))
    comm.advance()
    comm.grads({tag + '_w_gate': _mm_stack_tn_right(tag + "_dwg", dg, h, BF16)})
    dwu = _mm_stack_tn_right(tag + "_dwu", du, h, BF16)
    comm.advance()
    comm.grads({tag + '_w_up': dwu})
    return dg, du


def _ffn_bwd_input(tag, dy, saved, dgu, gain, wg, wu, comm):
    dg, du = dgu
    dh = _mm_stack_sum(tag + "_dh", (dg, du), (wg, wu), F32)
    comm.advance()
    return _rms_bwd(tag + "_drms", saved[0], gain, dh, res=dy)


def _local_step(x, mem, positions, sm, comm, target):
    t, d = x.shape
    nm = mem.shape[0]
    gs, gw = {}, {}

    class Weights(dict):
        def __missing__(self, name):
            self.update(comm.weights(next(k for k, names in enumerate(GATHERS) if name in names)))
            return self[name]

    wt = Weights()

    x1, ffn1_saved = _ffn_fwd("ffn1", x, sm['ffn1_norm'], wt)

    h2 = _rms_fwd("mix_rms", x1, sm['mix_norm'], BF16)
    w_in_ref = _unstack_cols(wt['w_in'])
    pieces = []
    for name, n in IN_PAD:
        piece = w_in_ref[:, REF_OFF[name]:REF_OFF[name] + REF_SIZE[name]]
        if n != REF_SIZE[name]:
            piece = jnp.pad(piece, ((0, 0), (0, n - REF_SIZE[name])))
        pieces.append(piece)
    w_in = jnp.concatenate(pieces, axis=1)
    z = _mm2("mix_in", h2, w_in, NN, F32)
    zs = {name: z[:, PAD_OFF[name]:PAD_OFF[name] + n] for name, n in IN_PAD}

    cq = _rms_fwd("mla_q_a_rms", zs['zq'], sm['q_a_norm'], BF16)
    q_raw = _mm_stack_nt_out("mla_q_up", cq, wt['w_q_up'], F32)
    ckv = _rms_fwd("mla_kv_a_rms", zs['zkv'], sm['kv_a_norm'], BF16)
    kv = _mm_stack_out("mla_kv_up", ckv, wt['w_kv_up'], F32)
    zkr = zs['zkr'][:, :MLA_ROPE]
    k_raw = jnp.concatenate([kv[:, :, :MLA_NOPE], jnp.broadcast_to(zkr[None], (MLA_HEADS, t, MLA_ROPE))], axis=2)
    v_mla = kv[:, :, MLA_NOPE:].astype(BF16)
    cos_t, sin_t, swap = _rope_tables(positions)
    q_raw2, k_raw2 = q_raw.reshape(MLA_HEADS * t, MLA_QK), k_raw.reshape(MLA_HEADS * t, MLA_QK)
    rope = (cos_t, sin_t, swap)
    qf = _rms_fwd("mla_q_rms", q_raw2, sm['mla_q_norm'], BF16, rope=rope, scale=MLA_QK ** -0.5).reshape(MLA_HEADS, t, MLA_QK)
    kf = _rms_fwd("mla_k_rms", k_raw2, sm['mla_k_norm'], BF16, rope=rope).reshape(MLA_HEADS, t, MLA_QK)
    o_mla = _attn_fwd("mla_attn", qf, kf, v_mla, True)

    w_g2 = jnp.pad(_unstack_cols(wt['gla_w_gate2']), ((0, LANES - GLA_GATE_RANK), (0, 0)))
    pre = _mm2("gla_gate_pre", zs['zg'], w_g2, NN, F32)
    log_a = _gate_fwd("gla_gate", pre, sm['gla_b_gate'])
    o_gla_raw, states = _gla_fwd("gla_scan", zs['gq'], zs['gk'], zs['gv'], log_a, GLA_HEADS)
    o_gla_n = _rms_fwd("gla_out_rms", o_gla_raw, sm['gla_out_norm'], F32)
    o_gla = _swiglu_fwd("gla_out_gate", zs['zr'], o_gla_n, BF16)

    cat = jnp.concatenate([o_mla, o_gla], axis=1)
    w_out = wt['w_out'].reshape(d, d)
    x2 = _mm2("mix_out", cat, w_out, NN, F32, res=x1)

    w_mq, w_mk, w_mv = (wt[n].reshape(d, MEM_HEADS * MEM_HEAD_DIM) for n in ('mem_w_q', 'mem_w_k', 'mem_w_v'))
    hq = _rms_fwd("mem_attn_rms", x2, sm['mem_attn_norm'], BF16)
    hm = _rms_fwd("mem_rms", mem, sm['mem_norm'], BF16)

    def heads_out(name, a, b, out_dtype):
        m, kk = a.shape
        tm = _tile(m, 512)
        return _mm(name, a, b, (m // tm, MEM_HEADS, 1), ((tm, kk), lambda i, h, k: (i, 0)),
                   ((kk, MEM_HEAD_DIM), lambda i, h, k: (0, h)), ((None, tm, MEM_HEAD_DIM), lambda i, h, k: (h, i, 0)),
                   (MEM_HEADS, m, MEM_HEAD_DIM), out_dtype, NN)

    mq_raw = heads_out("mem_q", hq, w_mq, F32)
    mk_raw = heads_out("mem_k", hm, w_mk, F32)
    mv = heads_out("mem_v", hm, w_mv, BF16)
    mq = _rms_fwd("mem_q_rms", mq_raw.reshape(MEM_HEADS * t, MEM_HEAD_DIM), sm['mem_q_norm'], BF16, scale=MEM_HEAD_DIM ** -0.5)
    mk = _rms_fwd("mem_k_rms", mk_raw.reshape(MEM_HEADS * nm, MEM_HEAD_DIM), sm['mem_k_norm'], BF16)
    mq, mk = mq.reshape(MEM_HEADS, t, MEM_HEAD_DIM), mk.reshape(MEM_HEADS, nm, MEM_HEAD_DIM)
    o_mem = _attn_fwd("mem_attn", mq, mk, mv, False)
    w_mo = wt['mem_w_o']
    mo_cols = w_mo.shape[2]
    tm = _tile(t, 512)
    x3 = _mm("mem_out", o_mem, w_mo, (t // tm, N_DEV, 1), ((tm, o_mem.shape[1]), lambda i, j, k: (i, 0)),
             ((None, o_mem.shape[1], mo_cols), lambda i, j, k: (j, 0, 0)), ((tm, mo_cols), lambda i, j, k: (i, j)),
             (t, d), F32, NN, res=x2)

    y, ffn2_saved = _ffn_fwd("ffn2", x3, sm['ffn2_norm'], wt)
    dy, dy_narrow, loss_lanes = _loss("loss", y, target)

    dgu = _ffn_bwd_weights("ffn2", dy_narrow, ffn2_saved, wt['ffn2_w_down'], comm)
    dx3, gs['ffn2_norm'] = _ffn_bwd_input("ffn2", dy, ffn2_saved, dgu, sm['ffn2_norm'],
                                          wt['ffn2_w_gate'], wt['ffn2_w_up'], comm)

    do_mem = _mm("mem_do", dx3, w_mo, (t // tm, 1, N_DEV), ((tm, mo_cols), lambda i, h, k: (i, k)),
                 ((None, w_mo.shape[1], mo_cols), lambda i, h, k: (k, 0, 0)), ((tm, w_mo.shape[1]), lambda i, h, k: (i, 0)),
                 (t, MEM_HEADS * MEM_HEAD_DIM), BF16, NT)
    tk = _tile(t, 512)
    gw['mem_w_o'] = _mm("mem_dwo", o_mem, dx3, (N_DEV, 1, t // tk), ((tk, o_mem.shape[1]), lambda j, i, k: (k, 0)),
                        ((tk, mo_cols), lambda j, i, k: (k, j)), ((None, o_mem.shape[1], mo_cols), lambda j, i, k: (j, 0, 0)),
                        w_mo.shape, BF16, TN)
    dmq, dmk, dmv = _attn_bwd("mem_dattn", mq, mk, mv, do_mem, False)
    dmq_raw, gs['mem_q_norm'] = _rms_bwd("mem_q_drms", mq_raw.reshape(MEM_HEADS * t, MEM_HEAD_DIM), sm['mem_q_norm'],
                                         dmq.reshape(MEM_HEADS * t, MEM_HEAD_DIM), scale=MEM_HEAD_DIM ** -0.5)
    dmk_raw, gs['mem_k_norm'] = _rms_bwd("mem_k_drms", mk_raw.reshape(MEM_HEADS * nm, MEM_HEAD_DIM), sm['mem_k_norm'],
                                         dmk.reshape(MEM_HEADS * nm, MEM_HEAD_DIM))
    dmq_raw = dmq_raw.reshape(MEM_HEADS, t, MEM_HEAD_DIM)
    dmk_raw = dmk_raw.reshape(MEM_HEADS, nm, MEM_HEAD_DIM)

    def heads_in_nt(name, a, b, res=None):
        m, n = a.shape[1], b.shape[0]
        tm_, tn_ = _tile(m, 512), _tile(n, 1024)
        return _mm(name, a, b, (m // tm_, n // tn_, MEM_HEADS), ((None, tm_, MEM_HEAD_DIM), lambda i, j, k: (k, i, 0)),
                   ((tn_, MEM_HEAD_DIM), lambda i, j, k: (j, k)), ((tm_, tn_), lambda i, j, k: (i, j)), (m, n), F32, NT,
                   None, res)

    def heads_tn(name, a, b):
        m, kp = a.shape
        tm_, tk_ = _tile(kp, 1024), _tile(m, 512)
        return _mm(name, a, b, (kp // tm_, MEM_HEADS, m // tk_), ((tk_, tm_), lambda i, h, k: (k, i)),
                   ((None, tk_, MEM_HEAD_DIM), lambda i, h, k: (h, k, 0)), ((tm_, MEM_HEAD_DIM), lambda i, h, k: (i, h)),
                   (kp, MEM_HEADS * MEM_HEAD_DIM), BF16, TN)

    dhq = heads_in_nt("mem_dhq", dmq_raw, w_mq)
    gw['mem_w_q'] = heads_tn("mem_dwq", hq, dmq_raw).reshape(wt['mem_w_q'].shape)
    dhm = heads_in_nt("mem_dhm_k", dmk_raw, w_mk)
    dhm = heads_in_nt("mem_dhm_v", dmv, w_mv, res=dhm)
    gw['mem_w_k'] = heads_tn("mem_dwk", hm, dmk_raw).reshape(wt['mem_w_k'].shape)
    gw['mem_w_v'] = heads_tn("mem_dwv", hm, dmv).reshape(wt['mem_w_v'].shape)
    _, gs['mem_norm'] = _rms_bwd("mem_drms", mem, sm['mem_norm'], dhm)
    comm.grads({n: gw[n] for n in MEMORY})
    dx2, gs['mem_attn_norm'] = _rms_bwd("mem_attn_drms", x2, sm['mem_attn_norm'], dhq, res=dx3)

    dcat = _mm2("mix_dcat", dx2, w_out, NT, F32)
    comm.grads({'w_out': _mm2("mix_dwout", cat, dx2, TN, BF16, tm=1024, tn=2048, tk=512).reshape(wt['w_out'].shape)})
    do_mla, do_gla = dcat[:, :MLA_HEADS * MLA_V], dcat[:, MLA_HEADS * MLA_V:]

    dzr, dgn = _swiglu_bwd("gla_out_dgate", do_gla, zs['zr'], o_gla_n, F32)
    do_gla_raw, gs['gla_out_norm'] = _rms_bwd("gla_out_drms", o_gla_raw, sm['gla_out_norm'], dgn)
    dgq, dgk, dgv, dlog_a = _gla_bwd("gla_dscan", zs['gq'], zs['gk'], zs['gv'], log_a, states, do_gla_raw, GLA_HEADS)
    dpre, gs['gla_b_gate'] = _gate_bwd("gla_dgate", pre, sm['gla_b_gate'], dlog_a)
    dw_g2 = _mm2("gla_dwgate", zs['zg'], dpre, TN, BF16, tk=512)
    comm.grads({'gla_w_gate2': _stack_cols(dw_g2[:GLA_GATE_RANK])})
    dzg = _mm2("gla_dzg", dpre, w_g2, NT, F32)

    dqf, dkf, dv_mla = _attn_bwd("mla_dattn", qf, kf, v_mla, do_mla, True)
    dq_raw, gs['mla_q_norm'] = _rms_bwd("mla_q_drms", q_raw2, sm['mla_q_norm'], dqf.reshape(MLA_HEADS * t, MLA_QK), rope=rope,
                                         scale=MLA_QK ** -0.5)
    dk_raw, gs['mla_k_norm'] = _rms_bwd("mla_k_drms", k_raw2, sm['mla_k_norm'], dkf.reshape(MLA_HEADS * t, MLA_QK), rope=rope)
    dq_raw = dq_raw.reshape(MLA_HEADS, t, MLA_QK)
    dk_raw = dk_raw.reshape(MLA_HEADS, t, MLA_QK)
    dkv = jnp.concatenate([dk_raw[:, :, :MLA_NOPE], dv_mla], axis=2)
    dzkr = jnp.sum(dk_raw[:, :, MLA_NOPE:], axis=0)
    comm.grads({'w_q_up': _mm_stack_tn_right("mla_dwq", dq_raw, cq, BF16),
                'w_kv_up': _mm_stack_tn_left("mla_dwkv", ckv, dkv, BF16)})
    dcq = _mm_stack_sum("mla_dcq", dq_raw, wt['w_q_up'], F32)
    dckv = _mm_stack_nt_sum("mla_dckv", dkv, wt['w_kv_up'], F32)
    dzq, gs['q_a_norm'] = _rms_bwd("mla_q_a_drms", zs['zq'], sm['q_a_norm'], dcq)
    dzkv, gs['kv_a_norm'] = _rms_bwd("mla_kv_a_drms", zs['zkv'], sm['kv_a_norm'], dckv)

    dzs = {'zq': dzq, 'zkv': dzkv, 'gq': dgq, 'gk': dgk, 'gv': dgv, 'zr': dzr,
           'zkr': jnp.pad(dzkr, ((0, 0), (0, LANES - MLA_ROPE))), 'zg': dzg}
    dz = jnp.concatenate([dzs[name].astype(BF16) for name, _ in IN_PAD], axis=1)
    dw_in = _mm2("mix_dwin", h2, dz, TN, BF16, tm=1024, tn=2048, tk=512)
    dw_in_ref = jnp.concatenate([dw_in[:, PAD_OFF[name]:PAD_OFF[name] + n] for name, n in IN_REF], axis=1)
    comm.grads({'w_in': _stack_cols(dw_in_ref)})
    dh2 = _mm2("mix_dh", dz, w_in, NT, F32)
    comm.advance()
    dx1, dx1_narrow, gs['mix_norm'] = _rms_bwd("mix_drms", x1, sm['mix_norm'], dh2, res=dx2, narrow=True)

    dgu = _ffn_bwd_weights("ffn1", dx1_narrow, ffn1_saved, wt['ffn1_w_down'], comm)
    grad_x, gs['ffn1_norm'] = _ffn_bwd_input("ffn1", dx1, ffn1_saved, dgu, sm['ffn1_norm'],
                                             wt['ffn1_w_gate'], wt['ffn1_w_up'], comm)
    return loss_lanes, grad_x, gs


def _pad_lanes(v):
    n = v.shape[1]
    return jnp.pad(v, ((0, 0), (0, -n % LANES)))


def _pack_small(vals):
    return jnp.concatenate([_pad_lanes(vals[n]) for n in SMALL], axis=1)


def _unpack_small(packed, like):
    out, off = {}, 0
    for n in SMALL:
        size = like[n].shape[1]
        out[n] = packed[:, off:off + size]
        off += size + (-size % LANES)
    return out


def kernel(x, mem, positions, ffn1_norm, ffn1_w_gate, ffn1_w_up, ffn1_w_down, mix_norm, w_in, q_a_norm, w_q_up, kv_a_norm, w_kv_up, mla_q_norm, mla_k_norm, gla_w_gate2, gla_b_gate, gla_out_norm, w_out, mem_attn_norm, mem_norm, mem_w_q, mem_w_k, mem_w_v, mem_w_o, mem_q_norm, mem_k_norm, ffn2_norm, ffn2_w_gate, ffn2_w_up, ffn2_w_down, loss_target, m_ffn1_norm, m_ffn1_w_gate, m_ffn1_w_up, m_ffn1_w_down, m_mix_norm, m_w_in, m_q_a_norm, m_w_q_up, m_kv_a_norm, m_w_kv_up, m_mla_q_norm, m_mla_k_norm, m_gla_w_gate2, m_gla_b_gate, m_gla_out_norm, m_w_out, m_mem_attn_norm, m_mem_norm, m_mem_w_q, m_mem_w_k, m_mem_w_v, m_mem_w_o, m_mem_q_norm, m_mem_k_norm, m_ffn2_norm, m_ffn2_w_gate, m_ffn2_w_up, m_ffn2_w_down, v_ffn1_norm, v_ffn1_w_gate, v_ffn1_w_up, v_ffn1_w_down, v_mix_norm, v_w_in, v_q_a_norm, v_w_q_up, v_kv_a_norm, v_w_kv_up, v_mla_q_norm, v_mla_k_norm, v_gla_w_gate2, v_gla_b_gate, v_gla_out_norm, v_w_out, v_mem_attn_norm, v_mem_norm, v_mem_w_q, v_mem_w_k, v_mem_w_v, v_mem_w_o, v_mem_q_norm, v_mem_k_norm, v_ffn2_norm, v_ffn2_w_gate, v_ffn2_w_up, v_ffn2_w_down):
    inp = dict(locals())
    x, mem, positions, target = inp['x'][0], inp['mem'][0], inp['positions'][0], inp['loss_target'][0]
    sm = {n: inp[n] for n in SMALL}
    out = {}

    def stored(key):
        name = key[2:] if key[:2] in ('m_', 'v_') else key
        return inp[key][0].T if name in TRANSPOSED else inp[key][0]

    def as_given(name, r):
        return r.T[None] if name in TRANSPOSED else r[None]

    class Comm:
        def __init__(self):
            self.gathers = {0: self.start(0)}
            self.forwards, self.exchanges, self.pairs = {}, [], []

        def start(self, k):
            return _gather_start(f"gather_start_{k}", [stored(n).astype(BF16) for n in GATHERS[k]])

        def forward(self, k):
            if k not in self.forwards:
                self.forwards[k] = _gather_mid(f"gather_mid_{k}", self.gathers[k])
                self.gathers.update({nxt: self.start(nxt) for nxt in NEXT_GATHERS.get(k, [])})

        def weights(self, k):
            self.forward(k)
            if k in EARLY_FORWARD:
                self.forward(EARLY_FORWARD[k])
            return dict(zip(GATHERS[k], _gather_end(f"gather_end_{k}", self.forwards[k])))

        def grads(self, stacks):
            names = list(stacks)
            if names[0] in TWO_STAGE:
                (n,) = names
                self.pairs.append((n, _stage_start("pair_start_" + n, _pair_copies, stacks[n], 1, (N_CHIP, N_CHIP, 1))))
            else:
                self.exchanges.append((names, _exchange_start("exchange_start_" + names[0], [stacks[n] for n in names])))

        def advance(self):
            for n, state in self.pairs:
                (land,) = _stage_wait("pair_wait_" + n, _pair_copies, state)
                pairs = _pair_sum("pair_sum_" + n, state[1], land)
                self.exchanges.append(([n], _stage_start("chip_start_" + n, _chip_copies, pairs, 1, (N_CHIP - 1, N_CHIP - 1, 1))))
            self.pairs = []

        def update(self, count):
            todo, self.exchanges = self.exchanges[:count], self.exchanges[count:]
            for names, state in todo:
                if names[0] in TWO_STAGE:
                    parts = _stage_wait("chip_wait_" + names[0], _chip_copies, state)
                else:
                    parts = _exchange_wait("exchange_wait_" + names[0], state)
                for n, p in zip(names, parts):
                    res = _adamw("adamw_" + n, stored(n), stored('m_' + n), stored('v_' + n), p)
                    for kind, r in zip(('grad_', 'delta_', 'new_m_', 'new_v_'), res):
                        out[kind + n] = as_given(n, r)

    _Chain.last = None
    comm = Comm()
    loss_lanes, grad_x, gs = _local_step(x, mem, positions, sm, comm, target)
    out['loss'] = lax.psum(jnp.sum(loss_lanes), ("x", "y", "c"))
    out['grad_x'] = grad_x[None]

    comm.update(len(comm.exchanges) - 3)
    small_parts = _all_gather("gather_small", [_pack_small(gs)])[0]
    res = _adamw("adamw_small", _pack_small(sm), _pack_small({n: inp['m_' + n] for n in SMALL}),
                 _pack_small({n: inp['v_' + n] for n in SMALL}), small_parts)
    for kind, r in zip(('grad_', 'delta_', 'new_m_', 'new_v_'), res):
        for n, val in _unpack_small(r, sm).items():
            out[kind + n] = val

    comm.update(3)

    names = ['loss', 'grad_x'] + [k + n for k in ('grad_', 'delta_', 'new_m_', 'new_v_') for n in WEIGHTS]
    return tuple(out[n] for n in names)
```

```python
import functools
import math

import jax
import jax.numpy as jnp
from jax import lax
from jax.experimental import pallas as pl
from jax.experimental.pallas import tpu as pltpu

F32 = jnp.float32
BF16 = jnp.bfloat16

N_DEV = 8
EPS = 1e-6
CHUNK = 64
MLA_HEADS, MLA_NOPE, MLA_ROPE, MLA_V = 8, 128, 64, 128
MLA_QK = MLA_NOPE + MLA_ROPE
MLA_Q_RANK, MLA_KV_RANK = 512, 256
ROPE_THETA = 10000.0
GLA_HEADS, GLA_DK, GLA_DV, GLA_GATE_RANK = 4, 128, 256, 16
GLA_TAU = 16.0
MEM_HEADS, MEM_HEAD_DIM = 4, 128
ADAM_LR, ADAM_B1, ADAM_B2, ADAM_EPS, ADAM_WD, ADAM_STEP = 0.001, 0.9, 0.999, 1e-08, 0.01, 10

V7X_VMEM_BYTES = 64 * 1024 * 1024
LANES = 128

NN = (((1,), (0,)), ((), ()))
NT = (((1,), (1,)), ((), ()))
TN = (((0,), (0,)), ((), ()))

WEIGHTS = ['ffn1_norm', 'ffn1_w_gate', 'ffn1_w_up', 'ffn1_w_down', 'mix_norm', 'w_in', 'q_a_norm', 'w_q_up',
           'kv_a_norm', 'w_kv_up', 'mla_q_norm', 'mla_k_norm', 'gla_w_gate2', 'gla_b_gate', 'gla_out_norm', 'w_out',
           'mem_attn_norm', 'mem_norm', 'mem_w_q', 'mem_w_k', 'mem_w_v', 'mem_w_o', 'mem_q_norm', 'mem_k_norm',
           'ffn2_norm', 'ffn2_w_gate', 'ffn2_w_up', 'ffn2_w_down']
SMALL = ['ffn1_norm', 'mix_norm', 'q_a_norm', 'kv_a_norm', 'mla_q_norm', 'mla_k_norm', 'gla_b_gate', 'gla_out_norm',
         'mem_attn_norm', 'mem_norm', 'mem_q_norm', 'mem_k_norm', 'ffn2_norm']
MIXER = ['w_in', 'w_q_up', 'w_kv_up', 'gla_w_gate2', 'w_out']
MEMORY = ['mem_w_q', 'mem_w_k', 'mem_w_v', 'mem_w_o']
TRANSPOSED = ['ffn1_w_gate', 'ffn1_w_up', 'ffn2_w_gate', 'ffn2_w_up', 'w_q_up']
TWO_STAGE = ['w_in', 'ffn1_w_down', 'ffn1_w_gate', 'ffn1_w_up']
GATHERS = [['ffn1_w_gate'], ['ffn1_w_up'], ['ffn1_w_down'], MIXER[:1], MIXER[1:], MEMORY,
           ['ffn2_w_gate', 'ffn2_w_up', 'ffn2_w_down']]
NEXT_GATHERS = {0: [1], 1: [2], 2: [3], 3: [4, 5, 6]}
EARLY_FORWARD = {5: 6}

IN_REF = [('zq', 512), ('zkv', 256), ('zkr', 64), ('gq', 512), ('gk', 512), ('gv', 1024), ('zg', 16), ('zr', 1024)]
IN_PAD = [('zq', 512), ('zkv', 256), ('gq', 512), ('gk', 512), ('gv', 1024), ('zr', 1024), ('zkr', 128), ('zg', 128)]
IN_WIDTH = sum(n for _, n in IN_REF)
IN_PAD_WIDTH = sum(n for _, n in IN_PAD)


def _offsets(layout):
    out, off = {}, 0
    for name, n in layout:
        out[name] = off
        off += n
    return out


REF_OFF, PAD_OFF = _offsets(IN_REF), _offsets(IN_PAD)
REF_SIZE = dict(IN_REF)


def _tile(n, pref):
    return pref if n % pref == 0 else n


def _block_bytes(blk, dtype):
    dims = [d for d in blk if d is not None]
    if len(dims) >= 1:
        dims[-1] = -(-dims[-1] // LANES) * LANES
    return math.prod(dims) * jnp.dtype(dtype).itemsize


def _vmem_limit(pipelined_bytes, resident_bytes=0):
    need = 2 * pipelined_bytes + resident_bytes + (8 << 20)
    return int(min(max(need, 32 << 20), V7X_VMEM_BYTES - (6 << 20)))


class _Chain:
    last = None


def _chained(body, *, in_specs, link=0, **kwargs):
    def call(*operands):
        dep = _Chain.last
        if dep is not None and any(o is dep for o in operands):
            dep = None
        if dep is None:
            res = pl.pallas_call(body, in_specs=in_specs, **kwargs)(*operands)
        else:
            n = len(operands)

            def chained_body(*refs):
                body(*refs[:n], *refs[n + 1:])

            res = pl.pallas_call(chained_body, in_specs=list(in_specs) + [pl.BlockSpec(memory_space=pl.ANY)],
                                 **kwargs)(*operands, dep)
        _Chain.last = res[link] if isinstance(res, (list, tuple)) else res
        return res

    return call


def _row_tile(rows, width, elements=1 << 18):
    lanes = -(-width // LANES) * LANES
    return _tile(rows, max(256, 1 << int(math.log2(max(1, elements // lanes)))))


def _streamed(*arrays):
    return [pltpu.with_memory_space_constraint(a, pltpu.HBM) for a in arrays]


def _rowwise_params(*semantics):
    return pltpu.CompilerParams(dimension_semantics=semantics, vmem_limit_bytes=48 << 20)


def _mm(name, a, b, grid, a_spec, b_spec, o_spec, out_shape, out_dtype, dims, scale=None, res=None, fuse=None):
    nk = grid[2]
    o_blk, o_map = o_spec
    acc_shape = tuple(d for d in o_blk if d is not None)
    extras = [res] if res is not None else (list(fuse[1]) if fuse else [])
    n_out = 2 if fuse else 1
    a_list, b_list = (list(a), list(b)) if isinstance(a, (tuple, list)) else ([a], [b])
    a_specs = a_spec if isinstance(a_spec, list) else [a_spec] * len(a_list)
    b_specs = b_spec if isinstance(b_spec, list) else [b_spec] * len(b_list)
    n_in = 2 * len(a_list)

    def body(*refs):
        a_refs, b_refs = refs[:n_in // 2], refs[n_in // 2:n_in]
        extra_refs = refs[n_in:n_in + len(extras)]
        out_refs = refs[n_in + len(extras):n_in + len(extras) + n_out]
        rest = refs[n_in + len(extras) + n_out:]
        r_ref = extra_refs[0] if res is not None else None
        o_ref = out_refs[0]

        def product():
            return sum(lax.dot_general(a_ref[...].astype(BF16), b_ref[...].astype(BF16), dims, preferred_element_type=F32)
                       for a_ref, b_ref in zip(a_refs, b_refs))

        def finish(r):
            if scale is not None:
                r = r * scale
            if r_ref is not None:
                r = r + r_ref[...].astype(F32)
            if fuse:
                for ref, val in zip(out_refs, fuse[0](r, *[e[...].astype(F32) for e in extra_refs])):
                    ref[...] = val.astype(ref.dtype)
            else:
                o_ref[...] = r.astype(o_ref.dtype)

        if nk == 1:
            finish(product())
        else:
            acc = rest[0]
            k = pl.program_id(2)

            @pl.when(k == 0)
            def _():
                acc[...] = product()

            @pl.when(k > 0)
            def _():
                acc[...] += product()

            @pl.when(k == nk - 1)
            def _():
                finish(acc[...])

    in_specs = [pl.BlockSpec(*spec) for spec in a_specs + b_specs]
    operands = a_list + b_list
    piped = sum(_block_bytes(spec[0], v.dtype) for spec, v in zip(a_specs + b_specs, operands))
    piped += _block_bytes(o_blk, out_dtype)
    for extra in extras:
        in_specs.append(pl.BlockSpec(o_blk, o_map))
        operands.append(extra)
        piped += _block_bytes(o_blk, extra.dtype)
    piped += (n_out - 1) * _block_bytes(o_blk, out_dtype)
    scratch = [pltpu.VMEM(acc_shape, F32)] if nk > 1 else []
    out_spec, out_sds = pl.BlockSpec(o_blk, o_map), jax.ShapeDtypeStruct(out_shape, out_dtype)
    return _chained(
        body, name=name, grid=grid, in_specs=in_specs, out_specs=(out_spec,) * n_out if fuse else out_spec,
        out_shape=(out_sds,) * n_out if fuse else out_sds, scratch_shapes=scratch,
        compiler_params=pltpu.CompilerParams(
            dimension_semantics=("parallel", "parallel", "arbitrary"),
            vmem_limit_bytes=_vmem_limit(piped, 3 * _block_bytes(acc_shape, F32))),
    )(*operands)


def _mm2(name, a, b, dims, out_dtype, tm=512, tn=1024, tk=2048, scale=None, res=None):
    if dims is NN:
        (m, kk), n = a.shape, b.shape[1]
    elif dims is NT:
        (m, kk), n = a.shape, b.shape[0]
    else:
        (kk, m), n = a.shape, b.shape[1]
    tm, tn, tk = _tile(m, tm), _tile(n, tn), _tile(kk, tk)
    a_spec = ((tk, tm), lambda i, j, k: (k, i)) if dims is TN else ((tm, tk), lambda i, j, k: (i, k))
    b_spec = ((tn, tk), lambda i, j, k: (j, k)) if dims is NT else ((tk, tn), lambda i, j, k: (k, j))
    return _mm(name, a, b, (m // tm, n // tn, kk // tk), a_spec, b_spec, ((tm, tn), lambda i, j, k: (i, j)),
               (m, n), out_dtype, dims, scale, res)


def _mm_stack_out(name, a, b, out_dtype, tm=512, tk=2048):
    (m, kk), (nj, _, n) = a.shape, b.shape
    tm, tk = _tile(m, tm), _tile(kk, tk)
    return _mm(name, a, b, (nj, m // tm, kk // tk), ((tm, tk), lambda j, i, k: (i, k)),
               ((None, tk, n), lambda j, i, k: (j, k, 0)), ((None, tm, n), lambda j, i, k: (j, i, 0)),
               (nj, m, n), out_dtype, NN)


def _mm_stack_nt_out(name, a, b, out_dtype, scale=None, fuse=None, tm=1024, tk=2048):
    (m, kk), (nj, n, _) = a.shape, b.shape
    tm, tk = _tile(m, tm), _tile(kk, tk)
    return _mm(name, a, b, (nj, m // tm, kk // tk), ((tm, tk), lambda j, i, k: (i, k)),
               ((None, n, tk), lambda j, i, k: (j, 0, k)), ((None, tm, n), lambda j, i, k: (j, i, 0)),
               (nj, m, n), out_dtype, NT, scale, fuse=fuse)


def _mm_stack_sum(name, a, b, out_dtype, scale=None, res=None, tm=512, tn=1024):
    (nj, m, f), n = (a[0] if isinstance(a, tuple) else a).shape, (b[0] if isinstance(b, tuple) else b).shape[2]
    tm, tn = _tile(m, tm), _tile(n, tn)
    pairs = list(zip(a, b)) if isinstance(a, tuple) else [(a, b)]
    step = 2 if nj % 2 == 0 else 1
    a_list, b_list, a_spec, b_spec = [], [], [], []
    for a_k, b_k in pairs:
        for s in range(step):
            a_list.append(a_k)
            b_list.append(b_k)
            a_spec.append(((None, tm, f), lambda i, j, k, s=s: (step * k + s, i, 0)))
            b_spec.append(((None, f, tn), lambda i, j, k, s=s: (step * k + s, 0, j)))
    return _mm(name, a_list, b_list, (m // tm, n // tn, nj // step), a_spec, b_spec, ((tm, tn), lambda i, j, k: (i, j)),
               (m, n), out_dtype, NN, scale, res)


def _mm_stack_nt_sum(name, a, b, out_dtype, res=None, tm=512, tn=1024):
    (nj, m, f), n = a.shape, b.shape[1]
    tm, tn = _tile(m, tm), _tile(n, tn)
    return _mm(name, a, b, (m // tm, n // tn, nj), ((None, tm, f), lambda i, j, k: (k, i, 0)),
               ((None, tn, f), lambda i, j, k: (k, j, 0)), ((tm, tn), lambda i, j, k: (i, j)),
               (m, n), out_dtype, NT, None, res)


def _mm_stack_tn_left(name, a, b, out_dtype, tm=1024, tk=512):
    (m, kp), (nj, _, n) = a.shape, b.shape
    tm, tk = _tile(kp, tm), _tile(m, tk)
    return _mm(name, a, b, (nj, kp // tm, m // tk), ((tk, tm), lambda j, i, k: (k, i)),
               ((None, tk, n), lambda j, i, k: (j, k, 0)), ((None, tm, n), lambda j, i, k: (j, i, 0)),
               (nj, kp, n), out_dtype, TN)


def _mm_stack_tn_right(name, a, b, out_dtype, scale=None, tn=2048, tk=1024):
    (nj, m, f), n = a.shape, b.shape[1]
    tn, tk = _tile(n, tn), _tile(m, tk)
    return _mm(name, a, b, (nj, n // tn, m // tk), ((None, tk, f), lambda j, i, k: (j, k, 0)),
               ((tk, tn), lambda j, i, k: (k, i)), ((None, f, tn), lambda j, i, k: (j, 0, i)),
               (nj, f, n), out_dtype, TN, scale)


def _split3(x):
    hi = x.astype(BF16)
    r1 = x - hi.astype(F32)
    mid = r1.astype(BF16)
    lo = (r1 - mid.astype(F32)).astype(BF16)
    return hi, mid, lo


def _swap_halves(x, swap):
    return sum(jnp.dot(part, swap, preferred_element_type=F32) for part in _split3(x))


def _rope_specs(rope, tm):
    cos_t, _, swap = rope
    nt = cos_t.shape[0] // tm
    tab = pl.BlockSpec((tm, cos_t.shape[1]), lambda i, c: (i % nt, 0))
    return [tab, tab, pl.BlockSpec(swap.shape, lambda i, c: (0, 0))]


def _rms_fwd(name, x, g, out_dtype, rope=None, scale=None):
    rows, cols = x.shape
    d = g.shape[1]
    tm = _row_tile(rows if rope is None else rope[0].shape[0], d)

    def body(x_ref, g_ref, *refs):
        xf = x_ref[...].astype(F32)
        r = lax.rsqrt(jnp.mean(xf * xf, axis=-1, keepdims=True) + EPS)
        y = xf * r * g_ref[...]
        if rope is not None:
            c_ref, s_ref, p_ref = refs[:3]
            y = y * c_ref[...] + _swap_halves(y, p_ref[...]) * s_ref[...]
        if scale is not None:
            y = y * scale
        refs[-1][...] = y.astype(refs[-1].dtype)

    return _chained(
        body, name=name, grid=(rows // tm, cols // d),
        in_specs=[pl.BlockSpec((tm, d), lambda i, c: (i, c)), pl.BlockSpec((1, d), lambda i, c: (0, 0))]
        + (_rope_specs(rope, tm) if rope is not None else []),
        out_specs=pl.BlockSpec((tm, d), lambda i, c: (i, c)),
        out_shape=jax.ShapeDtypeStruct((rows, cols), out_dtype),
        compiler_params=_rowwise_params("parallel", "parallel"),
    )(*_streamed(x), g, *(rope or ()))


def _rms_bwd(name, x, g, dy, res=None, rope=None, narrow=False, scale=None):
    rows, cols = x.shape
    d = g.shape[1]
    tm = _row_tile(rows if rope is None else rope[0].shape[0], d)

    def body(*refs):
        x_ref, g_ref, dy_ref = refs[:3]
        n_out = 3 if narrow else 2
        dx_ref, dg_ref = refs[-n_out], refs[-1]
        r_ref = refs[3] if res is not None else None
        xf = x_ref[...].astype(F32)
        r = lax.rsqrt(jnp.mean(xf * xf, axis=-1, keepdims=True) + EPS)
        xhat = xf * r
        dyf = dy_ref[...].astype(F32)
        if scale is not None:
            dyf = dyf * scale
        if rope is not None:
            c_ref, s_ref, p_ref = refs[-n_out - 3:-n_out]
            dyf = dyf * c_ref[...] + _swap_halves(dyf * s_ref[...], p_ref[...])

        @pl.when((pl.program_id(0) == 0) & (pl.program_id(1) == 0))
        def _():
            dg_ref[...] = jnp.zeros_like(dg_ref)

        dg_ref[...] += jnp.sum(dyf * xhat, axis=0, keepdims=True)
        dxh = dyf * g_ref[...]
        dx = r * (dxh - xhat * jnp.mean(dxh * xhat, axis=-1, keepdims=True))
        if r_ref is not None:
            dx = dx + r_ref[...].astype(F32)
        dx_ref[...] = dx
        if narrow:
            refs[-2][...] = dx.astype(BF16)

    blk = pl.BlockSpec((tm, d), lambda i, c: (i, c))
    in_specs = [blk, pl.BlockSpec((1, d), lambda i, c: (0, 0)), blk]
    operands = [x, g, dy]
    if res is not None:
        in_specs.append(blk)
        operands.append(res)
    if rope is not None:
        in_specs += _rope_specs(rope, tm)
        operands += list(rope)
    wide = [(blk, jax.ShapeDtypeStruct((rows, cols), F32))] + ([(blk, jax.ShapeDtypeStruct((rows, cols), BF16))] if narrow else [])
    outs = wide + [(pl.BlockSpec((1, d), lambda i, c: (0, 0)), jax.ShapeDtypeStruct((1, d), F32))]
    return _chained(
        body, name=name, grid=(rows // tm, cols // d), in_specs=in_specs,
        out_specs=tuple(spec for spec, _ in outs), out_shape=tuple(sds for _, sds in outs),
        compiler_params=_rowwise_params("arbitrary", "arbitrary"),
    )(*_streamed(x), g, *_streamed(*operands[2:4 if res is not None else 3]), *operands[4 if res is not None else 3:])


def _swiglu_fwd(name, g, u, out_dtype, tm=256):
    rows, cols = g.shape
    tm = _tile(rows, tm)

    def body(g_ref, u_ref, o_ref):
        gf = g_ref[...].astype(F32)
        o_ref[...] = (gf * jax.nn.sigmoid(gf) * u_ref[...].astype(F32)).astype(o_ref.dtype)

    blk = pl.BlockSpec((tm, cols), lambda i: (i, 0))
    return _chained(
        body, name=name, grid=(rows // tm,), in_specs=[blk, blk], out_specs=blk,
        out_shape=jax.ShapeDtypeStruct((rows, cols), out_dtype),
        compiler_params=_rowwise_params("parallel"),
    )(*_streamed(g, u))


def _swiglu_bwd(name, da, g, u, out_dtype, tm=256):
    rows, cols = g.shape
    tm = _tile(rows, tm)

    def body(da_ref, g_ref, u_ref, dg_ref, du_ref):
        gf = g_ref[...].astype(F32)
        daf = da_ref[...].astype(F32)
        sig = jax.nn.sigmoid(gf)
        du_ref[...] = (daf * gf * sig).astype(du_ref.dtype)
        dg_ref[...] = (daf * u_ref[...].astype(F32) * sig * (1.0 + gf * (1.0 - sig))).astype(dg_ref.dtype)

    blk = pl.BlockSpec((tm, cols), lambda i: (i, 0))
    sds = jax.ShapeDtypeStruct((rows, cols), out_dtype)
    return _chained(
        body, name=name, grid=(rows // tm,), in_specs=[blk, blk, blk], out_specs=(blk, blk), out_shape=(sds, sds),
        compiler_params=_rowwise_params("parallel"),
    )(*_streamed(da, g, u))


def _gate_fwd(name, pre, bias, tm=256):
    rows, cols = pre.shape
    tm = _tile(rows, tm)

    def body(p_ref, b_ref, o_ref):
        z = p_ref[...] + b_ref[...]
        o_ref[...] = (jnp.minimum(z, 0.0) - jnp.log(1.0 + jnp.exp(-jnp.abs(z)))) * (1.0 / GLA_TAU)

    blk = pl.BlockSpec((tm, cols), lambda i: (i, 0))
    return _chained(
        body, name=name, grid=(rows // tm,), in_specs=[blk, pl.BlockSpec((1, cols), lambda i: (0, 0))], out_specs=blk,
        out_shape=jax.ShapeDtypeStruct((rows, cols), F32),
        compiler_params=_rowwise_params("parallel"),
    )(pre, bias)


def _gate_bwd(name, pre, bias, dla, tm=256):
    rows, cols = pre.shape
    tm = _tile(rows, tm)

    def body(p_ref, b_ref, d_ref, dp_ref, db_ref):
        z = p_ref[...] + b_ref[...]
        dp = d_ref[...] * (1.0 / GLA_TAU) / (1.0 + jnp.exp(z))
        dp_ref[...] = dp

        @pl.when(pl.program_id(0) == 0)
        def _():
            db_ref[...] = jnp.zeros_like(db_ref)

        db_ref[...] += jnp.sum(dp, axis=0, keepdims=True)

    blk = pl.BlockSpec((tm, cols), lambda i: (i, 0))
    row = pl.BlockSpec((1, cols), lambda i: (0, 0))
    return _chained(
        body, name=name, grid=(rows // tm,), in_specs=[blk, row, blk], out_specs=(blk, row),
        out_shape=(jax.ShapeDtypeStruct((rows, cols), F32), jax.ShapeDtypeStruct((1, cols), F32)),
        compiler_params=_rowwise_params("arbitrary"),
    )(*_streamed(pre), bias, *_streamed(dla))


def _loss(name, y, target, tm=256):
    rows, d = y.shape
    tm = _tile(rows, tm)

    def body(y_ref, t_ref, dy_ref, dyb_ref, l_ref):
        err = y_ref[...] - t_ref[...]
        dy_ref[...] = err * (1.0 / d)
        dyb_ref[...] = (err * (1.0 / d)).astype(BF16)

        @pl.when(pl.program_id(0) == 0)
        def _():
            l_ref[...] = jnp.zeros_like(l_ref)

        sq = (err * err).reshape(tm // 8, 8, d)
        l_ref[...] += jnp.sum(sq, axis=0) * (0.5 / d)

    blk = pl.BlockSpec((tm, d), lambda i: (i, 0))
    return _chained(
        body, name=name, grid=(rows // tm,), in_specs=[blk, blk],
        out_specs=(blk, blk, pl.BlockSpec((8, d), lambda i: (0, 0))),
        out_shape=(jax.ShapeDtypeStruct((rows, d), F32), jax.ShapeDtypeStruct((rows, d), BF16),
                   jax.ShapeDtypeStruct((8, d), F32)),
        compiler_params=_rowwise_params("arbitrary"),
    )(*_streamed(y, target))


def _per_query_block(causal, nq, tq, tk, inner):
    if not causal:
        inner(0, tk)
        return
    for block in range(nq):
        pl.when(pl.program_id(1) == block)(functools.partial(inner, block * tq, (block + 1) * tq))


def _scores(q, k, causal):
    s = lax.dot_general(q, k, NT, preferred_element_type=F32)
    if causal:
        tq, kv = s.shape
        own = s[:, kv - tq:]
        qc = lax.broadcasted_iota(jnp.int32, own.shape, 0) // CHUNK
        kc = lax.broadcasted_iota(jnp.int32, own.shape, 1) // CHUNK
        own = jnp.where(kc <= qc, own, -1e30)
        s = own if kv == tq else jnp.concatenate([s[:, :kv - tq], own], axis=1)
    e = jnp.exp(s - jnp.max(s, axis=-1, keepdims=True))
    return e, jnp.sum(e, axis=-1, keepdims=True)


def _attn_fwd(name, q, k, v, causal, tq=256):
    nh, t, dk = q.shape
    tk, dv = k.shape[1], v.shape[2]
    tq = _tile(t, tq)

    def body(q_ref, k_ref, v_ref, o_ref):
        def inner(q0, kv):
            e, l = _scores(q_ref[...], k_ref[0:kv, :], causal)
            o = jnp.dot(e.astype(BF16), v_ref[0:kv, :], preferred_element_type=F32)
            o_ref[...] = (o / l).astype(o_ref.dtype)

        _per_query_block(causal, t // tq, tq, tk, inner)

    return _chained(
        body, name=name, grid=(nh, t // tq),
        in_specs=[pl.BlockSpec((None, tq, dk), lambda h, i: (h, i, 0)), pl.BlockSpec((None, tk, dk), lambda h, i: (h, 0, 0)),
                  pl.BlockSpec((None, tk, dv), lambda h, i: (h, 0, 0))],
        out_specs=pl.BlockSpec((tq, dv), lambda h, i: (i, h)),
        out_shape=jax.ShapeDtypeStruct((t, nh * dv), BF16),
        compiler_params=pltpu.CompilerParams(dimension_semantics=("parallel", "parallel"),
                                             vmem_limit_bytes=_vmem_limit(0, 6 * tq * tk * 4)),
    )(q, k, v)


def _attn_bwd(name, q, k, v, do, causal, tq=256):
    nh, t, dk = q.shape
    tk, dv = k.shape[1], v.shape[2]
    tq = _tile(t, tq)

    def body(q_ref, k_ref, v_ref, do_ref, dq_ref, dk_ref, dv_ref):
        @pl.when(pl.program_id(1) == 0)
        def _():
            dk_ref[...] = jnp.zeros_like(dk_ref)
            dv_ref[...] = jnp.zeros_like(dv_ref)

        def inner(q0, kv):
            qb, kb = q_ref[...], k_ref[0:kv, :]
            e, l = _scores(qb, kb, causal)
            p = e * (1.0 / l)
            dob = do_ref[...].astype(BF16)
            dp = lax.dot_general(dob, v_ref[0:kv, :], NT, preferred_element_type=F32)
            ds = (p * (dp - jnp.sum(p * dp, axis=-1, keepdims=True))).astype(BF16)
            dq_ref[...] = jnp.dot(ds, kb, preferred_element_type=F32)
            dk_ref[0:kv, :] += lax.dot_general(ds, qb, TN, preferred_element_type=F32)
            dv_ref[0:kv, :] += lax.dot_general(p.astype(BF16), dob, TN, preferred_element_type=F32)

        _per_query_block(causal, t // tq, tq, tk, inner)

    return _chained(
        body, name=name, grid=(nh, t // tq),
        in_specs=[pl.BlockSpec((None, tq, dk), lambda h, i: (h, i, 0)), pl.BlockSpec((None, tk, dk), lambda h, i: (h, 0, 0)),
                  pl.BlockSpec((None, tk, dv), lambda h, i: (h, 0, 0)), pl.BlockSpec((tq, dv), lambda h, i: (i, h))],
        out_specs=(pl.BlockSpec((None, tq, dk), lambda h, i: (h, i, 0)), pl.BlockSpec((None, tk, dk), lambda h, i: (h, 0, 0)),
                   pl.BlockSpec((None, tk, dv), lambda h, i: (h, 0, 0))),
        out_shape=(jax.ShapeDtypeStruct((nh, t, dk), F32), jax.ShapeDtypeStruct((nh, tk, dk), F32),
                   jax.ShapeDtypeStruct((nh, tk, dv), F32)),
        compiler_params=pltpu.CompilerParams(dimension_semantics=("parallel", "arbitrary"),
                                             vmem_limit_bytes=_vmem_limit(0, 10 * tq * tk * 4)),
    )(q, k, v, do)


def _tri(lower):
    r = lax.broadcasted_iota(jnp.int32, (CHUNK, CHUNK), 0)
    c = lax.broadcasted_iota(jnp.int32, (CHUNK, CHUNK), 1)
    return jnp.where((c <= r) if lower else (c >= r), 1.0, 0.0).astype(BF16)


def _tri_dot(tri, x):
    return sum(jnp.dot(tri, part, preferred_element_type=F32) for part in _split3(x))


def _gla_fwd(name, q, k, v, la, nh):
    t = q.shape[0]
    dk, dv = q.shape[1] // nh, v.shape[1] // nh
    nc = t // CHUNK

    def body(q_ref, k_ref, v_ref, g_ref, o_ref, st_ref, state):
        @pl.when(pl.program_id(0) == 0)
        def _():
            state[...] = jnp.zeros_like(state)

        g = g_ref[...]
        b = _tri_dot(_tri(True), g)
        b_end = jnp.sum(g, axis=0, keepdims=True)
        k_dec = (k_ref[...] * jnp.exp(b_end - b)).astype(BF16)
        decay = jnp.exp(b_end)
        qc = (q_ref[...] * (dk ** -0.5)).astype(BF16)
        vb = v_ref[...].astype(BF16)
        for h in range(nh):
            ks, vs = slice(h * dk, (h + 1) * dk), slice(h * dv, (h + 1) * dv)
            u_t = lax.dot_general(vb[:, vs], k_dec[:, ks], TN, preferred_element_type=F32)
            new = state[h] * decay[:, ks] + u_t
            state[h] = new
            st_ref[h] = new
            o_ref[:, vs] = lax.dot_general(qc[:, ks], new.astype(BF16), NT, preferred_element_type=F32)

    kblk = pl.BlockSpec((CHUNK, nh * dk), lambda n: (n, 0))
    vblk = pl.BlockSpec((CHUNK, nh * dv), lambda n: (n, 0))
    return _chained(
        body, name=name, grid=(nc,), in_specs=[kblk, kblk, vblk, kblk],
        out_specs=(vblk, pl.BlockSpec((nh, None, dv, dk), lambda n: (0, n, 0, 0))),
        out_shape=(jax.ShapeDtypeStruct((t, nh * dv), F32), jax.ShapeDtypeStruct((nh, nc, dv, dk), F32)),
        scratch_shapes=[pltpu.VMEM((nh, dv, dk), F32)],
        compiler_params=_rowwise_params("arbitrary"),
    )(q, k, v, la)


def _gla_bwd(name, q, k, v, la, states, do, nh):
    t = q.shape[0]
    dk, dv = q.shape[1] // nh, v.shape[1] // nh
    nc = t // CHUNK
    scale = dk ** -0.5

    def body(q_ref, k_ref, v_ref, g_ref, do_ref, st_ref, sp_ref, dq_ref, dk_ref, dv_ref, dg_ref, carry):
        i = pl.program_id(0)

        @pl.when(i == 0)
        def _():
            carry[...] = jnp.zeros_like(carry)

        g = g_ref[...]
        b = _tri_dot(_tri(True), g)
        b_end = jnp.sum(g, axis=0, keepdims=True)
        w = jnp.exp(b_end - b)
        decay = jnp.exp(b_end)
        k_dec = k_ref[...] * w
        k_decb = k_dec.astype(BF16)
        qc = (q_ref[...] * scale).astype(BF16)
        dob = do_ref[...].astype(BF16)
        vb = v_ref[...].astype(BF16)
        dk_dec, ddecay = [], []
        for h in range(nh):
            ks, vs = slice(h * dk, (h + 1) * dk), slice(h * dv, (h + 1) * dv)
            dq_ref[:, ks] = jnp.dot(dob[:, vs], st_ref[h].astype(BF16), preferred_element_type=F32) * scale
            g_t = carry[h] + lax.dot_general(dob[:, vs], qc[:, ks], TN, preferred_element_type=F32)
            g_tb = g_t.astype(BF16)
            dk_dec.append(jnp.dot(vb[:, vs], g_tb, preferred_element_type=F32))
            dv_ref[:, vs] = lax.dot_general(k_decb[:, ks], g_tb, NT, preferred_element_type=F32)
            prev = jnp.where(i < nc - 1, sp_ref[h], 0.0)
            ddecay.append(jnp.sum(g_t * prev, axis=0, keepdims=True))
            carry[h] = g_t * decay[:, ks]
        dk_dec = jnp.concatenate(dk_dec, axis=1)
        dk_ref[...] = dk_dec * w
        e = dk_dec * k_dec
        db_end = jnp.sum(e, axis=0, keepdims=True) + jnp.concatenate(ddecay, axis=1) * decay
        dg_ref[...] = _tri_dot(_tri(False), -e) + db_end

    kblk = pl.BlockSpec((CHUNK, nh * dk), lambda i: (nc - 1 - i, 0))
    vblk = pl.BlockSpec((CHUNK, nh * dv), lambda i: (nc - 1 - i, 0))
    ksds = jax.ShapeDtypeStruct((t, nh * dk), F32)
    return _chained(
        body, name=name, grid=(nc,),
        in_specs=[kblk, kblk, vblk, kblk, vblk,
                  pl.BlockSpec((nh, None, dv, dk), lambda i: (0, nc - 1 - i, 0, 0)),
                  pl.BlockSpec((nh, None, dv, dk), lambda i: (0, jnp.maximum(nc - 2 - i, 0), 0, 0))],
        out_specs=(kblk, kblk, vblk, kblk),
        out_shape=(ksds, ksds, jax.ShapeDtypeStruct((t, nh * dv), F32), ksds),
        scratch_shapes=[pltpu.VMEM((nh, dv, dk), F32)],
        compiler_params=_rowwise_params("arbitrary"),
    )(q, k, v, la, do, states, states)


def _adamw(name, w, m, v, parts):
    rows, cols = w.shape
    tm = 1 << int(math.log2(max(8, (1 << 18) // (-(-cols // LANES) * LANES))))
    while rows % tm and tm > 8:
        tm //= 2
    tm = _tile(rows, tm)

    def body(w_ref, m_ref, v_ref, p_ref, g_ref, d_ref, nm_ref, nv_ref):
        g = p_ref[0].astype(F32)
        for s in range(1, parts.shape[0]):
            g = g + p_ref[s].astype(F32)
        m_new = ADAM_B1 * m_ref[...] + (1.0 - ADAM_B1) * g
        v_new = ADAM_B2 * v_ref[...] + (1.0 - ADAM_B2) * jnp.square(g)
        m_hat = m_new / (1.0 - ADAM_B1 ** ADAM_STEP)
        v_hat = v_new / (1.0 - ADAM_B2 ** ADAM_STEP)
        g_ref[...] = g
        d_ref[...] = -ADAM_LR * (m_hat / (jnp.sqrt(v_hat) + ADAM_EPS) + ADAM_WD * w_ref[...])
        nm_ref[...] = m_new
        nv_ref[...] = v_new

    blk = pl.BlockSpec((tm, cols), lambda i: (i, 0))
    sds = jax.ShapeDtypeStruct((rows, cols), F32)
    return _chained(
        body, name=name, grid=(rows // tm,),
        in_specs=[blk, blk, blk, pl.BlockSpec((parts.shape[0], tm, cols), lambda i: (0, i, 0))],
        out_specs=(blk, blk, blk, blk), out_shape=(sds, sds, sds, sds),
        compiler_params=_rowwise_params("parallel"),
    )(*_streamed(w, m, v, parts))


HBM = pl.BlockSpec(memory_space=pl.ANY)
MESH = pl.DeviceIdType.MESH


def _all_gather(name, shards):
    n = len(shards)

    def body(*refs):
        ins, outs = refs[:n], refs[n:2 * n]
        send_sems, recv_sems, local_sems = refs[2 * n:]
        x, y, c = lax.axis_index("x"), lax.axis_index("y"), lax.axis_index("c")
        me, sibling = (x, y, c), (x, y, 1 - c)
        chips = [(1 - x, y), (x, 1 - y), (1 - x, 1 - y)]

        def copy(w, k, block, to, src=None):
            dst = outs[w].at[4 * block[0] + 2 * block[1] + block[2]]
            return pltpu.make_async_remote_copy(
                src_ref=dst if src is None else src, dst_ref=dst, send_sem=send_sems.at[7 * w + k],
                recv_sem=recv_sems.at[7 * w + k], device_id=to, device_id_type=MESH)

        mine, first, passed = [], [], []
        for w in range(n):
            cp = pltpu.make_async_copy(ins[w], outs[w].at[4 * x + 2 * y + c], local_sems.at[w])
            cp.start()
            mine.append(cp)
            first.append(copy(w, 0, me, sibling, src=ins[w]))
            first += [copy(w, 1 + j, me, (*chip, c), src=ins[w]) for j, chip in enumerate(chips)]
        for cp in first:
            cp.start()
        for w in range(n):
            for j, chip in enumerate(chips):
                copy(w, 1 + j, (*chip, c), me).wait_recv()
                cp = copy(w, 4 + j, (*chip, c), sibling)
                cp.start()
                passed.append(cp)
        for w in range(n):
            copy(w, 0, sibling, me).wait_recv()
            for j, chip in enumerate(chips):
                copy(w, 4 + j, (*chip, 1 - c), me).wait_recv()
        for cp in first + passed:
            cp.wait_send()
        for cp in mine:
            cp.wait()

    return _chained(
        body, name=name, in_specs=[HBM] * n, out_specs=[HBM] * n,
        out_shape=[jax.ShapeDtypeStruct((N_DEV,) + s.shape, s.dtype) for s in shards],
        scratch_shapes=[pltpu.SemaphoreType.DMA((7 * n,)), pltpu.SemaphoreType.DMA((7 * n,)),
                        pltpu.SemaphoreType.DMA((n,))],
    )(*shards)


HBM_SPEC = pl.BlockSpec(memory_space=pltpu.HBM)
SEM_SPEC = pl.BlockSpec(memory_space=pltpu.SEMAPHORE)
EFFECT = pltpu.SideEffectType.DATAFLOW_SIDE_EFFECTING
TOKEN = jax.ShapeDtypeStruct((8, LANES), F32)


def _in_hbm(a):
    return pltpu.with_memory_space_constraint(a, pltpu.HBM)


def _place():
    x, y, c = lax.axis_index("x"), lax.axis_index("y"), lax.axis_index("c")
    chips = [(1 - x, y), (x, 1 - y), (1 - x, 1 - y)]
    return x, y, c, chips


def _block_of(px, py, pc):
    return 4 * px + 2 * py + pc


def _gather_copies(ins, outs, sems, w):
    send_sems, recv_sems, local_sems = sems
    x, y, c, chips = _place()
    peers = [(x, y, 1 - c)] + [(*chip, c) for chip in chips]

    def copy(k, block, to, src):
        dst = outs[w].at[_block_of(*block)]
        return pltpu.make_async_remote_copy(src_ref=dst if src is None else src, dst_ref=dst, send_sem=send_sems.at[4 * w + k],
                                            recv_sem=recv_sems.at[4 * w + k], device_id=to, device_id_type=MESH)

    local = pltpu.make_async_copy(ins[w], outs[w].at[_block_of(x, y, c)], local_sems.at[w])
    sends = [copy(k, (x, y, c), peer, ins[w]) for k, peer in enumerate(peers)]
    recvs = [copy(k, peer, (x, y, c), None) for k, peer in enumerate(peers)]
    return local, sends, recvs


def _forward_copies(outs, sems, w):
    send_sems, recv_sems = sems
    x, y, c, chips = _place()

    def copy(j, block, to):
        dst = outs[w].at[_block_of(*block)]
        return pltpu.make_async_remote_copy(src_ref=dst, dst_ref=dst, send_sem=send_sems.at[3 * w + j],
                                            recv_sem=recv_sems.at[3 * w + j], device_id=to, device_id_type=MESH)

    sends = [copy(j, (*chip, c), (x, y, 1 - c)) for j, chip in enumerate(chips)]
    recvs = [copy(j, (*chip, 1 - c), (x, y, c)) for j, chip in enumerate(chips)]
    return sends, recvs


def _gather_start(name, shards):
    n = len(shards)

    def body(*refs):
        ins, outs, sems = refs[:n], refs[n:2 * n], refs[2 * n:2 * n + 3]
        for w in range(n):
            local, sends, _ = _gather_copies(ins, outs, sems, w)
            for cp in sends + [local]:
                cp.start()
        refs[-1][...] = jnp.zeros_like(refs[-1])

    lands = [lax.empty((N_DEV,) + s.shape, s.dtype) for s in shards]
    res = _chained(
        body, name=name, link=-1, in_specs=[HBM_SPEC] * (2 * n),
        out_specs=[SEM_SPEC] * 3 + [HBM_SPEC] * (2 * n) + [pl.BlockSpec(memory_space=pltpu.VMEM)],
        out_shape=[pltpu.SemaphoreType.DMA((4 * n,)), pltpu.SemaphoreType.DMA((4 * n,)), pltpu.SemaphoreType.DMA((n,))]
        + [pltpu.HBM(s.shape, s.dtype) for s in shards] + [pltpu.HBM(l.shape, l.dtype) for l in lands] + [TOKEN],
        input_output_aliases={i: 3 + i for i in range(2 * n)},
        compiler_params=pltpu.CompilerParams(has_side_effects=EFFECT),
    )(*[_in_hbm(s) for s in shards], *[_in_hbm(l) for l in lands])
    return n, res[:3], res[3:3 + n], res[3 + n:3 + 2 * n]


def _gather_mid(name, state):
    n, sems, shards, lands = state

    def body(*refs):
        ins, outs, sems_in = refs[:n], refs[n:2 * n], refs[2 * n:2 * n + 3]
        sems_out = refs[2 * n + 3:2 * n + 5]
        for w in range(n):
            local, sends, recvs = _gather_copies(ins, outs, sems_in, w)
            local.wait()
            for cp in sends:
                cp.wait_send()
            for cp in recvs:
                cp.wait_recv()
            for cp in _forward_copies(outs, sems_out, w)[0]:
                cp.start()
        refs[-1][...] = jnp.zeros_like(refs[-1])

    res = _chained(
        body, name=name, link=-1, in_specs=[HBM_SPEC] * (2 * n) + [SEM_SPEC] * 3,
        out_specs=[SEM_SPEC] * 2 + [HBM_SPEC] * n + [pl.BlockSpec(memory_space=pltpu.VMEM)],
        out_shape=[pltpu.SemaphoreType.DMA((3 * n,)), pltpu.SemaphoreType.DMA((3 * n,))]
        + [pltpu.HBM(l.shape, l.dtype) for l in lands] + [TOKEN],
        input_output_aliases={n + i: 2 + i for i in range(n)},
        compiler_params=pltpu.CompilerParams(has_side_effects=EFFECT),
    )(*shards, *lands, *sems)
    return n, res[:2], res[2:2 + n]


def _gather_end(name, state):
    n, sems, lands = state

    def body(*refs):
        outs, sems_in = refs[:n], refs[n:n + 2]
        for w in range(n):
            sends, recvs = _forward_copies(outs, sems_in, w)
            for cp in sends:
                cp.wait_send()
            for cp in recvs:
                cp.wait_recv()

    return _chained(
        body, name=name, in_specs=[HBM_SPEC] * n + [SEM_SPEC] * 2, out_specs=[HBM_SPEC] * n,
        out_shape=[pltpu.HBM(l.shape, l.dtype) for l in lands], input_output_aliases={i: i for i in range(n)},
        compiler_params=pltpu.CompilerParams(has_side_effects=EFFECT),
    )(*lands, *sems)


def _exchange_copies(ins, outs, sems, w):
    send_sems, recv_sems, local_sems = sems
    x, y, c, _ = _place()
    mine = _block_of(x, y, c)
    local = pltpu.make_async_copy(ins[w].at[mine], outs[w].at[mine], local_sems.at[w])
    remote = []
    for k in range(1, N_DEV):
        px, py, pc = x ^ (k >> 2), y ^ ((k >> 1) & 1), c ^ (k & 1)
        remote.append(pltpu.make_async_remote_copy(
            src_ref=ins[w].at[_block_of(px, py, pc)], dst_ref=outs[w].at[mine], send_sem=send_sems.at[7 * w + k - 1],
            recv_sem=recv_sems.at[7 * w + k - 1], device_id=(px, py, pc), device_id_type=MESH))
    return local, remote


def _exchange_start(name, stacks):
    n = len(stacks)

    def body(*refs):
        ins, outs, sems = refs[:n], refs[n:2 * n], refs[2 * n:2 * n + 3]
        for w in range(n):
            local, remote = _exchange_copies(ins, outs, sems, w)
            for cp in remote + [local]:
                cp.start()
        refs[-1][...] = jnp.zeros_like(refs[-1])

    lands = [lax.empty(s.shape, s.dtype) for s in stacks]
    if any(s is _Chain.last for s in stacks):
        _Chain.last = None
    res = _chained(
        body, name=name, link=-1, in_specs=[HBM_SPEC] * (2 * n),
        out_specs=[SEM_SPEC] * 3 + [HBM_SPEC] * (2 * n) + [pl.BlockSpec(memory_space=pltpu.VMEM)],
        out_shape=[pltpu.SemaphoreType.DMA((7 * n,)), pltpu.SemaphoreType.DMA((7 * n,)), pltpu.SemaphoreType.DMA((n,))]
        + [pltpu.HBM(s.shape, s.dtype) for s in stacks] * 2 + [TOKEN],
        input_output_aliases={i: 3 + i for i in range(2 * n)},
        compiler_params=pltpu.CompilerParams(has_side_effects=EFFECT),
    )(*[_in_hbm(s) for s in stacks], *[_in_hbm(l) for l in lands])
    return n, res[:3], res[3:3 + n], res[3 + n:3 + 2 * n]


def _exchange_wait(name, state):
    n, sems, stacks, lands = state

    def body(*refs):
        ins, outs, sems_in = refs[:n], refs[n:2 * n], refs[2 * n:2 * n + 3]
        for w in range(n):
            local, remote = _exchange_copies(ins, outs, sems_in, w)
            local.wait()
            for cp in remote:
                cp.wait_send()
                cp.wait_recv()

    return _chained(
        body, name=name, in_specs=[HBM_SPEC] * (2 * n) + [SEM_SPEC] * 3, out_specs=[HBM_SPEC] * n,
        out_shape=[pltpu.HBM(l.shape, l.dtype) for l in lands], input_output_aliases={n + i: i for i in range(n)},
        compiler_params=pltpu.CompilerParams(has_side_effects=EFFECT),
    )(*stacks, *lands, *sems)


N_CHIP = N_DEV // 2


def _pair_copies(stack, land, sems):
    send_sems, recv_sems, _ = sems
    x, y, c, _ = _place()
    remote = [pltpu.make_async_remote_copy(src_ref=stack.at[2 * k + 1 - c], dst_ref=land.at[k], send_sem=send_sems.at[k],
                                           recv_sem=recv_sems.at[k], device_id=(x, y, 1 - c), device_id_type=MESH)
              for k in range(N_CHIP)]
    return [], remote


def _chip_copies(pairs, land, sems):
    send_sems, recv_sems, local_sems = sems
    x, y, c, _ = _place()
    mine = 2 * x + y
    local = pltpu.make_async_copy(pairs.at[mine], land.at[mine], local_sems.at[0])
    remote = []
    for m in range(1, N_CHIP):
        px, py = x ^ (m >> 1), y ^ (m & 1)
        remote.append(pltpu.make_async_remote_copy(
            src_ref=pairs.at[2 * px + py], dst_ref=land.at[mine], send_sem=send_sems.at[m - 1],
            recv_sem=recv_sems.at[m - 1], device_id=(px, py, c), device_id_type=MESH))
    return local, remote


def _stage_start(name, copies_of, src_arr, n_land, n_sems):
    half = (N_CHIP,) + src_arr.shape[1:]

    def body(*refs):
        local, remote = copies_of(refs[0], *refs[1:1 + n_land], refs[1 + n_land:4 + n_land])
        for cp in remote + (local if isinstance(local, list) else [local]):
            cp.start()
        refs[-1][...] = jnp.zeros_like(refs[-1])

    if src_arr is _Chain.last:
        _Chain.last = None
    lands = [lax.empty(half, src_arr.dtype) for _ in range(n_land)]
    res = _chained(
        body, name=name, link=-1, in_specs=[HBM_SPEC] * (1 + n_land),
        out_specs=[SEM_SPEC] * 3 + [HBM_SPEC] * (1 + n_land) + [pl.BlockSpec(memory_space=pltpu.VMEM)],
        out_shape=[pltpu.SemaphoreType.DMA((n,)) for n in n_sems] + [pltpu.HBM(src_arr.shape, src_arr.dtype)]
        + [pltpu.HBM(half, src_arr.dtype)] * n_land + [TOKEN],
        input_output_aliases={i: 3 + i for i in range(1 + n_land)},
        compiler_params=pltpu.CompilerParams(has_side_effects=EFFECT),
    )(_in_hbm(src_arr), *[_in_hbm(l) for l in lands])
    return res[:3], res[3], res[4:4 + n_land]


def _stage_wait(name, copies_of, state):
    sems, src_arr, lands = state
    n_land = len(lands)

    def body(*refs):
        local, remote = copies_of(refs[0], *refs[1:1 + n_land], refs[1 + n_land:4 + n_land])
        for cp in (local if isinstance(local, list) else [local]):
            cp.wait()
        for cp in remote:
            cp.wait_send()
            cp.wait_recv()

    return _chained(
        body, name=name, in_specs=[HBM_SPEC] * (1 + n_land) + [SEM_SPEC] * 3, out_specs=[HBM_SPEC] * n_land,
        out_shape=[pltpu.HBM(l.shape, l.dtype) for l in lands],
        input_output_aliases={1 + i: i for i in range(n_land)},
        compiler_params=pltpu.CompilerParams(has_side_effects=EFFECT),
    )(src_arr, *lands, *sems)


def _pair_sum(name, stack, land, tm=64):
    n, rows, cols = land.shape
    tm = _tile(rows, tm)

    def body(c_ref, a_ref, b_ref, o_ref):
        o_ref[...] = (a_ref[...].astype(F32) + b_ref[...].astype(F32)).astype(o_ref.dtype)

    core = lax.axis_index("c").astype(jnp.int32).reshape(1)
    blk = pl.BlockSpec((n, tm, cols), lambda i, c_ref: (0, i, 0))
    mine = pl.BlockSpec((n, None, tm, cols), lambda i, c_ref: (0, c_ref[0], i, 0))
    _Chain.last = pl.pallas_call(
        body, name=name,
        grid_spec=pltpu.PrefetchScalarGridSpec(num_scalar_prefetch=1, grid=(rows // tm,), in_specs=[mine, blk], out_specs=blk),
        out_shape=jax.ShapeDtypeStruct(land.shape, land.dtype), compiler_params=_rowwise_params("parallel"),
    )(core, stack.reshape(n, 2, rows, cols), land)
    return _Chain.last


def _unstack_cols(w):
    return w.transpose(1, 0, 2).reshape(w.shape[1], N_DEV * w.shape[2])


def _stack_cols(w):
    return w.reshape(w.shape[0], N_DEV, w.shape[1] // N_DEV).transpose(1, 0, 2)


def _rope_tables(positions):
    half = MLA_ROPE // 2
    inv_freq = ROPE_THETA ** (-jnp.arange(half, dtype=F32) / half)
    ang = positions.astype(F32)[:, None] * inv_freq
    cos, sin = jnp.cos(ang), jnp.sin(ang)
    t = positions.shape[0]
    cos_t = jnp.concatenate([jnp.ones((t, MLA_NOPE), F32), cos, cos], axis=1)
    sin_t = jnp.concatenate([jnp.zeros((t, MLA_NOPE), F32), -sin, sin], axis=1)
    idx = jnp.arange(MLA_QK)
    partner = jnp.where(idx < MLA_NOPE, -1, jnp.where(idx < MLA_NOPE + half, idx + half, idx - half))
    swap = (idx[:, None] == partner[None, :]).astype(BF16)
    return cos_t, sin_t, swap


def _ffn_fwd(tag, x, gain, wt):
    h = _rms_fwd(tag + "_rms", x, gain, BF16)
    g = _mm_stack_nt_out(tag + "_gate", h, wt[tag + '_w_gate'], BF16)
    u, a = _mm_stack_nt_out(tag + "_up", h, wt[tag + '_w_up'], BF16,
                            fuse=(lambda u_blk, g_blk: (u_blk, g_blk * jax.nn.sigmoid(g_blk) * u_blk), [g]))
    y = _mm_stack_sum(tag + "_down", a, wt[tag + '_w_down'], F32, scale=0.5, res=x, tm=1024)
    return y, (x, h, g, u, a)


def _ffn_bwd_weights(tag, dy, saved, wd, comm):
    x, h, g, u, a = saved
    comm.grads({tag + '_w_down': _mm_stack_tn_right(tag + "_dwd", a, dy, BF16, scale=0.5)})

    def dact(da, g_blk, u_blk):
        sig = jax.nn.sigmoid(g_blk)
        return da * u_blk * sig * (1.0 + g_blk * (1.0 - sig)), da * g_blk * sig

    dg, du = _mm_stack_nt_out(tag + "_da", dy, wd, BF16, scale=0.5, fuse=(dact, [g, u]))
    comm.advance()
    comm.grads({tag + '_w_gate': _mm_stack_tn_right(tag + "_dwg", dg, h, BF16)})
    dwu = _mm_stack_tn_right(tag + "_dwu", du, h, BF16)
    comm.advance()
    comm.grads({tag + '_w_up': dwu})
    return dg, du


def _ffn_bwd_input(tag, dy, saved, dgu, gain, wg, wu, comm):
    dg, du = dgu
    dh = _mm_stack_sum(tag + "_dh", (dg, du), (wg, wu), F32)
    comm.advance()
    return _rms_bwd(tag + "_drms", saved[0], gain, dh, res=dy)


def _local_step(x, mem, positions, sm, comm, target):
    t, d = x.shape
    nm = mem.shape[0]
    gs, gw = {}, {}

    class Weights(dict):
        def __missing__(self, name):
            self.update(comm.weights(next(k for k, names in enumerate(GATHERS) if name in names)))
            return self[name]

    wt = Weights()

    x1, ffn1_saved = _ffn_fwd("ffn1", x, sm['ffn1_norm'], wt)

    h2 = _rms_fwd("mix_rms", x1, sm['mix_norm'], BF16)
    w_in_ref = _unstack_cols(wt['w_in'])
    pieces = []
    for name, n in IN_PAD:
        piece = w_in_ref[:, REF_OFF[name]:REF_OFF[name] + REF_SIZE[name]]
        if n != REF_SIZE[name]:
            piece = jnp.pad(piece, ((0, 0), (0, n - REF_SIZE[name])))
        pieces.append(piece)
    w_in = jnp.concatenate(pieces, axis=1)
    z = _mm2("mix_in", h2, w_in, NN, F32)
    zs = {name: z[:, PAD_OFF[name]:PAD_OFF[name] + n] for name, n in IN_PAD}

    cq = _rms_fwd("mla_q_a_rms", zs['zq'], sm['q_a_norm'], BF16)
    q_raw = _mm_stack_nt_out("mla_q_up", cq, wt['w_q_up'], F32)
    ckv = _rms_fwd("mla_kv_a_rms", zs['zkv'], sm['kv_a_norm'], BF16)
    kv = _mm_stack_out("mla_kv_up", ckv, wt['w_kv_up'], F32)
    zkr = zs['zkr'][:, :MLA_ROPE]
    k_raw = jnp.concatenate([kv[:, :, :MLA_NOPE], jnp.broadcast_to(zkr[None], (MLA_HEADS, t, MLA_ROPE))], axis=2)
    v_mla = kv[:, :, MLA_NOPE:].astype(BF16)
    cos_t, sin_t, swap = _rope_tables(positions)
    q_raw2, k_raw2 = q_raw.reshape(MLA_HEADS * t, MLA_QK), k_raw.reshape(MLA_HEADS * t, MLA_QK)
    rope = (cos_t, sin_t, swap)
    qf = _rms_fwd("mla_q_rms", q_raw2, sm['mla_q_norm'], BF16, rope=rope, scale=MLA_QK ** -0.5).reshape(MLA_HEADS, t, MLA_QK)
    kf = _rms_fwd("mla_k_rms", k_raw2, sm['mla_k_norm'], BF16, rope=rope).reshape(MLA_HEADS, t, MLA_QK)
    o_mla = _attn_fwd("mla_attn", qf, kf, v_mla, True, tq=512)

    w_g2 = jnp.pad(_unstack_cols(wt['gla_w_gate2']), ((0, LANES - GLA_GATE_RANK), (0, 0)))
    pre = _mm2("gla_gate_pre", zs['zg'], w_g2, NN, F32)
    log_a = _gate_fwd("gla_gate", pre, sm['gla_b_gate'])
    o_gla_raw, states = _gla_fwd("gla_scan", zs['gq'], zs['gk'], zs['gv'], log_a, GLA_HEADS)
    o_gla_n = _rms_fwd("gla_out_rms", o_gla_raw, sm['gla_out_norm'], F32)
    o_gla = _swiglu_fwd("gla_out_gate", zs['zr'], o_gla_n, BF16)

    cat = jnp.concatenate([o_mla, o_gla], axis=1)
    w_out = wt['w_out'].reshape(d, d)
    x2 = _mm2("mix_out", cat, w_out, NN, F32, res=x1)

    w_mq, w_mk, w_mv = (wt[n].reshape(d, MEM_HEADS * MEM_HEAD_DIM) for n in ('mem_w_q', 'mem_w_k', 'mem_w_v'))
    hq = _rms_fwd("mem_attn_rms", x2, sm['mem_attn_norm'], BF16)
    hm = _rms_fwd("mem_rms", mem, sm['mem_norm'], BF16)

    def heads_out(name, a, b, out_dtype):
        m, kk = a.shape
        tm = _tile(m, 512)
        return _mm(name, a, b, (m // tm, MEM_HEADS, 1), ((tm, kk), lambda i, h, k: (i, 0)),
                   ((kk, MEM_HEAD_DIM), lambda i, h, k: (0, h)), ((None, tm, MEM_HEAD_DIM), lambda i, h, k: (h, i, 0)),
                   (MEM_HEADS, m, MEM_HEAD_DIM), out_dtype, NN)

    mq_raw = heads_out("mem_q", hq, w_mq, F32)
    mk_raw = heads_out("mem_k", hm, w_mk, F32)
    mv = heads_out("mem_v", hm, w_mv, BF16)
    mq = _rms_fwd("mem_q_rms", mq_raw.reshape(MEM_HEADS * t, MEM_HEAD_DIM), sm['mem_q_norm'], BF16, scale=MEM_HEAD_DIM ** -0.5)
    mk = _rms_fwd("mem_k_rms", mk_raw.reshape(MEM_HEADS * nm, MEM_HEAD_DIM), sm['mem_k_norm'], BF16)
    mq, mk = mq.reshape(MEM_HEADS, t, MEM_HEAD_DIM), mk.reshape(MEM_HEADS, nm, MEM_HEAD_DIM)
    o_mem = _attn_fwd("mem_attn", mq, mk, mv, False)
    w_mo = wt['mem_w_o']
    mo_cols = w_mo.shape[2]
    tm = _tile(t, 512)
    x3 = _mm("mem_out", o_mem, w_mo, (t // tm, N_DEV, 1), ((tm, o_mem.shape[1]), lambda i, j, k: (i, 0)),
             ((None, o_mem.shape[1], mo_cols), lambda i, j, k: (j, 0, 0)), ((tm, mo_cols), lambda i, j, k: (i, j)),
             (t, d), F32, NN, res=x2)

    y, ffn2_saved = _ffn_fwd("ffn2", x3, sm['ffn2_norm'], wt)
    dy, dy_narrow, loss_lanes = _loss("loss", y, target)

    dgu = _ffn_bwd_weights("ffn2", dy_narrow, ffn2_saved, wt['ffn2_w_down'], comm)
    dx3, gs['ffn2_norm'] = _ffn_bwd_input("ffn2", dy, ffn2_saved, dgu, sm['ffn2_norm'],
                                          wt['ffn2_w_gate'], wt['ffn2_w_up'], comm)

    do_mem = _mm("mem_do", dx3, w_mo, (t // tm, 1, N_DEV), ((tm, mo_cols), lambda i, h, k: (i, k)),
                 ((None, w_mo.shape[1], mo_cols), lambda i, h, k: (k, 0, 0)), ((tm, w_mo.shape[1]), lambda i, h, k: (i, 0)),
                 (t, MEM_HEADS * MEM_HEAD_DIM), BF16, NT)
    tk = _tile(t, 512)
    gw['mem_w_o'] = _mm("mem_dwo", o_mem, dx3, (N_DEV, 1, t // tk), ((tk, o_mem.shape[1]), lambda j, i, k: (k, 0)),
                        ((tk, mo_cols), lambda j, i, k: (k, j)), ((None, o_mem.shape[1], mo_cols), lambda j, i, k: (j, 0, 0)),
                        w_mo.shape, BF16, TN)
    dmq, dmk, dmv = _attn_bwd("mem_dattn", mq, mk, mv, do_mem, False)
    dmq_raw, gs['mem_q_norm'] = _rms_bwd("mem_q_drms", mq_raw.reshape(MEM_HEADS * t, MEM_HEAD_DIM), sm['mem_q_norm'],
                                         dmq.reshape(MEM_HEADS * t, MEM_HEAD_DIM), scale=MEM_HEAD_DIM ** -0.5)
    dmk_raw, gs['mem_k_norm'] = _rms_bwd("mem_k_drms", mk_raw.reshape(MEM_HEADS * nm, MEM_HEAD_DIM), sm['mem_k_norm'],
                                         dmk.reshape(MEM_HEADS * nm, MEM_HEAD_DIM))
    dmq_raw = dmq_raw.reshape(MEM_HEADS, t, MEM_HEAD_DIM)
    dmk_raw = dmk_raw.reshape(MEM_HEADS, nm, MEM_HEAD_DIM)

    def heads_in_nt(name, a, b, res=None):
        m, n = a.shape[1], b.shape[0]
        tm_, tn_ = _tile(m, 512), _tile(n, 1024)
        return _mm(name, a, b, (m // tm_, n // tn_, MEM_HEADS), ((None, tm_, MEM_HEAD_DIM), lambda i, j, k: (k, i, 0)),
                   ((tn_, MEM_HEAD_DIM), lambda i, j, k: (j, k)), ((tm_, tn_), lambda i, j, k: (i, j)), (m, n), F32, NT,
                   None, res)

    def heads_tn(name, a, b):
        m, kp = a.shape
        tm_, tk_ = _tile(kp, 1024), _tile(m, 512)
        return _mm(name, a, b, (kp // tm_, MEM_HEADS, m // tk_), ((tk_, tm_), lambda i, h, k: (k, i)),
                   ((None, tk_, MEM_HEAD_DIM), lambda i, h, k: (h, k, 0)), ((tm_, MEM_HEAD_DIM), lambda i, h, k: (i, h)),
                   (kp, MEM_HEADS * MEM_HEAD_DIM), BF16, TN)

    dhq = heads_in_nt("mem_dhq", dmq_raw, w_mq)
    gw['mem_w_q'] = heads_tn("mem_dwq", hq, dmq_raw).reshape(wt['mem_w_q'].shape)
    dhm = heads_in_nt("mem_dhm_k", dmk_raw, w_mk)
    dhm = heads_in_nt("mem_dhm_v", dmv, w_mv, res=dhm)
    gw['mem_w_k'] = heads_tn("mem_dwk", hm, dmk_raw).reshape(wt['mem_w_k'].shape)
    gw['mem_w_v'] = heads_tn("mem_dwv", hm, dmv).reshape(wt['mem_w_v'].shape)
    _, gs['mem_norm'] = _rms_bwd("mem_drms", mem, sm['mem_norm'], dhm)
    comm.grads({n: gw[n] for n in MEMORY})
    dx2, gs['mem_attn_norm'] = _rms_bwd("mem_attn_drms", x2, sm['mem_attn_norm'], dhq, res=dx3)

    dcat = _mm2("mix_dcat", dx2, w_out, NT, F32)
    comm.grads({'w_out': _mm2("mix_dwout", cat, dx2, TN, BF16, tm=1024, tn=2048, tk=512).reshape(wt['w_out'].shape)})
    do_mla, do_gla = dcat[:, :MLA_HEADS * MLA_V], dcat[:, MLA_HEADS * MLA_V:]

    dzr, dgn = _swiglu_bwd("gla_out_dgate", do_gla, zs['zr'], o_gla_n, F32)
    do_gla_raw, gs['gla_out_norm'] = _rms_bwd("gla_out_drms", o_gla_raw, sm['gla_out_norm'], dgn)
    dgq, dgk, dgv, dlog_a = _gla_bwd("gla_dscan", zs['gq'], zs['gk'], zs['gv'], log_a, states, do_gla_raw, GLA_HEADS)
    dpre, gs['gla_b_gate'] = _gate_bwd("gla_dgate", pre, sm['gla_b_gate'], dlog_a)
    dw_g2 = _mm2("gla_dwgate", zs['zg'], dpre, TN, BF16, tk=512)
    comm.grads({'gla_w_gate2': _stack_cols(dw_g2[:GLA_GATE_RANK])})
    dzg = _mm2("gla_dzg", dpre, w_g2, NT, F32)

    dqf, dkf, dv_mla = _attn_bwd("mla_dattn", qf, kf, v_mla, do_mla, True, tq=512)
    dq_raw, gs['mla_q_norm'] = _rms_bwd("mla_q_drms", q_raw2, sm['mla_q_norm'], dqf.reshape(MLA_HEADS * t, MLA_QK), rope=rope,
                                         scale=MLA_QK ** -0.5)
    dk_raw, gs['mla_k_norm'] = _rms_bwd("mla_k_drms", k_raw2, sm['mla_k_norm'], dkf.reshape(MLA_HEADS * t, MLA_QK), rope=rope)
    dq_raw = dq_raw.reshape(MLA_HEADS, t, MLA_QK)
    dk_raw = dk_raw.reshape(MLA_HEADS, t, MLA_QK)
    dkv = jnp.concatenate([dk_raw[:, :, :MLA_NOPE], dv_mla], axis=2)
    dzkr = jnp.sum(dk_raw[:, :, MLA_NOPE:], axis=0)
    comm.grads({'w_q_up': _mm_stack_tn_right("mla_dwq", dq_raw, cq, BF16),
                'w_kv_up': _mm_stack_tn_left("mla_dwkv", ckv, dkv, BF16)})
    dcq = _mm_stack_sum("mla_dcq", dq_raw, wt['w_q_up'], F32)
    dckv = _mm_stack_nt_sum("mla_dckv", dkv, wt['w_kv_up'], F32)
    dzq, gs['q_a_norm'] = _rms_bwd("mla_q_a_drms", zs['zq'], sm['q_a_norm'], dcq)
    dzkv, gs['kv_a_norm'] = _rms_bwd("mla_kv_a_drms", zs['zkv'], sm['kv_a_norm'], dckv)

    dzs = {'zq': dzq, 'zkv': dzkv, 'gq': dgq, 'gk': dgk, 'gv': dgv, 'zr': dzr,
           'zkr': jnp.pad(dzkr, ((0, 0), (0, LANES - MLA_ROPE))), 'zg': dzg}
    dz = jnp.concatenate([dzs[name].astype(BF16) for name, _ in IN_PAD], axis=1)
    dw_in = _mm2("mix_dwin", h2, dz, TN, BF16, tm=1024, tn=2048, tk=512)
    dw_in_ref = jnp.concatenate([dw_in[:, PAD_OFF[name]:PAD_OFF[name] + n] for name, n in IN_REF], axis=1)
    comm.grads({'w_in': _stack_cols(dw_in_ref)})
    dh2 = _mm2("mix_dh", dz, w_in, NT, F32)
    comm.advance()
    dx1, dx1_narrow, gs['mix_norm'] = _rms_bwd("mix_drms", x1, sm['mix_norm'], dh2, res=dx2, narrow=True)

    dgu = _ffn_bwd_weights("ffn1", dx1_narrow, ffn1_saved, wt['ffn1_w_down'], comm)
    grad_x, gs['ffn1_norm'] = _ffn_bwd_input("ffn1", dx1, ffn1_saved, dgu, sm['ffn1_norm'],
                                             wt['ffn1_w_gate'], wt['ffn1_w_up'], comm)
    return loss_lanes, grad_x, gs


def _pad_lanes(v):
    n = v.shape[1]
    return jnp.pad(v, ((0, 0), (0, -n % LANES)))


def _pack_small(vals):
    return jnp.concatenate([_pad_lanes(vals[n]) for n in SMALL], axis=1)


def _unpack_small(packed, like):
    out, off = {}, 0
    for n in SMALL:
        size = like[n].shape[1]
        out[n] = packed[:, off:off + size]
        off += size + (-size % LANES)
    return out


def kernel(x, mem, positions, ffn1_norm, ffn1_w_gate, ffn1_w_up, ffn1_w_down, mix_norm, w_in, q_a_norm, w_q_up, kv_a_norm, w_kv_up, mla_q_norm, mla_k_norm, gla_w_gate2, gla_b_gate, gla_out_norm, w_out, mem_attn_norm, mem_norm, mem_w_q, mem_w_k, mem_w_v, mem_w_o, mem_q_norm, mem_k_norm, ffn2_norm, ffn2_w_gate, ffn2_w_up, ffn2_w_down, loss_target, m_ffn1_norm, m_ffn1_w_gate, m_ffn1_w_up, m_ffn1_w_down, m_mix_norm, m_w_in, m_q_a_norm, m_w_q_up, m_kv_a_norm, m_w_kv_up, m_mla_q_norm, m_mla_k_norm, m_gla_w_gate2, m_gla_b_gate, m_gla_out_norm, m_w_out, m_mem_attn_norm, m_mem_norm, m_mem_w_q, m_mem_w_k, m_mem_w_v, m_mem_w_o, m_mem_q_norm, m_mem_k_norm, m_ffn2_norm, m_ffn2_w_gate, m_ffn2_w_up, m_ffn2_w_down, v_ffn1_norm, v_ffn1_w_gate, v_ffn1_w_up, v_ffn1_w_down, v_mix_norm, v_w_in, v_q_a_norm, v_w_q_up, v_kv_a_norm, v_w_kv_up, v_mla_q_norm, v_mla_k_norm, v_gla_w_gate2, v_gla_b_gate, v_gla_out_norm, v_w_out, v_mem_attn_norm, v_mem_norm, v_mem_w_q, v_mem_w_k, v_mem_w_v, v_mem_w_o, v_mem_q_norm, v_mem_k_norm, v_ffn2_norm, v_ffn2_w_gate, v_ffn2_w_up, v_ffn2_w_down):
    inp = dict(locals())
    x, mem, positions, target = inp['x'][0], inp['mem'][0], inp['positions'][0], inp['loss_target'][0]
    sm = {n: inp[n] for n in SMALL}
    out = {}

    def stored(key):
        name = key[2:] if key[:2] in ('m_', 'v_') else key
        return inp[key][0].T if name in TRANSPOSED else inp[key][0]

    def as_given(name, r):
        return r.T[None] if name in TRANSPOSED else r[None]

    class Comm:
        def __init__(self):
            self.gathers = {0: self.start(0)}
            self.forwards, self.exchanges, self.pairs = {}, [], []

        def start(self, k):
            return _gather_start(f"gather_start_{k}", [stored(n).astype(BF16) for n in GATHERS[k]])

        def forward(self, k):
            if k not in self.forwards:
                self.forwards[k] = _gather_mid(f"gather_mid_{k}", self.gathers[k])
                self.gathers.update({nxt: self.start(nxt) for nxt in NEXT_GATHERS.get(k, [])})

        def weights(self, k):
            self.forward(k)
            if k in EARLY_FORWARD:
                self.forward(EARLY_FORWARD[k])
            return dict(zip(GATHERS[k], _gather_end(f"gather_end_{k}", self.forwards[k])))

        def grads(self, stacks):
            names = list(stacks)
            if names[0] in TWO_STAGE:
                (n,) = names
                self.pairs.append((n, _stage_start("pair_start_" + n, _pair_copies, stacks[n], 1, (N_CHIP, N_CHIP, 1))))
            else:
                self.exchanges.append((names, _exchange_start("exchange_start_" + names[0], [stacks[n] for n in names])))

        def advance(self):
            for n, state in self.pairs:
                (land,) = _stage_wait("pair_wait_" + n, _pair_copies, state)
                pairs = _pair_sum("pair_sum_" + n, state[1], land)
                self.exchanges.append(([n], _stage_start("chip_start_" + n, _chip_copies, pairs, 1, (N_CHIP - 1, N_CHIP - 1, 1))))
            self.pairs = []

        def update(self, count):
            todo, self.exchanges = self.exchanges[:count], self.exchanges[count:]
            for names, state in todo:
                if names[0] in TWO_STAGE:
                    parts = _stage_wait("chip_wait_" + names[0], _chip_copies, state)
                else:
                    parts = _exchange_wait("exchange_wait_" + names[0], state)
                for n, p in zip(names, parts):
                    res = _adamw("adamw_" + n, stored(n), stored('m_' + n), stored('v_' + n), p)
                    for kind, r in zip(('grad_', 'delta_', 'new_m_', 'new_v_'), res):
                        out[kind + n] = as_given(n, r)

    _Chain.last = None
    comm = Comm()
    loss_lanes, grad_x, gs = _local_step(x, mem, positions, sm, comm, target)
    out['loss'] = lax.psum(jnp.sum(loss_lanes), ("x", "y", "c"))
    out['grad_x'] = grad_x[None]

    comm.update(len(comm.exchanges) - 3)
    small_parts = _all_gather("gather_small", [_pack_small(gs)])[0]
    res = _adamw("adamw_small", _pack_small(sm), _pack_small({n: inp['m_' + n] for n in SMALL}),
                 _pack_small({n: inp['v_' + n] for n in SMALL}), small_parts)
    for kind, r in zip(('grad_', 'delta_', 'new_m_', 'new_v_'), res):
        for n, val in _unpack_small(r, sm).items():
            out[kind + n] = val

    comm.update(3)

    names = ['loss', 'grad_x'] + [k + n for k in ('grad_', 'delta_', 'new_m_', 'new_v_') for n in WEIGHTS]
    return tuple(out[n] for n in names)
```

```python
import functools
import math

import jax
import jax.numpy as jnp
from jax import lax
from jax.experimental import pallas as pl
from jax.experimental.pallas import tpu as pltpu

F32 = jnp.float32
BF16 = jnp.bfloat16

N_DEV = 8
EPS = 1e-6
CHUNK = 64
MLA_HEADS, MLA_NOPE, MLA_ROPE, MLA_V = 8, 128, 64, 128
MLA_QK = MLA_NOPE + MLA_ROPE
MLA_Q_RANK, MLA_KV_RANK = 512, 256
ROPE_THETA = 10000.0
GLA_HEADS, GLA_DK, GLA_DV, GLA_GATE_RANK = 4, 128, 256, 16
GLA_TAU = 16.0
MEM_HEADS, MEM_HEAD_DIM = 4, 128
ADAM_LR, ADAM_B1, ADAM_B2, ADAM_EPS, ADAM_WD, ADAM_STEP = 0.001, 0.9, 0.999, 1e-08, 0.01, 10

V7X_VMEM_BYTES = 64 * 1024 * 1024
LANES = 128

NN = (((1,), (0,)), ((), ()))
NT = (((1,), (1,)), ((), ()))
TN = (((0,), (0,)), ((), ()))

WEIGHTS = ['ffn1_norm', 'ffn1_w_gate', 'ffn1_w_up', 'ffn1_w_down', 'mix_norm', 'w_in', 'q_a_norm', 'w_q_up',
           'kv_a_norm', 'w_kv_up', 'mla_q_norm', 'mla_k_norm', 'gla_w_gate2', 'gla_b_gate', 'gla_out_norm', 'w_out',
           'mem_attn_norm', 'mem_norm', 'mem_w_q', 'mem_w_k', 'mem_w_v', 'mem_w_o', 'mem_q_norm', 'mem_k_norm',
           'ffn2_norm', 'ffn2_w_gate', 'ffn2_w_up', 'ffn2_w_down']
SMALL = ['ffn1_norm', 'mix_norm', 'q_a_norm', 'kv_a_norm', 'mla_q_norm', 'mla_k_norm', 'gla_b_gate', 'gla_out_norm',
         'mem_attn_norm', 'mem_norm', 'mem_q_norm', 'mem_k_norm', 'ffn2_norm']
MIXER = ['w_in', 'w_q_up', 'w_kv_up', 'gla_w_gate2', 'w_out']
MEMORY = ['mem_w_q', 'mem_w_k', 'mem_w_v', 'mem_w_o']
TRANSPOSED = ['ffn1_w_gate', 'ffn1_w_up', 'ffn2_w_gate', 'ffn2_w_up', 'w_q_up']
TWO_STAGE = ['w_in', 'ffn1_w_down', 'ffn1_w_gate', 'ffn1_w_up']
GATHERS = [['ffn1_w_gate'], ['ffn1_w_up'], ['ffn1_w_down'], MIXER[:1], MIXER[1:], MEMORY,
           ['ffn2_w_gate', 'ffn2_w_up', 'ffn2_w_down']]
NEXT_GATHERS = {0: [1], 1: [2], 2: [3], 3: [4, 5, 6]}
EARLY_FORWARD = {5: 6}

IN_REF = [('zq', 512), ('zkv', 256), ('zkr', 64), ('gq', 512), ('gk', 512), ('gv', 1024), ('zg', 16), ('zr', 1024)]
IN_PAD = [('zq', 512), ('zkv', 256), ('gq', 512), ('gk', 512), ('gv', 1024), ('zr', 1024), ('zkr', 128), ('zg', 128)]
IN_WIDTH = sum(n for _, n in IN_REF)
IN_PAD_WIDTH = sum(n for _, n in IN_PAD)


def _offsets(layout):
    out, off = {}, 0
    for name, n in layout:
        out[name] = off
        off += n
    return out


REF_OFF, PAD_OFF = _offsets(IN_REF), _offsets(IN_PAD)
REF_SIZE = dict(IN_REF)


def _tile(n, pref):
    return pref if n % pref == 0 else n


def _block_bytes(blk, dtype):
    dims = [d for d in blk if d is not None]
    if len(dims) >= 1:
        dims[-1] = -(-dims[-1] // LANES) * LANES
    return math.prod(dims) * jnp.dtype(dtype).itemsize


def _vmem_limit(pipelined_bytes, resident_bytes=0):
    need = 2 * pipelined_bytes + resident_bytes + (8 << 20)
    return int(min(max(need, 32 << 20), V7X_VMEM_BYTES - (6 << 20)))


class _Chain:
    last = None


def _chained(body, *, in_specs, link=0, **kwargs):
    def call(*operands):
        dep = _Chain.last
        if dep is not None and any(o is dep for o in operands):
            dep = None
        if dep is None:
            res = pl.pallas_call(body, in_specs=in_specs, **kwargs)(*operands)
        else:
            n = len(operands)

            def chained_body(*refs):
                body(*refs[:n], *refs[n + 1:])

            res = pl.pallas_call(chained_body, in_specs=list(in_specs) + [pl.BlockSpec(memory_space=pl.ANY)],
                                 **kwargs)(*operands, dep)
        _Chain.last = res[link] if isinstance(res, (list, tuple)) else res
        return res

    return call


def _row_tile(rows, width, elements=1 << 18):
    lanes = -(-width // LANES) * LANES
    return _tile(rows, max(256, 1 << int(math.log2(max(1, elements // lanes)))))


def _streamed(*arrays):
    return [pltpu.with_memory_space_constraint(a, pltpu.HBM) for a in arrays]


def _rowwise_params(*semantics):
    return pltpu.CompilerParams(dimension_semantics=semantics, vmem_limit_bytes=48 << 20)


def _mm(name, a, b, grid, a_spec, b_spec, o_spec, out_shape, out_dtype, dims, scale=None, res=None, fuse=None):
    nk = grid[2]
    o_blk, o_map = o_spec
    acc_shape = tuple(d for d in o_blk if d is not None)
    extras = [res] if res is not None else (list(fuse[1]) if fuse else [])
    n_out = 2 if fuse else 1
    a_list, b_list = (list(a), list(b)) if isinstance(a, (tuple, list)) else ([a], [b])
    a_specs = a_spec if isinstance(a_spec, list) else [a_spec] * len(a_list)
    b_specs = b_spec if isinstance(b_spec, list) else [b_spec] * len(b_list)
    n_in = 2 * len(a_list)

    def body(*refs):
        a_refs, b_refs = refs[:n_in // 2], refs[n_in // 2:n_in]
        extra_refs = refs[n_in:n_in + len(extras)]
        out_refs = refs[n_in + len(extras):n_in + len(extras) + n_out]
        rest = refs[n_in + len(extras) + n_out:]
        r_ref = extra_refs[0] if res is not None else None
        o_ref = out_refs[0]

        def product():
            return sum(lax.dot_general(a_ref[...].astype(BF16), b_ref[...].astype(BF16), dims, preferred_element_type=F32)
                       for a_ref, b_ref in zip(a_refs, b_refs))

        def finish(r):
            if scale is not None:
                r = r * scale
            if r_ref is not None:
                r = r + r_ref[...].astype(F32)
            if fuse:
                for ref, val in zip(out_refs, fuse[0](r, *[e[...].astype(F32) for e in extra_refs])):
                    ref[...] = val.astype(ref.dtype)
            else:
                o_ref[...] = r.astype(o_ref.dtype)

        if nk == 1:
            finish(product())
        else:
            acc = rest[0]
            k = pl.program_id(2)

            @pl.when(k == 0)
            def _():
                acc[...] = product()

            @pl.when(k > 0)
            def _():
                acc[...] += product()

            @pl.when(k == nk - 1)
            def _():
                finish(acc[...])

    in_specs = [pl.BlockSpec(*spec) for spec in a_specs + b_specs]
    operands = a_list + b_list
    piped = sum(_block_bytes(spec[0], v.dtype) for spec, v in zip(a_specs + b_specs, operands))
    piped += _block_bytes(o_blk, out_dtype)
    for extra in extras:
        in_specs.append(pl.BlockSpec(o_blk, o_map))
        operands.append(extra)
        piped += _block_bytes(o_blk, extra.dtype)
    piped += (n_out - 1) * _block_bytes(o_blk, out_dtype)
    scratch = [pltpu.VMEM(acc_shape, F32)] if nk > 1 else []
    out_spec, out_sds = pl.BlockSpec(o_blk, o_map), jax.ShapeDtypeStruct(out_shape, out_dtype)
    return _chained(
        body, name=name, grid=grid, in_specs=in_specs, out_specs=(out_spec,) * n_out if fuse else out_spec,
        out_shape=(out_sds,) * n_out if fuse else out_sds, scratch_shapes=scratch,
        compiler_params=pltpu.CompilerParams(
            dimension_semantics=("parallel", "parallel", "arbitrary"),
            vmem_limit_bytes=_vmem_limit(piped, 3 * _block_bytes(acc_shape, F32))),
    )(*operands)


def _mm2(name, a, b, dims, out_dtype, tm=512, tn=1024, tk=2048, scale=None, res=None):
    if dims is NN:
        (m, kk), n = a.shape, b.shape[1]
    elif dims is NT:
        (m, kk), n = a.shape, b.shape[0]
    else:
        (kk, m), n = a.shape, b.shape[1]
    tm, tn, tk = _tile(m, tm), _tile(n, tn), _tile(kk, tk)
    a_spec = ((tk, tm), lambda i, j, k: (k, i)) if dims is TN else ((tm, tk), lambda i, j, k: (i, k))
    b_spec = ((tn, tk), lambda i, j, k: (j, k)) if dims is NT else ((tk, tn), lambda i, j, k: (k, j))
    return _mm(name, a, b, (m // tm, n // tn, kk // tk), a_spec, b_spec, ((tm, tn), lambda i, j, k: (i, j)),
               (m, n), out_dtype, dims, scale, res)


def _mm_stack_out(name, a, b, out_dtype, tm=512, tk=2048):
    (m, kk), (nj, _, n) = a.shape, b.shape
    tm, tk = _tile(m, tm), _tile(kk, tk)
    return _mm(name, a, b, (nj, m // tm, kk // tk), ((tm, tk), lambda j, i, k: (i, k)),
               ((None, tk, n), lambda j, i, k: (j, k, 0)), ((None, tm, n), lambda j, i, k: (j, i, 0)),
               (nj, m, n), out_dtype, NN)


def _mm_stack_nt_out(name, a, b, out_dtype, scale=None, fuse=None, tm=1024, tk=2048):
    (m, kk), (nj, n, _) = a.shape, b.shape
    tm, tk = _tile(m, tm), _tile(kk, tk)
    return _mm(name, a, b, (nj, m // tm, kk // tk), ((tm, tk), lambda j, i, k: (i, k)),
               ((None, n, tk), lambda j, i, k: (j, 0, k)), ((None, tm, n), lambda j, i, k: (j, i, 0)),
               (nj, m, n), out_dtype, NT, scale, fuse=fuse)


def _mm_stack_sum(name, a, b, out_dtype, scale=None, res=None, tm=512, tn=1024):
    (nj, m, f), n = (a[0] if isinstance(a, tuple) else a).shape, (b[0] if isinstance(b, tuple) else b).shape[2]
    tm, tn = _tile(m, tm), _tile(n, tn)
    pairs = list(zip(a, b)) if isinstance(a, tuple) else [(a, b)]
    step = 2 if nj % 2 == 0 else 1
    a_list, b_list, a_spec, b_spec = [], [], [], []
    for a_k, b_k in pairs:
        for s in range(step):
            a_list.append(a_k)
            b_list.append(b_k)
            a_spec.append(((None, tm, f), lambda i, j, k, s=s: (step * k + s, i, 0)))
            b_spec.append(((None, f, tn), lambda i, j, k, s=s: (step * k + s, 0, j)))
    return _mm(name, a_list, b_list, (m // tm, n // tn, nj // step), a_spec, b_spec, ((tm, tn), lambda i, j, k: (i, j)),
               (m, n), out_dtype, NN, scale, res)


def _mm_stack_nt_sum(name, a, b, out_dtype, res=None, tm=512, tn=1024):
    (nj, m, f), n = a.shape, b.shape[1]
    tm, tn = _tile(m, tm), _tile(n, tn)
    return _mm(name, a, b, (m // tm, n // tn, nj), ((None, tm, f), lambda i, j, k: (k, i, 0)),
               ((None, tn, f), lambda i, j, k: (k, j, 0)), ((tm, tn), lambda i, j, k: (i, j)),
               (m, n), out_dtype, NT, None, res)


def _mm_stack_tn_left(name, a, b, out_dtype, tm=1024, tk=512):
    (m, kp), (nj, _, n) = a.shape, b.shape
    tm, tk = _tile(kp, tm), _tile(m, tk)
    return _mm(name, a, b, (nj, kp // tm, m // tk), ((tk, tm), lambda j, i, k: (k, i)),
               ((None, tk, n), lambda j, i, k: (j, k, 0)), ((None, tm, n), lambda j, i, k: (j, i, 0)),
               (nj, kp, n), out_dtype, TN)


def _mm_stack_tn_right(name, a, b, out_dtype, scale=None, tn=2048, tk=2048):
    (nj, m, f), n = a.shape, b.shape[1]
    tn, tk = _tile(n, tn), _tile(m, tk)
    return _mm(name, a, b, (nj, n // tn, m // tk), ((None, tk, f), lambda j, i, k: (j, k, 0)),
               ((tk, tn), lambda j, i, k: (k, i)), ((None, f, tn), lambda j, i, k: (j, 0, i)),
               (nj, f, n), out_dtype, TN, scale)


def _split3(x):
    hi = x.astype(BF16)
    r1 = x - hi.astype(F32)
    mid = r1.astype(BF16)
    lo = (r1 - mid.astype(F32)).astype(BF16)
    return hi, mid, lo


def _swap_halves(x, swap):
    return sum(jnp.dot(part, swap, preferred_element_type=F32) for part in _split3(x))


def _rope_specs(rope, tm):
    cos_t, _, swap = rope
    nt = cos_t.shape[0] // tm
    tab = pl.BlockSpec((tm, cos_t.shape[1]), lambda i, c: (i % nt, 0))
    return [tab, tab, pl.BlockSpec(swap.shape, lambda i, c: (0, 0))]


def _rms_fwd(name, x, g, out_dtype, rope=None, scale=None):
    rows, cols = x.shape
    d = g.shape[1]
    tm = _row_tile(rows if rope is None else rope[0].shape[0], d)

    def body(x_ref, g_ref, *refs):
        xf = x_ref[...].astype(F32)
        r = lax.rsqrt(jnp.mean(xf * xf, axis=-1, keepdims=True) + EPS)
        y = xf * r * g_ref[...]
        if rope is not None:
            c_ref, s_ref, p_ref = refs[:3]
            y = y * c_ref[...] + _swap_halves(y, p_ref[...]) * s_ref[...]
        if scale is not None:
            y = y * scale
        refs[-1][...] = y.astype(refs[-1].dtype)

    return _chained(
        body, name=name, grid=(rows // tm, cols // d),
        in_specs=[pl.BlockSpec((tm, d), lambda i, c: (i, c)), pl.BlockSpec((1, d), lambda i, c: (0, 0))]
        + (_rope_specs(rope, tm) if rope is not None else []),
        out_specs=pl.BlockSpec((tm, d), lambda i, c: (i, c)),
        out_shape=jax.ShapeDtypeStruct((rows, cols), out_dtype),
        compiler_params=_rowwise_params("parallel", "parallel"),
    )(*_streamed(x), g, *(rope or ()))


def _rms_bwd(name, x, g, dy, res=None, rope=None, narrow=False, scale=None):
    rows, cols = x.shape
    d = g.shape[1]
    tm = _row_tile(rows if rope is None else rope[0].shape[0], d)

    def body(*refs):
        x_ref, g_ref, dy_ref = refs[:3]
        n_out = 3 if narrow else 2
        dx_ref, dg_ref = refs[-n_out], refs[-1]
        r_ref = refs[3] if res is not None else None
        xf = x_ref[...].astype(F32)
        r = lax.rsqrt(jnp.mean(xf * xf, axis=-1, keepdims=True) + EPS)
        xhat = xf * r
        dyf = dy_ref[...].astype(F32)
        if scale is not None:
            dyf = dyf * scale
        if rope is not None:
            c_ref, s_ref, p_ref = refs[-n_out - 3:-n_out]
            dyf = dyf * c_ref[...] + _swap_halves(dyf * s_ref[...], p_ref[...])

        @pl.when((pl.program_id(0) == 0) & (pl.program_id(1) == 0))
        def _():
            dg_ref[...] = jnp.zeros_like(dg_ref)

        dg_ref[...] += jnp.sum(dyf * xhat, axis=0, keepdims=True)
        dxh = dyf * g_ref[...]
        dx = r * (dxh - xhat * jnp.mean(dxh * xhat, axis=-1, keepdims=True))
        if r_ref is not None:
            dx = dx + r_ref[...].astype(F32)
        dx_ref[...] = dx
        if narrow:
            refs[-2][...] = dx.astype(BF16)

    blk = pl.BlockSpec((tm, d), lambda i, c: (i, c))
    in_specs = [blk, pl.BlockSpec((1, d), lambda i, c: (0, 0)), blk]
    operands = [x, g, dy]
    if res is not None:
        in_specs.append(blk)
        operands.append(res)
    if rope is not None:
        in_specs += _rope_specs(rope, tm)
        operands += list(rope)
    wide = [(blk, jax.ShapeDtypeStruct((rows, cols), F32))] + ([(blk, jax.ShapeDtypeStruct((rows, cols), BF16))] if narrow else [])
    outs = wide + [(pl.BlockSpec((1, d), lambda i, c: (0, 0)), jax.ShapeDtypeStruct((1, d), F32))]
    return _chained(
        body, name=name, grid=(rows // tm, cols // d), in_specs=in_specs,
        out_specs=tuple(spec for spec, _ in outs), out_shape=tuple(sds for _, sds in outs),
        compiler_params=_rowwise_params("arbitrary", "arbitrary"),
    )(*_streamed(x), g, *_streamed(*operands[2:4 if res is not None else 3]), *operands[4 if res is not None else 3:])


def _swiglu_fwd(name, g, u, out_dtype, tm=256):
    rows, cols = g.shape
    tm = _tile(rows, tm)

    def body(g_ref, u_ref, o_ref):
        gf = g_ref[...].astype(F32)
        o_ref[...] = (gf * jax.nn.sigmoid(gf) * u_ref[...].astype(F32)).astype(o_ref.dtype)

    blk = pl.BlockSpec((tm, cols), lambda i: (i, 0))
    return _chained(
        body, name=name, grid=(rows // tm,), in_specs=[blk, blk], out_specs=blk,
        out_shape=jax.ShapeDtypeStruct((rows, cols), out_dtype),
        compiler_params=_rowwise_params("parallel"),
    )(*_streamed(g, u))


def _swiglu_bwd(name, da, g, u, out_dtype, tm=256):
    rows, cols = g.shape
    tm = _tile(rows, tm)

    def body(da_ref, g_ref, u_ref, dg_ref, du_ref):
        gf = g_ref[...].astype(F32)
        daf = da_ref[...].astype(F32)
        sig = jax.nn.sigmoid(gf)
        du_ref[...] = (daf * gf * sig).astype(du_ref.dtype)
        dg_ref[...] = (daf * u_ref[...].astype(F32) * sig * (1.0 + gf * (1.0 - sig))).astype(dg_ref.dtype)

    blk = pl.BlockSpec((tm, cols), lambda i: (i, 0))
    sds = jax.ShapeDtypeStruct((rows, cols), out_dtype)
    return _chained(
        body, name=name, grid=(rows // tm,), in_specs=[blk, blk, blk], out_specs=(blk, blk), out_shape=(sds, sds),
        compiler_params=_rowwise_params("parallel"),
    )(*_streamed(da, g, u))


def _gate_fwd(name, pre, bias, tm=256):
    rows, cols = pre.shape
    tm = _tile(rows, tm)

    def body(p_ref, b_ref, o_ref):
        z = p_ref[...] + b_ref[...]
        o_ref[...] = (jnp.minimum(z, 0.0) - jnp.log(1.0 + jnp.exp(-jnp.abs(z)))) * (1.0 / GLA_TAU)

    blk = pl.BlockSpec((tm, cols), lambda i: (i, 0))
    return _chained(
        body, name=name, grid=(rows // tm,), in_specs=[blk, pl.BlockSpec((1, cols), lambda i: (0, 0))], out_specs=blk,
        out_shape=jax.ShapeDtypeStruct((rows, cols), F32),
        compiler_params=_rowwise_params("parallel"),
    )(pre, bias)


def _gate_bwd(name, pre, bias, dla, tm=256):
    rows, cols = pre.shape
    tm = _tile(rows, tm)

    def body(p_ref, b_ref, d_ref, dp_ref, db_ref):
        z = p_ref[...] + b_ref[...]
        dp = d_ref[...] * (1.0 / GLA_TAU) / (1.0 + jnp.exp(z))
        dp_ref[...] = dp

        @pl.when(pl.program_id(0) == 0)
        def _():
            db_ref[...] = jnp.zeros_like(db_ref)

        db_ref[...] += jnp.sum(dp, axis=0, keepdims=True)

    blk = pl.BlockSpec((tm, cols), lambda i: (i, 0))
    row = pl.BlockSpec((1, cols), lambda i: (0, 0))
    return _chained(
        body, name=name, grid=(rows // tm,), in_specs=[blk, row, blk], out_specs=(blk, row),
        out_shape=(jax.ShapeDtypeStruct((rows, cols), F32), jax.ShapeDtypeStruct((1, cols), F32)),
        compiler_params=_rowwise_params("arbitrary"),
    )(*_streamed(pre), bias, *_streamed(dla))


def _loss(name, y, target, tm=256):
    rows, d = y.shape
    tm = _tile(rows, tm)

    def body(y_ref, t_ref, dy_ref, dyb_ref, l_ref):
        err = y_ref[...] - t_ref[...]
        dy_ref[...] = err * (1.0 / d)
        dyb_ref[...] = (err * (1.0 / d)).astype(BF16)

        @pl.when(pl.program_id(0) == 0)
        def _():
            l_ref[...] = jnp.zeros_like(l_ref)

        sq = (err * err).reshape(tm // 8, 8, d)
        l_ref[...] += jnp.sum(sq, axis=0) * (0.5 / d)

    blk = pl.BlockSpec((tm, d), lambda i: (i, 0))
    return _chained(
        body, name=name, grid=(rows // tm,), in_specs=[blk, blk],
        out_specs=(blk, blk, pl.BlockSpec((8, d), lambda i: (0, 0))),
        out_shape=(jax.ShapeDtypeStruct((rows, d), F32), jax.ShapeDtypeStruct((rows, d), BF16),
                   jax.ShapeDtypeStruct((8, d), F32)),
        compiler_params=_rowwise_params("arbitrary"),
    )(*_streamed(y, target))


def _per_query_block(causal, nq, tq, tk, inner):
    if not causal:
        inner(0, tk)
        return
    for block in range(nq):
        pl.when(pl.program_id(1) == block)(functools.partial(inner, block * tq, (block + 1) * tq))


def _scores(q, k, causal):
    s = lax.dot_general(q, k, NT, preferred_element_type=F32)
    if causal:
        tq, kv = s.shape
        own = s[:, kv - tq:]
        qc = lax.broadcasted_iota(jnp.int32, own.shape, 0) // CHUNK
        kc = lax.broadcasted_iota(jnp.int32, own.shape, 1) // CHUNK
        own = jnp.where(kc <= qc, own, -1e30)
        s = own if kv == tq else jnp.concatenate([s[:, :kv - tq], own], axis=1)
    e = jnp.exp(s - jnp.max(s, axis=-1, keepdims=True))
    return e, jnp.sum(e, axis=-1, keepdims=True)


def _attn_fwd(name, q, k, v, causal, tq=256):
    nh, t, dk = q.shape
    tk, dv = k.shape[1], v.shape[2]
    tq = _tile(t, tq)

    def body(q_ref, k_ref, v_ref, o_ref):
        def inner(q0, kv):
            e, l = _scores(q_ref[...], k_ref[0:kv, :], causal)
            o = jnp.dot(e.astype(BF16), v_ref[0:kv, :], preferred_element_type=F32)
            o_ref[...] = (o / l).astype(o_ref.dtype)

        _per_query_block(causal, t // tq, tq, tk, inner)

    return _chained(
        body, name=name, grid=(nh, t // tq),
        in_specs=[pl.BlockSpec((None, tq, dk), lambda h, i: (h, i, 0)), pl.BlockSpec((None, tk, dk), lambda h, i: (h, 0, 0)),
                  pl.BlockSpec((None, tk, dv), lambda h, i: (h, 0, 0))],
        out_specs=pl.BlockSpec((tq, dv), lambda h, i: (i, h)),
        out_shape=jax.ShapeDtypeStruct((t, nh * dv), BF16),
        compiler_params=pltpu.CompilerParams(dimension_semantics=("parallel", "parallel"),
                                             vmem_limit_bytes=_vmem_limit(0, 6 * tq * tk * 4)),
    )(q, k, v)


def _attn_bwd(name, q, k, v, do, causal, tq=256):
    nh, t, dk = q.shape
    tk, dv = k.shape[1], v.shape[2]
    tq = _tile(t, tq)

    def body(q_ref, k_ref, v_ref, do_ref, dq_ref, dk_ref, dv_ref):
        @pl.when(pl.program_id(1) == 0)
        def _():
            dk_ref[...] = jnp.zeros_like(dk_ref)
            dv_ref[...] = jnp.zeros_like(dv_ref)

        def inner(q0, kv):
            qb, kb = q_ref[...], k_ref[0:kv, :]
            e, l = _scores(qb, kb, causal)
            p = e * (1.0 / l)
            dob = do_ref[...].astype(BF16)
            dp = lax.dot_general(dob, v_ref[0:kv, :], NT, preferred_element_type=F32)
            ds = (p * (dp - jnp.sum(p * dp, axis=-1, keepdims=True))).astype(BF16)
            dq_ref[...] = jnp.dot(ds, kb, preferred_element_type=F32)
            dk_ref[0:kv, :] += lax.dot_general(ds, qb, TN, preferred_element_type=F32)
            dv_ref[0:kv, :] += lax.dot_general(p.astype(BF16), dob, TN, preferred_element_type=F32)

        _per_query_block(causal, t // tq, tq, tk, inner)

    return _chained(
        body, name=name, grid=(nh, t // tq),
        in_specs=[pl.BlockSpec((None, tq, dk), lambda h, i: (h, i, 0)), pl.BlockSpec((None, tk, dk), lambda h, i: (h, 0, 0)),
                  pl.BlockSpec((None, tk, dv), lambda h, i: (h, 0, 0)), pl.BlockSpec((tq, dv), lambda h, i: (i, h))],
        out_specs=(pl.BlockSpec((None, tq, dk), lambda h, i: (h, i, 0)), pl.BlockSpec((None, tk, dk), lambda h, i: (h, 0, 0)),
                   pl.BlockSpec((None, tk, dv), lambda h, i: (h, 0, 0))),
        out_shape=(jax.ShapeDtypeStruct((nh, t, dk), F32), jax.ShapeDtypeStruct((nh, tk, dk), F32),
                   jax.ShapeDtypeStruct((nh, tk, dv), F32)),
        compiler_params=pltpu.CompilerParams(dimension_semantics=("parallel", "arbitrary"),
                                             vmem_limit_bytes=_vmem_limit(0, 10 * tq * tk * 4)),
    )(q, k, v, do)


def _tri(lower):
    r = lax.broadcasted_iota(jnp.int32, (CHUNK, CHUNK), 0)
    c = lax.broadcasted_iota(jnp.int32, (CHUNK, CHUNK), 1)
    return jnp.where((c <= r) if lower else (c >= r), 1.0, 0.0).astype(BF16)


def _tri_dot(tri, x):
    return sum(jnp.dot(tri, part, preferred_element_type=F32) for part in _split3(x))


def _gla_fwd(name, q, k, v, la, nh):
    t = q.shape[0]
    dk, dv = q.shape[1] // nh, v.shape[1] // nh
    nc = t // CHUNK

    def body(q_ref, k_ref, v_ref, g_ref, o_ref, st_ref, state):
        @pl.when(pl.program_id(0) == 0)
        def _():
            state[...] = jnp.zeros_like(state)

        g = g_ref[...]
        b = _tri_dot(_tri(True), g)
        b_end = jnp.sum(g, axis=0, keepdims=True)
        k_dec = (k_ref[...] * jnp.exp(b_end - b)).astype(BF16)
        decay = jnp.exp(b_end)
        qc = (q_ref[...] * (dk ** -0.5)).astype(BF16)
        vb = v_ref[...].astype(BF16)
        for h in range(nh):
            ks, vs = slice(h * dk, (h + 1) * dk), slice(h * dv, (h + 1) * dv)
            u_t = lax.dot_general(vb[:, vs], k_dec[:, ks], TN, preferred_element_type=F32)
            new = state[h] * decay[:, ks] + u_t
            state[h] = new
            st_ref[h] = new
            o_ref[:, vs] = lax.dot_general(qc[:, ks], new.astype(BF16), NT, preferred_element_type=F32)

    kblk = pl.BlockSpec((CHUNK, nh * dk), lambda n: (n, 0))
    vblk = pl.BlockSpec((CHUNK, nh * dv), lambda n: (n, 0))
    return _chained(
        body, name=name, grid=(nc,), in_specs=[kblk, kblk, vblk, kblk],
        out_specs=(vblk, pl.BlockSpec((nh, None, dv, dk), lambda n: (0, n, 0, 0))),
        out_shape=(jax.ShapeDtypeStruct((t, nh * dv), F32), jax.ShapeDtypeStruct((nh, nc, dv, dk), F32)),
        scratch_shapes=[pltpu.VMEM((nh, dv, dk), F32)],
        compiler_params=_rowwise_params("arbitrary"),
    )(q, k, v, la)


def _gla_bwd(name, q, k, v, la, states, do, nh):
    t = q.shape[0]
    dk, dv = q.shape[1] // nh, v.shape[1] // nh
    nc = t // CHUNK
    scale = dk ** -0.5

    def body(q_ref, k_ref, v_ref, g_ref, do_ref, st_ref, sp_ref, dq_ref, dk_ref, dv_ref, dg_ref, carry):
        i = pl.program_id(0)

        @pl.when(i == 0)
        def _():
            carry[...] = jnp.zeros_like(carry)

        g = g_ref[...]
        b = _tri_dot(_tri(True), g)
        b_end = jnp.sum(g, axis=0, keepdims=True)
        w = jnp.exp(b_end - b)
        decay = jnp.exp(b_end)
        k_dec = k_ref[...] * w
        k_decb = k_dec.astype(BF16)
        qc = (q_ref[...] * scale).astype(BF16)
        dob = do_ref[...].astype(BF16)
        vb = v_ref[...].astype(BF16)
        dk_dec, ddecay = [], []
        for h in range(nh):
            ks, vs = slice(h * dk, (h + 1) * dk), slice(h * dv, (h + 1) * dv)
            dq_ref[:, ks] = jnp.dot(dob[:, vs], st_ref[h].astype(BF16), preferred_element_type=F32) * scale
            g_t = carry[h] + lax.dot_general(dob[:, vs], qc[:, ks], TN, preferred_element_type=F32)
            g_tb = g_t.astype(BF16)
            dk_dec.append(jnp.dot(vb[:, vs], g_tb, preferred_element_type=F32))
            dv_ref[:, vs] = lax.dot_general(k_decb[:, ks], g_tb, NT, preferred_element_type=F32)
            prev = jnp.where(i < nc - 1, sp_ref[h], 0.0)
            ddecay.append(jnp.sum(g_t * prev, axis=0, keepdims=True))
            carry[h] = g_t * decay[:, ks]
        dk_dec = jnp.concatenate(dk_dec, axis=1)
        dk_ref[...] = dk_dec * w
        e = dk_dec * k_dec
        db_end = jnp.sum(e, axis=0, keepdims=True) + jnp.concatenate(ddecay, axis=1) * decay
        dg_ref[...] = _tri_dot(_tri(False), -e) + db_end

    kblk = pl.BlockSpec((CHUNK, nh * dk), lambda i: (nc - 1 - i, 0))
    vblk = pl.BlockSpec((CHUNK, nh * dv), lambda i: (nc - 1 - i, 0))
    ksds = jax.ShapeDtypeStruct((t, nh * dk), F32)
    return _chained(
        body, name=name, grid=(nc,),
        in_specs=[kblk, kblk, vblk, kblk, vblk,
                  pl.BlockSpec((nh, None, dv, dk), lambda i: (0, nc - 1 - i, 0, 0)),
                  pl.BlockSpec((nh, None, dv, dk), lambda i: (0, jnp.maximum(nc - 2 - i, 0), 0, 0))],
        out_specs=(kblk, kblk, vblk, kblk),
        out_shape=(ksds, ksds, jax.ShapeDtypeStruct((t, nh * dv), F32), ksds),
        scratch_shapes=[pltpu.VMEM((nh, dv, dk), F32)],
        compiler_params=_rowwise_params("arbitrary"),
    )(q, k, v, la, do, states, states)


def _adamw(name, w, m, v, parts):
    rows, cols = w.shape
    tm = 1 << int(math.log2(max(8, (1 << 18) // (-(-cols // LANES) * LANES))))
    while rows % tm and tm > 8:
        tm //= 2
    tm = _tile(rows, tm)

    def body(w_ref, m_ref, v_ref, p_ref, g_ref, d_ref, nm_ref, nv_ref):
        g = p_ref[0].astype(F32)
        for s in range(1, parts.shape[0]):
            g = g + p_ref[s].astype(F32)
        m_new = ADAM_B1 * m_ref[...] + (1.0 - ADAM_B1) * g
        v_new = ADAM_B2 * v_ref[...] + (1.0 - ADAM_B2) * jnp.square(g)
        m_hat = m_new / (1.0 - ADAM_B1 ** ADAM_STEP)
        v_hat = v_new / (1.0 - ADAM_B2 ** ADAM_STEP)
        g_ref[...] = g
        d_ref[...] = -ADAM_LR * (m_hat / (jnp.sqrt(v_hat) + ADAM_EPS) + ADAM_WD * w_ref[...])
        nm_ref[...] = m_new
        nv_ref[...] = v_new

    blk = pl.BlockSpec((tm, cols), lambda i: (i, 0))
    sds = jax.ShapeDtypeStruct((rows, cols), F32)
    return _chained(
        body, name=name, grid=(rows // tm,),
        in_specs=[blk, blk, blk, pl.BlockSpec((parts.shape[0], tm, cols), lambda i: (0, i, 0))],
        out_specs=(blk, blk, blk, blk), out_shape=(sds, sds, sds, sds),
        compiler_params=_rowwise_params("parallel"),
    )(*_streamed(w, m, v, parts))


HBM = pl.BlockSpec(memory_space=pl.ANY)
MESH = pl.DeviceIdType.MESH


def _all_gather(name, shards):
    n = len(shards)

    def body(*refs):
        ins, outs = refs[:n], refs[n:2 * n]
        send_sems, recv_sems, local_sems = refs[2 * n:]
        x, y, c = lax.axis_index("x"), lax.axis_index("y"), lax.axis_index("c")
        me, sibling = (x, y, c), (x, y, 1 - c)
        chips = [(1 - x, y), (x, 1 - y), (1 - x, 1 - y)]

        def copy(w, k, block, to, src=None):
            dst = outs[w].at[4 * block[0] + 2 * block[1] + block[2]]
            return pltpu.make_async_remote_copy(
                src_ref=dst if src is None else src, dst_ref=dst, send_sem=send_sems.at[7 * w + k],
                recv_sem=recv_sems.at[7 * w + k], device_id=to, device_id_type=MESH)

        mine, first, passed = [], [], []
        for w in range(n):
            cp = pltpu.make_async_copy(ins[w], outs[w].at[4 * x + 2 * y + c], local_sems.at[w])
            cp.start()
            mine.append(cp)
            first.append(copy(w, 0, me, sibling, src=ins[w]))
            first += [copy(w, 1 + j, me, (*chip, c), src=ins[w]) for j, chip in enumerate(chips)]
        for cp in first:
            cp.start()
        for w in range(n):
            for j, chip in enumerate(chips):
                copy(w, 1 + j, (*chip, c), me).wait_recv()
                cp = copy(w, 4 + j, (*chip, c), sibling)
                cp.start()
                passed.append(cp)
        for w in range(n):
            copy(w, 0, sibling, me).wait_recv()
            for j, chip in enumerate(chips):
                copy(w, 4 + j, (*chip, 1 - c), me).wait_recv()
        for cp in first + passed:
            cp.wait_send()
        for cp in mine:
            cp.wait()

    return _chained(
        body, name=name, in_specs=[HBM] * n, out_specs=[HBM] * n,
        out_shape=[jax.ShapeDtypeStruct((N_DEV,) + s.shape, s.dtype) for s in shards],
        scratch_shapes=[pltpu.SemaphoreType.DMA((7 * n,)), pltpu.SemaphoreType.DMA((7 * n,)),
                        pltpu.SemaphoreType.DMA((n,))],
    )(*shards)


HBM_SPEC = pl.BlockSpec(memory_space=pltpu.HBM)
SEM_SPEC = pl.BlockSpec(memory_space=pltpu.SEMAPHORE)
EFFECT = pltpu.SideEffectType.DATAFLOW_SIDE_EFFECTING
TOKEN = jax.ShapeDtypeStruct((8, LANES), F32)


def _in_hbm(a):
    return pltpu.with_memory_space_constraint(a, pltpu.HBM)


def _place():
    x, y, c = lax.axis_index("x"), lax.axis_index("y"), lax.axis_index("c")
    chips = [(1 - x, y), (x, 1 - y), (1 - x, 1 - y)]
    return x, y, c, chips


def _block_of(px, py, pc):
    return 4 * px + 2 * py + pc


def _gather_copies(ins, outs, sems, w):
    send_sems, recv_sems, local_sems = sems
    x, y, c, chips = _place()
    peers = [(x, y, 1 - c)] + [(*chip, c) for chip in chips]

    def copy(k, block, to, src):
        dst = outs[w].at[_block_of(*block)]
        return pltpu.make_async_remote_copy(src_ref=dst if src is None else src, dst_ref=dst, send_sem=send_sems.at[4 * w + k],
                                            recv_sem=recv_sems.at[4 * w + k], device_id=to, device_id_type=MESH)

    local = pltpu.make_async_copy(ins[w], outs[w].at[_block_of(x, y, c)], local_sems.at[w])
    sends = [copy(k, (x, y, c), peer, ins[w]) for k, peer in enumerate(peers)]
    recvs = [copy(k, peer, (x, y, c), None) for k, peer in enumerate(peers)]
    return local, sends, recvs


def _forward_copies(outs, sems, w):
    send_sems, recv_sems = sems
    x, y, c, chips = _place()

    def copy(j, block, to):
        dst = outs[w].at[_block_of(*block)]
        return pltpu.make_async_remote_copy(src_ref=dst, dst_ref=dst, send_sem=send_sems.at[3 * w + j],
                                            recv_sem=recv_sems.at[3 * w + j], device_id=to, device_id_type=MESH)

    sends = [copy(j, (*chip, c), (x, y, 1 - c)) for j, chip in enumerate(chips)]
    recvs = [copy(j, (*chip, 1 - c), (x, y, c)) for j, chip in enumerate(chips)]
    return sends, recvs


def _gather_start(name, shards):
    n = len(shards)

    def body(*refs):
        ins, outs, sems = refs[:n], refs[n:2 * n], refs[2 * n:2 * n + 3]
        for w in range(n):
            local, sends, _ = _gather_copies(ins, outs, sems, w)
            for cp in sends + [local]:
                cp.start()
        refs[-1][...] = jnp.zeros_like(refs[-1])

    lands = [lax.empty((N_DEV,) + s.shape, s.dtype) for s in shards]
    res = _chained(
        body, name=name, link=-1, in_specs=[HBM_SPEC] * (2 * n),
        out_specs=[SEM_SPEC] * 3 + [HBM_SPEC] * (2 * n) + [pl.BlockSpec(memory_space=pltpu.VMEM)],
        out_shape=[pltpu.SemaphoreType.DMA((4 * n,)), pltpu.SemaphoreType.DMA((4 * n,)), pltpu.SemaphoreType.DMA((n,))]
        + [pltpu.HBM(s.shape, s.dtype) for s in shards] + [pltpu.HBM(l.shape, l.dtype) for l in lands] + [TOKEN],
        input_output_aliases={i: 3 + i for i in range(2 * n)},
        compiler_params=pltpu.CompilerParams(has_side_effects=EFFECT),
    )(*[_in_hbm(s) for s in shards], *[_in_hbm(l) for l in lands])
    return n, res[:3], res[3:3 + n], res[3 + n:3 + 2 * n]


def _gather_mid(name, state):
    n, sems, shards, lands = state

    def body(*refs):
        ins, outs, sems_in = refs[:n], refs[n:2 * n], refs[2 * n:2 * n + 3]
        sems_out = refs[2 * n + 3:2 * n + 5]
        for w in range(n):
            local, sends, recvs = _gather_copies(ins, outs, sems_in, w)
            local.wait()
            for cp in sends:
                cp.wait_send()
            for cp in recvs:
                cp.wait_recv()
            for cp in _forward_copies(outs, sems_out, w)[0]:
                cp.start()
        refs[-1][...] = jnp.zeros_like(refs[-1])

    res = _chained(
        body, name=name, link=-1, in_specs=[HBM_SPEC] * (2 * n) + [SEM_SPEC] * 3,
        out_specs=[SEM_SPEC] * 2 + [HBM_SPEC] * n + [pl.BlockSpec(memory_space=pltpu.VMEM)],
        out_shape=[pltpu.SemaphoreType.DMA((3 * n,)), pltpu.SemaphoreType.DMA((3 * n,))]
        + [pltpu.HBM(l.shape, l.dtype) for l in lands] + [TOKEN],
        input_output_aliases={n + i: 2 + i for i in range(n)},
        compiler_params=pltpu.CompilerParams(has_side_effects=EFFECT),
    )(*shards, *lands, *sems)
    return n, res[:2], res[2:2 + n]


def _gather_end(name, state):
    n, sems, lands = state

    def body(*refs):
        outs, sems_in = refs[:n], refs[n:n + 2]
        for w in range(n):
            sends, recvs = _forward_copies(outs, sems_in, w)
            for cp in sends:
                cp.wait_send()
            for cp in recvs:
                cp.wait_recv()

    return _chained(
        body, name=name, in_specs=[HBM_SPEC] * n + [SEM_SPEC] * 2, out_specs=[HBM_SPEC] * n,
        out_shape=[pltpu.HBM(l.shape, l.dtype) for l in lands], input_output_aliases={i: i for i in range(n)},
        compiler_params=pltpu.CompilerParams(has_side_effects=EFFECT),
    )(*lands, *sems)


def _exchange_copies(ins, outs, sems, w):
    send_sems, recv_sems, local_sems = sems
    x, y, c, _ = _place()
    mine = _block_of(x, y, c)
    local = pltpu.make_async_copy(ins[w].at[mine], outs[w].at[mine], local_sems.at[w])
    remote = []
    for k in range(1, N_DEV):
        px, py, pc = x ^ (k >> 2), y ^ ((k >> 1) & 1), c ^ (k & 1)
        remote.append(pltpu.make_async_remote_copy(
            src_ref=ins[w].at[_block_of(px, py, pc)], dst_ref=outs[w].at[mine], send_sem=send_sems.at[7 * w + k - 1],
            recv_sem=recv_sems.at[7 * w + k - 1], device_id=(px, py, pc), device_id_type=MESH))
    return local, remote


def _exchange_start(name, stacks):
    n = len(stacks)

    def body(*refs):
        ins, outs, sems = refs[:n], refs[n:2 * n], refs[2 * n:2 * n + 3]
        for w in range(n):
            local, remote = _exchange_copies(ins, outs, sems, w)
            for cp in remote + [local]:
                cp.start()
        refs[-1][...] = jnp.zeros_like(refs[-1])

    lands = [lax.empty(s.shape, s.dtype) for s in stacks]
    if any(s is _Chain.last for s in stacks):
        _Chain.last = None
    res = _chained(
        body, name=name, link=-1, in_specs=[HBM_SPEC] * (2 * n),
        out_specs=[SEM_SPEC] * 3 + [HBM_SPEC] * (2 * n) + [pl.BlockSpec(memory_space=pltpu.VMEM)],
        out_shape=[pltpu.SemaphoreType.DMA((7 * n,)), pltpu.SemaphoreType.DMA((7 * n,)), pltpu.SemaphoreType.DMA((n,))]
        + [pltpu.HBM(s.shape, s.dtype) for s in stacks] * 2 + [TOKEN],
        input_output_aliases={i: 3 + i for i in range(2 * n)},
        compiler_params=pltpu.CompilerParams(has_side_effects=EFFECT),
    )(*[_in_hbm(s) for s in stacks], *[_in_hbm(l) for l in lands])
    return n, res[:3], res[3:3 + n], res[3 + n:3 + 2 * n]


def _exchange_wait(name, state):
    n, sems, stacks, lands = state

    def body(*refs):
        ins, outs, sems_in = refs[:n], refs[n:2 * n], refs[2 * n:2 * n + 3]
        for w in range(n):
            local, remote = _exchange_copies(ins, outs, sems_in, w)
            local.wait()
            for cp in remote:
                cp.wait_send()
                cp.wait_recv()

    return _chained(
        body, name=name, in_specs=[HBM_SPEC] * (2 * n) + [SEM_SPEC] * 3, out_specs=[HBM_SPEC] * n,
        out_shape=[pltpu.HBM(l.shape, l.dtype) for l in lands], input_output_aliases={n + i: i for i in range(n)},
        compiler_params=pltpu.CompilerParams(has_side_effects=EFFECT),
    )(*stacks, *lands, *sems)


N_CHIP = N_DEV // 2


def _pair_copies(stack, land, sems):
    send_sems, recv_sems, _ = sems
    x, y, c, _ = _place()
    remote = [pltpu.make_async_remote_copy(src_ref=stack.at[2 * k + 1 - c], dst_ref=land.at[k], send_sem=send_sems.at[k],
                                           recv_sem=recv_sems.at[k], device_id=(x, y, 1 - c), device_id_type=MESH)
              for k in range(N_CHIP)]
    return [], remote


def _chip_copies(pairs, land, sems):
    send_sems, recv_sems, local_sems = sems
    x, y, c, _ = _place()
    mine = 2 * x + y
    local = pltpu.make_async_copy(pairs.at[mine], land.at[mine], local_sems.at[0])
    remote = []
    for m in range(1, N_CHIP):
        px, py = x ^ (m >> 1), y ^ (m & 1)
        remote.append(pltpu.make_async_remote_copy(
            src_ref=pairs.at[2 * px + py], dst_ref=land.at[mine], send_sem=send_sems.at[m - 1],
            recv_sem=recv_sems.at[m - 1], device_id=(px, py, c), device_id_type=MESH))
    return local, remote


def _stage_start(name, copies_of, src_arr, n_land, n_sems):
    half = (N_CHIP,) + src_arr.shape[1:]

    def body(*refs):
        local, remote = copies_of(refs[0], *refs[1:1 + n_land], refs[1 + n_land:4 + n_land])
        for cp in remote + (local if isinstance(local, list) else [local]):
            cp.start()
        refs[-1][...] = jnp.zeros_like(refs[-1])

    if src_arr is _Chain.last:
        _Chain.last = None
    lands = [lax.empty(half, src_arr.dtype) for _ in range(n_land)]
    res = _chained(
        body, name=name, link=-1, in_specs=[HBM_SPEC] * (1 + n_land),
        out_specs=[SEM_SPEC] * 3 + [HBM_SPEC] * (1 + n_land) + [pl.BlockSpec(memory_space=pltpu.VMEM)],
        out_shape=[pltpu.SemaphoreType.DMA((n,)) for n in n_sems] + [pltpu.HBM(src_arr.shape, src_arr.dtype)]
        + [pltpu.HBM(half, src_arr.dtype)] * n_land + [TOKEN],
        input_output_aliases={i: 3 + i for i in range(1 + n_land)},
        compiler_params=pltpu.CompilerParams(has_side_effects=EFFECT),
    )(_in_hbm(src_arr), *[_in_hbm(l) for l in lands])
    return res[:3], res[3], res[4:4 + n_land]


def _stage_wait(name, copies_of, state):
    sems, src_arr, lands = state
    n_land = len(lands)

    def body(*refs):
        local, remote = copies_of(refs[0], *refs[1:1 + n_land], refs[1 + n_land:4 + n_land])
        for cp in (local if isinstance(local, list) else [local]):
            cp.wait()
        for cp in remote:
            cp.wait_send()
            cp.wait_recv()

    return _chained(
        body, name=name, in_specs=[HBM_SPEC] * (1 + n_land) + [SEM_SPEC] * 3, out_specs=[HBM_SPEC] * n_land,
        out_shape=[pltpu.HBM(l.shape, l.dtype) for l in lands],
        input_output_aliases={1 + i: i for i in range(n_land)},
        compiler_params=pltpu.CompilerParams(has_side_effects=EFFECT),
    )(src_arr, *lands, *sems)


def _pair_sum(name, stack, land, tm=64):
    n, rows, cols = land.shape
    tm = _tile(rows, tm)

    def body(c_ref, a_ref, b_ref, o_ref):
        o_ref[...] = (a_ref[...].astype(F32) + b_ref[...].astype(F32)).astype(o_ref.dtype)

    core = lax.axis_index("c").astype(jnp.int32).reshape(1)
    blk = pl.BlockSpec((n, tm, cols), lambda i, c_ref: (0, i, 0))
    mine = pl.BlockSpec((n, None, tm, cols), lambda i, c_ref: (0, c_ref[0], i, 0))
    _Chain.last = pl.pallas_call(
        body, name=name,
        grid_spec=pltpu.PrefetchScalarGridSpec(num_scalar_prefetch=1, grid=(rows // tm,), in_specs=[mine, blk], out_specs=blk),
        out_shape=jax.ShapeDtypeStruct(land.shape, land.dtype), compiler_params=_rowwise_params("parallel"),
    )(core, stack.reshape(n, 2, rows, cols), land)
    return _Chain.last


def _unstack_cols(w):
    return w.transpose(1, 0, 2).reshape(w.shape[1], N_DEV * w.shape[2])


def _stack_cols(w):
    return w.reshape(w.shape[0], N_DEV, w.shape[1] // N_DEV).transpose(1, 0, 2)


def _rope_tables(positions):
    half = MLA_ROPE // 2
    inv_freq = ROPE_THETA ** (-jnp.arange(half, dtype=F32) / half)
    ang = positions.astype(F32)[:, None] * inv_freq
    cos, sin = jnp.cos(ang), jnp.sin(ang)
    t = positions.shape[0]
    cos_t = jnp.concatenate([jnp.ones((t, MLA_NOPE), F32), cos, cos], axis=1)
    sin_t = jnp.concatenate([jnp.zeros((t, MLA_NOPE), F32), -sin, sin], axis=1)
    idx = jnp.arange(MLA_QK)
    partner = jnp.where(idx < MLA_NOPE, -1, jnp.where(idx < MLA_NOPE + half, idx + half, idx - half))
    swap = (idx[:, None] == partner[None, :]).astype(BF16)
    return cos_t, sin_t, swap


def _ffn_fwd(tag, x, gain, wt):
    h = _rms_fwd(tag + "_rms", x, gain, BF16)
    g = _mm_stack_nt_out(tag + "_gate", h, wt[tag + '_w_gate'], BF16)
    u, a = _mm_stack_nt_out(tag + "_up", h, wt[tag + '_w_up'], BF16,
                            fuse=(lambda u_blk, g_blk: (u_blk, g_blk * jax.nn.sigmoid(g_blk) * u_blk), [g]))
    y = _mm_stack_sum(tag + "_down", a, wt[tag + '_w_down'], F32, scale=0.5, res=x, tm=1024)
    return y, (x, h, g, u, a)


def _ffn_bwd_weights(tag, dy, saved, wd, comm):
    x, h, g, u, a = saved
    comm.grads({tag + '_w_down': _mm_stack_tn_right(tag + "_dwd", a, dy, BF16, scale=0.5)})

    def dact(da, g_blk, u_blk):
        sig = jax.nn.sigmoid(g_blk)
        return da * u_blk * sig * (1.0 + g_blk * (1.0 - sig)), da * g_blk * sig

    dg, du = _mm_stack_nt_out(tag + "_da", dy, wd, BF16, scale=0.5, fuse=(dact, [g, u]))
    comm.advance()
    comm.grads({tag + '_w_gate': _mm_stack_tn_right(tag + "_dwg", dg, h, BF16)})
    dwu = _mm_stack_tn_right(tag + "_dwu", du, h, BF16)
    comm.advance()
    comm.grads({tag + '_w_up': dwu})
    return dg, du


def _ffn_bwd_input(tag, dy, saved, dgu, gain, wg, wu, comm):
    dg, du = dgu
    dh = _mm_stack_sum(tag + "_dh", (dg, du), (wg, wu), F32)
    comm.advance()
    return _rms_bwd(tag + "_drms", saved[0], gain, dh, res=dy)


def _local_step(x, mem, positions, sm, comm, target):
    t, d = x.shape
    nm = mem.shape[0]
    gs, gw = {}, {}

    class Weights(dict):
        def __missing__(self, name):
            self.update(comm.weights(next(k for k, names in enumerate(GATHERS) if name in names)))
            return self[name]

    wt = Weights()

    x1, ffn1_saved = _ffn_fwd("ffn1", x, sm['ffn1_norm'], wt)

    h2 = _rms_fwd("mix_rms", x1, sm['mix_norm'], BF16)
    w_in_ref = _unstack_cols(wt['w_in'])
    pieces = []
    for name, n in IN_PAD:
        piece = w_in_ref[:, REF_OFF[name]:REF_OFF[name] + REF_SIZE[name]]
        if n != REF_SIZE[name]:
            piece = jnp.pad(piece, ((0, 0), (0, n - REF_SIZE[name])))
        pieces.append(piece)
    w_in = jnp.concatenate(pieces, axis=1)
    z = _mm2("mix_in", h2, w_in, NN, F32)
    zs = {name: z[:, PAD_OFF[name]:PAD_OFF[name] + n] for name, n in IN_PAD}

    cq = _rms_fwd("mla_q_a_rms", zs['zq'], sm['q_a_norm'], BF16)
    q_raw = _mm_stack_nt_out("mla_q_up", cq, wt['w_q_up'], F32)
    ckv = _rms_fwd("mla_kv_a_rms", zs['zkv'], sm['kv_a_norm'], BF16)
    kv = _mm_stack_out("mla_kv_up", ckv, wt['w_kv_up'], F32)
    zkr = zs['zkr'][:, :MLA_ROPE]
    k_raw = jnp.concatenate([kv[:, :, :MLA_NOPE], jnp.broadcast_to(zkr[None], (MLA_HEADS, t, MLA_ROPE))], axis=2)
    v_mla = kv[:, :, MLA_NOPE:].astype(BF16)
    cos_t, sin_t, swap = _rope_tables(positions)
    q_raw2, k_raw2 = q_raw.reshape(MLA_HEADS * t, MLA_QK), k_raw.reshape(MLA_HEADS * t, MLA_QK)
    rope = (cos_t, sin_t, swap)
    qf = _rms_fwd("mla_q_rms", q_raw2, sm['mla_q_norm'], BF16, rope=rope, scale=MLA_QK ** -0.5).reshape(MLA_HEADS, t, MLA_QK)
    kf = _rms_fwd("mla_k_rms", k_raw2, sm['mla_k_norm'], BF16, rope=rope).reshape(MLA_HEADS, t, MLA_QK)
    o_mla = _attn_fwd("mla_attn", qf, kf, v_mla, True, tq=512)

    w_g2 = jnp.pad(_unstack_cols(wt['gla_w_gate2']), ((0, LANES - GLA_GATE_RANK), (0, 0)))
    pre = _mm2("gla_gate_pre", zs['zg'], w_g2, NN, F32)
    log_a = _gate_fwd("gla_gate", pre, sm['gla_b_gate'])
    o_gla_raw, states = _gla_fwd("gla_scan", zs['gq'], zs['gk'], zs['gv'], log_a, GLA_HEADS)
    o_gla_n = _rms_fwd("gla_out_rms", o_gla_raw, sm['gla_out_norm'], F32)
    o_gla = _swiglu_fwd("gla_out_gate", zs['zr'], o_gla_n, BF16)

    cat = jnp.concatenate([o_mla, o_gla], axis=1)
    w_out = wt['w_out'].reshape(d, d)
    x2 = _mm2("mix_out", cat, w_out, NN, F32, res=x1)

    w_mq, w_mk, w_mv = (wt[n].reshape(d, MEM_HEADS * MEM_HEAD_DIM) for n in ('mem_w_q', 'mem_w_k', 'mem_w_v'))
    hq = _rms_fwd("mem_attn_rms", x2, sm['mem_attn_norm'], BF16)
    hm = _rms_fwd("mem_rms", mem, sm['mem_norm'], BF16)

    def heads_out(name, a, b, out_dtype):
        m, kk = a.shape
        tm = _tile(m, 512)
        return _mm(name, a, b, (m // tm, MEM_HEADS, 1), ((tm, kk), lambda i, h, k: (i, 0)),
                   ((kk, MEM_HEAD_DIM), lambda i, h, k: (0, h)), ((None, tm, MEM_HEAD_DIM), lambda i, h, k: (h, i, 0)),
                   (MEM_HEADS, m, MEM_HEAD_DIM), out_dtype, NN)

    mq_raw = heads_out("mem_q", hq, w_mq, F32)
    mk_raw = heads_out("mem_k", hm, w_mk, F32)
    mv = heads_out("mem_v", hm, w_mv, BF16)
    mq = _rms_fwd("mem_q_rms", mq_raw.reshape(MEM_HEADS * t, MEM_HEAD_DIM), sm['mem_q_norm'], BF16, scale=MEM_HEAD_DIM ** -0.5)
    mk = _rms_fwd("mem_k_rms", mk_raw.reshape(MEM_HEADS * nm, MEM_HEAD_DIM), sm['mem_k_norm'], BF16)
    mq, mk = mq.reshape(MEM_HEADS, t, MEM_HEAD_DIM), mk.reshape(MEM_HEADS, nm, MEM_HEAD_DIM)
    o_mem = _attn_fwd("mem_attn", mq, mk, mv, False)
    w_mo = wt['mem_w_o']
    mo_cols = w_mo.shape[2]
    tm = _tile(t, 512)
    x3 = _mm("mem_out", o_mem, w_mo, (t // tm, N_DEV, 1), ((tm, o_mem.shape[1]), lambda i, j, k: (i, 0)),
             ((None, o_mem.shape[1], mo_cols), lambda i, j, k: (j, 0, 0)), ((tm, mo_cols), lambda i, j, k: (i, j)),
             (t, d), F32, NN, res=x2)

    y, ffn2_saved = _ffn_fwd("ffn2", x3, sm['ffn2_norm'], wt)
    dy, dy_narrow, loss_lanes = _loss("loss", y, target)

    dgu = _ffn_bwd_weights("ffn2", dy_narrow, ffn2_saved, wt['ffn2_w_down'], comm)
    dx3, gs['ffn2_norm'] = _ffn_bwd_input("ffn2", dy, ffn2_saved, dgu, sm['ffn2_norm'],
                                          wt['ffn2_w_gate'], wt['ffn2_w_up'], comm)

    do_mem = _mm("mem_do", dx3, w_mo, (t // tm, 1, N_DEV), ((tm, mo_cols), lambda i, h, k: (i, k)),
                 ((None, w_mo.shape[1], mo_cols), lambda i, h, k: (k, 0, 0)), ((tm, w_mo.shape[1]), lambda i, h, k: (i, 0)),
                 (t, MEM_HEADS * MEM_HEAD_DIM), BF16, NT)
    tk = _tile(t, 512)
    gw['mem_w_o'] = _mm("mem_dwo", o_mem, dx3, (N_DEV, 1, t // tk), ((tk, o_mem.shape[1]), lambda j, i, k: (k, 0)),
                        ((tk, mo_cols), lambda j, i, k: (k, j)), ((None, o_mem.shape[1], mo_cols), lambda j, i, k: (j, 0, 0)),
                        w_mo.shape, BF16, TN)
    dmq, dmk, dmv = _attn_bwd("mem_dattn", mq, mk, mv, do_mem, False)
    dmq_raw, gs['mem_q_norm'] = _rms_bwd("mem_q_drms", mq_raw.reshape(MEM_HEADS * t, MEM_HEAD_DIM), sm['mem_q_norm'],
                                         dmq.reshape(MEM_HEADS * t, MEM_HEAD_DIM), scale=MEM_HEAD_DIM ** -0.5)
    dmk_raw, gs['mem_k_norm'] = _rms_bwd("mem_k_drms", mk_raw.reshape(MEM_HEADS * nm, MEM_HEAD_DIM), sm['mem_k_norm'],
                                         dmk.reshape(MEM_HEADS * nm, MEM_HEAD_DIM))
    dmq_raw = dmq_raw.reshape(MEM_HEADS, t, MEM_HEAD_DIM)
    dmk_raw = dmk_raw.reshape(MEM_HEADS, nm, MEM_HEAD_DIM)

    def heads_in_nt(name, a, b, res=None):
        m, n = a.shape[1], b.shape[0]
        tm_, tn_ = _tile(m, 512), _tile(n, 1024)
        return _mm(name, a, b, (m // tm_, n // tn_, MEM_HEADS), ((None, tm_, MEM_HEAD_DIM), lambda i, j, k: (k, i, 0)),
                   ((tn_, MEM_HEAD_DIM), lambda i, j, k: (j, k)), ((tm_, tn_), lambda i, j, k: (i, j)), (m, n), F32, NT,
                   None, res)

    def heads_tn(name, a, b):
        m, kp = a.shape
        tm_, tk_ = _tile(kp, 1024), _tile(m, 512)
        return _mm(name, a, b, (kp // tm_, MEM_HEADS, m // tk_), ((tk_, tm_), lambda i, h, k: (k, i)),
                   ((None, tk_, MEM_HEAD_DIM), lambda i, h, k: (h, k, 0)), ((tm_, MEM_HEAD_DIM), lambda i, h, k: (i, h)),
                   (kp, MEM_HEADS * MEM_HEAD_DIM), BF16, TN)

    dhq = heads_in_nt("mem_dhq", dmq_raw, w_mq)
    gw['mem_w_q'] = heads_tn("mem_dwq", hq, dmq_raw).reshape(wt['mem_w_q'].shape)
    dhm = heads_in_nt("mem_dhm_k", dmk_raw, w_mk)
    dhm = heads_in_nt("mem_dhm_v", dmv, w_mv, res=dhm)
    gw['mem_w_k'] = heads_tn("mem_dwk", hm, dmk_raw).reshape(wt['mem_w_k'].shape)
    gw['mem_w_v'] = heads_tn("mem_dwv", hm, dmv).reshape(wt['mem_w_v'].shape)
    _, gs['mem_norm'] = _rms_bwd("mem_drms", mem, sm['mem_norm'], dhm)
    comm.grads({n: gw[n] for n in MEMORY})
    dx2, gs['mem_attn_norm'] = _rms_bwd("mem_attn_drms", x2, sm['mem_attn_norm'], dhq, res=dx3)

    dcat = _mm2("mix_dcat", dx2, w_out, NT, F32)
    comm.grads({'w_out': _mm2("mix_dwout", cat, dx2, TN, BF16, tm=1024, tn=2048, tk=2048).reshape(wt['w_out'].shape)})
    do_mla, do_gla = dcat[:, :MLA_HEADS * MLA_V], dcat[:, MLA_HEADS * MLA_V:]

    dzr, dgn = _swiglu_bwd("gla_out_dgate", do_gla, zs['zr'], o_gla_n, F32)
    do_gla_raw, gs['gla_out_norm'] = _rms_bwd("gla_out_drms", o_gla_raw, sm['gla_out_norm'], dgn)
    dgq, dgk, dgv, dlog_a = _gla_bwd("gla_dscan", zs['gq'], zs['gk'], zs['gv'], log_a, states, do_gla_raw, GLA_HEADS)
    dpre, gs['gla_b_gate'] = _gate_bwd("gla_dgate", pre, sm['gla_b_gate'], dlog_a)
    dw_g2 = _mm2("gla_dwgate", zs['zg'], dpre, TN, BF16, tk=512)
    comm.grads({'gla_w_gate2': _stack_cols(dw_g2[:GLA_GATE_RANK])})
    dzg = _mm2("gla_dzg", dpre, w_g2, NT, F32)

    dqf, dkf, dv_mla = _attn_bwd("mla_dattn", qf, kf, v_mla, do_mla, True, tq=512)
    dq_raw, gs['mla_q_norm'] = _rms_bwd("mla_q_drms", q_raw2, sm['mla_q_norm'], dqf.reshape(MLA_HEADS * t, MLA_QK), rope=rope,
                                         scale=MLA_QK ** -0.5)
    dk_raw, gs['mla_k_norm'] = _rms_bwd("mla_k_drms", k_raw2, sm['mla_k_norm'], dkf.reshape(MLA_HEADS * t, MLA_QK), rope=rope)
    dq_raw = dq_raw.reshape(MLA_HEADS, t, MLA_QK)
    dk_raw = dk_raw.reshape(MLA_HEADS, t, MLA_QK)
    dkv = jnp.concatenate([dk_raw[:, :, :MLA_NOPE], dv_mla], axis=2)
    dzkr = jnp.sum(dk_raw[:, :, MLA_NOPE:], axis=0)
    comm.grads({'w_q_up': _mm_stack_tn_right("mla_dwq", dq_raw, cq, BF16),
                'w_kv_up': _mm_stack_tn_left("mla_dwkv", ckv, dkv, BF16)})
    dcq = _mm_stack_sum("mla_dcq", dq_raw, wt['w_q_up'], F32)
    dckv = _mm_stack_nt_sum("mla_dckv", dkv, wt['w_kv_up'], F32)
    dzq, gs['q_a_norm'] = _rms_bwd("mla_q_a_drms", zs['zq'], sm['q_a_norm'], dcq)
    dzkv, gs['kv_a_norm'] = _rms_bwd("mla_kv_a_drms", zs['zkv'], sm['kv_a_norm'], dckv)

    dzs = {'zq': dzq, 'zkv': dzkv, 'gq': dgq, 'gk': dgk, 'gv': dgv, 'zr': dzr,
           'zkr': jnp.pad(dzkr, ((0, 0), (0, LANES - MLA_ROPE))), 'zg': dzg}
    dz = jnp.concatenate([dzs[name].astype(BF16) for name, _ in IN_PAD], axis=1)
    dw_in = _mm2("mix_dwin", h2, dz, TN, BF16, tm=1024, tn=2048, tk=2048)
    dw_in_ref = jnp.concatenate([dw_in[:, PAD_OFF[name]:PAD_OFF[name] + n] for name, n in IN_REF], axis=1)
    comm.grads({'w_in': _stack_cols(dw_in_ref)})
    dh2 = _mm2("mix_dh", dz, w_in, NT, F32)
    comm.advance()
    dx1, dx1_narrow, gs['mix_norm'] = _rms_bwd("mix_drms", x1, sm['mix_norm'], dh2, res=dx2, narrow=True)

    dgu = _ffn_bwd_weights("ffn1", dx1_narrow, ffn1_saved, wt['ffn1_w_down'], comm)
    grad_x, gs['ffn1_norm'] = _ffn_bwd_input("ffn1", dx1, ffn1_saved, dgu, sm['ffn1_norm'],
                                             wt['ffn1_w_gate'], wt['ffn1_w_up'], comm)
    return loss_lanes, grad_x, gs


def _pad_lanes(v):
    n = v.shape[1]
    return jnp.pad(v, ((0, 0), (0, -n % LANES)))


def _pack_small(vals):
    return jnp.concatenate([_pad_lanes(vals[n]) for n in SMALL], axis=1)


def _unpack_small(packed, like):
    out, off = {}, 0
    for n in SMALL:
        size = like[n].shape[1]
        out[n] = packed[:, off:off + size]
        off += size + (-size % LANES)
    return out


def kernel(x, mem, positions, ffn1_norm, ffn1_w_gate, ffn1_w_up, ffn1_w_down, mix_norm, w_in, q_a_norm, w_q_up, kv_a_norm, w_kv_up, mla_q_norm, mla_k_norm, gla_w_gate2, gla_b_gate, gla_out_norm, w_out, mem_attn_norm, mem_norm, mem_w_q, mem_w_k, mem_w_v, mem_w_o, mem_q_norm, mem_k_norm, ffn2_norm, ffn2_w_gate, ffn2_w_up, ffn2_w_down, loss_target, m_ffn1_norm, m_ffn1_w_gate, m_ffn1_w_up, m_ffn1_w_down, m_mix_norm, m_w_in, m_q_a_norm, m_w_q_up, m_kv_a_norm, m_w_kv_up, m_mla_q_norm, m_mla_k_norm, m_gla_w_gate2, m_gla_b_gate, m_gla_out_norm, m_w_out, m_mem_attn_norm, m_mem_norm, m_mem_w_q, m_mem_w_k, m_mem_w_v, m_mem_w_o, m_mem_q_norm, m_mem_k_norm, m_ffn2_norm, m_ffn2_w_gate, m_ffn2_w_up, m_ffn2_w_down, v_ffn1_norm, v_ffn1_w_gate, v_ffn1_w_up, v_ffn1_w_down, v_mix_norm, v_w_in, v_q_a_norm, v_w_q_up, v_kv_a_norm, v_w_kv_up, v_mla_q_norm, v_mla_k_norm, v_gla_w_gate2, v_gla_b_gate, v_gla_out_norm, v_w_out, v_mem_attn_norm, v_mem_norm, v_mem_w_q, v_mem_w_k, v_mem_w_v, v_mem_w_o, v_mem_q_norm, v_mem_k_norm, v_ffn2_norm, v_ffn2_w_gate, v_ffn2_w_up, v_ffn2_w_down):
    inp = dict(locals())
    x, mem, positions, target = inp['x'][0], inp['mem'][0], inp['positions'][0], inp['loss_target'][0]
    sm = {n: inp[n] for n in SMALL}
    out = {}

    def stored(key):
        name = key[2:] if key[:2] in ('m_', 'v_') else key
        return inp[key][0].T if name in TRANSPOSED else inp[key][0]

    def as_given(name, r):
        return r.T[None] if name in TRANSPOSED else r[None]

    class Comm:
        def __init__(self):
            self.gathers = {0: self.start(0)}
            self.forwards, self.exchanges, self.pairs = {}, [], []

        def start(self, k):
            return _gather_start(f"gather_start_{k}", [stored(n).astype(BF16) for n in GATHERS[k]])

        def forward(self, k):
            if k not in self.forwards:
                self.forwards[k] = _gather_mid(f"gather_mid_{k}", self.gathers[k])
                self.gathers.update({nxt: self.start(nxt) for nxt in NEXT_GATHERS.get(k, [])})

        def weights(self, k):
            self.forward(k)
            if k in EARLY_FORWARD:
                self.forward(EARLY_FORWARD[k])
            return dict(zip(GATHERS[k], _gather_end(f"gather_end_{k}", self.forwards[k])))

        def grads(self, stacks):
            names = list(stacks)
            if names[0] in TWO_STAGE:
                (n,) = names
                self.pairs.append((n, _stage_start("pair_start_" + n, _pair_copies, stacks[n], 1, (N_CHIP, N_CHIP, 1))))
            else:
                self.exchanges.append((names, _exchange_start("exchange_start_" + names[0], [stacks[n] for n in names])))

        def advance(self):
            for n, state in self.pairs:
                (land,) = _stage_wait("pair_wait_" + n, _pair_copies, state)
                pairs = _pair_sum("pair_sum_" + n, state[1], land)
                self.exchanges.append(([n], _stage_start("chip_start_" + n, _chip_copies, pairs, 1, (N_CHIP - 1, N_CHIP - 1, 1))))
            self.pairs = []

        def update(self, count):
            todo, self.exchanges = self.exchanges[:count], self.exchanges[count:]
            for names, state in todo:
                if names[0] in TWO_STAGE:
                    parts = _stage_wait("chip_wait_" + names[0], _chip_copies, state)
                else:
                    parts = _exchange_wait("exchange_wait_" + names[0], state)
                for n, p in zip(names, parts):
                    res = _adamw("adamw_" + n, stored(n), stored('m_' + n), stored('v_' + n), p)
                    for kind, r in zip(('grad_', 'delta_', 'new_m_', 'new_v_'), res):
                        out[kind + n] = as_given(n, r)

    _Chain.last = None
    comm = Comm()
    loss_lanes, grad_x, gs = _local_step(x, mem, positions, sm, comm, target)
    out['loss'] = lax.psum(jnp.sum(loss_lanes), ("x", "y", "c"))
    out['grad_x'] = grad_x[None]

    comm.update(len(comm.exchanges) - 3)
    small_parts = _all_gather("gather_small", [_pack_small(gs)])[0]
    res = _adamw("adamw_small", _pack_small(sm), _pack_small({n: inp['m_' + n] for n in SMALL}),
                 _pack_small({n: inp['v_' + n] for n in SMALL}), small_parts)
    for kind, r in zip(('grad_', 'delta_', 'new_m_', 'new_v_'), res):
        for n, val in _unpack_small(r, sm).items():
            out[kind + n] = val

    comm.update(3)

    names = ['loss', 'grad_x'] + [k + n for k in ('grad_', 'delta_', 'new_m_', 'new_v_') for n in WEIGHTS]
    return tuple(out[n] for n in names)
```

```python
import functools
import math

import jax
import jax.numpy as jnp
from jax import lax
from jax.experimental import pallas as pl
from jax.experimental.pallas import tpu as pltpu

F32 = jnp.float32
BF16 = jnp.bfloat16

N_DEV = 8
EPS = 1e-6
CHUNK = 64
MLA_HEADS, MLA_NOPE, MLA_ROPE, MLA_V = 8, 128, 64, 128
MLA_QK = MLA_NOPE + MLA_ROPE
MLA_Q_RANK, MLA_KV_RANK = 512, 256
ROPE_THETA = 10000.0
GLA_HEADS, GLA_DK, GLA_DV, GLA_GATE_RANK = 4, 128, 256, 16
GLA_TAU = 16.0
MEM_HEADS, MEM_HEAD_DIM = 4, 128
ADAM_LR, ADAM_B1, ADAM_B2, ADAM_EPS, ADAM_WD, ADAM_STEP = 0.001, 0.9, 0.999, 1e-08, 0.01, 10

V7X_VMEM_BYTES = 64 * 1024 * 1024
LANES = 128

NN = (((1,), (0,)), ((), ()))
NT = (((1,), (1,)), ((), ()))
TN = (((0,), (0,)), ((), ()))

WEIGHTS = ['ffn1_norm', 'ffn1_w_gate', 'ffn1_w_up', 'ffn1_w_down', 'mix_norm', 'w_in', 'q_a_norm', 'w_q_up',
           'kv_a_norm', 'w_kv_up', 'mla_q_norm', 'mla_k_norm', 'gla_w_gate2', 'gla_b_gate', 'gla_out_norm', 'w_out',
           'mem_attn_norm', 'mem_norm', 'mem_w_q', 'mem_w_k', 'mem_w_v', 'mem_w_o', 'mem_q_norm', 'mem_k_norm',
           'ffn2_norm', 'ffn2_w_gate', 'ffn2_w_up', 'ffn2_w_down']
SMALL = ['ffn1_norm', 'mix_norm', 'q_a_norm', 'kv_a_norm', 'mla_q_norm', 'mla_k_norm', 'gla_b_gate', 'gla_out_norm',
         'mem_attn_norm', 'mem_norm', 'mem_q_norm', 'mem_k_norm', 'ffn2_norm']
MIXER = ['w_in', 'w_q_up', 'w_kv_up', 'gla_w_gate2', 'w_out']
MEMORY = ['mem_w_q', 'mem_w_k', 'mem_w_v', 'mem_w_o']
TRANSPOSED = ['ffn1_w_gate', 'ffn1_w_up', 'ffn2_w_gate', 'ffn2_w_up', 'w_q_up']
TWO_STAGE = ['w_in', 'ffn1_w_down', 'ffn1_w_gate', 'ffn1_w_up']
GATHERS = [['ffn1_w_gate'], ['ffn1_w_up'], ['ffn1_w_down'], MIXER[:1], MIXER[1:], MEMORY,
           ['ffn2_w_gate', 'ffn2_w_up', 'ffn2_w_down']]
NEXT_GATHERS = {0: [1], 1: [2], 2: [3], 3: [4, 5, 6]}
EARLY_FORWARD = {5: 6}

IN_REF = [('zq', 512), ('zkv', 256), ('zkr', 64), ('gq', 512), ('gk', 512), ('gv', 1024), ('zg', 16), ('zr', 1024)]
IN_PAD = [('zq', 512), ('zkv', 256), ('gq', 512), ('gk', 512), ('gv', 1024), ('zr', 1024), ('zkr', 128), ('zg', 128)]
IN_WIDTH = sum(n for _, n in IN_REF)
IN_PAD_WIDTH = sum(n for _, n in IN_PAD)


def _offsets(layout):
    out, off = {}, 0
    for name, n in layout:
        out[name] = off
        off += n
    return out


REF_OFF, PAD_OFF = _offsets(IN_REF), _offsets(IN_PAD)
REF_SIZE = dict(IN_REF)


def _tile(n, pref):
    return pref if n % pref == 0 else n


def _block_bytes(blk, dtype):
    dims = [d for d in blk if d is not None]
    if len(dims) >= 1:
        dims[-1] = -(-dims[-1] // LANES) * LANES
    return math.prod(dims) * jnp.dtype(dtype).itemsize


def _vmem_limit(pipelined_bytes, resident_bytes=0):
    need = 2 * pipelined_bytes + resident_bytes + (8 << 20)
    return int(min(max(need, 32 << 20), V7X_VMEM_BYTES - (6 << 20)))


class _Chain:
    last = None


def _chained(body, *, in_specs, link=0, **kwargs):
    def call(*operands):
        dep = _Chain.last
        if dep is not None and any(o is dep for o in operands):
            dep = None
        if dep is None:
            res = pl.pallas_call(body, in_specs=in_specs, **kwargs)(*operands)
        else:
            n = len(operands)

            def chained_body(*refs):
                body(*refs[:n], *refs[n + 1:])

            res = pl.pallas_call(chained_body, in_specs=list(in_specs) + [pl.BlockSpec(memory_space=pl.ANY)],
                                 **kwargs)(*operands, dep)
        _Chain.last = res[link] if isinstance(res, (list, tuple)) else res
        return res

    return call


def _row_tile(rows, width, elements=1 << 18):
    lanes = -(-width // LANES) * LANES
    return _tile(rows, max(256, 1 << int(math.log2(max(1, elements // lanes)))))


def _streamed(*arrays):
    return [pltpu.with_memory_space_constraint(a, pltpu.HBM) for a in arrays]


def _rowwise_params(*semantics):
    return pltpu.CompilerParams(dimension_semantics=semantics, vmem_limit_bytes=48 << 20)


def _mm(name, a, b, grid, a_spec, b_spec, o_spec, out_shape, out_dtype, dims, scale=None, res=None, fuse=None):
    nk = grid[2]
    o_blk, o_map = o_spec
    acc_shape = tuple(d for d in o_blk if d is not None)
    extras = [res] if res is not None else (list(fuse[1]) if fuse else [])
    n_out = 2 if fuse else 1
    a_list, b_list = (list(a), list(b)) if isinstance(a, (tuple, list)) else ([a], [b])
    a_specs = a_spec if isinstance(a_spec, list) else [a_spec] * len(a_list)
    b_specs = b_spec if isinstance(b_spec, list) else [b_spec] * len(b_list)
    n_in = 2 * len(a_list)

    def body(*refs):
        a_refs, b_refs = refs[:n_in // 2], refs[n_in // 2:n_in]
        extra_refs = refs[n_in:n_in + len(extras)]
        out_refs = refs[n_in + len(extras):n_in + len(extras) + n_out]
        rest = refs[n_in + len(extras) + n_out:]
        r_ref = extra_refs[0] if res is not None else None
        o_ref = out_refs[0]

        def product():
            return sum(lax.dot_general(a_ref[...].astype(BF16), b_ref[...].astype(BF16), dims, preferred_element_type=F32)
                       for a_ref, b_ref in zip(a_refs, b_refs))

        def finish(r):
            if scale is not None:
                r = r * scale
            if r_ref is not None:
                r = r + r_ref[...].astype(F32)
            if fuse:
                for ref, val in zip(out_refs, fuse[0](r, *[e[...].astype(F32) for e in extra_refs])):
                    ref[...] = val.astype(ref.dtype)
            else:
                o_ref[...] = r.astype(o_ref.dtype)

        if nk == 1:
            finish(product())
        else:
            acc = rest[0]
            k = pl.program_id(2)

            @pl.when(k == 0)
            def _():
                acc[...] = product()

            @pl.when(k > 0)
            def _():
                acc[...] += product()

            @pl.when(k == nk - 1)
            def _():
                finish(acc[...])

    in_specs = [pl.BlockSpec(*spec) for spec in a_specs + b_specs]
    operands = a_list + b_list
    piped = sum(_block_bytes(spec[0], v.dtype) for spec, v in zip(a_specs + b_specs, operands))
    piped += _block_bytes(o_blk, out_dtype)
    for extra in extras:
        in_specs.append(pl.BlockSpec(o_blk, o_map))
        operands.append(extra)
        piped += _block_bytes(o_blk, extra.dtype)
    piped += (n_out - 1) * _block_bytes(o_blk, out_dtype)
    scratch = [pltpu.VMEM(acc_shape, F32)] if nk > 1 else []
    out_spec, out_sds = pl.BlockSpec(o_blk, o_map), jax.ShapeDtypeStruct(out_shape, out_dtype)
    return _chained(
        body, name=name, grid=grid, in_specs=in_specs, out_specs=(out_spec,) * n_out if fuse else out_spec,
        out_shape=(out_sds,) * n_out if fuse else out_sds, scratch_shapes=scratch,
        compiler_params=pltpu.CompilerParams(
            dimension_semantics=("parallel", "parallel", "arbitrary"),
            vmem_limit_bytes=_vmem_limit(piped, 3 * _block_bytes(acc_shape, F32))),
    )(*operands)


def _mm2(name, a, b, dims, out_dtype, tm=512, tn=1024, tk=2048, scale=None, res=None):
    if dims is NN:
        (m, kk), n = a.shape, b.shape[1]
    elif dims is NT:
        (m, kk), n = a.shape, b.shape[0]
    else:
        (kk, m), n = a.shape, b.shape[1]
    tm, tn, tk = _tile(m, tm), _tile(n, tn), _tile(kk, tk)
    a_spec = ((tk, tm), lambda i, j, k: (k, i)) if dims is TN else ((tm, tk), lambda i, j, k: (i, k))
    b_spec = ((tn, tk), lambda i, j, k: (j, k)) if dims is NT else ((tk, tn), lambda i, j, k: (k, j))
    return _mm(name, a, b, (m // tm, n // tn, kk // tk), a_spec, b_spec, ((tm, tn), lambda i, j, k: (i, j)),
               (m, n), out_dtype, dims, scale, res)


def _mm_stack_out(name, a, b, out_dtype, tm=512, tk=2048):
    (m, kk), (nj, _, n) = a.shape, b.shape
    tm, tk = _tile(m, tm), _tile(kk, tk)
    return _mm(name, a, b, (nj, m // tm, kk // tk), ((tm, tk), lambda j, i, k: (i, k)),
               ((None, tk, n), lambda j, i, k: (j, k, 0)), ((None, tm, n), lambda j, i, k: (j, i, 0)),
               (nj, m, n), out_dtype, NN)


def _mm_stack_nt_out(name, a, b, out_dtype, scale=None, fuse=None, tm=1024, tk=2048):
    (m, kk), (nj, n, _) = a.shape, b.shape
    tm, tk = _tile(m, tm), _tile(kk, tk)
    return _mm(name, a, b, (nj, m // tm, kk // tk), ((tm, tk), lambda j, i, k: (i, k)),
               ((None, n, tk), lambda j, i, k: (j, 0, k)), ((None, tm, n), lambda j, i, k: (j, i, 0)),
               (nj, m, n), out_dtype, NT, scale, fuse=fuse)


def _mm_stack_sum(name, a, b, out_dtype, scale=None, res=None, tm=512, tn=1024, step=2):
    (nj, m, f), n = (a[0] if isinstance(a, tuple) else a).shape, (b[0] if isinstance(b, tuple) else b).shape[2]
    tm, tn = _tile(m, tm), _tile(n, tn)
    pairs = list(zip(a, b)) if isinstance(a, tuple) else [(a, b)]
    step = step if nj % step == 0 else 1
    a_list, b_list, a_spec, b_spec = [], [], [], []
    for a_k, b_k in pairs:
        for s in range(step):
            a_list.append(a_k)
            b_list.append(b_k)
            a_spec.append(((None, tm, f), lambda i, j, k, s=s: (step * k + s, i, 0)))
            b_spec.append(((None, f, tn), lambda i, j, k, s=s: (step * k + s, 0, j)))
    return _mm(name, a_list, b_list, (m // tm, n // tn, nj // step), a_spec, b_spec, ((tm, tn), lambda i, j, k: (i, j)),
               (m, n), out_dtype, NN, scale, res)


def _mm_stack_nt_sum(name, a, b, out_dtype, res=None, tm=512, tn=1024):
    (nj, m, f), n = a.shape, b.shape[1]
    tm, tn = _tile(m, tm), _tile(n, tn)
    return _mm(name, a, b, (m // tm, n // tn, nj), ((None, tm, f), lambda i, j, k: (k, i, 0)),
               ((None, tn, f), lambda i, j, k: (k, j, 0)), ((tm, tn), lambda i, j, k: (i, j)),
               (m, n), out_dtype, NT, None, res)


def _mm_stack_tn_left(name, a, b, out_dtype, tm=1024, tk=512):
    (m, kp), (nj, _, n) = a.shape, b.shape
    tm, tk = _tile(kp, tm), _tile(m, tk)
    return _mm(name, a, b, (nj, kp // tm, m // tk), ((tk, tm), lambda j, i, k: (k, i)),
               ((None, tk, n), lambda j, i, k: (j, k, 0)), ((None, tm, n), lambda j, i, k: (j, i, 0)),
               (nj, kp, n), out_dtype, TN)


def _mm_stack_tn_right(name, a, b, out_dtype, scale=None, tn=2048, tk=2048):
    (nj, m, f), n = a.shape, b.shape[1]
    tn, tk = _tile(n, tn), _tile(m, tk)
    return _mm(name, a, b, (nj, n // tn, m // tk), ((None, tk, f), lambda j, i, k: (j, k, 0)),
               ((tk, tn), lambda j, i, k: (k, i)), ((None, f, tn), lambda j, i, k: (j, 0, i)),
               (nj, f, n), out_dtype, TN, scale)


def _split3(x):
    hi = x.astype(BF16)
    r1 = x - hi.astype(F32)
    mid = r1.astype(BF16)
    lo = (r1 - mid.astype(F32)).astype(BF16)
    return hi, mid, lo


def _swap_halves(x, swap):
    return sum(jnp.dot(part, swap, preferred_element_type=F32) for part in _split3(x))


def _rope_specs(rope, tm):
    cos_t, _, swap = rope
    nt = cos_t.shape[0] // tm
    tab = pl.BlockSpec((tm, cos_t.shape[1]), lambda i, c: (i % nt, 0))
    return [tab, tab, pl.BlockSpec(swap.shape, lambda i, c: (0, 0))]


def _rms_fwd(name, x, g, out_dtype, rope=None, scale=None):
    rows, cols = x.shape
    d = g.shape[1]
    tm = _row_tile(rows if rope is None else rope[0].shape[0], d)

    def body(x_ref, g_ref, *refs):
        xf = x_ref[...].astype(F32)
        r = lax.rsqrt(jnp.mean(xf * xf, axis=-1, keepdims=True) + EPS)
        y = xf * r * g_ref[...]
        if rope is not None:
            c_ref, s_ref, p_ref = refs[:3]
            y = y * c_ref[...] + _swap_halves(y, p_ref[...]) * s_ref[...]
        if scale is not None:
            y = y * scale
        refs[-1][...] = y.astype(refs[-1].dtype)

    return _chained(
        body, name=name, grid=(rows // tm, cols // d),
        in_specs=[pl.BlockSpec((tm, d), lambda i, c: (i, c)), pl.BlockSpec((1, d), lambda i, c: (0, 0))]
        + (_rope_specs(rope, tm) if rope is not None else []),
        out_specs=pl.BlockSpec((tm, d), lambda i, c: (i, c)),
        out_shape=jax.ShapeDtypeStruct((rows, cols), out_dtype),
        compiler_params=_rowwise_params("parallel", "parallel"),
    )(*_streamed(x), g, *(rope or ()))


def _rms_bwd(name, x, g, dy, res=None, rope=None, narrow=False, scale=None):
    rows, cols = x.shape
    d = g.shape[1]
    tm = _row_tile(rows if rope is None else rope[0].shape[0], d)

    def body(*refs):
        x_ref, g_ref, dy_ref = refs[:3]
        n_out = 3 if narrow else 2
        dx_ref, dg_ref = refs[-n_out], refs[-1]
        r_ref = refs[3] if res is not None else None
        xf = x_ref[...].astype(F32)
        r = lax.rsqrt(jnp.mean(xf * xf, axis=-1, keepdims=True) + EPS)
        xhat = xf * r
        dyf = dy_ref[...].astype(F32)
        if scale is not None:
            dyf = dyf * scale
        if rope is not None:
            c_ref, s_ref, p_ref = refs[-n_out - 3:-n_out]
            dyf = dyf * c_ref[...] + _swap_halves(dyf * s_ref[...], p_ref[...])

        @pl.when((pl.program_id(0) == 0) & (pl.program_id(1) == 0))
        def _():
            dg_ref[...] = jnp.zeros_like(dg_ref)

        dg_ref[...] += jnp.sum(dyf * xhat, axis=0, keepdims=True)
        dxh = dyf * g_ref[...]
        dx = r * (dxh - xhat * jnp.mean(dxh * xhat, axis=-1, keepdims=True))
        if r_ref is not None:
            dx = dx + r_ref[...].astype(F32)
        dx_ref[...] = dx
        if narrow:
            refs[-2][...] = dx.astype(BF16)

    blk = pl.BlockSpec((tm, d), lambda i, c: (i, c))
    in_specs = [blk, pl.BlockSpec((1, d), lambda i, c: (0, 0)), blk]
    operands = [x, g, dy]
    if res is not None:
        in_specs.append(blk)
        operands.append(res)
    if rope is not None:
        in_specs += _rope_specs(rope, tm)
        operands += list(rope)
    wide = [(blk, jax.ShapeDtypeStruct((rows, cols), F32))] + ([(blk, jax.ShapeDtypeStruct((rows, cols), BF16))] if narrow else [])
    outs = wide + [(pl.BlockSpec((1, d), lambda i, c: (0, 0)), jax.ShapeDtypeStruct((1, d), F32))]
    return _chained(
        body, name=name, grid=(rows // tm, cols // d), in_specs=in_specs,
        out_specs=tuple(spec for spec, _ in outs), out_shape=tuple(sds for _, sds in outs),
        compiler_params=_rowwise_params("arbitrary", "arbitrary"),
    )(*_streamed(x), g, *_streamed(*operands[2:4 if res is not None else 3]), *operands[4 if res is not None else 3:])


def _swiglu_fwd(name, g, u, out_dtype, tm=256):
    rows, cols = g.shape
    tm = _tile(rows, tm)

    def body(g_ref, u_ref, o_ref):
        gf = g_ref[...].astype(F32)
        o_ref[...] = (gf * jax.nn.sigmoid(gf) * u_ref[...].astype(F32)).astype(o_ref.dtype)

    blk = pl.BlockSpec((tm, cols), lambda i: (i, 0))
    return _chained(
        body, name=name, grid=(rows // tm,), in_specs=[blk, blk], out_specs=blk,
        out_shape=jax.ShapeDtypeStruct((rows, cols), out_dtype),
        compiler_params=_rowwise_params("parallel"),
    )(*_streamed(g, u))


def _swiglu_bwd(name, da, g, u, out_dtype, tm=256):
    rows, cols = g.shape
    tm = _tile(rows, tm)

    def body(da_ref, g_ref, u_ref, dg_ref, du_ref):
        gf = g_ref[...].astype(F32)
        daf = da_ref[...].astype(F32)
        sig = jax.nn.sigmoid(gf)
        du_ref[...] = (daf * gf * sig).astype(du_ref.dtype)
        dg_ref[...] = (daf * u_ref[...].astype(F32) * sig * (1.0 + gf * (1.0 - sig))).astype(dg_ref.dtype)

    blk = pl.BlockSpec((tm, cols), lambda i: (i, 0))
    sds = jax.ShapeDtypeStruct((rows, cols), out_dtype)
    return _chained(
        body, name=name, grid=(rows // tm,), in_specs=[blk, blk, blk], out_specs=(blk, blk), out_shape=(sds, sds),
        compiler_params=_rowwise_params("parallel"),
    )(*_streamed(da, g, u))


def _gate_fwd(name, pre, bias, tm=256):
    rows, cols = pre.shape
    tm = _tile(rows, tm)

    def body(p_ref, b_ref, o_ref):
        z = p_ref[...] + b_ref[...]
        o_ref[...] = (jnp.minimum(z, 0.0) - jnp.log(1.0 + jnp.exp(-jnp.abs(z)))) * (1.0 / GLA_TAU)

    blk = pl.BlockSpec((tm, cols), lambda i: (i, 0))
    return _chained(
        body, name=name, grid=(rows // tm,), in_specs=[blk, pl.BlockSpec((1, cols), lambda i: (0, 0))], out_specs=blk,
        out_shape=jax.ShapeDtypeStruct((rows, cols), F32),
        compiler_params=_rowwise_params("parallel"),
    )(pre, bias)


def _gate_bwd(name, pre, bias, dla, tm=256):
    rows, cols = pre.shape
    tm = _tile(rows, tm)

    def body(p_ref, b_ref, d_ref, dp_ref, db_ref):
        z = p_ref[...] + b_ref[...]
        dp = d_ref[...] * (1.0 / GLA_TAU) / (1.0 + jnp.exp(z))
        dp_ref[...] = dp

        @pl.when(pl.program_id(0) == 0)
        def _():
            db_ref[...] = jnp.zeros_like(db_ref)

        db_ref[...] += jnp.sum(dp, axis=0, keepdims=True)

    blk = pl.BlockSpec((tm, cols), lambda i: (i, 0))
    row = pl.BlockSpec((1, cols), lambda i: (0, 0))
    return _chained(
        body, name=name, grid=(rows // tm,), in_specs=[blk, row, blk], out_specs=(blk, row),
        out_shape=(jax.ShapeDtypeStruct((rows, cols), F32), jax.ShapeDtypeStruct((1, cols), F32)),
        compiler_params=_rowwise_params("arbitrary"),
    )(*_streamed(pre), bias, *_streamed(dla))


def _loss(name, y, target, tm=256):
    rows, d = y.shape
    tm = _tile(rows, tm)

    def body(y_ref, t_ref, dy_ref, dyb_ref, l_ref):
        err = y_ref[...] - t_ref[...]
        dy_ref[...] = err * (1.0 / d)
        dyb_ref[...] = (err * (1.0 / d)).astype(BF16)

        @pl.when(pl.program_id(0) == 0)
        def _():
            l_ref[...] = jnp.zeros_like(l_ref)

        sq = (err * err).reshape(tm // 8, 8, d)
        l_ref[...] += jnp.sum(sq, axis=0) * (0.5 / d)

    blk = pl.BlockSpec((tm, d), lambda i: (i, 0))
    return _chained(
        body, name=name, grid=(rows // tm,), in_specs=[blk, blk],
        out_specs=(blk, blk, pl.BlockSpec((8, d), lambda i: (0, 0))),
        out_shape=(jax.ShapeDtypeStruct((rows, d), F32), jax.ShapeDtypeStruct((rows, d), BF16),
                   jax.ShapeDtypeStruct((8, d), F32)),
        compiler_params=_rowwise_params("arbitrary"),
    )(*_streamed(y, target))


def _per_query_block(causal, nq, tq, tk, inner):
    if not causal:
        inner(0, tk)
        return
    for block in range(nq):
        pl.when(pl.program_id(1) == block)(functools.partial(inner, block * tq, (block + 1) * tq))


def _scores(q, k, causal):
    s = lax.dot_general(q, k, NT, preferred_element_type=F32)
    if causal:
        tq, kv = s.shape
        own = s[:, kv - tq:]
        qc = lax.broadcasted_iota(jnp.int32, own.shape, 0) // CHUNK
        kc = lax.broadcasted_iota(jnp.int32, own.shape, 1) // CHUNK
        own = jnp.where(kc <= qc, own, -1e30)
        s = own if kv == tq else jnp.concatenate([s[:, :kv - tq], own], axis=1)
    e = jnp.exp(s - jnp.max(s, axis=-1, keepdims=True))
    return e, jnp.sum(e, axis=-1, keepdims=True)


def _attn_fwd(name, q, k, v, causal, tq=256):
    nh, t, dk = q.shape
    tk, dv = k.shape[1], v.shape[2]
    tq = _tile(t, tq)

    def body(q_ref, k_ref, v_ref, o_ref):
        def inner(q0, kv):
            e, l = _scores(q_ref[...], k_ref[0:kv, :], causal)
            o = jnp.dot(e.astype(BF16), v_ref[0:kv, :], preferred_element_type=F32)
            o_ref[...] = (o / l).astype(o_ref.dtype)

        _per_query_block(causal, t // tq, tq, tk, inner)

    return _chained(
        body, name=name, grid=(nh, t // tq),
        in_specs=[pl.BlockSpec((None, tq, dk), lambda h, i: (h, i, 0)), pl.BlockSpec((None, tk, dk), lambda h, i: (h, 0, 0)),
                  pl.BlockSpec((None, tk, dv), lambda h, i: (h, 0, 0))],
        out_specs=pl.BlockSpec((tq, dv), lambda h, i: (i, h)),
        out_shape=jax.ShapeDtypeStruct((t, nh * dv), BF16),
        compiler_params=pltpu.CompilerParams(dimension_semantics=("parallel", "parallel"),
                                             vmem_limit_bytes=_vmem_limit(0, 6 * tq * tk * 4)),
    )(q, k, v)


def _attn_bwd(name, q, k, v, do, causal, tq=256):
    nh, t, dk = q.shape
    tk, dv = k.shape[1], v.shape[2]
    tq = _tile(t, tq)

    def body(q_ref, k_ref, v_ref, do_ref, dq_ref, dk_ref, dv_ref):
        @pl.when(pl.program_id(1) == 0)
        def _():
            dk_ref[...] = jnp.zeros_like(dk_ref)
            dv_ref[...] = jnp.zeros_like(dv_ref)

        def inner(q0, kv):
            qb, kb = q_ref[...], k_ref[0:kv, :]
            e, l = _scores(qb, kb, causal)
            p = e * (1.0 / l)
            dob = do_ref[...].astype(BF16)
            dp = lax.dot_general(dob, v_ref[0:kv, :], NT, preferred_element_type=F32)
            ds = (p * (dp - jnp.sum(p * dp, axis=-1, keepdims=True))).astype(BF16)
            dq_ref[...] = jnp.dot(ds, kb, preferred_element_type=F32)
            dk_ref[0:kv, :] += lax.dot_general(ds, qb, TN, preferred_element_type=F32)
            dv_ref[0:kv, :] += lax.dot_general(p.astype(BF16), dob, TN, preferred_element_type=F32)

        _per_query_block(causal, t // tq, tq, tk, inner)

    return _chained(
        body, name=name, grid=(nh, t // tq),
        in_specs=[pl.BlockSpec((None, tq, dk), lambda h, i: (h, i, 0)), pl.BlockSpec((None, tk, dk), lambda h, i: (h, 0, 0)),
                  pl.BlockSpec((None, tk, dv), lambda h, i: (h, 0, 0)), pl.BlockSpec((tq, dv), lambda h, i: (i, h))],
        out_specs=(pl.BlockSpec((None, tq, dk), lambda h, i: (h, i, 0)), pl.BlockSpec((None, tk, dk), lambda h, i: (h, 0, 0)),
                   pl.BlockSpec((None, tk, dv), lambda h, i: (h, 0, 0))),
        out_shape=(jax.ShapeDtypeStruct((nh, t, dk), F32), jax.ShapeDtypeStruct((nh, tk, dk), F32),
                   jax.ShapeDtypeStruct((nh, tk, dv), F32)),
        compiler_params=pltpu.CompilerParams(dimension_semantics=("parallel", "arbitrary"),
                                             vmem_limit_bytes=_vmem_limit(0, 10 * tq * tk * 4)),
    )(q, k, v, do)


def _tri(lower):
    r = lax.broadcasted_iota(jnp.int32, (CHUNK, CHUNK), 0)
    c = lax.broadcasted_iota(jnp.int32, (CHUNK, CHUNK), 1)
    return jnp.where((c <= r) if lower else (c >= r), 1.0, 0.0).astype(BF16)


def _tri_dot(tri, x):
    return sum(jnp.dot(tri, part, preferred_element_type=F32) for part in _split3(x))


def _gla_fwd(name, q, k, v, la, nh):
    t = q.shape[0]
    dk, dv = q.shape[1] // nh, v.shape[1] // nh
    nc = t // CHUNK

    def body(q_ref, k_ref, v_ref, g_ref, o_ref, st_ref, state):
        @pl.when(pl.program_id(0) == 0)
        def _():
            state[...] = jnp.zeros_like(state)

        g = g_ref[...]
        b = _tri_dot(_tri(True), g)
        b_end = jnp.sum(g, axis=0, keepdims=True)
        k_dec = (k_ref[...] * jnp.exp(b_end - b)).astype(BF16)
        decay = jnp.exp(b_end)
        qc = (q_ref[...] * (dk ** -0.5)).astype(BF16)
        vb = v_ref[...].astype(BF16)
        for h in range(nh):
            ks, vs = slice(h * dk, (h + 1) * dk), slice(h * dv, (h + 1) * dv)
            u_t = lax.dot_general(vb[:, vs], k_dec[:, ks], TN, preferred_element_type=F32)
            new = state[h] * decay[:, ks] + u_t
            state[h] = new
            st_ref[h] = new
            o_ref[:, vs] = lax.dot_general(qc[:, ks], new.astype(BF16), NT, preferred_element_type=F32)

    kblk = pl.BlockSpec((CHUNK, nh * dk), lambda n: (n, 0))
    vblk = pl.BlockSpec((CHUNK, nh * dv), lambda n: (n, 0))
    return _chained(
        body, name=name, grid=(nc,), in_specs=[kblk, kblk, vblk, kblk],
        out_specs=(vblk, pl.BlockSpec((nh, None, dv, dk), lambda n: (0, n, 0, 0))),
        out_shape=(jax.ShapeDtypeStruct((t, nh * dv), F32), jax.ShapeDtypeStruct((nh, nc, dv, dk), F32)),
        scratch_shapes=[pltpu.VMEM((nh, dv, dk), F32)],
        compiler_params=_rowwise_params("arbitrary"),
    )(q, k, v, la)


def _gla_bwd(name, q, k, v, la, states, do, nh):
    t = q.shape[0]
    dk, dv = q.shape[1] // nh, v.shape[1] // nh
    nc = t // CHUNK
    scale = dk ** -0.5

    def body(q_ref, k_ref, v_ref, g_ref, do_ref, st_ref, sp_ref, dq_ref, dk_ref, dv_ref, dg_ref, carry):
        i = pl.program_id(0)

        @pl.when(i == 0)
        def _():
            carry[...] = jnp.zeros_like(carry)

        g = g_ref[...]
        b = _tri_dot(_tri(True), g)
        b_end = jnp.sum(g, axis=0, keepdims=True)
        w = jnp.exp(b_end - b)
        decay = jnp.exp(b_end)
        k_dec = k_ref[...] * w
        k_decb = k_dec.astype(BF16)
        qc = (q_ref[...] * scale).astype(BF16)
        dob = do_ref[...].astype(BF16)
        vb = v_ref[...].astype(BF16)
        dk_dec, ddecay = [], []
        for h in range(nh):
            ks, vs = slice(h * dk, (h + 1) * dk), slice(h * dv, (h + 1) * dv)
            dq_ref[:, ks] = jnp.dot(dob[:, vs], st_ref[h].astype(BF16), preferred_element_type=F32) * scale
            g_t = carry[h] + lax.dot_general(dob[:, vs], qc[:, ks], TN, preferred_element_type=F32)
            g_tb = g_t.astype(BF16)
            dk_dec.append(jnp.dot(vb[:, vs], g_tb, preferred_element_type=F32))
            dv_ref[:, vs] = lax.dot_general(k_decb[:, ks], g_tb, NT, preferred_element_type=F32)
            prev = jnp.where(i < nc - 1, sp_ref[h], 0.0)
            ddecay.append(jnp.sum(g_t * prev, axis=0, keepdims=True))
            carry[h] = g_t * decay[:, ks]
        dk_dec = jnp.concatenate(dk_dec, axis=1)
        dk_ref[...] = dk_dec * w
        e = dk_dec * k_dec
        db_end = jnp.sum(e, axis=0, keepdims=True) + jnp.concatenate(ddecay, axis=1) * decay
        dg_ref[...] = _tri_dot(_tri(False), -e) + db_end

    kblk = pl.BlockSpec((CHUNK, nh * dk), lambda i: (nc - 1 - i, 0))
    vblk = pl.BlockSpec((CHUNK, nh * dv), lambda i: (nc - 1 - i, 0))
    ksds = jax.ShapeDtypeStruct((t, nh * dk), F32)
    return _chained(
        body, name=name, grid=(nc,),
        in_specs=[kblk, kblk, vblk, kblk, vblk,
                  pl.BlockSpec((nh, None, dv, dk), lambda i: (0, nc - 1 - i, 0, 0)),
                  pl.BlockSpec((nh, None, dv, dk), lambda i: (0, jnp.maximum(nc - 2 - i, 0), 0, 0))],
        out_specs=(kblk, kblk, vblk, kblk),
        out_shape=(ksds, ksds, jax.ShapeDtypeStruct((t, nh * dv), F32), ksds),
        scratch_shapes=[pltpu.VMEM((nh, dv, dk), F32)],
        compiler_params=_rowwise_params("arbitrary"),
    )(q, k, v, la, do, states, states)


def _adamw(name, w, m, v, parts):
    rows, cols = w.shape
    tm = 1 << int(math.log2(max(8, (1 << 18) // (-(-cols // LANES) * LANES))))
    while rows % tm and tm > 8:
        tm //= 2
    tm = _tile(rows, tm)

    def body(w_ref, m_ref, v_ref, p_ref, g_ref, d_ref, nm_ref, nv_ref):
        g = p_ref[0].astype(F32)
        for s in range(1, parts.shape[0]):
            g = g + p_ref[s].astype(F32)
        m_new = ADAM_B1 * m_ref[...] + (1.0 - ADAM_B1) * g
        v_new = ADAM_B2 * v_ref[...] + (1.0 - ADAM_B2) * jnp.square(g)
        m_hat = m_new / (1.0 - ADAM_B1 ** ADAM_STEP)
        v_hat = v_new / (1.0 - ADAM_B2 ** ADAM_STEP)
        g_ref[...] = g
        d_ref[...] = -ADAM_LR * (m_hat / (jnp.sqrt(v_hat) + ADAM_EPS) + ADAM_WD * w_ref[...])
        nm_ref[...] = m_new
        nv_ref[...] = v_new

    blk = pl.BlockSpec((tm, cols), lambda i: (i, 0))
    sds = jax.ShapeDtypeStruct((rows, cols), F32)
    return _chained(
        body, name=name, grid=(rows // tm,),
        in_specs=[blk, blk, blk, pl.BlockSpec((parts.shape[0], tm, cols), lambda i: (0, i, 0))],
        out_specs=(blk, blk, blk, blk), out_shape=(sds, sds, sds, sds),
        compiler_params=_rowwise_params("parallel"),
    )(*_streamed(w, m, v, parts))


HBM = pl.BlockSpec(memory_space=pl.ANY)
MESH = pl.DeviceIdType.MESH


def _all_gather(name, shards):
    n = len(shards)

    def body(*refs):
        ins, outs = refs[:n], refs[n:2 * n]
        send_sems, recv_sems, local_sems = refs[2 * n:]
        x, y, c = lax.axis_index("x"), lax.axis_index("y"), lax.axis_index("c")
        me, sibling = (x, y, c), (x, y, 1 - c)
        chips = [(1 - x, y), (x, 1 - y), (1 - x, 1 - y)]

        def copy(w, k, block, to, src=None):
            dst = outs[w].at[4 * block[0] + 2 * block[1] + block[2]]
            return pltpu.make_async_remote_copy(
                src_ref=dst if src is None else src, dst_ref=dst, send_sem=send_sems.at[7 * w + k],
                recv_sem=recv_sems.at[7 * w + k], device_id=to, device_id_type=MESH)

        mine, first, passed = [], [], []
        for w in range(n):
            cp = pltpu.make_async_copy(ins[w], outs[w].at[4 * x + 2 * y + c], local_sems.at[w])
            cp.start()
            mine.append(cp)
            first.append(copy(w, 0, me, sibling, src=ins[w]))
            first += [copy(w, 1 + j, me, (*chip, c), src=ins[w]) for j, chip in enumerate(chips)]
        for cp in first:
            cp.start()
        for w in range(n):
            for j, chip in enumerate(chips):
                copy(w, 1 + j, (*chip, c), me).wait_recv()
                cp = copy(w, 4 + j, (*chip, c), sibling)
                cp.start()
                passed.append(cp)
        for w in range(n):
            copy(w, 0, sibling, me).wait_recv()
            for j, chip in enumerate(chips):
                copy(w, 4 + j, (*chip, 1 - c), me).wait_recv()
        for cp in first + passed:
            cp.wait_send()
        for cp in mine:
            cp.wait()

    return _chained(
        body, name=name, in_specs=[HBM] * n, out_specs=[HBM] * n,
        out_shape=[jax.ShapeDtypeStruct((N_DEV,) + s.shape, s.dtype) for s in shards],
        scratch_shapes=[pltpu.SemaphoreType.DMA((7 * n,)), pltpu.SemaphoreType.DMA((7 * n,)),
                        pltpu.SemaphoreType.DMA((n,))],
    )(*shards)


HBM_SPEC = pl.BlockSpec(memory_space=pltpu.HBM)
SEM_SPEC = pl.BlockSpec(memory_space=pltpu.SEMAPHORE)
EFFECT = pltpu.SideEffectType.DATAFLOW_SIDE_EFFECTING
TOKEN = jax.ShapeDtypeStruct((8, LANES), F32)


def _in_hbm(a):
    return pltpu.with_memory_space_constraint(a, pltpu.HBM)


def _place():
    x, y, c = lax.axis_index("x"), lax.axis_index("y"), lax.axis_index("c")
    chips = [(1 - x, y), (x, 1 - y), (1 - x, 1 - y)]
    return x, y, c, chips


def _block_of(px, py, pc):
    return 4 * px + 2 * py + pc


def _gather_copies(ins, outs, sems, w):
    send_sems, recv_sems, local_sems = sems
    x, y, c, chips = _place()
    peers = [(x, y, 1 - c)] + [(*chip, c) for chip in chips]

    def copy(k, block, to, src):
        dst = outs[w].at[_block_of(*block)]
        return pltpu.make_async_remote_copy(src_ref=dst if src is None else src, dst_ref=dst, send_sem=send_sems.at[4 * w + k],
                                            recv_sem=recv_sems.at[4 * w + k], device_id=to, device_id_type=MESH)

    local = pltpu.make_async_copy(ins[w], outs[w].at[_block_of(x, y, c)], local_sems.at[w])
    sends = [copy(k, (x, y, c), peer, ins[w]) for k, peer in enumerate(peers)]
    recvs = [copy(k, peer, (x, y, c), None) for k, peer in enumerate(peers)]
    return local, sends, recvs


def _forward_copies(outs, sems, w):
    send_sems, recv_sems = sems
    x, y, c, chips = _place()

    def copy(j, block, to):
        dst = outs[w].at[_block_of(*block)]
        return pltpu.make_async_remote_copy(src_ref=dst, dst_ref=dst, send_sem=send_sems.at[3 * w + j],
                                            recv_sem=recv_sems.at[3 * w + j], device_id=to, device_id_type=MESH)

    sends = [copy(j, (*chip, c), (x, y, 1 - c)) for j, chip in enumerate(chips)]
    recvs = [copy(j, (*chip, 1 - c), (x, y, c)) for j, chip in enumerate(chips)]
    return sends, recvs


def _gather_start(name, shards):
    n = len(shards)

    def body(*refs):
        ins, outs, sems = refs[:n], refs[n:2 * n], refs[2 * n:2 * n + 3]
        for w in range(n):
            local, sends, _ = _gather_copies(ins, outs, sems, w)
            for cp in sends + [local]:
                cp.start()
        refs[-1][...] = jnp.zeros_like(refs[-1])

    lands = [lax.empty((N_DEV,) + s.shape, s.dtype) for s in shards]
    res = _chained(
        body, name=name, link=-1, in_specs=[HBM_SPEC] * (2 * n),
        out_specs=[SEM_SPEC] * 3 + [HBM_SPEC] * (2 * n) + [pl.BlockSpec(memory_space=pltpu.VMEM)],
        out_shape=[pltpu.SemaphoreType.DMA((4 * n,)), pltpu.SemaphoreType.DMA((4 * n,)), pltpu.SemaphoreType.DMA((n,))]
        + [pltpu.HBM(s.shape, s.dtype) for s in shards] + [pltpu.HBM(l.shape, l.dtype) for l in lands] + [TOKEN],
        input_output_aliases={i: 3 + i for i in range(2 * n)},
        compiler_params=pltpu.CompilerParams(has_side_effects=EFFECT),
    )(*[_in_hbm(s) for s in shards], *[_in_hbm(l) for l in lands])
    return n, res[:3], res[3:3 + n], res[3 + n:3 + 2 * n]


def _gather_mid(name, state):
    n, sems, shards, lands = state

    def body(*refs):
        ins, outs, sems_in = refs[:n], refs[n:2 * n], refs[2 * n:2 * n + 3]
        sems_out = refs[2 * n + 3:2 * n + 5]
        for w in range(n):
            local, sends, recvs = _gather_copies(ins, outs, sems_in, w)
            local.wait()
            for cp in sends:
                cp.wait_send()
            for cp in recvs:
                cp.wait_recv()
            for cp in _forward_copies(outs, sems_out, w)[0]:
                cp.start()
        refs[-1][...] = jnp.zeros_like(refs[-1])

    res = _chained(
        body, name=name, link=-1, in_specs=[HBM_SPEC] * (2 * n) + [SEM_SPEC] * 3,
        out_specs=[SEM_SPEC] * 2 + [HBM_SPEC] * n + [pl.BlockSpec(memory_space=pltpu.VMEM)],
        out_shape=[pltpu.SemaphoreType.DMA((3 * n,)), pltpu.SemaphoreType.DMA((3 * n,))]
        + [pltpu.HBM(l.shape, l.dtype) for l in lands] + [TOKEN],
        input_output_aliases={n + i: 2 + i for i in range(n)},
        compiler_params=pltpu.CompilerParams(has_side_effects=EFFECT),
    )(*shards, *lands, *sems)
    return n, res[:2], res[2:2 + n]


def _gather_end(name, state):
    n, sems, lands = state

    def body(*refs):
        outs, sems_in = refs[:n], refs[n:n + 2]
        for w in range(n):
            sends, recvs = _forward_copies(outs, sems_in, w)
            for cp in sends:
                cp.wait_send()
            for cp in recvs:
                cp.wait_recv()

    return _chained(
        body, name=name, in_specs=[HBM_SPEC] * n + [SEM_SPEC] * 2, out_specs=[HBM_SPEC] * n,
        out_shape=[pltpu.HBM(l.shape, l.dtype) for l in lands], input_output_aliases={i: i for i in range(n)},
        compiler_params=pltpu.CompilerParams(has_side_effects=EFFECT),
    )(*lands, *sems)


def _exchange_copies(ins, outs, sems, w):
    send_sems, recv_sems, local_sems = sems
    x, y, c, _ = _place()
    mine = _block_of(x, y, c)
    local = pltpu.make_async_copy(ins[w].at[mine], outs[w].at[mine], local_sems.at[w])
    remote = []
    for k in range(1, N_DEV):
        px, py, pc = x ^ (k >> 2), y ^ ((k >> 1) & 1), c ^ (k & 1)
        remote.append(pltpu.make_async_remote_copy(
            src_ref=ins[w].at[_block_of(px, py, pc)], dst_ref=outs[w].at[mine], send_sem=send_sems.at[7 * w + k - 1],
            recv_sem=recv_sems.at[7 * w + k - 1], device_id=(px, py, pc), device_id_type=MESH))
    return local, remote


def _exchange_start(name, stacks):
    n = len(stacks)

    def body(*refs):
        ins, outs, sems = refs[:n], refs[n:2 * n], refs[2 * n:2 * n + 3]
        for w in range(n):
            local, remote = _exchange_copies(ins, outs, sems, w)
            for cp in remote + [local]:
                cp.start()
        refs[-1][...] = jnp.zeros_like(refs[-1])

    lands = [lax.empty(s.shape, s.dtype) for s in stacks]
    if any(s is _Chain.last for s in stacks):
        _Chain.last = None
    res = _chained(
        body, name=name, link=-1, in_specs=[HBM_SPEC] * (2 * n),
        out_specs=[SEM_SPEC] * 3 + [HBM_SPEC] * (2 * n) + [pl.BlockSpec(memory_space=pltpu.VMEM)],
        out_shape=[pltpu.SemaphoreType.DMA((7 * n,)), pltpu.SemaphoreType.DMA((7 * n,)), pltpu.SemaphoreType.DMA((n,))]
        + [pltpu.HBM(s.shape, s.dtype) for s in stacks] * 2 + [TOKEN],
        input_output_aliases={i: 3 + i for i in range(2 * n)},
        compiler_params=pltpu.CompilerParams(has_side_effects=EFFECT),
    )(*[_in_hbm(s) for s in stacks], *[_in_hbm(l) for l in lands])
    return n, res[:3], res[3:3 + n], res[3 + n:3 + 2 * n]


def _exchange_wait(name, state):
    n, sems, stacks, lands = state

    def body(*refs):
        ins, outs, sems_in = refs[:n], refs[n:2 * n], refs[2 * n:2 * n + 3]
        for w in range(n):
            local, remote = _exchange_copies(ins, outs, sems_in, w)
            local.wait()
            for cp in remote:
                cp.wait_send()
                cp.wait_recv()

    return _chained(
        body, name=name, in_specs=[HBM_SPEC] * (2 * n) + [SEM_SPEC] * 3, out_specs=[HBM_SPEC] * n,
        out_shape=[pltpu.HBM(l.shape, l.dtype) for l in lands], input_output_aliases={n + i: i for i in range(n)},
        compiler_params=pltpu.CompilerParams(has_side_effects=EFFECT),
    )(*stacks, *lands, *sems)


N_CHIP = N_DEV // 2


def _pair_copies(stack, land, sems):
    send_sems, recv_sems, _ = sems
    x, y, c, _ = _place()
    remote = [pltpu.make_async_remote_copy(src_ref=stack.at[2 * k + 1 - c], dst_ref=land.at[k], send_sem=send_sems.at[k],
                                           recv_sem=recv_sems.at[k], device_id=(x, y, 1 - c), device_id_type=MESH)
              for k in range(N_CHIP)]
    return [], remote


def _chip_copies(pairs, land, sems):
    send_sems, recv_sems, local_sems = sems
    x, y, c, _ = _place()
    mine = 2 * x + y
    local = pltpu.make_async_copy(pairs.at[mine], land.at[mine], local_sems.at[0])
    remote = []
    for m in range(1, N_CHIP):
        px, py = x ^ (m >> 1), y ^ (m & 1)
        remote.append(pltpu.make_async_remote_copy(
            src_ref=pairs.at[2 * px + py], dst_ref=land.at[mine], send_sem=send_sems.at[m - 1],
            recv_sem=recv_sems.at[m - 1], device_id=(px, py, c), device_id_type=MESH))
    return local, remote


def _stage_start(name, copies_of, src_arr, n_land, n_sems):
    half = (N_CHIP,) + src_arr.shape[1:]

    def body(*refs):
        local, remote = copies_of(refs[0], *refs[1:1 + n_land], refs[1 + n_land:4 + n_land])
        for cp in remote + (local if isinstance(local, list) else [local]):
            cp.start()
        refs[-1][...] = jnp.zeros_like(refs[-1])

    if src_arr is _Chain.last:
        _Chain.last = None
    lands = [lax.empty(half, src_arr.dtype) for _ in range(n_land)]
    res = _chained(
        body, name=name, link=-1, in_specs=[HBM_SPEC] * (1 + n_land),
        out_specs=[SEM_SPEC] * 3 + [HBM_SPEC] * (1 + n_land) + [pl.BlockSpec(memory_space=pltpu.VMEM)],
        out_shape=[pltpu.SemaphoreType.DMA((n,)) for n in n_sems] + [pltpu.HBM(src_arr.shape, src_arr.dtype)]
        + [pltpu.HBM(half, src_arr.dtype)] * n_land + [TOKEN],
        input_output_aliases={i: 3 + i for i in range(1 + n_land)},
        compiler_params=pltpu.CompilerParams(has_side_effects=EFFECT),
    )(_in_hbm(src_arr), *[_in_hbm(l) for l in lands])
    return res[:3], res[3], res[4:4 + n_land]


def _stage_wait(name, copies_of, state):
    sems, src_arr, lands = state
    n_land = len(lands)

    def body(*refs):
        local, remote = copies_of(refs[0], *refs[1:1 + n_land], refs[1 + n_land:4 + n_land])
        for cp in (local if isinstance(local, list) else [local]):
            cp.wait()
        for cp in remote:
            cp.wait_send()
            cp.wait_recv()

    return _chained(
        body, name=name, in_specs=[HBM_SPEC] * (1 + n_land) + [SEM_SPEC] * 3, out_specs=[HBM_SPEC] * n_land,
        out_shape=[pltpu.HBM(l.shape, l.dtype) for l in lands],
        input_output_aliases={1 + i: i for i in range(n_land)},
        compiler_params=pltpu.CompilerParams(has_side_effects=EFFECT),
    )(src_arr, *lands, *sems)


def _pair_sum(name, stack, land, tm=64):
    n, rows, cols = land.shape
    tm = _tile(rows, tm)

    def body(c_ref, a_ref, b_ref, o_ref):
        o_ref[...] = (a_ref[...].astype(F32) + b_ref[...].astype(F32)).astype(o_ref.dtype)

    core = lax.axis_index("c").astype(jnp.int32).reshape(1)
    blk = pl.BlockSpec((n, tm, cols), lambda i, c_ref: (0, i, 0))
    mine = pl.BlockSpec((n, None, tm, cols), lambda i, c_ref: (0, c_ref[0], i, 0))
    _Chain.last = pl.pallas_call(
        body, name=name,
        grid_spec=pltpu.PrefetchScalarGridSpec(num_scalar_prefetch=1, grid=(rows // tm,), in_specs=[mine, blk], out_specs=blk),
        out_shape=jax.ShapeDtypeStruct(land.shape, land.dtype), compiler_params=_rowwise_params("parallel"),
    )(core, stack.reshape(n, 2, rows, cols), land)
    return _Chain.last


def _unstack_cols(w):
    return w.transpose(1, 0, 2).reshape(w.shape[1], N_DEV * w.shape[2])


def _stack_cols(w):
    return w.reshape(w.shape[0], N_DEV, w.shape[1] // N_DEV).transpose(1, 0, 2)


def _rope_tables(positions):
    half = MLA_ROPE // 2
    inv_freq = ROPE_THETA ** (-jnp.arange(half, dtype=F32) / half)
    ang = positions.astype(F32)[:, None] * inv_freq
    cos, sin = jnp.cos(ang), jnp.sin(ang)
    t = positions.shape[0]
    cos_t = jnp.concatenate([jnp.ones((t, MLA_NOPE), F32), cos, cos], axis=1)
    sin_t = jnp.concatenate([jnp.zeros((t, MLA_NOPE), F32), -sin, sin], axis=1)
    idx = jnp.arange(MLA_QK)
    partner = jnp.where(idx < MLA_NOPE, -1, jnp.where(idx < MLA_NOPE + half, idx + half, idx - half))
    swap = (idx[:, None] == partner[None, :]).astype(BF16)
    return cos_t, sin_t, swap


def _ffn_fwd(tag, x, gain, wt):
    h = _rms_fwd(tag + "_rms", x, gain, BF16)
    g = _mm_stack_nt_out(tag + "_gate", h, wt[tag + '_w_gate'], BF16)
    u, a = _mm_stack_nt_out(tag + "_up", h, wt[tag + '_w_up'], BF16,
                            fuse=(lambda u_blk, g_blk: (u_blk, g_blk * jax.nn.sigmoid(g_blk) * u_blk), [g]))
    y = _mm_stack_sum(tag + "_down", a, wt[tag + '_w_down'], F32, scale=0.5, res=x, step=N_DEV)
    return y, (x, h, g, u, a)


def _ffn_bwd_weights(tag, dy, saved, wd, comm):
    x, h, g, u, a = saved
    comm.grads({tag + '_w_down': _mm_stack_tn_right(tag + "_dwd", a, dy, BF16, scale=0.5)})

    def dact(da, g_blk, u_blk):
        sig = jax.nn.sigmoid(g_blk)
        return da * u_blk * sig * (1.0 + g_blk * (1.0 - sig)), da * g_blk * sig

    dg, du = _mm_stack_nt_out(tag + "_da", dy, wd, BF16, scale=0.5, fuse=(dact, [g, u]))
    comm.advance()
    comm.grads({tag + '_w_gate': _mm_stack_tn_right(tag + "_dwg", dg, h, BF16)})
    dwu = _mm_stack_tn_right(tag + "_dwu", du, h, BF16)
    comm.advance()
    comm.grads({tag + '_w_up': dwu})
    return dg, du


def _ffn_bwd_input(tag, dy, saved, dgu, gain, wg, wu, comm):
    dg, du = dgu
    dh = _mm_stack_sum(tag + "_dh", (dg, du), (wg, wu), F32)
    comm.advance()
    return _rms_bwd(tag + "_drms", saved[0], gain, dh, res=dy)


def _local_step(x, mem, positions, sm, comm, target):
    t, d = x.shape
    nm = mem.shape[0]
    gs, gw = {}, {}

    class Weights(dict):
        def __missing__(self, name):
            self.update(comm.weights(next(k for k, names in enumerate(GATHERS) if name in names)))
            return self[name]

    wt = Weights()

    x1, ffn1_saved = _ffn_fwd("ffn1", x, sm['ffn1_norm'], wt)

    h2 = _rms_fwd("mix_rms", x1, sm['mix_norm'], BF16)
    w_in_ref = _unstack_cols(wt['w_in'])
    pieces = []
    for name, n in IN_PAD:
        piece = w_in_ref[:, REF_OFF[name]:REF_OFF[name] + REF_SIZE[name]]
        if n != REF_SIZE[name]:
            piece = jnp.pad(piece, ((0, 0), (0, n - REF_SIZE[name])))
        pieces.append(piece)
    w_in = jnp.concatenate(pieces, axis=1)
    z = _mm2("mix_in", h2, w_in, NN, F32)
    zs = {name: z[:, PAD_OFF[name]:PAD_OFF[name] + n] for name, n in IN_PAD}

    cq = _rms_fwd("mla_q_a_rms", zs['zq'], sm['q_a_norm'], BF16)
    q_raw = _mm_stack_nt_out("mla_q_up", cq, wt['w_q_up'], F32)
    ckv = _rms_fwd("mla_kv_a_rms", zs['zkv'], sm['kv_a_norm'], BF16)
    kv = _mm_stack_out("mla_kv_up", ckv, wt['w_kv_up'], F32)
    zkr = zs['zkr'][:, :MLA_ROPE]
    k_raw = jnp.concatenate([kv[:, :, :MLA_NOPE], jnp.broadcast_to(zkr[None], (MLA_HEADS, t, MLA_ROPE))], axis=2)
    v_mla = kv[:, :, MLA_NOPE:].astype(BF16)
    cos_t, sin_t, swap = _rope_tables(positions)
    q_raw2, k_raw2 = q_raw.reshape(MLA_HEADS * t, MLA_QK), k_raw.reshape(MLA_HEADS * t, MLA_QK)
    rope = (cos_t, sin_t, swap)
    qf = _rms_fwd("mla_q_rms", q_raw2, sm['mla_q_norm'], BF16, rope=rope, scale=MLA_QK ** -0.5).reshape(MLA_HEADS, t, MLA_QK)
    kf = _rms_fwd("mla_k_rms", k_raw2, sm['mla_k_norm'], BF16, rope=rope).reshape(MLA_HEADS, t, MLA_QK)
    o_mla = _attn_fwd("mla_attn", qf, kf, v_mla, True, tq=512)

    w_g2 = jnp.pad(_unstack_cols(wt['gla_w_gate2']), ((0, LANES - GLA_GATE_RANK), (0, 0)))
    pre = _mm2("gla_gate_pre", zs['zg'], w_g2, NN, F32)
    log_a = _gate_fwd("gla_gate", pre, sm['gla_b_gate'])
    o_gla_raw, states = _gla_fwd("gla_scan", zs['gq'], zs['gk'], zs['gv'], log_a, GLA_HEADS)
    o_gla_n = _rms_fwd("gla_out_rms", o_gla_raw, sm['gla_out_norm'], F32)
    o_gla = _swiglu_fwd("gla_out_gate", zs['zr'], o_gla_n, BF16)

    cat = jnp.concatenate([o_mla, o_gla], axis=1)
    w_out = wt['w_out'].reshape(d, d)
    x2 = _mm2("mix_out", cat, w_out, NN, F32, res=x1)

    w_mq, w_mk, w_mv = (wt[n].reshape(d, MEM_HEADS * MEM_HEAD_DIM) for n in ('mem_w_q', 'mem_w_k', 'mem_w_v'))
    hq = _rms_fwd("mem_attn_rms", x2, sm['mem_attn_norm'], BF16)
    hm = _rms_fwd("mem_rms", mem, sm['mem_norm'], BF16)

    def heads_out(name, a, b, out_dtype):
        m, kk = a.shape
        tm = _tile(m, 512)
        return _mm(name, a, b, (m // tm, MEM_HEADS, 1), ((tm, kk), lambda i, h, k: (i, 0)),
                   ((kk, MEM_HEAD_DIM), lambda i, h, k: (0, h)), ((None, tm, MEM_HEAD_DIM), lambda i, h, k: (h, i, 0)),
                   (MEM_HEADS, m, MEM_HEAD_DIM), out_dtype, NN)

    mq_raw = heads_out("mem_q", hq, w_mq, F32)
    mk_raw = heads_out("mem_k", hm, w_mk, F32)
    mv = heads_out("mem_v", hm, w_mv, BF16)
    mq = _rms_fwd("mem_q_rms", mq_raw.reshape(MEM_HEADS * t, MEM_HEAD_DIM), sm['mem_q_norm'], BF16, scale=MEM_HEAD_DIM ** -0.5)
    mk = _rms_fwd("mem_k_rms", mk_raw.reshape(MEM_HEADS * nm, MEM_HEAD_DIM), sm['mem_k_norm'], BF16)
    mq, mk = mq.reshape(MEM_HEADS, t, MEM_HEAD_DIM), mk.reshape(MEM_HEADS, nm, MEM_HEAD_DIM)
    o_mem = _attn_fwd("mem_attn", mq, mk, mv, False)
    w_mo = wt['mem_w_o']
    mo_cols = w_mo.shape[2]
    tm = _tile(t, 512)
    x3 = _mm("mem_out", o_mem, w_mo, (t // tm, N_DEV, 1), ((tm, o_mem.shape[1]), lambda i, j, k: (i, 0)),
             ((None, o_mem.shape[1], mo_cols), lambda i, j, k: (j, 0, 0)), ((tm, mo_cols), lambda i, j, k: (i, j)),
             (t, d), F32, NN, res=x2)

    y, ffn2_saved = _ffn_fwd("ffn2", x3, sm['ffn2_norm'], wt)
    dy, dy_narrow, loss_lanes = _loss("loss", y, target)

    dgu = _ffn_bwd_weights("ffn2", dy_narrow, ffn2_saved, wt['ffn2_w_down'], comm)
    dx3, gs['ffn2_norm'] = _ffn_bwd_input("ffn2", dy, ffn2_saved, dgu, sm['ffn2_norm'],
                                          wt['ffn2_w_gate'], wt['ffn2_w_up'], comm)

    do_mem = _mm("mem_do", dx3, w_mo, (t // tm, 1, N_DEV), ((tm, mo_cols), lambda i, h, k: (i, k)),
                 ((None, w_mo.shape[1], mo_cols), lambda i, h, k: (k, 0, 0)), ((tm, w_mo.shape[1]), lambda i, h, k: (i, 0)),
                 (t, MEM_HEADS * MEM_HEAD_DIM), BF16, NT)
    tk = _tile(t, 512)
    gw['mem_w_o'] = _mm("mem_dwo", o_mem, dx3, (N_DEV, 1, t // tk), ((tk, o_mem.shape[1]), lambda j, i, k: (k, 0)),
                        ((tk, mo_cols), lambda j, i, k: (k, j)), ((None, o_mem.shape[1], mo_cols), lambda j, i, k: (j, 0, 0)),
                        w_mo.shape, BF16, TN)
    dmq, dmk, dmv = _attn_bwd("mem_dattn", mq, mk, mv, do_mem, False)
    dmq_raw, gs['mem_q_norm'] = _rms_bwd("mem_q_drms", mq_raw.reshape(MEM_HEADS * t, MEM_HEAD_DIM), sm['mem_q_norm'],
                                         dmq.reshape(MEM_HEADS * t, MEM_HEAD_DIM), scale=MEM_HEAD_DIM ** -0.5)
    dmk_raw, gs['mem_k_norm'] = _rms_bwd("mem_k_drms", mk_raw.reshape(MEM_HEADS * nm, MEM_HEAD_DIM), sm['mem_k_norm'],
                                         dmk.reshape(MEM_HEADS * nm, MEM_HEAD_DIM))
    dmq_raw = dmq_raw.reshape(MEM_HEADS, t, MEM_HEAD_DIM)
    dmk_raw = dmk_raw.reshape(MEM_HEADS, nm, MEM_HEAD_DIM)

    def heads_in_nt(name, a, b, res=None):
        m, n = a.shape[1], b.shape[0]
        tm_, tn_ = _tile(m, 512), _tile(n, 1024)
        return _mm(name, a, b, (m // tm_, n // tn_, MEM_HEADS), ((None, tm_, MEM_HEAD_DIM), lambda i, j, k: (k, i, 0)),
                   ((tn_, MEM_HEAD_DIM), lambda i, j, k: (j, k)), ((tm_, tn_), lambda i, j, k: (i, j)), (m, n), F32, NT,
                   None, res)

    def heads_tn(name, a, b):
        m, kp = a.shape
        tm_, tk_ = _tile(kp, 1024), _tile(m, 512)
        return _mm(name, a, b, (kp // tm_, MEM_HEADS, m // tk_), ((tk_, tm_), lambda i, h, k: (k, i)),
                   ((None, tk_, MEM_HEAD_DIM), lambda i, h, k: (h, k, 0)), ((tm_, MEM_HEAD_DIM), lambda i, h, k: (i, h)),
                   (kp, MEM_HEADS * MEM_HEAD_DIM), BF16, TN)

    dhq = heads_in_nt("mem_dhq", dmq_raw, w_mq)
    gw['mem_w_q'] = heads_tn("mem_dwq", hq, dmq_raw).reshape(wt['mem_w_q'].shape)
    dhm = heads_in_nt("mem_dhm_k", dmk_raw, w_mk)
    dhm = heads_in_nt("mem_dhm_v", dmv, w_mv, res=dhm)
    gw['mem_w_k'] = heads_tn("mem_dwk", hm, dmk_raw).reshape(wt['mem_w_k'].shape)
    gw['mem_w_v'] = heads_tn("mem_dwv", hm, dmv).reshape(wt['mem_w_v'].shape)
    _, gs['mem_norm'] = _rms_bwd("mem_drms", mem, sm['mem_norm'], dhm)
    comm.grads({n: gw[n] for n in MEMORY})
    dx2, gs['mem_attn_norm'] = _rms_bwd("mem_attn_drms", x2, sm['mem_attn_norm'], dhq, res=dx3)

    dcat = _mm2("mix_dcat", dx2, w_out, NT, F32)
    comm.grads({'w_out': _mm2("mix_dwout", cat, dx2, TN, BF16, tm=1024, tn=2048, tk=2048).reshape(wt['w_out'].shape)})
    do_mla, do_gla = dcat[:, :MLA_HEADS * MLA_V], dcat[:, MLA_HEADS * MLA_V:]

    dzr, dgn = _swiglu_bwd("gla_out_dgate", do_gla, zs['zr'], o_gla_n, F32)
    do_gla_raw, gs['gla_out_norm'] = _rms_bwd("gla_out_drms", o_gla_raw, sm['gla_out_norm'], dgn)
    dgq, dgk, dgv, dlog_a = _gla_bwd("gla_dscan", zs['gq'], zs['gk'], zs['gv'], log_a, states, do_gla_raw, GLA_HEADS)
    dpre, gs['gla_b_gate'] = _gate_bwd("gla_dgate", pre, sm['gla_b_gate'], dlog_a)
    dw_g2 = _mm2("gla_dwgate", zs['zg'], dpre, TN, BF16, tk=512)
    comm.grads({'gla_w_gate2': _stack_cols(dw_g2[:GLA_GATE_RANK])})
    dzg = _mm2("gla_dzg", dpre, w_g2, NT, F32)

    dqf, dkf, dv_mla = _attn_bwd("mla_dattn", qf, kf, v_mla, do_mla, True, tq=512)
    dq_raw, gs['mla_q_norm'] = _rms_bwd("mla_q_drms", q_raw2, sm['mla_q_norm'], dqf.reshape(MLA_HEADS * t, MLA_QK), rope=rope,
                                         scale=MLA_QK ** -0.5)
    dk_raw, gs['mla_k_norm'] = _rms_bwd("mla_k_drms", k_raw2, sm['mla_k_norm'], dkf.reshape(MLA_HEADS * t, MLA_QK), rope=rope)
    dq_raw = dq_raw.reshape(MLA_HEADS, t, MLA_QK)
    dk_raw = dk_raw.reshape(MLA_HEADS, t, MLA_QK)
    dkv = jnp.concatenate([dk_raw[:, :, :MLA_NOPE], dv_mla], axis=2)
    dzkr = jnp.sum(dk_raw[:, :, MLA_NOPE:], axis=0)
    comm.grads({'w_q_up': _mm_stack_tn_right("mla_dwq", dq_raw, cq, BF16),
                'w_kv_up': _mm_stack_tn_left("mla_dwkv", ckv, dkv, BF16)})
    dcq = _mm_stack_sum("mla_dcq", dq_raw, wt['w_q_up'], F32)
    dckv = _mm_stack_nt_sum("mla_dckv", dkv, wt['w_kv_up'], F32)
    dzq, gs['q_a_norm'] = _rms_bwd("mla_q_a_drms", zs['zq'], sm['q_a_norm'], dcq)
    dzkv, gs['kv_a_norm'] = _rms_bwd("mla_kv_a_drms", zs['zkv'], sm['kv_a_norm'], dckv)

    dzs = {'zq': dzq, 'zkv': dzkv, 'gq': dgq, 'gk': dgk, 'gv': dgv, 'zr': dzr,
           'zkr': jnp.pad(dzkr, ((0, 0), (0, LANES - MLA_ROPE))), 'zg': dzg}
    dz = jnp.concatenate([dzs[name].astype(BF16) for name, _ in IN_PAD], axis=1)
    dw_in = _mm2("mix_dwin", h2, dz, TN, BF16, tm=1024, tn=2048, tk=2048)
    dw_in_ref = jnp.concatenate([dw_in[:, PAD_OFF[name]:PAD_OFF[name] + n] for name, n in IN_REF], axis=1)
    comm.grads({'w_in': _stack_cols(dw_in_ref)})
    dh2 = _mm2("mix_dh", dz, w_in, NT, F32)
    comm.advance()
    dx1, dx1_narrow, gs['mix_norm'] = _rms_bwd("mix_drms", x1, sm['mix_norm'], dh2, res=dx2, narrow=True)

    dgu = _ffn_bwd_weights("ffn1", dx1_narrow, ffn1_saved, wt['ffn1_w_down'], comm)
    grad_x, gs['ffn1_norm'] = _ffn_bwd_input("ffn1", dx1, ffn1_saved, dgu, sm['ffn1_norm'],
                                             wt['ffn1_w_gate'], wt['ffn1_w_up'], comm)
    return loss_lanes, grad_x, gs


def _pad_lanes(v):
    n = v.shape[1]
    return jnp.pad(v, ((0, 0), (0, -n % LANES)))


def _pack_small(vals):
    return jnp.concatenate([_pad_lanes(vals[n]) for n in SMALL], axis=1)


def _unpack_small(packed, like):
    out, off = {}, 0
    for n in SMALL:
        size = like[n].shape[1]
        out[n] = packed[:, off:off + size]
        off += size + (-size % LANES)
    return out


def kernel(x, mem, positions, ffn1_norm, ffn1_w_gate, ffn1_w_up, ffn1_w_down, mix_norm, w_in, q_a_norm, w_q_up, kv_a_norm, w_kv_up, mla_q_norm, mla_k_norm, gla_w_gate2, gla_b_gate, gla_out_norm, w_out, mem_attn_norm, mem_norm, mem_w_q, mem_w_k, mem_w_v, mem_w_o, mem_q_norm, mem_k_norm, ffn2_norm, ffn2_w_gate, ffn2_w_up, ffn2_w_down, loss_target, m_ffn1_norm, m_ffn1_w_gate, m_ffn1_w_up, m_ffn1_w_down, m_mix_norm, m_w_in, m_q_a_norm, m_w_q_up, m_kv_a_norm, m_w_kv_up, m_mla_q_norm, m_mla_k_norm, m_gla_w_gate2, m_gla_b_gate, m_gla_out_norm, m_w_out, m_mem_attn_norm, m_mem_norm, m_mem_w_q, m_mem_w_k, m_mem_w_v, m_mem_w_o, m_mem_q_norm, m_mem_k_norm, m_ffn2_norm, m_ffn2_w_gate, m_ffn2_w_up, m_ffn2_w_down, v_ffn1_norm, v_ffn1_w_gate, v_ffn1_w_up, v_ffn1_w_down, v_mix_norm, v_w_in, v_q_a_norm, v_w_q_up, v_kv_a_norm, v_w_kv_up, v_mla_q_norm, v_mla_k_norm, v_gla_w_gate2, v_gla_b_gate, v_gla_out_norm, v_w_out, v_mem_attn_norm, v_mem_norm, v_mem_w_q, v_mem_w_k, v_mem_w_v, v_mem_w_o, v_mem_q_norm, v_mem_k_norm, v_ffn2_norm, v_ffn2_w_gate, v_ffn2_w_up, v_ffn2_w_down):
    inp = dict(locals())
    x, mem, positions, target = inp['x'][0], inp['mem'][0], inp['positions'][0], inp['loss_target'][0]
    sm = {n: inp[n] for n in SMALL}
    out = {}

    def stored(key):
        name = key[2:] if key[:2] in ('m_', 'v_') else key
        return inp[key][0].T if name in TRANSPOSED else inp[key][0]

    def as_given(name, r):
        return r.T[None] if name in TRANSPOSED else r[None]

    class Comm:
        def __init__(self):
            self.gathers = {0: self.start(0)}
            self.forwards, self.exchanges, self.pairs = {}, [], []

        def start(self, k):
            return _gather_start(f"gather_start_{k}", [stored(n).astype(BF16) for n in GATHERS[k]])

        def forward(self, k):
            if k not in self.forwards:
                self.forwards[k] = _gather_mid(f"gather_mid_{k}", self.gathers[k])
                self.gathers.update({nxt: self.start(nxt) for nxt in NEXT_GATHERS.get(k, [])})

        def weights(self, k):
            self.forward(k)
            if k in EARLY_FORWARD:
                self.forward(EARLY_FORWARD[k])
            return dict(zip(GATHERS[k], _gather_end(f"gather_end_{k}", self.forwards[k])))

        def grads(self, stacks):
            names = list(stacks)
            if names[0] in TWO_STAGE:
                (n,) = names
                self.pairs.append((n, _stage_start("pair_start_" + n, _pair_copies, stacks[n], 1, (N_CHIP, N_CHIP, 1))))
            else:
                self.exchanges.append((names, _exchange_start("exchange_start_" + names[0], [stacks[n] for n in names])))

        def advance(self):
            for n, state in self.pairs:
                (land,) = _stage_wait("pair_wait_" + n, _pair_copies, state)
                pairs = _pair_sum("pair_sum_" + n, state[1], land)
                self.exchanges.append(([n], _stage_start("chip_start_" + n, _chip_copies, pairs, 1, (N_CHIP - 1, N_CHIP - 1, 1))))
            self.pairs = []

        def update(self, count):
            todo, self.exchanges = self.exchanges[:count], self.exchanges[count:]
            for names, state in todo:
                if names[0] in TWO_STAGE:
                    parts = _stage_wait("chip_wait_" + names[0], _chip_copies, state)
                else:
                    parts = _exchange_wait("exchange_wait_" + names[0], state)
                for n, p in zip(names, parts):
                    res = _adamw("adamw_" + n, stored(n), stored('m_' + n), stored('v_' + n), p)
                    for kind, r in zip(('grad_', 'delta_', 'new_m_', 'new_v_'), res):
                        out[kind + n] = as_given(n, r)

    _Chain.last = None
    comm = Comm()
    loss_lanes, grad_x, gs = _local_step(x, mem, positions, sm, comm, target)
    out['loss'] = lax.psum(jnp.sum(loss_lanes), ("x", "y", "c"))
    out['grad_x'] = grad_x[None]

    comm.update(len(comm.exchanges) - 3)
    small_parts = _all_gather("gather_small", [_pack_small(gs)])[0]
    res = _adamw("adamw_small", _pack_small(sm), _pack_small({n: inp['m_' + n] for n in SMALL}),
                 _pack_small({n: inp['v_' + n] for n in SMALL}), small_parts)
    for kind, r in zip(('grad_', 'delta_', 'new_m_', 'new_v_'), res):
        for n, val in _unpack_small(r, sm).items():
            out[kind + n] = val

    comm.update(3)

    names = ['loss', 'grad_x'] + [k + n for k in ('grad_', 'delta_', 'new_m_', 'new_v_') for n in WEIGHTS]
    return tuple(out[n] for n in names)
```

```python
import functools
import math

import jax
import jax.numpy as jnp
from jax import lax
from jax.experimental import pallas as pl
from jax.experimental.pallas import tpu as pltpu

F32 = jnp.float32
BF16 = jnp.bfloat16

N_DEV = 8
EPS = 1e-6
CHUNK = 64
MLA_HEADS, MLA_NOPE, MLA_ROPE, MLA_V = 8, 128, 64, 128
MLA_QK = MLA_NOPE + MLA_ROPE
MLA_Q_RANK, MLA_KV_RANK = 512, 256
ROPE_THETA = 10000.0
GLA_HEADS, GLA_DK, GLA_DV, GLA_GATE_RANK = 4, 128, 256, 16
GLA_TAU = 16.0
MEM_HEADS, MEM_HEAD_DIM = 4, 128
ADAM_LR, ADAM_B1, ADAM_B2, ADAM_EPS, ADAM_WD, ADAM_STEP = 0.001, 0.9, 0.999, 1e-08, 0.01, 10

V7X_VMEM_BYTES = 64 * 1024 * 1024
LANES = 128

NN = (((1,), (0,)), ((), ()))
NT = (((1,), (1,)), ((), ()))
TN = (((0,), (0,)), ((), ()))

WEIGHTS = ['ffn1_norm', 'ffn1_w_gate', 'ffn1_w_up', 'ffn1_w_down', 'mix_norm', 'w_in', 'q_a_norm', 'w_q_up',
           'kv_a_norm', 'w_kv_up', 'mla_q_norm', 'mla_k_norm', 'gla_w_gate2', 'gla_b_gate', 'gla_out_norm', 'w_out',
           'mem_attn_norm', 'mem_norm', 'mem_w_q', 'mem_w_k', 'mem_w_v', 'mem_w_o', 'mem_q_norm', 'mem_k_norm',
           'ffn2_norm', 'ffn2_w_gate', 'ffn2_w_up', 'ffn2_w_down']
SMALL = ['ffn1_norm', 'mix_norm', 'q_a_norm', 'kv_a_norm', 'mla_q_norm', 'mla_k_norm', 'gla_b_gate', 'gla_out_norm',
         'mem_attn_norm', 'mem_norm', 'mem_q_norm', 'mem_k_norm', 'ffn2_norm']
MIXER = ['w_in', 'w_q_up', 'w_kv_up', 'gla_w_gate2', 'w_out']
MEMORY = ['mem_w_q', 'mem_w_k', 'mem_w_v', 'mem_w_o']
TRANSPOSED = ['ffn1_w_gate', 'ffn1_w_up', 'ffn2_w_gate', 'ffn2_w_up', 'w_q_up']
TWO_STAGE = ['w_in', 'ffn1_w_down', 'ffn1_w_gate', 'ffn1_w_up']
GATHERS = [['ffn1_w_gate'], ['ffn1_w_up'], ['ffn1_w_down'], MIXER[:1], MIXER[1:], MEMORY,
           ['ffn2_w_gate', 'ffn2_w_up', 'ffn2_w_down']]
NEXT_GATHERS = {0: [1], 1: [2], 2: [3], 3: [4, 5, 6]}
EARLY_FORWARD = {5: 6}

IN_REF = [('zq', 512), ('zkv', 256), ('zkr', 64), ('gq', 512), ('gk', 512), ('gv', 1024), ('zg', 16), ('zr', 1024)]
IN_PAD = [('zq', 512), ('zkv', 256), ('gq', 512), ('gk', 512), ('gv', 1024), ('zr', 1024), ('zkr', 128), ('zg', 128)]
IN_WIDTH = sum(n for _, n in IN_REF)
IN_PAD_WIDTH = sum(n for _, n in IN_PAD)


def _offsets(layout):
    out, off = {}, 0
    for name, n in layout:
        out[name] = off
        off += n
    return out


REF_OFF, PAD_OFF = _offsets(IN_REF), _offsets(IN_PAD)
REF_SIZE = dict(IN_REF)


def _tile(n, pref):
    return pref if n % pref == 0 else n


def _block_bytes(blk, dtype):
    dims = [d for d in blk if d is not None]
    if len(dims) >= 1:
        dims[-1] = -(-dims[-1] // LANES) * LANES
    return math.prod(dims) * jnp.dtype(dtype).itemsize


def _vmem_limit(pipelined_bytes, resident_bytes=0):
    need = 2 * pipelined_bytes + resident_bytes + (8 << 20)
    return int(min(max(need, 32 << 20), V7X_VMEM_BYTES - (6 << 20)))


class _Chain:
    last = None


def _chained(body, *, in_specs, link=0, **kwargs):
    def call(*operands):
        dep = _Chain.last
        if dep is not None and any(o is dep for o in operands):
            dep = None
        if dep is None:
            res = pl.pallas_call(body, in_specs=in_specs, **kwargs)(*operands)
        else:
            n = len(operands)

            def chained_body(*refs):
                body(*refs[:n], *refs[n + 1:])

            res = pl.pallas_call(chained_body, in_specs=list(in_specs) + [pl.BlockSpec(memory_space=pl.ANY)],
                                 **kwargs)(*operands, dep)
        _Chain.last = res[link] if isinstance(res, (list, tuple)) else res
        return res

    return call


def _row_tile(rows, width, elements=1 << 18):
    lanes = -(-width // LANES) * LANES
    return _tile(rows, max(256, 1 << int(math.log2(max(1, elements // lanes)))))


def _streamed(*arrays):
    return [pltpu.with_memory_space_constraint(a, pltpu.HBM) for a in arrays]


def _rowwise_params(*semantics):
    return pltpu.CompilerParams(dimension_semantics=semantics, vmem_limit_bytes=48 << 20)


def _mm(name, a, b, grid, a_spec, b_spec, o_spec, out_shape, out_dtype, dims, scale=None, res=None, fuse=None):
    nk = grid[2]
    o_blk, o_map = o_spec
    acc_shape = tuple(d for d in o_blk if d is not None)
    extras = [res] if res is not None else (list(fuse[1]) if fuse else [])
    n_out = 2 if fuse else 1
    a_list, b_list = (list(a), list(b)) if isinstance(a, (tuple, list)) else ([a], [b])
    a_specs = a_spec if isinstance(a_spec, list) else [a_spec] * len(a_list)
    b_specs = b_spec if isinstance(b_spec, list) else [b_spec] * len(b_list)
    n_in = 2 * len(a_list)

    def body(*refs):
        a_refs, b_refs = refs[:n_in // 2], refs[n_in // 2:n_in]
        extra_refs = refs[n_in:n_in + len(extras)]
        out_refs = refs[n_in + len(extras):n_in + len(extras) + n_out]
        rest = refs[n_in + len(extras) + n_out:]
        r_ref = extra_refs[0] if res is not None else None
        o_ref = out_refs[0]

        def product():
            return sum(lax.dot_general(a_ref[...].astype(BF16), b_ref[...].astype(BF16), dims, preferred_element_type=F32)
                       for a_ref, b_ref in zip(a_refs, b_refs))

        def finish(r):
            if scale is not None:
                r = r * scale
            if r_ref is not None:
                r = r + r_ref[...].astype(F32)
            if fuse:
                for ref, val in zip(out_refs, fuse[0](r, *[e[...].astype(F32) for e in extra_refs])):
                    ref[...] = val.astype(ref.dtype)
            else:
                o_ref[...] = r.astype(o_ref.dtype)

        if nk == 1:
            finish(product())
        else:
            acc = rest[0]
            k = pl.program_id(2)

            @pl.when(k == 0)
            def _():
                acc[...] = product()

            @pl.when(k > 0)
            def _():
                acc[...] += product()

            @pl.when(k == nk - 1)
            def _():
                finish(acc[...])

    in_specs = [pl.BlockSpec(*spec) for spec in a_specs + b_specs]
    operands = a_list + b_list
    piped = sum(_block_bytes(spec[0], v.dtype) for spec, v in zip(a_specs + b_specs, operands))
    piped += _block_bytes(o_blk, out_dtype)
    for extra in extras:
        in_specs.append(pl.BlockSpec(o_blk, o_map))
        operands.append(extra)
        piped += _block_bytes(o_blk, extra.dtype)
    piped += (n_out - 1) * _block_bytes(o_blk, out_dtype)
    scratch = [pltpu.VMEM(acc_shape, F32)] if nk > 1 else []
    out_spec, out_sds = pl.BlockSpec(o_blk, o_map), jax.ShapeDtypeStruct(out_shape, out_dtype)
    return _chained(
        body, name=name, grid=grid, in_specs=in_specs, out_specs=(out_spec,) * n_out if fuse else out_spec,
        out_shape=(out_sds,) * n_out if fuse else out_sds, scratch_shapes=scratch,
        compiler_params=pltpu.CompilerParams(
            dimension_semantics=("parallel", "parallel", "arbitrary"),
            vmem_limit_bytes=_vmem_limit(piped, 3 * _block_bytes(acc_shape, F32))),
    )(*operands)


def _mm2(name, a, b, dims, out_dtype, tm=512, tn=1024, tk=2048, scale=None, res=None):
    if dims is NN:
        (m, kk), n = a.shape, b.shape[1]
    elif dims is NT:
        (m, kk), n = a.shape, b.shape[0]
    else:
        (kk, m), n = a.shape, b.shape[1]
    tm, tn, tk = _tile(m, tm), _tile(n, tn), _tile(kk, tk)
    a_spec = ((tk, tm), lambda i, j, k: (k, i)) if dims is TN else ((tm, tk), lambda i, j, k: (i, k))
    b_spec = ((tn, tk), lambda i, j, k: (j, k)) if dims is NT else ((tk, tn), lambda i, j, k: (k, j))
    return _mm(name, a, b, (m // tm, n // tn, kk // tk), a_spec, b_spec, ((tm, tn), lambda i, j, k: (i, j)),
               (m, n), out_dtype, dims, scale, res)


def _mm_stack_out(name, a, b, out_dtype, tm=512, tk=2048):
    (m, kk), (nj, _, n) = a.shape, b.shape
    tm, tk = _tile(m, tm), _tile(kk, tk)
    return _mm(name, a, b, (nj, m // tm, kk // tk), ((tm, tk), lambda j, i, k: (i, k)),
               ((None, tk, n), lambda j, i, k: (j, k, 0)), ((None, tm, n), lambda j, i, k: (j, i, 0)),
               (nj, m, n), out_dtype, NN)


def _mm_stack_nt_out(name, a, b, out_dtype, scale=None, fuse=None, tm=1024, tk=2048):
    (m, kk), (nj, n, _) = a.shape, b.shape
    tm, tk = _tile(m, tm), _tile(kk, tk)
    return _mm(name, a, b, (nj, m // tm, kk // tk), ((tm, tk), lambda j, i, k: (i, k)),
               ((None, n, tk), lambda j, i, k: (j, 0, k)), ((None, tm, n), lambda j, i, k: (j, i, 0)),
               (nj, m, n), out_dtype, NT, scale, fuse=fuse)


def _mm_stack_sum(name, a, b, out_dtype, scale=None, res=None, tm=512, tn=1024, step=2):
    (nj, m, f), n = (a[0] if isinstance(a, tuple) else a).shape, (b[0] if isinstance(b, tuple) else b).shape[2]
    tm, tn = _tile(m, tm), _tile(n, tn)
    pairs = list(zip(a, b)) if isinstance(a, tuple) else [(a, b)]
    step = step if nj % step == 0 else 1
    a_list, b_list, a_spec, b_spec = [], [], [], []
    for a_k, b_k in pairs:
        for s in range(step):
            a_list.append(a_k)
            b_list.append(b_k)
            a_spec.append(((None, tm, f), lambda i, j, k, s=s: (step * k + s, i, 0)))
            b_spec.append(((None, f, tn), lambda i, j, k, s=s: (step * k + s, 0, j)))
    return _mm(name, a_list, b_list, (m // tm, n // tn, nj // step), a_spec, b_spec, ((tm, tn), lambda i, j, k: (i, j)),
               (m, n), out_dtype, NN, scale, res)


def _mm_stack_nt_sum(name, a, b, out_dtype, res=None, tm=512, tn=1024):
    (nj, m, f), n = a.shape, b.shape[1]
    tm, tn = _tile(m, tm), _tile(n, tn)
    return _mm(name, a, b, (m // tm, n // tn, nj), ((None, tm, f), lambda i, j, k: (k, i, 0)),
               ((None, tn, f), lambda i, j, k: (k, j, 0)), ((tm, tn), lambda i, j, k: (i, j)),
               (m, n), out_dtype, NT, None, res)


def _mm_stack_tn_left(name, a, b, out_dtype, tm=1024, tk=512):
    (m, kp), (nj, _, n) = a.shape, b.shape
    tm, tk = _tile(kp, tm), _tile(m, tk)
    return _mm(name, a, b, (nj, kp // tm, m // tk), ((tk, tm), lambda j, i, k: (k, i)),
               ((None, tk, n), lambda j, i, k: (j, k, 0)), ((None, tm, n), lambda j, i, k: (j, i, 0)),
               (nj, kp, n), out_dtype, TN)


def _mm_stack_tn_right(name, a, b, out_dtype, scale=None, tn=2048, tk=2048):
    (nj, m, f), n = a.shape, b.shape[1]
    tn, tk = _tile(n, tn), _tile(m, tk)
    return _mm(name, a, b, (nj, n // tn, m // tk), ((None, tk, f), lambda j, i, k: (j, k, 0)),
               ((tk, tn), lambda j, i, k: (k, i)), ((None, f, tn), lambda j, i, k: (j, 0, i)),
               (nj, f, n), out_dtype, TN, scale)


def _split3(x):
    hi = x.astype(BF16)
    r1 = x - hi.astype(F32)
    mid = r1.astype(BF16)
    lo = (r1 - mid.astype(F32)).astype(BF16)
    return hi, mid, lo


def _swap_halves(x, swap):
    return sum(jnp.dot(part, swap, preferred_element_type=F32) for part in _split3(x))


def _rope_specs(rope, tm):
    cos_t, _, swap = rope
    nt = cos_t.shape[0] // tm
    tab = pl.BlockSpec((tm, cos_t.shape[1]), lambda i, c: (i % nt, 0))
    return [tab, tab, pl.BlockSpec(swap.shape, lambda i, c: (0, 0))]


def _rms_fwd(name, x, g, out_dtype, rope=None, scale=None):
    rows, cols = x.shape
    d = g.shape[1]
    tm = _row_tile(rows if rope is None else rope[0].shape[0], d)

    def body(x_ref, g_ref, *refs):
        xf = x_ref[...].astype(F32)
        r = lax.rsqrt(jnp.mean(xf * xf, axis=-1, keepdims=True) + EPS)
        y = xf * r * g_ref[...]
        if rope is not None:
            c_ref, s_ref, p_ref = refs[:3]
            y = y * c_ref[...] + _swap_halves(y, p_ref[...]) * s_ref[...]
        if scale is not None:
            y = y * scale
        refs[-1][...] = y.astype(refs[-1].dtype)

    return _chained(
        body, name=name, grid=(rows // tm, cols // d),
        in_specs=[pl.BlockSpec((tm, d), lambda i, c: (i, c)), pl.BlockSpec((1, d), lambda i, c: (0, 0))]
        + (_rope_specs(rope, tm) if rope is not None else []),
        out_specs=pl.BlockSpec((tm, d), lambda i, c: (i, c)),
        out_shape=jax.ShapeDtypeStruct((rows, cols), out_dtype),
        compiler_params=_rowwise_params("parallel", "parallel"),
    )(*_streamed(x), g, *(rope or ()))


def _rms_bwd(name, x, g, dy, res=None, rope=None, narrow=False, scale=None):
    rows, cols = x.shape
    d = g.shape[1]
    tm = _row_tile(rows if rope is None else rope[0].shape[0], d)

    def body(*refs):
        x_ref, g_ref, dy_ref = refs[:3]
        n_out = 3 if narrow else 2
        dx_ref, dg_ref = refs[-n_out], refs[-1]
        r_ref = refs[3] if res is not None else None
        xf = x_ref[...].astype(F32)
        r = lax.rsqrt(jnp.mean(xf * xf, axis=-1, keepdims=True) + EPS)
        xhat = xf * r
        dyf = dy_ref[...].astype(F32)
        if scale is not None:
            dyf = dyf * scale
        if rope is not None:
            c_ref, s_ref, p_ref = refs[-n_out - 3:-n_out]
            dyf = dyf * c_ref[...] + _swap_halves(dyf * s_ref[...], p_ref[...])

        @pl.when((pl.program_id(0) == 0) & (pl.program_id(1) == 0))
        def _():
            dg_ref[...] = jnp.zeros_like(dg_ref)

        dg_ref[...] += jnp.sum(dyf * xhat, axis=0, keepdims=True)
        dxh = dyf * g_ref[...]
        dx = r * (dxh - xhat * jnp.mean(dxh * xhat, axis=-1, keepdims=True))
        if r_ref is not None:
            dx = dx + r_ref[...].astype(F32)
        dx_ref[...] = dx
        if narrow:
            refs[-2][...] = dx.astype(BF16)

    blk = pl.BlockSpec((tm, d), lambda i, c: (i, c))
    in_specs = [blk, pl.BlockSpec((1, d), lambda i, c: (0, 0)), blk]
    operands = [x, g, dy]
    if res is not None:
        in_specs.append(blk)
        operands.append(res)
    if rope is not None:
        in_specs += _rope_specs(rope, tm)
        operands += list(rope)
    wide = [(blk, jax.ShapeDtypeStruct((rows, cols), F32))] + ([(blk, jax.ShapeDtypeStruct((rows, cols), BF16))] if narrow else [])
    outs = wide + [(pl.BlockSpec((1, d), lambda i, c: (0, 0)), jax.ShapeDtypeStruct((1, d), F32))]
    return _chained(
        body, name=name, grid=(rows // tm, cols // d), in_specs=in_specs,
        out_specs=tuple(spec for spec, _ in outs), out_shape=tuple(sds for _, sds in outs),
        compiler_params=_rowwise_params("arbitrary", "arbitrary"),
    )(*_streamed(x), g, *_streamed(*operands[2:4 if res is not None else 3]), *operands[4 if res is not None else 3:])


def _swiglu_fwd(name, g, u, out_dtype, tm=256):
    rows, cols = g.shape
    tm = _tile(rows, tm)

    def body(g_ref, u_ref, o_ref):
        gf = g_ref[...].astype(F32)
        o_ref[...] = (gf * jax.nn.sigmoid(gf) * u_ref[...].astype(F32)).astype(o_ref.dtype)

    blk = pl.BlockSpec((tm, cols), lambda i: (i, 0))
    return _chained(
        body, name=name, grid=(rows // tm,), in_specs=[blk, blk], out_specs=blk,
        out_shape=jax.ShapeDtypeStruct((rows, cols), out_dtype),
        compiler_params=_rowwise_params("parallel"),
    )(*_streamed(g, u))


def _swiglu_bwd(name, da, g, u, out_dtype, tm=256):
    rows, cols = g.shape
    tm = _tile(rows, tm)

    def body(da_ref, g_ref, u_ref, dg_ref, du_ref):
        gf = g_ref[...].astype(F32)
        daf = da_ref[...].astype(F32)
        sig = jax.nn.sigmoid(gf)
        du_ref[...] = (daf * gf * sig).astype(du_ref.dtype)
        dg_ref[...] = (daf * u_ref[...].astype(F32) * sig * (1.0 + gf * (1.0 - sig))).astype(dg_ref.dtype)

    blk = pl.BlockSpec((tm, cols), lambda i: (i, 0))
    sds = jax.ShapeDtypeStruct((rows, cols), out_dtype)
    return _chained(
        body, name=name, grid=(rows // tm,), in_specs=[blk, blk, blk], out_specs=(blk, blk), out_shape=(sds, sds),
        compiler_params=_rowwise_params("parallel"),
    )(*_streamed(da, g, u))


def _gate_fwd(name, pre, bias, tm=256):
    rows, cols = pre.shape
    tm = _tile(rows, tm)

    def body(p_ref, b_ref, o_ref):
        z = p_ref[...] + b_ref[...]
        o_ref[...] = (jnp.minimum(z, 0.0) - jnp.log(1.0 + jnp.exp(-jnp.abs(z)))) * (1.0 / GLA_TAU)

    blk = pl.BlockSpec((tm, cols), lambda i: (i, 0))
    return _chained(
        body, name=name, grid=(rows // tm,), in_specs=[blk, pl.BlockSpec((1, cols), lambda i: (0, 0))], out_specs=blk,
        out_shape=jax.ShapeDtypeStruct((rows, cols), F32),
        compiler_params=_rowwise_params("parallel"),
    )(pre, bias)


def _gate_bwd(name, pre, bias, dla, tm=256):
    rows, cols = pre.shape
    tm = _tile(rows, tm)

    def body(p_ref, b_ref, d_ref, dp_ref, db_ref):
        z = p_ref[...] + b_ref[...]
        dp = d_ref[...] * (1.0 / GLA_TAU) / (1.0 + jnp.exp(z))
        dp_ref[...] = dp

        @pl.when(pl.program_id(0) == 0)
        def _():
            db_ref[...] = jnp.zeros_like(db_ref)

        db_ref[...] += jnp.sum(dp, axis=0, keepdims=True)

    blk = pl.BlockSpec((tm, cols), lambda i: (i, 0))
    row = pl.BlockSpec((1, cols), lambda i: (0, 0))
    return _chained(
        body, name=name, grid=(rows // tm,), in_specs=[blk, row, blk], out_specs=(blk, row),
        out_shape=(jax.ShapeDtypeStruct((rows, cols), F32), jax.ShapeDtypeStruct((1, cols), F32)),
        compiler_params=_rowwise_params("arbitrary"),
    )(*_streamed(pre), bias, *_streamed(dla))


def _loss(name, y, target, tm=256):
    rows, d = y.shape
    tm = _tile(rows, tm)

    def body(y_ref, t_ref, dy_ref, dyb_ref, l_ref):
        err = y_ref[...] - t_ref[...]
        dy_ref[...] = err * (1.0 / d)
        dyb_ref[...] = (err * (1.0 / d)).astype(BF16)

        @pl.when(pl.program_id(0) == 0)
        def _():
            l_ref[...] = jnp.zeros_like(l_ref)

        sq = (err * err).reshape(tm // 8, 8, d)
        l_ref[...] += jnp.sum(sq, axis=0) * (0.5 / d)

    blk = pl.BlockSpec((tm, d), lambda i: (i, 0))
    return _chained(
        body, name=name, grid=(rows // tm,), in_specs=[blk, blk],
        out_specs=(blk, blk, pl.BlockSpec((8, d), lambda i: (0, 0))),
        out_shape=(jax.ShapeDtypeStruct((rows, d), F32), jax.ShapeDtypeStruct((rows, d), BF16),
                   jax.ShapeDtypeStruct((8, d), F32)),
        compiler_params=_rowwise_params("arbitrary"),
    )(*_streamed(y, target))


def _per_query_block(causal, nq, tq, tk, inner):
    if not causal:
        inner(0, tk)
        return
    for block in range(nq):
        pl.when(pl.program_id(1) == block)(functools.partial(inner, block * tq, (block + 1) * tq))


def _scores(q, k, causal):
    s = lax.dot_general(q, k, NT, preferred_element_type=F32)
    if causal:
        tq, kv = s.shape
        own = s[:, kv - tq:]
        qc = lax.broadcasted_iota(jnp.int32, own.shape, 0) // CHUNK
        kc = lax.broadcasted_iota(jnp.int32, own.shape, 1) // CHUNK
        own = jnp.where(kc <= qc, own, -1e30)
        s = own if kv == tq else jnp.concatenate([s[:, :kv - tq], own], axis=1)
    e = jnp.exp(s - jnp.max(s, axis=-1, keepdims=True))
    return e, jnp.sum(e, axis=-1, keepdims=True)


def _attn_fwd(name, q, k, v, causal, tq=256):
    nh, t, dk = q.shape
    tk, dv = k.shape[1], v.shape[2]
    tq = _tile(t, tq)

    def body(q_ref, k_ref, v_ref, o_ref):
        def inner(q0, kv):
            e, l = _scores(q_ref[...], k_ref[0:kv, :], causal)
            o = jnp.dot(e.astype(BF16), v_ref[0:kv, :], preferred_element_type=F32)
            o_ref[...] = (o / l).astype(o_ref.dtype)

        _per_query_block(causal, t // tq, tq, tk, inner)

    return _chained(
        body, name=name, grid=(nh, t // tq),
        in_specs=[pl.BlockSpec((None, tq, dk), lambda h, i: (h, i, 0)), pl.BlockSpec((None, tk, dk), lambda h, i: (h, 0, 0)),
                  pl.BlockSpec((None, tk, dv), lambda h, i: (h, 0, 0))],
        out_specs=pl.BlockSpec((tq, dv), lambda h, i: (i, h)),
        out_shape=jax.ShapeDtypeStruct((t, nh * dv), BF16),
        compiler_params=pltpu.CompilerParams(dimension_semantics=("parallel", "parallel"),
                                             vmem_limit_bytes=_vmem_limit(0, 6 * tq * tk * 4)),
    )(q, k, v)


def _attn_bwd(name, q, k, v, do, causal, tq=256):
    nh, t, dk = q.shape
    tk, dv = k.shape[1], v.shape[2]
    tq = _tile(t, tq)

    def body(q_ref, k_ref, v_ref, do_ref, dq_ref, dk_ref, dv_ref):
        @pl.when(pl.program_id(1) == 0)
        def _():
            dk_ref[...] = jnp.zeros_like(dk_ref)
            dv_ref[...] = jnp.zeros_like(dv_ref)

        def inner(q0, kv):
            qb, kb = q_ref[...], k_ref[0:kv, :]
            e, l = _scores(qb, kb, causal)
            p = e * (1.0 / l)
            dob = do_ref[...].astype(BF16)
            dp = lax.dot_general(dob, v_ref[0:kv, :], NT, preferred_element_type=F32)
            ds = (p * (dp - jnp.sum(p * dp, axis=-1, keepdims=True))).astype(BF16)
            dq_ref[...] = jnp.dot(ds, kb, preferred_element_type=F32)
            dk_ref[0:kv, :] += lax.dot_general(ds, qb, TN, preferred_element_type=F32)
            dv_ref[0:kv, :] += lax.dot_general(p.astype(BF16), dob, TN, preferred_element_type=F32)

        _per_query_block(causal, t // tq, tq, tk, inner)

    return _chained(
        body, name=name, grid=(nh, t // tq),
        in_specs=[pl.BlockSpec((None, tq, dk), lambda h, i: (h, i, 0)), pl.BlockSpec((None, tk, dk), lambda h, i: (h, 0, 0)),
                  pl.BlockSpec((None, tk, dv), lambda h, i: (h, 0, 0)), pl.BlockSpec((tq, dv), lambda h, i: (i, h))],
        out_specs=(pl.BlockSpec((None, tq, dk), lambda h, i: (h, i, 0)), pl.BlockSpec((None, tk, dk), lambda h, i: (h, 0, 0)),
                   pl.BlockSpec((None, tk, dv), lambda h, i: (h, 0, 0))),
        out_shape=(jax.ShapeDtypeStruct((nh, t, dk), F32), jax.ShapeDtypeStruct((nh, tk, dk), F32),
                   jax.ShapeDtypeStruct((nh, tk, dv), F32)),
        compiler_params=pltpu.CompilerParams(dimension_semantics=("parallel", "arbitrary"),
                                             vmem_limit_bytes=_vmem_limit(0, 10 * tq * tk * 4)),
    )(q, k, v, do)


def _tri(lower):
    r = lax.broadcasted_iota(jnp.int32, (CHUNK, CHUNK), 0)
    c = lax.broadcasted_iota(jnp.int32, (CHUNK, CHUNK), 1)
    return jnp.where((c <= r) if lower else (c >= r), 1.0, 0.0).astype(BF16)


def _tri_dot(tri, x):
    return sum(jnp.dot(tri, part, preferred_element_type=F32) for part in _split3(x))


def _gla_fwd(name, q, k, v, la, nh):
    t = q.shape[0]
    dk, dv = q.shape[1] // nh, v.shape[1] // nh
    nc = t // CHUNK

    def body(q_ref, k_ref, v_ref, g_ref, o_ref, st_ref, state):
        @pl.when(pl.program_id(0) == 0)
        def _():
            state[...] = jnp.zeros_like(state)

        g = g_ref[...]
        b = _tri_dot(_tri(True), g)
        b_end = jnp.sum(g, axis=0, keepdims=True)
        k_dec = (k_ref[...] * jnp.exp(b_end - b)).astype(BF16)
        decay = jnp.exp(b_end)
        qc = (q_ref[...] * (dk ** -0.5)).astype(BF16)
        vb = v_ref[...].astype(BF16)
        for h in range(nh):
            ks, vs = slice(h * dk, (h + 1) * dk), slice(h * dv, (h + 1) * dv)
            u_t = lax.dot_general(vb[:, vs], k_dec[:, ks], TN, preferred_element_type=F32)
            new = state[h] * decay[:, ks] + u_t
            state[h] = new
            st_ref[h] = new
            o_ref[:, vs] = lax.dot_general(qc[:, ks], new.astype(BF16), NT, preferred_element_type=F32)

    kblk = pl.BlockSpec((CHUNK, nh * dk), lambda n: (n, 0))
    vblk = pl.BlockSpec((CHUNK, nh * dv), lambda n: (n, 0))
    return _chained(
        body, name=name, grid=(nc,), in_specs=[kblk, kblk, vblk, kblk],
        out_specs=(vblk, pl.BlockSpec((nh, None, dv, dk), lambda n: (0, n, 0, 0))),
        out_shape=(jax.ShapeDtypeStruct((t, nh * dv), F32), jax.ShapeDtypeStruct((nh, nc, dv, dk), F32)),
        scratch_shapes=[pltpu.VMEM((nh, dv, dk), F32)],
        compiler_params=_rowwise_params("arbitrary"),
    )(q, k, v, la)


def _gla_bwd(name, q, k, v, la, states, do, nh):
    t = q.shape[0]
    dk, dv = q.shape[1] // nh, v.shape[1] // nh
    nc = t // CHUNK
    scale = dk ** -0.5

    def body(q_ref, k_ref, v_ref, g_ref, do_ref, st_ref, sp_ref, dq_ref, dk_ref, dv_ref, dg_ref, carry):
        i = pl.program_id(0)

        @pl.when(i == 0)
        def _():
            carry[...] = jnp.zeros_like(carry)

        g = g_ref[...]
        b = _tri_dot(_tri(True), g)
        b_end = jnp.sum(g, axis=0, keepdims=True)
        w = jnp.exp(b_end - b)
        decay = jnp.exp(b_end)
        k_dec = k_ref[...] * w
        k_decb = k_dec.astype(BF16)
        qc = (q_ref[...] * scale).astype(BF16)
        dob = do_ref[...].astype(BF16)
        vb = v_ref[...].astype(BF16)
        dk_dec, ddecay = [], []
        for h in range(nh):
            ks, vs = slice(h * dk, (h + 1) * dk), slice(h * dv, (h + 1) * dv)
            dq_ref[:, ks] = jnp.dot(dob[:, vs], st_ref[h].astype(BF16), preferred_element_type=F32) * scale
            g_t = carry[h] + lax.dot_general(dob[:, vs], qc[:, ks], TN, preferred_element_type=F32)
            g_tb = g_t.astype(BF16)
            dk_dec.append(jnp.dot(vb[:, vs], g_tb, preferred_element_type=F32))
            dv_ref[:, vs] = lax.dot_general(k_decb[:, ks], g_tb, NT, preferred_element_type=F32)
            prev = jnp.where(i < nc - 1, sp_ref[h], 0.0)
            ddecay.append(jnp.sum(g_t * prev, axis=0, keepdims=True))
            carry[h] = g_t * decay[:, ks]
        dk_dec = jnp.concatenate(dk_dec, axis=1)
        dk_ref[...] = dk_dec * w
        e = dk_dec * k_dec
        db_end = jnp.sum(e, axis=0, keepdims=True) + jnp.concatenate(ddecay, axis=1) * decay
        dg_ref[...] = _tri_dot(_tri(False), -e) + db_end

    kblk = pl.BlockSpec((CHUNK, nh * dk), lambda i: (nc - 1 - i, 0))
    vblk = pl.BlockSpec((CHUNK, nh * dv), lambda i: (nc - 1 - i, 0))
    ksds = jax.ShapeDtypeStruct((t, nh * dk), F32)
    return _chained(
        body, name=name, grid=(nc,),
        in_specs=[kblk, kblk, vblk, kblk, vblk,
                  pl.BlockSpec((nh, None, dv, dk), lambda i: (0, nc - 1 - i, 0, 0)),
                  pl.BlockSpec((nh, None, dv, dk), lambda i: (0, jnp.maximum(nc - 2 - i, 0), 0, 0))],
        out_specs=(kblk, kblk, vblk, kblk),
        out_shape=(ksds, ksds, jax.ShapeDtypeStruct((t, nh * dv), F32), ksds),
        scratch_shapes=[pltpu.VMEM((nh, dv, dk), F32)],
        compiler_params=_rowwise_params("arbitrary"),
    )(q, k, v, la, do, states, states)


def _adamw(name, w, m, v, parts):
    rows, cols = w.shape
    tm = 1 << int(math.log2(max(8, (1 << 18) // (-(-cols // LANES) * LANES))))
    while rows % tm and tm > 8:
        tm //= 2
    tm = _tile(rows, tm)

    def body(w_ref, m_ref, v_ref, p_ref, g_ref, d_ref, nm_ref, nv_ref):
        g = p_ref[0].astype(F32)
        for s in range(1, parts.shape[0]):
            g = g + p_ref[s].astype(F32)
        m_new = ADAM_B1 * m_ref[...] + (1.0 - ADAM_B1) * g
        v_new = ADAM_B2 * v_ref[...] + (1.0 - ADAM_B2) * jnp.square(g)
        m_hat = m_new / (1.0 - ADAM_B1 ** ADAM_STEP)
        v_hat = v_new / (1.0 - ADAM_B2 ** ADAM_STEP)
        g_ref[...] = g
        d_ref[...] = -ADAM_LR * (m_hat / (jnp.sqrt(v_hat) + ADAM_EPS) + ADAM_WD * w_ref[...])
        nm_ref[...] = m_new
        nv_ref[...] = v_new

    blk = pl.BlockSpec((tm, cols), lambda i: (i, 0))
    sds = jax.ShapeDtypeStruct((rows, cols), F32)
    return _chained(
        body, name=name, grid=(rows // tm,),
        in_specs=[blk, blk, blk, pl.BlockSpec((parts.shape[0], tm, cols), lambda i: (0, i, 0))],
        out_specs=(blk, blk, blk, blk), out_shape=(sds, sds, sds, sds),
        compiler_params=_rowwise_params("parallel"),
    )(*_streamed(w, m, v, parts))


HBM = pl.BlockSpec(memory_space=pl.ANY)
MESH = pl.DeviceIdType.MESH


def _all_gather(name, shards):
    n = len(shards)

    def body(*refs):
        ins, outs = refs[:n], refs[n:2 * n]
        send_sems, recv_sems, local_sems = refs[2 * n:]
        x, y, c = lax.axis_index("x"), lax.axis_index("y"), lax.axis_index("c")
        me, sibling = (x, y, c), (x, y, 1 - c)
        chips = [(1 - x, y), (x, 1 - y), (1 - x, 1 - y)]

        def copy(w, k, block, to, src=None):
            dst = outs[w].at[4 * block[0] + 2 * block[1] + block[2]]
            return pltpu.make_async_remote_copy(
                src_ref=dst if src is None else src, dst_ref=dst, send_sem=send_sems.at[7 * w + k],
                recv_sem=recv_sems.at[7 * w + k], device_id=to, device_id_type=MESH)

        mine, first, passed = [], [], []
        for w in range(n):
            cp = pltpu.make_async_copy(ins[w], outs[w].at[4 * x + 2 * y + c], local_sems.at[w])
            cp.start()
            mine.append(cp)
            first.append(copy(w, 0, me, sibling, src=ins[w]))
            first += [copy(w, 1 + j, me, (*chip, c), src=ins[w]) for j, chip in enumerate(chips)]
        for cp in first:
            cp.start()
        for w in range(n):
            for j, chip in enumerate(chips):
                copy(w, 1 + j, (*chip, c), me).wait_recv()
                cp = copy(w, 4 + j, (*chip, c), sibling)
                cp.start()
                passed.append(cp)
        for w in range(n):
            copy(w, 0, sibling, me).wait_recv()
            for j, chip in enumerate(chips):
                copy(w, 4 + j, (*chip, 1 - c), me).wait_recv()
        for cp in first + passed:
            cp.wait_send()
        for cp in mine:
            cp.wait()

    return _chained(
        body, name=name, in_specs=[HBM] * n, out_specs=[HBM] * n,
        out_shape=[jax.ShapeDtypeStruct((N_DEV,) + s.shape, s.dtype) for s in shards],
        scratch_shapes=[pltpu.SemaphoreType.DMA((7 * n,)), pltpu.SemaphoreType.DMA((7 * n,)),
                        pltpu.SemaphoreType.DMA((n,))],
    )(*shards)


HBM_SPEC = pl.BlockSpec(memory_space=pltpu.HBM)
SEM_SPEC = pl.BlockSpec(memory_space=pltpu.SEMAPHORE)
EFFECT = pltpu.SideEffectType.DATAFLOW_SIDE_EFFECTING
TOKEN = jax.ShapeDtypeStruct((8, LANES), F32)


def _in_hbm(a):
    return pltpu.with_memory_space_constraint(a, pltpu.HBM)


def _place():
    x, y, c = lax.axis_index("x"), lax.axis_index("y"), lax.axis_index("c")
    chips = [(1 - x, y), (x, 1 - y), (1 - x, 1 - y)]
    return x, y, c, chips


def _block_of(px, py, pc):
    return 4 * px + 2 * py + pc


def _gather_copies(ins, outs, sems, w):
    send_sems, recv_sems, local_sems = sems
    x, y, c, chips = _place()
    peers = [(x, y, 1 - c)] + [(*chip, c) for chip in chips]

    def copy(k, block, to, src):
        dst = outs[w].at[_block_of(*block)]
        return pltpu.make_async_remote_copy(src_ref=dst if src is None else src, dst_ref=dst, send_sem=send_sems.at[4 * w + k],
                                            recv_sem=recv_sems.at[4 * w + k], device_id=to, device_id_type=MESH)

    local = pltpu.make_async_copy(ins[w], outs[w].at[_block_of(x, y, c)], local_sems.at[w])
    sends = [copy(k, (x, y, c), peer, ins[w]) for k, peer in enumerate(peers)]
    recvs = [copy(k, peer, (x, y, c), None) for k, peer in enumerate(peers)]
    return local, sends, recvs


def _forward_copies(outs, sems, w):
    send_sems, recv_sems = sems
    x, y, c, chips = _place()

    def copy(j, block, to):
        dst = outs[w].at[_block_of(*block)]
        return pltpu.make_async_remote_copy(src_ref=dst, dst_ref=dst, send_sem=send_sems.at[3 * w + j],
                                            recv_sem=recv_sems.at[3 * w + j], device_id=to, device_id_type=MESH)

    sends = [copy(j, (*chip, c), (x, y, 1 - c)) for j, chip in enumerate(chips)]
    recvs = [copy(j, (*chip, 1 - c), (x, y, c)) for j, chip in enumerate(chips)]
    return sends, recvs


def _gather_start(name, shards):
    n = len(shards)

    def body(*refs):
        ins, outs, sems = refs[:n], refs[n:2 * n], refs[2 * n:2 * n + 3]
        for w in range(n):
            local, sends, _ = _gather_copies(ins, outs, sems, w)
            for cp in sends + [local]:
                cp.start()
        refs[-1][...] = jnp.zeros_like(refs[-1])

    lands = [lax.empty((N_DEV,) + s.shape, s.dtype) for s in shards]
    res = _chained(
        body, name=name, link=-1, in_specs=[HBM_SPEC] * (2 * n),
        out_specs=[SEM_SPEC] * 3 + [HBM_SPEC] * (2 * n) + [pl.BlockSpec(memory_space=pltpu.VMEM)],
        out_shape=[pltpu.SemaphoreType.DMA((4 * n,)), pltpu.SemaphoreType.DMA((4 * n,)), pltpu.SemaphoreType.DMA((n,))]
        + [pltpu.HBM(s.shape, s.dtype) for s in shards] + [pltpu.HBM(l.shape, l.dtype) for l in lands] + [TOKEN],
        input_output_aliases={i: 3 + i for i in range(2 * n)},
        compiler_params=pltpu.CompilerParams(has_side_effects=EFFECT),
    )(*[_in_hbm(s) for s in shards], *[_in_hbm(l) for l in lands])
    return n, res[:3], res[3:3 + n], res[3 + n:3 + 2 * n]


def _gather_mid(name, state):
    n, sems, shards, lands = state

    def body(*refs):
        ins, outs, sems_in = refs[:n], refs[n:2 * n], refs[2 * n:2 * n + 3]
        sems_out = refs[2 * n + 3:2 * n + 5]
        for w in range(n):
            local, sends, recvs = _gather_copies(ins, outs, sems_in, w)
            local.wait()
            for cp in sends:
                cp.wait_send()
            for cp in recvs:
                cp.wait_recv()
            for cp in _forward_copies(outs, sems_out, w)[0]:
                cp.start()
        refs[-1][...] = jnp.zeros_like(refs[-1])

    res = _chained(
        body, name=name, link=-1, in_specs=[HBM_SPEC] * (2 * n) + [SEM_SPEC] * 3,
        out_specs=[SEM_SPEC] * 2 + [HBM_SPEC] * n + [pl.BlockSpec(memory_space=pltpu.VMEM)],
        out_shape=[pltpu.SemaphoreType.DMA((3 * n,)), pltpu.SemaphoreType.DMA((3 * n,))]
        + [pltpu.HBM(l.shape, l.dtype) for l in lands] + [TOKEN],
        input_output_aliases={n + i: 2 + i for i in range(n)},
        compiler_params=pltpu.CompilerParams(has_side_effects=EFFECT),
    )(*shards, *lands, *sems)
    return n, res[:2], res[2:2 + n]


def _gather_end(name, state):
    n, sems, lands = state

    def body(*refs):
        outs, sems_in = refs[:n], refs[n:n + 2]
        for w in range(n):
            sends, recvs = _forward_copies(outs, sems_in, w)
            for cp in sends:
                cp.wait_send()
            for cp in recvs:
                cp.wait_recv()

    return _chained(
        body, name=name, in_specs=[HBM_SPEC] * n + [SEM_SPEC] * 2, out_specs=[HBM_SPEC] * n,
        out_shape=[pltpu.HBM(l.shape, l.dtype) for l in lands], input_output_aliases={i: i for i in range(n)},
        compiler_params=pltpu.CompilerParams(has_side_effects=EFFECT),
    )(*lands, *sems)


def _exchange_copies(ins, outs, sems, w):
    send_sems, recv_sems, local_sems = sems
    x, y, c, _ = _place()
    mine = _block_of(x, y, c)
    local = pltpu.make_async_copy(ins[w].at[mine], outs[w].at[mine], local_sems.at[w])
    remote = []
    for k in range(1, N_DEV):
        px, py, pc = x ^ (k >> 2), y ^ ((k >> 1) & 1), c ^ (k & 1)
        remote.append(pltpu.make_async_remote_copy(
            src_ref=ins[w].at[_block_of(px, py, pc)], dst_ref=outs[w].at[mine], send_sem=send_sems.at[7 * w + k - 1],
            recv_sem=recv_sems.at[7 * w + k - 1], device_id=(px, py, pc), device_id_type=MESH))
    return local, remote


def _exchange_start(name, stacks):
    n = len(stacks)

    def body(*refs):
        ins, outs, sems = refs[:n], refs[n:2 * n], refs[2 * n:2 * n + 3]
        for w in range(n):
            local, remote = _exchange_copies(ins, outs, sems, w)
            for cp in remote + [local]:
                cp.start()
        refs[-1][...] = jnp.zeros_like(refs[-1])

    lands = [lax.empty(s.shape, s.dtype) for s in stacks]
    if any(s is _Chain.last for s in stacks):
        _Chain.last = None
    res = _chained(
        body, name=name, link=-1, in_specs=[HBM_SPEC] * (2 * n),
        out_specs=[SEM_SPEC] * 3 + [HBM_SPEC] * (2 * n) + [pl.BlockSpec(memory_space=pltpu.VMEM)],
        out_shape=[pltpu.SemaphoreType.DMA((7 * n,)), pltpu.SemaphoreType.DMA((7 * n,)), pltpu.SemaphoreType.DMA((n,))]
        + [pltpu.HBM(s.shape, s.dtype) for s in stacks] * 2 + [TOKEN],
        input_output_aliases={i: 3 + i for i in range(2 * n)},
        compiler_params=pltpu.CompilerParams(has_side_effects=EFFECT),
    )(*[_in_hbm(s) for s in stacks], *[_in_hbm(l) for l in lands])
    return n, res[:3], res[3:3 + n], res[3 + n:3 + 2 * n]


def _exchange_wait(name, state):
    n, sems, stacks, lands = state

    def body(*refs):
        ins, outs, sems_in = refs[:n], refs[n:2 * n], refs[2 * n:2 * n + 3]
        for w in range(n):
            local, remote = _exchange_copies(ins, outs, sems_in, w)
            local.wait()
            for cp in remote:
                cp.wait_send()
                cp.wait_recv()

    return _chained(
        body, name=name, in_specs=[HBM_SPEC] * (2 * n) + [SEM_SPEC] * 3, out_specs=[HBM_SPEC] * n,
        out_shape=[pltpu.HBM(l.shape, l.dtype) for l in lands], input_output_aliases={n + i: i for i in range(n)},
        compiler_params=pltpu.CompilerParams(has_side_effects=EFFECT),
    )(*stacks, *lands, *sems)


N_CHIP = N_DEV // 2


def _pair_copies(stack, land, sems):
    send_sems, recv_sems, _ = sems
    x, y, c, _ = _place()
    remote = [pltpu.make_async_remote_copy(src_ref=stack.at[2 * k + 1 - c], dst_ref=land.at[k], send_sem=send_sems.at[k],
                                           recv_sem=recv_sems.at[k], device_id=(x, y, 1 - c), device_id_type=MESH)
              for k in range(N_CHIP)]
    return [], remote


def _chip_copies(pairs, land, sems):
    send_sems, recv_sems, local_sems = sems
    x, y, c, _ = _place()
    mine = 2 * x + y
    local = pltpu.make_async_copy(pairs.at[mine], land.at[mine], local_sems.at[0])
    remote = []
    for m in range(1, N_CHIP):
        px, py = x ^ (m >> 1), y ^ (m & 1)
        remote.append(pltpu.make_async_remote_copy(
            src_ref=pairs.at[2 * px + py], dst_ref=land.at[mine], send_sem=send_sems.at[m - 1],
            recv_sem=recv_sems.at[m - 1], device_id=(px, py, c), device_id_type=MESH))
    return local, remote


def _stage_start(name, copies_of, src_arr, n_land, n_sems):
    half = (N_CHIP,) + src_arr.shape[1:]

    def body(*refs):
        local, remote = copies_of(refs[0], *refs[1:1 + n_land], refs[1 + n_land:4 + n_land])
        for cp in remote + (local if isinstance(local, list) else [local]):
            cp.start()
        refs[-1][...] = jnp.zeros_like(refs[-1])

    if src_arr is _Chain.last:
        _Chain.last = None
    lands = [lax.empty(half, src_arr.dtype) for _ in range(n_land)]
    res = _chained(
        body, name=name, link=-1, in_specs=[HBM_SPEC] * (1 + n_land),
        out_specs=[SEM_SPEC] * 3 + [HBM_SPEC] * (1 + n_land) + [pl.BlockSpec(memory_space=pltpu.VMEM)],
        out_shape=[pltpu.SemaphoreType.DMA((n,)) for n in n_sems] + [pltpu.HBM(src_arr.shape, src_arr.dtype)]
        + [pltpu.HBM(half, src_arr.dtype)] * n_land + [TOKEN],
        input_output_aliases={i: 3 + i for i in range(1 + n_land)},
        compiler_params=pltpu.CompilerParams(has_side_effects=EFFECT),
    )(_in_hbm(src_arr), *[_in_hbm(l) for l in lands])
    return res[:3], res[3], res[4:4 + n_land]


def _stage_wait(name, copies_of, state):
    sems, src_arr, lands = state
    n_land = len(lands)

    def body(*refs):
        local, remote = copies_of(refs[0], *refs[1:1 + n_land], refs[1 + n_land:4 + n_land])
        for cp in (local if isinstance(local, list) else [local]):
            cp.wait()
        for cp in remote:
            cp.wait_send()
            cp.wait_recv()

    return _chained(
        body, name=name, in_specs=[HBM_SPEC] * (1 + n_land) + [SEM_SPEC] * 3, out_specs=[HBM_SPEC] * n_land,
        out_shape=[pltpu.HBM(l.shape, l.dtype) for l in lands],
        input_output_aliases={1 + i: i for i in range(n_land)},
        compiler_params=pltpu.CompilerParams(has_side_effects=EFFECT),
    )(src_arr, *lands, *sems)


def _pair_sum(name, stack, land, tm=64):
    n, rows, cols = land.shape
    tm = _tile(rows, tm)

    def body(c_ref, a_ref, b_ref, o_ref):
        o_ref[...] = (a_ref[...].astype(F32) + b_ref[...].astype(F32)).astype(o_ref.dtype)

    core = lax.axis_index("c").astype(jnp.int32).reshape(1)
    blk = pl.BlockSpec((n, tm, cols), lambda i, c_ref: (0, i, 0))
    mine = pl.BlockSpec((n, None, tm, cols), lambda i, c_ref: (0, c_ref[0], i, 0))
    _Chain.last = pl.pallas_call(
        body, name=name,
        grid_spec=pltpu.PrefetchScalarGridSpec(num_scalar_prefetch=1, grid=(rows // tm,), in_specs=[mine, blk], out_specs=blk),
        out_shape=jax.ShapeDtypeStruct(land.shape, land.dtype), compiler_params=_rowwise_params("parallel"),
    )(core, stack.reshape(n, 2, rows, cols), land)
    return _Chain.last


def _unstack_cols(w):
    return w.transpose(1, 0, 2).reshape(w.shape[1], N_DEV * w.shape[2])


def _stack_cols(w):
    return w.reshape(w.shape[0], N_DEV, w.shape[1] // N_DEV).transpose(1, 0, 2)


def _rope_tables(positions):
    half = MLA_ROPE // 2
    inv_freq = ROPE_THETA ** (-jnp.arange(half, dtype=F32) / half)
    ang = positions.astype(F32)[:, None] * inv_freq
    cos, sin = jnp.cos(ang), jnp.sin(ang)
    t = positions.shape[0]
    cos_t = jnp.concatenate([jnp.ones((t, MLA_NOPE), F32), cos, cos], axis=1)
    sin_t = jnp.concatenate([jnp.zeros((t, MLA_NOPE), F32), -sin, sin], axis=1)
    idx = jnp.arange(MLA_QK)
    partner = jnp.where(idx < MLA_NOPE, -1, jnp.where(idx < MLA_NOPE + half, idx + half, idx - half))
    swap = (idx[:, None] == partner[None, :]).astype(BF16)
    return cos_t, sin_t, swap


def _ffn_fwd(tag, x, gain, wt):
    h = _rms_fwd(tag + "_rms", x, gain, BF16)
    g = _mm_stack_nt_out(tag + "_gate", h, wt[tag + '_w_gate'], BF16)
    u, a = _mm_stack_nt_out(tag + "_up", h, wt[tag + '_w_up'], BF16,
                            fuse=(lambda u_blk, g_blk: (u_blk, g_blk * jax.nn.sigmoid(g_blk) * u_blk), [g]))
    y = _mm_stack_sum(tag + "_down", a, wt[tag + '_w_down'], F32, scale=0.5, res=x, step=N_DEV)
    return y, (x, h, g, u, a)


def _ffn_bwd_weights(tag, dy, saved, wd, comm):
    x, h, g, u, a = saved
    comm.grads({tag + '_w_down': _mm_stack_tn_right(tag + "_dwd", a, dy, BF16, scale=0.5)})

    def dact(da, g_blk, u_blk):
        sig = jax.nn.sigmoid(g_blk)
        return da * u_blk * sig * (1.0 + g_blk * (1.0 - sig)), da * g_blk * sig

    dg, du = _mm_stack_nt_out(tag + "_da", dy, wd, BF16, scale=0.5, fuse=(dact, [g, u]))
    comm.advance()
    comm.grads({tag + '_w_gate': _mm_stack_tn_right(tag + "_dwg", dg, h, BF16)})
    dwu = _mm_stack_tn_right(tag + "_dwu", du, h, BF16)
    comm.advance()
    comm.grads({tag + '_w_up': dwu})
    return dg, du


def _ffn_bwd_input(tag, dy, saved, dgu, gain, wg, wu, comm):
    dg, du = dgu
    dh = _mm_stack_sum(tag + "_dh", (dg, du), (wg, wu), F32, tm=1024)
    comm.advance()
    return _rms_bwd(tag + "_drms", saved[0], gain, dh, res=dy)


def _local_step(x, mem, positions, sm, comm, target):
    t, d = x.shape
    nm = mem.shape[0]
    gs, gw = {}, {}

    class Weights(dict):
        def __missing__(self, name):
            self.update(comm.weights(next(k for k, names in enumerate(GATHERS) if name in names)))
            return self[name]

    wt = Weights()

    x1, ffn1_saved = _ffn_fwd("ffn1", x, sm['ffn1_norm'], wt)

    h2 = _rms_fwd("mix_rms", x1, sm['mix_norm'], BF16)
    w_in_ref = _unstack_cols(wt['w_in'])
    pieces = []
    for name, n in IN_PAD:
        piece = w_in_ref[:, REF_OFF[name]:REF_OFF[name] + REF_SIZE[name]]
        if n != REF_SIZE[name]:
            piece = jnp.pad(piece, ((0, 0), (0, n - REF_SIZE[name])))
        pieces.append(piece)
    w_in = jnp.concatenate(pieces, axis=1)
    z = _mm2("mix_in", h2, w_in, NN, F32)
    zs = {name: z[:, PAD_OFF[name]:PAD_OFF[name] + n] for name, n in IN_PAD}

    cq = _rms_fwd("mla_q_a_rms", zs['zq'], sm['q_a_norm'], BF16)
    q_raw = _mm_stack_nt_out("mla_q_up", cq, wt['w_q_up'], F32)
    ckv = _rms_fwd("mla_kv_a_rms", zs['zkv'], sm['kv_a_norm'], BF16)
    kv = _mm_stack_out("mla_kv_up", ckv, wt['w_kv_up'], F32)
    zkr = zs['zkr'][:, :MLA_ROPE]
    k_raw = jnp.concatenate([kv[:, :, :MLA_NOPE], jnp.broadcast_to(zkr[None], (MLA_HEADS, t, MLA_ROPE))], axis=2)
    v_mla = kv[:, :, MLA_NOPE:].astype(BF16)
    cos_t, sin_t, swap = _rope_tables(positions)
    q_raw2, k_raw2 = q_raw.reshape(MLA_HEADS * t, MLA_QK), k_raw.reshape(MLA_HEADS * t, MLA_QK)
    rope = (cos_t, sin_t, swap)
    qf = _rms_fwd("mla_q_rms", q_raw2, sm['mla_q_norm'], BF16, rope=rope, scale=MLA_QK ** -0.5).reshape(MLA_HEADS, t, MLA_QK)
    kf = _rms_fwd("mla_k_rms", k_raw2, sm['mla_k_norm'], BF16, rope=rope).reshape(MLA_HEADS, t, MLA_QK)
    o_mla = _attn_fwd("mla_attn", qf, kf, v_mla, True, tq=512)

    w_g2 = jnp.pad(_unstack_cols(wt['gla_w_gate2']), ((0, LANES - GLA_GATE_RANK), (0, 0)))
    pre = _mm2("gla_gate_pre", zs['zg'], w_g2, NN, F32)
    log_a = _gate_fwd("gla_gate", pre, sm['gla_b_gate'])
    o_gla_raw, states = _gla_fwd("gla_scan", zs['gq'], zs['gk'], zs['gv'], log_a, GLA_HEADS)
    o_gla_n = _rms_fwd("gla_out_rms", o_gla_raw, sm['gla_out_norm'], F32)
    o_gla = _swiglu_fwd("gla_out_gate", zs['zr'], o_gla_n, BF16)

    cat = jnp.concatenate([o_mla, o_gla], axis=1)
    w_out = wt['w_out'].reshape(d, d)
    x2 = _mm2("mix_out", cat, w_out, NN, F32, res=x1)

    w_mq, w_mk, w_mv = (wt[n].reshape(d, MEM_HEADS * MEM_HEAD_DIM) for n in ('mem_w_q', 'mem_w_k', 'mem_w_v'))
    hq = _rms_fwd("mem_attn_rms", x2, sm['mem_attn_norm'], BF16)
    hm = _rms_fwd("mem_rms", mem, sm['mem_norm'], BF16)

    def heads_out(name, a, b, out_dtype):
        m, kk = a.shape
        tm = _tile(m, 512)
        return _mm(name, a, b, (m // tm, MEM_HEADS, 1), ((tm, kk), lambda i, h, k: (i, 0)),
                   ((kk, MEM_HEAD_DIM), lambda i, h, k: (0, h)), ((None, tm, MEM_HEAD_DIM), lambda i, h, k: (h, i, 0)),
                   (MEM_HEADS, m, MEM_HEAD_DIM), out_dtype, NN)

    mq_raw = heads_out("mem_q", hq, w_mq, F32)
    mk_raw = heads_out("mem_k", hm, w_mk, F32)
    mv = heads_out("mem_v", hm, w_mv, BF16)
    mq = _rms_fwd("mem_q_rms", mq_raw.reshape(MEM_HEADS * t, MEM_HEAD_DIM), sm['mem_q_norm'], BF16, scale=MEM_HEAD_DIM ** -0.5)
    mk = _rms_fwd("mem_k_rms", mk_raw.reshape(MEM_HEADS * nm, MEM_HEAD_DIM), sm['mem_k_norm'], BF16)
    mq, mk = mq.reshape(MEM_HEADS, t, MEM_HEAD_DIM), mk.reshape(MEM_HEADS, nm, MEM_HEAD_DIM)
    o_mem = _attn_fwd("mem_attn", mq, mk, mv, False)
    w_mo = wt['mem_w_o']
    mo_cols = w_mo.shape[2]
    tm = _tile(t, 512)
    x3 = _mm("mem_out", o_mem, w_mo, (t // tm, N_DEV, 1), ((tm, o_mem.shape[1]), lambda i, j, k: (i, 0)),
             ((None, o_mem.shape[1], mo_cols), lambda i, j, k: (j, 0, 0)), ((tm, mo_cols), lambda i, j, k: (i, j)),
             (t, d), F32, NN, res=x2)

    y, ffn2_saved = _ffn_fwd("ffn2", x3, sm['ffn2_norm'], wt)
    dy, dy_narrow, loss_lanes = _loss("loss", y, target)

    dgu = _ffn_bwd_weights("ffn2", dy_narrow, ffn2_saved, wt['ffn2_w_down'], comm)
    dx3, gs['ffn2_norm'] = _ffn_bwd_input("ffn2", dy, ffn2_saved, dgu, sm['ffn2_norm'],
                                          wt['ffn2_w_gate'], wt['ffn2_w_up'], comm)

    do_mem = _mm("mem_do", dx3, w_mo, (t // tm, 1, N_DEV), ((tm, mo_cols), lambda i, h, k: (i, k)),
                 ((None, w_mo.shape[1], mo_cols), lambda i, h, k: (k, 0, 0)), ((tm, w_mo.shape[1]), lambda i, h, k: (i, 0)),
                 (t, MEM_HEADS * MEM_HEAD_DIM), BF16, NT)
    tk = _tile(t, 512)
    gw['mem_w_o'] = _mm("mem_dwo", o_mem, dx3, (N_DEV, 1, t // tk), ((tk, o_mem.shape[1]), lambda j, i, k: (k, 0)),
                        ((tk, mo_cols), lambda j, i, k: (k, j)), ((None, o_mem.shape[1], mo_cols), lambda j, i, k: (j, 0, 0)),
                        w_mo.shape, BF16, TN)
    dmq, dmk, dmv = _attn_bwd("mem_dattn", mq, mk, mv, do_mem, False)
    dmq_raw, gs['mem_q_norm'] = _rms_bwd("mem_q_drms", mq_raw.reshape(MEM_HEADS * t, MEM_HEAD_DIM), sm['mem_q_norm'],
                                         dmq.reshape(MEM_HEADS * t, MEM_HEAD_DIM), scale=MEM_HEAD_DIM ** -0.5)
    dmk_raw, gs['mem_k_norm'] = _rms_bwd("mem_k_drms", mk_raw.reshape(MEM_HEADS * nm, MEM_HEAD_DIM), sm['mem_k_norm'],
                                         dmk.reshape(MEM_HEADS * nm, MEM_HEAD_DIM))
    dmq_raw = dmq_raw.reshape(MEM_HEADS, t, MEM_HEAD_DIM)
    dmk_raw = dmk_raw.reshape(MEM_HEADS, nm, MEM_HEAD_DIM)

    def heads_in_nt(name, a, b, res=None):
        m, n = a.shape[1], b.shape[0]
        tm_, tn_ = _tile(m, 512), _tile(n, 1024)
        return _mm(name, a, b, (m // tm_, n // tn_, MEM_HEADS), ((None, tm_, MEM_HEAD_DIM), lambda i, j, k: (k, i, 0)),
                   ((tn_, MEM_HEAD_DIM), lambda i, j, k: (j, k)), ((tm_, tn_), lambda i, j, k: (i, j)), (m, n), F32, NT,
                   None, res)

    def heads_tn(name, a, b):
        m, kp = a.shape
        tm_, tk_ = _tile(kp, 1024), _tile(m, 512)
        return _mm(name, a, b, (kp // tm_, MEM_HEADS, m // tk_), ((tk_, tm_), lambda i, h, k: (k, i)),
                   ((None, tk_, MEM_HEAD_DIM), lambda i, h, k: (h, k, 0)), ((tm_, MEM_HEAD_DIM), lambda i, h, k: (i, h)),
                   (kp, MEM_HEADS * MEM_HEAD_DIM), BF16, TN)

    dhq = heads_in_nt("mem_dhq", dmq_raw, w_mq)
    gw['mem_w_q'] = heads_tn("mem_dwq", hq, dmq_raw).reshape(wt['mem_w_q'].shape)
    dhm = heads_in_nt("mem_dhm_k", dmk_raw, w_mk)
    dhm = heads_in_nt("mem_dhm_v", dmv, w_mv, res=dhm)
    gw['mem_w_k'] = heads_tn("mem_dwk", hm, dmk_raw).reshape(wt['mem_w_k'].shape)
    gw['mem_w_v'] = heads_tn("mem_dwv", hm, dmv).reshape(wt['mem_w_v'].shape)
    _, gs['mem_norm'] = _rms_bwd("mem_drms", mem, sm['mem_norm'], dhm)
    comm.grads({n: gw[n] for n in MEMORY})
    dx2, gs['mem_attn_norm'] = _rms_bwd("mem_attn_drms", x2, sm['mem_attn_norm'], dhq, res=dx3)

    dcat = _mm2("mix_dcat", dx2, w_out, NT, F32)
    comm.grads({'w_out': _mm2("mix_dwout", cat, dx2, TN, BF16, tm=1024, tn=2048, tk=2048).reshape(wt['w_out'].shape)})
    do_mla, do_gla = dcat[:, :MLA_HEADS * MLA_V], dcat[:, MLA_HEADS * MLA_V:]

    dzr, dgn = _swiglu_bwd("gla_out_dgate", do_gla, zs['zr'], o_gla_n, F32)
    do_gla_raw, gs['gla_out_norm'] = _rms_bwd("gla_out_drms", o_gla_raw, sm['gla_out_norm'], dgn)
    dgq, dgk, dgv, dlog_a = _gla_bwd("gla_dscan", zs['gq'], zs['gk'], zs['gv'], log_a, states, do_gla_raw, GLA_HEADS)
    dpre, gs['gla_b_gate'] = _gate_bwd("gla_dgate", pre, sm['gla_b_gate'], dlog_a)
    dw_g2 = _mm2("gla_dwgate", zs['zg'], dpre, TN, BF16, tk=512)
    comm.grads({'gla_w_gate2': _stack_cols(dw_g2[:GLA_GATE_RANK])})
    dzg = _mm2("gla_dzg", dpre, w_g2, NT, F32)

    dqf, dkf, dv_mla = _attn_bwd("mla_dattn", qf, kf, v_mla, do_mla, True, tq=512)
    dq_raw, gs['mla_q_norm'] = _rms_bwd("mla_q_drms", q_raw2, sm['mla_q_norm'], dqf.reshape(MLA_HEADS * t, MLA_QK), rope=rope,
                                         scale=MLA_QK ** -0.5)
    dk_raw, gs['mla_k_norm'] = _rms_bwd("mla_k_drms", k_raw2, sm['mla_k_norm'], dkf.reshape(MLA_HEADS * t, MLA_QK), rope=rope)
    dq_raw = dq_raw.reshape(MLA_HEADS, t, MLA_QK)
    dk_raw = dk_raw.reshape(MLA_HEADS, t, MLA_QK)
    dkv = jnp.concatenate([dk_raw[:, :, :MLA_NOPE], dv_mla], axis=2)
    dzkr = jnp.sum(dk_raw[:, :, MLA_NOPE:], axis=0)
    comm.grads({'w_q_up': _mm_stack_tn_right("mla_dwq", dq_raw, cq, BF16),
                'w_kv_up': _mm_stack_tn_left("mla_dwkv", ckv, dkv, BF16)})
    dcq = _mm_stack_sum("mla_dcq", dq_raw, wt['w_q_up'], F32)
    dckv = _mm_stack_nt_sum("mla_dckv", dkv, wt['w_kv_up'], F32)
    dzq, gs['q_a_norm'] = _rms_bwd("mla_q_a_drms", zs['zq'], sm['q_a_norm'], dcq)
    dzkv, gs['kv_a_norm'] = _rms_bwd("mla_kv_a_drms", zs['zkv'], sm['kv_a_norm'], dckv)

    dzs = {'zq': dzq, 'zkv': dzkv, 'gq': dgq, 'gk': dgk, 'gv': dgv, 'zr': dzr,
           'zkr': jnp.pad(dzkr, ((0, 0), (0, LANES - MLA_ROPE))), 'zg': dzg}
    dz = jnp.concatenate([dzs[name].astype(BF16) for name, _ in IN_PAD], axis=1)
    dw_in = _mm2("mix_dwin", h2, dz, TN, BF16, tm=1024, tn=2048, tk=2048)
    dw_in_ref = jnp.concatenate([dw_in[:, PAD_OFF[name]:PAD_OFF[name] + n] for name, n in IN_REF], axis=1)
    comm.grads({'w_in': _stack_cols(dw_in_ref)})
    dh2 = _mm2("mix_dh", dz, w_in, NT, F32)
    comm.advance()
    dx1, dx1_narrow, gs['mix_norm'] = _rms_bwd("mix_drms", x1, sm['mix_norm'], dh2, res=dx2, narrow=True)

    dgu = _ffn_bwd_weights("ffn1", dx1_narrow, ffn1_saved, wt['ffn1_w_down'], comm)
    grad_x, gs['ffn1_norm'] = _ffn_bwd_input("ffn1", dx1, ffn1_saved, dgu, sm['ffn1_norm'],
                                             wt['ffn1_w_gate'], wt['ffn1_w_up'], comm)
    return loss_lanes, grad_x, gs


def _pad_lanes(v):
    n = v.shape[1]
    return jnp.pad(v, ((0, 0), (0, -n % LANES)))


def _pack_small(vals):
    return jnp.concatenate([_pad_lanes(vals[n]) for n in SMALL], axis=1)


def _unpack_small(packed, like):
    out, off = {}, 0
    for n in SMALL:
        size = like[n].shape[1]
        out[n] = packed[:, off:off + size]
        off += size + (-size % LANES)
    return out


def kernel(x, mem, positions, ffn1_norm, ffn1_w_gate, ffn1_w_up, ffn1_w_down, mix_norm, w_in, q_a_norm, w_q_up, kv_a_norm, w_kv_up, mla_q_norm, mla_k_norm, gla_w_gate2, gla_b_gate, gla_out_norm, w_out, mem_attn_norm, mem_norm, mem_w_q, mem_w_k, mem_w_v, mem_w_o, mem_q_norm, mem_k_norm, ffn2_norm, ffn2_w_gate, ffn2_w_up, ffn2_w_down, loss_target, m_ffn1_norm, m_ffn1_w_gate, m_ffn1_w_up, m_ffn1_w_down, m_mix_norm, m_w_in, m_q_a_norm, m_w_q_up, m_kv_a_norm, m_w_kv_up, m_mla_q_norm, m_mla_k_norm, m_gla_w_gate2, m_gla_b_gate, m_gla_out_norm, m_w_out, m_mem_attn_norm, m_mem_norm, m_mem_w_q, m_mem_w_k, m_mem_w_v, m_mem_w_o, m_mem_q_norm, m_mem_k_norm, m_ffn2_norm, m_ffn2_w_gate, m_ffn2_w_up, m_ffn2_w_down, v_ffn1_norm, v_ffn1_w_gate, v_ffn1_w_up, v_ffn1_w_down, v_mix_norm, v_w_in, v_q_a_norm, v_w_q_up, v_kv_a_norm, v_w_kv_up, v_mla_q_norm, v_mla_k_norm, v_gla_w_gate2, v_gla_b_gate, v_gla_out_norm, v_w_out, v_mem_attn_norm, v_mem_norm, v_mem_w_q, v_mem_w_k, v_mem_w_v, v_mem_w_o, v_mem_q_norm, v_mem_k_norm, v_ffn2_norm, v_ffn2_w_gate, v_ffn2_w_up, v_ffn2_w_down):
    inp = dict(locals())
    x, mem, positions, target = inp['x'][0], inp['mem'][0], inp['positions'][0], inp['loss_target'][0]
    sm = {n: inp[n] for n in SMALL}
    out = {}

    def stored(key):
        name = key[2:] if key[:2] in ('m_', 'v_') else key
        return inp[key][0].T if name in TRANSPOSED else inp[key][0]

    def as_given(name, r):
        return r.T[None] if name in TRANSPOSED else r[None]

    class Comm:
        def __init__(self):
            self.gathers = {0: self.start(0)}
            self.forwards, self.exchanges, self.pairs = {}, [], []

        def start(self, k):
            return _gather_start(f"gather_start_{k}", [stored(n).astype(BF16) for n in GATHERS[k]])

        def forward(self, k):
            if k not in self.forwards:
                self.forwards[k] = _gather_mid(f"gather_mid_{k}", self.gathers[k])
                self.gathers.update({nxt: self.start(nxt) for nxt in NEXT_GATHERS.get(k, [])})

        def weights(self, k):
            self.forward(k)
            if k in EARLY_FORWARD:
                self.forward(EARLY_FORWARD[k])
            return dict(zip(GATHERS[k], _gather_end(f"gather_end_{k}", self.forwards[k])))

        def grads(self, stacks):
            names = list(stacks)
            if names[0] in TWO_STAGE:
                (n,) = names
                self.pairs.append((n, _stage_start("pair_start_" + n, _pair_copies, stacks[n], 1, (N_CHIP, N_CHIP, 1))))
            else:
                self.exchanges.append((names, _exchange_start("exchange_start_" + names[0], [stacks[n] for n in names])))

        def advance(self):
            for n, state in self.pairs:
                (land,) = _stage_wait("pair_wait_" + n, _pair_copies, state)
                pairs = _pair_sum("pair_sum_" + n, state[1], land)
                self.exchanges.append(([n], _stage_start("chip_start_" + n, _chip_copies, pairs, 1, (N_CHIP - 1, N_CHIP - 1, 1))))
            self.pairs = []

        def update(self, count):
            todo, self.exchanges = self.exchanges[:count], self.exchanges[count:]
            for names, state in todo:
                if names[0] in TWO_STAGE:
                    parts = _stage_wait("chip_wait_" + names[0], _chip_copies, state)
                else:
                    parts = _exchange_wait("exchange_wait_" + names[0], state)
                for n, p in zip(names, parts):
                    res = _adamw("adamw_" + n, stored(n), stored('m_' + n), stored('v_' + n), p)
                    for kind, r in zip(('grad_', 'delta_', 'new_m_', 'new_v_'), res):
                        out[kind + n] = as_given(n, r)

    _Chain.last = None
    comm = Comm()
    loss_lanes, grad_x, gs = _local_step(x, mem, positions, sm, comm, target)
    out['loss'] = lax.psum(jnp.sum(loss_lanes), ("x", "y", "c"))
    out['grad_x'] = grad_x[None]

    comm.update(len(comm.exchanges) - 3)
    small_parts = _all_gather("gather_small", [_pack_small(gs)])[0]
    res = _adamw("adamw_small", _pack_small(sm), _pack_small({n: inp['m_' + n] for n in SMALL}),
                 _pack_small({n: inp['v_' + n] for n in SMALL}), small_parts)
    for kind, r in zip(('grad_', 'delta_', 'new_m_', 'new_v_'), res):
        for n, val in _unpack_small(r, sm).items():
            out[kind + n] = val

    comm.update(3)

    names = ['loss', 'grad_x'] + [k + n for k in ('grad_', 'delta_', 'new_m_', 'new_v_') for n in WEIGHTS]
    return tuple(out[n] for n in names)
```
